```python
import math
import jax
import jax.numpy as jnp
from jax import lax
import numpy as np

D_MODEL = 1024
BATCH = 16
SEQ = 2048
DEPTH = 2

N_META = 16
CHUNK = 128
EPS = 1e-6

SSD_HEADS = 16
SSD_HEAD_DIM = 64
SSD_INNER = SSD_HEADS * SSD_HEAD_DIM
SSD_GROUPS = 4
SSD_STATE = 128
SSD_CONV = 4
SSD_XBC = SSD_INNER + 2 * SSD_GROUPS * SSD_STATE

RET_HEADS = 4
RET_QK_DIM = 256
RET_V_DIM = 256
RET_QK_WIDTH = RET_HEADS * RET_QK_DIM
RET_WIDTH = RET_HEADS * RET_V_DIM

SB_HEADS = 16
SB_HEAD_DIM = 64
SB_WIDTH = SB_HEADS * SB_HEAD_DIM

LRU_WIDTH = 1024
LRU_BLOCKS = 8
LRU_BLOCK = LRU_WIDTH // LRU_BLOCKS
LRU_CONV = 4
LRU_C = 8.0

FFN_DIM = 2816
FFN_CONV = 3

MIX0_IN = SSD_INNER + SSD_XBC + SSD_HEADS + 2 * RET_QK_WIDTH + 2 * RET_WIDTH
MIX0_OUT = SSD_INNER + RET_WIDTH
MIX1_IN = 3 * SB_WIDTH + 2 * LRU_WIDTH
MIX1_OUT = SB_WIDTH + LRU_WIDTH

kernel_name = 'hybrid_ssd_retention_stickbreak_rglru_block'


def rmsnorm(x, g):
    xf = x.astype(jnp.float32)
    y = xf * lax.rsqrt(jnp.mean(xf * xf, -1, keepdims=True) + EPS)
    return (y * g).astype(x.dtype)


def causal_dwconv(x, w, b):
    K, C = w.shape
    y = lax.conv_general_dilated(x, w[:, None, :], window_strides=(1,), padding=[(K - 1, 0)],
                                 dimension_numbers=('NWC', 'WIO', 'NWC'), feature_group_count=C)
    return y + b


def split_cols(u, sizes):
    return jnp.split(u, np.cumsum(sizes)[:-1].tolist(), axis=-1)


def front_pad(t, pad):
    return jnp.pad(t, ((0, 0), (pad, 0)) + ((0, 0),) * (t.ndim - 2))


def segsum(a):
    L = a.shape[-1]
    cs = jnp.cumsum(a, -1)
    seg = cs[..., :, None] - cs[..., None, :]
    return jnp.where(jnp.tril(jnp.ones((L, L), bool)), seg, -jnp.inf)


def ssd_group(z, xbc, dt_raw, conv_w, conv_b, dt_bias, a_log, d_skip, norm_g):
    Bsz, T, _ = xbc.shape
    f32 = jnp.float32
    E = SSD_HEADS // SSD_GROUPS
    xbc = jax.nn.silu(causal_dwconv(xbc, conv_w, conv_b)).astype(f32)
    xs, bm, cm = jnp.split(xbc, [SSD_INNER, SSD_INNER + SSD_GROUPS * SSD_STATE], -1)
    dt = jax.nn.softplus(dt_raw.astype(f32) + dt_bias.astype(f32))
    a = -jnp.exp(a_log.astype(f32))
    pad = (-T) % CHUNK
    P = T + pad
    nc = P // CHUNK
    X = front_pad(xs * jnp.repeat(dt, SSD_HEAD_DIM, -1), pad).reshape(Bsz, nc, CHUNK, SSD_GROUPS, E, SSD_HEAD_DIM)
    Bm = front_pad(bm, pad).reshape(Bsz, nc, CHUNK, SSD_GROUPS, SSD_STATE)
    Cm = front_pad(cm, pad).reshape(Bsz, nc, CHUNK, SSD_GROUPS, SSD_STATE)
    a_dt = front_pad(dt * a, pad).reshape(Bsz, nc, CHUNK, SSD_GROUPS, E).transpose(0, 3, 4, 1, 2)
    a_cs = jnp.cumsum(a_dt, -1)
    cb = jnp.einsum('bclgn,bcsgn->bgcls', Cm, Bm)
    y_diag = jnp.einsum('bgecls,bcsgep->bclgep', cb[:, :, None] * jnp.exp(segsum(a_dt)), X)
    decay_states = jnp.exp(a_cs[..., -1:] - a_cs).transpose(0, 3, 4, 1, 2)
    states = jnp.einsum('bclgn,bclgep->bcgepn', Bm, X * decay_states[..., None])
    chunk_tot = jnp.pad(a_cs[..., -1], ((0, 0), (0, 0), (0, 0), (1, 0)))
    decay_chunk = jnp.exp(segsum(chunk_tot))
    states = jnp.concatenate([jnp.zeros_like(states[:, :1]), states], 1)
    prev = jnp.einsum('bgezc,bcgepn->bzgepn', decay_chunk, states)[:, :-1]
    y_off = jnp.einsum('bclgn,bcgepn->bclgep', Cm, prev) * jnp.exp(a_cs).transpose(0, 3, 4, 1, 2)[..., None]
    y = (y_diag + y_off).reshape(Bsz, P, SSD_INNER)[:, pad:]
    y = y + xs * jnp.repeat(d_skip.astype(f32), SSD_HEAD_DIM)
    y = (y * jax.nn.silu(z.astype(f32))).reshape(Bsz, T, SSD_GROUPS, SSD_INNER // SSD_GROUPS)
    y = y * lax.rsqrt(jnp.mean(y * y, -1, keepdims=True) + EPS)
    return (y.reshape(Bsz, T, SSD_INNER) * norm_g).astype(z.dtype)


def rotate_retnet(x, pos):
    half = x.shape[-1] // 2
    inv_freq = 1.0 / (10000.0 ** (jnp.arange(half, dtype=jnp.float32) / (half - 1)))
    ang = pos[:, None] * inv_freq[None, :]
    cos = jnp.cos(ang)[None, :, None, :]
    sin = jnp.sin(ang)[None, :, None, :]
    x1, x2 = x[..., :half], x[..., half:]
    return jnp.concatenate([x1 * cos - x2 * sin, x1 * sin + x2 * cos], -1)


def retention_group(q, k, v, g, norm_g):
    Bsz, T, _ = q.shape
    f32 = jnp.float32
    pos = jnp.arange(T, dtype=f32)
    q = rotate_retnet(q.astype(f32).reshape(Bsz, T, RET_HEADS, RET_QK_DIM), pos)
    k = rotate_retnet(k.astype(f32).reshape(Bsz, T, RET_HEADS, RET_QK_DIM), pos) * RET_QK_DIM ** -0.5
    v = v.astype(f32).reshape(Bsz, T, RET_HEADS, RET_V_DIM)
    pad = (-T) % CHUNK
    P = T + pad
    nc = P // CHUNK
    q = front_pad(q, pad).reshape(Bsz, nc, CHUNK, RET_HEADS, RET_QK_DIM)
    k = front_pad(k, pad).reshape(Bsz, nc, CHUNK, RET_HEADS, RET_QK_DIM)
    v = front_pad(v, pad).reshape(Bsz, nc, CHUNK, RET_HEADS, RET_V_DIM)
    log_gamma = jnp.log1p(-jnp.exp2(-5.0 - jnp.arange(RET_HEADS, dtype=f32)))
    idx = jnp.arange(CHUNK, dtype=f32)
    diff = idx[:, None] - idx[None, :]
    decay = jnp.where(diff >= 0, jnp.exp(log_gamma[:, None, None] * jnp.maximum(diff, 0.0)), 0.0)
    scores = jnp.einsum('bclhd,bcshd->bhcls', q, k) * decay[None, :, None]
    inner = jnp.einsum('bhcls,bcshe->bclhe', scores, v)
    zeta = jnp.exp(log_gamma[None, :] * (CHUNK - 1 - idx)[:, None])
    kv = jnp.einsum('bclhd,bclhe->bchde', k * zeta[..., None], v)
    chunk_decay = jnp.exp(CHUNK * log_gamma)[None, :, None, None]

    def step(R, kv_c):
        return chunk_decay * R + kv_c, R

    _, R_prev = lax.scan(step, jnp.zeros_like(kv[:, 0]), jnp.moveaxis(kv, 1, 0))
    R_prev = jnp.moveaxis(R_prev, 0, 1)
    xi = jnp.exp(log_gamma[None, :] * (idx + 1.0)[:, None])
    cross = jnp.einsum('bclhd,bchde->bclhe', q, R_prev) * xi[..., None]
    o = (inner + cross).reshape(Bsz, P, RET_HEADS, RET_V_DIM)[:, pad:]
    o = o - jnp.mean(o, -1, keepdims=True)
    o = o * lax.rsqrt(jnp.mean(o * o, -1, keepdims=True) + EPS)
    o = o.reshape(Bsz, T, RET_WIDTH) * norm_g
    return (jax.nn.silu(g.astype(f32)) * o).astype(g.dtype)


def stick_breaking_attention(q, k, v):
    Bsz, T, H, D = q.shape
    f32 = jnp.float32
    scale = D ** -0.5
    bounds = [(0, N_META)] + [(s, min(s + CHUNK, T)) for s in range(N_META, T, CHUNK)]
    outs = []
    for s0, s1 in bounds:
        z = jnp.einsum('bqhd,bkhd->bhqk', q[:, s0:s1].astype(f32), k[:, :s1].astype(f32)) * scale
        strict = jnp.arange(s1)[None, :] < jnp.arange(s0, s1)[:, None]
        log_1m = jnp.where(strict, jax.nn.log_sigmoid(-z), 0.0)
        after = lax.cumsum(log_1m, axis=3, reverse=True) - log_1m
        w = jnp.where(strict, jnp.exp(jax.nn.log_sigmoid(z) + after), 0.0)
        outs.append(jnp.einsum('bhqk,bkhd->bqhd', w, v[:, :s1].astype(f32)))
    return jnp.concatenate(outs, 1)


def rg_lru(x, w_a, b_a, w_x, b_x, lam):
    Bsz, T, W = x.shape
    f32 = jnp.float32
    x = x.astype(f32)
    xb = x.reshape(Bsz, T, LRU_BLOCKS, LRU_BLOCK)
    r = jax.nn.sigmoid(jnp.einsum('btni,nij->btnj', xb, w_a.astype(f32)).reshape(Bsz, T, W) + b_a)
    i = jax.nn.sigmoid(jnp.einsum('btni,nij->btnj', xb, w_x.astype(f32)).reshape(Bsz, T, W) + b_x)
    log_a = -LRU_C * r * jax.nn.softplus(-lam.astype(f32))
    a = jnp.exp(log_a)
    b = jnp.sqrt(jnp.maximum(-jnp.expm1(2.0 * log_a), 0.0)) * (i * x)

    def combine(left, right):
        a_l, b_l = left
        a_r, b_r = right
        return a_l * a_r, a_r * b_l + b_r

    _, hs = lax.associative_scan(combine, (a, b), axis=1)
    return hs


def ssd_retention_mixer(h, w_in, ssd_conv_w, ssd_conv_b, ssd_dt_bias, ssd_a_log, ssd_d, ssd_norm, ret_norm, w_out):
    u = h @ w_in
    z, xbc, dt_raw, q, k, v, g = split_cols(u, (SSD_INNER, SSD_XBC, SSD_HEADS, RET_QK_WIDTH, RET_QK_WIDTH, RET_WIDTH, RET_WIDTH))
    y_ssd = ssd_group(z, xbc, dt_raw, ssd_conv_w, ssd_conv_b, ssd_dt_bias, ssd_a_log, ssd_d, ssd_norm)
    y_ret = retention_group(q, k, v, g, ret_norm)
    return jnp.concatenate([y_ssd, y_ret], -1) @ w_out


def sb_lru_mixer(h, w_in, lru_conv_w, lru_conv_b, lru_wa, lru_ba, lru_wx, lru_bx, lru_lambda, w_out):
    Bsz, T, _ = h.shape
    u = h @ w_in
    q, k, v, gate, xr = split_cols(u, (SB_WIDTH, SB_WIDTH, SB_WIDTH, LRU_WIDTH, LRU_WIDTH))
    shp = (Bsz, T, SB_HEADS, SB_HEAD_DIM)
    y_sb = stick_breaking_attention(q.reshape(shp), k.reshape(shp), v.reshape(shp)).reshape(Bsz, T, SB_WIDTH)
    xr = causal_dwconv(xr, lru_conv_w, lru_conv_b)
    y_lru = rg_lru(xr, lru_wa, lru_ba, lru_wx, lru_bx, lru_lambda) * jax.nn.gelu(gate.astype(jnp.float32))
    return jnp.concatenate([y_sb.astype(h.dtype), y_lru.astype(h.dtype)], -1) @ w_out


def conv_ffn(h, w_in, conv_w, conv_b, w_out):
    u = causal_dwconv(h @ w_in, conv_w, conv_b)
    g, up = jnp.split(u, 2, -1)
    return (jax.nn.silu(g) * up) @ w_out


def _fwd_setup_inputs(seed: int = 0) -> dict:
    key = jax.random.key(seed)
    ks = iter(jax.random.split(key, 64))
    f32 = jnp.float32

    def nrm(shape, scale):
        return jax.random.normal(next(ks), shape, f32) * scale

    def gain(n):
        return 1.0 + nrm((n,), 0.02)

    dt0 = jnp.exp(jax.random.uniform(next(ks), (SSD_HEADS,), f32, math.log(1e-3), math.log(1e-1)))
    dt_bias = dt0 + jnp.log(-jnp.expm1(-dt0))
    a_log = jnp.log(jax.random.uniform(next(ks), (SSD_HEADS,), f32, 1.0, 16.0))
    a_lru = jax.random.uniform(next(ks), (LRU_WIDTH,), f32, 0.9, 0.999) ** (1.0 / LRU_C)
    lam = jnp.log(a_lru) - jnp.log1p(-a_lru)
    return {
        'x': nrm((BATCH, SEQ, D_MODEL), 1.0),
        'meta_tokens': nrm((N_META, D_MODEL), 1.0),
        'l0_mix_norm': gain(D_MODEL),
        'l0_w_in': nrm((D_MODEL, MIX0_IN), D_MODEL ** -0.5),
        'l0_ssd_conv_w': nrm((SSD_CONV, SSD_XBC), SSD_CONV ** -0.5),
        'l0_ssd_conv_b': nrm((SSD_XBC,), 0.01),
        'l0_ssd_dt_bias': dt_bias,
        'l0_ssd_a_log': a_log,
        'l0_ssd_d': gain(SSD_HEADS),
        'l0_ssd_norm': gain(SSD_INNER),
        'l0_ret_norm': gain(RET_WIDTH),
        'l0_w_out': nrm((MIX0_OUT, D_MODEL), MIX0_OUT ** -0.5),
        'l0_ffn_norm': gain(D_MODEL),
        'l0_ffn_w_in': nrm((D_MODEL, 2 * FFN_DIM), D_MODEL ** -0.5),
        'l0_ffn_conv_w': nrm((FFN_CONV, 2 * FFN_DIM), FFN_CONV ** -0.5),
        'l0_ffn_conv_b': nrm((2 * FFN_DIM,), 0.01),
        'l0_ffn_w_out': nrm((FFN_DIM, D_MODEL), FFN_DIM ** -0.5),
        'l1_mix_norm': gain(D_MODEL),
        'l1_w_in': nrm((D_MODEL, MIX1_IN), D_MODEL ** -0.5),
        'l1_lru_conv_w': nrm((LRU_CONV, LRU_WIDTH), LRU_CONV ** -0.5),
        'l1_lru_conv_b': nrm((LRU_WIDTH,), 0.01),
        'l1_lru_wa': nrm((LRU_BLOCKS, LRU_BLOCK, LRU_BLOCK), LRU_BLOCK ** -0.5),
        'l1_lru_ba': nrm((LRU_WIDTH,), 0.01),
        'l1_lru_wx': nrm((LRU_BLOCKS, LRU_BLOCK, LRU_BLOCK), LRU_BLOCK ** -0.5),
        'l1_lru_bx': nrm((LRU_WIDTH,), 0.01),
        'l1_lru_lambda': lam,
        'l1_w_out': nrm((MIX1_OUT, D_MODEL), MIX1_OUT ** -0.5),
        'l1_ffn_norm': gain(D_MODEL),
        'l1_ffn_w_in': nrm((D_MODEL, 2 * FFN_DIM), D_MODEL ** -0.5),
        'l1_ffn_conv_w': nrm((FFN_CONV, 2 * FFN_DIM), FFN_CONV ** -0.5),
        'l1_ffn_conv_b': nrm((2 * FFN_DIM,), 0.01),
        'l1_ffn_w_out': nrm((FFN_DIM, D_MODEL), FFN_DIM ** -0.5),
        'final_norm': gain(D_MODEL),
    }


def _fwd_reference(x, meta_tokens, l0_mix_norm, l0_w_in, l0_ssd_conv_w, l0_ssd_conv_b, l0_ssd_dt_bias, l0_ssd_a_log,
              l0_ssd_d, l0_ssd_norm, l0_ret_norm, l0_w_out, l0_ffn_norm, l0_ffn_w_in, l0_ffn_conv_w, l0_ffn_conv_b,
              l0_ffn_w_out, l1_mix_norm, l1_w_in, l1_lru_conv_w, l1_lru_conv_b, l1_lru_wa, l1_lru_ba, l1_lru_wx,
              l1_lru_bx, l1_lru_lambda, l1_w_out, l1_ffn_norm, l1_ffn_w_in, l1_ffn_conv_w, l1_ffn_conv_b,
              l1_ffn_w_out, final_norm):
    Bsz = x.shape[0]
    meta = jnp.broadcast_to(meta_tokens[None].astype(x.dtype), (Bsz, N_META, D_MODEL))
    h = jnp.concatenate([meta, x], 1)
    even_layers = [(l0_mix_norm,
                    (l0_w_in, l0_ssd_conv_w, l0_ssd_conv_b, l0_ssd_dt_bias, l0_ssd_a_log, l0_ssd_d, l0_ssd_norm, l0_ret_norm, l0_w_out),
                    l0_ffn_norm, (l0_ffn_w_in, l0_ffn_conv_w, l0_ffn_conv_b, l0_ffn_w_out))]
    odd_layers = [(l1_mix_norm,
                   (l1_w_in, l1_lru_conv_w, l1_lru_conv_b, l1_lru_wa, l1_lru_ba, l1_lru_wx, l1_lru_bx, l1_lru_lambda, l1_w_out),
                   l1_ffn_norm, (l1_ffn_w_in, l1_ffn_conv_w, l1_ffn_conv_b, l1_ffn_w_out))]
    for layer in range(DEPTH):
        if layer % 2 == 0:
            mix_norm, mix_params, ffn_norm, ffn_params = even_layers[layer // 2]
            h = h + ssd_retention_mixer(rmsnorm(h, mix_norm), *mix_params)
        else:
            mix_norm, mix_params, ffn_norm, ffn_params = odd_layers[layer // 2]
            h = h + sb_lru_mixer(rmsnorm(h, mix_norm), *mix_params)
        h = h + conv_ffn(rmsnorm(h, ffn_norm), *ffn_params)
    return rmsnorm(h, final_norm)[:, N_META:]


import jax as _jax
import jax.numpy as _jnp

TWIN_FORMAT = 'train_step'
FWD_PARAMS = ['x', 'meta_tokens', 'l0_mix_norm', 'l0_w_in', 'l0_ssd_conv_w', 'l0_ssd_conv_b', 'l0_ssd_dt_bias', 'l0_ssd_a_log', 'l0_ssd_d', 'l0_ssd_norm', 'l0_ret_norm', 'l0_w_out', 'l0_ffn_norm', 'l0_ffn_w_in', 'l0_ffn_conv_w', 'l0_ffn_conv_b', 'l0_ffn_w_out', 'l1_mix_norm', 'l1_w_in', 'l1_lru_conv_w', 'l1_lru_conv_b', 'l1_lru_wa', 'l1_lru_ba', 'l1_lru_wx', 'l1_lru_bx', 'l1_lru_lambda', 'l1_w_out', 'l1_ffn_norm', 'l1_ffn_w_in', 'l1_ffn_conv_w', 'l1_ffn_conv_b', 'l1_ffn_w_out', 'final_norm']
TWIN_WEIGHTS = ['meta_tokens', 'l0_mix_norm', 'l0_w_in', 'l0_ssd_conv_w', 'l0_ssd_conv_b', 'l0_ssd_dt_bias', 'l0_ssd_a_log', 'l0_ssd_d', 'l0_ssd_norm', 'l0_ret_norm', 'l0_w_out', 'l0_ffn_norm', 'l0_ffn_w_in', 'l0_ffn_conv_w', 'l0_ffn_conv_b', 'l0_ffn_w_out', 'l1_mix_norm', 'l1_w_in', 'l1_lru_conv_w', 'l1_lru_conv_b', 'l1_lru_wa', 'l1_lru_ba', 'l1_lru_wx', 'l1_lru_bx', 'l1_lru_lambda', 'l1_w_out', 'l1_ffn_norm', 'l1_ffn_w_in', 'l1_ffn_conv_w', 'l1_ffn_conv_b', 'l1_ffn_w_out', 'final_norm']
TWIN_DIFF_INPUT = 'x'
TWIN_INPUTS = ['x', 'meta_tokens', 'l0_mix_norm', 'l0_w_in', 'l0_ssd_conv_w', 'l0_ssd_conv_b', 'l0_ssd_dt_bias', 'l0_ssd_a_log', 'l0_ssd_d', 'l0_ssd_norm', 'l0_ret_norm', 'l0_w_out', 'l0_ffn_norm', 'l0_ffn_w_in', 'l0_ffn_conv_w', 'l0_ffn_conv_b', 'l0_ffn_w_out', 'l1_mix_norm', 'l1_w_in', 'l1_lru_conv_w', 'l1_lru_conv_b', 'l1_lru_wa', 'l1_lru_ba', 'l1_lru_wx', 'l1_lru_bx', 'l1_lru_lambda', 'l1_w_out', 'l1_ffn_norm', 'l1_ffn_w_in', 'l1_ffn_conv_w', 'l1_ffn_conv_b', 'l1_ffn_w_out', 'final_norm', 'loss_target', 'm_meta_tokens', 'm_l0_mix_norm', 'm_l0_w_in', 'm_l0_ssd_conv_w', 'm_l0_ssd_conv_b', 'm_l0_ssd_dt_bias', 'm_l0_ssd_a_log', 'm_l0_ssd_d', 'm_l0_ssd_norm', 'm_l0_ret_norm', 'm_l0_w_out', 'm_l0_ffn_norm', 'm_l0_ffn_w_in', 'm_l0_ffn_conv_w', 'm_l0_ffn_conv_b', 'm_l0_ffn_w_out', 'm_l1_mix_norm', 'm_l1_w_in', 'm_l1_lru_conv_w', 'm_l1_lru_conv_b', 'm_l1_lru_wa', 'm_l1_lru_ba', 'm_l1_lru_wx', 'm_l1_lru_bx', 'm_l1_lru_lambda', 'm_l1_w_out', 'm_l1_ffn_norm', 'm_l1_ffn_w_in', 'm_l1_ffn_conv_w', 'm_l1_ffn_conv_b', 'm_l1_ffn_w_out', 'm_final_norm', 'v_meta_tokens', 'v_l0_mix_norm', 'v_l0_w_in', 'v_l0_ssd_conv_w', 'v_l0_ssd_conv_b', 'v_l0_ssd_dt_bias', 'v_l0_ssd_a_log', 'v_l0_ssd_d', 'v_l0_ssd_norm', 'v_l0_ret_norm', 'v_l0_w_out', 'v_l0_ffn_norm', 'v_l0_ffn_w_in', 'v_l0_ffn_conv_w', 'v_l0_ffn_conv_b', 'v_l0_ffn_w_out', 'v_l1_mix_norm', 'v_l1_w_in', 'v_l1_lru_conv_w', 'v_l1_lru_conv_b', 'v_l1_lru_wa', 'v_l1_lru_ba', 'v_l1_lru_wx', 'v_l1_lru_bx', 'v_l1_lru_lambda', 'v_l1_w_out', 'v_l1_ffn_norm', 'v_l1_ffn_w_in', 'v_l1_ffn_conv_w', 'v_l1_ffn_conv_b', 'v_l1_ffn_w_out', 'v_final_norm']
TWIN_OUTPUTS = ['loss', 'grad_x', 'grad_meta_tokens', 'grad_l0_mix_norm', 'grad_l0_w_in', 'grad_l0_ssd_conv_w', 'grad_l0_ssd_conv_b', 'grad_l0_ssd_dt_bias', 'grad_l0_ssd_a_log', 'grad_l0_ssd_d', 'grad_l0_ssd_norm', 'grad_l0_ret_norm', 'grad_l0_w_out', 'grad_l0_ffn_norm', 'grad_l0_ffn_w_in', 'grad_l0_ffn_conv_w', 'grad_l0_ffn_conv_b', 'grad_l0_ffn_w_out', 'grad_l1_mix_norm', 'grad_l1_w_in', 'grad_l1_lru_conv_w', 'grad_l1_lru_conv_b', 'grad_l1_lru_wa', 'grad_l1_lru_ba', 'grad_l1_lru_wx', 'grad_l1_lru_bx', 'grad_l1_lru_lambda', 'grad_l1_w_out', 'grad_l1_ffn_norm', 'grad_l1_ffn_w_in', 'grad_l1_ffn_conv_w', 'grad_l1_ffn_conv_b', 'grad_l1_ffn_w_out', 'grad_final_norm', 'delta_meta_tokens', 'delta_l0_mix_norm', 'delta_l0_w_in', 'delta_l0_ssd_conv_w', 'delta_l0_ssd_conv_b', 'delta_l0_ssd_dt_bias', 'delta_l0_ssd_a_log', 'delta_l0_ssd_d', 'delta_l0_ssd_norm', 'delta_l0_ret_norm', 'delta_l0_w_out', 'delta_l0_ffn_norm', 'delta_l0_ffn_w_in', 'delta_l0_ffn_conv_w', 'delta_l0_ffn_conv_b', 'delta_l0_ffn_w_out', 'delta_l1_mix_norm', 'delta_l1_w_in', 'delta_l1_lru_conv_w', 'delta_l1_lru_conv_b', 'delta_l1_lru_wa', 'delta_l1_lru_ba', 'delta_l1_lru_wx', 'delta_l1_lru_bx', 'delta_l1_lru_lambda', 'delta_l1_w_out', 'delta_l1_ffn_norm', 'delta_l1_ffn_w_in', 'delta_l1_ffn_conv_w', 'delta_l1_ffn_conv_b', 'delta_l1_ffn_w_out', 'delta_final_norm', 'new_m_meta_tokens', 'new_m_l0_mix_norm', 'new_m_l0_w_in', 'new_m_l0_ssd_conv_w', 'new_m_l0_ssd_conv_b', 'new_m_l0_ssd_dt_bias', 'new_m_l0_ssd_a_log', 'new_m_l0_ssd_d', 'new_m_l0_ssd_norm', 'new_m_l0_ret_norm', 'new_m_l0_w_out', 'new_m_l0_ffn_norm', 'new_m_l0_ffn_w_in', 'new_m_l0_ffn_conv_w', 'new_m_l0_ffn_conv_b', 'new_m_l0_ffn_w_out', 'new_m_l1_mix_norm', 'new_m_l1_w_in', 'new_m_l1_lru_conv_w', 'new_m_l1_lru_conv_b', 'new_m_l1_lru_wa', 'new_m_l1_lru_ba', 'new_m_l1_lru_wx', 'new_m_l1_lru_bx', 'new_m_l1_lru_lambda', 'new_m_l1_w_out', 'new_m_l1_ffn_norm', 'new_m_l1_ffn_w_in', 'new_m_l1_ffn_conv_w', 'new_m_l1_ffn_conv_b', 'new_m_l1_ffn_w_out', 'new_m_final_norm', 'new_v_meta_tokens', 'new_v_l0_mix_norm', 'new_v_l0_w_in', 'new_v_l0_ssd_conv_w', 'new_v_l0_ssd_conv_b', 'new_v_l0_ssd_dt_bias', 'new_v_l0_ssd_a_log', 'new_v_l0_ssd_d', 'new_v_l0_ssd_norm', 'new_v_l0_ret_norm', 'new_v_l0_w_out', 'new_v_l0_ffn_norm', 'new_v_l0_ffn_w_in', 'new_v_l0_ffn_conv_w', 'new_v_l0_ffn_conv_b', 'new_v_l0_ffn_w_out', 'new_v_l1_mix_norm', 'new_v_l1_w_in', 'new_v_l1_lru_conv_w', 'new_v_l1_lru_conv_b', 'new_v_l1_lru_wa', 'new_v_l1_lru_ba', 'new_v_l1_lru_wx', 'new_v_l1_lru_bx', 'new_v_l1_lru_lambda', 'new_v_l1_w_out', 'new_v_l1_ffn_norm', 'new_v_l1_ffn_w_in', 'new_v_l1_ffn_conv_w', 'new_v_l1_ffn_conv_b', 'new_v_l1_ffn_w_out', 'new_v_final_norm']
TWIN_LEAF_KINDS = {'loss': 'loss', 'grad_x': 'grad_x', 'grad_meta_tokens': 'grad_w', 'grad_l0_mix_norm': 'grad_w', 'grad_l0_w_in': 'grad_w', 'grad_l0_ssd_conv_w': 'grad_w', 'grad_l0_ssd_conv_b': 'grad_w', 'grad_l0_ssd_dt_bias': 'grad_w', 'grad_l0_ssd_a_log': 'grad_w', 'grad_l0_ssd_d': 'grad_w', 'grad_l0_ssd_norm': 'grad_w', 'grad_l0_ret_norm': 'grad_w', 'grad_l0_w_out': 'grad_w', 'grad_l0_ffn_norm': 'grad_w', 'grad_l0_ffn_w_in': 'grad_w', 'grad_l0_ffn_conv_w': 'grad_w', 'grad_l0_ffn_conv_b': 'grad_w', 'grad_l0_ffn_w_out': 'grad_w', 'grad_l1_mix_norm': 'grad_w', 'grad_l1_w_in': 'grad_w', 'grad_l1_lru_conv_w': 'grad_w', 'grad_l1_lru_conv_b': 'grad_w', 'grad_l1_lru_wa': 'grad_w', 'grad_l1_lru_ba': 'grad_w', 'grad_l1_lru_wx': 'grad_w', 'grad_l1_lru_bx': 'grad_w', 'grad_l1_lru_lambda': 'grad_w', 'grad_l1_w_out': 'grad_w', 'grad_l1_ffn_norm': 'grad_w', 'grad_l1_ffn_w_in': 'grad_w', 'grad_l1_ffn_conv_w': 'grad_w', 'grad_l1_ffn_conv_b': 'grad_w', 'grad_l1_ffn_w_out': 'grad_w', 'grad_final_norm': 'grad_w', 'delta_meta_tokens': 'delta_w', 'delta_l0_mix_norm': 'delta_w', 'delta_l0_w_in': 'delta_w', 'delta_l0_ssd_conv_w': 'delta_w', 'delta_l0_ssd_conv_b': 'delta_w', 'delta_l0_ssd_dt_bias': 'delta_w', 'delta_l0_ssd_a_log': 'delta_w', 'delta_l0_ssd_d': 'delta_w', 'delta_l0_ssd_norm': 'delta_w', 'delta_l0_ret_norm': 'delta_w', 'delta_l0_w_out': 'delta_w', 'delta_l0_ffn_norm': 'delta_w', 'delta_l0_ffn_w_in': 'delta_w', 'delta_l0_ffn_conv_w': 'delta_w', 'delta_l0_ffn_conv_b': 'delta_w', 'delta_l0_ffn_w_out': 'delta_w', 'delta_l1_mix_norm': 'delta_w', 'delta_l1_w_in': 'delta_w', 'delta_l1_lru_conv_w': 'delta_w', 'delta_l1_lru_conv_b': 'delta_w', 'delta_l1_lru_wa': 'delta_w', 'delta_l1_lru_ba': 'delta_w', 'delta_l1_lru_wx': 'delta_w', 'delta_l1_lru_bx': 'delta_w', 'delta_l1_lru_lambda': 'delta_w', 'delta_l1_w_out': 'delta_w', 'delta_l1_ffn_norm': 'delta_w', 'delta_l1_ffn_w_in': 'delta_w', 'delta_l1_ffn_conv_w': 'delta_w', 'delta_l1_ffn_conv_b': 'delta_w', 'delta_l1_ffn_w_out': 'delta_w', 'delta_final_norm': 'delta_w', 'new_m_meta_tokens': 'new_m', 'new_m_l0_mix_norm': 'new_m', 'new_m_l0_w_in': 'new_m', 'new_m_l0_ssd_conv_w': 'new_m', 'new_m_l0_ssd_conv_b': 'new_m', 'new_m_l0_ssd_dt_bias': 'new_m', 'new_m_l0_ssd_a_log': 'new_m', 'new_m_l0_ssd_d': 'new_m', 'new_m_l0_ssd_norm': 'new_m', 'new_m_l0_ret_norm': 'new_m', 'new_m_l0_w_out': 'new_m', 'new_m_l0_ffn_norm': 'new_m', 'new_m_l0_ffn_w_in': 'new_m', 'new_m_l0_ffn_conv_w': 'new_m', 'new_m_l0_ffn_conv_b': 'new_m', 'new_m_l0_ffn_w_out': 'new_m', 'new_m_l1_mix_norm': 'new_m', 'new_m_l1_w_in': 'new_m', 'new_m_l1_lru_conv_w': 'new_m', 'new_m_l1_lru_conv_b': 'new_m', 'new_m_l1_lru_wa': 'new_m', 'new_m_l1_lru_ba': 'new_m', 'new_m_l1_lru_wx': 'new_m', 'new_m_l1_lru_bx': 'new_m', 'new_m_l1_lru_lambda': 'new_m', 'new_m_l1_w_out': 'new_m', 'new_m_l1_ffn_norm': 'new_m', 'new_m_l1_ffn_w_in': 'new_m', 'new_m_l1_ffn_conv_w': 'new_m', 'new_m_l1_ffn_conv_b': 'new_m', 'new_m_l1_ffn_w_out': 'new_m', 'new_m_final_norm': 'new_m', 'new_v_meta_tokens': 'new_v', 'new_v_l0_mix_norm': 'new_v', 'new_v_l0_w_in': 'new_v', 'new_v_l0_ssd_conv_w': 'new_v', 'new_v_l0_ssd_conv_b': 'new_v', 'new_v_l0_ssd_dt_bias': 'new_v', 'new_v_l0_ssd_a_log': 'new_v', 'new_v_l0_ssd_d': 'new_v', 'new_v_l0_ssd_norm': 'new_v', 'new_v_l0_ret_norm': 'new_v', 'new_v_l0_w_out': 'new_v', 'new_v_l0_ffn_norm': 'new_v', 'new_v_l0_ffn_w_in': 'new_v', 'new_v_l0_ffn_conv_w': 'new_v', 'new_v_l0_ffn_conv_b': 'new_v', 'new_v_l0_ffn_w_out': 'new_v', 'new_v_l1_mix_norm': 'new_v', 'new_v_l1_w_in': 'new_v', 'new_v_l1_lru_conv_w': 'new_v', 'new_v_l1_lru_conv_b': 'new_v', 'new_v_l1_lru_wa': 'new_v', 'new_v_l1_lru_ba': 'new_v', 'new_v_l1_lru_wx': 'new_v', 'new_v_l1_lru_bx': 'new_v', 'new_v_l1_lru_lambda': 'new_v', 'new_v_l1_w_out': 'new_v', 'new_v_l1_ffn_norm': 'new_v', 'new_v_l1_ffn_w_in': 'new_v', 'new_v_l1_ffn_conv_w': 'new_v', 'new_v_l1_ffn_conv_b': 'new_v', 'new_v_l1_ffn_w_out': 'new_v', 'new_v_final_norm': 'new_v'}


def _forward(args):
    return _fwd_reference(*[args[k] for k in FWD_PARAMS])


def _output_shape():
    out = _jax.eval_shape(lambda: _forward(_fwd_setup_inputs(0)))
    return out.shape, out.dtype

N_MICROBATCH = 1
ADAM_LR = 0.001
ADAM_B1 = 0.9
ADAM_B2 = 0.999
ADAM_EPS = 1e-08
ADAM_WD = 0.01
ADAM_STEP = 10
PER_EXAMPLE_BATCH_AXIS = {'x': 0, 'loss_target': 0}
SHARED_INPUTS = []
_WEIGHT_DTYPES = {'meta_tokens': _jnp.float32, 'l0_mix_norm': _jnp.float32, 'l0_w_in': _jnp.float32, 'l0_ssd_conv_w': _jnp.float32, 'l0_ssd_conv_b': _jnp.float32, 'l0_ssd_dt_bias': _jnp.float32, 'l0_ssd_a_log': _jnp.float32, 'l0_ssd_d': _jnp.float32, 'l0_ssd_norm': _jnp.float32, 'l0_ret_norm': _jnp.float32, 'l0_w_out': _jnp.float32, 'l0_ffn_norm': _jnp.float32, 'l0_ffn_w_in': _jnp.float32, 'l0_ffn_conv_w': _jnp.float32, 'l0_ffn_conv_b': _jnp.float32, 'l0_ffn_w_out': _jnp.float32, 'l1_mix_norm': _jnp.float32, 'l1_w_in': _jnp.float32, 'l1_lru_conv_w': _jnp.float32, 'l1_lru_conv_b': _jnp.float32, 'l1_lru_wa': _jnp.float32, 'l1_lru_ba': _jnp.float32, 'l1_lru_wx': _jnp.float32, 'l1_lru_bx': _jnp.float32, 'l1_lru_lambda': _jnp.float32, 'l1_w_out': _jnp.float32, 'l1_ffn_norm': _jnp.float32, 'l1_ffn_w_in': _jnp.float32, 'l1_ffn_conv_w': _jnp.float32, 'l1_ffn_conv_b': _jnp.float32, 'l1_ffn_w_out': _jnp.float32, 'final_norm': _jnp.float32}
MOMENT_SCALE = {'meta_tokens': 9.528074e-03, 'l0_mix_norm': 2.363920e-01, 'l0_w_in': 8.635439e-02, 'l0_ssd_conv_w': 9.345771e-02, 'l0_ssd_conv_b': 1.726948e-01, 'l0_ssd_dt_bias': 2.892195e-01, 'l0_ssd_a_log': 4.429460e-01, 'l0_ssd_d': 5.090369e-01, 'l0_ssd_norm': 1.310812e-01, 'l0_ret_norm': 7.217757e-02, 'l0_w_out': 1.457592e-01, 'l0_ffn_norm': 1.339666e-01, 'l0_ffn_w_in': 5.253606e-02, 'l0_ffn_conv_w': 5.177982e-02, 'l0_ffn_conv_b': 6.124453e-02, 'l0_ffn_w_out': 8.627699e-02, 'l1_mix_norm': 1.006514e-01, 'l1_w_in': 4.664536e-02, 'l1_lru_conv_w': 5.699460e-02, 'l1_lru_conv_b': 4.718289e-01, 'l1_lru_wa': 1.321843e-02, 'l1_lru_ba': 1.078393e-02, 'l1_lru_wx': 2.377254e-02, 'l1_lru_bx': 2.038642e-02, 'l1_lru_lambda': 2.348095e-02, 'l1_w_out': 7.634835e-02, 'l1_ffn_norm': 1.000618e-01, 'l1_ffn_w_in': 4.100494e-02, 'l1_ffn_conv_w': 4.087739e-02, 'l1_ffn_conv_b': 3.928525e-02, 'l1_ffn_w_out': 6.696065e-02, 'final_norm': 3.197157e+01}


def _to_microbatches(a, axis):
    t = _jnp.moveaxis(a, axis, 0)
    t = t.reshape((N_MICROBATCH, t.shape[0] // N_MICROBATCH) + t.shape[1:])
    return _jnp.moveaxis(t, 1, axis + 1)


def setup_inputs(seed: int = 0) -> dict:
    inp = _fwd_setup_inputs(seed)
    key = _jax.random.fold_in(_jax.random.key(seed), 7919)
    shape, _ = _output_shape()
    out = dict(inp)
    out["loss_target"] = _jax.random.normal(_jax.random.fold_in(key, 0), shape, _jnp.float32)
    for i, name in enumerate(TWIN_WEIGHTS):
        w = inp[name].astype(_jnp.float32)
        if MOMENT_SCALE is None:
            s = _jnp.sqrt(_jnp.mean(_jnp.square(w)) + 1e-30)
        else:
            s = MOMENT_SCALE[name]
        km, kv = _jax.random.split(_jax.random.fold_in(key, i + 1))
        out[name] = w
        out["m_" + name] = s * _jax.random.normal(km, w.shape, _jnp.float32)
        out["v_" + name] = (s * s) * _jax.random.uniform(kv, w.shape, _jnp.float32, 0.5, 1.5)
    if N_MICROBATCH > 1:
        for name, axis in PER_EXAMPLE_BATCH_AXIS.items():
            out[name] = _to_microbatches(out[name], axis)
    return {'x': out['x'], 'meta_tokens': out['meta_tokens'], 'l0_mix_norm': out['l0_mix_norm'], 'l0_w_in': out['l0_w_in'], 'l0_ssd_conv_w': out['l0_ssd_conv_w'], 'l0_ssd_conv_b': out['l0_ssd_conv_b'], 'l0_ssd_dt_bias': out['l0_ssd_dt_bias'], 'l0_ssd_a_log': out['l0_ssd_a_log'], 'l0_ssd_d': out['l0_ssd_d'], 'l0_ssd_norm': out['l0_ssd_norm'], 'l0_ret_norm': out['l0_ret_norm'], 'l0_w_out': out['l0_w_out'], 'l0_ffn_norm': out['l0_ffn_norm'], 'l0_ffn_w_in': out['l0_ffn_w_in'], 'l0_ffn_conv_w': out['l0_ffn_conv_w'], 'l0_ffn_conv_b': out['l0_ffn_conv_b'], 'l0_ffn_w_out': out['l0_ffn_w_out'], 'l1_mix_norm': out['l1_mix_norm'], 'l1_w_in': out['l1_w_in'], 'l1_lru_conv_w': out['l1_lru_conv_w'], 'l1_lru_conv_b': out['l1_lru_conv_b'], 'l1_lru_wa': out['l1_lru_wa'], 'l1_lru_ba': out['l1_lru_ba'], 'l1_lru_wx': out['l1_lru_wx'], 'l1_lru_bx': out['l1_lru_bx'], 'l1_lru_lambda': out['l1_lru_lambda'], 'l1_w_out': out['l1_w_out'], 'l1_ffn_norm': out['l1_ffn_norm'], 'l1_ffn_w_in': out['l1_ffn_w_in'], 'l1_ffn_conv_w': out['l1_ffn_conv_w'], 'l1_ffn_conv_b': out['l1_ffn_conv_b'], 'l1_ffn_w_out': out['l1_ffn_w_out'], 'final_norm': out['final_norm'], 'loss_target': out['loss_target'], 'm_meta_tokens': out['m_meta_tokens'], 'm_l0_mix_norm': out['m_l0_mix_norm'], 'm_l0_w_in': out['m_l0_w_in'], 'm_l0_ssd_conv_w': out['m_l0_ssd_conv_w'], 'm_l0_ssd_conv_b': out['m_l0_ssd_conv_b'], 'm_l0_ssd_dt_bias': out['m_l0_ssd_dt_bias'], 'm_l0_ssd_a_log': out['m_l0_ssd_a_log'], 'm_l0_ssd_d': out['m_l0_ssd_d'], 'm_l0_ssd_norm': out['m_l0_ssd_norm'], 'm_l0_ret_norm': out['m_l0_ret_norm'], 'm_l0_w_out': out['m_l0_w_out'], 'm_l0_ffn_norm': out['m_l0_ffn_norm'], 'm_l0_ffn_w_in': out['m_l0_ffn_w_in'], 'm_l0_ffn_conv_w': out['m_l0_ffn_conv_w'], 'm_l0_ffn_conv_b': out['m_l0_ffn_conv_b'], 'm_l0_ffn_w_out': out['m_l0_ffn_w_out'], 'm_l1_mix_norm': out['m_l1_mix_norm'], 'm_l1_w_in': out['m_l1_w_in'], 'm_l1_lru_conv_w': out['m_l1_lru_conv_w'], 'm_l1_lru_conv_b': out['m_l1_lru_conv_b'], 'm_l1_lru_wa': out['m_l1_lru_wa'], 'm_l1_lru_ba': out['m_l1_lru_ba'], 'm_l1_lru_wx': out['m_l1_lru_wx'], 'm_l1_lru_bx': out['m_l1_lru_bx'], 'm_l1_lru_lambda': out['m_l1_lru_lambda'], 'm_l1_w_out': out['m_l1_w_out'], 'm_l1_ffn_norm': out['m_l1_ffn_norm'], 'm_l1_ffn_w_in': out['m_l1_ffn_w_in'], 'm_l1_ffn_conv_w': out['m_l1_ffn_conv_w'], 'm_l1_ffn_conv_b': out['m_l1_ffn_conv_b'], 'm_l1_ffn_w_out': out['m_l1_ffn_w_out'], 'm_final_norm': out['m_final_norm'], 'v_meta_tokens': out['v_meta_tokens'], 'v_l0_mix_norm': out['v_l0_mix_norm'], 'v_l0_w_in': out['v_l0_w_in'], 'v_l0_ssd_conv_w': out['v_l0_ssd_conv_w'], 'v_l0_ssd_conv_b': out['v_l0_ssd_conv_b'], 'v_l0_ssd_dt_bias': out['v_l0_ssd_dt_bias'], 'v_l0_ssd_a_log': out['v_l0_ssd_a_log'], 'v_l0_ssd_d': out['v_l0_ssd_d'], 'v_l0_ssd_norm': out['v_l0_ssd_norm'], 'v_l0_ret_norm': out['v_l0_ret_norm'], 'v_l0_w_out': out['v_l0_w_out'], 'v_l0_ffn_norm': out['v_l0_ffn_norm'], 'v_l0_ffn_w_in': out['v_l0_ffn_w_in'], 'v_l0_ffn_conv_w': out['v_l0_ffn_conv_w'], 'v_l0_ffn_conv_b': out['v_l0_ffn_conv_b'], 'v_l0_ffn_w_out': out['v_l0_ffn_w_out'], 'v_l1_mix_norm': out['v_l1_mix_norm'], 'v_l1_w_in': out['v_l1_w_in'], 'v_l1_lru_conv_w': out['v_l1_lru_conv_w'], 'v_l1_lru_conv_b': out['v_l1_lru_conv_b'], 'v_l1_lru_wa': out['v_l1_lru_wa'], 'v_l1_lru_ba': out['v_l1_lru_ba'], 'v_l1_lru_wx': out['v_l1_lru_wx'], 'v_l1_lru_bx': out['v_l1_lru_bx'], 'v_l1_lru_lambda': out['v_l1_lru_lambda'], 'v_l1_w_out': out['v_l1_w_out'], 'v_l1_ffn_norm': out['v_l1_ffn_norm'], 'v_l1_ffn_w_in': out['v_l1_ffn_w_in'], 'v_l1_ffn_conv_w': out['v_l1_ffn_conv_w'], 'v_l1_ffn_conv_b': out['v_l1_ffn_conv_b'], 'v_l1_ffn_w_out': out['v_l1_ffn_w_out'], 'v_final_norm': out['v_final_norm']}


def _loss(weights, diff, rest, loss_target):
    with _jax.named_scope("forward"):
        args = {**rest, TWIN_DIFF_INPUT: diff, **{k: w.astype(_WEIGHT_DTYPES[k]) for k, w in weights.items()}}
        y = _forward(args)
    with _jax.named_scope("loss_head"):
        err = _jnp.square(y.astype(_jnp.float32) - loss_target)
        return 0.5 * _jnp.sum(_jnp.mean(err, axis=-1)) if err.ndim else 0.5 * err


def _adamw(w, g, m, v):
    m = ADAM_B1 * m + (1.0 - ADAM_B1) * g
    v = ADAM_B2 * v + (1.0 - ADAM_B2) * _jnp.square(g)
    m_hat = m / (1.0 - ADAM_B1 ** ADAM_STEP)
    v_hat = v / (1.0 - ADAM_B2 ** ADAM_STEP)
    delta = -ADAM_LR * (m_hat / (_jnp.sqrt(v_hat) + ADAM_EPS) + ADAM_WD * w)
    return delta, m, v


def reference(x, meta_tokens, l0_mix_norm, l0_w_in, l0_ssd_conv_w, l0_ssd_conv_b, l0_ssd_dt_bias, l0_ssd_a_log, l0_ssd_d, l0_ssd_norm, l0_ret_norm, l0_w_out, l0_ffn_norm, l0_ffn_w_in, l0_ffn_conv_w, l0_ffn_conv_b, l0_ffn_w_out, l1_mix_norm, l1_w_in, l1_lru_conv_w, l1_lru_conv_b, l1_lru_wa, l1_lru_ba, l1_lru_wx, l1_lru_bx, l1_lru_lambda, l1_w_out, l1_ffn_norm, l1_ffn_w_in, l1_ffn_conv_w, l1_ffn_conv_b, l1_ffn_w_out, final_norm, loss_target, m_meta_tokens, m_l0_mix_norm, m_l0_w_in, m_l0_ssd_conv_w, m_l0_ssd_conv_b, m_l0_ssd_dt_bias, m_l0_ssd_a_log, m_l0_ssd_d, m_l0_ssd_norm, m_l0_ret_norm, m_l0_w_out, m_l0_ffn_norm, m_l0_ffn_w_in, m_l0_ffn_conv_w, m_l0_ffn_conv_b, m_l0_ffn_w_out, m_l1_mix_norm, m_l1_w_in, m_l1_lru_conv_w, m_l1_lru_conv_b, m_l1_lru_wa, m_l1_lru_ba, m_l1_lru_wx, m_l1_lru_bx, m_l1_lru_lambda, m_l1_w_out, m_l1_ffn_norm, m_l1_ffn_w_in, m_l1_ffn_conv_w, m_l1_ffn_conv_b, m_l1_ffn_w_out, m_final_norm, v_meta_tokens, v_l0_mix_norm, v_l0_w_in, v_l0_ssd_conv_w, v_l0_ssd_conv_b, v_l0_ssd_dt_bias, v_l0_ssd_a_log, v_l0_ssd_d, v_l0_ssd_norm, v_l0_ret_norm, v_l0_w_out, v_l0_ffn_norm, v_l0_ffn_w_in, v_l0_ffn_conv_w, v_l0_ffn_conv_b, v_l0_ffn_w_out, v_l1_mix_norm, v_l1_w_in, v_l1_lru_conv_w, v_l1_lru_conv_b, v_l1_lru_wa, v_l1_lru_ba, v_l1_lru_wx, v_l1_lru_bx, v_l1_lru_lambda, v_l1_w_out, v_l1_ffn_norm, v_l1_ffn_w_in, v_l1_ffn_conv_w, v_l1_ffn_conv_b, v_l1_ffn_w_out, v_final_norm):
    given = dict(x=x, meta_tokens=meta_tokens, l0_mix_norm=l0_mix_norm, l0_w_in=l0_w_in, l0_ssd_conv_w=l0_ssd_conv_w, l0_ssd_conv_b=l0_ssd_conv_b, l0_ssd_dt_bias=l0_ssd_dt_bias, l0_ssd_a_log=l0_ssd_a_log, l0_ssd_d=l0_ssd_d, l0_ssd_norm=l0_ssd_norm, l0_ret_norm=l0_ret_norm, l0_w_out=l0_w_out, l0_ffn_norm=l0_ffn_norm, l0_ffn_w_in=l0_ffn_w_in, l0_ffn_conv_w=l0_ffn_conv_w, l0_ffn_conv_b=l0_ffn_conv_b, l0_ffn_w_out=l0_ffn_w_out, l1_mix_norm=l1_mix_norm, l1_w_in=l1_w_in, l1_lru_conv_w=l1_lru_conv_w, l1_lru_conv_b=l1_lru_conv_b, l1_lru_wa=l1_lru_wa, l1_lru_ba=l1_lru_ba, l1_lru_wx=l1_lru_wx, l1_lru_bx=l1_lru_bx, l1_lru_lambda=l1_lru_lambda, l1_w_out=l1_w_out, l1_ffn_norm=l1_ffn_norm, l1_ffn_w_in=l1_ffn_w_in, l1_ffn_conv_w=l1_ffn_conv_w, l1_ffn_conv_b=l1_ffn_conv_b, l1_ffn_w_out=l1_ffn_w_out, final_norm=final_norm, loss_target=loss_target, m_meta_tokens=m_meta_tokens, m_l0_mix_norm=m_l0_mix_norm, m_l0_w_in=m_l0_w_in, m_l0_ssd_conv_w=m_l0_ssd_conv_w, m_l0_ssd_conv_b=m_l0_ssd_conv_b, m_l0_ssd_dt_bias=m_l0_ssd_dt_bias, m_l0_ssd_a_log=m_l0_ssd_a_log, m_l0_ssd_d=m_l0_ssd_d, m_l0_ssd_norm=m_l0_ssd_norm, m_l0_ret_norm=m_l0_ret_norm, m_l0_w_out=m_l0_w_out, m_l0_ffn_norm=m_l0_ffn_norm, m_l0_ffn_w_in=m_l0_ffn_w_in, m_l0_ffn_conv_w=m_l0_ffn_conv_w, m_l0_ffn_conv_b=m_l0_ffn_conv_b, m_l0_ffn_w_out=m_l0_ffn_w_out, m_l1_mix_norm=m_l1_mix_norm, m_l1_w_in=m_l1_w_in, m_l1_lru_conv_w=m_l1_lru_conv_w, m_l1_lru_conv_b=m_l1_lru_conv_b, m_l1_lru_wa=m_l1_lru_wa, m_l1_lru_ba=m_l1_lru_ba, m_l1_lru_wx=m_l1_lru_wx, m_l1_lru_bx=m_l1_lru_bx, m_l1_lru_lambda=m_l1_lru_lambda, m_l1_w_out=m_l1_w_out, m_l1_ffn_norm=m_l1_ffn_norm, m_l1_ffn_w_in=m_l1_ffn_w_in, m_l1_ffn_conv_w=m_l1_ffn_conv_w, m_l1_ffn_conv_b=m_l1_ffn_conv_b, m_l1_ffn_w_out=m_l1_ffn_w_out, m_final_norm=m_final_norm, v_meta_tokens=v_meta_tokens, v_l0_mix_norm=v_l0_mix_norm, v_l0_w_in=v_l0_w_in, v_l0_ssd_conv_w=v_l0_ssd_conv_w, v_l0_ssd_conv_b=v_l0_ssd_conv_b, v_l0_ssd_dt_bias=v_l0_ssd_dt_bias, v_l0_ssd_a_log=v_l0_ssd_a_log, v_l0_ssd_d=v_l0_ssd_d, v_l0_ssd_norm=v_l0_ssd_norm, v_l0_ret_norm=v_l0_ret_norm, v_l0_w_out=v_l0_w_out, v_l0_ffn_norm=v_l0_ffn_norm, v_l0_ffn_w_in=v_l0_ffn_w_in, v_l0_ffn_conv_w=v_l0_ffn_conv_w, v_l0_ffn_conv_b=v_l0_ffn_conv_b, v_l0_ffn_w_out=v_l0_ffn_w_out, v_l1_mix_norm=v_l1_mix_norm, v_l1_w_in=v_l1_w_in, v_l1_lru_conv_w=v_l1_lru_conv_w, v_l1_lru_conv_b=v_l1_lru_conv_b, v_l1_lru_wa=v_l1_lru_wa, v_l1_lru_ba=v_l1_lru_ba, v_l1_lru_wx=v_l1_lru_wx, v_l1_lru_bx=v_l1_lru_bx, v_l1_lru_lambda=v_l1_lru_lambda, v_l1_w_out=v_l1_w_out, v_l1_ffn_norm=v_l1_ffn_norm, v_l1_ffn_w_in=v_l1_ffn_w_in, v_l1_ffn_conv_w=v_l1_ffn_conv_w, v_l1_ffn_conv_b=v_l1_ffn_conv_b, v_l1_ffn_w_out=v_l1_ffn_w_out, v_final_norm=v_final_norm)
    weights = {n: given[n] for n in TWIN_WEIGHTS}
    shared = {n: given[n] for n in SHARED_INPUTS}
    per_example = {n: given[n] for n in ['x']}
    grad_fn = _jax.value_and_grad(_loss, argnums=(0, 1))

    def one_microbatch(ex, loss_target):
        ex = dict(ex)
        diff = ex.pop(TWIN_DIFF_INPUT)
        return grad_fn(weights, diff, {**shared, **ex}, loss_target)

    if N_MICROBATCH == 1:
        loss, (grad_w, grad_x) = one_microbatch(per_example, given["loss_target"])
    else:
        def body(carry, xs):
            loss_sum, grad_sum = carry
            l_k, (gw_k, gx_k) = one_microbatch(xs[0], xs[1])
            with _jax.named_scope("update"):
                return (loss_sum + l_k, _jax.tree.map(_jnp.add, grad_sum, gw_k)), gx_k

        init = (_jnp.zeros((), _jnp.float32), _jax.tree.map(_jnp.zeros_like, weights))
        (loss, grad_w), grad_x = _jax.lax.scan(body, init, (per_example, given["loss_target"]))
    with _jax.named_scope("update"):
        delta_w, new_m, new_v = {}, {}, {}
        for n in TWIN_WEIGHTS:
            delta_w[n], new_m[n], new_v[n] = _adamw(weights[n], grad_w[n], given["m_" + n], given["v_" + n])
    return (loss, grad_x, *[grad_w[n] for n in TWIN_WEIGHTS], *[delta_w[n] for n in TWIN_WEIGHTS],
            *[new_m[n] for n in TWIN_WEIGHTS], *[new_v[n] for n in TWIN_WEIGHTS])
```

```python
import math

import numpy as np
import jax
import jax.numpy as jnp
from jax import lax
from jax.experimental import pallas as pl
from jax.experimental.pallas import tpu as pltpu

F32 = jnp.float32
BF16 = jnp.bfloat16
_MXU = jnp.bfloat16

D = 1024
CH = 128
N_META = 16
PAD = CH - N_META
EPS = 1e-6

SSD_HEADS = 16
SSD_HD = 64
SSD_GROUPS = 4
RET_HEADS = 4
RET_DK = 256
SB_HEADS = 16
SB_HD = 64
LRU_BLOCKS = 8
LRU_C = 8.0
FFN = 2816
U0_Z = 4096
U0_XBC = 5120

VMEM_LIMIT = 56 * 1024 * 1024


def _cparams(sem):
    return pltpu.CompilerParams(dimension_semantics=sem, vmem_limit_bytes=VMEM_LIMIT)


def _dot(a, b, dims=((1,), (0,))):
    return lax.dot_general(a.astype(_MXU), b.astype(_MXU), (dims, ((), ())), preferred_element_type=F32)


def _dot_nt(a, b):
    return _dot(a, b, ((1,), (1,)))


def _dot_tn(a, b):
    return _dot(a.T, b)


def _dot_exact(a, b):
    return lax.dot_general(a, b, (((1,), (0,)), ((), ())), preferred_element_type=F32,
                           precision=lax.Precision.HIGHEST)


def _dot_split(x, m01):
    hi = x.astype(BF16)
    lo = (x - hi.astype(F32)).astype(BF16)
    m = m01.astype(BF16)
    return jnp.dot(hi, m, preferred_element_type=F32) + jnp.dot(lo, m, preferred_element_type=F32)


def _sigmoid(x):
    return jax.nn.sigmoid(x)


def _softplus(x):
    return jnp.maximum(x, 0.0) + jnp.log1p(jnp.exp(-jnp.abs(x)))


def _silu(x):
    return x * _sigmoid(x)


def _dsilu(x):
    s = _sigmoid(x)
    return s * (1.0 + x * (1.0 - s))


_GELU_C = math.sqrt(2.0 / math.pi)


def _gelu(x):
    return 0.5 * x * (1.0 + jnp.tanh(_GELU_C * (x + 0.044715 * x * x * x)))


def _dgelu(x):
    t = jnp.tanh(_GELU_C * (x + 0.044715 * x * x * x))
    return 0.5 * (1.0 + t) + 0.5 * x * (1.0 - t * t) * _GELU_C * (1.0 + 3.0 * 0.044715 * x * x)


def _row_ids(n, cols=1):
    return lax.broadcasted_iota(jnp.int32, (n, cols), 0)


def _lane_ids(rows, n):
    return lax.broadcasted_iota(jnp.int32, (rows, n), 1)


def _real_rows(chunk):
    return chunk * CH + _row_ids(CH) >= PAD


def _shift_down(prev8, cur, s):
    cat = jnp.concatenate([prev8, cur], axis=0)
    return pltpu.roll(cat, s, axis=0)[8:]


def _shift_up(cur, next8, s):
    n = cur.shape[0]
    cat = jnp.concatenate([cur, next8], axis=0)
    return pltpu.roll(cat, n + 8 - s, axis=0)[:n]


def _conv_pre(prev8, cur, w_ref, b_ref, K):
    acc = cur * w_ref[K - 1:K, :] + b_ref[...]
    for s in range(1, K):
        acc = acc + _shift_down(prev8, cur, s) * w_ref[K - 1 - s:K - s, :]
    return acc


def _prev8_map(nch, col):
    return lambda b, c: (jnp.maximum((b * nch + c) * (CH // 8) - 1, 0), col)


def _matmul(a, b, mode, out_dtype, tm, tn, tk, name, add=None, b_off=0):
    if mode == "nn":
        (M, K), (_, N) = a.shape, b.shape
    elif mode == "nt":
        (M, K), N = a.shape, b.shape[0]
    else:
        (K, M), (_, N) = a.shape, b.shape
    tm, tn, tk = min(tm, M), min(tn, N), min(tk, K)
    assert M % tm == 0 and N % tn == 0 and K % tk == 0 and b_off % tk == 0, (name, M, N, K, tm, tn, tk)
    koff = b_off // tk
    nk = K // tk
    dims = {"nn": ((1,), (0,)), "nt": ((1,), (1,)), "tn": ((0,), (0,))}[mode]
    if mode == "tn":
        a_spec = pl.BlockSpec((tk, tm), lambda i, j, k: (k, i))
    else:
        a_spec = pl.BlockSpec((tm, tk), lambda i, j, k: (i, k))
    if mode == "nt":
        b_spec = pl.BlockSpec((tn, tk), lambda i, j, k: (j, k + koff))
    else:
        b_spec = pl.BlockSpec((tk, tn), lambda i, j, k: (k, j))
    o_spec = pl.BlockSpec((tm, tn), lambda i, j, k: (i, j))
    has_add = add is not None

    def body(a_ref, b_ref, *rest):
        if has_add:
            add_ref, o_ref, acc = rest
        else:
            o_ref, acc = rest
        k = pl.program_id(2)

        @pl.when(k == 0)
        def _():
            acc[...] = jnp.zeros_like(acc)

        acc[...] += _dot(a_ref[...], b_ref[...], dims)

        @pl.when(k == nk - 1)
        def _():
            r = acc[...]
            if has_add:
                r = r + add_ref[...].astype(F32)
            o_ref[...] = r.astype(out_dtype)

    in_specs = [a_spec, b_spec] + ([o_spec] if has_add else [])
    args = (a, b) + ((add,) if has_add else ())
    return pl.pallas_call(
        body, name=name, grid=(M // tm, N // tn, nk),
        in_specs=in_specs, out_specs=o_spec,
        out_shape=jax.ShapeDtypeStruct((M, N), out_dtype),
        scratch_shapes=[pltpu.VMEM((tm, tn), F32)],
        compiler_params=_cparams(("parallel", "parallel", "arbitrary")),
    )(*args)


def _tile(n, prefs):
    for t in prefs:
        if n % t == 0:
            return t
    return n


def _mm(a, b, mode, out_dtype, name, add=None, b_off=0):
    if mode == "tn":
        K, M = a.shape
        N = b.shape[1]
        tm, tn, tk = _tile(M, (1024, 1408, 512, 256, 128)), _tile(N, (512, 256, 128)), _tile(K, (2176, 384, 256, 128))
    else:
        M, K = a.shape
        N = b.shape[1] if mode == "nn" else b.shape[0]
        tm, tn, tk = _tile(M, (1088, 768, 384, 256, 128)), _tile(N, (512, 256, 128)), _tile(K, (1024, 1408, 512, 256, 128))
    return _matmul(a, b, mode, out_dtype, tm, tn, tk, name, add=add, b_off=b_off)


def _rmsnorm_fwd(h, g, name):
    R = h.shape[0]
    tr = 2 * CH

    def body(h_ref, g_ref, o_ref):
        x = h_ref[...]
        r = lax.rsqrt(jnp.mean(x * x, axis=-1, keepdims=True) + EPS)
        o_ref[...] = (x * r * g_ref[...]).astype(o_ref.dtype)

    return pl.pallas_call(
        body, name=name, grid=(R // tr,),
        in_specs=[pl.BlockSpec((tr, D), lambda i: (i, 0)), pl.BlockSpec((1, D), lambda i: (0, 0))],
        out_specs=pl.BlockSpec((tr, D), lambda i: (i, 0)),
        out_shape=jax.ShapeDtypeStruct((R, D), _MXU),
        compiler_params=_cparams(("parallel",)),
    )(h, g)


def _rmsnorm_bwd(h, g, dn, dres, nch, name):
    R = h.shape[0]

    def body(h_ref, g_ref, dn_ref, dres_ref, dh_ref, dg_ref):
        i = pl.program_id(0)
        x = h_ref[...]
        r = lax.rsqrt(jnp.mean(x * x, axis=-1, keepdims=True) + EPS)
        xhat = x * r
        dn_v = dn_ref[...]
        dx = dn_v * g_ref[...]
        dh = r * (dx - xhat * jnp.mean(dx * xhat, axis=-1, keepdims=True))
        dh_ref[...] = jnp.where(_real_rows(i % nch), dres_ref[...] + dh, 0.0)

        @pl.when(i == 0)
        def _():
            dg_ref[...] = jnp.zeros_like(dg_ref)

        dg_ref[...] += jnp.sum(dn_v * xhat, axis=0, keepdims=True)

    row = pl.BlockSpec((CH, D), lambda i: (i, 0))
    vec = pl.BlockSpec((1, D), lambda i: (0, 0))
    return pl.pallas_call(
        body, name=name, grid=(R // CH,),
        in_specs=[row, vec, row, row], out_specs=[row, vec],
        out_shape=[jax.ShapeDtypeStruct((R, D), F32), jax.ShapeDtypeStruct((1, D), F32)],
        compiler_params=_cparams(("arbitrary",)),
    )(h, g, dn, dres)


def _ssd_prep(u0, udt, conv_w, conv_b, dt_bias, B, nch):
    R = u0.shape[0]

    def body(xs_ref, xsp_ref, bc_ref, bcp_ref, udt_ref, w0_ref, w1_ref, b0_ref, b1_ref, dtb_ref,
             act_ref, dt_ref, dtt_ref):
        keep = _real_rows(pl.program_id(1))
        a0 = _silu(_conv_pre(xsp_ref[...], xs_ref[...], w0_ref, b0_ref, 4))
        a1 = _silu(_conv_pre(bcp_ref[...], bc_ref[...], w1_ref, b1_ref, 4))
        act_ref[:, :1024] = jnp.where(keep, a0, 0.0)
        act_ref[:, 1024:] = jnp.where(keep, a1, 0.0)
        ok = jnp.logical_and(keep, _lane_ids(1, 128) < SSD_HEADS)
        dt = jnp.where(ok, _softplus(udt_ref[...] + dtb_ref[...]), 0.0)
        dt_ref[...] = dt
        dtt_ref[...] = dt.T

    row = lambda col: pl.BlockSpec((CH, 1024), lambda b, c: (b * nch + c, col))
    prev = lambda col: pl.BlockSpec((8, 1024), _prev8_map(nch, col))
    return pl.pallas_call(
        body, name="ssd_prep", grid=(B, nch),
        in_specs=[row(5), prev(5), row(6), prev(6),
                  pl.BlockSpec((CH, 128), lambda b, c: (b * nch + c, 0)),
                  pl.BlockSpec((4, 1024), lambda b, c: (0, 0)), pl.BlockSpec((4, 1024), lambda b, c: (0, 1)),
                  pl.BlockSpec((1, 1024), lambda b, c: (0, 0)), pl.BlockSpec((1, 1024), lambda b, c: (0, 1)),
                  pl.BlockSpec((1, 128), lambda b, c: (0, 0))],
        out_specs=[pl.BlockSpec((CH, 2048), lambda b, c: (b * nch + c, 0)),
                   pl.BlockSpec((CH, 128), lambda b, c: (b * nch + c, 0)),
                   pl.BlockSpec((128, CH), lambda b, c: (0, b * nch + c))],
        out_shape=[jax.ShapeDtypeStruct((R, 2048), F32), jax.ShapeDtypeStruct((R, 128), F32),
                   jax.ShapeDtypeStruct((128, R), F32)],
        compiler_params=_cparams(("parallel", "parallel")),
    )(u0, u0, u0, u0, udt, conv_w, conv_w, conv_b, conv_b, dt_bias)


def _ssd_head_terms(h, a_vec, dt_v, dtt_v, dsk_v):
    lane = _lane_ids(1, 128)
    sub = _row_ids(128)
    r = _row_ids(CH, CH)
    cidx = _lane_ids(CH, CH)
    a_h = jnp.sum(jnp.where(lane == h, a_vec, 0.0), axis=1, keepdims=True)
    dt_col = jnp.sum(jnp.where(lane == h, dt_v, 0.0), axis=1, keepdims=True)
    dt_row = jnp.sum(jnp.where(sub == h, dtt_v, 0.0), axis=0, keepdims=True)
    cs_col = jnp.sum(jnp.where(r >= cidx, dt_row * a_h, 0.0), axis=1, keepdims=True)
    cs_row = jnp.sum(jnp.where(r <= cidx, dt_col * a_h, 0.0), axis=0, keepdims=True)
    tot = jnp.sum(dt_col * a_h, axis=0, keepdims=True)
    dsk = jnp.sum(jnp.where(lane == h, dsk_v, 0.0), axis=1, keepdims=True)
    return a_h, dt_col, cs_col, cs_row, tot, dsk


def _ssd_fwd(act, u0, dt, dtt, a_log, d_skip, norm_g, B, nch):
    R = act.shape[0]

    def body(xs_ref, bm_ref, cm_ref, z_ref, dt_ref, dtt_ref, alog_ref, dsk_ref, ng_ref,
             out_ref, ypre_ref, hin_ref, H):
        g = pl.program_id(1)
        c = pl.program_id(2)

        @pl.when(c == 0)
        def _():
            H[...] = jnp.zeros_like(H)

        hin_ref[...] = H[...]
        a_vec = -jnp.exp(alog_ref[...])
        dt_v = dt_ref[...]
        dtt_v = dtt_ref[...]
        hm = _lane_ids(1, 128) < SSD_HD
        r = _row_ids(CH, CH)
        cidx = _lane_ids(CH, CH)
        Bm = bm_ref[...]
        Cm = cm_ref[...]
        CB = _dot_nt(Cm, Bm)
        ys = []
        for pair in range(2):
            cols = slice(128 * pair, 128 * pair + 128)
            xraw = xs_ref[:, cols]
            t = [_ssd_head_terms(4 * g + 2 * pair + j, a_vec, dt_v, dtt_v, dsk_ref[...]) for j in range(2)]
            sel = lambda f: jnp.where(hm, f(t[0]), f(t[1]))
            dtp = sel(lambda q: q[1])
            Ep = sel(lambda q: jnp.exp(q[2]))
            Wp = sel(lambda q: jnp.exp(q[4] - q[2]))
            etot = sel(lambda q: jnp.exp(q[4]))
            dsk = sel(lambda q: q[5])
            X = xraw * dtp
            ydiag = jnp.zeros((CH, 128), F32)
            for j in range(2):
                Lm = jnp.where(r >= cidx, jnp.exp(t[j][2] - t[j][3]), 0.0)
                Xh = jnp.where(hm if j == 0 else jnp.logical_not(hm), X, 0.0)
                ydiag = ydiag + _dot(CB * Lm, Xh)
            Hp = H[:, cols]
            yoff = Ep * _dot(Cm, Hp)
            S = _dot(Bm.T, X * Wp)
            H[:, cols] = etot * Hp + S
            ys.append(ydiag + yoff + xraw * dsk)
        y = jnp.concatenate(ys, axis=1)
        ypre_ref[...] = y
        yg = y * _silu(z_ref[...])
        rr = lax.rsqrt(jnp.mean(yg * yg, axis=-1, keepdims=True) + EPS)
        out_ref[...] = jnp.where(_real_rows(c), yg * rr * ng_ref[...], 0.0).astype(out_ref.dtype)

    rowb = lambda w, colf: pl.BlockSpec((CH, w), lambda b, g, c: (b * nch + c, colf(g)))
    vec = pl.BlockSpec((1, 128), lambda b, g, c: (0, 0))
    return pl.pallas_call(
        body, name="ssd_fwd", grid=(B, SSD_GROUPS, nch),
        in_specs=[rowb(256, lambda g: g), rowb(128, lambda g: 8 + g), rowb(128, lambda g: 12 + g),
                  rowb(256, lambda g: 16 + g), rowb(128, lambda g: 0),
                  pl.BlockSpec((128, CH), lambda b, g, c: (0, b * nch + c)),
                  vec, vec, pl.BlockSpec((1, 256), lambda b, g, c: (0, g))],
        out_specs=[rowb(256, lambda g: g), rowb(256, lambda g: g),
                   pl.BlockSpec((None, None, None, 128, 256), lambda b, g, c: (b, g, c, 0, 0))],
        out_shape=[jax.ShapeDtypeStruct((R, 2048), _MXU), jax.ShapeDtypeStruct((R, 1024), F32),
                   jax.ShapeDtypeStruct((B, SSD_GROUPS, nch, 128, 256), F32)],
        scratch_shapes=[pltpu.VMEM((128, 256), F32)],
        compiler_params=_cparams(("parallel", "parallel", "arbitrary")),
    )(act, act, act, u0, dt, dtt, a_log, d_skip, norm_g)


def _ssd_bwd(dycat, ypre, u0, act, dt, dtt, hin, a_log, d_skip, norm_g, B, nch):
    R = act.shape[0]

    def body(dy_ref, ypre_ref, z_ref, xs_ref, bm_ref, cm_ref, dt_ref, dtt_ref, hin_ref, alog_ref, dsk_ref, ng_ref,
             dz_ref, dxs_ref, db_ref, dc_ref, ddt_ref, pg_ref, dH):
        g = pl.program_id(1)
        c = nch - 1 - pl.program_id(2)

        @pl.when(pl.program_id(2) == 0)
        def _():
            dH[...] = jnp.zeros_like(dH)
            pg_ref[...] = jnp.zeros_like(pg_ref)

        z = z_ref[...]
        y = ypre_ref[...]
        ng = ng_ref[...]
        dout = jnp.where(_real_rows(c), dy_ref[...], 0.0)
        sz = _sigmoid(z)
        yg = y * z * sz
        rr = lax.rsqrt(jnp.mean(yg * yg, axis=-1, keepdims=True) + EPS)
        nrm = yg * rr
        pg_ref[0:1, :] += jnp.sum(dout * nrm, axis=0, keepdims=True)
        dn = dout * ng
        dyg = rr * (dn - nrm * jnp.mean(dn * nrm, axis=-1, keepdims=True))
        dy = dyg * z * sz
        dz_ref[...] = dyg * y * (sz * (1.0 + z * (1.0 - sz)))

        a_vec = -jnp.exp(alog_ref[...])
        dt_v = dt_ref[...]
        dtt_v = dtt_ref[...]
        lane = _lane_ids(1, 128)
        hm = lane < SSD_HD
        r = _row_ids(CH, CH)
        cidx = _lane_ids(CH, CH)
        last = _row_ids(CH) == CH - 1
        Bm = bm_ref[...]
        Cm = cm_ref[...]
        CB = _dot_nt(Cm, Bm)
        CBT = _dot_nt(Bm, Cm)
        dB = jnp.zeros((CH, 128), F32)
        dC = jnp.zeros((CH, 128), F32)
        dcs_all = jnp.zeros((CH, 128), F32)
        dtx_all = jnp.zeros((CH, 128), F32)
        dd_row = jnp.zeros((1, 128), F32)
        dxs = []
        for pair in range(2):
            cols = slice(128 * pair, 128 * pair + 128)
            xraw = xs_ref[:, cols]
            dyp = dy[:, cols]
            heads = [4 * g + 2 * pair + j for j in range(2)]
            t = [_ssd_head_terms(heads[j], a_vec, dt_v, dtt_v, dsk_ref[...]) for j in range(2)]
            sel = lambda f: jnp.where(hm, f(t[0]), f(t[1]))
            hsum = lambda v, j: jnp.sum(jnp.where(hm if j == 0 else jnp.logical_not(hm), v, 0.0), axis=1, keepdims=True)
            dtp = sel(lambda q: q[1])
            Ep = sel(lambda q: jnp.exp(q[2]))
            Wp = sel(lambda q: jnp.exp(q[4] - q[2]))
            etot = sel(lambda q: jnp.exp(q[4]))
            dsk = sel(lambda q: q[5])
            X = xraw * dtp
            Hp = hin_ref[:, cols]
            dHn = dH[:, cols]
            dskip = jnp.sum(dyp * xraw, axis=0, keepdims=True)
            yoff = Ep * _dot(Cm, Hp)
            dE = dyp * yoff
            dC = dC + _dot_nt(dyp * Ep, Hp)
            dH[:, cols] = etot * dHn + _dot(Cm.T, dyp * Ep)
            BdS = _dot(Bm, dHn)
            dX = Wp * BdS
            ew = X * BdS * Wp
            dB = dB + _dot_nt(X * Wp, dHn)
            hh = jnp.sum(dHn * Hp, axis=0, keepdims=True) * etot
            for j in range(2):
                hmask = hm if j == 0 else jnp.logical_not(hm)
                cs_col, cs_row = t[j][2], t[j][3]
                Lm = jnp.where(r >= cidx, jnp.exp(cs_col - cs_row), 0.0)
                LmT = jnp.where(cidx >= r, jnp.exp(cs_row - cs_col), 0.0)
                dyh = jnp.where(hmask, dyp, 0.0)
                Xh = jnp.where(hmask, X, 0.0)
                dM = _dot_nt(dyh, Xh)
                dMT = _dot_nt(Xh, dyh)
                M = CB * Lm
                MT = CBT * LmT
                dX = dX + _dot(MT, dyh)
                dC = dC + _dot(dM * Lm, Bm)
                dB = dB + _dot(dMT * LmT, Cm)
                g_rows = jnp.sum(dM * M, axis=1, keepdims=True)
                g_cols = jnp.sum(dMT * MT, axis=1, keepdims=True)
                dtot = (jnp.sum(hsum(ew, j), axis=0, keepdims=True)
                        + jnp.sum(jnp.where(hmask, hh, 0.0), axis=1, keepdims=True))
                dcs = g_rows - g_cols + hsum(dE, j) - hsum(ew, j) + jnp.where(last, dtot, 0.0)
                dcs_all = dcs_all + jnp.where(lane == heads[j], dcs, 0.0)
                dtx_all = dtx_all + jnp.where(lane == heads[j], hsum(dX * xraw, j), 0.0)
                dd_row = dd_row + jnp.where(lane == heads[j],
                                            jnp.sum(jnp.where(hmask, dskip, 0.0), axis=1, keepdims=True), 0.0)
            dxs.append(dX * dtp + dyp * dsk)
        dxs_ref[...] = jnp.concatenate(dxs, axis=1)
        db_ref[...] = dB
        dc_ref[...] = dC
        dadt = _dot_exact(jnp.where(cidx >= r, 1.0, 0.0), dcs_all)
        ddt_ref[...] = dadt * a_vec + dtx_all
        pg_ref[1:2, 0:128] += dd_row
        pg_ref[2:3, 0:128] += jnp.sum(dadt * dt_v, axis=0, keepdims=True) * a_vec

    rowb = lambda w, colf: pl.BlockSpec((CH, w), lambda b, g, c: (b * nch + nch - 1 - c, colf(g)))
    vec = pl.BlockSpec((1, 128), lambda b, g, c: (0, 0))
    return pl.pallas_call(
        body, name="ssd_bwd", grid=(B, SSD_GROUPS, nch),
        in_specs=[rowb(256, lambda g: g), rowb(256, lambda g: g), rowb(256, lambda g: 16 + g), rowb(256, lambda g: g),
                  rowb(128, lambda g: 8 + g), rowb(128, lambda g: 12 + g), rowb(128, lambda g: 0),
                  pl.BlockSpec((128, CH), lambda b, g, c: (0, b * nch + nch - 1 - c)),
                  pl.BlockSpec((None, None, None, 128, 256), lambda b, g, c: (b, g, nch - 1 - c, 0, 0)),
                  vec, vec, pl.BlockSpec((1, 256), lambda b, g, c: (0, g))],
        out_specs=[rowb(256, lambda g: g), rowb(256, lambda g: g), rowb(128, lambda g: g), rowb(128, lambda g: g),
                   rowb(128, lambda g: g),
                   pl.BlockSpec((None, None, 8, 256), lambda b, g, c: (b, g, 0, 0))],
        out_shape=[jax.ShapeDtypeStruct((R, 1024), F32), jax.ShapeDtypeStruct((R, 1024), F32),
                   jax.ShapeDtypeStruct((R, 512), F32), jax.ShapeDtypeStruct((R, 512), F32),
                   jax.ShapeDtypeStruct((R, 512), F32), jax.ShapeDtypeStruct((B, SSD_GROUPS, 8, 256), F32)],
        scratch_shapes=[pltpu.VMEM((128, 256), F32)],
        compiler_params=_cparams(("parallel", "parallel", "arbitrary")),
    )(dycat, ypre, u0, act, act, act, dt, dtt, hin, a_log, d_skip, norm_g)


def _ssd_prep_bwd(dxs, dB, dC, ddt4, u0, udt, conv_w, conv_b, dt_bias, B, nch):
    R = u0.shape[0]

    def body(dxs_ref, db_ref, dc_ref, ddt_ref, xs_ref, xsp_ref, bc_ref, bcp_ref, udt_ref, w0_ref, w1_ref, b0_ref, b1_ref,
             dtb_ref, dpre_ref, ddtr_ref, pgd_ref):
        c = pl.program_id(1)

        @pl.when(c == 0)
        def _():
            pgd_ref[...] = jnp.zeros_like(pgd_ref)

        keep = _real_rows(c)
        p0 = _conv_pre(xsp_ref[...], xs_ref[...], w0_ref, b0_ref, 4)
        p1 = _conv_pre(bcp_ref[...], bc_ref[...], w1_ref, b1_ref, 4)
        dpre_ref[:, :1024] = jnp.where(keep, dxs_ref[...] * _dsilu(p0), 0.0)
        dpre_ref[:, 1024:] = jnp.where(keep, jnp.concatenate([db_ref[...], dc_ref[...]], axis=1) * _dsilu(p1), 0.0)
        ddt = ddt_ref[:, 0:128] + ddt_ref[:, 128:256] + ddt_ref[:, 256:384] + ddt_ref[:, 384:512]
        ok = jnp.logical_and(keep, _lane_ids(1, 128) < SSD_HEADS)
        dr = jnp.where(ok, ddt * _sigmoid(udt_ref[...] + dtb_ref[...]), 0.0)
        ddtr_ref[...] = dr
        pgd_ref[0:1, :] += jnp.sum(dr, axis=0, keepdims=True)

    rw = lambda w: pl.BlockSpec((CH, w), lambda b, c: (b * nch + c, 0))
    row = lambda col: pl.BlockSpec((CH, 1024), lambda b, c: (b * nch + c, col))
    prev = lambda col: pl.BlockSpec((8, 1024), _prev8_map(nch, col))
    return pl.pallas_call(
        body, name="ssd_prep_bwd", grid=(B, nch),
        in_specs=[rw(1024), rw(512), rw(512), rw(512), row(5), prev(5), row(6), prev(6), rw(128),
                  pl.BlockSpec((4, 1024), lambda b, c: (0, 0)), pl.BlockSpec((4, 1024), lambda b, c: (0, 1)),
                  pl.BlockSpec((1, 1024), lambda b, c: (0, 0)), pl.BlockSpec((1, 1024), lambda b, c: (0, 1)),
                  pl.BlockSpec((1, 128), lambda b, c: (0, 0))],
        out_specs=[rw(2048), rw(128), pl.BlockSpec((None, 8, 128), lambda b, c: (b, 0, 0))],
        out_shape=[jax.ShapeDtypeStruct((R, 2048), F32), jax.ShapeDtypeStruct((R, 128), F32),
                   jax.ShapeDtypeStruct((B, 8, 128), F32)],
        compiler_params=_cparams(("parallel", "arbitrary")),
    )(dxs, dB, dC, ddt4, u0, u0, u0, u0, udt, conv_w, conv_w, conv_b, conv_b, dt_bias)


def _conv_bwd(dpre, xin, xin_col, w, K, name, tc=512):
    R, C = dpre.shape
    assert C % tc == 0 and xin_col % tc == 0
    nr = R // CH
    xoff = xin_col // tc

    def body(dp_ref, dpn_ref, x_ref, xp_ref, w_ref, din_ref, dw_ref):
        i = pl.program_id(1)

        @pl.when(i == 0)
        def _():
            dw_ref[...] = jnp.zeros_like(dw_ref)

        dp = dp_ref[...]
        nxt = dpn_ref[...] * (i < nr - 1).astype(F32)
        x = x_ref[...]
        xp = xp_ref[...]
        din = dp * w_ref[K - 1:K, :]
        dw_ref[K - 1:K, :] += jnp.sum(dp * x, axis=0, keepdims=True)
        dw_ref[7:8, :] += jnp.sum(dp, axis=0, keepdims=True)
        for s in range(1, K):
            din = din + _shift_up(dp, nxt, s) * w_ref[K - 1 - s:K - s, :]
            dw_ref[K - 1 - s:K - s, :] += jnp.sum(dp * _shift_down(xp, x, s), axis=0, keepdims=True)
        din_ref[...] = din

    return pl.pallas_call(
        body, name=name, grid=(C // tc, nr),
        in_specs=[pl.BlockSpec((CH, tc), lambda j, i: (i, j)),
                  pl.BlockSpec((8, tc), lambda j, i: (jnp.minimum((i + 1) * (CH // 8), nr * (CH // 8) - 1), j)),
                  pl.BlockSpec((CH, tc), lambda j, i: (i, xoff + j)),
                  pl.BlockSpec((8, tc), lambda j, i: (jnp.maximum(i * (CH // 8) - 1, 0), xoff + j)),
                  pl.BlockSpec((K, tc), lambda j, i: (0, j))],
        out_specs=[pl.BlockSpec((CH, tc), lambda j, i: (i, j)),
                   pl.BlockSpec((8, tc), lambda j, i: (0, j))],
        out_shape=[jax.ShapeDtypeStruct((R, C), F32), jax.ShapeDtypeStruct((8, C), F32)],
        compiler_params=_cparams(("parallel", "arbitrary")),
    )(dpre, dpre, xin, xin, w)


_RET_LG = [float(v) for v in np.log1p(-np.exp2(-5.0 - np.arange(RET_HEADS, dtype=np.float32))).astype(np.float32)]
_RET_SCALE = RET_DK ** -0.5


def _rope_tables(nch):
    half = RET_DK // 2
    inv_freq = 1.0 / (10000.0 ** (jnp.arange(half, dtype=F32) / (half - 1)))
    pos = jnp.arange(nch * CH, dtype=F32) - PAD
    ang = pos[:, None] * inv_freq[None, :]
    return jnp.cos(ang), jnp.sin(ang)


def _rot(x, cos, sin):
    x1, x2 = x[:, :128], x[:, 128:]
    return jnp.concatenate([x1 * cos - x2 * sin, x1 * sin + x2 * cos], axis=1)


def _unrot(d, cos, sin):
    d1, d2 = d[:, :128], d[:, 128:]
    return jnp.concatenate([d1 * cos + d2 * sin, d2 * cos - d1 * sin], axis=1)


def _ret_decays(lg):
    r = _row_ids(CH, CH)
    cidx = _lane_ids(CH, CH)
    diff = (r - cidx).astype(F32)
    decay = jnp.where(r >= cidx, jnp.exp(lg * jnp.maximum(diff, 0.0)), 0.0)
    decay_t = jnp.where(cidx >= r, jnp.exp(lg * jnp.maximum(-diff, 0.0)), 0.0)
    idx = _row_ids(CH).astype(F32)
    zeta = jnp.exp(lg * (CH - 1.0 - idx))
    xi = jnp.exp(lg * (idx + 1.0))
    return decay, decay_t, zeta, xi


def _ret_fwd(u0, ycat, cos, sin, norm_g, B, nch):
    R = u0.shape[0]

    def body(u_ref, cos_ref, sin_ref, ng_ref, ycat_in, out_ref, opre_ref, rin_ref, Rst):
        c = pl.program_id(1)

        @pl.when(c == 0)
        def _():
            Rst[...] = jnp.zeros_like(Rst)

        cos_v, sin_v = cos_ref[...], sin_ref[...]
        for h in range(RET_HEADS):
            lg = _RET_LG[h]
            cols = slice(256 * h, 256 * h + 256)
            decay, _, zeta, xi = _ret_decays(lg)
            qr = _rot(u_ref[:, cols], cos_v, sin_v)
            kr = _rot(u_ref[:, 1024 + 256 * h:1024 + 256 * h + 256], cos_v, sin_v) * _RET_SCALE
            v = u_ref[:, 2048 + 256 * h:2048 + 256 * h + 256]
            gate = u_ref[:, 3072 + 256 * h:3072 + 256 * h + 256]
            Rh = Rst[h]
            rin_ref[h] = Rh
            inner = _dot(_dot_nt(qr, kr) * decay, v)
            cross = _dot(qr, Rh) * xi
            Rst[h] = math.exp(CH * lg) * Rh + _dot((kr * zeta).T, v)
            o = inner + cross
            opre_ref[:, cols] = o
            oc = o - jnp.mean(o, axis=-1, keepdims=True)
            rr = lax.rsqrt(jnp.mean(oc * oc, axis=-1, keepdims=True) + EPS)
            out_ref[:, cols] = (_silu(gate) * (oc * rr * ng_ref[:, cols])).astype(out_ref.dtype)

    return pl.pallas_call(
        body, name="ret_fwd", grid=(B, nch),
        in_specs=[pl.BlockSpec((CH, 4096), lambda b, c: (b * nch + c, 0)),
                  pl.BlockSpec((CH, 128), lambda b, c: (c, 0)), pl.BlockSpec((CH, 128), lambda b, c: (c, 0)),
                  pl.BlockSpec((1, 1024), lambda b, c: (0, 0)),
                  pl.BlockSpec(memory_space=pl.ANY)],
        out_specs=[pl.BlockSpec((CH, 1024), lambda b, c: (b * nch + c, 1)),
                   pl.BlockSpec((CH, 1024), lambda b, c: (b * nch + c, 0)),
                   pl.BlockSpec((None, None, RET_HEADS, 256, 256), lambda b, c: (b, c, 0, 0, 0))],
        out_shape=[jax.ShapeDtypeStruct(ycat.shape, ycat.dtype), jax.ShapeDtypeStruct((R, 1024), F32),
                   jax.ShapeDtypeStruct((B, nch, RET_HEADS, 256, 256), F32)],
        scratch_shapes=[pltpu.VMEM((RET_HEADS, 256, 256), F32)],
        input_output_aliases={4: 0},
        compiler_params=_cparams(("parallel", "arbitrary")),
    )(u0, cos, sin, norm_g, ycat)


def _ret_bwd(dycat, u0, opre, rin, cos, sin, norm_g, B, nch):
    R = u0.shape[0]

    def body(dy_ref, u_ref, opre_ref, rin_ref, cos_ref, sin_ref, ng_ref, du_ref, pg_ref, dR):
        @pl.when(pl.program_id(1) == 0)
        def _():
            dR[...] = jnp.zeros_like(dR)
            pg_ref[...] = jnp.zeros_like(pg_ref)

        cos_v, sin_v = cos_ref[...], sin_ref[...]
        for h in range(RET_HEADS):
            lg = _RET_LG[h]
            cols = slice(256 * h, 256 * h + 256)
            decay, decay_t, zeta, xi = _ret_decays(lg)
            qr = _rot(u_ref[:, cols], cos_v, sin_v)
            kr = _rot(u_ref[:, 1024 + 256 * h:1024 + 256 * h + 256], cos_v, sin_v) * _RET_SCALE
            v = u_ref[:, 2048 + 256 * h:2048 + 256 * h + 256]
            gate = u_ref[:, 3072 + 256 * h:3072 + 256 * h + 256]
            ng = ng_ref[:, cols]
            o = opre_ref[:, cols]
            oc = o - jnp.mean(o, axis=-1, keepdims=True)
            rr = lax.rsqrt(jnp.mean(oc * oc, axis=-1, keepdims=True) + EPS)
            ohat = oc * rr
            dout = dy_ref[:, cols]
            du_ref[:, 3072 + 256 * h:3072 + 256 * h + 256] = dout * (ohat * ng) * _dsilu(gate)
            don = dout * _silu(gate)
            pg_ref[0:1, cols] += jnp.sum(don * ohat, axis=0, keepdims=True)
            dohat = don * ng
            do = rr * (dohat - jnp.mean(dohat, axis=-1, keepdims=True)
                       - ohat * jnp.mean(dohat * ohat, axis=-1, keepdims=True))
            Rh = rin_ref[h]
            dRn = dR[h]
            sc_t = _dot_nt(kr, qr) * decay_t
            dv = _dot(sc_t, do) + _dot(kr * zeta, dRn)
            ds = _dot_nt(do, v) * decay
            ds_t = _dot_nt(v, do) * decay_t
            dox = do * xi
            dq = _dot(ds, kr) + _dot_nt(dox, Rh)
            dk = _dot(ds_t, qr) + zeta * _dot_nt(v, dRn)
            dR[h] = math.exp(CH * lg) * dRn + _dot(qr.T, dox)
            du_ref[:, cols] = _unrot(dq, cos_v, sin_v)
            du_ref[:, 1024 + 256 * h:1024 + 256 * h + 256] = _unrot(dk, cos_v, sin_v) * _RET_SCALE
            du_ref[:, 2048 + 256 * h:2048 + 256 * h + 256] = dv

    rmap = lambda b, c: (b * nch + nch - 1 - c, 0)
    return pl.pallas_call(
        body, name="ret_bwd", grid=(B, nch),
        in_specs=[pl.BlockSpec((CH, 1024), lambda b, c: (b * nch + nch - 1 - c, 1)),
                  pl.BlockSpec((CH, 4096), rmap), pl.BlockSpec((CH, 1024), rmap),
                  pl.BlockSpec((None, None, RET_HEADS, 256, 256), lambda b, c: (b, nch - 1 - c, 0, 0, 0)),
                  pl.BlockSpec((CH, 128), lambda b, c: (nch - 1 - c, 0)),
                  pl.BlockSpec((CH, 128), lambda b, c: (nch - 1 - c, 0)),
                  pl.BlockSpec((1, 1024), lambda b, c: (0, 0))],
        out_specs=[pl.BlockSpec((CH, 4096), rmap), pl.BlockSpec((None, 8, 1024), lambda b, c: (b, 0, 0))],
        out_shape=[jax.ShapeDtypeStruct((R, 4096), F32), jax.ShapeDtypeStruct((B, 8, 1024), F32)],
        scratch_shapes=[pltpu.VMEM((RET_HEADS, 256, 256), F32)],
        compiler_params=_cparams(("parallel", "arbitrary")),
    )(dycat, u0, opre, rin, cos, sin, norm_g)


_SB_SCALE = SB_HD ** -0.5


def _sb_valid(qb, kb):
    qpos = qb * CH + _row_ids(CH, CH)
    kpos = kb * CH + _lane_ids(CH, CH)
    return jnp.logical_and(kpos < qpos, kpos >= PAD)


def _sb_fwd(u1, B, nch):
    R = u1.shape[0]
    Pn = nch * CH

    def body(q_ref, k_ref, v_ref, out_ref, s_ref):
        qb = pl.program_id(2)
        q = q_ref[...] * _SB_SCALE
        hm = _lane_ids(1, 128) < SB_HD
        masks = [hm, jnp.logical_not(hm)]
        mgt = (_row_ids(CH, CH) > _lane_ids(CH, CH)).astype(F32)

        def step(i, carry):
            out, a0, a1 = carry
            kb = qb - i
            start = pl.multiple_of(kb * CH, CH)
            kblk = k_ref[pl.ds(start, CH), :]
            vblk = v_ref[pl.ds(start, CH), :]
            valid = _sb_valid(qb, kb)
            accs = [a0, a1]
            for j in range(2):
                z = _dot_nt(jnp.where(masks[j], q, 0.0), kblk)
                sp = _softplus(z)
                lm = jnp.where(valid, -sp, 0.0)
                after = _dot_split(lm, mgt) + accs[j]
                w = jnp.where(valid, jnp.exp(z - sp + after), 0.0)
                out = out + jnp.where(masks[j], _dot(w, vblk), 0.0)
                accs[j] = accs[j] + jnp.sum(lm, axis=1, keepdims=True)
            return out, accs[0], accs[1]

        zero = jnp.zeros((CH, 1), F32)
        out, a0, a1 = lax.fori_loop(0, qb + 1, step, (jnp.zeros((CH, 128), F32), zero, zero))
        out_ref[...] = out.astype(out_ref.dtype)
        s_ref[...] = jnp.where(hm, a0, a1)

    qspec = lambda off: pl.BlockSpec((CH, 128), lambda b, hp, qb: (b * nch + qb, off + hp))
    kspec = lambda off: pl.BlockSpec((Pn, 128), lambda b, hp, qb: (b, off + hp))
    return pl.pallas_call(
        body, name="sb_fwd", grid=(B, SB_HEADS // 2, nch),
        in_specs=[qspec(0), kspec(8), kspec(16)],
        out_specs=[qspec(0), qspec(0)],
        out_shape=[jax.ShapeDtypeStruct((R, 2048), _MXU), jax.ShapeDtypeStruct((R, 1024), F32)],
        compiler_params=_cparams(("parallel", "parallel", "arbitrary")),
    )(u1, u1, u1)


def _sb_bwd(dycat, u1, stot, B, nch):
    R = u1.shape[0]
    Pn = nch * CH

    def body(q_ref, k_ref, v_ref, do_ref, s_ref, dq_ref, dk_ref, dv_ref):
        qb = pl.program_id(2)

        @pl.when(qb == 0)
        def _():
            dk_ref[...] = jnp.zeros_like(dk_ref)
            dv_ref[...] = jnp.zeros_like(dv_ref)

        q = q_ref[...] * _SB_SCALE
        dout = do_ref[...]
        stv = s_ref[...]
        lane = _lane_ids(1, 128)
        hm = lane < SB_HD
        masks = [hm, jnp.logical_not(hm)]
        rr = _row_ids(CH, CH)
        cc = _lane_ids(CH, CH)
        mle = (rr <= cc).astype(F32)
        mlt = (rr < cc).astype(F32)
        stot_h = [jnp.sum(jnp.where(lane == 0, stv, 0.0), axis=1, keepdims=True),
                  jnp.sum(jnp.where(lane == SB_HD, stv, 0.0), axis=1, keepdims=True)]

        def step(kb, carry):
            dq, p0, p1, g0, g1 = carry
            start = pl.multiple_of(kb * CH, CH)
            kblk = k_ref[pl.ds(start, CH), :]
            vblk = v_ref[pl.ds(start, CH), :]
            valid = _sb_valid(qb, kb)
            pacc, gacc = [p0, p1], [g0, g1]
            dkb = jnp.zeros((CH, 128), F32)
            dvb = jnp.zeros((CH, 128), F32)
            for j in range(2):
                qh = jnp.where(masks[j], q, 0.0)
                doh = jnp.where(masks[j], dout, 0.0)
                z = _dot_nt(qh, kblk)
                sp = _softplus(z)
                lm = jnp.where(valid, -sp, 0.0)
                after = stot_h[j] - (_dot_split(lm, mle) + pacc[j])
                w = jnp.where(valid, jnp.exp(z - sp + after), 0.0)
                gg = _dot_nt(doh, vblk) * w
                gp = _dot_split(gg, mlt) + gacc[j]
                sig = jnp.exp(z - sp)
                dz = jnp.where(valid, gg * (1.0 - sig) - gp * sig, 0.0)
                dq = dq + jnp.where(masks[j], _dot(dz, kblk), 0.0)
                dkb = dkb + _dot_tn(dz, qh)
                dvb = dvb + _dot_tn(w, doh)
                pacc[j] = pacc[j] + jnp.sum(lm, axis=1, keepdims=True)
                gacc[j] = gacc[j] + jnp.sum(gg, axis=1, keepdims=True)
            dk_ref[pl.ds(start, CH), :] += dkb
            dv_ref[pl.ds(start, CH), :] += dvb
            return dq, pacc[0], pacc[1], gacc[0], gacc[1]

        zero = jnp.zeros((CH, 1), F32)
        dq = lax.fori_loop(0, qb + 1, step, (jnp.zeros((CH, 128), F32), zero, zero, zero, zero))[0]
        dq_ref[...] = dq * _SB_SCALE

    qspec = lambda off: pl.BlockSpec((CH, 128), lambda b, hp, qb: (b * nch + qb, off + hp))
    kspec = lambda off: pl.BlockSpec((Pn, 128), lambda b, hp, qb: (b, off + hp))
    full = jax.ShapeDtypeStruct((R, 1024), F32)
    return pl.pallas_call(
        body, name="sb_bwd", grid=(B, SB_HEADS // 2, nch),
        in_specs=[qspec(0), kspec(8), kspec(16), qspec(0), qspec(0)],
        out_specs=[qspec(0), kspec(0), kspec(0)],
        out_shape=[full, full, full],
        compiler_params=_cparams(("parallel", "parallel", "arbitrary")),
    )(u1, u1, u1, dycat, stot)


def _neg_expm1(x):
    series = -(x * (1.0 + x * (0.5 + x * (1.0 / 6.0 + x * (1.0 / 24.0)))))
    return jnp.where(x > -0.05, series, 1.0 - jnp.exp(x))


def _lru_gates(x, wa_ref, ba_ref, wx_ref, bx_ref, lam_ref):
    rs, is_ = [], []
    for n in range(LRU_BLOCKS):
        xb = x[:, 128 * n:128 * n + 128]
        rs.append(_dot(xb, wa_ref[n]))
        is_.append(_dot(xb, wx_ref[n]))
    r = _sigmoid(jnp.concatenate(rs, axis=1) + ba_ref[...])
    i = _sigmoid(jnp.concatenate(is_, axis=1) + bx_ref[...])
    sp = _softplus(-lam_ref[...])
    la = -LRU_C * r * sp
    a = jnp.exp(la)
    mult = jnp.sqrt(jnp.maximum(_neg_expm1(2.0 * la), 0.0))
    return r, i, sp, a, mult


def _lru_fwd(u1, ycat, conv_w, conv_b, wa, ba, wx, bx, lam, B, nch):
    R = u1.shape[0]

    def body(x_ref, xp_ref, gate_ref, cw_ref, cb_ref, wa_ref, ba_ref, wx_ref, bx_ref, lam_ref, ycat_in,
             out_ref, hs_ref, hc):
        c = pl.program_id(1)

        @pl.when(c == 0)
        def _():
            hc[...] = jnp.zeros_like(hc)

        x = _conv_pre(xp_ref[...], x_ref[...], cw_ref, cb_ref, 4)
        r, i, sp, a, mult = _lru_gates(x, wa_ref, ba_ref, wx_ref, bx_ref, lam_ref)
        b = jnp.where(_real_rows(c), mult * (i * x), 0.0)
        rows = _row_ids(CH)
        s = 1
        while s < CH:
            a_s = jnp.where(rows >= s, pltpu.roll(a, s, axis=0), 1.0)
            b_s = jnp.where(rows >= s, pltpu.roll(b, s, axis=0), 0.0)
            b = a * b_s + b
            a = a * a_s
            s *= 2
        h = a * hc[0:1, :] + b
        hs_ref[...] = h
        hc[0:1, :] = hs_ref[CH - 1:CH, :]
        out_ref[...] = (h * _gelu(gate_ref[...])).astype(out_ref.dtype)

    row = lambda col: pl.BlockSpec((CH, 1024), lambda b, c: (b * nch + c, col))
    vec = pl.BlockSpec((1, 1024), lambda b, c: (0, 0))
    wsp = pl.BlockSpec((LRU_BLOCKS, 128, 128), lambda b, c: (0, 0, 0))
    return pl.pallas_call(
        body, name="lru_fwd", grid=(B, nch),
        in_specs=[row(4), pl.BlockSpec((8, 1024), _prev8_map(nch, 4)), row(3),
                  pl.BlockSpec((4, 1024), lambda b, c: (0, 0)), vec, wsp, vec, wsp, vec, vec,
                  pl.BlockSpec(memory_space=pl.ANY)],
        out_specs=[row(1), row(0)],
        out_shape=[jax.ShapeDtypeStruct(ycat.shape, ycat.dtype), jax.ShapeDtypeStruct((R, 1024), F32)],
        scratch_shapes=[pltpu.VMEM((8, 1024), F32)],
        input_output_aliases={10: 0},
        compiler_params=_cparams(("parallel", "arbitrary")),
    )(u1, u1, u1, conv_w, conv_b, wa, ba, wx, bx, lam, ycat)


def _lru_bwd(dycat, u1, hs, conv_w, conv_b, wa, ba, wx, bx, lam, B, nch):
    R = u1.shape[0]

    def body(dy_ref, x_ref, xp_ref, gate_ref, hs_ref, hsp_ref, cw_ref, cb_ref, wa_ref, ba_ref, wx_ref, bx_ref, lam_ref,
             dgate_ref, dxc_ref, pg_ref, dwa_ref, dwx_ref, lc):
        c = nch - 1 - pl.program_id(1)

        @pl.when(pl.program_id(1) == 0)
        def _():
            lc[...] = jnp.zeros_like(lc)
            pg_ref[...] = jnp.zeros_like(pg_ref)
            dwa_ref[...] = jnp.zeros_like(dwa_ref)
            dwx_ref[...] = jnp.zeros_like(dwx_ref)

        x = _conv_pre(xp_ref[...], x_ref[...], cw_ref, cb_ref, 4)
        r, i, sp, a, mult = _lru_gates(x, wa_ref, ba_ref, wx_ref, bx_ref, lam_ref)
        h = hs_ref[...]
        hprev = _shift_down(hsp_ref[...], h, 1)
        gate = gate_ref[...]
        dy = dy_ref[...]
        dgate_ref[...] = dy * h * _dgelu(gate)
        rows = _row_ids(CH)
        lam_t = dy * _gelu(gate) + jnp.where(rows == CH - 1, lc[0:1, :], 0.0)
        coef = jnp.where(rows < CH - 1, pltpu.roll(a, CH - 1, axis=0), 0.0)
        s = 1
        while s < CH:
            c_s = jnp.where(rows < CH - s, pltpu.roll(coef, CH - s, axis=0), 1.0)
            l_s = jnp.where(rows < CH - s, pltpu.roll(lam_t, CH - s, axis=0), 0.0)
            lam_t = coef * l_s + lam_t
            coef = coef * c_s
            s *= 2
        lc[0:1, :] = jnp.sum(jnp.where(rows == 0, a * lam_t, 0.0), axis=0, keepdims=True)
        db = jnp.where(_real_rows(c), lam_t, 0.0)
        da = db * hprev
        dmult = db * (i * x)
        di = db * mult * x
        dx = db * mult * i
        pos = mult > 0.0
        dla = da * a + jnp.where(pos, -dmult * (a * a) / jnp.where(pos, mult, 1.0), 0.0)
        dr = dla * (-LRU_C * sp)
        pg_ref[2:3, :] += jnp.sum(dla * (LRU_C * r) * _sigmoid(-lam_ref[...]), axis=0, keepdims=True)
        dpr = dr * r * (1.0 - r)
        dpi = di * i * (1.0 - i)
        pg_ref[0:1, :] += jnp.sum(dpr, axis=0, keepdims=True)
        pg_ref[1:2, :] += jnp.sum(dpi, axis=0, keepdims=True)
        dxs = []
        for n in range(LRU_BLOCKS):
            blk = slice(128 * n, 128 * n + 128)
            dxs.append(dx[:, blk] + _dot_nt(dpr[:, blk], wa_ref[n]) + _dot_nt(dpi[:, blk], wx_ref[n]))
            dwa_ref[n] += _dot_tn(x[:, blk], dpr[:, blk])
            dwx_ref[n] += _dot_tn(x[:, blk], dpi[:, blk])
        dxc_ref[...] = jnp.concatenate(dxs, axis=1)

    rmap = lambda col: (lambda b, c: (b * nch + nch - 1 - c, col))
    row = lambda col: pl.BlockSpec((CH, 1024), rmap(col))
    prev = lambda col: pl.BlockSpec(
        (8, 1024), lambda b, c: (jnp.maximum((b * nch + nch - 1 - c) * (CH // 8) - 1, 0), col))
    vec = pl.BlockSpec((1, 1024), lambda b, c: (0, 0))
    wsp = pl.BlockSpec((LRU_BLOCKS, 128, 128), lambda b, c: (0, 0, 0))
    full = jax.ShapeDtypeStruct((R, 1024), F32)
    return pl.pallas_call(
        body, name="lru_bwd", grid=(B, nch),
        in_specs=[row(1), row(4), prev(4), row(3), row(0), prev(0),
                  pl.BlockSpec((4, 1024), lambda b, c: (0, 0)), vec, wsp, vec, wsp, vec, vec],
        out_specs=[row(0), row(0), pl.BlockSpec((None, 8, 1024), lambda b, c: (b, 0, 0)),
                   pl.BlockSpec((None, LRU_BLOCKS, 128, 128), lambda b, c: (b, 0, 0, 0)),
                   pl.BlockSpec((None, LRU_BLOCKS, 128, 128), lambda b, c: (b, 0, 0, 0))],
        out_shape=[full, full, jax.ShapeDtypeStruct((B, 8, 1024), F32),
                   jax.ShapeDtypeStruct((B, LRU_BLOCKS, 128, 128), F32),
                   jax.ShapeDtypeStruct((B, LRU_BLOCKS, 128, 128), F32)],
        scratch_shapes=[pltpu.VMEM((8, 1024), F32)],
        compiler_params=_cparams(("parallel", "arbitrary")),
    )(dycat, u1, u1, u1, hs, hs, conv_w, conv_b, wa, ba, wx, bx, lam)


_FFN_TC = 256


def _ffn_specs(nch):
    nt = FFN // _FFN_TC
    row = lambda off: pl.BlockSpec((CH, _FFN_TC), lambda b, c, j: (b * nch + c, off + j))
    prev = lambda off: pl.BlockSpec(
        (8, _FFN_TC), lambda b, c, j: (jnp.maximum((b * nch + c) * (CH // 8) - 1, 0), off + j))
    wsp = lambda off: pl.BlockSpec((3, _FFN_TC), lambda b, c, j: (0, off + j))
    bsp = lambda off: pl.BlockSpec((1, _FFN_TC), lambda b, c, j: (0, off + j))
    return nt, row, [row(0), prev(0), row(nt), prev(nt), wsp(0), wsp(nt), bsp(0), bsp(nt)]


def _ffn_act_fwd(uf, conv_w, conv_b, B, nch):
    R = uf.shape[0]
    nt, row, specs = _ffn_specs(nch)

    def body(g_ref, gp_ref, u_ref, up_ref, wg_ref, wu_ref, bg_ref, bu_ref, o_ref):
        cg = _conv_pre(gp_ref[...], g_ref[...], wg_ref, bg_ref, 3)
        cu = _conv_pre(up_ref[...], u_ref[...], wu_ref, bu_ref, 3)
        o_ref[...] = jnp.where(_real_rows(pl.program_id(1)), _silu(cg) * cu, 0.0).astype(o_ref.dtype)

    return pl.pallas_call(
        body, name="ffn_act_fwd", grid=(B, nch, nt),
        in_specs=specs, out_specs=row(0),
        out_shape=jax.ShapeDtypeStruct((R, FFN), _MXU),
        compiler_params=_cparams(("parallel", "parallel", "parallel")),
    )(uf, uf, uf, uf, conv_w, conv_w, conv_b, conv_b)


def _ffn_act_bwd(da, uf, conv_w, conv_b, B, nch):
    R = uf.shape[0]
    nt, row, specs = _ffn_specs(nch)

    def body(da_ref, g_ref, gp_ref, u_ref, up_ref, wg_ref, wu_ref, bg_ref, bu_ref, dg_ref, du_ref):
        cg = _conv_pre(gp_ref[...], g_ref[...], wg_ref, bg_ref, 3)
        cu = _conv_pre(up_ref[...], u_ref[...], wu_ref, bu_ref, 3)
        dav = jnp.where(_real_rows(pl.program_id(1)), da_ref[...], 0.0)
        dg_ref[...] = dav * cu * _dsilu(cg)
        du_ref[...] = dav * _silu(cg)

    full = jax.ShapeDtypeStruct((R, FFN), F32)
    return pl.pallas_call(
        body, name="ffn_act_bwd", grid=(B, nch, nt),
        in_specs=[row(0)] + specs, out_specs=[row(0), row(0)],
        out_shape=[full, full],
        compiler_params=_cparams(("parallel", "parallel", "parallel")),
    )(da, uf, uf, uf, uf, conv_w, conv_w, conv_b, conv_b)


def _head(h, g, target, B, nch):
    R = h.shape[0]

    def body(h_ref, g_ref, t_ref, dh_ref, loss_ref, dg_ref):
        c = pl.program_id(1)

        @pl.when(c == 0)
        def _():
            dh_ref[...] = jnp.zeros_like(dh_ref)
            loss_ref[...] = jnp.zeros_like(loss_ref)
            dg_ref[...] = jnp.zeros_like(dg_ref)

        @pl.when(c > 0)
        def _():
            x = h_ref[...]
            gv = g_ref[...]
            r = lax.rsqrt(jnp.mean(x * x, axis=-1, keepdims=True) + EPS)
            xhat = x * r
            e = xhat * gv - t_ref[...]
            loss_ref[...] += 0.5 * jnp.sum(jnp.mean(e * e, axis=-1, keepdims=True), axis=0, keepdims=True)
            dy = e * (1.0 / D)
            dg_ref[0:1, :] += jnp.sum(dy * xhat, axis=0, keepdims=True)
            dx = dy * gv
            dh_ref[...] = r * (dx - xhat * jnp.mean(dx * xhat, axis=-1, keepdims=True))

    row = pl.BlockSpec((CH, D), lambda b, c: (b * nch + c, 0))
    return pl.pallas_call(
        body, name="head", grid=(B, nch),
        in_specs=[row, pl.BlockSpec((1, D), lambda b, c: (0, 0)),
                  pl.BlockSpec((CH, D), lambda b, c: (b * (nch - 1) + jnp.maximum(c - 1, 0), 0))],
        out_specs=[row, pl.BlockSpec((None, 8, 128), lambda b, c: (b, 0, 0)),
                   pl.BlockSpec((None, 8, D), lambda b, c: (b, 0, 0))],
        out_shape=[jax.ShapeDtypeStruct((R, D), F32), jax.ShapeDtypeStruct((B, 8, 128), F32),
                   jax.ShapeDtypeStruct((B, 8, D), F32)],
        compiler_params=_cparams(("parallel", "arbitrary")),
    )(h, g, target)


ADAM_LR = 0.001
ADAM_B1 = 0.9
ADAM_B2 = 0.999
ADAM_EPS = 1e-08
ADAM_WD = 0.01
ADAM_STEP = 10


def _adamw(w, g, m, v, name):
    Rr, C = w.shape
    tr = 64 if Rr % 64 == 0 else Rr

    def body(w_ref, g_ref, m_ref, v_ref, d_ref, nm_ref, nv_ref):
        gv = g_ref[...]
        nm = ADAM_B1 * m_ref[...] + (1.0 - ADAM_B1) * gv
        nv = ADAM_B2 * v_ref[...] + (1.0 - ADAM_B2) * (gv * gv)
        m_hat = nm / (1.0 - ADAM_B1 ** ADAM_STEP)
        v_hat = nv / (1.0 - ADAM_B2 ** ADAM_STEP)
        d_ref[...] = -ADAM_LR * (m_hat / (jnp.sqrt(v_hat) + ADAM_EPS) + ADAM_WD * w_ref[...])
        nm_ref[...] = nm
        nv_ref[...] = nv

    spec = pl.BlockSpec((tr, C), lambda i: (i, 0))
    sh = jax.ShapeDtypeStruct((Rr, C), F32)
    return pl.pallas_call(
        body, name=name, grid=(Rr // tr,),
        in_specs=[spec] * 4, out_specs=[spec] * 3, out_shape=[sh] * 3,
        compiler_params=_cparams(("parallel",)),
    )(w, g, m, v)


_MESH = pl.DeviceIdType.MESH
_ANY = pl.BlockSpec(memory_space=pl.ANY)


def _place():
    x, y, c = lax.axis_index("x"), lax.axis_index("y"), lax.axis_index("c")
    chips = [(1 - x, y), (x, 1 - y), (1 - x, 1 - y)]
    return x, y, c, chips


def _rcopy(src, dst, ssem, rsem, dev):
    return pltpu.make_async_remote_copy(src_ref=src, dst_ref=dst, send_sem=ssem, recv_sem=rsem,
                                        device_id=dev, device_id_type=_MESH)


def _gather_shards(bigs, small):
    nb = len(bigs)
    halves = [b.shape[0] // 2 for b in bigs]

    def body(*refs):
        ins, outs = refs[:nb + 1], refs[nb + 1:2 * nb + 2]
        ssem, rsem, fssem, frsem, lsem = refs[2 * nb + 2:]
        x, y, c, chips = _place()
        k = 2 * x + y
        sib = (x, y, 1 - c)

        def part(a, slot, hc):
            if a == nb:
                return outs[a].at[slot]
            return outs[a].at[slot, pl.ds(hc * halves[a], halves[a]), :]

        local = [pltpu.make_async_copy(ins[a], outs[a].at[k], lsem.at[a]) for a in range(nb + 1)]
        for cp in local:
            cp.start()
        first = []
        for a in range(nb + 1):
            src = ins[a] if a == nb else ins[a].at[pl.ds(c * halves[a], halves[a]), :]
            for j, (cx, cy) in enumerate(chips):
                first.append(_rcopy(src, part(a, k, c), ssem.at[3 * a + j], rsem.at[3 * a + j], (cx, cy, c)))
        for cp in first:
            cp.start()
        passed = []
        for a in range(nb + 1):
            for j, (cx, cy) in enumerate(chips):
                got = part(a, 2 * cx + cy, c)
                _rcopy(got, got, ssem.at[3 * a + j], rsem.at[3 * a + j], (cx, cy, c)).wait_recv()
                if a < nb:
                    fw = _rcopy(got, got, fssem.at[3 * a + j], frsem.at[3 * a + j], sib)
                    fw.start()
                    passed.append(fw)
        for a in range(nb):
            for j, (cx, cy) in enumerate(chips):
                got = part(a, 2 * cx + cy, 1 - c)
                _rcopy(got, got, fssem.at[3 * a + j], frsem.at[3 * a + j], sib).wait_recv()
        for cp in first + passed:
            cp.wait_send()
        for cp in local:
            cp.wait()

    arrs = list(bigs) + [small]
    n = 3 * (nb + 1)
    return pl.pallas_call(
        body, name="gather_shards",
        in_specs=[_ANY] * (nb + 1), out_specs=[_ANY] * (nb + 1),
        out_shape=[jax.ShapeDtypeStruct((4,) + a.shape, a.dtype) for a in arrs],
        scratch_shapes=[pltpu.SemaphoreType.DMA((n,)), pltpu.SemaphoreType.DMA((n,)),
                        pltpu.SemaphoreType.DMA((n,)), pltpu.SemaphoreType.DMA((n,)),
                        pltpu.SemaphoreType.DMA((nb + 1,))],
    )(*arrs)


def _swap_halves(grads):
    na = len(grads)
    halves = [g.shape[1] // 2 for g in grads]

    def body(*refs):
        ins, outs = refs[:na], refs[na:2 * na]
        ssem, rsem = refs[2 * na:]
        x, y, c, _ = _place()
        sib = (x, y, 1 - c)
        cps = [_rcopy(ins[a].at[:, pl.ds((1 - c) * halves[a], halves[a]), :], outs[a], ssem.at[a], rsem.at[a], sib)
               for a in range(na)]
        for cp in cps:
            cp.start()
        for cp in cps:
            cp.wait()

    return pl.pallas_call(
        body, name="swap_halves",
        in_specs=[_ANY] * na, out_specs=[_ANY] * na,
        out_shape=[jax.ShapeDtypeStruct((4, g.shape[1] // 2, g.shape[2]), g.dtype) for g in grads],
        scratch_shapes=[pltpu.SemaphoreType.DMA((na,)), pltpu.SemaphoreType.DMA((na,))],
    )(*grads)


_SUM_ROWS = 32


def _chip_sum(grad, recv, core, name):
    _, r, cdim = grad.shape
    rh = r // 2
    tr = _SUM_ROWS
    assert rh % tr == 0
    nblk = rh // tr

    def body(core_ref, g_ref, r_ref, o_ref):
        o_ref[...] = (g_ref[...] + r_ref[...]).astype(o_ref.dtype)

    return pl.pallas_call(
        body, name=name,
        grid_spec=pltpu.PrefetchScalarGridSpec(
            num_scalar_prefetch=1, grid=(4, nblk),
            in_specs=[pl.BlockSpec((None, tr, cdim), lambda s, i, cr: (s, cr[0] * nblk + i, 0)),
                      pl.BlockSpec((None, tr, cdim), lambda s, i, cr: (s, i, 0))],
            out_specs=pl.BlockSpec((None, tr, cdim), lambda s, i, cr: (s, i, 0))),
        out_shape=jax.ShapeDtypeStruct((4, rh, cdim), BF16),
        compiler_params=_cparams(("parallel", "parallel")),
    )(core, grad, recv)


def _scatter_sums(sums):
    na = len(sums)

    def body(*refs):
        ins, outs = refs[:na], refs[na:2 * na]
        ssem, rsem, lsem = refs[2 * na:]
        x, y, c, chips = _place()
        k = 2 * x + y
        local = [pltpu.make_async_copy(ins[a].at[k], outs[a].at[k], lsem.at[a]) for a in range(na)]
        for cp in local:
            cp.start()
        cps = []
        for a in range(na):
            for j, (cx, cy) in enumerate(chips):
                cps.append(_rcopy(ins[a].at[2 * cx + cy], outs[a].at[k], ssem.at[3 * a + j], rsem.at[3 * a + j],
                                  (cx, cy, c)))
        for cp in cps:
            cp.start()
        for a in range(na):
            for j, (cx, cy) in enumerate(chips):
                got = outs[a].at[2 * cx + cy]
                _rcopy(got, got, ssem.at[3 * a + j], rsem.at[3 * a + j], (cx, cy, c)).wait_recv()
        for cp in cps:
            cp.wait_send()
        for cp in local:
            cp.wait()

    return pl.pallas_call(
        body, name="scatter_sums",
        in_specs=[_ANY] * na, out_specs=[_ANY] * na,
        out_shape=[jax.ShapeDtypeStruct(s.shape, s.dtype) for s in sums],
        scratch_shapes=[pltpu.SemaphoreType.DMA((3 * na,)), pltpu.SemaphoreType.DMA((3 * na,)),
                        pltpu.SemaphoreType.DMA((na,))],
    )(*sums)


def _sum_chips(parts, name):
    _, rh, cdim = parts.shape
    tr = _SUM_ROWS

    def body(p_ref, o_ref):
        acc = p_ref[0].astype(F32)
        for j in range(1, 4):
            acc = acc + p_ref[j].astype(F32)
        o_ref[...] = acc

    return pl.pallas_call(
        body, name=name, grid=(rh // tr,),
        in_specs=[pl.BlockSpec((4, tr, cdim), lambda i: (0, i, 0))],
        out_specs=pl.BlockSpec((tr, cdim), lambda i: (i, 0)),
        out_shape=jax.ShapeDtypeStruct((rh, cdim), F32),
        compiler_params=_cparams(("parallel",)),
    )(parts)


def _join_halves(reds):
    na = len(reds)

    def body(*refs):
        ins, outs = refs[:na], refs[na:2 * na]
        ssem, rsem, lsem = refs[2 * na:]
        x, y, c, _ = _place()
        sib = (x, y, 1 - c)
        local, cps = [], []
        for a in range(na):
            rh = reds[a].shape[0]
            mine = outs[a].at[pl.ds(c * rh, rh), :]
            local.append(pltpu.make_async_copy(ins[a], mine, lsem.at[a]))
            cps.append(_rcopy(ins[a], mine, ssem.at[a], rsem.at[a], sib))
        for cp in local + cps:
            cp.start()
        for a in range(na):
            rh = reds[a].shape[0]
            theirs = outs[a].at[pl.ds((1 - c) * rh, rh), :]
            _rcopy(theirs, theirs, ssem.at[a], rsem.at[a], sib).wait_recv()
        for cp in cps:
            cp.wait_send()
        for cp in local:
            cp.wait()

    return pl.pallas_call(
        body, name="join_halves",
        in_specs=[_ANY] * na, out_specs=[_ANY] * na,
        out_shape=[jax.ShapeDtypeStruct((2 * r.shape[0], r.shape[1]), r.dtype) for r in reds],
        scratch_shapes=[pltpu.SemaphoreType.DMA((na,)), pltpu.SemaphoreType.DMA((na,)), pltpu.SemaphoreType.DMA((na,))],
    )(*reds)


def _allreduce_small(buf):
    n = buf.shape[0]

    def body(in_ref, out_ref, recv, ssem, rsem):
        x, y, c, _ = _place()
        peers = [(x, y, 1 - c), (1 - x, y, c), (x, 1 - y, c)]
        out_ref[...] = in_ref[...]
        for r, peer in enumerate(peers):
            cp = _rcopy(out_ref, recv.at[r], ssem.at[r], rsem.at[r], peer)
            cp.start()
            cp.wait()
            out_ref[...] = out_ref[...] + recv[r]

    vm = pl.BlockSpec(memory_space=pltpu.VMEM)
    return pl.pallas_call(
        body, name="allreduce_small",
        in_specs=[vm], out_specs=vm,
        out_shape=jax.ShapeDtypeStruct(buf.shape, F32),
        scratch_shapes=[pltpu.VMEM((3, n, 128), F32), pltpu.SemaphoreType.DMA((3,)), pltpu.SemaphoreType.DMA((3,))],
        compiler_params=pltpu.CompilerParams(vmem_limit_bytes=VMEM_LIMIT),
    )(buf)


_W_NAMES = ['meta_tokens', 'l0_mix_norm', 'l0_w_in', 'l0_ssd_conv_w', 'l0_ssd_conv_b', 'l0_ssd_dt_bias', 'l0_ssd_a_log',
            'l0_ssd_d', 'l0_ssd_norm', 'l0_ret_norm', 'l0_w_out', 'l0_ffn_norm', 'l0_ffn_w_in', 'l0_ffn_conv_w',
            'l0_ffn_conv_b', 'l0_ffn_w_out', 'l1_mix_norm', 'l1_w_in', 'l1_lru_conv_w', 'l1_lru_conv_b', 'l1_lru_wa',
            'l1_lru_ba', 'l1_lru_wx', 'l1_lru_bx', 'l1_lru_lambda', 'l1_w_out', 'l1_ffn_norm', 'l1_ffn_w_in',
            'l1_ffn_conv_w', 'l1_ffn_conv_b', 'l1_ffn_w_out', 'final_norm']
_IN_NAMES = ['x'] + _W_NAMES + ['loss_target'] + ['m_' + n for n in _W_NAMES] + ['v_' + n for n in _W_NAMES]
_BIG = ['l0_w_in', 'l0_w_out', 'l0_ffn_w_in', 'l0_ffn_w_out', 'l1_w_in', 'l1_w_out', 'l1_ffn_w_in', 'l1_ffn_w_out']
_BIG_COLS = ('l0_w_in', 'l0_ffn_w_in', 'l1_w_in', 'l1_ffn_w_in')
_SMALL_SHARDED = ['meta_tokens', 'l0_ssd_conv_w', 'l0_ffn_conv_w', 'l1_lru_conv_w', 'l1_ffn_conv_w']
_SMALL = [n for n in _W_NAMES if n not in _BIG]


def _pack(arrs):
    flat = []
    for a in arrs:
        v = a.reshape(-1).astype(F32)
        flat.append(jnp.pad(v, (0, (-v.shape[0]) % 128)))
    v = jnp.concatenate(flat)
    v = jnp.pad(v, (0, (-v.shape[0]) % 1024))
    return v.reshape(-1, 128)


def _unpack(buf, shapes):
    out, row = [], 0
    for sh in shapes:
        n = int(np.prod(sh))
        rows = -(-n // 128)
        out.append(buf[row:row + rows].reshape(-1)[:n].reshape(sh))
        row += rows
    return out


def kernel(x, meta_tokens, l0_mix_norm, l0_w_in, l0_ssd_conv_w, l0_ssd_conv_b, l0_ssd_dt_bias, l0_ssd_a_log, l0_ssd_d, l0_ssd_norm, l0_ret_norm, l0_w_out, l0_ffn_norm, l0_ffn_w_in, l0_ffn_conv_w, l0_ffn_conv_b, l0_ffn_w_out, l1_mix_norm, l1_w_in, l1_lru_conv_w, l1_lru_conv_b, l1_lru_wa, l1_lru_ba, l1_lru_wx, l1_lru_bx, l1_lru_lambda, l1_w_out, l1_ffn_norm, l1_ffn_w_in, l1_ffn_conv_w, l1_ffn_conv_b, l1_ffn_w_out, final_norm, loss_target, m_meta_tokens, m_l0_mix_norm, m_l0_w_in, m_l0_ssd_conv_w, m_l0_ssd_conv_b, m_l0_ssd_dt_bias, m_l0_ssd_a_log, m_l0_ssd_d, m_l0_ssd_norm, m_l0_ret_norm, m_l0_w_out, m_l0_ffn_norm, m_l0_ffn_w_in, m_l0_ffn_conv_w, m_l0_ffn_conv_b, m_l0_ffn_w_out, m_l1_mix_norm, m_l1_w_in, m_l1_lru_conv_w, m_l1_lru_conv_b, m_l1_lru_wa, m_l1_lru_ba, m_l1_lru_wx, m_l1_lru_bx, m_l1_lru_lambda, m_l1_w_out, m_l1_ffn_norm, m_l1_ffn_w_in, m_l1_ffn_conv_w, m_l1_ffn_conv_b, m_l1_ffn_w_out, m_final_norm, v_meta_tokens, v_l0_mix_norm, v_l0_w_in, v_l0_ssd_conv_w, v_l0_ssd_conv_b, v_l0_ssd_dt_bias, v_l0_ssd_a_log, v_l0_ssd_d, v_l0_ssd_norm, v_l0_ret_norm, v_l0_w_out, v_l0_ffn_norm, v_l0_ffn_w_in, v_l0_ffn_conv_w, v_l0_ffn_conv_b, v_l0_ffn_w_out, v_l1_mix_norm, v_l1_w_in, v_l1_lru_conv_w, v_l1_lru_conv_b, v_l1_lru_wa, v_l1_lru_ba, v_l1_lru_wx, v_l1_lru_bx, v_l1_lru_lambda, v_l1_w_out, v_l1_ffn_norm, v_l1_ffn_w_in, v_l1_ffn_conv_w, v_l1_ffn_conv_b, v_l1_ffn_w_out, v_final_norm):
    args = (x, meta_tokens, l0_mix_norm, l0_w_in, l0_ssd_conv_w, l0_ssd_conv_b, l0_ssd_dt_bias, l0_ssd_a_log, l0_ssd_d, l0_ssd_norm, l0_ret_norm, l0_w_out, l0_ffn_norm, l0_ffn_w_in, l0_ffn_conv_w, l0_ffn_conv_b, l0_ffn_w_out, l1_mix_norm, l1_w_in, l1_lru_conv_w, l1_lru_conv_b, l1_lru_wa, l1_lru_ba, l1_lru_wx, l1_lru_bx, l1_lru_lambda, l1_w_out, l1_ffn_norm, l1_ffn_w_in, l1_ffn_conv_w, l1_ffn_conv_b, l1_ffn_w_out, final_norm, loss_target, m_meta_tokens, m_l0_mix_norm, m_l0_w_in, m_l0_ssd_conv_w, m_l0_ssd_conv_b, m_l0_ssd_dt_bias, m_l0_ssd_a_log, m_l0_ssd_d, m_l0_ssd_norm, m_l0_ret_norm, m_l0_w_out, m_l0_ffn_norm, m_l0_ffn_w_in, m_l0_ffn_conv_w, m_l0_ffn_conv_b, m_l0_ffn_w_out, m_l1_mix_norm, m_l1_w_in, m_l1_lru_conv_w, m_l1_lru_conv_b, m_l1_lru_wa, m_l1_lru_ba, m_l1_lru_wx, m_l1_lru_bx, m_l1_lru_lambda, m_l1_w_out, m_l1_ffn_norm, m_l1_ffn_w_in, m_l1_ffn_conv_w, m_l1_ffn_conv_b, m_l1_ffn_w_out, m_final_norm, v_meta_tokens, v_l0_mix_norm, v_l0_w_in, v_l0_ssd_conv_w, v_l0_ssd_conv_b, v_l0_ssd_dt_bias, v_l0_ssd_a_log, v_l0_ssd_d, v_l0_ssd_norm, v_l0_ret_norm, v_l0_w_out, v_l0_ffn_norm, v_l0_ffn_w_in, v_l0_ffn_conv_w, v_l0_ffn_conv_b, v_l0_ffn_w_out, v_l1_mix_norm, v_l1_w_in, v_l1_lru_conv_w, v_l1_lru_conv_b, v_l1_lru_wa, v_l1_lru_ba, v_l1_lru_wx, v_l1_lru_bx, v_l1_lru_lambda, v_l1_w_out, v_l1_ffn_norm, v_l1_ffn_w_in, v_l1_ffn_conv_w, v_l1_ffn_conv_b, v_l1_ffn_w_out, v_final_norm)
    p = dict(zip(_IN_NAMES, args))
    B, seq, _ = x.shape
    nch = (seq + CH) // CH
    Pn = nch * CH
    R = B * Pn
    chip = 2 * lax.axis_index("x") + lax.axis_index("y")
    row2 = lambda v: v.reshape(1, -1)
    pad128 = lambda v: jnp.pad(v, (0, 128 - v.shape[0])).reshape(1, 128)

    small_shapes = [p[n].shape for n in _SMALL_SHARDED]
    gathered = _gather_shards([p[n].astype(_MXU) for n in _BIG], _pack([p[n] for n in _SMALL_SHARDED]))
    g_big, g_small = gathered[:-1], gathered[-1]
    W = {}
    for n, g in zip(_BIG, g_big):
        W[n] = jnp.concatenate([g[k] for k in range(4)], axis=1) if n in _BIG_COLS else g.reshape(-1, g.shape[2])
    per_chip = [_unpack(g_small[k], small_shapes) for k in range(4)]
    for i, n in enumerate(_SMALL_SHARDED):
        W[n] = jnp.concatenate([per_chip[k][i] for k in range(4)], axis=1)
    w0 = W['l0_w_in']
    w0_main = jnp.concatenate([w0[:, 3088:], w0[:, :3072]], axis=1)
    w0_dt = jnp.pad(w0[:, 3072:3088], ((0, 0), (0, 112)))
    cos, sin = _rope_tables(nch)

    meta = jnp.broadcast_to(W['meta_tokens'][None], (B, N_META, D))
    h0 = jnp.concatenate([jnp.zeros((B, PAD, D), F32), meta, x], axis=1).reshape(R, D)
    n0 = _rmsnorm_fwd(h0, row2(p['l0_mix_norm']), "norm_l0_mix")
    u0 = _mm(n0, w0_main, "nn", F32, "l0_in_proj")
    udt = _mm(n0, w0_dt, "nn", F32, "l0_dt_proj")
    a_log, d_skip, dt_bias = pad128(p['l0_ssd_a_log']), pad128(p['l0_ssd_d']), pad128(p['l0_ssd_dt_bias'])
    ssd_cb = row2(p['l0_ssd_conv_b'])
    act, dt, dtt = _ssd_prep(u0, udt, W['l0_ssd_conv_w'], ssd_cb, dt_bias, B, nch)
    ycat0, ypre, hin = _ssd_fwd(act, u0, dt, dtt, a_log, d_skip, row2(p['l0_ssd_norm']), B, nch)
    ycat0, opre, rin = _ret_fwd(u0, ycat0, cos, sin, row2(p['l0_ret_norm']), B, nch)
    h1 = _mm(ycat0, W['l0_w_out'], "nn", F32, "l0_out_proj", add=h0)
    n1 = _rmsnorm_fwd(h1, row2(p['l0_ffn_norm']), "norm_l0_ffn")
    uf0 = _mm(n1, W['l0_ffn_w_in'], "nn", F32, "l0_ffn_in")
    f0_cb = row2(p['l0_ffn_conv_b'])
    a0 = _ffn_act_fwd(uf0, W['l0_ffn_conv_w'], f0_cb, B, nch)
    h2 = _mm(a0, W['l0_ffn_w_out'], "nn", F32, "l0_ffn_out", add=h1)
    n2 = _rmsnorm_fwd(h2, row2(p['l1_mix_norm']), "norm_l1_mix")
    u1 = _mm(n2, W['l1_w_in'], "nn", F32, "l1_in_proj")
    lru = (W['l1_lru_conv_w'], row2(p['l1_lru_conv_b']), p['l1_lru_wa'], row2(p['l1_lru_ba']), p['l1_lru_wx'],
           row2(p['l1_lru_bx']), row2(p['l1_lru_lambda']))
    ycat1, stot = _sb_fwd(u1, B, nch)
    ycat1, hs = _lru_fwd(u1, ycat1, *lru, B, nch)
    h3 = _mm(ycat1, W['l1_w_out'], "nn", F32, "l1_out_proj", add=h2)
    n3 = _rmsnorm_fwd(h3, row2(p['l1_ffn_norm']), "norm_l1_ffn")
    uf1 = _mm(n3, W['l1_ffn_w_in'], "nn", F32, "l1_ffn_in")
    f1_cb = row2(p['l1_ffn_conv_b'])
    a1 = _ffn_act_fwd(uf1, W['l1_ffn_conv_w'], f1_cb, B, nch)
    h4 = _mm(a1, W['l1_ffn_w_out'], "nn", F32, "l1_ffn_out", add=h3)
    dh4, lossp, dgf = _head(h4, row2(p['final_norm']), p['loss_target'].reshape(B * seq, D), B, nch)
    loss = lax.psum(jnp.sum(lossp[:, 0, 0]), ("x", "y", "c"))

    G = {'final_norm': dgf[:, 0].sum(0)}

    def ffn_bwd(layer, dh_out, h_in, n_in, uf, a_act, cb):
        pre = f"l{layer}_"
        w_in, w_out, cw = W[pre + 'ffn_w_in'], W[pre + 'ffn_w_out'], W[pre + 'ffn_conv_w']
        da = _mm(dh_out, w_out, "nt", F32, pre + "ffn_out_dgrad")
        G[pre + 'ffn_w_out'] = _mm(a_act, dh_out, "tn", F32, pre + "ffn_out_wgrad")
        dcg, dcu = _ffn_act_bwd(da, uf, cw, cb, B, nch)
        dug, dwg = _conv_bwd(dcg, uf, 0, cw[:, :FFN], 3, pre + "ffn_conv_bwd_g", tc=_FFN_TC)
        duu, dwu = _conv_bwd(dcu, uf, FFN, cw[:, FFN:], 3, pre + "ffn_conv_bwd_u", tc=_FFN_TC)
        G[pre + 'ffn_conv_w'] = jnp.concatenate([dwg[:3], dwu[:3]], axis=1)
        G[pre + 'ffn_conv_b'] = jnp.concatenate([dwg[7], dwu[7]])
        dn = _mm(dug, w_in, "nt", F32, pre + "ffn_in_dgrad_g")
        dn = _mm(duu, w_in, "nt", F32, pre + "ffn_in_dgrad_u", add=dn, b_off=FFN)
        G[pre + 'ffn_w_in'] = jnp.concatenate([_mm(n_in, dug, "tn", F32, pre + "ffn_in_wgrad_g"),
                                               _mm(n_in, duu, "tn", F32, pre + "ffn_in_wgrad_u")], axis=1)
        dh_in, dg = _rmsnorm_bwd(h_in, row2(p[pre + 'ffn_norm']), dn, dh_out, nch, pre + "ffn_norm_bwd")
        G[pre + 'ffn_norm'] = dg[0]
        return dh_in

    dh3 = ffn_bwd(1, dh4, h3, n3, uf1, a1, f1_cb)
    dy1 = _mm(dh3, W['l1_w_out'], "nt", F32, "l1_out_dgrad")
    G['l1_w_out'] = _mm(ycat1, dh3, "tn", F32, "l1_out_wgrad")
    dq, dk, dv = _sb_bwd(dy1, u1, stot, B, nch)
    dgate, dxc, pgl, dwa, dwx = _lru_bwd(dy1, u1, hs, *lru, B, nch)
    dxr, dcw = _conv_bwd(dxc, u1, 4096, W['l1_lru_conv_w'], 4, "l1_lru_conv_bwd")
    pgl = pgl.sum(0)
    G['l1_lru_ba'], G['l1_lru_bx'], G['l1_lru_lambda'] = pgl[0], pgl[1], pgl[2]
    G['l1_lru_wa'], G['l1_lru_wx'] = dwa.sum(0), dwx.sum(0)
    G['l1_lru_conv_w'], G['l1_lru_conv_b'] = dcw[:4], dcw[7]
    dn, dws = None, []
    for i, piece in enumerate((dq, dk, dv, dgate, dxr)):
        dn = _mm(piece, W['l1_w_in'], "nt", F32, f"l1_in_dgrad_{i}", add=dn, b_off=1024 * i)
        dws.append(_mm(n2, piece, "tn", F32, f"l1_in_wgrad_{i}"))
    G['l1_w_in'] = jnp.concatenate(dws, axis=1)
    dh2, dg = _rmsnorm_bwd(h2, row2(p['l1_mix_norm']), dn, dh3, nch, "l1_mix_norm_bwd")
    G['l1_mix_norm'] = dg[0]

    dh1 = ffn_bwd(0, dh2, h1, n1, uf0, a0, f0_cb)
    dy0 = _mm(dh1, W['l0_w_out'], "nt", F32, "l0_out_dgrad")
    G['l0_w_out'] = _mm(ycat0, dh1, "tn", F32, "l0_out_wgrad")
    dz, dxs, dbm, dcm, ddt4, pgs = _ssd_bwd(dy0, ypre, u0, act, dt, dtt, hin, a_log, d_skip, row2(p['l0_ssd_norm']), B, nch)
    dpre, ddtr, pgd = _ssd_prep_bwd(dxs, dbm, dcm, ddt4, u0, udt, W['l0_ssd_conv_w'], ssd_cb, dt_bias, B, nch)
    dxbc, dcw0 = _conv_bwd(dpre, u0, U0_XBC, W['l0_ssd_conv_w'], 4, "l0_ssd_conv_bwd")
    dqkvg, pgr = _ret_bwd(dy0, u0, opre, rin, cos, sin, row2(p['l0_ret_norm']), B, nch)
    pgs = pgs.sum(0)
    G['l0_ssd_norm'] = pgs[:, 0, :].reshape(-1)
    G['l0_ssd_d'] = pgs[:, 1, :128].sum(0)[:SSD_HEADS]
    G['l0_ssd_a_log'] = pgs[:, 2, :128].sum(0)[:SSD_HEADS]
    G['l0_ssd_dt_bias'] = pgd.sum(0)[0, :SSD_HEADS]
    G['l0_ssd_conv_w'], G['l0_ssd_conv_b'] = dcw0[:4], dcw0[7]
    G['l0_ret_norm'] = pgr.sum(0)[0]
    dn = _mm(dqkvg, w0_main, "nt", F32, "l0_in_dgrad_qkvg")
    dn = _mm(dz, w0_main, "nt", F32, "l0_in_dgrad_z", add=dn, b_off=U0_Z)
    dn = _mm(dxbc, w0_main, "nt", F32, "l0_in_dgrad_xbc", add=dn, b_off=U0_XBC)
    dn = _mm(ddtr, w0_dt, "nt", F32, "l0_in_dgrad_dt", add=dn)
    G['l0_w_in'] = jnp.concatenate([
        _mm(n0, dz, "tn", F32, "l0_in_wgrad_z"), _mm(n0, dxbc, "tn", F32, "l0_in_wgrad_xbc"),
        _mm(n0, ddtr, "tn", F32, "l0_in_wgrad_dt")[:, :SSD_HEADS], _mm(n0, dqkvg, "tn", F32, "l0_in_wgrad_qkvg")], axis=1)
    dh0, dg = _rmsnorm_bwd(h0, row2(p['l0_mix_norm']), dn, dh1, nch, "l0_mix_norm_bwd")
    G['l0_mix_norm'] = dg[0]
    dh0 = dh0.reshape(B, Pn, D)
    grad_x = dh0[:, CH:]
    G['meta_tokens'] = dh0[:, PAD:CH].sum(0)

    core = lax.axis_index("c").reshape(1).astype(jnp.int32)
    stacked = []
    for n in _BIG:
        g = G[n]
        if n in _BIG_COLS:
            stacked.append(g.reshape(g.shape[0], 4, g.shape[1] // 4).transpose(1, 0, 2))
        else:
            stacked.append(g.reshape(4, g.shape[0] // 4, g.shape[1]))
    theirs = _swap_halves(stacked)
    sums = [_chip_sum(g, t, core, "chip_sum_" + n) for n, g, t in zip(_BIG, stacked, theirs)]
    parts = _scatter_sums(sums)
    reds = [_sum_chips(q, "sum_chips_" + n) for n, q in zip(_BIG, parts)]
    grads = dict(zip(_BIG, _join_halves(reds)))
    small_full = _unpack(_allreduce_small(_pack([G[n] for n in _SMALL])), [G[n].shape for n in _SMALL])
    for n, g in zip(_SMALL, small_full):
        if n in _SMALL_SHARDED:
            cs = g.shape[1] // 4
            g = lax.dynamic_slice_in_dim(g, chip * cs, cs, axis=1)
        grads[n] = g.reshape(p[n].shape)

    delta, new_m, new_v = {}, {}, {}
    for n in _BIG:
        delta[n], new_m[n], new_v[n] = _adamw(p[n], grads[n], p['m_' + n], p['v_' + n], "adamw_" + n)
    shapes = [p[n].shape for n in _SMALL]
    outs = _adamw(_pack([p[n] for n in _SMALL]), _pack([grads[n] for n in _SMALL]), _pack([p['m_' + n] for n in _SMALL]),
                  _pack([p['v_' + n] for n in _SMALL]), "adamw_small")
    for dst, buf in zip((delta, new_m, new_v), outs):
        for n, a in zip(_SMALL, _unpack(buf, shapes)):
            dst[n] = a
    return (loss, grad_x, *[grads[n] for n in _W_NAMES], *[delta[n] for n in _W_NAMES],
            *[new_m[n] for n in _W_NAMES], *[new_v[n] for n in _W_NAMES])
```

```python
import math

import numpy as np
import jax
import jax.numpy as jnp
from jax import lax
from jax.experimental import pallas as pl
from jax.experimental.pallas import tpu as pltpu

F32 = jnp.float32
BF16 = jnp.bfloat16
_MXU = jnp.bfloat16

D = 1024
CH = 128
N_META = 16
PAD = CH - N_META
EPS = 1e-6

SSD_HEADS = 16
SSD_HD = 64
SSD_GROUPS = 4
RET_HEADS = 4
RET_DK = 256
SB_HEADS = 16
SB_HD = 64
LRU_BLOCKS = 8
LRU_C = 8.0
FFN = 2816
U0_Z = 4096
U0_XBC = 5120

VMEM_LIMIT = 56 * 1024 * 1024


def _cparams(sem):
    return pltpu.CompilerParams(dimension_semantics=sem, vmem_limit_bytes=VMEM_LIMIT)


def _dot(a, b, dims=((1,), (0,))):
    return lax.dot_general(a.astype(_MXU), b.astype(_MXU), (dims, ((), ())), preferred_element_type=F32)


def _dot_nt(a, b):
    return _dot(a, b, ((1,), (1,)))


def _dot_tn(a, b):
    return _dot(a.T, b)


def _dot_exact(a, b):
    return lax.dot_general(a, b, (((1,), (0,)), ((), ())), preferred_element_type=F32,
                           precision=lax.Precision.HIGHEST)


def _dot_split(x, m01):
    hi = x.astype(BF16)
    lo = (x - hi.astype(F32)).astype(BF16)
    m = m01.astype(BF16)
    return jnp.dot(hi, m, preferred_element_type=F32) + jnp.dot(lo, m, preferred_element_type=F32)


def _sigmoid(x):
    return jax.nn.sigmoid(x)


def _softplus(x):
    return jnp.maximum(x, 0.0) + jnp.log1p(jnp.exp(-jnp.abs(x)))


def _silu(x):
    return x * _sigmoid(x)


def _dsilu(x):
    s = _sigmoid(x)
    return s * (1.0 + x * (1.0 - s))


_GELU_C = math.sqrt(2.0 / math.pi)


def _gelu(x):
    return 0.5 * x * (1.0 + jnp.tanh(_GELU_C * (x + 0.044715 * x * x * x)))


def _dgelu(x):
    t = jnp.tanh(_GELU_C * (x + 0.044715 * x * x * x))
    return 0.5 * (1.0 + t) + 0.5 * x * (1.0 - t * t) * _GELU_C * (1.0 + 3.0 * 0.044715 * x * x)


def _row_ids(n, cols=1):
    return lax.broadcasted_iota(jnp.int32, (n, cols), 0)


def _lane_ids(rows, n):
    return lax.broadcasted_iota(jnp.int32, (rows, n), 1)


def _real_rows(chunk):
    return chunk * CH + _row_ids(CH) >= PAD


def _shift_down(prev8, cur, s):
    cat = jnp.concatenate([prev8, cur], axis=0)
    return pltpu.roll(cat, s, axis=0)[8:]


def _shift_up(cur, next8, s):
    n = cur.shape[0]
    cat = jnp.concatenate([cur, next8], axis=0)
    return pltpu.roll(cat, n + 8 - s, axis=0)[:n]


def _conv_pre(prev8, cur, w_ref, b_ref, K):
    acc = cur * w_ref[K - 1:K, :] + b_ref[...]
    for s in range(1, K):
        acc = acc + _shift_down(prev8, cur, s) * w_ref[K - 1 - s:K - s, :]
    return acc


def _prev8_map(nch, col):
    return lambda b, c: (jnp.maximum((b * nch + c) * (CH // 8) - 1, 0), col)


def _matmul(a, b, mode, out_dtype, tm, tn, tk, name, add=None, b_off=0):
    if mode == "nn":
        (M, K), (_, N) = a.shape, b.shape
    elif mode == "nt":
        (M, K), N = a.shape, b.shape[0]
    else:
        (K, M), (_, N) = a.shape, b.shape
    tm, tn, tk = min(tm, M), min(tn, N), min(tk, K)
    assert M % tm == 0 and N % tn == 0 and K % tk == 0 and b_off % tk == 0, (name, M, N, K, tm, tn, tk)
    koff = b_off // tk
    nk = K // tk
    dims = {"nn": ((1,), (0,)), "nt": ((1,), (1,)), "tn": ((0,), (0,))}[mode]
    if mode == "tn":
        a_spec = pl.BlockSpec((tk, tm), lambda i, j, k: (k, i))
    else:
        a_spec = pl.BlockSpec((tm, tk), lambda i, j, k: (i, k))
    if mode == "nt":
        b_spec = pl.BlockSpec((tn, tk), lambda i, j, k: (j, k + koff))
    else:
        b_spec = pl.BlockSpec((tk, tn), lambda i, j, k: (k, j))
    o_spec = pl.BlockSpec((tm, tn), lambda i, j, k: (i, j))
    has_add = add is not None

    def body(a_ref, b_ref, *rest):
        if has_add:
            add_ref, o_ref, acc = rest
        else:
            o_ref, acc = rest
        k = pl.program_id(2)

        @pl.when(k == 0)
        def _():
            acc[...] = jnp.zeros_like(acc)

        acc[...] += _dot(a_ref[...], b_ref[...], dims)

        @pl.when(k == nk - 1)
        def _():
            r = acc[...]
            if has_add:
                r = r + add_ref[...].astype(F32)
            o_ref[...] = r.astype(out_dtype)

    in_specs = [a_spec, b_spec] + ([o_spec] if has_add else [])
    args = (a, b) + ((add,) if has_add else ())
    return pl.pallas_call(
        body, name=name, grid=(M // tm, N // tn, nk),
        in_specs=in_specs, out_specs=o_spec,
        out_shape=jax.ShapeDtypeStruct((M, N), out_dtype),
        scratch_shapes=[pltpu.VMEM((tm, tn), F32)],
        compiler_params=_cparams(("parallel", "parallel", "arbitrary")),
    )(*args)


def _tile(n, prefs):
    for t in prefs:
        if n % t == 0:
            return t
    return n


def _mm(a, b, mode, out_dtype, name, add=None, b_off=0):
    if mode == "tn":
        K, M = a.shape
        N = b.shape[1]
        tm, tn, tk = _tile(M, (1024, 1408, 512, 256, 128)), _tile(N, (512, 256, 128)), _tile(K, (2176, 384, 256, 128))
    else:
        M, K = a.shape
        N = b.shape[1] if mode == "nn" else b.shape[0]
        tm, tn, tk = _tile(M, (1088, 768, 384, 256, 128)), _tile(N, (512, 256, 128)), _tile(K, (1024, 1408, 512, 256, 128))
    return _matmul(a, b, mode, out_dtype, tm, tn, tk, name, add=add, b_off=b_off)


def _rmsnorm_fwd(h, g, name):
    R = h.shape[0]
    tr = 2 * CH

    def body(h_ref, g_ref, o_ref):
        x = h_ref[...]
        r = lax.rsqrt(jnp.mean(x * x, axis=-1, keepdims=True) + EPS)
        o_ref[...] = (x * r * g_ref[...]).astype(o_ref.dtype)

    return pl.pallas_call(
        body, name=name, grid=(R // tr,),
        in_specs=[pl.BlockSpec((tr, D), lambda i: (i, 0)), pl.BlockSpec((1, D), lambda i: (0, 0))],
        out_specs=pl.BlockSpec((tr, D), lambda i: (i, 0)),
        out_shape=jax.ShapeDtypeStruct((R, D), _MXU),
        compiler_params=_cparams(("parallel",)),
    )(h, g)


def _rmsnorm_bwd(h, g, dn, dres, nch, name):
    R = h.shape[0]

    def body(h_ref, g_ref, dn_ref, dres_ref, dh_ref, dg_ref):
        i = pl.program_id(0)
        x = h_ref[...]
        r = lax.rsqrt(jnp.mean(x * x, axis=-1, keepdims=True) + EPS)
        xhat = x * r
        dn_v = dn_ref[...]
        dx = dn_v * g_ref[...]
        dh = r * (dx - xhat * jnp.mean(dx * xhat, axis=-1, keepdims=True))
        dh_ref[...] = jnp.where(_real_rows(i % nch), dres_ref[...] + dh, 0.0)

        @pl.when(i == 0)
        def _():
            dg_ref[...] = jnp.zeros_like(dg_ref)

        dg_ref[...] += jnp.sum(dn_v * xhat, axis=0, keepdims=True)

    row = pl.BlockSpec((CH, D), lambda i: (i, 0))
    vec = pl.BlockSpec((1, D), lambda i: (0, 0))
    return pl.pallas_call(
        body, name=name, grid=(R // CH,),
        in_specs=[row, vec, row, row], out_specs=[row, vec],
        out_shape=[jax.ShapeDtypeStruct((R, D), F32), jax.ShapeDtypeStruct((1, D), F32)],
        compiler_params=_cparams(("arbitrary",)),
    )(h, g, dn, dres)


def _ssd_prep(u0, udt, conv_w, conv_b, dt_bias, B, nch):
    R = u0.shape[0]

    def body(xs_ref, xsp_ref, bc_ref, bcp_ref, udt_ref, w0_ref, w1_ref, b0_ref, b1_ref, dtb_ref,
             act_ref, dt_ref, dtt_ref):
        keep = _real_rows(pl.program_id(1))
        a0 = _silu(_conv_pre(xsp_ref[...], xs_ref[...], w0_ref, b0_ref, 4))
        a1 = _silu(_conv_pre(bcp_ref[...], bc_ref[...], w1_ref, b1_ref, 4))
        act_ref[:, :1024] = jnp.where(keep, a0, 0.0)
        act_ref[:, 1024:] = jnp.where(keep, a1, 0.0)
        ok = jnp.logical_and(keep, _lane_ids(1, 128) < SSD_HEADS)
        dt = jnp.where(ok, _softplus(udt_ref[...] + dtb_ref[...]), 0.0)
        dt_ref[...] = dt
        dtt_ref[...] = dt.T

    row = lambda col: pl.BlockSpec((CH, 1024), lambda b, c: (b * nch + c, col))
    prev = lambda col: pl.BlockSpec((8, 1024), _prev8_map(nch, col))
    return pl.pallas_call(
        body, name="ssd_prep", grid=(B, nch),
        in_specs=[row(5), prev(5), row(6), prev(6),
                  pl.BlockSpec((CH, 128), lambda b, c: (b * nch + c, 0)),
                  pl.BlockSpec((4, 1024), lambda b, c: (0, 0)), pl.BlockSpec((4, 1024), lambda b, c: (0, 1)),
                  pl.BlockSpec((1, 1024), lambda b, c: (0, 0)), pl.BlockSpec((1, 1024), lambda b, c: (0, 1)),
                  pl.BlockSpec((1, 128), lambda b, c: (0, 0))],
        out_specs=[pl.BlockSpec((CH, 2048), lambda b, c: (b * nch + c, 0)),
                   pl.BlockSpec((CH, 128), lambda b, c: (b * nch + c, 0)),
                   pl.BlockSpec((128, CH), lambda b, c: (0, b * nch + c))],
        out_shape=[jax.ShapeDtypeStruct((R, 2048), F32), jax.ShapeDtypeStruct((R, 128), F32),
                   jax.ShapeDtypeStruct((128, R), F32)],
        compiler_params=_cparams(("parallel", "parallel")),
    )(u0, u0, u0, u0, udt, conv_w, conv_w, conv_b, conv_b, dt_bias)


def _ssd_head_terms(h, a_vec, dt_v, dtt_v, dsk_v):
    lane = _lane_ids(1, 128)
    sub = _row_ids(128)
    r = _row_ids(CH, CH)
    cidx = _lane_ids(CH, CH)
    a_h = jnp.sum(jnp.where(lane == h, a_vec, 0.0), axis=1, keepdims=True)
    dt_col = jnp.sum(jnp.where(lane == h, dt_v, 0.0), axis=1, keepdims=True)
    dt_row = jnp.sum(jnp.where(sub == h, dtt_v, 0.0), axis=0, keepdims=True)
    cs_col = jnp.sum(jnp.where(r >= cidx, dt_row * a_h, 0.0), axis=1, keepdims=True)
    cs_row = jnp.sum(jnp.where(r <= cidx, dt_col * a_h, 0.0), axis=0, keepdims=True)
    tot = jnp.sum(dt_col * a_h, axis=0, keepdims=True)
    dsk = jnp.sum(jnp.where(lane == h, dsk_v, 0.0), axis=1, keepdims=True)
    return a_h, dt_col, cs_col, cs_row, tot, dsk


def _ssd_fwd(act, u0, dt, dtt, a_log, d_skip, norm_g, B, nch):
    R = act.shape[0]

    def body(xs_ref, bm_ref, cm_ref, z_ref, dt_ref, dtt_ref, alog_ref, dsk_ref, ng_ref,
             out_ref, ypre_ref, hin_ref, H):
        g = pl.program_id(1)
        c = pl.program_id(2)

        @pl.when(c == 0)
        def _():
            H[...] = jnp.zeros_like(H)

        hin_ref[...] = H[...]
        a_vec = -jnp.exp(alog_ref[...])
        dt_v = dt_ref[...]
        dtt_v = dtt_ref[...]
        hm = _lane_ids(1, 128) < SSD_HD
        r = _row_ids(CH, CH)
        cidx = _lane_ids(CH, CH)
        Bm = bm_ref[...]
        Cm = cm_ref[...]
        CB = _dot_nt(Cm, Bm)
        ys = []
        for pair in range(2):
            cols = slice(128 * pair, 128 * pair + 128)
            xraw = xs_ref[:, cols]
            t = [_ssd_head_terms(4 * g + 2 * pair + j, a_vec, dt_v, dtt_v, dsk_ref[...]) for j in range(2)]
            sel = lambda f: jnp.where(hm, f(t[0]), f(t[1]))
            dtp = sel(lambda q: q[1])
            Ep = sel(lambda q: jnp.exp(q[2]))
            Wp = sel(lambda q: jnp.exp(q[4] - q[2]))
            etot = sel(lambda q: jnp.exp(q[4]))
            dsk = sel(lambda q: q[5])
            X = xraw * dtp
            ydiag = jnp.zeros((CH, 128), F32)
            for j in range(2):
                Lm = jnp.where(r >= cidx, jnp.exp(t[j][2] - t[j][3]), 0.0)
                Xh = jnp.where(hm if j == 0 else jnp.logical_not(hm), X, 0.0)
                ydiag = ydiag + _dot(CB * Lm, Xh)
            Hp = H[:, cols]
            yoff = Ep * _dot(Cm, Hp)
            S = _dot(Bm.T, X * Wp)
            H[:, cols] = etot * Hp + S
            ys.append(ydiag + yoff + xraw * dsk)
        y = jnp.concatenate(ys, axis=1)
        ypre_ref[...] = y
        yg = y * _silu(z_ref[...])
        rr = lax.rsqrt(jnp.mean(yg * yg, axis=-1, keepdims=True) + EPS)
        out_ref[...] = jnp.where(_real_rows(c), yg * rr * ng_ref[...], 0.0).astype(out_ref.dtype)

    rowb = lambda w, colf: pl.BlockSpec((CH, w), lambda b, g, c: (b * nch + c, colf(g)))
    vec = pl.BlockSpec((1, 128), lambda b, g, c: (0, 0))
    return pl.pallas_call(
        body, name="ssd_fwd", grid=(B, SSD_GROUPS, nch),
        in_specs=[rowb(256, lambda g: g), rowb(128, lambda g: 8 + g), rowb(128, lambda g: 12 + g),
                  rowb(256, lambda g: 16 + g), rowb(128, lambda g: 0),
                  pl.BlockSpec((128, CH), lambda b, g, c: (0, b * nch + c)),
                  vec, vec, pl.BlockSpec((1, 256), lambda b, g, c: (0, g))],
        out_specs=[rowb(256, lambda g: g), rowb(256, lambda g: g),
                   pl.BlockSpec((None, None, None, 128, 256), lambda b, g, c: (b, g, c, 0, 0))],
        out_shape=[jax.ShapeDtypeStruct((R, 2048), _MXU), jax.ShapeDtypeStruct((R, 1024), F32),
                   jax.ShapeDtypeStruct((B, SSD_GROUPS, nch, 128, 256), F32)],
        scratch_shapes=[pltpu.VMEM((128, 256), F32)],
        compiler_params=_cparams(("parallel", "parallel", "arbitrary")),
    )(act, act, act, u0, dt, dtt, a_log, d_skip, norm_g)


def _ssd_bwd(dycat, ypre, u0, act, dt, dtt, hin, a_log, d_skip, norm_g, B, nch):
    R = act.shape[0]

    def body(dy_ref, ypre_ref, z_ref, xs_ref, bm_ref, cm_ref, dt_ref, dtt_ref, hin_ref, alog_ref, dsk_ref, ng_ref,
             dz_ref, dxs_ref, db_ref, dc_ref, ddt_ref, pg_ref, dH):
        g = pl.program_id(1)
        c = nch - 1 - pl.program_id(2)

        @pl.when(pl.program_id(2) == 0)
        def _():
            dH[...] = jnp.zeros_like(dH)
            pg_ref[...] = jnp.zeros_like(pg_ref)

        z = z_ref[...]
        y = ypre_ref[...]
        ng = ng_ref[...]
        dout = jnp.where(_real_rows(c), dy_ref[...], 0.0)
        sz = _sigmoid(z)
        yg = y * z * sz
        rr = lax.rsqrt(jnp.mean(yg * yg, axis=-1, keepdims=True) + EPS)
        nrm = yg * rr
        pg_ref[0:1, :] += jnp.sum(dout * nrm, axis=0, keepdims=True)
        dn = dout * ng
        dyg = rr * (dn - nrm * jnp.mean(dn * nrm, axis=-1, keepdims=True))
        dy = dyg * z * sz
        dz_ref[...] = dyg * y * (sz * (1.0 + z * (1.0 - sz)))

        a_vec = -jnp.exp(alog_ref[...])
        dt_v = dt_ref[...]
        dtt_v = dtt_ref[...]
        lane = _lane_ids(1, 128)
        hm = lane < SSD_HD
        r = _row_ids(CH, CH)
        cidx = _lane_ids(CH, CH)
        last = _row_ids(CH) == CH - 1
        Bm = bm_ref[...]
        Cm = cm_ref[...]
        CB = _dot_nt(Cm, Bm)
        CBT = _dot_nt(Bm, Cm)
        dB = jnp.zeros((CH, 128), F32)
        dC = jnp.zeros((CH, 128), F32)
        dcs_all = jnp.zeros((CH, 128), F32)
        dtx_all = jnp.zeros((CH, 128), F32)
        dd_row = jnp.zeros((1, 128), F32)
        dxs = []
        for pair in range(2):
            cols = slice(128 * pair, 128 * pair + 128)
            xraw = xs_ref[:, cols]
            dyp = dy[:, cols]
            heads = [4 * g + 2 * pair + j for j in range(2)]
            t = [_ssd_head_terms(heads[j], a_vec, dt_v, dtt_v, dsk_ref[...]) for j in range(2)]
            sel = lambda f: jnp.where(hm, f(t[0]), f(t[1]))
            hsum = lambda v, j: jnp.sum(jnp.where(hm if j == 0 else jnp.logical_not(hm), v, 0.0), axis=1, keepdims=True)
            dtp = sel(lambda q: q[1])
            Ep = sel(lambda q: jnp.exp(q[2]))
            Wp = sel(lambda q: jnp.exp(q[4] - q[2]))
            etot = sel(lambda q: jnp.exp(q[4]))
            dsk = sel(lambda q: q[5])
            X = xraw * dtp
            Hp = hin_ref[:, cols]
            dHn = dH[:, cols]
            dskip = jnp.sum(dyp * xraw, axis=0, keepdims=True)
            yoff = Ep * _dot(Cm, Hp)
            dE = dyp * yoff
            dC = dC + _dot_nt(dyp * Ep, Hp)
            dH[:, cols] = etot * dHn + _dot(Cm.T, dyp * Ep)
            BdS = _dot(Bm, dHn)
            dX = Wp * BdS
            ew = X * BdS * Wp
            dB = dB + _dot_nt(X * Wp, dHn)
            hh = jnp.sum(dHn * Hp, axis=0, keepdims=True) * etot
            for j in range(2):
                hmask = hm if j == 0 else jnp.logical_not(hm)
                cs_col, cs_row = t[j][2], t[j][3]
                Lm = jnp.where(r >= cidx, jnp.exp(cs_col - cs_row), 0.0)
                LmT = jnp.where(cidx >= r, jnp.exp(cs_row - cs_col), 0.0)
                dyh = jnp.where(hmask, dyp, 0.0)
                Xh = jnp.where(hmask, X, 0.0)
                dM = _dot_nt(dyh, Xh)
                dMT = _dot_nt(Xh, dyh)
                M = CB * Lm
                MT = CBT * LmT
                dX = dX + _dot(MT, dyh)
                dC = dC + _dot(dM * Lm, Bm)
                dB = dB + _dot(dMT * LmT, Cm)
                g_rows = jnp.sum(dM * M, axis=1, keepdims=True)
                g_cols = jnp.sum(dMT * MT, axis=1, keepdims=True)
                dtot = (jnp.sum(hsum(ew, j), axis=0, keepdims=True)
                        + jnp.sum(jnp.where(hmask, hh, 0.0), axis=1, keepdims=True))
                dcs = g_rows - g_cols + hsum(dE, j) - hsum(ew, j) + jnp.where(last, dtot, 0.0)
                dcs_all = dcs_all + jnp.where(lane == heads[j], dcs, 0.0)
                dtx_all = dtx_all + jnp.where(lane == heads[j], hsum(dX * xraw, j), 0.0)
                dd_row = dd_row + jnp.where(lane == heads[j],
                                            jnp.sum(jnp.where(hmask, dskip, 0.0), axis=1, keepdims=True), 0.0)
            dxs.append(dX * dtp + dyp * dsk)
        dxs_ref[...] = jnp.concatenate(dxs, axis=1)
        db_ref[...] = dB
        dc_ref[...] = dC
        dadt = _dot_exact(jnp.where(cidx >= r, 1.0, 0.0), dcs_all)
        ddt_ref[...] = dadt * a_vec + dtx_all
        pg_ref[1:2, 0:128] += dd_row
        pg_ref[2:3, 0:128] += jnp.sum(dadt * dt_v, axis=0, keepdims=True) * a_vec

    rowb = lambda w, colf: pl.BlockSpec((CH, w), lambda b, g, c: (b * nch + nch - 1 - c, colf(g)))
    vec = pl.BlockSpec((1, 128), lambda b, g, c: (0, 0))
    return pl.pallas_call(
        body, name="ssd_bwd", grid=(B, SSD_GROUPS, nch),
        in_specs=[rowb(256, lambda g: g), rowb(256, lambda g: g), rowb(256, lambda g: 16 + g), rowb(256, lambda g: g),
                  rowb(128, lambda g: 8 + g), rowb(128, lambda g: 12 + g), rowb(128, lambda g: 0),
                  pl.BlockSpec((128, CH), lambda b, g, c: (0, b * nch + nch - 1 - c)),
                  pl.BlockSpec((None, None, None, 128, 256), lambda b, g, c: (b, g, nch - 1 - c, 0, 0)),
                  vec, vec, pl.BlockSpec((1, 256), lambda b, g, c: (0, g))],
        out_specs=[rowb(256, lambda g: g), rowb(256, lambda g: g), rowb(128, lambda g: g), rowb(128, lambda g: g),
                   rowb(128, lambda g: g),
                   pl.BlockSpec((None, None, 8, 256), lambda b, g, c: (b, g, 0, 0))],
        out_shape=[jax.ShapeDtypeStruct((R, 1024), F32), jax.ShapeDtypeStruct((R, 1024), F32),
                   jax.ShapeDtypeStruct((R, 512), F32), jax.ShapeDtypeStruct((R, 512), F32),
                   jax.ShapeDtypeStruct((R, 512), F32), jax.ShapeDtypeStruct((B, SSD_GROUPS, 8, 256), F32)],
        scratch_shapes=[pltpu.VMEM((128, 256), F32)],
        compiler_params=_cparams(("parallel", "parallel", "arbitrary")),
    )(dycat, ypre, u0, act, act, act, dt, dtt, hin, a_log, d_skip, norm_g)


def _ssd_prep_bwd(dxs, dB, dC, ddt4, u0, udt, conv_w, conv_b, dt_bias, B, nch):
    R = u0.shape[0]

    def body(dxs_ref, db_ref, dc_ref, ddt_ref, xs_ref, xsp_ref, bc_ref, bcp_ref, udt_ref, w0_ref, w1_ref, b0_ref, b1_ref,
             dtb_ref, dpre_ref, ddtr_ref, pgd_ref):
        c = pl.program_id(1)

        @pl.when(c == 0)
        def _():
            pgd_ref[...] = jnp.zeros_like(pgd_ref)

        keep = _real_rows(c)
        p0 = _conv_pre(xsp_ref[...], xs_ref[...], w0_ref, b0_ref, 4)
        p1 = _conv_pre(bcp_ref[...], bc_ref[...], w1_ref, b1_ref, 4)
        dpre_ref[:, :1024] = jnp.where(keep, dxs_ref[...] * _dsilu(p0), 0.0)
        dpre_ref[:, 1024:] = jnp.where(keep, jnp.concatenate([db_ref[...], dc_ref[...]], axis=1) * _dsilu(p1), 0.0)
        ddt = ddt_ref[:, 0:128] + ddt_ref[:, 128:256] + ddt_ref[:, 256:384] + ddt_ref[:, 384:512]
        ok = jnp.logical_and(keep, _lane_ids(1, 128) < SSD_HEADS)
        dr = jnp.where(ok, ddt * _sigmoid(udt_ref[...] + dtb_ref[...]), 0.0)
        ddtr_ref[...] = dr
        pgd_ref[0:1, :] += jnp.sum(dr, axis=0, keepdims=True)

    rw = lambda w: pl.BlockSpec((CH, w), lambda b, c: (b * nch + c, 0))
    row = lambda col: pl.BlockSpec((CH, 1024), lambda b, c: (b * nch + c, col))
    prev = lambda col: pl.BlockSpec((8, 1024), _prev8_map(nch, col))
    return pl.pallas_call(
        body, name="ssd_prep_bwd", grid=(B, nch),
        in_specs=[rw(1024), rw(512), rw(512), rw(512), row(5), prev(5), row(6), prev(6), rw(128),
                  pl.BlockSpec((4, 1024), lambda b, c: (0, 0)), pl.BlockSpec((4, 1024), lambda b, c: (0, 1)),
                  pl.BlockSpec((1, 1024), lambda b, c: (0, 0)), pl.BlockSpec((1, 1024), lambda b, c: (0, 1)),
                  pl.BlockSpec((1, 128), lambda b, c: (0, 0))],
        out_specs=[rw(2048), rw(128), pl.BlockSpec((None, 8, 128), lambda b, c: (b, 0, 0))],
        out_shape=[jax.ShapeDtypeStruct((R, 2048), F32), jax.ShapeDtypeStruct((R, 128), F32),
                   jax.ShapeDtypeStruct((B, 8, 128), F32)],
        compiler_params=_cparams(("parallel", "arbitrary")),
    )(dxs, dB, dC, ddt4, u0, u0, u0, u0, udt, conv_w, conv_w, conv_b, conv_b, dt_bias)


def _conv_bwd(dpre, xin, xin_col, w, K, name, tc=1024):
    R, C = dpre.shape
    assert C % tc == 0 and xin_col % tc == 0
    nr = R // CH
    xoff = xin_col // tc

    def body(dp_ref, dpn_ref, x_ref, xp_ref, w_ref, din_ref, dw_ref):
        i = pl.program_id(1)

        @pl.when(i == 0)
        def _():
            dw_ref[...] = jnp.zeros_like(dw_ref)

        dp = dp_ref[...]
        nxt = dpn_ref[...] * (i < nr - 1).astype(F32)
        x = x_ref[...]
        xp = xp_ref[...]
        din = dp * w_ref[K - 1:K, :]
        dw_ref[K - 1:K, :] += jnp.sum(dp * x, axis=0, keepdims=True)
        dw_ref[7:8, :] += jnp.sum(dp, axis=0, keepdims=True)
        for s in range(1, K):
            din = din + _shift_up(dp, nxt, s) * w_ref[K - 1 - s:K - s, :]
            dw_ref[K - 1 - s:K - s, :] += jnp.sum(dp * _shift_down(xp, x, s), axis=0, keepdims=True)
        din_ref[...] = din

    return pl.pallas_call(
        body, name=name, grid=(C // tc, nr),
        in_specs=[pl.BlockSpec((CH, tc), lambda j, i: (i, j)),
                  pl.BlockSpec((8, tc), lambda j, i: (jnp.minimum((i + 1) * (CH // 8), nr * (CH // 8) - 1), j)),
                  pl.BlockSpec((CH, tc), lambda j, i: (i, xoff + j)),
                  pl.BlockSpec((8, tc), lambda j, i: (jnp.maximum(i * (CH // 8) - 1, 0), xoff + j)),
                  pl.BlockSpec((K, tc), lambda j, i: (0, j))],
        out_specs=[pl.BlockSpec((CH, tc), lambda j, i: (i, j)),
                   pl.BlockSpec((8, tc), lambda j, i: (0, j))],
        out_shape=[jax.ShapeDtypeStruct((R, C), F32), jax.ShapeDtypeStruct((8, C), F32)],
        compiler_params=_cparams(("parallel", "arbitrary")),
    )(dpre, dpre, xin, xin, w)


_RET_LG = [float(v) for v in np.log1p(-np.exp2(-5.0 - np.arange(RET_HEADS, dtype=np.float32))).astype(np.float32)]
_RET_SCALE = RET_DK ** -0.5


def _rope_tables(nch):
    half = RET_DK // 2
    inv_freq = 1.0 / (10000.0 ** (jnp.arange(half, dtype=F32) / (half - 1)))
    pos = jnp.arange(nch * CH, dtype=F32) - PAD
    ang = pos[:, None] * inv_freq[None, :]
    return jnp.cos(ang), jnp.sin(ang)


def _rot(x, cos, sin):
    x1, x2 = x[:, :128], x[:, 128:]
    return jnp.concatenate([x1 * cos - x2 * sin, x1 * sin + x2 * cos], axis=1)


def _unrot(d, cos, sin):
    d1, d2 = d[:, :128], d[:, 128:]
    return jnp.concatenate([d1 * cos + d2 * sin, d2 * cos - d1 * sin], axis=1)


def _ret_decays(lg):
    r = _row_ids(CH, CH)
    cidx = _lane_ids(CH, CH)
    diff = (r - cidx).astype(F32)
    decay = jnp.where(r >= cidx, jnp.exp(lg * jnp.maximum(diff, 0.0)), 0.0)
    decay_t = jnp.where(cidx >= r, jnp.exp(lg * jnp.maximum(-diff, 0.0)), 0.0)
    idx = _row_ids(CH).astype(F32)
    zeta = jnp.exp(lg * (CH - 1.0 - idx))
    xi = jnp.exp(lg * (idx + 1.0))
    return decay, decay_t, zeta, xi


def _ret_fwd(u0, ycat, cos, sin, norm_g, B, nch):
    R = u0.shape[0]

    def body(u_ref, cos_ref, sin_ref, ng_ref, ycat_in, out_ref, opre_ref, rin_ref, Rst):
        c = pl.program_id(1)

        @pl.when(c == 0)
        def _():
            Rst[...] = jnp.zeros_like(Rst)

        cos_v, sin_v = cos_ref[...], sin_ref[...]
        for h in range(RET_HEADS):
            lg = _RET_LG[h]
            cols = slice(256 * h, 256 * h + 256)
            decay, _, zeta, xi = _ret_decays(lg)
            qr = _rot(u_ref[:, cols], cos_v, sin_v)
            kr = _rot(u_ref[:, 1024 + 256 * h:1024 + 256 * h + 256], cos_v, sin_v) * _RET_SCALE
            v = u_ref[:, 2048 + 256 * h:2048 + 256 * h + 256]
            gate = u_ref[:, 3072 + 256 * h:3072 + 256 * h + 256]
            Rh = Rst[h]
            rin_ref[h] = Rh
            inner = _dot(_dot_nt(qr, kr) * decay, v)
            cross = _dot(qr, Rh) * xi
            Rst[h] = math.exp(CH * lg) * Rh + _dot((kr * zeta).T, v)
            o = inner + cross
            opre_ref[:, cols] = o
            oc = o - jnp.mean(o, axis=-1, keepdims=True)
            rr = lax.rsqrt(jnp.mean(oc * oc, axis=-1, keepdims=True) + EPS)
            out_ref[:, cols] = (_silu(gate) * (oc * rr * ng_ref[:, cols])).astype(out_ref.dtype)

    return pl.pallas_call(
        body, name="ret_fwd", grid=(B, nch),
        in_specs=[pl.BlockSpec((CH, 4096), lambda b, c: (b * nch + c, 0)),
                  pl.BlockSpec((CH, 128), lambda b, c: (c, 0)), pl.BlockSpec((CH, 128), lambda b, c: (c, 0)),
                  pl.BlockSpec((1, 1024), lambda b, c: (0, 0)),
                  pl.BlockSpec(memory_space=pl.ANY)],
        out_specs=[pl.BlockSpec((CH, 1024), lambda b, c: (b * nch + c, 1)),
                   pl.BlockSpec((CH, 1024), lambda b, c: (b * nch + c, 0)),
                   pl.BlockSpec((None, None, RET_HEADS, 256, 256), lambda b, c: (b, c, 0, 0, 0))],
        out_shape=[jax.ShapeDtypeStruct(ycat.shape, ycat.dtype), jax.ShapeDtypeStruct((R, 1024), F32),
                   jax.ShapeDtypeStruct((B, nch, RET_HEADS, 256, 256), F32)],
        scratch_shapes=[pltpu.VMEM((RET_HEADS, 256, 256), F32)],
        input_output_aliases={4: 0},
        compiler_params=_cparams(("parallel", "arbitrary")),
    )(u0, cos, sin, norm_g, ycat)


def _ret_bwd(dycat, u0, opre, rin, cos, sin, norm_g, B, nch):
    R = u0.shape[0]

    def body(dy_ref, u_ref, opre_ref, rin_ref, cos_ref, sin_ref, ng_ref, du_ref, pg_ref, dR):
        @pl.when(pl.program_id(1) == 0)
        def _():
            dR[...] = jnp.zeros_like(dR)
            pg_ref[...] = jnp.zeros_like(pg_ref)

        cos_v, sin_v = cos_ref[...], sin_ref[...]
        for h in range(RET_HEADS):
            lg = _RET_LG[h]
            cols = slice(256 * h, 256 * h + 256)
            decay, decay_t, zeta, xi = _ret_decays(lg)
            qr = _rot(u_ref[:, cols], cos_v, sin_v)
            kr = _rot(u_ref[:, 1024 + 256 * h:1024 + 256 * h + 256], cos_v, sin_v) * _RET_SCALE
            v = u_ref[:, 2048 + 256 * h:2048 + 256 * h + 256]
            gate = u_ref[:, 3072 + 256 * h:3072 + 256 * h + 256]
            ng = ng_ref[:, cols]
            o = opre_ref[:, cols]
            oc = o - jnp.mean(o, axis=-1, keepdims=True)
            rr = lax.rsqrt(jnp.mean(oc * oc, axis=-1, keepdims=True) + EPS)
            ohat = oc * rr
            dout = dy_ref[:, cols]
            du_ref[:, 3072 + 256 * h:3072 + 256 * h + 256] = dout * (ohat * ng) * _dsilu(gate)
            don = dout * _silu(gate)
            pg_ref[0:1, cols] += jnp.sum(don * ohat, axis=0, keepdims=True)
            dohat = don * ng
            do = rr * (dohat - jnp.mean(dohat, axis=-1, keepdims=True)
                       - ohat * jnp.mean(dohat * ohat, axis=-1, keepdims=True))
            Rh = rin_ref[h]
            dRn = dR[h]
            sc_t = _dot_nt(kr, qr) * decay_t
            dv = _dot(sc_t, do) + _dot(kr * zeta, dRn)
            ds = _dot_nt(do, v) * decay
            ds_t = _dot_nt(v, do) * decay_t
            dox = do * xi
            dq = _dot(ds, kr) + _dot_nt(dox, Rh)
            dk = _dot(ds_t, qr) + zeta * _dot_nt(v, dRn)
            dR[h] = math.exp(CH * lg) * dRn + _dot(qr.T, dox)
            du_ref[:, cols] = _unrot(dq, cos_v, sin_v)
            du_ref[:, 1024 + 256 * h:1024 + 256 * h + 256] = _unrot(dk, cos_v, sin_v) * _RET_SCALE
            du_ref[:, 2048 + 256 * h:2048 + 256 * h + 256] = dv

    rmap = lambda b, c: (b * nch + nch - 1 - c, 0)
    return pl.pallas_call(
        body, name="ret_bwd", grid=(B, nch),
        in_specs=[pl.BlockSpec((CH, 1024), lambda b, c: (b * nch + nch - 1 - c, 1)),
                  pl.BlockSpec((CH, 4096), rmap), pl.BlockSpec((CH, 1024), rmap),
                  pl.BlockSpec((None, None, RET_HEADS, 256, 256), lambda b, c: (b, nch - 1 - c, 0, 0, 0)),
                  pl.BlockSpec((CH, 128), lambda b, c: (nch - 1 - c, 0)),
                  pl.BlockSpec((CH, 128), lambda b, c: (nch - 1 - c, 0)),
                  pl.BlockSpec((1, 1024), lambda b, c: (0, 0))],
        out_specs=[pl.BlockSpec((CH, 4096), rmap), pl.BlockSpec((None, 8, 1024), lambda b, c: (b, 0, 0))],
        out_shape=[jax.ShapeDtypeStruct((R, 4096), F32), jax.ShapeDtypeStruct((B, 8, 1024), F32)],
        scratch_shapes=[pltpu.VMEM((RET_HEADS, 256, 256), F32)],
        compiler_params=_cparams(("parallel", "arbitrary")),
    )(dycat, u0, opre, rin, cos, sin, norm_g)


_SB_SCALE = SB_HD ** -0.5


_SB_NB = 2


def _sb_valid(qb, kb, live):
    qpos = qb * CH + jnp.bitwise_and(_row_ids(2 * CH, CH), CH - 1)
    kpos = kb * CH + _lane_ids(2 * CH, CH)
    first = PAD + (1 - live) * (1 << 24)
    return jnp.logical_and(kpos < qpos, kpos >= first)


def _sb_softplus(z):
    return jnp.maximum(z, 0.0) + jnp.log(1.0 + jnp.exp(-jnp.abs(z)))


def _stack_heads(x):
    hm = _lane_ids(1, 128) < SB_HD
    return jnp.concatenate([jnp.where(hm, x, 0.0), jnp.where(hm, 0.0, x)], axis=0)


def _unstack_heads(x2):
    return jnp.where(_lane_ids(1, 128) < SB_HD, x2[:CH], x2[CH:])


def _sb_fwd(u1, B, nch):
    R = u1.shape[0]
    Pn = nch * CH

    def body(q_ref, k_ref, v_ref, out_ref, s_ref):
        qb = pl.program_id(2)
        q2 = _stack_heads(q_ref[...] * _SB_SCALE)
        mgt = (_row_ids(CH, CH) > _lane_ids(CH, CH)).astype(F32)

        def step(i, carry):
            out2, acc = carry
            blocks = []
            for t in range(_SB_NB):
                kb = qb - _SB_NB * i - t
                live = (kb >= 0).astype(jnp.int32)
                kbc = jnp.maximum(kb, 0)
                start = pl.multiple_of(kbc * CH, CH)
                valid = _sb_valid(qb, kbc, live)
                z = _dot_nt(q2, k_ref[pl.ds(start, CH), :])
                sp = _sb_softplus(z)
                lm = jnp.where(valid, -sp, 0.0)
                blocks.append((valid, z - sp, _dot_split(lm, mgt), jnp.sum(lm, axis=1, keepdims=True), start))
            for valid, ls, loc, rs, start in blocks:
                w = jnp.where(valid, jnp.exp(ls + loc + acc), 0.0)
                out2 = out2 + _dot(w, v_ref[pl.ds(start, CH), :])
                acc = acc + rs
            return out2, acc

        trips = (qb + _SB_NB) // _SB_NB
        out2, acc = lax.fori_loop(0, trips, step, (jnp.zeros((2 * CH, 128), F32), jnp.zeros((2 * CH, 1), F32)))
        out_ref[...] = _unstack_heads(out2).astype(out_ref.dtype)
        s_ref[...] = _unstack_heads(jnp.broadcast_to(acc, (2 * CH, 128)))

    qspec = lambda off: pl.BlockSpec((CH, 128), lambda b, hp, qb: (b * nch + qb, off + hp))
    kspec = lambda off: pl.BlockSpec((Pn, 128), lambda b, hp, qb: (b, off + hp))
    return pl.pallas_call(
        body, name="sb_fwd", grid=(B, SB_HEADS // 2, nch),
        in_specs=[qspec(0), kspec(8), kspec(16)],
        out_specs=[qspec(0), qspec(0)],
        out_shape=[jax.ShapeDtypeStruct((R, 2048), _MXU), jax.ShapeDtypeStruct((R, 1024), F32)],
        compiler_params=_cparams(("parallel", "parallel", "arbitrary")),
    )(u1, u1, u1)


def _sb_bwd(dycat, u1, stot, B, nch):
    R = u1.shape[0]
    Pn = nch * CH

    def body(q_ref, k_ref, v_ref, do_ref, s_ref, dq_ref, dk_ref, dv_ref):
        qb = pl.program_id(2)

        @pl.when(qb == 0)
        def _():
            dk_ref[...] = jnp.zeros_like(dk_ref)
            dv_ref[...] = jnp.zeros_like(dv_ref)

        q2 = _stack_heads(q_ref[...] * _SB_SCALE)
        do2 = _stack_heads(do_ref[...])
        stv = s_ref[...]
        lane = _lane_ids(1, 128)
        s2 = jnp.concatenate([jnp.sum(jnp.where(lane == 0, stv, 0.0), axis=1, keepdims=True),
                              jnp.sum(jnp.where(lane == SB_HD, stv, 0.0), axis=1, keepdims=True)], axis=0)
        rr = _row_ids(CH, CH)
        cc = _lane_ids(CH, CH)
        mle = (rr <= cc).astype(F32)
        mlt = (rr < cc).astype(F32)

        def step(i, carry):
            dq2, pacc, gacc = carry
            blocks = []
            for t in range(_SB_NB):
                kb = _SB_NB * i + t
                live = (kb <= qb).astype(jnp.int32)
                start = pl.multiple_of(jnp.minimum(kb, qb) * CH, CH)
                valid = _sb_valid(qb, jnp.minimum(kb, qb), live)
                z = _dot_nt(q2, k_ref[pl.ds(start, CH), :])
                sp = _sb_softplus(z)
                lm = jnp.where(valid, -sp, 0.0)
                blocks.append((valid, z - sp, _dot_split(lm, mle), jnp.sum(lm, axis=1, keepdims=True), start))
            stage = []
            for valid, ls, ploc, rs, start in blocks:
                w = jnp.where(valid, jnp.exp(ls + (s2 - (ploc + pacc))), 0.0)
                gg = _dot_nt(do2, v_ref[pl.ds(start, CH), :]) * w
                stage.append((valid, ls, w, gg, _dot_split(gg, mlt), jnp.sum(gg, axis=1, keepdims=True), start))
                pacc = pacc + rs
            for valid, ls, w, gg, gloc, gs, start in stage:
                sig = jnp.exp(ls)
                dz = jnp.where(valid, gg * (1.0 - sig) - (gloc + gacc) * sig, 0.0)
                dq2 = dq2 + _dot(dz, k_ref[pl.ds(start, CH), :])
                dk_ref[pl.ds(start, CH), :] += _dot_tn(dz, q2)
                dv_ref[pl.ds(start, CH), :] += _dot_tn(w, do2)
                gacc = gacc + gs
            return dq2, pacc, gacc

        zero = jnp.zeros((2 * CH, 1), F32)
        trips = (qb + _SB_NB) // _SB_NB
        dq2 = lax.fori_loop(0, trips, step, (jnp.zeros((2 * CH, 128), F32), zero, zero))[0]
        dq_ref[...] = _unstack_heads(dq2) * _SB_SCALE

    qspec = lambda off: pl.BlockSpec((CH, 128), lambda b, hp, qb: (b * nch + qb, off + hp))
    kspec = lambda off: pl.BlockSpec((Pn, 128), lambda b, hp, qb: (b, off + hp))
    full = jax.ShapeDtypeStruct((R, 1024), F32)
    return pl.pallas_call(
        body, name="sb_bwd", grid=(B, SB_HEADS // 2, nch),
        in_specs=[qspec(0), kspec(8), kspec(16), qspec(0), qspec(0)],
        out_specs=[qspec(0), kspec(0), kspec(0)],
        out_shape=[full, full, full],
        compiler_params=_cparams(("parallel", "parallel", "arbitrary")),
    )(u1, u1, u1, dycat, stot)


def _neg_expm1(x):
    series = -(x * (1.0 + x * (0.5 + x * (1.0 / 6.0 + x * (1.0 / 24.0)))))
    return jnp.where(x > -0.05, series, 1.0 - jnp.exp(x))


def _lru_gates(x, wa_ref, ba_ref, wx_ref, bx_ref, lam_ref):
    rs, is_ = [], []
    for n in range(LRU_BLOCKS):
        xb = x[:, 128 * n:128 * n + 128]
        rs.append(_dot(xb, wa_ref[n]))
        is_.append(_dot(xb, wx_ref[n]))
    r = _sigmoid(jnp.concatenate(rs, axis=1) + ba_ref[...])
    i = _sigmoid(jnp.concatenate(is_, axis=1) + bx_ref[...])
    sp = _softplus(-lam_ref[...])
    la = -LRU_C * r * sp
    a = jnp.exp(la)
    mult = jnp.sqrt(jnp.maximum(_neg_expm1(2.0 * la), 0.0))
    return r, i, sp, a, mult


def _lru_fwd(u1, ycat, conv_w, conv_b, wa, ba, wx, bx, lam, B, nch):
    R = u1.shape[0]

    def body(x_ref, xp_ref, gate_ref, cw_ref, cb_ref, wa_ref, ba_ref, wx_ref, bx_ref, lam_ref, ycat_in,
             out_ref, hs_ref, hc):
        c = pl.program_id(1)

        @pl.when(c == 0)
        def _():
            hc[...] = jnp.zeros_like(hc)

        x = _conv_pre(xp_ref[...], x_ref[...], cw_ref, cb_ref, 4)
        r, i, sp, a, mult = _lru_gates(x, wa_ref, ba_ref, wx_ref, bx_ref, lam_ref)
        b = jnp.where(_real_rows(c), mult * (i * x), 0.0)
        rows = _row_ids(CH)
        s = 1
        while s < CH:
            a_s = jnp.where(rows >= s, pltpu.roll(a, s, axis=0), 1.0)
            b_s = jnp.where(rows >= s, pltpu.roll(b, s, axis=0), 0.0)
            b = a * b_s + b
            a = a * a_s
            s *= 2
        h = a * hc[0:1, :] + b
        hs_ref[...] = h
        hc[0:1, :] = hs_ref[CH - 1:CH, :]
        out_ref[...] = (h * _gelu(gate_ref[...])).astype(out_ref.dtype)

    row = lambda col: pl.BlockSpec((CH, 1024), lambda b, c: (b * nch + c, col))
    vec = pl.BlockSpec((1, 1024), lambda b, c: (0, 0))
    wsp = pl.BlockSpec((LRU_BLOCKS, 128, 128), lambda b, c: (0, 0, 0))
    return pl.pallas_call(
        body, name="lru_fwd", grid=(B, nch),
        in_specs=[row(4), pl.BlockSpec((8, 1024), _prev8_map(nch, 4)), row(3),
                  pl.BlockSpec((4, 1024), lambda b, c: (0, 0)), vec, wsp, vec, wsp, vec, vec,
                  pl.BlockSpec(memory_space=pl.ANY)],
        out_specs=[row(1), row(0)],
        out_shape=[jax.ShapeDtypeStruct(ycat.shape, ycat.dtype), jax.ShapeDtypeStruct((R, 1024), F32)],
        scratch_shapes=[pltpu.VMEM((8, 1024), F32)],
        input_output_aliases={10: 0},
        compiler_params=_cparams(("parallel", "arbitrary")),
    )(u1, u1, u1, conv_w, conv_b, wa, ba, wx, bx, lam, ycat)


def _lru_bwd(dycat, u1, hs, conv_w, conv_b, wa, ba, wx, bx, lam, B, nch):
    R = u1.shape[0]

    def body(dy_ref, x_ref, xp_ref, gate_ref, hs_ref, hsp_ref, cw_ref, cb_ref, wa_ref, ba_ref, wx_ref, bx_ref, lam_ref,
             dgate_ref, dxc_ref, pg_ref, dwa_ref, dwx_ref, lc):
        c = nch - 1 - pl.program_id(1)

        @pl.when(pl.program_id(1) == 0)
        def _():
            lc[...] = jnp.zeros_like(lc)
            pg_ref[...] = jnp.zeros_like(pg_ref)
            dwa_ref[...] = jnp.zeros_like(dwa_ref)
            dwx_ref[...] = jnp.zeros_like(dwx_ref)

        x = _conv_pre(xp_ref[...], x_ref[...], cw_ref, cb_ref, 4)
        r, i, sp, a, mult = _lru_gates(x, wa_ref, ba_ref, wx_ref, bx_ref, lam_ref)
        h = hs_ref[...]
        hprev = _shift_down(hsp_ref[...], h, 1)
        gate = gate_ref[...]
        dy = dy_ref[...]
        dgate_ref[...] = dy * h * _dgelu(gate)
        rows = _row_ids(CH)
        lam_t = dy * _gelu(gate) + jnp.where(rows == CH - 1, lc[0:1, :], 0.0)
        coef = jnp.where(rows < CH - 1, pltpu.roll(a, CH - 1, axis=0), 0.0)
        s = 1
        while s < CH:
            c_s = jnp.where(rows < CH - s, pltpu.roll(coef, CH - s, axis=0), 1.0)
            l_s = jnp.where(rows < CH - s, pltpu.roll(lam_t, CH - s, axis=0), 0.0)
            lam_t = coef * l_s + lam_t
            coef = coef * c_s
            s *= 2
        lc[0:1, :] = jnp.sum(jnp.where(rows == 0, a * lam_t, 0.0), axis=0, keepdims=True)
        db = jnp.where(_real_rows(c), lam_t, 0.0)
        da = db * hprev
        dmult = db * (i * x)
        di = db * mult * x
        dx = db * mult * i
        pos = mult > 0.0
        dla = da * a + jnp.where(pos, -dmult * (a * a) / jnp.where(pos, mult, 1.0), 0.0)
        dr = dla * (-LRU_C * sp)
        pg_ref[2:3, :] += jnp.sum(dla * (LRU_C * r) * _sigmoid(-lam_ref[...]), axis=0, keepdims=True)
        dpr = dr * r * (1.0 - r)
        dpi = di * i * (1.0 - i)
        pg_ref[0:1, :] += jnp.sum(dpr, axis=0, keepdims=True)
        pg_ref[1:2, :] += jnp.sum(dpi, axis=0, keepdims=True)
        dxs = []
        for n in range(LRU_BLOCKS):
            blk = slice(128 * n, 128 * n + 128)
            dxs.append(dx[:, blk] + _dot_nt(dpr[:, blk], wa_ref[n]) + _dot_nt(dpi[:, blk], wx_ref[n]))
            dwa_ref[n] += _dot_tn(x[:, blk], dpr[:, blk])
            dwx_ref[n] += _dot_tn(x[:, blk], dpi[:, blk])
        dxc_ref[...] = jnp.concatenate(dxs, axis=1)

    rmap = lambda col: (lambda b, c: (b * nch + nch - 1 - c, col))
    row = lambda col: pl.BlockSpec((CH, 1024), rmap(col))
    prev = lambda col: pl.BlockSpec(
        (8, 1024), lambda b, c: (jnp.maximum((b * nch + nch - 1 - c) * (CH // 8) - 1, 0), col))
    vec = pl.BlockSpec((1, 1024), lambda b, c: (0, 0))
    wsp = pl.BlockSpec((LRU_BLOCKS, 128, 128), lambda b, c: (0, 0, 0))
    full = jax.ShapeDtypeStruct((R, 1024), F32)
    return pl.pallas_call(
        body, name="lru_bwd", grid=(B, nch),
        in_specs=[row(1), row(4), prev(4), row(3), row(0), prev(0),
                  pl.BlockSpec((4, 1024), lambda b, c: (0, 0)), vec, wsp, vec, wsp, vec, vec],
        out_specs=[row(0), row(0), pl.BlockSpec((None, 8, 1024), lambda b, c: (b, 0, 0)),
                   pl.BlockSpec((None, LRU_BLOCKS, 128, 128), lambda b, c: (b, 0, 0, 0)),
                   pl.BlockSpec((None, LRU_BLOCKS, 128, 128), lambda b, c: (b, 0, 0, 0))],
        out_shape=[full, full, jax.ShapeDtypeStruct((B, 8, 1024), F32),
                   jax.ShapeDtypeStruct((B, LRU_BLOCKS, 128, 128), F32),
                   jax.ShapeDtypeStruct((B, LRU_BLOCKS, 128, 128), F32)],
        scratch_shapes=[pltpu.VMEM((8, 1024), F32)],
        compiler_params=_cparams(("parallel", "arbitrary")),
    )(dycat, u1, u1, u1, hs, hs, conv_w, conv_b, wa, ba, wx, bx, lam)


_FFN_TC = FFN // 2


def _ffn_specs(nch):
    nt = FFN // _FFN_TC
    row = lambda off: pl.BlockSpec((CH, _FFN_TC), lambda b, c, j: (b * nch + c, off + j))
    prev = lambda off: pl.BlockSpec(
        (8, _FFN_TC), lambda b, c, j: (jnp.maximum((b * nch + c) * (CH // 8) - 1, 0), off + j))
    wsp = lambda off: pl.BlockSpec((3, _FFN_TC), lambda b, c, j: (0, off + j))
    bsp = lambda off: pl.BlockSpec((1, _FFN_TC), lambda b, c, j: (0, off + j))
    return nt, row, [row(0), prev(0), row(nt), prev(nt), wsp(0), wsp(nt), bsp(0), bsp(nt)]


def _ffn_act_fwd(uf, conv_w, conv_b, B, nch):
    R = uf.shape[0]
    nt, row, specs = _ffn_specs(nch)

    def body(g_ref, gp_ref, u_ref, up_ref, wg_ref, wu_ref, bg_ref, bu_ref, o_ref):
        cg = _conv_pre(gp_ref[...], g_ref[...], wg_ref, bg_ref, 3)
        cu = _conv_pre(up_ref[...], u_ref[...], wu_ref, bu_ref, 3)
        o_ref[...] = jnp.where(_real_rows(pl.program_id(1)), _silu(cg) * cu, 0.0).astype(o_ref.dtype)

    return pl.pallas_call(
        body, name="ffn_act_fwd", grid=(B, nch, nt),
        in_specs=specs, out_specs=row(0),
        out_shape=jax.ShapeDtypeStruct((R, FFN), _MXU),
        compiler_params=_cparams(("parallel", "parallel", "parallel")),
    )(uf, uf, uf, uf, conv_w, conv_w, conv_b, conv_b)


def _ffn_act_bwd(da, uf, conv_w, conv_b, B, nch):
    R = uf.shape[0]
    nt, row, specs = _ffn_specs(nch)

    def body(da_ref, g_ref, gp_ref, u_ref, up_ref, wg_ref, wu_ref, bg_ref, bu_ref, dg_ref, du_ref):
        cg = _conv_pre(gp_ref[...], g_ref[...], wg_ref, bg_ref, 3)
        cu = _conv_pre(up_ref[...], u_ref[...], wu_ref, bu_ref, 3)
        dav = jnp.where(_real_rows(pl.program_id(1)), da_ref[...], 0.0)
        dg_ref[...] = dav * cu * _dsilu(cg)
        du_ref[...] = dav * _silu(cg)

    full = jax.ShapeDtypeStruct((R, FFN), F32)
    return pl.pallas_call(
        body, name="ffn_act_bwd", grid=(B, nch, nt),
        in_specs=[row(0)] + specs, out_specs=[row(0), row(0)],
        out_shape=[full, full],
        compiler_params=_cparams(("parallel", "parallel", "parallel")),
    )(da, uf, uf, uf, uf, conv_w, conv_w, conv_b, conv_b)


def _head(h, g, target, B, nch):
    R = h.shape[0]

    def body(h_ref, g_ref, t_ref, dh_ref, loss_ref, dg_ref):
        c = pl.program_id(1)

        @pl.when(c == 0)
        def _():
            dh_ref[...] = jnp.zeros_like(dh_ref)
            loss_ref[...] = jnp.zeros_like(loss_ref)
            dg_ref[...] = jnp.zeros_like(dg_ref)

        @pl.when(c > 0)
        def _():
            x = h_ref[...]
            gv = g_ref[...]
            r = lax.rsqrt(jnp.mean(x * x, axis=-1, keepdims=True) + EPS)
            xhat = x * r
            e = xhat * gv - t_ref[...]
            loss_ref[...] += 0.5 * jnp.sum(jnp.mean(e * e, axis=-1, keepdims=True), axis=0, keepdims=True)
            dy = e * (1.0 / D)
            dg_ref[0:1, :] += jnp.sum(dy * xhat, axis=0, keepdims=True)
            dx = dy * gv
            dh_ref[...] = r * (dx - xhat * jnp.mean(dx * xhat, axis=-1, keepdims=True))

    row = pl.BlockSpec((CH, D), lambda b, c: (b * nch + c, 0))
    return pl.pallas_call(
        body, name="head", grid=(B, nch),
        in_specs=[row, pl.BlockSpec((1, D), lambda b, c: (0, 0)),
                  pl.BlockSpec((CH, D), lambda b, c: (b * (nch - 1) + jnp.maximum(c - 1, 0), 0))],
        out_specs=[row, pl.BlockSpec((None, 8, 128), lambda b, c: (b, 0, 0)),
                   pl.BlockSpec((None, 8, D), lambda b, c: (b, 0, 0))],
        out_shape=[jax.ShapeDtypeStruct((R, D), F32), jax.ShapeDtypeStruct((B, 8, 128), F32),
                   jax.ShapeDtypeStruct((B, 8, D), F32)],
        compiler_params=_cparams(("parallel", "arbitrary")),
    )(h, g, target)


ADAM_LR = 0.001
ADAM_B1 = 0.9
ADAM_B2 = 0.999
ADAM_EPS = 1e-08
ADAM_WD = 0.01
ADAM_STEP = 10


def _adamw(w, g, m, v, name):
    Rr, C = w.shape
    tr = 64 if Rr % 64 == 0 else Rr

    def body(w_ref, g_ref, m_ref, v_ref, d_ref, nm_ref, nv_ref):
        gv = g_ref[...]
        nm = ADAM_B1 * m_ref[...] + (1.0 - ADAM_B1) * gv
        nv = ADAM_B2 * v_ref[...] + (1.0 - ADAM_B2) * (gv * gv)
        m_hat = nm / (1.0 - ADAM_B1 ** ADAM_STEP)
        v_hat = nv / (1.0 - ADAM_B2 ** ADAM_STEP)
        d_ref[...] = -ADAM_LR * (m_hat / (jnp.sqrt(v_hat) + ADAM_EPS) + ADAM_WD * w_ref[...])
        nm_ref[...] = nm
        nv_ref[...] = nv

    spec = pl.BlockSpec((tr, C), lambda i: (i, 0))
    sh = jax.ShapeDtypeStruct((Rr, C), F32)
    return pl.pallas_call(
        body, name=name, grid=(Rr // tr,),
        in_specs=[spec] * 4, out_specs=[spec] * 3, out_shape=[sh] * 3,
        compiler_params=_cparams(("parallel",)),
    )(w, g, m, v)


_MESH = pl.DeviceIdType.MESH
_ANY = pl.BlockSpec(memory_space=pl.ANY)


def _place():
    x, y, c = lax.axis_index("x"), lax.axis_index("y"), lax.axis_index("c")
    chips = [(1 - x, y), (x, 1 - y), (1 - x, 1 - y)]
    return x, y, c, chips


def _rcopy(src, dst, ssem, rsem, dev):
    return pltpu.make_async_remote_copy(src_ref=src, dst_ref=dst, send_sem=ssem, recv_sem=rsem,
                                        device_id=dev, device_id_type=_MESH)


def _gather_shards(bigs, small):
    nb = len(bigs)

    def body(*refs):
        ins, outs = refs[:nb + 1], refs[nb + 1:2 * nb + 2]
        ssem, rsem, fssem, frsem, lsem = refs[2 * nb + 2:]
        x, y, c, chips = _place()
        k = 2 * x + y
        sib = (x, y, 1 - c)

        def part(a, slot, hc):
            return outs[a].at[slot] if a == nb else outs[a].at[slot, hc]

        local = [pltpu.make_async_copy(ins[a], outs[a].at[k], lsem.at[a]) for a in range(nb + 1)]
        for cp in local:
            cp.start()
        first = []
        for a in range(nb + 1):
            src = ins[a] if a == nb else ins[a].at[c]
            for j, (cx, cy) in enumerate(chips):
                first.append(_rcopy(src, part(a, k, c), ssem.at[3 * a + j], rsem.at[3 * a + j], (cx, cy, c)))
        for cp in first:
            cp.start()
        passed = []
        for a in range(nb + 1):
            for j, (cx, cy) in enumerate(chips):
                got = part(a, 2 * cx + cy, c)
                _rcopy(got, got, ssem.at[3 * a + j], rsem.at[3 * a + j], (cx, cy, c)).wait_recv()
                if a < nb:
                    fw = _rcopy(got, got, fssem.at[3 * a + j], frsem.at[3 * a + j], sib)
                    fw.start()
                    passed.append(fw)
        for a in range(nb):
            for j, (cx, cy) in enumerate(chips):
                got = part(a, 2 * cx + cy, 1 - c)
                _rcopy(got, got, fssem.at[3 * a + j], frsem.at[3 * a + j], sib).wait_recv()
        for cp in first + passed:
            cp.wait_send()
        for cp in local:
            cp.wait()

    arrs = list(bigs) + [small]
    n = 3 * (nb + 1)
    return pl.pallas_call(
        body, name="gather_shards",
        in_specs=[_ANY] * (nb + 1), out_specs=[_ANY] * (nb + 1),
        out_shape=[jax.ShapeDtypeStruct((4,) + a.shape, a.dtype) for a in arrs],
        scratch_shapes=[pltpu.SemaphoreType.DMA((n,)), pltpu.SemaphoreType.DMA((n,)),
                        pltpu.SemaphoreType.DMA((n,)), pltpu.SemaphoreType.DMA((n,)),
                        pltpu.SemaphoreType.DMA((nb + 1,))],
    )(*arrs)


def _swap_halves(grads):
    na = len(grads)
    halves = [g.shape[1] // 2 for g in grads]

    def body(*refs):
        ins, outs = refs[:na], refs[na:2 * na]
        ssem, rsem = refs[2 * na:]
        x, y, c, _ = _place()
        sib = (x, y, 1 - c)
        cps = [_rcopy(ins[a].at[:, pl.ds((1 - c) * halves[a], halves[a]), :], outs[a], ssem.at[a], rsem.at[a], sib)
               for a in range(na)]
        for cp in cps:
            cp.start()
        for cp in cps:
            cp.wait()

    return pl.pallas_call(
        body, name="swap_halves",
        in_specs=[_ANY] * na, out_specs=[_ANY] * na,
        out_shape=[jax.ShapeDtypeStruct((4, g.shape[1] // 2, g.shape[2]), g.dtype) for g in grads],
        scratch_shapes=[pltpu.SemaphoreType.DMA((na,)), pltpu.SemaphoreType.DMA((na,))],
    )(*grads)


_SUM_ROWS = 32


def _chip_sum(grad, recv, core, name):
    _, r, cdim = grad.shape
    rh = r // 2
    tr = _SUM_ROWS
    assert rh % tr == 0
    nblk = rh // tr

    def body(core_ref, g_ref, r_ref, o_ref):
        o_ref[...] = (g_ref[...] + r_ref[...]).astype(o_ref.dtype)

    return pl.pallas_call(
        body, name=name,
        grid_spec=pltpu.PrefetchScalarGridSpec(
            num_scalar_prefetch=1, grid=(4, nblk),
            in_specs=[pl.BlockSpec((None, tr, cdim), lambda s, i, cr: (s, cr[0] * nblk + i, 0)),
                      pl.BlockSpec((None, tr, cdim), lambda s, i, cr: (s, i, 0))],
            out_specs=pl.BlockSpec((None, tr, cdim), lambda s, i, cr: (s, i, 0))),
        out_shape=jax.ShapeDtypeStruct((4, rh, cdim), BF16),
        compiler_params=_cparams(("parallel", "parallel")),
    )(core, grad, recv)


def _scatter_sums(sums):
    na = len(sums)

    def body(*refs):
        ins, outs = refs[:na], refs[na:2 * na]
        ssem, rsem, lsem = refs[2 * na:]
        x, y, c, chips = _place()
        k = 2 * x + y
        local = [pltpu.make_async_copy(ins[a].at[k], outs[a].at[k], lsem.at[a]) for a in range(na)]
        for cp in local:
            cp.start()
        cps = []
        for a in range(na):
            for j, (cx, cy) in enumerate(chips):
                cps.append(_rcopy(ins[a].at[2 * cx + cy], outs[a].at[k], ssem.at[3 * a + j], rsem.at[3 * a + j],
                                  (cx, cy, c)))
        for cp in cps:
            cp.start()
        for a in range(na):
            for j, (cx, cy) in enumerate(chips):
                got = outs[a].at[2 * cx + cy]
                _rcopy(got, got, ssem.at[3 * a + j], rsem.at[3 * a + j], (cx, cy, c)).wait_recv()
        for cp in cps:
            cp.wait_send()
        for cp in local:
            cp.wait()

    return pl.pallas_call(
        body, name="scatter_sums",
        in_specs=[_ANY] * na, out_specs=[_ANY] * na,
        out_shape=[jax.ShapeDtypeStruct(s.shape, s.dtype) for s in sums],
        scratch_shapes=[pltpu.SemaphoreType.DMA((3 * na,)), pltpu.SemaphoreType.DMA((3 * na,)),
                        pltpu.SemaphoreType.DMA((na,))],
    )(*sums)


def _sum_chips(parts, name):
    _, rh, cdim = parts.shape
    tr = _SUM_ROWS

    def body(p_ref, o_ref):
        acc = p_ref[0].astype(F32)
        for j in range(1, 4):
            acc = acc + p_ref[j].astype(F32)
        o_ref[...] = acc

    return pl.pallas_call(
        body, name=name, grid=(rh // tr,),
        in_specs=[pl.BlockSpec((4, tr, cdim), lambda i: (0, i, 0))],
        out_specs=pl.BlockSpec((tr, cdim), lambda i: (i, 0)),
        out_shape=jax.ShapeDtypeStruct((rh, cdim), F32),
        compiler_params=_cparams(("parallel",)),
    )(parts)


def _join_halves(reds):
    na = len(reds)

    def body(*refs):
        ins, outs = refs[:na], refs[na:2 * na]
        ssem, rsem, lsem = refs[2 * na:]
        x, y, c, _ = _place()
        sib = (x, y, 1 - c)
        local, cps = [], []
        for a in range(na):
            mine = outs[a].at[c]
            local.append(pltpu.make_async_copy(ins[a], mine, lsem.at[a]))
            cps.append(_rcopy(ins[a], mine, ssem.at[a], rsem.at[a], sib))
        for cp in local + cps:
            cp.start()
        for a in range(na):
            theirs = outs[a].at[1 - c]
            _rcopy(theirs, theirs, ssem.at[a], rsem.at[a], sib).wait_recv()
        for cp in cps:
            cp.wait_send()
        for cp in local:
            cp.wait()

    return pl.pallas_call(
        body, name="join_halves",
        in_specs=[_ANY] * na, out_specs=[_ANY] * na,
        out_shape=[jax.ShapeDtypeStruct((2,) + r.shape, r.dtype) for r in reds],
        scratch_shapes=[pltpu.SemaphoreType.DMA((na,)), pltpu.SemaphoreType.DMA((na,)), pltpu.SemaphoreType.DMA((na,))],
    )(*reds)


def _allreduce_small(buf):
    n = buf.shape[0]

    def body(in_ref, out_ref, recv, ssem, rsem):
        x, y, c, _ = _place()
        peers = [(x, y, 1 - c), (1 - x, y, c), (x, 1 - y, c)]
        out_ref[...] = in_ref[...]
        for r, peer in enumerate(peers):
            cp = _rcopy(out_ref, recv.at[r], ssem.at[r], rsem.at[r], peer)
            cp.start()
            cp.wait()
            out_ref[...] = out_ref[...] + recv[r]

    vm = pl.BlockSpec(memory_space=pltpu.VMEM)
    return pl.pallas_call(
        body, name="allreduce_small",
        in_specs=[vm], out_specs=vm,
        out_shape=jax.ShapeDtypeStruct(buf.shape, F32),
        scratch_shapes=[pltpu.VMEM((3, n, 128), F32), pltpu.SemaphoreType.DMA((3,)), pltpu.SemaphoreType.DMA((3,))],
        compiler_params=pltpu.CompilerParams(vmem_limit_bytes=VMEM_LIMIT),
    )(buf)


_W_NAMES = ['meta_tokens', 'l0_mix_norm', 'l0_w_in', 'l0_ssd_conv_w', 'l0_ssd_conv_b', 'l0_ssd_dt_bias', 'l0_ssd_a_log',
            'l0_ssd_d', 'l0_ssd_norm', 'l0_ret_norm', 'l0_w_out', 'l0_ffn_norm', 'l0_ffn_w_in', 'l0_ffn_conv_w',
            'l0_ffn_conv_b', 'l0_ffn_w_out', 'l1_mix_norm', 'l1_w_in', 'l1_lru_conv_w', 'l1_lru_conv_b', 'l1_lru_wa',
            'l1_lru_ba', 'l1_lru_wx', 'l1_lru_bx', 'l1_lru_lambda', 'l1_w_out', 'l1_ffn_norm', 'l1_ffn_w_in',
            'l1_ffn_conv_w', 'l1_ffn_conv_b', 'l1_ffn_w_out', 'final_norm']
_IN_NAMES = ['x'] + _W_NAMES + ['loss_target'] + ['m_' + n for n in _W_NAMES] + ['v_' + n for n in _W_NAMES]
_BIG = ['l0_w_in', 'l0_w_out', 'l0_ffn_w_in', 'l0_ffn_w_out', 'l1_w_in', 'l1_w_out', 'l1_ffn_w_in', 'l1_ffn_w_out']
_BIG_COLS = ('l0_w_in', 'l0_ffn_w_in', 'l1_w_in', 'l1_ffn_w_in')
_SMALL_SHARDED = ['meta_tokens', 'l0_ssd_conv_w', 'l0_ffn_conv_w', 'l1_lru_conv_w', 'l1_ffn_conv_w']
_SMALL = [n for n in _W_NAMES if n not in _BIG]


def _pack(arrs):
    flat = []
    for a in arrs:
        v = a.reshape(-1).astype(F32)
        flat.append(jnp.pad(v, (0, (-v.shape[0]) % 128)))
    v = jnp.concatenate(flat)
    v = jnp.pad(v, (0, (-v.shape[0]) % 1024))
    return v.reshape(-1, 128)


def _unpack(buf, shapes):
    out, row = [], 0
    for sh in shapes:
        n = int(np.prod(sh))
        rows = -(-n // 128)
        out.append(buf[row:row + rows].reshape(-1)[:n].reshape(sh))
        row += rows
    return out


def kernel(x, meta_tokens, l0_mix_norm, l0_w_in, l0_ssd_conv_w, l0_ssd_conv_b, l0_ssd_dt_bias, l0_ssd_a_log, l0_ssd_d, l0_ssd_norm, l0_ret_norm, l0_w_out, l0_ffn_norm, l0_ffn_w_in, l0_ffn_conv_w, l0_ffn_conv_b, l0_ffn_w_out, l1_mix_norm, l1_w_in, l1_lru_conv_w, l1_lru_conv_b, l1_lru_wa, l1_lru_ba, l1_lru_wx, l1_lru_bx, l1_lru_lambda, l1_w_out, l1_ffn_norm, l1_ffn_w_in, l1_ffn_conv_w, l1_ffn_conv_b, l1_ffn_w_out, final_norm, loss_target, m_meta_tokens, m_l0_mix_norm, m_l0_w_in, m_l0_ssd_conv_w, m_l0_ssd_conv_b, m_l0_ssd_dt_bias, m_l0_ssd_a_log, m_l0_ssd_d, m_l0_ssd_norm, m_l0_ret_norm, m_l0_w_out, m_l0_ffn_norm, m_l0_ffn_w_in, m_l0_ffn_conv_w, m_l0_ffn_conv_b, m_l0_ffn_w_out, m_l1_mix_norm, m_l1_w_in, m_l1_lru_conv_w, m_l1_lru_conv_b, m_l1_lru_wa, m_l1_lru_ba, m_l1_lru_wx, m_l1_lru_bx, m_l1_lru_lambda, m_l1_w_out, m_l1_ffn_norm, m_l1_ffn_w_in, m_l1_ffn_conv_w, m_l1_ffn_conv_b, m_l1_ffn_w_out, m_final_norm, v_meta_tokens, v_l0_mix_norm, v_l0_w_in, v_l0_ssd_conv_w, v_l0_ssd_conv_b, v_l0_ssd_dt_bias, v_l0_ssd_a_log, v_l0_ssd_d, v_l0_ssd_norm, v_l0_ret_norm, v_l0_w_out, v_l0_ffn_norm, v_l0_ffn_w_in, v_l0_ffn_conv_w, v_l0_ffn_conv_b, v_l0_ffn_w_out, v_l1_mix_norm, v_l1_w_in, v_l1_lru_conv_w, v_l1_lru_conv_b, v_l1_lru_wa, v_l1_lru_ba, v_l1_lru_wx, v_l1_lru_bx, v_l1_lru_lambda, v_l1_w_out, v_l1_ffn_norm, v_l1_ffn_w_in, v_l1_ffn_conv_w, v_l1_ffn_conv_b, v_l1_ffn_w_out, v_final_norm):
    args = (x, meta_tokens, l0_mix_norm, l0_w_in, l0_ssd_conv_w, l0_ssd_conv_b, l0_ssd_dt_bias, l0_ssd_a_log, l0_ssd_d, l0_ssd_norm, l0_ret_norm, l0_w_out, l0_ffn_norm, l0_ffn_w_in, l0_ffn_conv_w, l0_ffn_conv_b, l0_ffn_w_out, l1_mix_norm, l1_w_in, l1_lru_conv_w, l1_lru_conv_b, l1_lru_wa, l1_lru_ba, l1_lru_wx, l1_lru_bx, l1_lru_lambda, l1_w_out, l1_ffn_norm, l1_ffn_w_in, l1_ffn_conv_w, l1_ffn_conv_b, l1_ffn_w_out, final_norm, loss_target, m_meta_tokens, m_l0_mix_norm, m_l0_w_in, m_l0_ssd_conv_w, m_l0_ssd_conv_b, m_l0_ssd_dt_bias, m_l0_ssd_a_log, m_l0_ssd_d, m_l0_ssd_norm, m_l0_ret_norm, m_l0_w_out, m_l0_ffn_norm, m_l0_ffn_w_in, m_l0_ffn_conv_w, m_l0_ffn_conv_b, m_l0_ffn_w_out, m_l1_mix_norm, m_l1_w_in, m_l1_lru_conv_w, m_l1_lru_conv_b, m_l1_lru_wa, m_l1_lru_ba, m_l1_lru_wx, m_l1_lru_bx, m_l1_lru_lambda, m_l1_w_out, m_l1_ffn_norm, m_l1_ffn_w_in, m_l1_ffn_conv_w, m_l1_ffn_conv_b, m_l1_ffn_w_out, m_final_norm, v_meta_tokens, v_l0_mix_norm, v_l0_w_in, v_l0_ssd_conv_w, v_l0_ssd_conv_b, v_l0_ssd_dt_bias, v_l0_ssd_a_log, v_l0_ssd_d, v_l0_ssd_norm, v_l0_ret_norm, v_l0_w_out, v_l0_ffn_norm, v_l0_ffn_w_in, v_l0_ffn_conv_w, v_l0_ffn_conv_b, v_l0_ffn_w_out, v_l1_mix_norm, v_l1_w_in, v_l1_lru_conv_w, v_l1_lru_conv_b, v_l1_lru_wa, v_l1_lru_ba, v_l1_lru_wx, v_l1_lru_bx, v_l1_lru_lambda, v_l1_w_out, v_l1_ffn_norm, v_l1_ffn_w_in, v_l1_ffn_conv_w, v_l1_ffn_conv_b, v_l1_ffn_w_out, v_final_norm)
    p = dict(zip(_IN_NAMES, args))
    B, seq, _ = x.shape
    nch = (seq + CH) // CH
    Pn = nch * CH
    R = B * Pn
    chip = 2 * lax.axis_index("x") + lax.axis_index("y")
    row2 = lambda v: v.reshape(1, -1)
    pad128 = lambda v: jnp.pad(v, (0, 128 - v.shape[0])).reshape(1, 128)

    small_shapes = [p[n].shape for n in _SMALL_SHARDED]
    halved = lambda w: w.astype(_MXU).reshape(2, w.shape[0] // 2, w.shape[1])
    gathered = _gather_shards([halved(p[n]) for n in _BIG], _pack([p[n] for n in _SMALL_SHARDED]))
    g_big, g_small = gathered[:-1], gathered[-1]
    W = {}
    for n, g in zip(_BIG, g_big):
        g = g.reshape(4, -1, g.shape[3])
        W[n] = jnp.concatenate([g[k] for k in range(4)], axis=1) if n in _BIG_COLS else g.reshape(-1, g.shape[2])
    per_chip = [_unpack(g_small[k], small_shapes) for k in range(4)]
    for i, n in enumerate(_SMALL_SHARDED):
        W[n] = jnp.concatenate([per_chip[k][i] for k in range(4)], axis=1)
    w0 = W['l0_w_in']
    w0_main = jnp.concatenate([w0[:, 3088:], w0[:, :3072]], axis=1)
    w0_dt = jnp.pad(w0[:, 3072:3088], ((0, 0), (0, 112)))
    cos, sin = _rope_tables(nch)

    meta = jnp.broadcast_to(W['meta_tokens'][None], (B, N_META, D))
    h0 = jnp.concatenate([jnp.zeros((B, PAD, D), F32), meta, x], axis=1).reshape(R, D)
    n0 = _rmsnorm_fwd(h0, row2(p['l0_mix_norm']), "norm_l0_mix")
    u0 = _mm(n0, w0_main, "nn", F32, "l0_in_proj")
    udt = _mm(n0, w0_dt, "nn", F32, "l0_dt_proj")
    a_log, d_skip, dt_bias = pad128(p['l0_ssd_a_log']), pad128(p['l0_ssd_d']), pad128(p['l0_ssd_dt_bias'])
    ssd_cb = row2(p['l0_ssd_conv_b'])
    act, dt, dtt = _ssd_prep(u0, udt, W['l0_ssd_conv_w'], ssd_cb, dt_bias, B, nch)
    ycat0, ypre, hin = _ssd_fwd(act, u0, dt, dtt, a_log, d_skip, row2(p['l0_ssd_norm']), B, nch)
    ycat0, opre, rin = _ret_fwd(u0, ycat0, cos, sin, row2(p['l0_ret_norm']), B, nch)
    h1 = _mm(ycat0, W['l0_w_out'], "nn", F32, "l0_out_proj", add=h0)
    n1 = _rmsnorm_fwd(h1, row2(p['l0_ffn_norm']), "norm_l0_ffn")
    uf0 = _mm(n1, W['l0_ffn_w_in'], "nn", F32, "l0_ffn_in")
    f0_cb = row2(p['l0_ffn_conv_b'])
    a0 = _ffn_act_fwd(uf0, W['l0_ffn_conv_w'], f0_cb, B, nch)
    h2 = _mm(a0, W['l0_ffn_w_out'], "nn", F32, "l0_ffn_out", add=h1)
    n2 = _rmsnorm_fwd(h2, row2(p['l1_mix_norm']), "norm_l1_mix")
    u1 = _mm(n2, W['l1_w_in'], "nn", F32, "l1_in_proj")
    lru = (W['l1_lru_conv_w'], row2(p['l1_lru_conv_b']), p['l1_lru_wa'], row2(p['l1_lru_ba']), p['l1_lru_wx'],
           row2(p['l1_lru_bx']), row2(p['l1_lru_lambda']))
    ycat1, stot = _sb_fwd(u1, B, nch)
    ycat1, hs = _lru_fwd(u1, ycat1, *lru, B, nch)
    h3 = _mm(ycat1, W['l1_w_out'], "nn", F32, "l1_out_proj", add=h2)
    n3 = _rmsnorm_fwd(h3, row2(p['l1_ffn_norm']), "norm_l1_ffn")
    uf1 = _mm(n3, W['l1_ffn_w_in'], "nn", F32, "l1_ffn_in")
    f1_cb = row2(p['l1_ffn_conv_b'])
    a1 = _ffn_act_fwd(uf1, W['l1_ffn_conv_w'], f1_cb, B, nch)
    h4 = _mm(a1, W['l1_ffn_w_out'], "nn", F32, "l1_ffn_out", add=h3)
    dh4, lossp, dgf = _head(h4, row2(p['final_norm']), p['loss_target'].reshape(B * seq, D), B, nch)
    loss = lax.psum(jnp.sum(lossp[:, 0, 0]), ("x", "y", "c"))

    G = {'final_norm': dgf[:, 0].sum(0)}

    def ffn_bwd(layer, dh_out, h_in, n_in, uf, a_act, cb):
        pre = f"l{layer}_"
        w_in, w_out, cw = W[pre + 'ffn_w_in'], W[pre + 'ffn_w_out'], W[pre + 'ffn_conv_w']
        da = _mm(dh_out, w_out, "nt", F32, pre + "ffn_out_dgrad")
        G[pre + 'ffn_w_out'] = _mm(a_act, dh_out, "tn", F32, pre + "ffn_out_wgrad")
        dcg, dcu = _ffn_act_bwd(da, uf, cw, cb, B, nch)
        dug, dwg = _conv_bwd(dcg, uf, 0, cw[:, :FFN], 3, pre + "ffn_conv_bwd_g", tc=_FFN_TC)
        duu, dwu = _conv_bwd(dcu, uf, FFN, cw[:, FFN:], 3, pre + "ffn_conv_bwd_u", tc=_FFN_TC)
        G[pre + 'ffn_conv_w'] = jnp.concatenate([dwg[:3], dwu[:3]], axis=1)
        G[pre + 'ffn_conv_b'] = jnp.concatenate([dwg[7], dwu[7]])
        dn = _mm(dug, w_in, "nt", F32, pre + "ffn_in_dgrad_g")
        dn = _mm(duu, w_in, "nt", F32, pre + "ffn_in_dgrad_u", add=dn, b_off=FFN)
        G[pre + 'ffn_w_in'] = jnp.concatenate([_mm(n_in, dug, "tn", F32, pre + "ffn_in_wgrad_g"),
                                               _mm(n_in, duu, "tn", F32, pre + "ffn_in_wgrad_u")], axis=1)
        dh_in, dg = _rmsnorm_bwd(h_in, row2(p[pre + 'ffn_norm']), dn, dh_out, nch, pre + "ffn_norm_bwd")
        G[pre + 'ffn_norm'] = dg[0]
        return dh_in

    dh3 = ffn_bwd(1, dh4, h3, n3, uf1, a1, f1_cb)
    dy1 = _mm(dh3, W['l1_w_out'], "nt", F32, "l1_out_dgrad")
    G['l1_w_out'] = _mm(ycat1, dh3, "tn", F32, "l1_out_wgrad")
    dq, dk, dv = _sb_bwd(dy1, u1, stot, B, nch)
    dgate, dxc, pgl, dwa, dwx = _lru_bwd(dy1, u1, hs, *lru, B, nch)
    dxr, dcw = _conv_bwd(dxc, u1, 4096, W['l1_lru_conv_w'], 4, "l1_lru_conv_bwd")
    pgl = pgl.sum(0)
    G['l1_lru_ba'], G['l1_lru_bx'], G['l1_lru_lambda'] = pgl[0], pgl[1], pgl[2]
    G['l1_lru_wa'], G['l1_lru_wx'] = dwa.sum(0), dwx.sum(0)
    G['l1_lru_conv_w'], G['l1_lru_conv_b'] = dcw[:4], dcw[7]
    dn, dws = None, []
    for i, piece in enumerate((dq, dk, dv, dgate, dxr)):
        dn = _mm(piece, W['l1_w_in'], "nt", F32, f"l1_in_dgrad_{i}", add=dn, b_off=1024 * i)
        dws.append(_mm(n2, piece, "tn", F32, f"l1_in_wgrad_{i}"))
    G['l1_w_in'] = jnp.concatenate(dws, axis=1)
    dh2, dg = _rmsnorm_bwd(h2, row2(p['l1_mix_norm']), dn, dh3, nch, "l1_mix_norm_bwd")
    G['l1_mix_norm'] = dg[0]

    dh1 = ffn_bwd(0, dh2, h1, n1, uf0, a0, f0_cb)
    dy0 = _mm(dh1, W['l0_w_out'], "nt", F32, "l0_out_dgrad")
    G['l0_w_out'] = _mm(ycat0, dh1, "tn", F32, "l0_out_wgrad")
    dz, dxs, dbm, dcm, ddt4, pgs = _ssd_bwd(dy0, ypre, u0, act, dt, dtt, hin, a_log, d_skip, row2(p['l0_ssd_norm']), B, nch)
    dpre, ddtr, pgd = _ssd_prep_bwd(dxs, dbm, dcm, ddt4, u0, udt, W['l0_ssd_conv_w'], ssd_cb, dt_bias, B, nch)
    dxbc, dcw0 = _conv_bwd(dpre, u0, U0_XBC, W['l0_ssd_conv_w'], 4, "l0_ssd_conv_bwd")
    dqkvg, pgr = _ret_bwd(dy0, u0, opre, rin, cos, sin, row2(p['l0_ret_norm']), B, nch)
    pgs = pgs.sum(0)
    G['l0_ssd_norm'] = pgs[:, 0, :].reshape(-1)
    G['l0_ssd_d'] = pgs[:, 1, :128].sum(0)[:SSD_HEADS]
    G['l0_ssd_a_log'] = pgs[:, 2, :128].sum(0)[:SSD_HEADS]
    G['l0_ssd_dt_bias'] = pgd.sum(0)[0, :SSD_HEADS]
    G['l0_ssd_conv_w'], G['l0_ssd_conv_b'] = dcw0[:4], dcw0[7]
    G['l0_ret_norm'] = pgr.sum(0)[0]
    dn = _mm(dqkvg, w0_main, "nt", F32, "l0_in_dgrad_qkvg")
    dn = _mm(dz, w0_main, "nt", F32, "l0_in_dgrad_z", add=dn, b_off=U0_Z)
    dn = _mm(dxbc, w0_main, "nt", F32, "l0_in_dgrad_xbc", add=dn, b_off=U0_XBC)
    dn = _mm(ddtr, w0_dt, "nt", F32, "l0_in_dgrad_dt", add=dn)
    G['l0_w_in'] = jnp.concatenate([
        _mm(n0, dz, "tn", F32, "l0_in_wgrad_z"), _mm(n0, dxbc, "tn", F32, "l0_in_wgrad_xbc"),
        _mm(n0, ddtr, "tn", F32, "l0_in_wgrad_dt")[:, :SSD_HEADS], _mm(n0, dqkvg, "tn", F32, "l0_in_wgrad_qkvg")], axis=1)
    dh0, dg = _rmsnorm_bwd(h0, row2(p['l0_mix_norm']), dn, dh1, nch, "l0_mix_norm_bwd")
    G['l0_mix_norm'] = dg[0]
    dh0 = dh0.reshape(B, Pn, D)
    grad_x = dh0[:, CH:]
    G['meta_tokens'] = dh0[:, PAD:CH].sum(0)

    core = lax.axis_index("c").reshape(1).astype(jnp.int32)
    stacked = []
    for n in _BIG:
        g = G[n]
        if n in _BIG_COLS:
            stacked.append(g.reshape(g.shape[0], 4, g.shape[1] // 4).transpose(1, 0, 2))
        else:
            stacked.append(g.reshape(4, g.shape[0] // 4, g.shape[1]))
    theirs = _swap_halves(stacked)
    sums = [_chip_sum(g, t, core, "chip_sum_" + n) for n, g, t in zip(_BIG, stacked, theirs)]
    parts = _scatter_sums(sums)
    reds = [_sum_chips(q, "sum_chips_" + n) for n, q in zip(_BIG, parts)]
    grads = {n: g.reshape(-1, g.shape[2]) for n, g in zip(_BIG, _join_halves(reds))}
    small_full = _unpack(_allreduce_small(_pack([G[n] for n in _SMALL])), [G[n].shape for n in _SMALL])
    for n, g in zip(_SMALL, small_full):
        if n in _SMALL_SHARDED:
            cs = g.shape[1] // 4
            g = lax.dynamic_slice_in_dim(g, chip * cs, cs, axis=1)
        grads[n] = g.reshape(p[n].shape)

    delta, new_m, new_v = {}, {}, {}
    for n in _BIG:
        delta[n], new_m[n], new_v[n] = _adamw(p[n], grads[n], p['m_' + n], p['v_' + n], "adamw_" + n)
    shapes = [p[n].shape for n in _SMALL]
    outs = _adamw(_pack([p[n] for n in _SMALL]), _pack([grads[n] for n in _SMALL]), _pack([p['m_' + n] for n in _SMALL]),
                  _pack([p['v_' + n] for n in _SMALL]), "adamw_small")
    for dst, buf in zip((delta, new_m, new_v), outs):
        for n, a in zip(_SMALL, _unpack(buf, shapes)):
            dst[n] = a
    return (loss, grad_x, *[grads[n] for n in _W_NAMES], *[delta[n] for n in _W_NAMES],
            *[new_m[n] for n in _W_NAMES], *[new_v[n] for n in _W_NAMES])
```

```python
import math

import numpy as np
import jax
import jax.numpy as jnp
from jax import lax
from jax.experimental import pallas as pl
from jax.experimental.pallas import tpu as pltpu

F32 = jnp.float32
BF16 = jnp.bfloat16
_MXU = jnp.bfloat16

D = 1024
CH = 128
N_META = 16
PAD = CH - N_META
EPS = 1e-6

SSD_HEADS = 16
SSD_HD = 64
SSD_GROUPS = 4
RET_HEADS = 4
RET_DK = 256
SB_HEADS = 16
SB_HD = 64
LRU_BLOCKS = 8
LRU_C = 8.0
FFN = 2816
U0_Z = 4096
U0_XBC = 5120

VMEM_LIMIT = 56 * 1024 * 1024


def _cparams(sem):
    return pltpu.CompilerParams(dimension_semantics=sem, vmem_limit_bytes=VMEM_LIMIT)


def _dot(a, b, dims=((1,), (0,))):
    return lax.dot_general(a.astype(_MXU), b.astype(_MXU), (dims, ((), ())), preferred_element_type=F32)


def _dot_nt(a, b):
    return _dot(a, b, ((1,), (1,)))


def _dot_tn(a, b):
    return _dot(a.T, b)


def _dot_exact(a, b):
    return lax.dot_general(a, b, (((1,), (0,)), ((), ())), preferred_element_type=F32,
                           precision=lax.Precision.HIGHEST)


def _dot_split(x, m01):
    hi = x.astype(BF16)
    lo = (x - hi.astype(F32)).astype(BF16)
    m = m01.astype(BF16)
    return jnp.dot(hi, m, preferred_element_type=F32) + jnp.dot(lo, m, preferred_element_type=F32)


def _sigmoid(x):
    return jax.nn.sigmoid(x)


def _softplus(x):
    return jnp.maximum(x, 0.0) + jnp.log1p(jnp.exp(-jnp.abs(x)))


def _silu(x):
    return x * _sigmoid(x)


def _dsilu(x):
    s = _sigmoid(x)
    return s * (1.0 + x * (1.0 - s))


_GELU_C = math.sqrt(2.0 / math.pi)


def _gelu(x):
    return 0.5 * x * (1.0 + jnp.tanh(_GELU_C * (x + 0.044715 * x * x * x)))


def _dgelu(x):
    t = jnp.tanh(_GELU_C * (x + 0.044715 * x * x * x))
    return 0.5 * (1.0 + t) + 0.5 * x * (1.0 - t * t) * _GELU_C * (1.0 + 3.0 * 0.044715 * x * x)


def _row_ids(n, cols=1):
    return lax.broadcasted_iota(jnp.int32, (n, cols), 0)


def _lane_ids(rows, n):
    return lax.broadcasted_iota(jnp.int32, (rows, n), 1)


def _real_rows(chunk):
    return chunk * CH + _row_ids(CH) >= PAD


def _shift_down(prev8, cur, s):
    cat = jnp.concatenate([prev8, cur], axis=0)
    return pltpu.roll(cat, s, axis=0)[8:]


def _shift_up(cur, next8, s):
    n = cur.shape[0]
    cat = jnp.concatenate([cur, next8], axis=0)
    return pltpu.roll(cat, n + 8 - s, axis=0)[:n]


def _conv_pre(prev8, cur, w_ref, b_ref, K):
    acc = cur * w_ref[K - 1:K, :] + b_ref[...]
    for s in range(1, K):
        acc = acc + _shift_down(prev8, cur, s) * w_ref[K - 1 - s:K - s, :]
    return acc


def _prev8_map(nch, col):
    return lambda b, c: (jnp.maximum((b * nch + c) * (CH // 8) - 1, 0), col)


def _matmul(a, b, mode, out_dtype, tm, tn, tk, name, add=None, b_off=0):
    if mode == "nn":
        (M, K), (_, N) = a.shape, b.shape
    elif mode == "nt":
        (M, K), N = a.shape, b.shape[0]
    else:
        (K, M), (_, N) = a.shape, b.shape
    tm, tn, tk = min(tm, M), min(tn, N), min(tk, K)
    assert M % tm == 0 and N % tn == 0 and K % tk == 0 and b_off % tk == 0, (name, M, N, K, tm, tn, tk)
    koff = b_off // tk
    nk = K // tk
    dims = {"nn": ((1,), (0,)), "nt": ((1,), (1,)), "tn": ((0,), (0,))}[mode]
    if mode == "tn":
        a_spec = pl.BlockSpec((tk, tm), lambda i, j, k: (k, i))
    else:
        a_spec = pl.BlockSpec((tm, tk), lambda i, j, k: (i, k))
    if mode == "nt":
        b_spec = pl.BlockSpec((tn, tk), lambda i, j, k: (j, k + koff))
    else:
        b_spec = pl.BlockSpec((tk, tn), lambda i, j, k: (k, j))
    o_spec = pl.BlockSpec((tm, tn), lambda i, j, k: (i, j))
    has_add = add is not None

    def body(a_ref, b_ref, *rest):
        if has_add:
            add_ref, o_ref, acc = rest
        else:
            o_ref, acc = rest
        k = pl.program_id(2)

        @pl.when(k == 0)
        def _():
            acc[...] = jnp.zeros_like(acc)

        acc[...] += _dot(a_ref[...], b_ref[...], dims)

        @pl.when(k == nk - 1)
        def _():
            r = acc[...]
            if has_add:
                r = r + add_ref[...].astype(F32)
            o_ref[...] = r.astype(out_dtype)

    in_specs = [a_spec, b_spec] + ([o_spec] if has_add else [])
    args = (a, b) + ((add,) if has_add else ())
    return pl.pallas_call(
        body, name=name, grid=(M // tm, N // tn, nk),
        in_specs=in_specs, out_specs=o_spec,
        out_shape=jax.ShapeDtypeStruct((M, N), out_dtype),
        scratch_shapes=[pltpu.VMEM((tm, tn), F32)],
        compiler_params=_cparams(("parallel", "parallel", "arbitrary")),
    )(*args)


def _tile(n, prefs):
    for t in prefs:
        if n % t == 0:
            return t
    return n


def _mm(a, b, mode, out_dtype, name, add=None, b_off=0):
    if mode == "tn":
        K, M = a.shape
        N = b.shape[1]
        tm, tn, tk = _tile(M, (1024, 1408, 512, 256, 128)), _tile(N, (512, 256, 128)), _tile(K, (2176, 384, 256, 128))
    else:
        M, K = a.shape
        N = b.shape[1] if mode == "nn" else b.shape[0]
        tm, tn, tk = _tile(M, (1088, 768, 384, 256, 128)), _tile(N, (1024, 512, 256, 128)), _tile(K, (1024, 1408, 512, 256, 128))
    return _matmul(a, b, mode, out_dtype, tm, tn, tk, name, add=add, b_off=b_off)


def _rmsnorm_fwd(h, g, name):
    R = h.shape[0]
    tr = 2 * CH

    def body(h_ref, g_ref, o_ref):
        x = h_ref[...]
        r = lax.rsqrt(jnp.mean(x * x, axis=-1, keepdims=True) + EPS)
        o_ref[...] = (x * r * g_ref[...]).astype(o_ref.dtype)

    return pl.pallas_call(
        body, name=name, grid=(R // tr,),
        in_specs=[pl.BlockSpec((tr, D), lambda i: (i, 0)), pl.BlockSpec((1, D), lambda i: (0, 0))],
        out_specs=pl.BlockSpec((tr, D), lambda i: (i, 0)),
        out_shape=jax.ShapeDtypeStruct((R, D), _MXU),
        compiler_params=_cparams(("parallel",)),
    )(h, g)


def _rmsnorm_bwd(h, g, dn, dres, nch, name):
    R = h.shape[0]

    def body(h_ref, g_ref, dn_ref, dres_ref, dh_ref, dg_ref):
        i = pl.program_id(0)
        x = h_ref[...]
        r = lax.rsqrt(jnp.mean(x * x, axis=-1, keepdims=True) + EPS)
        xhat = x * r
        dn_v = dn_ref[...]
        dx = dn_v * g_ref[...]
        dh = r * (dx - xhat * jnp.mean(dx * xhat, axis=-1, keepdims=True))
        dh_ref[...] = jnp.where(_real_rows(i % nch), dres_ref[...] + dh, 0.0)

        @pl.when(i == 0)
        def _():
            dg_ref[...] = jnp.zeros_like(dg_ref)

        dg_ref[...] += jnp.sum(dn_v * xhat, axis=0, keepdims=True)

    row = pl.BlockSpec((CH, D), lambda i: (i, 0))
    vec = pl.BlockSpec((1, D), lambda i: (0, 0))
    return pl.pallas_call(
        body, name=name, grid=(R // CH,),
        in_specs=[row, vec, row, row], out_specs=[row, vec],
        out_shape=[jax.ShapeDtypeStruct((R, D), F32), jax.ShapeDtypeStruct((1, D), F32)],
        compiler_params=_cparams(("arbitrary",)),
    )(h, g, dn, dres)


def _ssd_prep(u0, udt, conv_w, conv_b, dt_bias, B, nch):
    R = u0.shape[0]

    def body(xs_ref, xsp_ref, bc_ref, bcp_ref, udt_ref, w0_ref, w1_ref, b0_ref, b1_ref, dtb_ref,
             act_ref, dt_ref, dtt_ref):
        keep = _real_rows(pl.program_id(1))
        a0 = _silu(_conv_pre(xsp_ref[...], xs_ref[...], w0_ref, b0_ref, 4))
        a1 = _silu(_conv_pre(bcp_ref[...], bc_ref[...], w1_ref, b1_ref, 4))
        act_ref[:, :1024] = jnp.where(keep, a0, 0.0)
        act_ref[:, 1024:] = jnp.where(keep, a1, 0.0)
        ok = jnp.logical_and(keep, _lane_ids(1, 128) < SSD_HEADS)
        dt = jnp.where(ok, _softplus(udt_ref[...] + dtb_ref[...]), 0.0)
        dt_ref[...] = dt
        dtt_ref[...] = dt.T

    row = lambda col: pl.BlockSpec((CH, 1024), lambda b, c: (b * nch + c, col))
    prev = lambda col: pl.BlockSpec((8, 1024), _prev8_map(nch, col))
    return pl.pallas_call(
        body, name="ssd_prep", grid=(B, nch),
        in_specs=[row(5), prev(5), row(6), prev(6),
                  pl.BlockSpec((CH, 128), lambda b, c: (b * nch + c, 0)),
                  pl.BlockSpec((4, 1024), lambda b, c: (0, 0)), pl.BlockSpec((4, 1024), lambda b, c: (0, 1)),
                  pl.BlockSpec((1, 1024), lambda b, c: (0, 0)), pl.BlockSpec((1, 1024), lambda b, c: (0, 1)),
                  pl.BlockSpec((1, 128), lambda b, c: (0, 0))],
        out_specs=[pl.BlockSpec((CH, 2048), lambda b, c: (b * nch + c, 0)),
                   pl.BlockSpec((CH, 128), lambda b, c: (b * nch + c, 0)),
                   pl.BlockSpec((128, CH), lambda b, c: (0, b * nch + c))],
        out_shape=[jax.ShapeDtypeStruct((R, 2048), F32), jax.ShapeDtypeStruct((R, 128), F32),
                   jax.ShapeDtypeStruct((128, R), F32)],
        compiler_params=_cparams(("parallel", "parallel")),
    )(u0, u0, u0, u0, udt, conv_w, conv_w, conv_b, conv_b, dt_bias)


def _ssd_head_terms(h, a_vec, dt_v, dtt_v, dsk_v):
    lane = _lane_ids(1, 128)
    sub = _row_ids(128)
    r = _row_ids(CH, CH)
    cidx = _lane_ids(CH, CH)
    a_h = jnp.sum(jnp.where(lane == h, a_vec, 0.0), axis=1, keepdims=True)
    dt_col = jnp.sum(jnp.where(lane == h, dt_v, 0.0), axis=1, keepdims=True)
    dt_row = jnp.sum(jnp.where(sub == h, dtt_v, 0.0), axis=0, keepdims=True)
    cs_col = jnp.sum(jnp.where(r >= cidx, dt_row * a_h, 0.0), axis=1, keepdims=True)
    cs_row = jnp.sum(jnp.where(r <= cidx, dt_col * a_h, 0.0), axis=0, keepdims=True)
    tot = jnp.sum(dt_col * a_h, axis=0, keepdims=True)
    dsk = jnp.sum(jnp.where(lane == h, dsk_v, 0.0), axis=1, keepdims=True)
    return a_h, dt_col, cs_col, cs_row, tot, dsk


def _ssd_fwd(act, u0, dt, dtt, a_log, d_skip, norm_g, B, nch):
    R = act.shape[0]

    def body(xs_ref, bm_ref, cm_ref, z_ref, dt_ref, dtt_ref, alog_ref, dsk_ref, ng_ref,
             out_ref, ypre_ref, hin_ref, H):
        g = pl.program_id(1)
        c = pl.program_id(2)

        @pl.when(c == 0)
        def _():
            H[...] = jnp.zeros_like(H)

        hin_ref[...] = H[...]
        a_vec = -jnp.exp(alog_ref[...])
        dt_v = dt_ref[...]
        dtt_v = dtt_ref[...]
        hm = _lane_ids(1, 128) < SSD_HD
        r = _row_ids(CH, CH)
        cidx = _lane_ids(CH, CH)
        Bm = bm_ref[...]
        Cm = cm_ref[...]
        CB = _dot_nt(Cm, Bm)
        ys = []
        for pair in range(2):
            cols = slice(128 * pair, 128 * pair + 128)
            xraw = xs_ref[:, cols]
            t = [_ssd_head_terms(4 * g + 2 * pair + j, a_vec, dt_v, dtt_v, dsk_ref[...]) for j in range(2)]
            sel = lambda f: jnp.where(hm, f(t[0]), f(t[1]))
            dtp = sel(lambda q: q[1])
            Ep = sel(lambda q: jnp.exp(q[2]))
            Wp = sel(lambda q: jnp.exp(q[4] - q[2]))
            etot = sel(lambda q: jnp.exp(q[4]))
            dsk = sel(lambda q: q[5])
            X = xraw * dtp
            ydiag = jnp.zeros((CH, 128), F32)
            for j in range(2):
                Lm = jnp.where(r >= cidx, jnp.exp(t[j][2] - t[j][3]), 0.0)
                Xh = jnp.where(hm if j == 0 else jnp.logical_not(hm), X, 0.0)
                ydiag = ydiag + _dot(CB * Lm, Xh)
            Hp = H[:, cols]
            yoff = Ep * _dot(Cm, Hp)
            S = _dot(Bm.T, X * Wp)
            H[:, cols] = etot * Hp + S
            ys.append(ydiag + yoff + xraw * dsk)
        y = jnp.concatenate(ys, axis=1)
        ypre_ref[...] = y
        yg = y * _silu(z_ref[...])
        rr = lax.rsqrt(jnp.mean(yg * yg, axis=-1, keepdims=True) + EPS)
        out_ref[...] = jnp.where(_real_rows(c), yg * rr * ng_ref[...], 0.0).astype(out_ref.dtype)

    rowb = lambda w, colf: pl.BlockSpec((CH, w), lambda b, g, c: (b * nch + c, colf(g)))
    vec = pl.BlockSpec((1, 128), lambda b, g, c: (0, 0))
    return pl.pallas_call(
        body, name="ssd_fwd", grid=(B, SSD_GROUPS, nch),
        in_specs=[rowb(256, lambda g: g), rowb(128, lambda g: 8 + g), rowb(128, lambda g: 12 + g),
                  rowb(256, lambda g: 16 + g), rowb(128, lambda g: 0),
                  pl.BlockSpec((128, CH), lambda b, g, c: (0, b * nch + c)),
                  vec, vec, pl.BlockSpec((1, 256), lambda b, g, c: (0, g))],
        out_specs=[rowb(256, lambda g: g), rowb(256, lambda g: g),
                   pl.BlockSpec((None, None, None, 128, 256), lambda b, g, c: (b, g, c, 0, 0))],
        out_shape=[jax.ShapeDtypeStruct((R, 2048), _MXU), jax.ShapeDtypeStruct((R, 1024), F32),
                   jax.ShapeDtypeStruct((B, SSD_GROUPS, nch, 128, 256), F32)],
        scratch_shapes=[pltpu.VMEM((128, 256), F32)],
        compiler_params=_cparams(("parallel", "parallel", "arbitrary")),
    )(act, act, act, u0, dt, dtt, a_log, d_skip, norm_g)


def _ssd_bwd(dycat, ypre, u0, act, dt, dtt, hin, a_log, d_skip, norm_g, B, nch):
    R = act.shape[0]

    def body(dy_ref, ypre_ref, z_ref, xs_ref, bm_ref, cm_ref, dt_ref, dtt_ref, hin_ref, alog_ref, dsk_ref, ng_ref,
             dz_ref, dxs_ref, db_ref, dc_ref, ddt_ref, pg_ref, dH):
        g = pl.program_id(1)
        c = nch - 1 - pl.program_id(2)

        @pl.when(pl.program_id(2) == 0)
        def _():
            dH[...] = jnp.zeros_like(dH)
            pg_ref[...] = jnp.zeros_like(pg_ref)

        z = z_ref[...]
        y = ypre_ref[...]
        ng = ng_ref[...]
        dout = jnp.where(_real_rows(c), dy_ref[...], 0.0)
        sz = _sigmoid(z)
        yg = y * z * sz
        rr = lax.rsqrt(jnp.mean(yg * yg, axis=-1, keepdims=True) + EPS)
        nrm = yg * rr
        pg_ref[0:1, :] += jnp.sum(dout * nrm, axis=0, keepdims=True)
        dn = dout * ng
        dyg = rr * (dn - nrm * jnp.mean(dn * nrm, axis=-1, keepdims=True))
        dy = dyg * z * sz
        dz_ref[...] = dyg * y * (sz * (1.0 + z * (1.0 - sz)))

        a_vec = -jnp.exp(alog_ref[...])
        dt_v = dt_ref[...]
        dtt_v = dtt_ref[...]
        lane = _lane_ids(1, 128)
        hm = lane < SSD_HD
        r = _row_ids(CH, CH)
        cidx = _lane_ids(CH, CH)
        last = _row_ids(CH) == CH - 1
        Bm = bm_ref[...]
        Cm = cm_ref[...]
        CB = _dot_nt(Cm, Bm)
        CBT = _dot_nt(Bm, Cm)
        dB = jnp.zeros((CH, 128), F32)
        dC = jnp.zeros((CH, 128), F32)
        dcs_all = jnp.zeros((CH, 128), F32)
        dtx_all = jnp.zeros((CH, 128), F32)
        dd_row = jnp.zeros((1, 128), F32)
        dxs = []
        for pair in range(2):
            cols = slice(128 * pair, 128 * pair + 128)
            xraw = xs_ref[:, cols]
            dyp = dy[:, cols]
            heads = [4 * g + 2 * pair + j for j in range(2)]
            t = [_ssd_head_terms(heads[j], a_vec, dt_v, dtt_v, dsk_ref[...]) for j in range(2)]
            sel = lambda f: jnp.where(hm, f(t[0]), f(t[1]))
            hsum = lambda v, j: jnp.sum(jnp.where(hm if j == 0 else jnp.logical_not(hm), v, 0.0), axis=1, keepdims=True)
            dtp = sel(lambda q: q[1])
            Ep = sel(lambda q: jnp.exp(q[2]))
            Wp = sel(lambda q: jnp.exp(q[4] - q[2]))
            etot = sel(lambda q: jnp.exp(q[4]))
            dsk = sel(lambda q: q[5])
            X = xraw * dtp
            Hp = hin_ref[:, cols]
            dHn = dH[:, cols]
            dskip = jnp.sum(dyp * xraw, axis=0, keepdims=True)
            yoff = Ep * _dot(Cm, Hp)
            dE = dyp * yoff
            dC = dC + _dot_nt(dyp * Ep, Hp)
            dH[:, cols] = etot * dHn + _dot(Cm.T, dyp * Ep)
            BdS = _dot(Bm, dHn)
            dX = Wp * BdS
            ew = X * BdS * Wp
            dB = dB + _dot_nt(X * Wp, dHn)
            hh = jnp.sum(dHn * Hp, axis=0, keepdims=True) * etot
            for j in range(2):
                hmask = hm if j == 0 else jnp.logical_not(hm)
                cs_col, cs_row = t[j][2], t[j][3]
                Lm = jnp.where(r >= cidx, jnp.exp(cs_col - cs_row), 0.0)
                LmT = jnp.where(cidx >= r, jnp.exp(cs_row - cs_col), 0.0)
                dyh = jnp.where(hmask, dyp, 0.0)
                Xh = jnp.where(hmask, X, 0.0)
                dM = _dot_nt(dyh, Xh)
                dMT = _dot_nt(Xh, dyh)
                M = CB * Lm
                MT = CBT * LmT
                dX = dX + _dot(MT, dyh)
                dC = dC + _dot(dM * Lm, Bm)
                dB = dB + _dot(dMT * LmT, Cm)
                g_rows = jnp.sum(dM * M, axis=1, keepdims=True)
                g_cols = jnp.sum(dMT * MT, axis=1, keepdims=True)
                dtot = (jnp.sum(hsum(ew, j), axis=0, keepdims=True)
                        + jnp.sum(jnp.where(hmask, hh, 0.0), axis=1, keepdims=True))
                dcs = g_rows - g_cols + hsum(dE, j) - hsum(ew, j) + jnp.where(last, dtot, 0.0)
                dcs_all = dcs_all + jnp.where(lane == heads[j], dcs, 0.0)
                dtx_all = dtx_all + jnp.where(lane == heads[j], hsum(dX * xraw, j), 0.0)
                dd_row = dd_row + jnp.where(lane == heads[j],
                                            jnp.sum(jnp.where(hmask, dskip, 0.0), axis=1, keepdims=True), 0.0)
            dxs.append(dX * dtp + dyp * dsk)
        dxs_ref[...] = jnp.concatenate(dxs, axis=1)
        db_ref[...] = dB
        dc_ref[...] = dC
        dadt = _dot_exact(jnp.where(cidx >= r, 1.0, 0.0), dcs_all)
        ddt_ref[...] = dadt * a_vec + dtx_all
        pg_ref[1:2, 0:128] += dd_row
        pg_ref[2:3, 0:128] += jnp.sum(dadt * dt_v, axis=0, keepdims=True) * a_vec

    rowb = lambda w, colf: pl.BlockSpec((CH, w), lambda b, g, c: (b * nch + nch - 1 - c, colf(g)))
    vec = pl.BlockSpec((1, 128), lambda b, g, c: (0, 0))
    return pl.pallas_call(
        body, name="ssd_bwd", grid=(B, SSD_GROUPS, nch),
        in_specs=[rowb(256, lambda g: g), rowb(256, lambda g: g), rowb(256, lambda g: 16 + g), rowb(256, lambda g: g),
                  rowb(128, lambda g: 8 + g), rowb(128, lambda g: 12 + g), rowb(128, lambda g: 0),
                  pl.BlockSpec((128, CH), lambda b, g, c: (0, b * nch + nch - 1 - c)),
                  pl.BlockSpec((None, None, None, 128, 256), lambda b, g, c: (b, g, nch - 1 - c, 0, 0)),
                  vec, vec, pl.BlockSpec((1, 256), lambda b, g, c: (0, g))],
        out_specs=[rowb(256, lambda g: g), rowb(256, lambda g: g), rowb(128, lambda g: g), rowb(128, lambda g: g),
                   rowb(128, lambda g: g),
                   pl.BlockSpec((None, None, 8, 256), lambda b, g, c: (b, g, 0, 0))],
        out_shape=[jax.ShapeDtypeStruct((R, 1024), F32), jax.ShapeDtypeStruct((R, 1024), F32),
                   jax.ShapeDtypeStruct((R, 512), F32), jax.ShapeDtypeStruct((R, 512), F32),
                   jax.ShapeDtypeStruct((R, 512), F32), jax.ShapeDtypeStruct((B, SSD_GROUPS, 8, 256), F32)],
        scratch_shapes=[pltpu.VMEM((128, 256), F32)],
        compiler_params=_cparams(("parallel", "parallel", "arbitrary")),
    )(dycat, ypre, u0, act, act, act, dt, dtt, hin, a_log, d_skip, norm_g)


def _ssd_prep_bwd(dxs, dB, dC, ddt4, u0, udt, conv_w, conv_b, dt_bias, B, nch):
    R = u0.shape[0]

    def body(dxs_ref, db_ref, dc_ref, ddt_ref, xs_ref, xsp_ref, bc_ref, bcp_ref, udt_ref, w0_ref, w1_ref, b0_ref, b1_ref,
             dtb_ref, dpre_ref, ddtr_ref, pgd_ref):
        c = pl.program_id(1)

        @pl.when(c == 0)
        def _():
            pgd_ref[...] = jnp.zeros_like(pgd_ref)

        keep = _real_rows(c)
        p0 = _conv_pre(xsp_ref[...], xs_ref[...], w0_ref, b0_ref, 4)
        p1 = _conv_pre(bcp_ref[...], bc_ref[...], w1_ref, b1_ref, 4)
        dpre_ref[:, :1024] = jnp.where(keep, dxs_ref[...] * _dsilu(p0), 0.0)
        dpre_ref[:, 1024:] = jnp.where(keep, jnp.concatenate([db_ref[...], dc_ref[...]], axis=1) * _dsilu(p1), 0.0)
        ddt = ddt_ref[:, 0:128] + ddt_ref[:, 128:256] + ddt_ref[:, 256:384] + ddt_ref[:, 384:512]
        ok = jnp.logical_and(keep, _lane_ids(1, 128) < SSD_HEADS)
        dr = jnp.where(ok, ddt * _sigmoid(udt_ref[...] + dtb_ref[...]), 0.0)
        ddtr_ref[...] = dr
        pgd_ref[0:1, :] += jnp.sum(dr, axis=0, keepdims=True)

    rw = lambda w: pl.BlockSpec((CH, w), lambda b, c: (b * nch + c, 0))
    row = lambda col: pl.BlockSpec((CH, 1024), lambda b, c: (b * nch + c, col))
    prev = lambda col: pl.BlockSpec((8, 1024), _prev8_map(nch, col))
    return pl.pallas_call(
        body, name="ssd_prep_bwd", grid=(B, nch),
        in_specs=[rw(1024), rw(512), rw(512), rw(512), row(5), prev(5), row(6), prev(6), rw(128),
                  pl.BlockSpec((4, 1024), lambda b, c: (0, 0)), pl.BlockSpec((4, 1024), lambda b, c: (0, 1)),
                  pl.BlockSpec((1, 1024), lambda b, c: (0, 0)), pl.BlockSpec((1, 1024), lambda b, c: (0, 1)),
                  pl.BlockSpec((1, 128), lambda b, c: (0, 0))],
        out_specs=[rw(2048), rw(128), pl.BlockSpec((None, 8, 128), lambda b, c: (b, 0, 0))],
        out_shape=[jax.ShapeDtypeStruct((R, 2048), F32), jax.ShapeDtypeStruct((R, 128), F32),
                   jax.ShapeDtypeStruct((B, 8, 128), F32)],
        compiler_params=_cparams(("parallel", "arbitrary")),
    )(dxs, dB, dC, ddt4, u0, u0, u0, u0, udt, conv_w, conv_w, conv_b, conv_b, dt_bias)


def _conv_bwd(dpre, xin, xin_col, w, K, name, tc=1024):
    R, C = dpre.shape
    assert C % tc == 0 and xin_col % tc == 0
    nr = R // CH
    xoff = xin_col // tc

    def body(dp_ref, dpn_ref, x_ref, xp_ref, w_ref, din_ref, dw_ref):
        i = pl.program_id(1)

        @pl.when(i == 0)
        def _():
            dw_ref[...] = jnp.zeros_like(dw_ref)

        dp = dp_ref[...]
        nxt = dpn_ref[...] * (i < nr - 1).astype(F32)
        x = x_ref[...]
        xp = xp_ref[...]
        din = dp * w_ref[K - 1:K, :]
        dw_ref[K - 1:K, :] += jnp.sum(dp * x, axis=0, keepdims=True)
        dw_ref[7:8, :] += jnp.sum(dp, axis=0, keepdims=True)
        for s in range(1, K):
            din = din + _shift_up(dp, nxt, s) * w_ref[K - 1 - s:K - s, :]
            dw_ref[K - 1 - s:K - s, :] += jnp.sum(dp * _shift_down(xp, x, s), axis=0, keepdims=True)
        din_ref[...] = din

    return pl.pallas_call(
        body, name=name, grid=(C // tc, nr),
        in_specs=[pl.BlockSpec((CH, tc), lambda j, i: (i, j)),
                  pl.BlockSpec((8, tc), lambda j, i: (jnp.minimum((i + 1) * (CH // 8), nr * (CH // 8) - 1), j)),
                  pl.BlockSpec((CH, tc), lambda j, i: (i, xoff + j)),
                  pl.BlockSpec((8, tc), lambda j, i: (jnp.maximum(i * (CH // 8) - 1, 0), xoff + j)),
                  pl.BlockSpec((K, tc), lambda j, i: (0, j))],
        out_specs=[pl.BlockSpec((CH, tc), lambda j, i: (i, j)),
                   pl.BlockSpec((8, tc), lambda j, i: (0, j))],
        out_shape=[jax.ShapeDtypeStruct((R, C), F32), jax.ShapeDtypeStruct((8, C), F32)],
        compiler_params=_cparams(("parallel", "arbitrary")),
    )(dpre, dpre, xin, xin, w)


_RET_LG = [float(v) for v in np.log1p(-np.exp2(-5.0 - np.arange(RET_HEADS, dtype=np.float32))).astype(np.float32)]
_RET_SCALE = RET_DK ** -0.5


def _rope_tables(nch):
    half = RET_DK // 2
    inv_freq = 1.0 / (10000.0 ** (jnp.arange(half, dtype=F32) / (half - 1)))
    pos = jnp.arange(nch * CH, dtype=F32) - PAD
    ang = pos[:, None] * inv_freq[None, :]
    return jnp.cos(ang), jnp.sin(ang)


def _rot(x, cos, sin):
    x1, x2 = x[:, :128], x[:, 128:]
    return jnp.concatenate([x1 * cos - x2 * sin, x1 * sin + x2 * cos], axis=1)


def _unrot(d, cos, sin):
    d1, d2 = d[:, :128], d[:, 128:]
    return jnp.concatenate([d1 * cos + d2 * sin, d2 * cos - d1 * sin], axis=1)


def _ret_decays(lg):
    r = _row_ids(CH, CH)
    cidx = _lane_ids(CH, CH)
    diff = (r - cidx).astype(F32)
    decay = jnp.where(r >= cidx, jnp.exp(lg * jnp.maximum(diff, 0.0)), 0.0)
    decay_t = jnp.where(cidx >= r, jnp.exp(lg * jnp.maximum(-diff, 0.0)), 0.0)
    idx = _row_ids(CH).astype(F32)
    zeta = jnp.exp(lg * (CH - 1.0 - idx))
    xi = jnp.exp(lg * (idx + 1.0))
    return decay, decay_t, zeta, xi


def _ret_fwd(u0, ycat, cos, sin, norm_g, B, nch):
    R = u0.shape[0]

    def body(u_ref, cos_ref, sin_ref, ng_ref, ycat_in, out_ref, opre_ref, rin_ref, Rst):
        c = pl.program_id(1)

        @pl.when(c == 0)
        def _():
            Rst[...] = jnp.zeros_like(Rst)

        cos_v, sin_v = cos_ref[...], sin_ref[...]
        for h in range(RET_HEADS):
            lg = _RET_LG[h]
            cols = slice(256 * h, 256 * h + 256)
            decay, _, zeta, xi = _ret_decays(lg)
            qr = _rot(u_ref[:, cols], cos_v, sin_v)
            kr = _rot(u_ref[:, 1024 + 256 * h:1024 + 256 * h + 256], cos_v, sin_v) * _RET_SCALE
            v = u_ref[:, 2048 + 256 * h:2048 + 256 * h + 256]
            gate = u_ref[:, 3072 + 256 * h:3072 + 256 * h + 256]
            Rh = Rst[h]
            rin_ref[h] = Rh
            inner = _dot(_dot_nt(qr, kr) * decay, v)
            cross = _dot(qr, Rh) * xi
            Rst[h] = math.exp(CH * lg) * Rh + _dot((kr * zeta).T, v)
            o = inner + cross
            opre_ref[:, cols] = o
            oc = o - jnp.mean(o, axis=-1, keepdims=True)
            rr = lax.rsqrt(jnp.mean(oc * oc, axis=-1, keepdims=True) + EPS)
            out_ref[:, cols] = (_silu(gate) * (oc * rr * ng_ref[:, cols])).astype(out_ref.dtype)

    return pl.pallas_call(
        body, name="ret_fwd", grid=(B, nch),
        in_specs=[pl.BlockSpec((CH, 4096), lambda b, c: (b * nch + c, 0)),
                  pl.BlockSpec((CH, 128), lambda b, c: (c, 0)), pl.BlockSpec((CH, 128), lambda b, c: (c, 0)),
                  pl.BlockSpec((1, 1024), lambda b, c: (0, 0)),
                  pl.BlockSpec(memory_space=pl.ANY)],
        out_specs=[pl.BlockSpec((CH, 1024), lambda b, c: (b * nch + c, 1)),
                   pl.BlockSpec((CH, 1024), lambda b, c: (b * nch + c, 0)),
                   pl.BlockSpec((None, None, RET_HEADS, 256, 256), lambda b, c: (b, c, 0, 0, 0))],
        out_shape=[jax.ShapeDtypeStruct(ycat.shape, ycat.dtype), jax.ShapeDtypeStruct((R, 1024), F32),
                   jax.ShapeDtypeStruct((B, nch, RET_HEADS, 256, 256), F32)],
        scratch_shapes=[pltpu.VMEM((RET_HEADS, 256, 256), F32)],
        input_output_aliases={4: 0},
        compiler_params=_cparams(("parallel", "arbitrary")),
    )(u0, cos, sin, norm_g, ycat)


def _ret_bwd(dycat, u0, opre, rin, cos, sin, norm_g, B, nch):
    R = u0.shape[0]

    def body(dy_ref, u_ref, opre_ref, rin_ref, cos_ref, sin_ref, ng_ref, du_ref, pg_ref, dR):
        @pl.when(pl.program_id(1) == 0)
        def _():
            dR[...] = jnp.zeros_like(dR)
            pg_ref[...] = jnp.zeros_like(pg_ref)

        cos_v, sin_v = cos_ref[...], sin_ref[...]
        for h in range(RET_HEADS):
            lg = _RET_LG[h]
            cols = slice(256 * h, 256 * h + 256)
            decay, decay_t, zeta, xi = _ret_decays(lg)
            qr = _rot(u_ref[:, cols], cos_v, sin_v)
            kr = _rot(u_ref[:, 1024 + 256 * h:1024 + 256 * h + 256], cos_v, sin_v) * _RET_SCALE
            v = u_ref[:, 2048 + 256 * h:2048 + 256 * h + 256]
            gate = u_ref[:, 3072 + 256 * h:3072 + 256 * h + 256]
            ng = ng_ref[:, cols]
            o = opre_ref[:, cols]
            oc = o - jnp.mean(o, axis=-1, keepdims=True)
            rr = lax.rsqrt(jnp.mean(oc * oc, axis=-1, keepdims=True) + EPS)
            ohat = oc * rr
            dout = dy_ref[:, cols]
            du_ref[:, 3072 + 256 * h:3072 + 256 * h + 256] = dout * (ohat * ng) * _dsilu(gate)
            don = dout * _silu(gate)
            pg_ref[0:1, cols] += jnp.sum(don * ohat, axis=0, keepdims=True)
            dohat = don * ng
            do = rr * (dohat - jnp.mean(dohat, axis=-1, keepdims=True)
                       - ohat * jnp.mean(dohat * ohat, axis=-1, keepdims=True))
            Rh = rin_ref[h]
            dRn = dR[h]
            sc_t = _dot_nt(kr, qr) * decay_t
            dv = _dot(sc_t, do) + _dot(kr * zeta, dRn)
            ds = _dot_nt(do, v) * decay
            ds_t = _dot_nt(v, do) * decay_t
            dox = do * xi
            dq = _dot(ds, kr) + _dot_nt(dox, Rh)
            dk = _dot(ds_t, qr) + zeta * _dot_nt(v, dRn)
            dR[h] = math.exp(CH * lg) * dRn + _dot(qr.T, dox)
            du_ref[:, cols] = _unrot(dq, cos_v, sin_v)
            du_ref[:, 1024 + 256 * h:1024 + 256 * h + 256] = _unrot(dk, cos_v, sin_v) * _RET_SCALE
            du_ref[:, 2048 + 256 * h:2048 + 256 * h + 256] = dv

    rmap = lambda b, c: (b * nch + nch - 1 - c, 0)
    return pl.pallas_call(
        body, name="ret_bwd", grid=(B, nch),
        in_specs=[pl.BlockSpec((CH, 1024), lambda b, c: (b * nch + nch - 1 - c, 1)),
                  pl.BlockSpec((CH, 4096), rmap), pl.BlockSpec((CH, 1024), rmap),
                  pl.BlockSpec((None, None, RET_HEADS, 256, 256), lambda b, c: (b, nch - 1 - c, 0, 0, 0)),
                  pl.BlockSpec((CH, 128), lambda b, c: (nch - 1 - c, 0)),
                  pl.BlockSpec((CH, 128), lambda b, c: (nch - 1 - c, 0)),
                  pl.BlockSpec((1, 1024), lambda b, c: (0, 0))],
        out_specs=[pl.BlockSpec((CH, 4096), rmap), pl.BlockSpec((None, 8, 1024), lambda b, c: (b, 0, 0))],
        out_shape=[jax.ShapeDtypeStruct((R, 4096), F32), jax.ShapeDtypeStruct((B, 8, 1024), F32)],
        scratch_shapes=[pltpu.VMEM((RET_HEADS, 256, 256), F32)],
        compiler_params=_cparams(("parallel", "arbitrary")),
    )(dycat, u0, opre, rin, cos, sin, norm_g)


_SB_SCALE = SB_HD ** -0.5


_SB_NB = 4


def _sb_valid(qb, kb, live):
    qpos = qb * CH + jnp.bitwise_and(_row_ids(2 * CH, CH), CH - 1)
    kpos = kb * CH + _lane_ids(2 * CH, CH)
    first = PAD + (1 - live) * (1 << 24)
    return jnp.logical_and(kpos < qpos, kpos >= first)


def _sb_softplus(z):
    return jnp.maximum(z, 0.0) + jnp.log(1.0 + jnp.exp(-jnp.abs(z)))


def _stack_heads(x):
    hm = _lane_ids(1, 128) < SB_HD
    return jnp.concatenate([jnp.where(hm, x, 0.0), jnp.where(hm, 0.0, x)], axis=0)


def _unstack_heads(x2):
    return jnp.where(_lane_ids(1, 128) < SB_HD, x2[:CH], x2[CH:])


def _sb_fwd(u1, B, nch):
    R = u1.shape[0]
    Pn = nch * CH

    def body(q_ref, k_ref, v_ref, out_ref, s_ref):
        qb = pl.program_id(2)
        q2 = _stack_heads(q_ref[...] * _SB_SCALE)
        mgt = (_row_ids(CH, CH) > _lane_ids(CH, CH)).astype(F32)

        def step(i, carry):
            out2, acc = carry
            blocks = []
            for t in range(_SB_NB):
                kb = qb - _SB_NB * i - t
                live = (kb >= 0).astype(jnp.int32)
                kbc = jnp.maximum(kb, 0)
                start = pl.multiple_of(kbc * CH, CH)
                valid = _sb_valid(qb, kbc, live)
                z = _dot_nt(q2, k_ref[pl.ds(start, CH), :])
                sp = _sb_softplus(z)
                lm = jnp.where(valid, -sp, 0.0)
                blocks.append((valid, z - sp, _dot_split(lm, mgt), jnp.sum(lm, axis=1, keepdims=True), start))
            for valid, ls, loc, rs, start in blocks:
                w = jnp.where(valid, jnp.exp(ls + loc + acc), 0.0)
                out2 = out2 + _dot(w, v_ref[pl.ds(start, CH), :])
                acc = acc + rs
            return out2, acc

        trips = (qb + _SB_NB) // _SB_NB
        out2, acc = lax.fori_loop(0, trips, step, (jnp.zeros((2 * CH, 128), F32), jnp.zeros((2 * CH, 1), F32)))
        out_ref[...] = _unstack_heads(out2).astype(out_ref.dtype)
        s_ref[...] = _unstack_heads(jnp.broadcast_to(acc, (2 * CH, 128)))

    qspec = lambda off: pl.BlockSpec((CH, 128), lambda b, hp, qb: (b * nch + qb, off + hp))
    kspec = lambda off: pl.BlockSpec((Pn, 128), lambda b, hp, qb: (b, off + hp))
    return pl.pallas_call(
        body, name="sb_fwd", grid=(B, SB_HEADS // 2, nch),
        in_specs=[qspec(0), kspec(8), kspec(16)],
        out_specs=[qspec(0), qspec(0)],
        out_shape=[jax.ShapeDtypeStruct((R, 2048), _MXU), jax.ShapeDtypeStruct((R, 1024), F32)],
        compiler_params=_cparams(("parallel", "parallel", "arbitrary")),
    )(u1, u1, u1)


def _sb_bwd(dycat, u1, stot, B, nch):
    R = u1.shape[0]
    Pn = nch * CH

    def body(q_ref, k_ref, v_ref, do_ref, s_ref, dq_ref, dk_ref, dv_ref):
        qb = pl.program_id(2)

        @pl.when(qb == 0)
        def _():
            dk_ref[...] = jnp.zeros_like(dk_ref)
            dv_ref[...] = jnp.zeros_like(dv_ref)

        q2 = _stack_heads(q_ref[...] * _SB_SCALE)
        do2 = _stack_heads(do_ref[...])
        stv = s_ref[...]
        lane = _lane_ids(1, 128)
        s2 = jnp.concatenate([jnp.sum(jnp.where(lane == 0, stv, 0.0), axis=1, keepdims=True),
                              jnp.sum(jnp.where(lane == SB_HD, stv, 0.0), axis=1, keepdims=True)], axis=0)
        rr = _row_ids(CH, CH)
        cc = _lane_ids(CH, CH)
        mle = (rr <= cc).astype(F32)
        mlt = (rr < cc).astype(F32)

        def step(i, carry):
            dq2, pacc, gacc = carry
            blocks = []
            for t in range(_SB_NB):
                kb = _SB_NB * i + t
                live = (kb <= qb).astype(jnp.int32)
                start = pl.multiple_of(jnp.minimum(kb, qb) * CH, CH)
                valid = _sb_valid(qb, jnp.minimum(kb, qb), live)
                z = _dot_nt(q2, k_ref[pl.ds(start, CH), :])
                sp = _sb_softplus(z)
                lm = jnp.where(valid, -sp, 0.0)
                blocks.append((valid, z - sp, _dot_split(lm, mle), jnp.sum(lm, axis=1, keepdims=True), start))
            stage = []
            for valid, ls, ploc, rs, start in blocks:
                w = jnp.where(valid, jnp.exp(ls + (s2 - (ploc + pacc))), 0.0)
                gg = _dot_nt(do2, v_ref[pl.ds(start, CH), :]) * w
                stage.append((valid, ls, w, gg, _dot_split(gg, mlt), jnp.sum(gg, axis=1, keepdims=True), start))
                pacc = pacc + rs
            for valid, ls, w, gg, gloc, gs, start in stage:
                sig = jnp.exp(ls)
                dz = jnp.where(valid, gg * (1.0 - sig) - (gloc + gacc) * sig, 0.0)
                dq2 = dq2 + _dot(dz, k_ref[pl.ds(start, CH), :])
                dk_ref[pl.ds(start, CH), :] += _dot_tn(dz, q2)
                dv_ref[pl.ds(start, CH), :] += _dot_tn(w, do2)
                gacc = gacc + gs
            return dq2, pacc, gacc

        zero = jnp.zeros((2 * CH, 1), F32)
        trips = (qb + _SB_NB) // _SB_NB
        dq2 = lax.fori_loop(0, trips, step, (jnp.zeros((2 * CH, 128), F32), zero, zero))[0]
        dq_ref[...] = _unstack_heads(dq2) * _SB_SCALE

    qspec = lambda off: pl.BlockSpec((CH, 128), lambda b, hp, qb: (b * nch + qb, off + hp))
    kspec = lambda off: pl.BlockSpec((Pn, 128), lambda b, hp, qb: (b, off + hp))
    full = jax.ShapeDtypeStruct((R, 1024), F32)
    return pl.pallas_call(
        body, name="sb_bwd", grid=(B, SB_HEADS // 2, nch),
        in_specs=[qspec(0), kspec(8), kspec(16), qspec(0), qspec(0)],
        out_specs=[qspec(0), kspec(0), kspec(0)],
        out_shape=[full, full, full],
        compiler_params=_cparams(("parallel", "parallel", "arbitrary")),
    )(u1, u1, u1, dycat, stot)


def _neg_expm1(x):
    series = -(x * (1.0 + x * (0.5 + x * (1.0 / 6.0 + x * (1.0 / 24.0)))))
    return jnp.where(x > -0.05, series, 1.0 - jnp.exp(x))


def _lru_gates(x, wa_ref, ba_ref, wx_ref, bx_ref, lam_ref):
    rs, is_ = [], []
    for n in range(LRU_BLOCKS):
        xb = x[:, 128 * n:128 * n + 128]
        rs.append(_dot(xb, wa_ref[n]))
        is_.append(_dot(xb, wx_ref[n]))
    r = _sigmoid(jnp.concatenate(rs, axis=1) + ba_ref[...])
    i = _sigmoid(jnp.concatenate(is_, axis=1) + bx_ref[...])
    sp = _softplus(-lam_ref[...])
    la = -LRU_C * r * sp
    a = jnp.exp(la)
    mult = jnp.sqrt(jnp.maximum(_neg_expm1(2.0 * la), 0.0))
    return r, i, sp, a, mult


def _lru_fwd(u1, ycat, conv_w, conv_b, wa, ba, wx, bx, lam, B, nch):
    R = u1.shape[0]

    def body(x_ref, xp_ref, gate_ref, cw_ref, cb_ref, wa_ref, ba_ref, wx_ref, bx_ref, lam_ref, ycat_in,
             out_ref, hs_ref, hc):
        c = pl.program_id(1)

        @pl.when(c == 0)
        def _():
            hc[...] = jnp.zeros_like(hc)

        x = _conv_pre(xp_ref[...], x_ref[...], cw_ref, cb_ref, 4)
        r, i, sp, a, mult = _lru_gates(x, wa_ref, ba_ref, wx_ref, bx_ref, lam_ref)
        b = jnp.where(_real_rows(c), mult * (i * x), 0.0)
        rows = _row_ids(CH)
        s = 1
        while s < CH:
            a_s = jnp.where(rows >= s, pltpu.roll(a, s, axis=0), 1.0)
            b_s = jnp.where(rows >= s, pltpu.roll(b, s, axis=0), 0.0)
            b = a * b_s + b
            a = a * a_s
            s *= 2
        h = a * hc[0:1, :] + b
        hs_ref[...] = h
        hc[0:1, :] = hs_ref[CH - 1:CH, :]
        out_ref[...] = (h * _gelu(gate_ref[...])).astype(out_ref.dtype)

    row = lambda col: pl.BlockSpec((CH, 1024), lambda b, c: (b * nch + c, col))
    vec = pl.BlockSpec((1, 1024), lambda b, c: (0, 0))
    wsp = pl.BlockSpec((LRU_BLOCKS, 128, 128), lambda b, c: (0, 0, 0))
    return pl.pallas_call(
        body, name="lru_fwd", grid=(B, nch),
        in_specs=[row(4), pl.BlockSpec((8, 1024), _prev8_map(nch, 4)), row(3),
                  pl.BlockSpec((4, 1024), lambda b, c: (0, 0)), vec, wsp, vec, wsp, vec, vec,
                  pl.BlockSpec(memory_space=pl.ANY)],
        out_specs=[row(1), row(0)],
        out_shape=[jax.ShapeDtypeStruct(ycat.shape, ycat.dtype), jax.ShapeDtypeStruct((R, 1024), F32)],
        scratch_shapes=[pltpu.VMEM((8, 1024), F32)],
        input_output_aliases={10: 0},
        compiler_params=_cparams(("parallel", "arbitrary")),
    )(u1, u1, u1, conv_w, conv_b, wa, ba, wx, bx, lam, ycat)


def _lru_bwd(dycat, u1, hs, conv_w, conv_b, wa, ba, wx, bx, lam, B, nch):
    R = u1.shape[0]

    def body(dy_ref, x_ref, xp_ref, gate_ref, hs_ref, hsp_ref, cw_ref, cb_ref, wa_ref, ba_ref, wx_ref, bx_ref, lam_ref,
             dgate_ref, dxc_ref, pg_ref, dwa_ref, dwx_ref, lc):
        c = nch - 1 - pl.program_id(1)

        @pl.when(pl.program_id(1) == 0)
        def _():
            lc[...] = jnp.zeros_like(lc)
            pg_ref[...] = jnp.zeros_like(pg_ref)
            dwa_ref[...] = jnp.zeros_like(dwa_ref)
            dwx_ref[...] = jnp.zeros_like(dwx_ref)

        x = _conv_pre(xp_ref[...], x_ref[...], cw_ref, cb_ref, 4)
        r, i, sp, a, mult = _lru_gates(x, wa_ref, ba_ref, wx_ref, bx_ref, lam_ref)
        h = hs_ref[...]
        hprev = _shift_down(hsp_ref[...], h, 1)
        gate = gate_ref[...]
        dy = dy_ref[...]
        dgate_ref[...] = dy * h * _dgelu(gate)
        rows = _row_ids(CH)
        lam_t = dy * _gelu(gate) + jnp.where(rows == CH - 1, lc[0:1, :], 0.0)
        coef = jnp.where(rows < CH - 1, pltpu.roll(a, CH - 1, axis=0), 0.0)
        s = 1
        while s < CH:
            c_s = jnp.where(rows < CH - s, pltpu.roll(coef, CH - s, axis=0), 1.0)
            l_s = jnp.where(rows < CH - s, pltpu.roll(lam_t, CH - s, axis=0), 0.0)
            lam_t = coef * l_s + lam_t
            coef = coef * c_s
            s *= 2
        lc[0:1, :] = jnp.sum(jnp.where(rows == 0, a * lam_t, 0.0), axis=0, keepdims=True)
        db = jnp.where(_real_rows(c), lam_t, 0.0)
        da = db * hprev
        dmult = db * (i * x)
        di = db * mult * x
        dx = db * mult * i
        pos = mult > 0.0
        dla = da * a + jnp.where(pos, -dmult * (a * a) / jnp.where(pos, mult, 1.0), 0.0)
        dr = dla * (-LRU_C * sp)
        pg_ref[2:3, :] += jnp.sum(dla * (LRU_C * r) * _sigmoid(-lam_ref[...]), axis=0, keepdims=True)
        dpr = dr * r * (1.0 - r)
        dpi = di * i * (1.0 - i)
        pg_ref[0:1, :] += jnp.sum(dpr, axis=0, keepdims=True)
        pg_ref[1:2, :] += jnp.sum(dpi, axis=0, keepdims=True)
        dxs = []
        for n in range(LRU_BLOCKS):
            blk = slice(128 * n, 128 * n + 128)
            dxs.append(dx[:, blk] + _dot_nt(dpr[:, blk], wa_ref[n]) + _dot_nt(dpi[:, blk], wx_ref[n]))
            dwa_ref[n] += _dot_tn(x[:, blk], dpr[:, blk])
            dwx_ref[n] += _dot_tn(x[:, blk], dpi[:, blk])
        dxc_ref[...] = jnp.concatenate(dxs, axis=1)

    rmap = lambda col: (lambda b, c: (b * nch + nch - 1 - c, col))
    row = lambda col: pl.BlockSpec((CH, 1024), rmap(col))
    prev = lambda col: pl.BlockSpec(
        (8, 1024), lambda b, c: (jnp.maximum((b * nch + nch - 1 - c) * (CH // 8) - 1, 0), col))
    vec = pl.BlockSpec((1, 1024), lambda b, c: (0, 0))
    wsp = pl.BlockSpec((LRU_BLOCKS, 128, 128), lambda b, c: (0, 0, 0))
    full = jax.ShapeDtypeStruct((R, 1024), F32)
    return pl.pallas_call(
        body, name="lru_bwd", grid=(B, nch),
        in_specs=[row(1), row(4), prev(4), row(3), row(0), prev(0),
                  pl.BlockSpec((4, 1024), lambda b, c: (0, 0)), vec, wsp, vec, wsp, vec, vec],
        out_specs=[row(0), row(0), pl.BlockSpec((None, 8, 1024), lambda b, c: (b, 0, 0)),
                   pl.BlockSpec((None, LRU_BLOCKS, 128, 128), lambda b, c: (b, 0, 0, 0)),
                   pl.BlockSpec((None, LRU_BLOCKS, 128, 128), lambda b, c: (b, 0, 0, 0))],
        out_shape=[full, full, jax.ShapeDtypeStruct((B, 8, 1024), F32),
                   jax.ShapeDtypeStruct((B, LRU_BLOCKS, 128, 128), F32),
                   jax.ShapeDtypeStruct((B, LRU_BLOCKS, 128, 128), F32)],
        scratch_shapes=[pltpu.VMEM((8, 1024), F32)],
        compiler_params=_cparams(("parallel", "arbitrary")),
    )(dycat, u1, u1, u1, hs, hs, conv_w, conv_b, wa, ba, wx, bx, lam)


_FFN_TC = FFN // 2


def _ffn_specs(nch):
    nt = FFN // _FFN_TC
    row = lambda off: pl.BlockSpec((CH, _FFN_TC), lambda b, c, j: (b * nch + c, off + j))
    prev = lambda off: pl.BlockSpec(
        (8, _FFN_TC), lambda b, c, j: (jnp.maximum((b * nch + c) * (CH // 8) - 1, 0), off + j))
    wsp = lambda off: pl.BlockSpec((3, _FFN_TC), lambda b, c, j: (0, off + j))
    bsp = lambda off: pl.BlockSpec((1, _FFN_TC), lambda b, c, j: (0, off + j))
    return nt, row, [row(0), prev(0), row(nt), prev(nt), wsp(0), wsp(nt), bsp(0), bsp(nt)]


def _ffn_act_fwd(uf, conv_w, conv_b, B, nch):
    R = uf.shape[0]
    nt, row, specs = _ffn_specs(nch)

    def body(g_ref, gp_ref, u_ref, up_ref, wg_ref, wu_ref, bg_ref, bu_ref, o_ref):
        cg = _conv_pre(gp_ref[...], g_ref[...], wg_ref, bg_ref, 3)
        cu = _conv_pre(up_ref[...], u_ref[...], wu_ref, bu_ref, 3)
        o_ref[...] = jnp.where(_real_rows(pl.program_id(1)), _silu(cg) * cu, 0.0).astype(o_ref.dtype)

    return pl.pallas_call(
        body, name="ffn_act_fwd", grid=(B, nch, nt),
        in_specs=specs, out_specs=row(0),
        out_shape=jax.ShapeDtypeStruct((R, FFN), _MXU),
        compiler_params=_cparams(("parallel", "parallel", "parallel")),
    )(uf, uf, uf, uf, conv_w, conv_w, conv_b, conv_b)


def _ffn_act_bwd(da, uf, conv_w, conv_b, B, nch):
    R = uf.shape[0]
    nt, row, specs = _ffn_specs(nch)

    def body(da_ref, g_ref, gp_ref, u_ref, up_ref, wg_ref, wu_ref, bg_ref, bu_ref, dg_ref, du_ref):
        cg = _conv_pre(gp_ref[...], g_ref[...], wg_ref, bg_ref, 3)
        cu = _conv_pre(up_ref[...], u_ref[...], wu_ref, bu_ref, 3)
        dav = jnp.where(_real_rows(pl.program_id(1)), da_ref[...], 0.0)
        dg_ref[...] = dav * cu * _dsilu(cg)
        du_ref[...] = dav * _silu(cg)

    full = jax.ShapeDtypeStruct((R, FFN), F32)
    return pl.pallas_call(
        body, name="ffn_act_bwd", grid=(B, nch, nt),
        in_specs=[row(0)] + specs, out_specs=[row(0), row(0)],
        out_shape=[full, full],
        compiler_params=_cparams(("parallel", "parallel", "parallel")),
    )(da, uf, uf, uf, uf, conv_w, conv_w, conv_b, conv_b)


def _head(h, g, target, B, nch):
    R = h.shape[0]

    def body(h_ref, g_ref, t_ref, dh_ref, loss_ref, dg_ref):
        c = pl.program_id(1)

        @pl.when(c == 0)
        def _():
            dh_ref[...] = jnp.zeros_like(dh_ref)
            loss_ref[...] = jnp.zeros_like(loss_ref)
            dg_ref[...] = jnp.zeros_like(dg_ref)

        @pl.when(c > 0)
        def _():
            x = h_ref[...]
            gv = g_ref[...]
            r = lax.rsqrt(jnp.mean(x * x, axis=-1, keepdims=True) + EPS)
            xhat = x * r
            e = xhat * gv - t_ref[...]
            loss_ref[...] += 0.5 * jnp.sum(jnp.mean(e * e, axis=-1, keepdims=True), axis=0, keepdims=True)
            dy = e * (1.0 / D)
            dg_ref[0:1, :] += jnp.sum(dy * xhat, axis=0, keepdims=True)
            dx = dy * gv
            dh_ref[...] = r * (dx - xhat * jnp.mean(dx * xhat, axis=-1, keepdims=True))

    row = pl.BlockSpec((CH, D), lambda b, c: (b * nch + c, 0))
    return pl.pallas_call(
        body, name="head", grid=(B, nch),
        in_specs=[row, pl.BlockSpec((1, D), lambda b, c: (0, 0)),
                  pl.BlockSpec((CH, D), lambda b, c: (b * (nch - 1) + jnp.maximum(c - 1, 0), 0))],
        out_specs=[row, pl.BlockSpec((None, 8, 128), lambda b, c: (b, 0, 0)),
                   pl.BlockSpec((None, 8, D), lambda b, c: (b, 0, 0))],
        out_shape=[jax.ShapeDtypeStruct((R, D), F32), jax.ShapeDtypeStruct((B, 8, 128), F32),
                   jax.ShapeDtypeStruct((B, 8, D), F32)],
        compiler_params=_cparams(("parallel", "arbitrary")),
    )(h, g, target)


ADAM_LR = 0.001
ADAM_B1 = 0.9
ADAM_B2 = 0.999
ADAM_EPS = 1e-08
ADAM_WD = 0.01
ADAM_STEP = 10


def _adamw(w, g, m, v, name):
    Rr, C = w.shape
    tr = _tile(Rr, (256, 64))

    def body(w_ref, g_ref, m_ref, v_ref, d_ref, nm_ref, nv_ref):
        gv = g_ref[...]
        nm = ADAM_B1 * m_ref[...] + (1.0 - ADAM_B1) * gv
        nv = ADAM_B2 * v_ref[...] + (1.0 - ADAM_B2) * (gv * gv)
        m_hat = nm / (1.0 - ADAM_B1 ** ADAM_STEP)
        v_hat = nv / (1.0 - ADAM_B2 ** ADAM_STEP)
        d_ref[...] = -ADAM_LR * (m_hat / (jnp.sqrt(v_hat) + ADAM_EPS) + ADAM_WD * w_ref[...])
        nm_ref[...] = nm
        nv_ref[...] = nv

    spec = pl.BlockSpec((tr, C), lambda i: (i, 0))
    sh = jax.ShapeDtypeStruct((Rr, C), F32)
    return pl.pallas_call(
        body, name=name, grid=(Rr // tr,),
        in_specs=[spec] * 4, out_specs=[spec] * 3, out_shape=[sh] * 3,
        compiler_params=_cparams(("parallel",)),
    )(w, g, m, v)


_MESH = pl.DeviceIdType.MESH
_ANY = pl.BlockSpec(memory_space=pl.ANY)


def _place():
    x, y, c = lax.axis_index("x"), lax.axis_index("y"), lax.axis_index("c")
    chips = [(1 - x, y), (x, 1 - y), (1 - x, 1 - y)]
    return x, y, c, chips


def _rcopy(src, dst, ssem, rsem, dev):
    return pltpu.make_async_remote_copy(src_ref=src, dst_ref=dst, send_sem=ssem, recv_sem=rsem,
                                        device_id=dev, device_id_type=_MESH)


def _gather_shards(bigs, small):
    nb = len(bigs)

    def body(*refs):
        ins, outs = refs[:nb + 1], refs[nb + 1:2 * nb + 2]
        ssem, rsem, fssem, frsem = refs[2 * nb + 2:]
        x, y, c, chips = _place()
        k = 2 * x + y
        sib = (x, y, 1 - c)

        def part(a, slot, hc):
            return outs[a].at[slot] if a == nb else outs[a].at[slot, hc]

        first = []
        for a in range(nb + 1):
            src = ins[a] if a == nb else ins[a].at[c]
            for j, (cx, cy) in enumerate(chips):
                first.append(_rcopy(src, part(a, k, c), ssem.at[3 * a + j], rsem.at[3 * a + j], (cx, cy, c)))
        for cp in first:
            cp.start()
        passed = []
        for a in range(nb + 1):
            for j, (cx, cy) in enumerate(chips):
                got = part(a, 2 * cx + cy, c)
                _rcopy(got, got, ssem.at[3 * a + j], rsem.at[3 * a + j], (cx, cy, c)).wait_recv()
                if a < nb:
                    fw = _rcopy(got, got, fssem.at[3 * a + j], frsem.at[3 * a + j], sib)
                    fw.start()
                    passed.append(fw)
        for a in range(nb):
            for j, (cx, cy) in enumerate(chips):
                got = part(a, 2 * cx + cy, 1 - c)
                _rcopy(got, got, fssem.at[3 * a + j], frsem.at[3 * a + j], sib).wait_recv()
        for cp in first + passed:
            cp.wait_send()

    arrs = list(bigs) + [small]
    n = 3 * (nb + 1)
    return pl.pallas_call(
        body, name="gather_shards",
        in_specs=[_ANY] * (nb + 1), out_specs=[_ANY] * (nb + 1),
        out_shape=[jax.ShapeDtypeStruct((4,) + a.shape, a.dtype) for a in arrs],
        scratch_shapes=[pltpu.SemaphoreType.DMA((n,)), pltpu.SemaphoreType.DMA((n,)),
                        pltpu.SemaphoreType.DMA((n,)), pltpu.SemaphoreType.DMA((n,))],
    )(*arrs)


def _swap_halves(grads):
    na = len(grads)
    halves = [g.shape[1] // 2 for g in grads]

    def body(*refs):
        ins, outs = refs[:na], refs[na:2 * na]
        ssem, rsem = refs[2 * na:]
        x, y, c, _ = _place()
        sib = (x, y, 1 - c)
        cps = [_rcopy(ins[a].at[:, pl.ds((1 - c) * halves[a], halves[a]), :], outs[a], ssem.at[a], rsem.at[a], sib)
               for a in range(na)]
        for cp in cps:
            cp.start()
        for cp in cps:
            cp.wait()

    return pl.pallas_call(
        body, name="swap_halves",
        in_specs=[_ANY] * na, out_specs=[_ANY] * na,
        out_shape=[jax.ShapeDtypeStruct((4, g.shape[1] // 2, g.shape[2]), g.dtype) for g in grads],
        scratch_shapes=[pltpu.SemaphoreType.DMA((na,)), pltpu.SemaphoreType.DMA((na,))],
    )(*grads)


def _sum_rows(rh):
    return rh if rh <= 512 else _tile(rh, (512, 256, 128, 64, 32))


def _chip_sum(grad, recv, core, name):
    _, r, cdim = grad.shape
    rh = r // 2
    tr = _sum_rows(rh)
    nblk = rh // tr

    def body(core_ref, g_ref, r_ref, o_ref):
        o_ref[...] = (g_ref[...] + r_ref[...]).astype(o_ref.dtype)

    return pl.pallas_call(
        body, name=name,
        grid_spec=pltpu.PrefetchScalarGridSpec(
            num_scalar_prefetch=1, grid=(4, nblk),
            in_specs=[pl.BlockSpec((None, tr, cdim), lambda s, i, cr: (s, cr[0] * nblk + i, 0)),
                      pl.BlockSpec((None, tr, cdim), lambda s, i, cr: (s, i, 0))],
            out_specs=pl.BlockSpec((None, tr, cdim), lambda s, i, cr: (s, i, 0))),
        out_shape=jax.ShapeDtypeStruct((4, rh, cdim), BF16),
        compiler_params=_cparams(("parallel", "parallel")),
    )(core, grad, recv)


def _scatter_sums(sums):
    na = len(sums)

    def body(*refs):
        ins, outs = refs[:na], refs[na:2 * na]
        ssem, rsem, lsem = refs[2 * na:]
        x, y, c, chips = _place()
        k = 2 * x + y
        local = [pltpu.make_async_copy(ins[a].at[k], outs[a].at[k], lsem.at[a]) for a in range(na)]
        for cp in local:
            cp.start()
        cps = []
        for a in range(na):
            for j, (cx, cy) in enumerate(chips):
                cps.append(_rcopy(ins[a].at[2 * cx + cy], outs[a].at[k], ssem.at[3 * a + j], rsem.at[3 * a + j],
                                  (cx, cy, c)))
        for cp in cps:
            cp.start()
        for a in range(na):
            for j, (cx, cy) in enumerate(chips):
                got = outs[a].at[2 * cx + cy]
                _rcopy(got, got, ssem.at[3 * a + j], rsem.at[3 * a + j], (cx, cy, c)).wait_recv()
        for cp in cps:
            cp.wait_send()
        for cp in local:
            cp.wait()

    return pl.pallas_call(
        body, name="scatter_sums",
        in_specs=[_ANY] * na, out_specs=[_ANY] * na,
        out_shape=[jax.ShapeDtypeStruct(s.shape, s.dtype) for s in sums],
        scratch_shapes=[pltpu.SemaphoreType.DMA((3 * na,)), pltpu.SemaphoreType.DMA((3 * na,)),
                        pltpu.SemaphoreType.DMA((na,))],
    )(*sums)


def _sum_chips(parts, name):
    _, rh, cdim = parts.shape
    tr = _sum_rows(rh)

    def body(p_ref, o_ref):
        acc = p_ref[0].astype(F32)
        for j in range(1, 4):
            acc = acc + p_ref[j].astype(F32)
        o_ref[...] = acc

    return pl.pallas_call(
        body, name=name, grid=(rh // tr,),
        in_specs=[pl.BlockSpec((4, tr, cdim), lambda i: (0, i, 0))],
        out_specs=pl.BlockSpec((tr, cdim), lambda i: (i, 0)),
        out_shape=jax.ShapeDtypeStruct((rh, cdim), F32),
        compiler_params=_cparams(("parallel",)),
    )(parts)


def _join_halves(reds):
    na = len(reds)

    def body(*refs):
        ins, outs = refs[:na], refs[na:2 * na]
        ssem, rsem = refs[2 * na:]
        x, y, c, _ = _place()
        cps = [_rcopy(ins[a], outs[a], ssem.at[a], rsem.at[a], (x, y, 1 - c)) for a in range(na)]
        for cp in cps:
            cp.start()
        for cp in cps:
            cp.wait()

    return pl.pallas_call(
        body, name="join_halves",
        in_specs=[_ANY] * na, out_specs=[_ANY] * na,
        out_shape=[jax.ShapeDtypeStruct(r.shape, r.dtype) for r in reds],
        scratch_shapes=[pltpu.SemaphoreType.DMA((na,)), pltpu.SemaphoreType.DMA((na,))],
    )(*reds)


def _allreduce_small(buf):
    n = buf.shape[0]

    def body(in_ref, out_ref, recv, ssem, rsem):
        x, y, c, _ = _place()
        peers = [(x, y, 1 - c), (1 - x, y, c), (x, 1 - y, c)]
        out_ref[...] = in_ref[...]
        for r, peer in enumerate(peers):
            cp = _rcopy(out_ref, recv.at[r], ssem.at[r], rsem.at[r], peer)
            cp.start()
            cp.wait()
            out_ref[...] = out_ref[...] + recv[r]

    vm = pl.BlockSpec(memory_space=pltpu.VMEM)
    return pl.pallas_call(
        body, name="allreduce_small",
        in_specs=[vm], out_specs=vm,
        out_shape=jax.ShapeDtypeStruct(buf.shape, F32),
        scratch_shapes=[pltpu.VMEM((3, n, 128), F32), pltpu.SemaphoreType.DMA((3,)), pltpu.SemaphoreType.DMA((3,))],
        compiler_params=pltpu.CompilerParams(vmem_limit_bytes=VMEM_LIMIT),
    )(buf)


_W_NAMES = ['meta_tokens', 'l0_mix_norm', 'l0_w_in', 'l0_ssd_conv_w', 'l0_ssd_conv_b', 'l0_ssd_dt_bias', 'l0_ssd_a_log',
            'l0_ssd_d', 'l0_ssd_norm', 'l0_ret_norm', 'l0_w_out', 'l0_ffn_norm', 'l0_ffn_w_in', 'l0_ffn_conv_w',
            'l0_ffn_conv_b', 'l0_ffn_w_out', 'l1_mix_norm', 'l1_w_in', 'l1_lru_conv_w', 'l1_lru_conv_b', 'l1_lru_wa',
            'l1_lru_ba', 'l1_lru_wx', 'l1_lru_bx', 'l1_lru_lambda', 'l1_w_out', 'l1_ffn_norm', 'l1_ffn_w_in',
            'l1_ffn_conv_w', 'l1_ffn_conv_b', 'l1_ffn_w_out', 'final_norm']
_IN_NAMES = ['x'] + _W_NAMES + ['loss_target'] + ['m_' + n for n in _W_NAMES] + ['v_' + n for n in _W_NAMES]
_BIG = ['l0_w_in', 'l0_w_out', 'l0_ffn_w_in', 'l0_ffn_w_out', 'l1_w_in', 'l1_w_out', 'l1_ffn_w_in', 'l1_ffn_w_out']
_BIG_COLS = ('l0_w_in', 'l0_ffn_w_in', 'l1_w_in', 'l1_ffn_w_in')
_SMALL_SHARDED = ['meta_tokens', 'l0_ssd_conv_w', 'l0_ffn_conv_w', 'l1_lru_conv_w', 'l1_ffn_conv_w']
_SMALL = [n for n in _W_NAMES if n not in _BIG]


def _pack(arrs):
    flat = []
    for a in arrs:
        v = a.reshape(-1).astype(F32)
        flat.append(jnp.pad(v, (0, (-v.shape[0]) % 128)))
    v = jnp.concatenate(flat)
    v = jnp.pad(v, (0, (-v.shape[0]) % 1024))
    return v.reshape(-1, 128)


def _unpack(buf, shapes):
    out, row = [], 0
    for sh in shapes:
        n = int(np.prod(sh))
        rows = -(-n // 128)
        out.append(buf[row:row + rows].reshape(-1)[:n].reshape(sh))
        row += rows
    return out


def kernel(x, meta_tokens, l0_mix_norm, l0_w_in, l0_ssd_conv_w, l0_ssd_conv_b, l0_ssd_dt_bias, l0_ssd_a_log, l0_ssd_d, l0_ssd_norm, l0_ret_norm, l0_w_out, l0_ffn_norm, l0_ffn_w_in, l0_ffn_conv_w, l0_ffn_conv_b, l0_ffn_w_out, l1_mix_norm, l1_w_in, l1_lru_conv_w, l1_lru_conv_b, l1_lru_wa, l1_lru_ba, l1_lru_wx, l1_lru_bx, l1_lru_lambda, l1_w_out, l1_ffn_norm, l1_ffn_w_in, l1_ffn_conv_w, l1_ffn_conv_b, l1_ffn_w_out, final_norm, loss_target, m_meta_tokens, m_l0_mix_norm, m_l0_w_in, m_l0_ssd_conv_w, m_l0_ssd_conv_b, m_l0_ssd_dt_bias, m_l0_ssd_a_log, m_l0_ssd_d, m_l0_ssd_norm, m_l0_ret_norm, m_l0_w_out, m_l0_ffn_norm, m_l0_ffn_w_in, m_l0_ffn_conv_w, m_l0_ffn_conv_b, m_l0_ffn_w_out, m_l1_mix_norm, m_l1_w_in, m_l1_lru_conv_w, m_l1_lru_conv_b, m_l1_lru_wa, m_l1_lru_ba, m_l1_lru_wx, m_l1_lru_bx, m_l1_lru_lambda, m_l1_w_out, m_l1_ffn_norm, m_l1_ffn_w_in, m_l1_ffn_conv_w, m_l1_ffn_conv_b, m_l1_ffn_w_out, m_final_norm, v_meta_tokens, v_l0_mix_norm, v_l0_w_in, v_l0_ssd_conv_w, v_l0_ssd_conv_b, v_l0_ssd_dt_bias, v_l0_ssd_a_log, v_l0_ssd_d, v_l0_ssd_norm, v_l0_ret_norm, v_l0_w_out, v_l0_ffn_norm, v_l0_ffn_w_in, v_l0_ffn_conv_w, v_l0_ffn_conv_b, v_l0_ffn_w_out, v_l1_mix_norm, v_l1_w_in, v_l1_lru_conv_w, v_l1_lru_conv_b, v_l1_lru_wa, v_l1_lru_ba, v_l1_lru_wx, v_l1_lru_bx, v_l1_lru_lambda, v_l1_w_out, v_l1_ffn_norm, v_l1_ffn_w_in, v_l1_ffn_conv_w, v_l1_ffn_conv_b, v_l1_ffn_w_out, v_final_norm):
    args = (x, meta_tokens, l0_mix_norm, l0_w_in, l0_ssd_conv_w, l0_ssd_conv_b, l0_ssd_dt_bias, l0_ssd_a_log, l0_ssd_d, l0_ssd_norm, l0_ret_norm, l0_w_out, l0_ffn_norm, l0_ffn_w_in, l0_ffn_conv_w, l0_ffn_conv_b, l0_ffn_w_out, l1_mix_norm, l1_w_in, l1_lru_conv_w, l1_lru_conv_b, l1_lru_wa, l1_lru_ba, l1_lru_wx, l1_lru_bx, l1_lru_lambda, l1_w_out, l1_ffn_norm, l1_ffn_w_in, l1_ffn_conv_w, l1_ffn_conv_b, l1_ffn_w_out, final_norm, loss_target, m_meta_tokens, m_l0_mix_norm, m_l0_w_in, m_l0_ssd_conv_w, m_l0_ssd_conv_b, m_l0_ssd_dt_bias, m_l0_ssd_a_log, m_l0_ssd_d, m_l0_ssd_norm, m_l0_ret_norm, m_l0_w_out, m_l0_ffn_norm, m_l0_ffn_w_in, m_l0_ffn_conv_w, m_l0_ffn_conv_b, m_l0_ffn_w_out, m_l1_mix_norm, m_l1_w_in, m_l1_lru_conv_w, m_l1_lru_conv_b, m_l1_lru_wa, m_l1_lru_ba, m_l1_lru_wx, m_l1_lru_bx, m_l1_lru_lambda, m_l1_w_out, m_l1_ffn_norm, m_l1_ffn_w_in, m_l1_ffn_conv_w, m_l1_ffn_conv_b, m_l1_ffn_w_out, m_final_norm, v_meta_tokens, v_l0_mix_norm, v_l0_w_in, v_l0_ssd_conv_w, v_l0_ssd_conv_b, v_l0_ssd_dt_bias, v_l0_ssd_a_log, v_l0_ssd_d, v_l0_ssd_norm, v_l0_ret_norm, v_l0_w_out, v_l0_ffn_norm, v_l0_ffn_w_in, v_l0_ffn_conv_w, v_l0_ffn_conv_b, v_l0_ffn_w_out, v_l1_mix_norm, v_l1_w_in, v_l1_lru_conv_w, v_l1_lru_conv_b, v_l1_lru_wa, v_l1_lru_ba, v_l1_lru_wx, v_l1_lru_bx, v_l1_lru_lambda, v_l1_w_out, v_l1_ffn_norm, v_l1_ffn_w_in, v_l1_ffn_conv_w, v_l1_ffn_conv_b, v_l1_ffn_w_out, v_final_norm)
    p = dict(zip(_IN_NAMES, args))
    B, seq, _ = x.shape
    nch = (seq + CH) // CH
    Pn = nch * CH
    R = B * Pn
    chip = 2 * lax.axis_index("x") + lax.axis_index("y")
    row2 = lambda v: v.reshape(1, -1)
    pad128 = lambda v: jnp.pad(v, (0, 128 - v.shape[0])).reshape(1, 128)

    small_shapes = [p[n].shape for n in _SMALL_SHARDED]
    halved = lambda w: w.astype(_MXU).reshape(2, w.shape[0] // 2, w.shape[1])
    mine = [halved(p[n]) for n in _BIG] + [_pack([p[n] for n in _SMALL_SHARDED])]
    gathered = [lax.dynamic_update_index_in_dim(g, own, chip, 0) for g, own in zip(_gather_shards(mine[:-1], mine[-1]), mine)]
    g_big, g_small = gathered[:-1], gathered[-1]
    W = {}
    for n, g in zip(_BIG, g_big):
        g = g.reshape(4, -1, g.shape[3])
        W[n] = jnp.concatenate([g[k] for k in range(4)], axis=1) if n in _BIG_COLS else g.reshape(-1, g.shape[2])
    per_chip = [_unpack(g_small[k], small_shapes) for k in range(4)]
    for i, n in enumerate(_SMALL_SHARDED):
        W[n] = jnp.concatenate([per_chip[k][i] for k in range(4)], axis=1)
    w0 = W['l0_w_in']
    w0_main = jnp.concatenate([w0[:, 3088:], w0[:, :3072]], axis=1)
    w0_dt = jnp.pad(w0[:, 3072:3088], ((0, 0), (0, 112)))
    cos, sin = _rope_tables(nch)

    meta = jnp.broadcast_to(W['meta_tokens'][None], (B, N_META, D))
    h0 = jnp.concatenate([jnp.zeros((B, PAD, D), F32), meta, x], axis=1).reshape(R, D)
    n0 = _rmsnorm_fwd(h0, row2(p['l0_mix_norm']), "norm_l0_mix")
    u0 = _mm(n0, w0_main, "nn", F32, "l0_in_proj")
    udt = _mm(n0, w0_dt, "nn", F32, "l0_dt_proj")
    a_log, d_skip, dt_bias = pad128(p['l0_ssd_a_log']), pad128(p['l0_ssd_d']), pad128(p['l0_ssd_dt_bias'])
    ssd_cb = row2(p['l0_ssd_conv_b'])
    act, dt, dtt = _ssd_prep(u0, udt, W['l0_ssd_conv_w'], ssd_cb, dt_bias, B, nch)
    ycat0, ypre, hin = _ssd_fwd(act, u0, dt, dtt, a_log, d_skip, row2(p['l0_ssd_norm']), B, nch)
    ycat0, opre, rin = _ret_fwd(u0, ycat0, cos, sin, row2(p['l0_ret_norm']), B, nch)
    h1 = _mm(ycat0, W['l0_w_out'], "nn", F32, "l0_out_proj", add=h0)
    n1 = _rmsnorm_fwd(h1, row2(p['l0_ffn_norm']), "norm_l0_ffn")
    uf0 = _mm(n1, W['l0_ffn_w_in'], "nn", F32, "l0_ffn_in")
    f0_cb = row2(p['l0_ffn_conv_b'])
    a0 = _ffn_act_fwd(uf0, W['l0_ffn_conv_w'], f0_cb, B, nch)
    h2 = _mm(a0, W['l0_ffn_w_out'], "nn", F32, "l0_ffn_out", add=h1)
    n2 = _rmsnorm_fwd(h2, row2(p['l1_mix_norm']), "norm_l1_mix")
    u1 = _mm(n2, W['l1_w_in'], "nn", F32, "l1_in_proj")
    lru = (W['l1_lru_conv_w'], row2(p['l1_lru_conv_b']), p['l1_lru_wa'], row2(p['l1_lru_ba']), p['l1_lru_wx'],
           row2(p['l1_lru_bx']), row2(p['l1_lru_lambda']))
    ycat1, stot = _sb_fwd(u1, B, nch)
    ycat1, hs = _lru_fwd(u1, ycat1, *lru, B, nch)
    h3 = _mm(ycat1, W['l1_w_out'], "nn", F32, "l1_out_proj", add=h2)
    n3 = _rmsnorm_fwd(h3, row2(p['l1_ffn_norm']), "norm_l1_ffn")
    uf1 = _mm(n3, W['l1_ffn_w_in'], "nn", F32, "l1_ffn_in")
    f1_cb = row2(p['l1_ffn_conv_b'])
    a1 = _ffn_act_fwd(uf1, W['l1_ffn_conv_w'], f1_cb, B, nch)
    h4 = _mm(a1, W['l1_ffn_w_out'], "nn", F32, "l1_ffn_out", add=h3)
    dh4, lossp, dgf = _head(h4, row2(p['final_norm']), p['loss_target'].reshape(B * seq, D), B, nch)
    loss = lax.psum(jnp.sum(lossp[:, 0, 0]), ("x", "y", "c"))

    G = {'final_norm': dgf[:, 0].sum(0)}

    def ffn_bwd(layer, dh_out, h_in, n_in, uf, a_act, cb):
        pre = f"l{layer}_"
        w_in, w_out, cw = W[pre + 'ffn_w_in'], W[pre + 'ffn_w_out'], W[pre + 'ffn_conv_w']
        da = _mm(dh_out, w_out, "nt", F32, pre + "ffn_out_dgrad")
        G[pre + 'ffn_w_out'] = _mm(a_act, dh_out, "tn", F32, pre + "ffn_out_wgrad")
        dcg, dcu = _ffn_act_bwd(da, uf, cw, cb, B, nch)
        dug, dwg = _conv_bwd(dcg, uf, 0, cw[:, :FFN], 3, pre + "ffn_conv_bwd_g", tc=_FFN_TC)
        duu, dwu = _conv_bwd(dcu, uf, FFN, cw[:, FFN:], 3, pre + "ffn_conv_bwd_u", tc=_FFN_TC)
        G[pre + 'ffn_conv_w'] = jnp.concatenate([dwg[:3], dwu[:3]], axis=1)
        G[pre + 'ffn_conv_b'] = jnp.concatenate([dwg[7], dwu[7]])
        dn = _mm(dug, w_in, "nt", F32, pre + "ffn_in_dgrad_g")
        dn = _mm(duu, w_in, "nt", F32, pre + "ffn_in_dgrad_u", add=dn, b_off=FFN)
        G[pre + 'ffn_w_in'] = jnp.concatenate([_mm(n_in, dug, "tn", F32, pre + "ffn_in_wgrad_g"),
                                               _mm(n_in, duu, "tn", F32, pre + "ffn_in_wgrad_u")], axis=1)
        dh_in, dg = _rmsnorm_bwd(h_in, row2(p[pre + 'ffn_norm']), dn, dh_out, nch, pre + "ffn_norm_bwd")
        G[pre + 'ffn_norm'] = dg[0]
        return dh_in

    dh3 = ffn_bwd(1, dh4, h3, n3, uf1, a1, f1_cb)
    dy1 = _mm(dh3, W['l1_w_out'], "nt", F32, "l1_out_dgrad")
    G['l1_w_out'] = _mm(ycat1, dh3, "tn", F32, "l1_out_wgrad")
    dq, dk, dv = _sb_bwd(dy1, u1, stot, B, nch)
    dgate, dxc, pgl, dwa, dwx = _lru_bwd(dy1, u1, hs, *lru, B, nch)
    dxr, dcw = _conv_bwd(dxc, u1, 4096, W['l1_lru_conv_w'], 4, "l1_lru_conv_bwd")
    pgl = pgl.sum(0)
    G['l1_lru_ba'], G['l1_lru_bx'], G['l1_lru_lambda'] = pgl[0], pgl[1], pgl[2]
    G['l1_lru_wa'], G['l1_lru_wx'] = dwa.sum(0), dwx.sum(0)
    G['l1_lru_conv_w'], G['l1_lru_conv_b'] = dcw[:4], dcw[7]
    dn, dws = None, []
    for i, piece in enumerate((dq, dk, dv, dgate, dxr)):
        dn = _mm(piece, W['l1_w_in'], "nt", F32, f"l1_in_dgrad_{i}", add=dn, b_off=1024 * i)
        dws.append(_mm(n2, piece, "tn", F32, f"l1_in_wgrad_{i}"))
    G['l1_w_in'] = jnp.concatenate(dws, axis=1)
    dh2, dg = _rmsnorm_bwd(h2, row2(p['l1_mix_norm']), dn, dh3, nch, "l1_mix_norm_bwd")
    G['l1_mix_norm'] = dg[0]

    dh1 = ffn_bwd(0, dh2, h1, n1, uf0, a0, f0_cb)
    dy0 = _mm(dh1, W['l0_w_out'], "nt", F32, "l0_out_dgrad")
    G['l0_w_out'] = _mm(ycat0, dh1, "tn", F32, "l0_out_wgrad")
    dz, dxs, dbm, dcm, ddt4, pgs = _ssd_bwd(dy0, ypre, u0, act, dt, dtt, hin, a_log, d_skip, row2(p['l0_ssd_norm']), B, nch)
    dpre, ddtr, pgd = _ssd_prep_bwd(dxs, dbm, dcm, ddt4, u0, udt, W['l0_ssd_conv_w'], ssd_cb, dt_bias, B, nch)
    dxbc, dcw0 = _conv_bwd(dpre, u0, U0_XBC, W['l0_ssd_conv_w'], 4, "l0_ssd_conv_bwd")
    dqkvg, pgr = _ret_bwd(dy0, u0, opre, rin, cos, sin, row2(p['l0_ret_norm']), B, nch)
    pgs = pgs.sum(0)
    G['l0_ssd_norm'] = pgs[:, 0, :].reshape(-1)
    G['l0_ssd_d'] = pgs[:, 1, :128].sum(0)[:SSD_HEADS]
    G['l0_ssd_a_log'] = pgs[:, 2, :128].sum(0)[:SSD_HEADS]
    G['l0_ssd_dt_bias'] = pgd.sum(0)[0, :SSD_HEADS]
    G['l0_ssd_conv_w'], G['l0_ssd_conv_b'] = dcw0[:4], dcw0[7]
    G['l0_ret_norm'] = pgr.sum(0)[0]
    dn = _mm(dqkvg, w0_main, "nt", F32, "l0_in_dgrad_qkvg")
    dn = _mm(dz, w0_main, "nt", F32, "l0_in_dgrad_z", add=dn, b_off=U0_Z)
    dn = _mm(dxbc, w0_main, "nt", F32, "l0_in_dgrad_xbc", add=dn, b_off=U0_XBC)
    dn = _mm(ddtr, w0_dt, "nt", F32, "l0_in_dgrad_dt", add=dn)
    G['l0_w_in'] = jnp.concatenate([
        _mm(n0, dz, "tn", F32, "l0_in_wgrad_z"), _mm(n0, dxbc, "tn", F32, "l0_in_wgrad_xbc"),
        _mm(n0, ddtr, "tn", F32, "l0_in_wgrad_dt")[:, :SSD_HEADS], _mm(n0, dqkvg, "tn", F32, "l0_in_wgrad_qkvg")], axis=1)
    dh0, dg = _rmsnorm_bwd(h0, row2(p['l0_mix_norm']), dn, dh1, nch, "l0_mix_norm_bwd")
    G['l0_mix_norm'] = dg[0]
    dh0 = dh0.reshape(B, Pn, D)
    grad_x = dh0[:, CH:]
    G['meta_tokens'] = dh0[:, PAD:CH].sum(0)

    core = lax.axis_index("c").reshape(1).astype(jnp.int32)
    stacked = []
    for n in _BIG:
        g = G[n]
        if n in _BIG_COLS:
            stacked.append(g.reshape(g.shape[0], 4, g.shape[1] // 4).transpose(1, 0, 2))
        else:
            stacked.append(g.reshape(4, g.shape[0] // 4, g.shape[1]))
    theirs = _swap_halves(stacked)
    sums = [_chip_sum(g, t, core, "chip_sum_" + n) for n, g, t in zip(_BIG, stacked, theirs)]
    parts = _scatter_sums(sums)
    reds = [_sum_chips(q, "sum_chips_" + n) for n, q in zip(_BIG, parts)]
    grads = {}
    for n, own, other in zip(_BIG, reds, _join_halves(reds)):
        both = jnp.where(core[0] == 0, jnp.stack([own, other]), jnp.stack([other, own]))
        grads[n] = both.reshape(-1, both.shape[2])
    small_full = _unpack(_allreduce_small(_pack([G[n] for n in _SMALL])), [G[n].shape for n in _SMALL])
    for n, g in zip(_SMALL, small_full):
        if n in _SMALL_SHARDED:
            cs = g.shape[1] // 4
            g = lax.dynamic_slice_in_dim(g, chip * cs, cs, axis=1)
        grads[n] = g.reshape(p[n].shape)

    delta, new_m, new_v = {}, {}, {}
    for n in _BIG:
        delta[n], new_m[n], new_v[n] = _adamw(p[n], grads[n], p['m_' + n], p['v_' + n], "adamw_" + n)
    shapes = [p[n].shape for n in _SMALL]
    outs = _adamw(_pack([p[n] for n in _SMALL]), _pack([grads[n] for n in _SMALL]), _pack([p['m_' + n] for n in _SMALL]),
                  _pack([p['v_' + n] for n in _SMALL]), "adamw_small")
    for dst, buf in zip((delta, new_m, new_v), outs):
        for n, a in zip(_SMALL, _unpack(buf, shapes)):
            dst[n] = a
    return (loss, grad_x, *[grads[n] for n in _W_NAMES], *[delta[n] for n in _W_NAMES],
            *[new_m[n] for n in _W_NAMES], *[new_v[n] for n in _W_NAMES])
```

```python
import math

import numpy as np
import jax
import jax.numpy as jnp
from jax import lax
from jax.experimental import pallas as pl
from jax.experimental.pallas import tpu as pltpu

F32 = jnp.float32
BF16 = jnp.bfloat16
_MXU = jnp.bfloat16

D = 1024
CH = 128
N_META = 16
PAD = CH - N_META
EPS = 1e-6

SSD_HEADS = 16
SSD_HD = 64
SSD_GROUPS = 4
RET_HEADS = 4
RET_DK = 256
SB_HEADS = 16
SB_HD = 64
LRU_BLOCKS = 8
LRU_C = 8.0
FFN = 2816
U0_Z = 4096
U0_XBC = 5120

VMEM_LIMIT = 56 * 1024 * 1024


def _cparams(sem):
    return pltpu.CompilerParams(dimension_semantics=sem, vmem_limit_bytes=VMEM_LIMIT)


def _dot(a, b, dims=((1,), (0,))):
    return lax.dot_general(a.astype(_MXU), b.astype(_MXU), (dims, ((), ())), preferred_element_type=F32)


def _dot_nt(a, b):
    return _dot(a, b, ((1,), (1,)))


def _dot_tn(a, b):
    return _dot(a.T, b)


def _dot_exact(a, b):
    return lax.dot_general(a, b, (((1,), (0,)), ((), ())), preferred_element_type=F32,
                           precision=lax.Precision.HIGHEST)


def _dot_split(x, m01):
    hi = x.astype(BF16)
    lo = (x - hi.astype(F32)).astype(BF16)
    m = m01.astype(BF16)
    return jnp.dot(hi, m, preferred_element_type=F32) + jnp.dot(lo, m, preferred_element_type=F32)


def _sigmoid(x):
    return jax.nn.sigmoid(x)


def _softplus(x):
    return jnp.maximum(x, 0.0) + jnp.log1p(jnp.exp(-jnp.abs(x)))


def _silu(x):
    return x * _sigmoid(x)


def _dsilu(x):
    s = _sigmoid(x)
    return s * (1.0 + x * (1.0 - s))


_GELU_C = math.sqrt(2.0 / math.pi)


def _gelu(x):
    return 0.5 * x * (1.0 + jnp.tanh(_GELU_C * (x + 0.044715 * x * x * x)))


def _dgelu(x):
    t = jnp.tanh(_GELU_C * (x + 0.044715 * x * x * x))
    return 0.5 * (1.0 + t) + 0.5 * x * (1.0 - t * t) * _GELU_C * (1.0 + 3.0 * 0.044715 * x * x)


def _row_ids(n, cols=1):
    return lax.broadcasted_iota(jnp.int32, (n, cols), 0)


def _lane_ids(rows, n):
    return lax.broadcasted_iota(jnp.int32, (rows, n), 1)


def _real_rows(chunk):
    return chunk * CH + _row_ids(CH) >= PAD


def _shift_down(prev8, cur, s):
    cat = jnp.concatenate([prev8, cur], axis=0)
    return pltpu.roll(cat, s, axis=0)[8:]


def _shift_up(cur, next8, s):
    n = cur.shape[0]
    cat = jnp.concatenate([cur, next8], axis=0)
    return pltpu.roll(cat, n + 8 - s, axis=0)[:n]


def _conv_pre(prev8, cur, w_ref, b_ref, K):
    acc = cur * w_ref[K - 1:K, :] + b_ref[...]
    for s in range(1, K):
        acc = acc + _shift_down(prev8, cur, s) * w_ref[K - 1 - s:K - s, :]
    return acc


def _prev8_map(nch, col):
    return lambda b, c: (jnp.maximum((b * nch + c) * (CH // 8) - 1, 0), col)


def _matmul(a, b, mode, out_dtype, tm, tn, tk, name, add=None, b_off=0):
    if mode == "nn":
        (M, K), (_, N) = a.shape, b.shape
    elif mode == "nt":
        (M, K), N = a.shape, b.shape[0]
    else:
        (K, M), (_, N) = a.shape, b.shape
    tm, tn, tk = min(tm, M), min(tn, N), min(tk, K)
    assert M % tm == 0 and N % tn == 0 and K % tk == 0 and b_off % tk == 0, (name, M, N, K, tm, tn, tk)
    koff = b_off // tk
    nk = K // tk
    dims = {"nn": ((1,), (0,)), "nt": ((1,), (1,)), "tn": ((0,), (0,))}[mode]
    if mode == "tn":
        a_spec = pl.BlockSpec((tk, tm), lambda i, j, k: (k, i))
    else:
        a_spec = pl.BlockSpec((tm, tk), lambda i, j, k: (i, k))
    if mode == "nt":
        b_spec = pl.BlockSpec((tn, tk), lambda i, j, k: (j, k + koff))
    else:
        b_spec = pl.BlockSpec((tk, tn), lambda i, j, k: (k, j))
    o_spec = pl.BlockSpec((tm, tn), lambda i, j, k: (i, j))
    has_add = add is not None

    def body(a_ref, b_ref, *rest):
        if has_add:
            add_ref, o_ref, acc = rest
        else:
            o_ref, acc = rest
        k = pl.program_id(2)

        @pl.when(k == 0)
        def _():
            acc[...] = jnp.zeros_like(acc)

        acc[...] += _dot(a_ref[...], b_ref[...], dims)

        @pl.when(k == nk - 1)
        def _():
            r = acc[...]
            if has_add:
                r = r + add_ref[...].astype(F32)
            o_ref[...] = r.astype(out_dtype)

    in_specs = [a_spec, b_spec] + ([o_spec] if has_add else [])
    args = (a, b) + ((add,) if has_add else ())
    return pl.pallas_call(
        body, name=name, grid=(M // tm, N // tn, nk),
        in_specs=in_specs, out_specs=o_spec,
        out_shape=jax.ShapeDtypeStruct((M, N), out_dtype),
        scratch_shapes=[pltpu.VMEM((tm, tn), F32)],
        compiler_params=_cparams(("parallel", "parallel", "arbitrary")),
    )(*args)


def _tile(n, prefs):
    for t in prefs:
        if n % t == 0:
            return t
    return n


def _mm(a, b, mode, out_dtype, name, add=None, b_off=0):
    if mode == "tn":
        K, M = a.shape
        N = b.shape[1]
        tm, tn, tk = _tile(M, (1024, 1408, 512, 256, 128)), _tile(N, (1024, 512, 256, 128)), _tile(K, (2176, 384, 256, 128))
    else:
        M, K = a.shape
        N = b.shape[1] if mode == "nn" else b.shape[0]
        tm = _tile(M, (1088, 1024, 768, 512, 384, 256, 128))
        tn = _tile(N, (1024, 512, 256, 128))
        tk = _tile(K, (2176, 1024, 1408, 512, 256, 128))
    return _matmul(a, b, mode, out_dtype, tm, tn, tk, name, add=add, b_off=b_off)


def _rmsnorm_fwd(h, g, name):
    R = h.shape[0]
    tr = 2 * CH

    def body(h_ref, g_ref, o_ref, ot_ref):
        x = h_ref[...]
        r = lax.rsqrt(jnp.mean(x * x, axis=-1, keepdims=True) + EPS)
        y = x * r * g_ref[...]
        o_ref[...] = y.astype(o_ref.dtype)
        ot_ref[...] = y.T.astype(ot_ref.dtype)

    return pl.pallas_call(
        body, name=name, grid=(R // tr,),
        in_specs=[pl.BlockSpec((tr, D), lambda i: (i, 0)), pl.BlockSpec((1, D), lambda i: (0, 0))],
        out_specs=[pl.BlockSpec((tr, D), lambda i: (i, 0)), pl.BlockSpec((D, tr), lambda i: (0, i))],
        out_shape=[jax.ShapeDtypeStruct((R, D), _MXU), jax.ShapeDtypeStruct((D, R), _MXU)],
        compiler_params=_cparams(("parallel",)),
    )(h, g)


def _rmsnorm_bwd(h, g, dn, dres, nch, name):
    R = h.shape[0]

    def body(h_ref, g_ref, dn_ref, dres_ref, dh_ref, dg_ref):
        i = pl.program_id(0)
        x = h_ref[...]
        r = lax.rsqrt(jnp.mean(x * x, axis=-1, keepdims=True) + EPS)
        xhat = x * r
        dn_v = dn_ref[...]
        dx = dn_v * g_ref[...]
        dh = r * (dx - xhat * jnp.mean(dx * xhat, axis=-1, keepdims=True))
        dh_ref[...] = jnp.where(_real_rows(i % nch), dres_ref[...] + dh, 0.0)

        @pl.when(i == 0)
        def _():
            dg_ref[...] = jnp.zeros_like(dg_ref)

        dg_ref[...] += jnp.sum(dn_v * xhat, axis=0, keepdims=True)

    row = pl.BlockSpec((CH, D), lambda i: (i, 0))
    vec = pl.BlockSpec((1, D), lambda i: (0, 0))
    return pl.pallas_call(
        body, name=name, grid=(R // CH,),
        in_specs=[row, vec, row, row], out_specs=[row, vec],
        out_shape=[jax.ShapeDtypeStruct((R, D), F32), jax.ShapeDtypeStruct((1, D), F32)],
        compiler_params=_cparams(("arbitrary",)),
    )(h, g, dn, dres)


def _ssd_prep(u0, udt, conv_w, conv_b, dt_bias, B, nch):
    R = u0.shape[0]

    def body(xs_ref, xsp_ref, bc_ref, bcp_ref, udt_ref, w0_ref, w1_ref, b0_ref, b1_ref, dtb_ref,
             act_ref, dt_ref, dtt_ref):
        keep = _real_rows(pl.program_id(1))
        a0 = _silu(_conv_pre(xsp_ref[...], xs_ref[...], w0_ref, b0_ref, 4))
        a1 = _silu(_conv_pre(bcp_ref[...], bc_ref[...], w1_ref, b1_ref, 4))
        act_ref[:, :1024] = jnp.where(keep, a0, 0.0)
        act_ref[:, 1024:] = jnp.where(keep, a1, 0.0)
        ok = jnp.logical_and(keep, _lane_ids(1, 128) < SSD_HEADS)
        dt = jnp.where(ok, _softplus(udt_ref[...] + dtb_ref[...]), 0.0)
        dt_ref[...] = dt
        dtt_ref[...] = dt.T

    row = lambda col: pl.BlockSpec((CH, 1024), lambda b, c: (b * nch + c, col))
    prev = lambda col: pl.BlockSpec((8, 1024), _prev8_map(nch, col))
    return pl.pallas_call(
        body, name="ssd_prep", grid=(B, nch),
        in_specs=[row(5), prev(5), row(6), prev(6),
                  pl.BlockSpec((CH, 128), lambda b, c: (b * nch + c, 0)),
                  pl.BlockSpec((4, 1024), lambda b, c: (0, 0)), pl.BlockSpec((4, 1024), lambda b, c: (0, 1)),
                  pl.BlockSpec((1, 1024), lambda b, c: (0, 0)), pl.BlockSpec((1, 1024), lambda b, c: (0, 1)),
                  pl.BlockSpec((1, 128), lambda b, c: (0, 0))],
        out_specs=[pl.BlockSpec((CH, 2048), lambda b, c: (b * nch + c, 0)),
                   pl.BlockSpec((CH, 128), lambda b, c: (b * nch + c, 0)),
                   pl.BlockSpec((128, CH), lambda b, c: (0, b * nch + c))],
        out_shape=[jax.ShapeDtypeStruct((R, 2048), F32), jax.ShapeDtypeStruct((R, 128), F32),
                   jax.ShapeDtypeStruct((128, R), F32)],
        compiler_params=_cparams(("parallel", "parallel")),
    )(u0, u0, u0, u0, udt, conv_w, conv_w, conv_b, conv_b, dt_bias)


def _ssd_head_terms(h, a_vec, dt_v, dtt_v, dsk_v):
    lane = _lane_ids(1, 128)
    sub = _row_ids(128)
    r = _row_ids(CH, CH)
    cidx = _lane_ids(CH, CH)
    a_h = jnp.sum(jnp.where(lane == h, a_vec, 0.0), axis=1, keepdims=True)
    dt_col = jnp.sum(jnp.where(lane == h, dt_v, 0.0), axis=1, keepdims=True)
    dt_row = jnp.sum(jnp.where(sub == h, dtt_v, 0.0), axis=0, keepdims=True)
    cs_col = jnp.sum(jnp.where(r >= cidx, dt_row * a_h, 0.0), axis=1, keepdims=True)
    cs_row = jnp.sum(jnp.where(r <= cidx, dt_col * a_h, 0.0), axis=0, keepdims=True)
    tot = jnp.sum(dt_col * a_h, axis=0, keepdims=True)
    dsk = jnp.sum(jnp.where(lane == h, dsk_v, 0.0), axis=1, keepdims=True)
    return a_h, dt_col, cs_col, cs_row, tot, dsk


def _ssd_fwd(act, u0, dt, dtt, a_log, d_skip, norm_g, B, nch):
    R = act.shape[0]

    def body(xs_ref, bm_ref, cm_ref, z_ref, dt_ref, dtt_ref, alog_ref, dsk_ref, ng_ref,
             out_ref, ypre_ref, hin_ref, H):
        g = pl.program_id(1)
        c = pl.program_id(2)

        @pl.when(c == 0)
        def _():
            H[...] = jnp.zeros_like(H)

        hin_ref[...] = H[...]
        a_vec = -jnp.exp(alog_ref[...])
        dt_v = dt_ref[...]
        dtt_v = dtt_ref[...]
        hm = _lane_ids(1, 128) < SSD_HD
        r = _row_ids(CH, CH)
        cidx = _lane_ids(CH, CH)
        Bm = bm_ref[...]
        Cm = cm_ref[...]
        CB = _dot_nt(Cm, Bm)
        ys = []
        for pair in range(2):
            cols = slice(128 * pair, 128 * pair + 128)
            xraw = xs_ref[:, cols]
            t = [_ssd_head_terms(4 * g + 2 * pair + j, a_vec, dt_v, dtt_v, dsk_ref[...]) for j in range(2)]
            sel = lambda f: jnp.where(hm, f(t[0]), f(t[1]))
            dtp = sel(lambda q: q[1])
            Ep = sel(lambda q: jnp.exp(q[2]))
            Wp = sel(lambda q: jnp.exp(q[4] - q[2]))
            etot = sel(lambda q: jnp.exp(q[4]))
            dsk = sel(lambda q: q[5])
            X = xraw * dtp
            ydiag = jnp.zeros((CH, 128), F32)
            for j in range(2):
                Lm = jnp.where(r >= cidx, jnp.exp(t[j][2] - t[j][3]), 0.0)
                Xh = jnp.where(hm if j == 0 else jnp.logical_not(hm), X, 0.0)
                ydiag = ydiag + _dot(CB * Lm, Xh)
            Hp = H[:, cols]
            yoff = Ep * _dot(Cm, Hp)
            S = _dot(Bm.T, X * Wp)
            H[:, cols] = etot * Hp + S
            ys.append(ydiag + yoff + xraw * dsk)
        y = jnp.concatenate(ys, axis=1)
        ypre_ref[...] = y
        yg = y * _silu(z_ref[...])
        rr = lax.rsqrt(jnp.mean(yg * yg, axis=-1, keepdims=True) + EPS)
        out_ref[...] = jnp.where(_real_rows(c), yg * rr * ng_ref[...], 0.0).astype(out_ref.dtype)

    rowb = lambda w, colf: pl.BlockSpec((CH, w), lambda b, g, c: (b * nch + c, colf(g)))
    vec = pl.BlockSpec((1, 128), lambda b, g, c: (0, 0))
    return pl.pallas_call(
        body, name="ssd_fwd", grid=(B, SSD_GROUPS, nch),
        in_specs=[rowb(256, lambda g: g), rowb(128, lambda g: 8 + g), rowb(128, lambda g: 12 + g),
                  rowb(256, lambda g: 16 + g), rowb(128, lambda g: 0),
                  pl.BlockSpec((128, CH), lambda b, g, c: (0, b * nch + c)),
                  vec, vec, pl.BlockSpec((1, 256), lambda b, g, c: (0, g))],
        out_specs=[rowb(256, lambda g: g), rowb(256, lambda g: g),
                   pl.BlockSpec((None, None, None, 128, 256), lambda b, g, c: (b, g, c, 0, 0))],
        out_shape=[jax.ShapeDtypeStruct((R, 2048), _MXU), jax.ShapeDtypeStruct((R, 1024), F32),
                   jax.ShapeDtypeStruct((B, SSD_GROUPS, nch, 128, 256), F32)],
        scratch_shapes=[pltpu.VMEM((128, 256), F32)],
        compiler_params=_cparams(("parallel", "parallel", "arbitrary")),
    )(act, act, act, u0, dt, dtt, a_log, d_skip, norm_g)


def _ssd_bwd(dycat, ypre, u0, act, dt, dtt, hin, a_log, d_skip, norm_g, B, nch):
    R = act.shape[0]

    def body(dy_ref, ypre_ref, z_ref, xs_ref, bm_ref, cm_ref, dt_ref, dtt_ref, hin_ref, alog_ref, dsk_ref, ng_ref,
             dz_ref, dxs_ref, db_ref, dc_ref, ddt_ref, pg_ref, dH):
        g = pl.program_id(1)
        c = nch - 1 - pl.program_id(2)

        @pl.when(pl.program_id(2) == 0)
        def _():
            dH[...] = jnp.zeros_like(dH)
            pg_ref[...] = jnp.zeros_like(pg_ref)

        z = z_ref[...]
        y = ypre_ref[...]
        ng = ng_ref[...]
        dout = jnp.where(_real_rows(c), dy_ref[...], 0.0)
        sz = _sigmoid(z)
        yg = y * z * sz
        rr = lax.rsqrt(jnp.mean(yg * yg, axis=-1, keepdims=True) + EPS)
        nrm = yg * rr
        pg_ref[0:1, :] += jnp.sum(dout * nrm, axis=0, keepdims=True)
        dn = dout * ng
        dyg = rr * (dn - nrm * jnp.mean(dn * nrm, axis=-1, keepdims=True))
        dy = dyg * z * sz
        dz_ref[...] = (dyg * y * (sz * (1.0 + z * (1.0 - sz)))).astype(dz_ref.dtype)

        a_vec = -jnp.exp(alog_ref[...])
        dt_v = dt_ref[...]
        dtt_v = dtt_ref[...]
        lane = _lane_ids(1, 128)
        hm = lane < SSD_HD
        r = _row_ids(CH, CH)
        cidx = _lane_ids(CH, CH)
        last = _row_ids(CH) == CH - 1
        Bm = bm_ref[...]
        Cm = cm_ref[...]
        CB = _dot_nt(Cm, Bm)
        CBT = _dot_nt(Bm, Cm)
        dB = jnp.zeros((CH, 128), F32)
        dC = jnp.zeros((CH, 128), F32)
        dcs_all = jnp.zeros((CH, 128), F32)
        dtx_all = jnp.zeros((CH, 128), F32)
        dd_row = jnp.zeros((1, 128), F32)
        dxs = []
        for pair in range(2):
            cols = slice(128 * pair, 128 * pair + 128)
            xraw = xs_ref[:, cols]
            dyp = dy[:, cols]
            heads = [4 * g + 2 * pair + j for j in range(2)]
            t = [_ssd_head_terms(heads[j], a_vec, dt_v, dtt_v, dsk_ref[...]) for j in range(2)]
            sel = lambda f: jnp.where(hm, f(t[0]), f(t[1]))
            hsum = lambda v, j: jnp.sum(jnp.where(hm if j == 0 else jnp.logical_not(hm), v, 0.0), axis=1, keepdims=True)
            dtp = sel(lambda q: q[1])
            Ep = sel(lambda q: jnp.exp(q[2]))
            Wp = sel(lambda q: jnp.exp(q[4] - q[2]))
            etot = sel(lambda q: jnp.exp(q[4]))
            dsk = sel(lambda q: q[5])
            X = xraw * dtp
            Hp = hin_ref[:, cols]
            dHn = dH[:, cols]
            dskip = jnp.sum(dyp * xraw, axis=0, keepdims=True)
            yoff = Ep * _dot(Cm, Hp)
            dE = dyp * yoff
            dC = dC + _dot_nt(dyp * Ep, Hp)
            dH[:, cols] = etot * dHn + _dot(Cm.T, dyp * Ep)
            BdS = _dot(Bm, dHn)
            dX = Wp * BdS
            ew = X * BdS * Wp
            dB = dB + _dot_nt(X * Wp, dHn)
            hh = jnp.sum(dHn * Hp, axis=0, keepdims=True) * etot
            for j in range(2):
                hmask = hm if j == 0 else jnp.logical_not(hm)
                cs_col, cs_row = t[j][2], t[j][3]
                Lm = jnp.where(r >= cidx, jnp.exp(cs_col - cs_row), 0.0)
                LmT = jnp.where(cidx >= r, jnp.exp(cs_row - cs_col), 0.0)
                dyh = jnp.where(hmask, dyp, 0.0)
                Xh = jnp.where(hmask, X, 0.0)
                dM = _dot_nt(dyh, Xh)
                dMT = _dot_nt(Xh, dyh)
                M = CB * Lm
                MT = CBT * LmT
                dX = dX + _dot(MT, dyh)
                dC = dC + _dot(dM * Lm, Bm)
                dB = dB + _dot(dMT * LmT, Cm)
                g_rows = jnp.sum(dM * M, axis=1, keepdims=True)
                g_cols = jnp.sum(dMT * MT, axis=1, keepdims=True)
                dtot = (jnp.sum(hsum(ew, j), axis=0, keepdims=True)
                        + jnp.sum(jnp.where(hmask, hh, 0.0), axis=1, keepdims=True))
                dcs = g_rows - g_cols + hsum(dE, j) - hsum(ew, j) + jnp.where(last, dtot, 0.0)
                dcs_all = dcs_all + jnp.where(lane == heads[j], dcs, 0.0)
                dtx_all = dtx_all + jnp.where(lane == heads[j], hsum(dX * xraw, j), 0.0)
                dd_row = dd_row + jnp.where(lane == heads[j],
                                            jnp.sum(jnp.where(hmask, dskip, 0.0), axis=1, keepdims=True), 0.0)
            dxs.append(dX * dtp + dyp * dsk)
        dxs_ref[...] = jnp.concatenate(dxs, axis=1)
        db_ref[...] = dB
        dc_ref[...] = dC
        dadt = _dot_exact(jnp.where(cidx >= r, 1.0, 0.0), dcs_all)
        ddt_ref[...] = dadt * a_vec + dtx_all
        pg_ref[1:2, 0:128] += dd_row
        pg_ref[2:3, 0:128] += jnp.sum(dadt * dt_v, axis=0, keepdims=True) * a_vec

    rowb = lambda w, colf: pl.BlockSpec((CH, w), lambda b, g, c: (b * nch + nch - 1 - c, colf(g)))
    vec = pl.BlockSpec((1, 128), lambda b, g, c: (0, 0))
    return pl.pallas_call(
        body, name="ssd_bwd", grid=(B, SSD_GROUPS, nch),
        in_specs=[rowb(256, lambda g: g), rowb(256, lambda g: g), rowb(256, lambda g: 16 + g), rowb(256, lambda g: g),
                  rowb(128, lambda g: 8 + g), rowb(128, lambda g: 12 + g), rowb(128, lambda g: 0),
                  pl.BlockSpec((128, CH), lambda b, g, c: (0, b * nch + nch - 1 - c)),
                  pl.BlockSpec((None, None, None, 128, 256), lambda b, g, c: (b, g, nch - 1 - c, 0, 0)),
                  vec, vec, pl.BlockSpec((1, 256), lambda b, g, c: (0, g))],
        out_specs=[rowb(256, lambda g: g), rowb(256, lambda g: g), rowb(128, lambda g: g), rowb(128, lambda g: g),
                   rowb(128, lambda g: g),
                   pl.BlockSpec((None, None, 8, 256), lambda b, g, c: (b, g, 0, 0))],
        out_shape=[jax.ShapeDtypeStruct((R, 1024), _MXU), jax.ShapeDtypeStruct((R, 1024), F32),
                   jax.ShapeDtypeStruct((R, 512), F32), jax.ShapeDtypeStruct((R, 512), F32),
                   jax.ShapeDtypeStruct((R, 512), F32), jax.ShapeDtypeStruct((B, SSD_GROUPS, 8, 256), F32)],
        scratch_shapes=[pltpu.VMEM((128, 256), F32)],
        compiler_params=_cparams(("parallel", "parallel", "arbitrary")),
    )(dycat, ypre, u0, act, act, act, dt, dtt, hin, a_log, d_skip, norm_g)


def _ssd_prep_bwd(dxs, dB, dC, ddt4, u0, udt, conv_w, conv_b, dt_bias, B, nch):
    R = u0.shape[0]

    def body(dxs_ref, db_ref, dc_ref, ddt_ref, xs_ref, xsp_ref, bc_ref, bcp_ref, udt_ref, w0_ref, w1_ref, b0_ref, b1_ref,
             dtb_ref, dpre_ref, ddtr_ref, pgd_ref):
        c = pl.program_id(1)

        @pl.when(c == 0)
        def _():
            pgd_ref[...] = jnp.zeros_like(pgd_ref)

        keep = _real_rows(c)
        p0 = _conv_pre(xsp_ref[...], xs_ref[...], w0_ref, b0_ref, 4)
        p1 = _conv_pre(bcp_ref[...], bc_ref[...], w1_ref, b1_ref, 4)
        dpre_ref[:, :1024] = jnp.where(keep, dxs_ref[...] * _dsilu(p0), 0.0)
        dpre_ref[:, 1024:] = jnp.where(keep, jnp.concatenate([db_ref[...], dc_ref[...]], axis=1) * _dsilu(p1), 0.0)
        ddt = ddt_ref[:, 0:128] + ddt_ref[:, 128:256] + ddt_ref[:, 256:384] + ddt_ref[:, 384:512]
        ok = jnp.logical_and(keep, _lane_ids(1, 128) < SSD_HEADS)
        dr = jnp.where(ok, ddt * _sigmoid(udt_ref[...] + dtb_ref[...]), 0.0)
        ddtr_ref[...] = dr
        pgd_ref[0:1, :] += jnp.sum(dr, axis=0, keepdims=True)

    rw = lambda w: pl.BlockSpec((CH, w), lambda b, c: (b * nch + c, 0))
    row = lambda col: pl.BlockSpec((CH, 1024), lambda b, c: (b * nch + c, col))
    prev = lambda col: pl.BlockSpec((8, 1024), _prev8_map(nch, col))
    return pl.pallas_call(
        body, name="ssd_prep_bwd", grid=(B, nch),
        in_specs=[rw(1024), rw(512), rw(512), rw(512), row(5), prev(5), row(6), prev(6), rw(128),
                  pl.BlockSpec((4, 1024), lambda b, c: (0, 0)), pl.BlockSpec((4, 1024), lambda b, c: (0, 1)),
                  pl.BlockSpec((1, 1024), lambda b, c: (0, 0)), pl.BlockSpec((1, 1024), lambda b, c: (0, 1)),
                  pl.BlockSpec((1, 128), lambda b, c: (0, 0))],
        out_specs=[rw(2048), rw(128), pl.BlockSpec((None, 8, 128), lambda b, c: (b, 0, 0))],
        out_shape=[jax.ShapeDtypeStruct((R, 2048), F32), jax.ShapeDtypeStruct((R, 128), F32),
                   jax.ShapeDtypeStruct((B, 8, 128), F32)],
        compiler_params=_cparams(("parallel", "arbitrary")),
    )(dxs, dB, dC, ddt4, u0, u0, u0, u0, udt, conv_w, conv_w, conv_b, conv_b, dt_bias)


def _conv_bwd(dpre, xin, xin_col, w, K, name, tc=1024):
    R, C = dpre.shape
    assert C % tc == 0 and xin_col % tc == 0
    nr = R // CH
    xoff = xin_col // tc

    def body(dp_ref, dpn_ref, x_ref, xp_ref, w_ref, din_ref, dw_ref):
        i = pl.program_id(1)

        @pl.when(i == 0)
        def _():
            dw_ref[...] = jnp.zeros_like(dw_ref)

        dp = dp_ref[...]
        nxt = dpn_ref[...] * (i < nr - 1).astype(F32)
        x = x_ref[...]
        xp = xp_ref[...]
        din = dp * w_ref[K - 1:K, :]
        dw_ref[K - 1:K, :] += jnp.sum(dp * x, axis=0, keepdims=True)
        dw_ref[7:8, :] += jnp.sum(dp, axis=0, keepdims=True)
        for s in range(1, K):
            din = din + _shift_up(dp, nxt, s) * w_ref[K - 1 - s:K - s, :]
            dw_ref[K - 1 - s:K - s, :] += jnp.sum(dp * _shift_down(xp, x, s), axis=0, keepdims=True)
        din_ref[...] = din.astype(din_ref.dtype)

    return pl.pallas_call(
        body, name=name, grid=(C // tc, nr),
        in_specs=[pl.BlockSpec((CH, tc), lambda j, i: (i, j)),
                  pl.BlockSpec((8, tc), lambda j, i: (jnp.minimum((i + 1) * (CH // 8), nr * (CH // 8) - 1), j)),
                  pl.BlockSpec((CH, tc), lambda j, i: (i, xoff + j)),
                  pl.BlockSpec((8, tc), lambda j, i: (jnp.maximum(i * (CH // 8) - 1, 0), xoff + j)),
                  pl.BlockSpec((K, tc), lambda j, i: (0, j))],
        out_specs=[pl.BlockSpec((CH, tc), lambda j, i: (i, j)),
                   pl.BlockSpec((8, tc), lambda j, i: (0, j))],
        out_shape=[jax.ShapeDtypeStruct((R, C), _MXU), jax.ShapeDtypeStruct((8, C), F32)],
        compiler_params=_cparams(("parallel", "arbitrary")),
    )(dpre, dpre, xin, xin, w)


_RET_LG = [float(v) for v in np.log1p(-np.exp2(-5.0 - np.arange(RET_HEADS, dtype=np.float32))).astype(np.float32)]
_RET_SCALE = RET_DK ** -0.5


def _rope_tables(nch):
    half = RET_DK // 2
    inv_freq = 1.0 / (10000.0 ** (jnp.arange(half, dtype=F32) / (half - 1)))
    pos = jnp.arange(nch * CH, dtype=F32) - PAD
    ang = pos[:, None] * inv_freq[None, :]
    return jnp.cos(ang), jnp.sin(ang)


def _rot(x, cos, sin):
    x1, x2 = x[:, :128], x[:, 128:]
    return jnp.concatenate([x1 * cos - x2 * sin, x1 * sin + x2 * cos], axis=1)


def _unrot(d, cos, sin):
    d1, d2 = d[:, :128], d[:, 128:]
    return jnp.concatenate([d1 * cos + d2 * sin, d2 * cos - d1 * sin], axis=1)


def _ret_decays(lg):
    r = _row_ids(CH, CH)
    cidx = _lane_ids(CH, CH)
    diff = (r - cidx).astype(F32)
    decay = jnp.where(r >= cidx, jnp.exp(lg * jnp.maximum(diff, 0.0)), 0.0)
    decay_t = jnp.where(cidx >= r, jnp.exp(lg * jnp.maximum(-diff, 0.0)), 0.0)
    idx = _row_ids(CH).astype(F32)
    zeta = jnp.exp(lg * (CH - 1.0 - idx))
    xi = jnp.exp(lg * (idx + 1.0))
    return decay, decay_t, zeta, xi


def _ret_fwd(u0, ycat, cos, sin, norm_g, B, nch):
    R = u0.shape[0]

    def body(u_ref, cos_ref, sin_ref, ng_ref, ycat_in, out_ref, opre_ref, rin_ref, Rst):
        c = pl.program_id(1)

        @pl.when(c == 0)
        def _():
            Rst[...] = jnp.zeros_like(Rst)

        cos_v, sin_v = cos_ref[...], sin_ref[...]
        for h in range(RET_HEADS):
            lg = _RET_LG[h]
            cols = slice(256 * h, 256 * h + 256)
            decay, _, zeta, xi = _ret_decays(lg)
            qr = _rot(u_ref[:, cols], cos_v, sin_v)
            kr = _rot(u_ref[:, 1024 + 256 * h:1024 + 256 * h + 256], cos_v, sin_v) * _RET_SCALE
            v = u_ref[:, 2048 + 256 * h:2048 + 256 * h + 256]
            gate = u_ref[:, 3072 + 256 * h:3072 + 256 * h + 256]
            Rh = Rst[h]
            rin_ref[h] = Rh
            inner = _dot(_dot_nt(qr, kr) * decay, v)
            cross = _dot(qr, Rh) * xi
            Rst[h] = math.exp(CH * lg) * Rh + _dot((kr * zeta).T, v)
            o = inner + cross
            opre_ref[:, cols] = o
            oc = o - jnp.mean(o, axis=-1, keepdims=True)
            rr = lax.rsqrt(jnp.mean(oc * oc, axis=-1, keepdims=True) + EPS)
            out_ref[:, cols] = (_silu(gate) * (oc * rr * ng_ref[:, cols])).astype(out_ref.dtype)

    return pl.pallas_call(
        body, name="ret_fwd", grid=(B, nch),
        in_specs=[pl.BlockSpec((CH, 4096), lambda b, c: (b * nch + c, 0)),
                  pl.BlockSpec((CH, 128), lambda b, c: (c, 0)), pl.BlockSpec((CH, 128), lambda b, c: (c, 0)),
                  pl.BlockSpec((1, 1024), lambda b, c: (0, 0)),
                  pl.BlockSpec(memory_space=pl.ANY)],
        out_specs=[pl.BlockSpec((CH, 1024), lambda b, c: (b * nch + c, 1)),
                   pl.BlockSpec((CH, 1024), lambda b, c: (b * nch + c, 0)),
                   pl.BlockSpec((None, None, RET_HEADS, 256, 256), lambda b, c: (b, c, 0, 0, 0))],
        out_shape=[jax.ShapeDtypeStruct(ycat.shape, ycat.dtype), jax.ShapeDtypeStruct((R, 1024), F32),
                   jax.ShapeDtypeStruct((B, nch, RET_HEADS, 256, 256), F32)],
        scratch_shapes=[pltpu.VMEM((RET_HEADS, 256, 256), F32)],
        input_output_aliases={4: 0},
        compiler_params=_cparams(("parallel", "arbitrary")),
    )(u0, cos, sin, norm_g, ycat)


def _ret_bwd(dycat, u0, opre, rin, cos, sin, norm_g, B, nch):
    R = u0.shape[0]

    def body(dy_ref, u_ref, opre_ref, rin_ref, cos_ref, sin_ref, ng_ref, du_ref, pg_ref, dR):
        @pl.when(pl.program_id(1) == 0)
        def _():
            dR[...] = jnp.zeros_like(dR)
            pg_ref[...] = jnp.zeros_like(pg_ref)

        cos_v, sin_v = cos_ref[...], sin_ref[...]
        for h in range(RET_HEADS):
            lg = _RET_LG[h]
            cols = slice(256 * h, 256 * h + 256)
            decay, decay_t, zeta, xi = _ret_decays(lg)
            qr = _rot(u_ref[:, cols], cos_v, sin_v)
            kr = _rot(u_ref[:, 1024 + 256 * h:1024 + 256 * h + 256], cos_v, sin_v) * _RET_SCALE
            v = u_ref[:, 2048 + 256 * h:2048 + 256 * h + 256]
            gate = u_ref[:, 3072 + 256 * h:3072 + 256 * h + 256]
            ng = ng_ref[:, cols]
            o = opre_ref[:, cols]
            oc = o - jnp.mean(o, axis=-1, keepdims=True)
            rr = lax.rsqrt(jnp.mean(oc * oc, axis=-1, keepdims=True) + EPS)
            ohat = oc * rr
            dout = dy_ref[:, cols]
            du_ref[:, 3072 + 256 * h:3072 + 256 * h + 256] = (dout * (ohat * ng) * _dsilu(gate)).astype(du_ref.dtype)
            don = dout * _silu(gate)
            pg_ref[0:1, cols] += jnp.sum(don * ohat, axis=0, keepdims=True)
            dohat = don * ng
            do = rr * (dohat - jnp.mean(dohat, axis=-1, keepdims=True)
                       - ohat * jnp.mean(dohat * ohat, axis=-1, keepdims=True))
            Rh = rin_ref[h]
            dRn = dR[h]
            sc_t = _dot_nt(kr, qr) * decay_t
            dv = _dot(sc_t, do) + _dot(kr * zeta, dRn)
            ds = _dot_nt(do, v) * decay
            ds_t = _dot_nt(v, do) * decay_t
            dox = do * xi
            dq = _dot(ds, kr) + _dot_nt(dox, Rh)
            dk = _dot(ds_t, qr) + zeta * _dot_nt(v, dRn)
            dR[h] = math.exp(CH * lg) * dRn + _dot(qr.T, dox)
            du_ref[:, cols] = _unrot(dq, cos_v, sin_v).astype(du_ref.dtype)
            du_ref[:, 1024 + 256 * h:1024 + 256 * h + 256] = (_unrot(dk, cos_v, sin_v) * _RET_SCALE).astype(du_ref.dtype)
            du_ref[:, 2048 + 256 * h:2048 + 256 * h + 256] = dv.astype(du_ref.dtype)

    rmap = lambda b, c: (b * nch + nch - 1 - c, 0)
    return pl.pallas_call(
        body, name="ret_bwd", grid=(B, nch),
        in_specs=[pl.BlockSpec((CH, 1024), lambda b, c: (b * nch + nch - 1 - c, 1)),
                  pl.BlockSpec((CH, 4096), rmap), pl.BlockSpec((CH, 1024), rmap),
                  pl.BlockSpec((None, None, RET_HEADS, 256, 256), lambda b, c: (b, nch - 1 - c, 0, 0, 0)),
                  pl.BlockSpec((CH, 128), lambda b, c: (nch - 1 - c, 0)),
                  pl.BlockSpec((CH, 128), lambda b, c: (nch - 1 - c, 0)),
                  pl.BlockSpec((1, 1024), lambda b, c: (0, 0))],
        out_specs=[pl.BlockSpec((CH, 4096), rmap), pl.BlockSpec((None, 8, 1024), lambda b, c: (b, 0, 0))],
        out_shape=[jax.ShapeDtypeStruct((R, 4096), _MXU), jax.ShapeDtypeStruct((B, 8, 1024), F32)],
        scratch_shapes=[pltpu.VMEM((RET_HEADS, 256, 256), F32)],
        compiler_params=_cparams(("parallel", "arbitrary")),
    )(dycat, u0, opre, rin, cos, sin, norm_g)


_SB_SCALE = SB_HD ** -0.5


_SB_NB = 4


def _sb_valid(qb, kb, live):
    qpos = qb * CH + jnp.bitwise_and(_row_ids(2 * CH, CH), CH - 1)
    kpos = kb * CH + _lane_ids(2 * CH, CH)
    first = PAD + (1 - live) * (1 << 24)
    return jnp.logical_and(kpos < qpos, kpos >= first)


def _sb_softplus(z):
    return jnp.maximum(z, 0.0) + jnp.log(1.0 + jnp.exp(-jnp.abs(z)))


def _stack_heads(x):
    hm = _lane_ids(1, 128) < SB_HD
    return jnp.concatenate([jnp.where(hm, x, 0.0), jnp.where(hm, 0.0, x)], axis=0)


def _unstack_heads(x2):
    return jnp.where(_lane_ids(1, 128) < SB_HD, x2[:CH], x2[CH:])


def _sb_fwd(u1, B, nch):
    R = u1.shape[0]
    Pn = nch * CH

    def body(q_ref, k_ref, v_ref, out_ref, s_ref):
        qb = pl.program_id(2)
        q2 = _stack_heads(q_ref[...] * _SB_SCALE).astype(_MXU)
        mgt = (_row_ids(CH, CH) > _lane_ids(CH, CH)).astype(F32)

        def step(i, carry):
            out2, acc = carry
            blocks = []
            for t in range(_SB_NB):
                kb = qb - _SB_NB * i - t
                live = (kb >= 0).astype(jnp.int32)
                kbc = jnp.maximum(kb, 0)
                start = pl.multiple_of(kbc * CH, CH)
                valid = _sb_valid(qb, kbc, live)
                z = _dot_nt(q2, k_ref[pl.ds(start, CH), :])
                sp = _sb_softplus(z)
                lm = jnp.where(valid, -sp, 0.0)
                blocks.append((valid, z - sp, _dot_split(lm, mgt), jnp.sum(lm, axis=1, keepdims=True), start))
            for valid, ls, loc, rs, start in blocks:
                w = jnp.where(valid, jnp.exp(ls + loc + acc), 0.0)
                out2 = out2 + _dot(w, v_ref[pl.ds(start, CH), :])
                acc = acc + rs
            return out2, acc

        trips = (qb + _SB_NB) // _SB_NB
        out2, acc = lax.fori_loop(0, trips, step, (jnp.zeros((2 * CH, 128), F32), jnp.zeros((2 * CH, 1), F32)))
        out_ref[...] = _unstack_heads(out2).astype(out_ref.dtype)
        s_ref[...] = _unstack_heads(jnp.broadcast_to(acc, (2 * CH, 128)))

    qspec = lambda off: pl.BlockSpec((CH, 128), lambda b, hp, qb: (b * nch + qb, off + hp))
    kspec = lambda off: pl.BlockSpec((Pn, 128), lambda b, hp, qb: (b, off + hp))
    return pl.pallas_call(
        body, name="sb_fwd", grid=(B, SB_HEADS // 2, nch),
        in_specs=[qspec(0), kspec(8), kspec(16)],
        out_specs=[qspec(0), qspec(0)],
        out_shape=[jax.ShapeDtypeStruct((R, 2048), _MXU), jax.ShapeDtypeStruct((R, 1024), F32)],
        compiler_params=_cparams(("parallel", "parallel", "arbitrary")),
    )(u1, u1, u1)


def _sb_bwd(dycat, u1, stot, B, nch):
    R = u1.shape[0]
    Pn = nch * CH

    def body(q_ref, k_ref, v_ref, do_ref, s_ref, dq_ref, dk_ref, dv_ref):
        qb = pl.program_id(2)

        @pl.when(qb == 0)
        def _():
            dk_ref[...] = jnp.zeros_like(dk_ref)
            dv_ref[...] = jnp.zeros_like(dv_ref)

        q2 = _stack_heads(q_ref[...] * _SB_SCALE).astype(_MXU)
        do2 = _stack_heads(do_ref[...]).astype(_MXU)
        stv = s_ref[...]
        lane = _lane_ids(1, 128)
        s2 = jnp.concatenate([jnp.sum(jnp.where(lane == 0, stv, 0.0), axis=1, keepdims=True),
                              jnp.sum(jnp.where(lane == SB_HD, stv, 0.0), axis=1, keepdims=True)], axis=0)
        rr = _row_ids(CH, CH)
        cc = _lane_ids(CH, CH)
        mle = (rr <= cc).astype(F32)
        mlt = (rr < cc).astype(F32)

        def step(i, carry):
            dq2, pacc, gacc = carry
            blocks = []
            for t in range(_SB_NB):
                kb = _SB_NB * i + t
                live = (kb <= qb).astype(jnp.int32)
                start = pl.multiple_of(jnp.minimum(kb, qb) * CH, CH)
                valid = _sb_valid(qb, jnp.minimum(kb, qb), live)
                z = _dot_nt(q2, k_ref[pl.ds(start, CH), :])
                sp = _sb_softplus(z)
                lm = jnp.where(valid, -sp, 0.0)
                blocks.append((valid, z - sp, _dot_split(lm, mle), jnp.sum(lm, axis=1, keepdims=True), start))
            stage = []
            for valid, ls, ploc, rs, start in blocks:
                w = jnp.where(valid, jnp.exp(ls + (s2 - (ploc + pacc))), 0.0)
                gg = _dot_nt(do2, v_ref[pl.ds(start, CH), :]) * w
                stage.append((valid, ls, w, gg, _dot_split(gg, mlt), jnp.sum(gg, axis=1, keepdims=True), start))
                pacc = pacc + rs
            for valid, ls, w, gg, gloc, gs, start in stage:
                sig = jnp.exp(ls)
                dz = jnp.where(valid, gg * (1.0 - sig) - (gloc + gacc) * sig, 0.0)
                dq2 = dq2 + _dot(dz, k_ref[pl.ds(start, CH), :])
                dk_ref[pl.ds(start, CH), :] += _dot_tn(dz, q2)
                dv_ref[pl.ds(start, CH), :] += _dot_tn(w, do2)
                gacc = gacc + gs
            return dq2, pacc, gacc

        zero = jnp.zeros((2 * CH, 1), F32)
        trips = (qb + _SB_NB) // _SB_NB
        dq2 = lax.fori_loop(0, trips, step, (jnp.zeros((2 * CH, 128), F32), zero, zero))[0]
        dq_ref[...] = (_unstack_heads(dq2) * _SB_SCALE).astype(dq_ref.dtype)

    qspec = lambda off: pl.BlockSpec((CH, 128), lambda b, hp, qb: (b * nch + qb, off + hp))
    kspec = lambda off: pl.BlockSpec((Pn, 128), lambda b, hp, qb: (b, off + hp))
    full = jax.ShapeDtypeStruct((R, 1024), F32)
    return pl.pallas_call(
        body, name="sb_bwd", grid=(B, SB_HEADS // 2, nch),
        in_specs=[qspec(0), kspec(8), kspec(16), qspec(0), qspec(0)],
        out_specs=[qspec(0), kspec(0), kspec(0)],
        out_shape=[jax.ShapeDtypeStruct((R, 1024), _MXU), full, full],
        compiler_params=_cparams(("parallel", "parallel", "arbitrary")),
    )(u1, u1, u1, dycat, stot)


def _neg_expm1(x):
    series = -(x * (1.0 + x * (0.5 + x * (1.0 / 6.0 + x * (1.0 / 24.0)))))
    return jnp.where(x > -0.05, series, 1.0 - jnp.exp(x))


def _lru_gates(x, wa_ref, ba_ref, wx_ref, bx_ref, lam_ref):
    rs, is_ = [], []
    for n in range(LRU_BLOCKS):
        xb = x[:, 128 * n:128 * n + 128]
        rs.append(_dot(xb, wa_ref[n]))
        is_.append(_dot(xb, wx_ref[n]))
    r = _sigmoid(jnp.concatenate(rs, axis=1) + ba_ref[...])
    i = _sigmoid(jnp.concatenate(is_, axis=1) + bx_ref[...])
    sp = _softplus(-lam_ref[...])
    la = -LRU_C * r * sp
    a = jnp.exp(la)
    mult = jnp.sqrt(jnp.maximum(_neg_expm1(2.0 * la), 0.0))
    return r, i, sp, a, mult


def _lru_fwd(u1, ycat, conv_w, conv_b, wa, ba, wx, bx, lam, B, nch):
    R = u1.shape[0]

    def body(x_ref, xp_ref, gate_ref, cw_ref, cb_ref, wa_ref, ba_ref, wx_ref, bx_ref, lam_ref, ycat_in,
             out_ref, hs_ref, hc):
        c = pl.program_id(1)

        @pl.when(c == 0)
        def _():
            hc[...] = jnp.zeros_like(hc)

        x = _conv_pre(xp_ref[...], x_ref[...], cw_ref, cb_ref, 4)
        r, i, sp, a, mult = _lru_gates(x, wa_ref, ba_ref, wx_ref, bx_ref, lam_ref)
        b = jnp.where(_real_rows(c), mult * (i * x), 0.0)
        rows = _row_ids(CH)
        s = 1
        while s < CH:
            a_s = jnp.where(rows >= s, pltpu.roll(a, s, axis=0), 1.0)
            b_s = jnp.where(rows >= s, pltpu.roll(b, s, axis=0), 0.0)
            b = a * b_s + b
            a = a * a_s
            s *= 2
        h = a * hc[0:1, :] + b
        hs_ref[...] = h
        hc[0:1, :] = hs_ref[CH - 1:CH, :]
        out_ref[...] = (h * _gelu(gate_ref[...])).astype(out_ref.dtype)

    row = lambda col: pl.BlockSpec((CH, 1024), lambda b, c: (b * nch + c, col))
    vec = pl.BlockSpec((1, 1024), lambda b, c: (0, 0))
    wsp = pl.BlockSpec((LRU_BLOCKS, 128, 128), lambda b, c: (0, 0, 0))
    return pl.pallas_call(
        body, name="lru_fwd", grid=(B, nch),
        in_specs=[row(4), pl.BlockSpec((8, 1024), _prev8_map(nch, 4)), row(3),
                  pl.BlockSpec((4, 1024), lambda b, c: (0, 0)), vec, wsp, vec, wsp, vec, vec,
                  pl.BlockSpec(memory_space=pl.ANY)],
        out_specs=[row(1), row(0)],
        out_shape=[jax.ShapeDtypeStruct(ycat.shape, ycat.dtype), jax.ShapeDtypeStruct((R, 1024), F32)],
        scratch_shapes=[pltpu.VMEM((8, 1024), F32)],
        input_output_aliases={10: 0},
        compiler_params=_cparams(("parallel", "arbitrary")),
    )(u1, u1, u1, conv_w, conv_b, wa, ba, wx, bx, lam, ycat)


def _lru_bwd(dycat, u1, hs, conv_w, conv_b, wa, ba, wx, bx, lam, B, nch):
    R = u1.shape[0]

    def body(dy_ref, x_ref, xp_ref, gate_ref, hs_ref, hsp_ref, cw_ref, cb_ref, wa_ref, ba_ref, wx_ref, bx_ref, lam_ref,
             dgate_ref, dxc_ref, pg_ref, dwa_ref, dwx_ref, lc):
        c = nch - 1 - pl.program_id(1)

        @pl.when(pl.program_id(1) == 0)
        def _():
            lc[...] = jnp.zeros_like(lc)
            pg_ref[...] = jnp.zeros_like(pg_ref)
            dwa_ref[...] = jnp.zeros_like(dwa_ref)
            dwx_ref[...] = jnp.zeros_like(dwx_ref)

        x = _conv_pre(xp_ref[...], x_ref[...], cw_ref, cb_ref, 4)
        r, i, sp, a, mult = _lru_gates(x, wa_ref, ba_ref, wx_ref, bx_ref, lam_ref)
        h = hs_ref[...]
        hprev = _shift_down(hsp_ref[...], h, 1)
        gate = gate_ref[...]
        dy = dy_ref[...]
        dgate_ref[...] = (dy * h * _dgelu(gate)).astype(dgate_ref.dtype)
        rows = _row_ids(CH)
        lam_t = dy * _gelu(gate) + jnp.where(rows == CH - 1, lc[0:1, :], 0.0)
        coef = jnp.where(rows < CH - 1, pltpu.roll(a, CH - 1, axis=0), 0.0)
        s = 1
        while s < CH:
            c_s = jnp.where(rows < CH - s, pltpu.roll(coef, CH - s, axis=0), 1.0)
            l_s = jnp.where(rows < CH - s, pltpu.roll(lam_t, CH - s, axis=0), 0.0)
            lam_t = coef * l_s + lam_t
            coef = coef * c_s
            s *= 2
        lc[0:1, :] = jnp.sum(jnp.where(rows == 0, a * lam_t, 0.0), axis=0, keepdims=True)
        db = jnp.where(_real_rows(c), lam_t, 0.0)
        da = db * hprev
        dmult = db * (i * x)
        di = db * mult * x
        dx = db * mult * i
        pos = mult > 0.0
        dla = da * a + jnp.where(pos, -dmult * (a * a) / jnp.where(pos, mult, 1.0), 0.0)
        dr = dla * (-LRU_C * sp)
        pg_ref[2:3, :] += jnp.sum(dla * (LRU_C * r) * _sigmoid(-lam_ref[...]), axis=0, keepdims=True)
        dpr = dr * r * (1.0 - r)
        dpi = di * i * (1.0 - i)
        pg_ref[0:1, :] += jnp.sum(dpr, axis=0, keepdims=True)
        pg_ref[1:2, :] += jnp.sum(dpi, axis=0, keepdims=True)
        dxs = []
        for n in range(LRU_BLOCKS):
            blk = slice(128 * n, 128 * n + 128)
            dxs.append(dx[:, blk] + _dot_nt(dpr[:, blk], wa_ref[n]) + _dot_nt(dpi[:, blk], wx_ref[n]))
            dwa_ref[n] += _dot_tn(x[:, blk], dpr[:, blk])
            dwx_ref[n] += _dot_tn(x[:, blk], dpi[:, blk])
        dxc_ref[...] = jnp.concatenate(dxs, axis=1)

    rmap = lambda col: (lambda b, c: (b * nch + nch - 1 - c, col))
    row = lambda col: pl.BlockSpec((CH, 1024), rmap(col))
    prev = lambda col: pl.BlockSpec(
        (8, 1024), lambda b, c: (jnp.maximum((b * nch + nch - 1 - c) * (CH // 8) - 1, 0), col))
    vec = pl.BlockSpec((1, 1024), lambda b, c: (0, 0))
    wsp = pl.BlockSpec((LRU_BLOCKS, 128, 128), lambda b, c: (0, 0, 0))
    full = jax.ShapeDtypeStruct((R, 1024), F32)
    return pl.pallas_call(
        body, name="lru_bwd", grid=(B, nch),
        in_specs=[row(1), row(4), prev(4), row(3), row(0), prev(0),
                  pl.BlockSpec((4, 1024), lambda b, c: (0, 0)), vec, wsp, vec, wsp, vec, vec],
        out_specs=[row(0), row(0), pl.BlockSpec((None, 8, 1024), lambda b, c: (b, 0, 0)),
                   pl.BlockSpec((None, LRU_BLOCKS, 128, 128), lambda b, c: (b, 0, 0, 0)),
                   pl.BlockSpec((None, LRU_BLOCKS, 128, 128), lambda b, c: (b, 0, 0, 0))],
        out_shape=[jax.ShapeDtypeStruct((R, 1024), _MXU), full, jax.ShapeDtypeStruct((B, 8, 1024), F32),
                   jax.ShapeDtypeStruct((B, LRU_BLOCKS, 128, 128), F32),
                   jax.ShapeDtypeStruct((B, LRU_BLOCKS, 128, 128), F32)],
        scratch_shapes=[pltpu.VMEM((8, 1024), F32)],
        compiler_params=_cparams(("parallel", "arbitrary")),
    )(dycat, u1, u1, u1, hs, hs, conv_w, conv_b, wa, ba, wx, bx, lam)


_FFN_TC = FFN // 2


def _ffn_specs(nch):
    nt = FFN // _FFN_TC
    row = lambda off: pl.BlockSpec((CH, _FFN_TC), lambda b, c, j: (b * nch + c, off + j))
    prev = lambda off: pl.BlockSpec(
        (8, _FFN_TC), lambda b, c, j: (jnp.maximum((b * nch + c) * (CH // 8) - 1, 0), off + j))
    wsp = lambda off: pl.BlockSpec((3, _FFN_TC), lambda b, c, j: (0, off + j))
    bsp = lambda off: pl.BlockSpec((1, _FFN_TC), lambda b, c, j: (0, off + j))
    return nt, row, [row(0), prev(0), row(nt), prev(nt), wsp(0), wsp(nt), bsp(0), bsp(nt)]


def _ffn_act_fwd(uf, conv_w, conv_b, B, nch):
    R = uf.shape[0]
    nt, row, specs = _ffn_specs(nch)

    def body(g_ref, gp_ref, u_ref, up_ref, wg_ref, wu_ref, bg_ref, bu_ref, o_ref):
        cg = _conv_pre(gp_ref[...], g_ref[...], wg_ref, bg_ref, 3)
        cu = _conv_pre(up_ref[...], u_ref[...], wu_ref, bu_ref, 3)
        o_ref[...] = jnp.where(_real_rows(pl.program_id(1)), _silu(cg) * cu, 0.0).astype(o_ref.dtype)

    return pl.pallas_call(
        body, name="ffn_act_fwd", grid=(B, nch, nt),
        in_specs=specs, out_specs=row(0),
        out_shape=jax.ShapeDtypeStruct((R, FFN), _MXU),
        compiler_params=_cparams(("parallel", "parallel", "parallel")),
    )(uf, uf, uf, uf, conv_w, conv_w, conv_b, conv_b)


def _ffn_act_bwd(da, uf, conv_w, conv_b, nch, name):
    R = uf.shape[0]
    nt = FFN // _FFN_TC
    nr = R // CH
    K = 3

    def body(da_ref, dan_ref, g_ref, gp_ref, gn_ref, u_ref, up_ref, un_ref, wg_ref, wu_ref, bg_ref, bu_ref,
             dug_ref, duu_ref, dwg_ref, dwu_ref):
        i = pl.program_id(1)

        @pl.when(i == 0)
        def _():
            dwg_ref[...] = jnp.zeros_like(dwg_ref)
            dwu_ref[...] = jnp.zeros_like(dwu_ref)

        c = i % nch
        ext = CH + 8
        rows = _row_ids(ext)
        follows = (c < nch - 1).astype(jnp.int32)
        keep = jnp.logical_and(c * CH + rows >= PAD, rows < CH + 8 * follows)
        dav = jnp.where(keep, jnp.concatenate([da_ref[...], dan_ref[...]], axis=0), 0.0)

        def conv_ext(x_ref, xp_ref, xn_ref, w_ref, b_ref):
            cat = jnp.concatenate([xp_ref[...], x_ref[...], xn_ref[...]], axis=0)
            acc = cat[8:] * w_ref[K - 1:K, :] + b_ref[...]
            for s in range(1, K):
                acc = acc + pltpu.roll(cat, s, axis=0)[8:] * w_ref[K - 1 - s:K - s, :]
            return acc

        cg = conv_ext(g_ref, gp_ref, gn_ref, wg_ref, bg_ref)
        cu = conv_ext(u_ref, up_ref, un_ref, wu_ref, bu_ref)
        dcg = dav * cu * _dsilu(cg)
        dcu = dav * _silu(cg)
        for dc, x_ref, xp_ref, w_ref, din_ref, dw_ref in ((dcg, g_ref, gp_ref, wg_ref, dug_ref, dwg_ref),
                                                          (dcu, u_ref, up_ref, wu_ref, duu_ref, dwu_ref)):
            dp = dc[:CH]
            x = x_ref[...]
            din = dp * w_ref[K - 1:K, :]
            dw_ref[K - 1:K, :] += jnp.sum(dp * x, axis=0, keepdims=True)
            dw_ref[7:8, :] += jnp.sum(dp, axis=0, keepdims=True)
            for s in range(1, K):
                din = din + pltpu.roll(dc, ext - s, axis=0)[:CH] * w_ref[K - 1 - s:K - s, :]
                dw_ref[K - 1 - s:K - s, :] += jnp.sum(dp * _shift_down(xp_ref[...], x, s), axis=0, keepdims=True)
            din_ref[...] = din.astype(din_ref.dtype)

    row = lambda off: pl.BlockSpec((CH, _FFN_TC), lambda j, i: (i, off + j))
    prev = lambda off: pl.BlockSpec((8, _FFN_TC), lambda j, i: (jnp.maximum(i * (CH // 8) - 1, 0), off + j))
    nxt = lambda off: pl.BlockSpec(
        (8, _FFN_TC), lambda j, i: (jnp.minimum((i + 1) * (CH // 8), nr * (CH // 8) - 1), off + j))
    wsp = lambda off: pl.BlockSpec((K, _FFN_TC), lambda j, i: (0, off + j))
    bsp = lambda off: pl.BlockSpec((1, _FFN_TC), lambda j, i: (0, off + j))
    acc = pl.BlockSpec((8, _FFN_TC), lambda j, i: (0, j))
    half = jax.ShapeDtypeStruct((R, FFN), _MXU)
    dwsh = jax.ShapeDtypeStruct((8, FFN), F32)
    return pl.pallas_call(
        body, name=name, grid=(nt, nr),
        in_specs=[row(0), nxt(0), row(0), prev(0), nxt(0), row(nt), prev(nt), nxt(nt), wsp(0), wsp(nt), bsp(0), bsp(nt)],
        out_specs=[row(0), row(0), acc, acc],
        out_shape=[half, half, dwsh, dwsh],
        compiler_params=_cparams(("parallel", "arbitrary")),
    )(da, da, uf, uf, uf, uf, uf, uf, conv_w, conv_w, conv_b, conv_b)


def _head(h, g, target, B, nch):
    R = h.shape[0]

    def body(h_ref, g_ref, t_ref, dh_ref, loss_ref, dg_ref):
        c = pl.program_id(1)

        @pl.when(c == 0)
        def _():
            dh_ref[...] = jnp.zeros_like(dh_ref)
            loss_ref[...] = jnp.zeros_like(loss_ref)
            dg_ref[...] = jnp.zeros_like(dg_ref)

        @pl.when(c > 0)
        def _():
            x = h_ref[...]
            gv = g_ref[...]
            r = lax.rsqrt(jnp.mean(x * x, axis=-1, keepdims=True) + EPS)
            xhat = x * r
            e = xhat * gv - t_ref[...]
            loss_ref[...] += 0.5 * jnp.sum(jnp.mean(e * e, axis=-1, keepdims=True), axis=0, keepdims=True)
            dy = e * (1.0 / D)
            dg_ref[0:1, :] += jnp.sum(dy * xhat, axis=0, keepdims=True)
            dx = dy * gv
            dh_ref[...] = r * (dx - xhat * jnp.mean(dx * xhat, axis=-1, keepdims=True))

    row = pl.BlockSpec((CH, D), lambda b, c: (b * nch + c, 0))
    return pl.pallas_call(
        body, name="head", grid=(B, nch),
        in_specs=[row, pl.BlockSpec((1, D), lambda b, c: (0, 0)),
                  pl.BlockSpec((CH, D), lambda b, c: (b * (nch - 1) + jnp.maximum(c - 1, 0), 0))],
        out_specs=[row, pl.BlockSpec((None, 8, 128), lambda b, c: (b, 0, 0)),
                   pl.BlockSpec((None, 8, D), lambda b, c: (b, 0, 0))],
        out_shape=[jax.ShapeDtypeStruct((R, D), F32), jax.ShapeDtypeStruct((B, 8, 128), F32),
                   jax.ShapeDtypeStruct((B, 8, D), F32)],
        compiler_params=_cparams(("parallel", "arbitrary")),
    )(h, g, target)


ADAM_LR = 0.001
ADAM_B1 = 0.9
ADAM_B2 = 0.999
ADAM_EPS = 1e-08
ADAM_WD = 0.01
ADAM_STEP = 10


def _adamw(w, g, m, v, name):
    Rr, C = w.shape
    tr = _tile(Rr, (256, 64))

    def body(w_ref, g_ref, m_ref, v_ref, d_ref, nm_ref, nv_ref):
        gv = g_ref[...]
        nm = ADAM_B1 * m_ref[...] + (1.0 - ADAM_B1) * gv
        nv = ADAM_B2 * v_ref[...] + (1.0 - ADAM_B2) * (gv * gv)
        m_hat = nm / (1.0 - ADAM_B1 ** ADAM_STEP)
        v_hat = nv / (1.0 - ADAM_B2 ** ADAM_STEP)
        d_ref[...] = -ADAM_LR * (m_hat / (jnp.sqrt(v_hat) + ADAM_EPS) + ADAM_WD * w_ref[...])
        nm_ref[...] = nm
        nv_ref[...] = nv

    spec = pl.BlockSpec((tr, C), lambda i: (i, 0))
    sh = jax.ShapeDtypeStruct((Rr, C), F32)
    return pl.pallas_call(
        body, name=name, grid=(Rr // tr,),
        in_specs=[spec] * 4, out_specs=[spec] * 3, out_shape=[sh] * 3,
        compiler_params=_cparams(("parallel",)),
    )(w, g, m, v)


_MESH = pl.DeviceIdType.MESH
_ANY = pl.BlockSpec(memory_space=pl.ANY)


def _place():
    x, y, c = lax.axis_index("x"), lax.axis_index("y"), lax.axis_index("c")
    chips = [(1 - x, y), (x, 1 - y), (1 - x, 1 - y)]
    return x, y, c, chips


def _rcopy(src, dst, ssem, rsem, dev):
    return pltpu.make_async_remote_copy(src_ref=src, dst_ref=dst, send_sem=ssem, recv_sem=rsem,
                                        device_id=dev, device_id_type=_MESH)


def _gather_shards(bigs, small):
    nb = len(bigs)

    def body(*refs):
        ins, outs = refs[:nb + 1], refs[nb + 1:2 * nb + 2]
        ssem, rsem, fssem, frsem = refs[2 * nb + 2:]
        x, y, c, chips = _place()
        k = 2 * x + y
        sib = (x, y, 1 - c)

        def part(a, slot, hc):
            return outs[a].at[slot] if a == nb else outs[a].at[slot, hc]

        first = []
        for a in range(nb + 1):
            src = ins[a] if a == nb else ins[a].at[c]
            for j, (cx, cy) in enumerate(chips):
                first.append(_rcopy(src, part(a, k, c), ssem.at[3 * a + j], rsem.at[3 * a + j], (cx, cy, c)))
        for cp in first:
            cp.start()
        passed = []
        for a in range(nb + 1):
            for j, (cx, cy) in enumerate(chips):
                got = part(a, 2 * cx + cy, c)
                _rcopy(got, got, ssem.at[3 * a + j], rsem.at[3 * a + j], (cx, cy, c)).wait_recv()
                if a < nb:
                    fw = _rcopy(got, got, fssem.at[3 * a + j], frsem.at[3 * a + j], sib)
                    fw.start()
                    passed.append(fw)
        for a in range(nb):
            for j, (cx, cy) in enumerate(chips):
                got = part(a, 2 * cx + cy, 1 - c)
                _rcopy(got, got, fssem.at[3 * a + j], frsem.at[3 * a + j], sib).wait_recv()
        for cp in first + passed:
            cp.wait_send()

    arrs = list(bigs) + [small]
    n = 3 * (nb + 1)
    return pl.pallas_call(
        body, name="gather_shards",
        in_specs=[_ANY] * (nb + 1), out_specs=[_ANY] * (nb + 1),
        out_shape=[jax.ShapeDtypeStruct((4,) + a.shape, a.dtype) for a in arrs],
        scratch_shapes=[pltpu.SemaphoreType.DMA((n,)), pltpu.SemaphoreType.DMA((n,)),
                        pltpu.SemaphoreType.DMA((n,)), pltpu.SemaphoreType.DMA((n,))],
    )(*arrs)


def _swap_halves(grads):
    na = len(grads)
    halves = [g.shape[1] // 2 for g in grads]

    def body(*refs):
        ins, outs = refs[:na], refs[na:2 * na]
        ssem, rsem = refs[2 * na:]
        x, y, c, _ = _place()
        sib = (x, y, 1 - c)
        cps = [_rcopy(ins[a].at[:, pl.ds((1 - c) * halves[a], halves[a]), :], outs[a], ssem.at[a], rsem.at[a], sib)
               for a in range(na)]
        for cp in cps:
            cp.start()
        for cp in cps:
            cp.wait()

    return pl.pallas_call(
        body, name="swap_halves",
        in_specs=[_ANY] * na, out_specs=[_ANY] * na,
        out_shape=[jax.ShapeDtypeStruct((4, g.shape[1] // 2, g.shape[2]), g.dtype) for g in grads],
        scratch_shapes=[pltpu.SemaphoreType.DMA((na,)), pltpu.SemaphoreType.DMA((na,))],
    )(*grads)


def _sum_rows(rh):
    return rh if rh <= 512 else _tile(rh, (512, 256, 128, 64, 32))


def _chip_sum(grad, recv, core, name):
    _, r, cdim = grad.shape
    rh = r // 2
    tr = _sum_rows(rh)
    nblk = rh // tr

    def body(core_ref, g_ref, r_ref, o_ref):
        o_ref[...] = (g_ref[...] + r_ref[...]).astype(o_ref.dtype)

    return pl.pallas_call(
        body, name=name,
        grid_spec=pltpu.PrefetchScalarGridSpec(
            num_scalar_prefetch=1, grid=(4, nblk),
            in_specs=[pl.BlockSpec((None, tr, cdim), lambda s, i, cr: (s, cr[0] * nblk + i, 0)),
                      pl.BlockSpec((None, tr, cdim), lambda s, i, cr: (s, i, 0))],
            out_specs=pl.BlockSpec((None, tr, cdim), lambda s, i, cr: (s, i, 0))),
        out_shape=jax.ShapeDtypeStruct((4, rh, cdim), BF16),
        compiler_params=_cparams(("parallel", "parallel")),
    )(core, grad, recv)


def _scatter_sums(sums):
    na = len(sums)

    def body(*refs):
        ins, outs = refs[:na], refs[na:2 * na]
        ssem, rsem, lsem = refs[2 * na:]
        x, y, c, chips = _place()
        k = 2 * x + y
        local = [pltpu.make_async_copy(ins[a].at[k], outs[a].at[k], lsem.at[a]) for a in range(na)]
        for cp in local:
            cp.start()
        cps = []
        for a in range(na):
            for j, (cx, cy) in enumerate(chips):
                cps.append(_rcopy(ins[a].at[2 * cx + cy], outs[a].at[k], ssem.at[3 * a + j], rsem.at[3 * a + j],
                                  (cx, cy, c)))
        for cp in cps:
            cp.start()
        for a in range(na):
            for j, (cx, cy) in enumerate(chips):
                got = outs[a].at[2 * cx + cy]
                _rcopy(got, got, ssem.at[3 * a + j], rsem.at[3 * a + j], (cx, cy, c)).wait_recv()
        for cp in cps:
            cp.wait_send()
        for cp in local:
            cp.wait()

    return pl.pallas_call(
        body, name="scatter_sums",
        in_specs=[_ANY] * na, out_specs=[_ANY] * na,
        out_shape=[jax.ShapeDtypeStruct(s.shape, s.dtype) for s in sums],
        scratch_shapes=[pltpu.SemaphoreType.DMA((3 * na,)), pltpu.SemaphoreType.DMA((3 * na,)),
                        pltpu.SemaphoreType.DMA((na,))],
    )(*sums)


def _sum_chips(parts, name):
    _, rh, cdim = parts.shape
    tr = _sum_rows(rh)

    def body(p_ref, o_ref):
        acc = p_ref[0].astype(F32)
        for j in range(1, 4):
            acc = acc + p_ref[j].astype(F32)
        o_ref[...] = acc

    return pl.pallas_call(
        body, name=name, grid=(rh // tr,),
        in_specs=[pl.BlockSpec((4, tr, cdim), lambda i: (0, i, 0))],
        out_specs=pl.BlockSpec((tr, cdim), lambda i: (i, 0)),
        out_shape=jax.ShapeDtypeStruct((rh, cdim), F32),
        compiler_params=_cparams(("parallel",)),
    )(parts)


def _join_halves(reds):
    na = len(reds)

    def body(*refs):
        ins, outs = refs[:na], refs[na:2 * na]
        ssem, rsem = refs[2 * na:]
        x, y, c, _ = _place()
        cps = [_rcopy(ins[a], outs[a], ssem.at[a], rsem.at[a], (x, y, 1 - c)) for a in range(na)]
        for cp in cps:
            cp.start()
        for cp in cps:
            cp.wait()

    return pl.pallas_call(
        body, name="join_halves",
        in_specs=[_ANY] * na, out_specs=[_ANY] * na,
        out_shape=[jax.ShapeDtypeStruct(r.shape, r.dtype) for r in reds],
        scratch_shapes=[pltpu.SemaphoreType.DMA((na,)), pltpu.SemaphoreType.DMA((na,))],
    )(*reds)


def _allreduce_small(buf):
    n = buf.shape[0]

    def body(in_ref, out_ref, recv, ssem, rsem):
        x, y, c, _ = _place()
        peers = [(x, y, 1 - c), (1 - x, y, c), (x, 1 - y, c)]
        out_ref[...] = in_ref[...]
        for r, peer in enumerate(peers):
            cp = _rcopy(out_ref, recv.at[r], ssem.at[r], rsem.at[r], peer)
            cp.start()
            cp.wait()
            out_ref[...] = out_ref[...] + recv[r]

    vm = pl.BlockSpec(memory_space=pltpu.VMEM)
    return pl.pallas_call(
        body, name="allreduce_small",
        in_specs=[vm], out_specs=vm,
        out_shape=jax.ShapeDtypeStruct(buf.shape, F32),
        scratch_shapes=[pltpu.VMEM((3, n, 128), F32), pltpu.SemaphoreType.DMA((3,)), pltpu.SemaphoreType.DMA((3,))],
        compiler_params=pltpu.CompilerParams(vmem_limit_bytes=VMEM_LIMIT),
    )(buf)


_W_NAMES = ['meta_tokens', 'l0_mix_norm', 'l0_w_in', 'l0_ssd_conv_w', 'l0_ssd_conv_b', 'l0_ssd_dt_bias', 'l0_ssd_a_log',
            'l0_ssd_d', 'l0_ssd_norm', 'l0_ret_norm', 'l0_w_out', 'l0_ffn_norm', 'l0_ffn_w_in', 'l0_ffn_conv_w',
            'l0_ffn_conv_b', 'l0_ffn_w_out', 'l1_mix_norm', 'l1_w_in', 'l1_lru_conv_w', 'l1_lru_conv_b', 'l1_lru_wa',
            'l1_lru_ba', 'l1_lru_wx', 'l1_lru_bx', 'l1_lru_lambda', 'l1_w_out', 'l1_ffn_norm', 'l1_ffn_w_in',
            'l1_ffn_conv_w', 'l1_ffn_conv_b', 'l1_ffn_w_out', 'final_norm']
_IN_NAMES = ['x'] + _W_NAMES + ['loss_target'] + ['m_' + n for n in _W_NAMES] + ['v_' + n for n in _W_NAMES]
_BIG = ['l0_w_in', 'l0_w_out', 'l0_ffn_w_in', 'l0_ffn_w_out', 'l1_w_in', 'l1_w_out', 'l1_ffn_w_in', 'l1_ffn_w_out']
_BIG_COLS = ('l0_w_in', 'l0_ffn_w_in', 'l1_w_in', 'l1_ffn_w_in')
_SMALL_SHARDED = ['meta_tokens', 'l0_ssd_conv_w', 'l0_ffn_conv_w', 'l1_lru_conv_w', 'l1_ffn_conv_w']
_SMALL = [n for n in _W_NAMES if n not in _BIG]


def _pack(arrs):
    flat = []
    for a in arrs:
        v = a.reshape(-1).astype(F32)
        flat.append(jnp.pad(v, (0, (-v.shape[0]) % 128)))
    v = jnp.concatenate(flat)
    v = jnp.pad(v, (0, (-v.shape[0]) % 1024))
    return v.reshape(-1, 128)


def _unpack(buf, shapes):
    out, row = [], 0
    for sh in shapes:
        n = int(np.prod(sh))
        rows = -(-n // 128)
        out.append(buf[row:row + rows].reshape(-1)[:n].reshape(sh))
        row += rows
    return out


def kernel(x, meta_tokens, l0_mix_norm, l0_w_in, l0_ssd_conv_w, l0_ssd_conv_b, l0_ssd_dt_bias, l0_ssd_a_log, l0_ssd_d, l0_ssd_norm, l0_ret_norm, l0_w_out, l0_ffn_norm, l0_ffn_w_in, l0_ffn_conv_w, l0_ffn_conv_b, l0_ffn_w_out, l1_mix_norm, l1_w_in, l1_lru_conv_w, l1_lru_conv_b, l1_lru_wa, l1_lru_ba, l1_lru_wx, l1_lru_bx, l1_lru_lambda, l1_w_out, l1_ffn_norm, l1_ffn_w_in, l1_ffn_conv_w, l1_ffn_conv_b, l1_ffn_w_out, final_norm, loss_target, m_meta_tokens, m_l0_mix_norm, m_l0_w_in, m_l0_ssd_conv_w, m_l0_ssd_conv_b, m_l0_ssd_dt_bias, m_l0_ssd_a_log, m_l0_ssd_d, m_l0_ssd_norm, m_l0_ret_norm, m_l0_w_out, m_l0_ffn_norm, m_l0_ffn_w_in, m_l0_ffn_conv_w, m_l0_ffn_conv_b, m_l0_ffn_w_out, m_l1_mix_norm, m_l1_w_in, m_l1_lru_conv_w, m_l1_lru_conv_b, m_l1_lru_wa, m_l1_lru_ba, m_l1_lru_wx, m_l1_lru_bx, m_l1_lru_lambda, m_l1_w_out, m_l1_ffn_norm, m_l1_ffn_w_in, m_l1_ffn_conv_w, m_l1_ffn_conv_b, m_l1_ffn_w_out, m_final_norm, v_meta_tokens, v_l0_mix_norm, v_l0_w_in, v_l0_ssd_conv_w, v_l0_ssd_conv_b, v_l0_ssd_dt_bias, v_l0_ssd_a_log, v_l0_ssd_d, v_l0_ssd_norm, v_l0_ret_norm, v_l0_w_out, v_l0_ffn_norm, v_l0_ffn_w_in, v_l0_ffn_conv_w, v_l0_ffn_conv_b, v_l0_ffn_w_out, v_l1_mix_norm, v_l1_w_in, v_l1_lru_conv_w, v_l1_lru_conv_b, v_l1_lru_wa, v_l1_lru_ba, v_l1_lru_wx, v_l1_lru_bx, v_l1_lru_lambda, v_l1_w_out, v_l1_ffn_norm, v_l1_ffn_w_in, v_l1_ffn_conv_w, v_l1_ffn_conv_b, v_l1_ffn_w_out, v_final_norm):
    args = (x, meta_tokens, l0_mix_norm, l0_w_in, l0_ssd_conv_w, l0_ssd_conv_b, l0_ssd_dt_bias, l0_ssd_a_log, l0_ssd_d, l0_ssd_norm, l0_ret_norm, l0_w_out, l0_ffn_norm, l0_ffn_w_in, l0_ffn_conv_w, l0_ffn_conv_b, l0_ffn_w_out, l1_mix_norm, l1_w_in, l1_lru_conv_w, l1_lru_conv_b, l1_lru_wa, l1_lru_ba, l1_lru_wx, l1_lru_bx, l1_lru_lambda, l1_w_out, l1_ffn_norm, l1_ffn_w_in, l1_ffn_conv_w, l1_ffn_conv_b, l1_ffn_w_out, final_norm, loss_target, m_meta_tokens, m_l0_mix_norm, m_l0_w_in, m_l0_ssd_conv_w, m_l0_ssd_conv_b, m_l0_ssd_dt_bias, m_l0_ssd_a_log, m_l0_ssd_d, m_l0_ssd_norm, m_l0_ret_norm, m_l0_w_out, m_l0_ffn_norm, m_l0_ffn_w_in, m_l0_ffn_conv_w, m_l0_ffn_conv_b, m_l0_ffn_w_out, m_l1_mix_norm, m_l1_w_in, m_l1_lru_conv_w, m_l1_lru_conv_b, m_l1_lru_wa, m_l1_lru_ba, m_l1_lru_wx, m_l1_lru_bx, m_l1_lru_lambda, m_l1_w_out, m_l1_ffn_norm, m_l1_ffn_w_in, m_l1_ffn_conv_w, m_l1_ffn_conv_b, m_l1_ffn_w_out, m_final_norm, v_meta_tokens, v_l0_mix_norm, v_l0_w_in, v_l0_ssd_conv_w, v_l0_ssd_conv_b, v_l0_ssd_dt_bias, v_l0_ssd_a_log, v_l0_ssd_d, v_l0_ssd_norm, v_l0_ret_norm, v_l0_w_out, v_l0_ffn_norm, v_l0_ffn_w_in, v_l0_ffn_conv_w, v_l0_ffn_conv_b, v_l0_ffn_w_out, v_l1_mix_norm, v_l1_w_in, v_l1_lru_conv_w, v_l1_lru_conv_b, v_l1_lru_wa, v_l1_lru_ba, v_l1_lru_wx, v_l1_lru_bx, v_l1_lru_lambda, v_l1_w_out, v_l1_ffn_norm, v_l1_ffn_w_in, v_l1_ffn_conv_w, v_l1_ffn_conv_b, v_l1_ffn_w_out, v_final_norm)
    p = dict(zip(_IN_NAMES, args))
    B, seq, _ = x.shape
    nch = (seq + CH) // CH
    Pn = nch * CH
    R = B * Pn
    chip = 2 * lax.axis_index("x") + lax.axis_index("y")
    row2 = lambda v: v.reshape(1, -1)
    pad128 = lambda v: jnp.pad(v, (0, 128 - v.shape[0])).reshape(1, 128)

    small_shapes = [p[n].shape for n in _SMALL_SHARDED]
    halved = lambda w: w.astype(_MXU).reshape(2, w.shape[0] // 2, w.shape[1])
    mine = [halved(p[n]) for n in _BIG] + [_pack([p[n] for n in _SMALL_SHARDED])]
    gathered = [lax.dynamic_update_index_in_dim(g, own, chip, 0) for g, own in zip(_gather_shards(mine[:-1], mine[-1]), mine)]
    g_big, g_small = gathered[:-1], gathered[-1]
    W = {}
    for n, g in zip(_BIG, g_big):
        g = g.reshape(4, -1, g.shape[3])
        W[n] = jnp.concatenate([g[k] for k in range(4)], axis=1) if n in _BIG_COLS else g.reshape(-1, g.shape[2])
    per_chip = [_unpack(g_small[k], small_shapes) for k in range(4)]
    for i, n in enumerate(_SMALL_SHARDED):
        W[n] = jnp.concatenate([per_chip[k][i] for k in range(4)], axis=1)
    w0 = W['l0_w_in']
    w0_main = jnp.concatenate([w0[:, 3088:], w0[:, :3072]], axis=1)
    w0_dt = jnp.pad(w0[:, 3072:3088], ((0, 0), (0, 112)))
    cos, sin = _rope_tables(nch)

    meta = jnp.broadcast_to(W['meta_tokens'][None], (B, N_META, D))
    h0 = jnp.concatenate([jnp.zeros((B, PAD, D), F32), meta, x], axis=1).reshape(R, D)
    n0, n0t = _rmsnorm_fwd(h0, row2(p['l0_mix_norm']), "norm_l0_mix")
    u0 = _mm(n0, w0_main, "nn", F32, "l0_in_proj")
    udt = _mm(n0, w0_dt, "nn", F32, "l0_dt_proj")
    a_log, d_skip, dt_bias = pad128(p['l0_ssd_a_log']), pad128(p['l0_ssd_d']), pad128(p['l0_ssd_dt_bias'])
    ssd_cb = row2(p['l0_ssd_conv_b'])
    act, dt, dtt = _ssd_prep(u0, udt, W['l0_ssd_conv_w'], ssd_cb, dt_bias, B, nch)
    ycat0, ypre, hin = _ssd_fwd(act, u0, dt, dtt, a_log, d_skip, row2(p['l0_ssd_norm']), B, nch)
    ycat0, opre, rin = _ret_fwd(u0, ycat0, cos, sin, row2(p['l0_ret_norm']), B, nch)
    h1 = _mm(ycat0, W['l0_w_out'], "nn", F32, "l0_out_proj", add=h0)
    n1, n1t = _rmsnorm_fwd(h1, row2(p['l0_ffn_norm']), "norm_l0_ffn")
    uf0 = _mm(n1, W['l0_ffn_w_in'], "nn", F32, "l0_ffn_in")
    f0_cb = row2(p['l0_ffn_conv_b'])
    a0 = _ffn_act_fwd(uf0, W['l0_ffn_conv_w'], f0_cb, B, nch)
    h2 = _mm(a0, W['l0_ffn_w_out'], "nn", F32, "l0_ffn_out", add=h1)
    n2, n2t = _rmsnorm_fwd(h2, row2(p['l1_mix_norm']), "norm_l1_mix")
    u1 = _mm(n2, W['l1_w_in'], "nn", F32, "l1_in_proj")
    lru = (W['l1_lru_conv_w'], row2(p['l1_lru_conv_b']), p['l1_lru_wa'], row2(p['l1_lru_ba']), p['l1_lru_wx'],
           row2(p['l1_lru_bx']), row2(p['l1_lru_lambda']))
    ycat1, stot = _sb_fwd(u1, B, nch)
    ycat1, hs = _lru_fwd(u1, ycat1, *lru, B, nch)
    h3 = _mm(ycat1, W['l1_w_out'], "nn", F32, "l1_out_proj", add=h2)
    n3, n3t = _rmsnorm_fwd(h3, row2(p['l1_ffn_norm']), "norm_l1_ffn")
    uf1 = _mm(n3, W['l1_ffn_w_in'], "nn", F32, "l1_ffn_in")
    f1_cb = row2(p['l1_ffn_conv_b'])
    a1 = _ffn_act_fwd(uf1, W['l1_ffn_conv_w'], f1_cb, B, nch)
    h4 = _mm(a1, W['l1_ffn_w_out'], "nn", F32, "l1_ffn_out", add=h3)
    dh4, lossp, dgf = _head(h4, row2(p['final_norm']), p['loss_target'].reshape(B * seq, D), B, nch)
    loss = lax.psum(jnp.sum(lossp[:, 0, 0]), ("x", "y", "c"))

    G = {'final_norm': dgf[:, 0].sum(0)}

    def ffn_bwd(layer, dh_out, h_in, nt_in, uf, a_act, cb):
        pre = f"l{layer}_"
        w_in, w_out, cw = W[pre + 'ffn_w_in'], W[pre + 'ffn_w_out'], W[pre + 'ffn_conv_w']
        da = _mm(dh_out, w_out, "nt", F32, pre + "ffn_out_dgrad")
        G[pre + 'ffn_w_out'] = _mm(a_act, dh_out, "tn", F32, pre + "ffn_out_wgrad")
        dug, duu, dwg, dwu = _ffn_act_bwd(da, uf, cw, cb, nch, pre + "ffn_act_bwd")
        G[pre + 'ffn_conv_w'] = jnp.concatenate([dwg[:3], dwu[:3]], axis=1)
        G[pre + 'ffn_conv_b'] = jnp.concatenate([dwg[7], dwu[7]])
        dn = _mm(dug, w_in, "nt", F32, pre + "ffn_in_dgrad_g")
        dn = _mm(duu, w_in, "nt", F32, pre + "ffn_in_dgrad_u", add=dn, b_off=FFN)
        G[pre + 'ffn_w_in'] = jnp.concatenate([_mm(nt_in, dug, "nn", F32, pre + "ffn_in_wgrad_g"),
                                               _mm(nt_in, duu, "nn", F32, pre + "ffn_in_wgrad_u")], axis=1)
        dh_in, dg = _rmsnorm_bwd(h_in, row2(p[pre + 'ffn_norm']), dn, dh_out, nch, pre + "ffn_norm_bwd")
        G[pre + 'ffn_norm'] = dg[0]
        return dh_in

    dh3 = ffn_bwd(1, dh4, h3, n3t, uf1, a1, f1_cb)
    dy1 = _mm(dh3, W['l1_w_out'], "nt", F32, "l1_out_dgrad")
    G['l1_w_out'] = _mm(ycat1, dh3, "tn", F32, "l1_out_wgrad")
    dq, dk, dv = _sb_bwd(dy1, u1, stot, B, nch)
    dgate, dxc, pgl, dwa, dwx = _lru_bwd(dy1, u1, hs, *lru, B, nch)
    dxr, dcw = _conv_bwd(dxc, u1, 4096, W['l1_lru_conv_w'], 4, "l1_lru_conv_bwd")
    pgl = pgl.sum(0)
    G['l1_lru_ba'], G['l1_lru_bx'], G['l1_lru_lambda'] = pgl[0], pgl[1], pgl[2]
    G['l1_lru_wa'], G['l1_lru_wx'] = dwa.sum(0), dwx.sum(0)
    G['l1_lru_conv_w'], G['l1_lru_conv_b'] = dcw[:4], dcw[7]
    dn, dws = None, []
    for i, piece in enumerate((dq, dk, dv, dgate, dxr)):
        dn = _mm(piece, W['l1_w_in'], "nt", F32, f"l1_in_dgrad_{i}", add=dn, b_off=1024 * i)
        dws.append(_mm(n2t, piece, "nn", F32, f"l1_in_wgrad_{i}"))
    G['l1_w_in'] = jnp.concatenate(dws, axis=1)
    dh2, dg = _rmsnorm_bwd(h2, row2(p['l1_mix_norm']), dn, dh3, nch, "l1_mix_norm_bwd")
    G['l1_mix_norm'] = dg[0]

    dh1 = ffn_bwd(0, dh2, h1, n1t, uf0, a0, f0_cb)
    dy0 = _mm(dh1, W['l0_w_out'], "nt", F32, "l0_out_dgrad")
    G['l0_w_out'] = _mm(ycat0, dh1, "tn", F32, "l0_out_wgrad")
    dz, dxs, dbm, dcm, ddt4, pgs = _ssd_bwd(dy0, ypre, u0, act, dt, dtt, hin, a_log, d_skip, row2(p['l0_ssd_norm']), B, nch)
    dpre, ddtr, pgd = _ssd_prep_bwd(dxs, dbm, dcm, ddt4, u0, udt, W['l0_ssd_conv_w'], ssd_cb, dt_bias, B, nch)
    dxbc, dcw0 = _conv_bwd(dpre, u0, U0_XBC, W['l0_ssd_conv_w'], 4, "l0_ssd_conv_bwd")
    dqkvg, pgr = _ret_bwd(dy0, u0, opre, rin, cos, sin, row2(p['l0_ret_norm']), B, nch)
    pgs = pgs.sum(0)
    G['l0_ssd_norm'] = pgs[:, 0, :].reshape(-1)
    G['l0_ssd_d'] = pgs[:, 1, :128].sum(0)[:SSD_HEADS]
    G['l0_ssd_a_log'] = pgs[:, 2, :128].sum(0)[:SSD_HEADS]
    G['l0_ssd_dt_bias'] = pgd.sum(0)[0, :SSD_HEADS]
    G['l0_ssd_conv_w'], G['l0_ssd_conv_b'] = dcw0[:4], dcw0[7]
    G['l0_ret_norm'] = pgr.sum(0)[0]
    dn = _mm(dqkvg, w0_main, "nt", F32, "l0_in_dgrad_qkvg")
    dn = _mm(dz, w0_main, "nt", F32, "l0_in_dgrad_z", add=dn, b_off=U0_Z)
    dn = _mm(dxbc, w0_main, "nt", F32, "l0_in_dgrad_xbc", add=dn, b_off=U0_XBC)
    dn = _mm(ddtr, w0_dt, "nt", F32, "l0_in_dgrad_dt", add=dn)
    G['l0_w_in'] = jnp.concatenate([
        _mm(n0t, dz, "nn", F32, "l0_in_wgrad_z"), _mm(n0t, dxbc, "nn", F32, "l0_in_wgrad_xbc"),
        _mm(n0t, ddtr, "nn", F32, "l0_in_wgrad_dt")[:, :SSD_HEADS], _mm(n0t, dqkvg, "nn", F32, "l0_in_wgrad_qkvg")], axis=1)
    dh0, dg = _rmsnorm_bwd(h0, row2(p['l0_mix_norm']), dn, dh1, nch, "l0_mix_norm_bwd")
    G['l0_mix_norm'] = dg[0]
    dh0 = dh0.reshape(B, Pn, D)
    grad_x = dh0[:, CH:]
    G['meta_tokens'] = dh0[:, PAD:CH].sum(0)

    core = lax.axis_index("c").reshape(1).astype(jnp.int32)
    stacked = []
    for n in _BIG:
        g = G[n]
        if n in _BIG_COLS:
            stacked.append(g.reshape(g.shape[0], 4, g.shape[1] // 4).transpose(1, 0, 2))
        else:
            stacked.append(g.reshape(4, g.shape[0] // 4, g.shape[1]))
    theirs = _swap_halves(stacked)
    sums = [_chip_sum(g, t, core, "chip_sum_" + n) for n, g, t in zip(_BIG, stacked, theirs)]
    parts = _scatter_sums(sums)
    reds = [_sum_chips(q, "sum_chips_" + n) for n, q in zip(_BIG, parts)]
    grads = {}
    for n, own, other in zip(_BIG, reds, _join_halves(reds)):
        both = jnp.where(core[0] == 0, jnp.stack([own, other]), jnp.stack([other, own]))
        grads[n] = both.reshape(-1, both.shape[2])
    small_full = _unpack(_allreduce_small(_pack([G[n] for n in _SMALL])), [G[n].shape for n in _SMALL])
    for n, g in zip(_SMALL, small_full):
        if n in _SMALL_SHARDED:
            cs = g.shape[1] // 4
            g = lax.dynamic_slice_in_dim(g, chip * cs, cs, axis=1)
        grads[n] = g.reshape(p[n].shape)

    delta, new_m, new_v = {}, {}, {}
    for n in _BIG:
        delta[n], new_m[n], new_v[n] = _adamw(p[n], grads[n], p['m_' + n], p['v_' + n], "adamw_" + n)
    shapes = [p[n].shape for n in _SMALL]
    outs = _adamw(_pack([p[n] for n in _SMALL]), _pack([grads[n] for n in _SMALL]), _pack([p['m_' + n] for n in _SMALL]),
                  _pack([p['v_' + n] for n in _SMALL]), "adamw_small")
    for dst, buf in zip((delta, new_m, new_v), outs):
        for n, a in zip(_SMALL, _unpack(buf, shapes)):
            dst[n] = a
    return (loss, grad_x, *[grads[n] for n in _W_NAMES], *[delta[n] for n in _W_NAMES],
            *[new_m[n] for n in _W_NAMES], *[new_v[n] for n in _W_NAMES])
```

```python
import math

import numpy as np
import jax
import jax.numpy as jnp
from jax import lax
from jax.experimental import pallas as pl
from jax.experimental.pallas import tpu as pltpu

F32 = jnp.float32
BF16 = jnp.bfloat16
_MXU = jnp.bfloat16

D = 1024
CH = 128
N_META = 16
PAD = CH - N_META
EPS = 1e-6

SSD_HEADS = 16
SSD_HD = 64
SSD_GROUPS = 4
RET_HEADS = 4
RET_DK = 256
SB_HEADS = 16
SB_HD = 64
LRU_BLOCKS = 8
LRU_C = 8.0
FFN = 2816
U0_Z = 4096
U0_XBC = 5120

VMEM_LIMIT = 56 * 1024 * 1024


def _cparams(sem):
    return pltpu.CompilerParams(dimension_semantics=sem, vmem_limit_bytes=VMEM_LIMIT)


def _dot(a, b, dims=((1,), (0,))):
    return lax.dot_general(a.astype(_MXU), b.astype(_MXU), (dims, ((), ())), preferred_element_type=F32)


def _dot_nt(a, b):
    return _dot(a, b, ((1,), (1,)))


def _dot_tn(a, b):
    return _dot(a.T, b)


def _dot_exact(a, b):
    return lax.dot_general(a, b, (((1,), (0,)), ((), ())), preferred_element_type=F32,
                           precision=lax.Precision.HIGHEST)


def _dot_split(x, m01):
    hi = x.astype(BF16)
    lo = (x - hi.astype(F32)).astype(BF16)
    m = m01.astype(BF16)
    return jnp.dot(hi, m, preferred_element_type=F32) + jnp.dot(lo, m, preferred_element_type=F32)


def _sigmoid(x):
    return 0.5 * jnp.tanh(0.5 * x) + 0.5


def _softplus(x):
    return jnp.maximum(x, 0.0) + jnp.log1p(jnp.exp(-jnp.abs(x)))


def _silu(x):
    return x * _sigmoid(x)


def _dsilu(x):
    s = _sigmoid(x)
    return s * (1.0 + x * (1.0 - s))


_GELU_C = math.sqrt(2.0 / math.pi)


def _gelu(x):
    return 0.5 * x * (1.0 + jnp.tanh(_GELU_C * (x + 0.044715 * x * x * x)))


def _dgelu(x):
    t = jnp.tanh(_GELU_C * (x + 0.044715 * x * x * x))
    return 0.5 * (1.0 + t) + 0.5 * x * (1.0 - t * t) * _GELU_C * (1.0 + 3.0 * 0.044715 * x * x)


def _row_ids(n, cols=1):
    return lax.broadcasted_iota(jnp.int32, (n, cols), 0)


def _lane_ids(rows, n):
    return lax.broadcasted_iota(jnp.int32, (rows, n), 1)


def _real_rows(chunk):
    return chunk * CH + _row_ids(CH) >= PAD


def _shift_down(prev8, cur, s):
    cat = jnp.concatenate([prev8, cur], axis=0)
    return pltpu.roll(cat, s, axis=0)[8:]


def _shift_up(cur, next8, s):
    n = cur.shape[0]
    cat = jnp.concatenate([cur, next8], axis=0)
    return pltpu.roll(cat, n + 8 - s, axis=0)[:n]


def _conv_pre(prev8, cur, w_ref, b_ref, K):
    acc = cur * w_ref[K - 1:K, :] + b_ref[...]
    for s in range(1, K):
        acc = acc + _shift_down(prev8, cur, s) * w_ref[K - 1 - s:K - s, :]
    return acc


def _prev8_map(nch, col):
    return lambda b, c: (jnp.maximum((b * nch + c) * (CH // 8) - 1, 0), col)


def _matmul(a, b, mode, out_dtype, tm, tn, tk, name, add=None, b_off=0):
    if mode == "nn":
        (M, K), (_, N) = a.shape, b.shape
    elif mode == "nt":
        (M, K), N = a.shape, b.shape[0]
    else:
        (K, M), (_, N) = a.shape, b.shape
    tm, tn, tk = min(tm, M), min(tn, N), min(tk, K)
    assert M % tm == 0 and N % tn == 0 and K % tk == 0 and b_off % tk == 0, (name, M, N, K, tm, tn, tk)
    koff = b_off // tk
    nk = K // tk
    dims = {"nn": ((1,), (0,)), "nt": ((1,), (1,)), "tn": ((0,), (0,))}[mode]
    if mode == "tn":
        a_spec = pl.BlockSpec((tk, tm), lambda i, j, k: (k, i))
    else:
        a_spec = pl.BlockSpec((tm, tk), lambda i, j, k: (i, k))
    if mode == "nt":
        b_spec = pl.BlockSpec((tn, tk), lambda i, j, k: (j, k + koff))
    else:
        b_spec = pl.BlockSpec((tk, tn), lambda i, j, k: (k, j))
    o_spec = pl.BlockSpec((tm, tn), lambda i, j, k: (i, j))
    has_add = add is not None

    def body(a_ref, b_ref, *rest):
        if has_add:
            add_ref, o_ref, acc = rest
        else:
            o_ref, acc = rest
        k = pl.program_id(2)

        @pl.when(k == 0)
        def _():
            acc[...] = jnp.zeros_like(acc)

        acc[...] += _dot(a_ref[...], b_ref[...], dims)

        @pl.when(k == nk - 1)
        def _():
            r = acc[...]
            if has_add:
                r = r + add_ref[...].astype(F32)
            o_ref[...] = r.astype(out_dtype)

    in_specs = [a_spec, b_spec] + ([o_spec] if has_add else [])
    args = (a, b) + ((add,) if has_add else ())
    return pl.pallas_call(
        body, name=name, grid=(M // tm, N // tn, nk),
        in_specs=in_specs, out_specs=o_spec,
        out_shape=jax.ShapeDtypeStruct((M, N), out_dtype),
        scratch_shapes=[pltpu.VMEM((tm, tn), F32)],
        compiler_params=_cparams(("parallel", "parallel", "arbitrary")),
    )(*args)


def _tile(n, prefs):
    for t in prefs:
        if n % t == 0:
            return t
    return n


def _mm(a, b, mode, out_dtype, name, add=None, b_off=0):
    if mode == "tn":
        K, M = a.shape
        N = b.shape[1]
        tm, tn, tk = _tile(M, (1024, 1408, 512, 256, 128)), _tile(N, (1024, 512, 256, 128)), _tile(K, (2176, 384, 256, 128))
    else:
        M, K = a.shape
        N = b.shape[1] if mode == "nn" else b.shape[0]
        tm = _tile(M, (1088, 1024, 768, 512, 384, 256, 128))
        tn = _tile(N, (1024, 512, 256, 128))
        tk = _tile(K, (2176, 1024, 1408, 512, 256, 128))
    return _matmul(a, b, mode, out_dtype, tm, tn, tk, name, add=add, b_off=b_off)


def _rmsnorm_fwd(h, g, name):
    R = h.shape[0]
    tr = 2 * CH

    def body(h_ref, g_ref, o_ref, ot_ref):
        x = h_ref[...]
        r = lax.rsqrt(jnp.mean(x * x, axis=-1, keepdims=True) + EPS)
        y = x * r * g_ref[...]
        o_ref[...] = y.astype(o_ref.dtype)
        ot_ref[...] = y.T.astype(ot_ref.dtype)

    return pl.pallas_call(
        body, name=name, grid=(R // tr,),
        in_specs=[pl.BlockSpec((tr, D), lambda i: (i, 0)), pl.BlockSpec((1, D), lambda i: (0, 0))],
        out_specs=[pl.BlockSpec((tr, D), lambda i: (i, 0)), pl.BlockSpec((D, tr), lambda i: (0, i))],
        out_shape=[jax.ShapeDtypeStruct((R, D), _MXU), jax.ShapeDtypeStruct((D, R), _MXU)],
        compiler_params=_cparams(("parallel",)),
    )(h, g)


def _rmsnorm_bwd(h, g, dn, dres, nch, name):
    R = h.shape[0]

    def body(h_ref, g_ref, dn_ref, dres_ref, dh_ref, dg_ref):
        i = pl.program_id(0)
        x = h_ref[...]
        r = lax.rsqrt(jnp.mean(x * x, axis=-1, keepdims=True) + EPS)
        xhat = x * r
        dn_v = dn_ref[...]
        dx = dn_v * g_ref[...]
        dh = r * (dx - xhat * jnp.mean(dx * xhat, axis=-1, keepdims=True))
        dh_ref[...] = jnp.where(_real_rows(i % nch), dres_ref[...] + dh, 0.0)

        @pl.when(i == 0)
        def _():
            dg_ref[...] = jnp.zeros_like(dg_ref)

        dg_ref[...] += jnp.sum(dn_v * xhat, axis=0, keepdims=True)

    row = pl.BlockSpec((CH, D), lambda i: (i, 0))
    vec = pl.BlockSpec((1, D), lambda i: (0, 0))
    return pl.pallas_call(
        body, name=name, grid=(R // CH,),
        in_specs=[row, vec, row, row], out_specs=[row, vec],
        out_shape=[jax.ShapeDtypeStruct((R, D), F32), jax.ShapeDtypeStruct((1, D), F32)],
        compiler_params=_cparams(("arbitrary",)),
    )(h, g, dn, dres)


def _ssd_prep(u0, udt, conv_w, conv_b, dt_bias, B, nch, rider=None):
    R = u0.shape[0]

    def body(xs_ref, xsp_ref, bc_ref, bcp_ref, udt_ref, w0_ref, w1_ref, b0_ref, b1_ref, dtb_ref,
             act_ref, dt_ref, dtt_ref):
        keep = _real_rows(pl.program_id(1))
        a0 = _silu(_conv_pre(xsp_ref[...], xs_ref[...], w0_ref, b0_ref, 4))
        a1 = _silu(_conv_pre(bcp_ref[...], bc_ref[...], w1_ref, b1_ref, 4))
        act_ref[:, :1024] = jnp.where(keep, a0, 0.0)
        act_ref[:, 1024:] = jnp.where(keep, a1, 0.0)
        ok = jnp.logical_and(keep, _lane_ids(1, 128) < SSD_HEADS)
        dt = jnp.where(ok, _softplus(udt_ref[...] + dtb_ref[...]), 0.0)
        dt_ref[...] = dt
        dtt_ref[...] = dt.T

    row = lambda col: pl.BlockSpec((CH, 1024), lambda b, c: (b * nch + c, col))
    prev = lambda col: pl.BlockSpec((8, 1024), _prev8_map(nch, col))
    kw = dict(
        grid=(B, nch),
        in_specs=[row(5), prev(5), row(6), prev(6),
                  pl.BlockSpec((CH, 128), lambda b, c: (b * nch + c, 0)),
                  pl.BlockSpec((4, 1024), lambda b, c: (0, 0)), pl.BlockSpec((4, 1024), lambda b, c: (0, 1)),
                  pl.BlockSpec((1, 1024), lambda b, c: (0, 0)), pl.BlockSpec((1, 1024), lambda b, c: (0, 1)),
                  pl.BlockSpec((1, 128), lambda b, c: (0, 0))],
        out_specs=[pl.BlockSpec((CH, 2048), lambda b, c: (b * nch + c, 0)),
                   pl.BlockSpec((CH, 128), lambda b, c: (b * nch + c, 0)),
                   pl.BlockSpec((128, CH), lambda b, c: (0, b * nch + c))],
        out_shape=[jax.ShapeDtypeStruct((R, 2048), F32), jax.ShapeDtypeStruct((R, 128), F32),
                   jax.ShapeDtypeStruct((128, R), F32)])
    return _call(body, "ssd_prep", ("arbitrary", "arbitrary"), kw,
                 (u0, u0, u0, u0, udt, conv_w, conv_w, conv_b, conv_b, dt_bias), rider)


def _ssd_head_terms(h, a_vec, dt_v, dtt_v, dsk_v):
    lane = _lane_ids(1, 128)
    sub = _row_ids(128)
    r = _row_ids(CH, CH)
    cidx = _lane_ids(CH, CH)
    a_h = jnp.sum(jnp.where(lane == h, a_vec, 0.0), axis=1, keepdims=True)
    dt_col = jnp.sum(jnp.where(lane == h, dt_v, 0.0), axis=1, keepdims=True)
    dt_row = jnp.sum(jnp.where(sub == h, dtt_v, 0.0), axis=0, keepdims=True)
    cs_col = jnp.sum(jnp.where(r >= cidx, dt_row * a_h, 0.0), axis=1, keepdims=True)
    cs_row = jnp.sum(jnp.where(r <= cidx, dt_col * a_h, 0.0), axis=0, keepdims=True)
    tot = jnp.sum(dt_col * a_h, axis=0, keepdims=True)
    dsk = jnp.sum(jnp.where(lane == h, dsk_v, 0.0), axis=1, keepdims=True)
    return a_h, dt_col, cs_col, cs_row, tot, dsk


def _ssd_fwd(act, u0, dt, dtt, a_log, d_skip, norm_g, B, nch, rider=None):
    R = act.shape[0]

    def body(xs_ref, bm_ref, cm_ref, z_ref, dt_ref, dtt_ref, alog_ref, dsk_ref, ng_ref,
             out_ref, ypre_ref, hin_ref, H):
        g = pl.program_id(1)
        c = pl.program_id(2)

        @pl.when(c == 0)
        def _():
            H[...] = jnp.zeros_like(H)

        hin_ref[...] = H[...]
        a_vec = -jnp.exp(alog_ref[...])
        dt_v = dt_ref[...]
        dtt_v = dtt_ref[...]
        hm = _lane_ids(1, 128) < SSD_HD
        r = _row_ids(CH, CH)
        cidx = _lane_ids(CH, CH)
        Bm = bm_ref[...]
        Cm = cm_ref[...]
        CB = _dot_nt(Cm, Bm)
        ys = []
        for pair in range(2):
            cols = slice(128 * pair, 128 * pair + 128)
            xraw = xs_ref[:, cols]
            t = [_ssd_head_terms(4 * g + 2 * pair + j, a_vec, dt_v, dtt_v, dsk_ref[...]) for j in range(2)]
            sel = lambda f: jnp.where(hm, f(t[0]), f(t[1]))
            dtp = sel(lambda q: q[1])
            Ep = sel(lambda q: jnp.exp(q[2]))
            Wp = sel(lambda q: jnp.exp(q[4] - q[2]))
            etot = sel(lambda q: jnp.exp(q[4]))
            dsk = sel(lambda q: q[5])
            X = xraw * dtp
            ydiag = jnp.zeros((CH, 128), F32)
            for j in range(2):
                Lm = jnp.where(r >= cidx, jnp.exp(t[j][2] - t[j][3]), 0.0)
                Xh = jnp.where(hm if j == 0 else jnp.logical_not(hm), X, 0.0)
                ydiag = ydiag + _dot(CB * Lm, Xh)
            Hp = H[:, cols]
            yoff = Ep * _dot(Cm, Hp)
            S = _dot(Bm.T, X * Wp)
            H[:, cols] = etot * Hp + S
            ys.append(ydiag + yoff + xraw * dsk)
        y = jnp.concatenate(ys, axis=1)
        ypre_ref[...] = y
        yg = y * _silu(z_ref[...])
        rr = lax.rsqrt(jnp.mean(yg * yg, axis=-1, keepdims=True) + EPS)
        out_ref[...] = jnp.where(_real_rows(c), yg * rr * ng_ref[...], 0.0).astype(out_ref.dtype)

    rowb = lambda w, colf: pl.BlockSpec((CH, w), lambda b, g, c: (b * nch + c, colf(g)))
    vec = pl.BlockSpec((1, 128), lambda b, g, c: (0, 0))
    kw = dict(
        grid=(B, SSD_GROUPS, nch),
        in_specs=[rowb(256, lambda g: g), rowb(128, lambda g: 8 + g), rowb(128, lambda g: 12 + g),
                  rowb(256, lambda g: 16 + g), rowb(128, lambda g: 0),
                  pl.BlockSpec((128, CH), lambda b, g, c: (0, b * nch + c)),
                  vec, vec, pl.BlockSpec((1, 256), lambda b, g, c: (0, g))],
        out_specs=[rowb(256, lambda g: g), rowb(256, lambda g: g),
                   pl.BlockSpec((None, None, None, 128, 256), lambda b, g, c: (b, g, c, 0, 0))],
        out_shape=[jax.ShapeDtypeStruct((R, 2048), _MXU), jax.ShapeDtypeStruct((R, 1024), F32),
                   jax.ShapeDtypeStruct((B, SSD_GROUPS, nch, 128, 256), F32)],
        scratch_shapes=[pltpu.VMEM((128, 256), F32)])
    return _call(body, "ssd_fwd", ("arbitrary", "arbitrary", "arbitrary"), kw,
                 (act, act, act, u0, dt, dtt, a_log, d_skip, norm_g), rider)


def _ssd_bwd(dycat, ypre, u0, act, dt, dtt, hin, a_log, d_skip, norm_g, B, nch, rider=None):
    R = act.shape[0]

    def body(dy_ref, ypre_ref, z_ref, xs_ref, bm_ref, cm_ref, dt_ref, dtt_ref, hin_ref, alog_ref, dsk_ref, ng_ref,
             dz_ref, dxs_ref, db_ref, dc_ref, ddt_ref, pg_ref, dH):
        g = pl.program_id(1)
        c = nch - 1 - pl.program_id(2)

        @pl.when(pl.program_id(2) == 0)
        def _():
            dH[...] = jnp.zeros_like(dH)
            pg_ref[...] = jnp.zeros_like(pg_ref)

        z = z_ref[...]
        y = ypre_ref[...]
        ng = ng_ref[...]
        dout = jnp.where(_real_rows(c), dy_ref[...], 0.0)
        sz = _sigmoid(z)
        yg = y * z * sz
        rr = lax.rsqrt(jnp.mean(yg * yg, axis=-1, keepdims=True) + EPS)
        nrm = yg * rr
        pg_ref[0:1, :] += jnp.sum(dout * nrm, axis=0, keepdims=True)
        dn = dout * ng
        dyg = rr * (dn - nrm * jnp.mean(dn * nrm, axis=-1, keepdims=True))
        dy = dyg * z * sz
        dz_ref[...] = (dyg * y * (sz * (1.0 + z * (1.0 - sz)))).astype(dz_ref.dtype)

        a_vec = -jnp.exp(alog_ref[...])
        dt_v = dt_ref[...]
        dtt_v = dtt_ref[...]
        lane = _lane_ids(1, 128)
        hm = lane < SSD_HD
        r = _row_ids(CH, CH)
        cidx = _lane_ids(CH, CH)
        last = _row_ids(CH) == CH - 1
        Bm = bm_ref[...]
        Cm = cm_ref[...]
        CB = _dot_nt(Cm, Bm)
        CBT = _dot_nt(Bm, Cm)
        dB = jnp.zeros((CH, 128), F32)
        dC = jnp.zeros((CH, 128), F32)
        dcs_all = jnp.zeros((CH, 128), F32)
        dtx_all = jnp.zeros((CH, 128), F32)
        dd_row = jnp.zeros((1, 128), F32)
        dxs = []
        for pair in range(2):
            cols = slice(128 * pair, 128 * pair + 128)
            xraw = xs_ref[:, cols]
            dyp = dy[:, cols]
            heads = [4 * g + 2 * pair + j for j in range(2)]
            t = [_ssd_head_terms(heads[j], a_vec, dt_v, dtt_v, dsk_ref[...]) for j in range(2)]
            sel = lambda f: jnp.where(hm, f(t[0]), f(t[1]))
            hsum = lambda v, j: jnp.sum(jnp.where(hm if j == 0 else jnp.logical_not(hm), v, 0.0), axis=1, keepdims=True)
            dtp = sel(lambda q: q[1])
            Ep = sel(lambda q: jnp.exp(q[2]))
            Wp = sel(lambda q: jnp.exp(q[4] - q[2]))
            etot = sel(lambda q: jnp.exp(q[4]))
            dsk = sel(lambda q: q[5])
            X = xraw * dtp
            Hp = hin_ref[:, cols]
            dHn = dH[:, cols]
            dskip = jnp.sum(dyp * xraw, axis=0, keepdims=True)
            yoff = Ep * _dot(Cm, Hp)
            dE = dyp * yoff
            dC = dC + _dot_nt(dyp * Ep, Hp)
            dH[:, cols] = etot * dHn + _dot(Cm.T, dyp * Ep)
            BdS = _dot(Bm, dHn)
            dX = Wp * BdS
            ew = X * BdS * Wp
            dB = dB + _dot_nt(X * Wp, dHn)
            hh = jnp.sum(dHn * Hp, axis=0, keepdims=True) * etot
            for j in range(2):
                hmask = hm if j == 0 else jnp.logical_not(hm)
                cs_col, cs_row = t[j][2], t[j][3]
                Lm = jnp.where(r >= cidx, jnp.exp(cs_col - cs_row), 0.0)
                LmT = jnp.where(cidx >= r, jnp.exp(cs_row - cs_col), 0.0)
                dyh = jnp.where(hmask, dyp, 0.0)
                Xh = jnp.where(hmask, X, 0.0)
                dM = _dot_nt(dyh, Xh)
                dMT = _dot_nt(Xh, dyh)
                M = CB * Lm
                MT = CBT * LmT
                dX = dX + _dot(MT, dyh)
                dC = dC + _dot(dM * Lm, Bm)
                dB = dB + _dot(dMT * LmT, Cm)
                g_rows = jnp.sum(dM * M, axis=1, keepdims=True)
                g_cols = jnp.sum(dMT * MT, axis=1, keepdims=True)
                dtot = (jnp.sum(hsum(ew, j), axis=0, keepdims=True)
                        + jnp.sum(jnp.where(hmask, hh, 0.0), axis=1, keepdims=True))
                dcs = g_rows - g_cols + hsum(dE, j) - hsum(ew, j) + jnp.where(last, dtot, 0.0)
                dcs_all = dcs_all + jnp.where(lane == heads[j], dcs, 0.0)
                dtx_all = dtx_all + jnp.where(lane == heads[j], hsum(dX * xraw, j), 0.0)
                dd_row = dd_row + jnp.where(lane == heads[j],
                                            jnp.sum(jnp.where(hmask, dskip, 0.0), axis=1, keepdims=True), 0.0)
            dxs.append(dX * dtp + dyp * dsk)
        dxs_ref[...] = jnp.concatenate(dxs, axis=1)
        db_ref[...] = dB
        dc_ref[...] = dC
        dadt = _dot_exact(jnp.where(cidx >= r, 1.0, 0.0), dcs_all)
        ddt_ref[...] = dadt * a_vec + dtx_all
        pg_ref[1:2, 0:128] += dd_row
        pg_ref[2:3, 0:128] += jnp.sum(dadt * dt_v, axis=0, keepdims=True) * a_vec

    rowb = lambda w, colf: pl.BlockSpec((CH, w), lambda b, g, c: (b * nch + nch - 1 - c, colf(g)))
    vec = pl.BlockSpec((1, 128), lambda b, g, c: (0, 0))
    kw = dict(
        grid=(B, SSD_GROUPS, nch),
        in_specs=[rowb(256, lambda g: g), rowb(256, lambda g: g), rowb(256, lambda g: 16 + g), rowb(256, lambda g: g),
                  rowb(128, lambda g: 8 + g), rowb(128, lambda g: 12 + g), rowb(128, lambda g: 0),
                  pl.BlockSpec((128, CH), lambda b, g, c: (0, b * nch + nch - 1 - c)),
                  pl.BlockSpec((None, None, None, 128, 256), lambda b, g, c: (b, g, nch - 1 - c, 0, 0)),
                  vec, vec, pl.BlockSpec((1, 256), lambda b, g, c: (0, g))],
        out_specs=[rowb(256, lambda g: g), rowb(256, lambda g: g), rowb(128, lambda g: g), rowb(128, lambda g: g),
                   rowb(128, lambda g: g),
                   pl.BlockSpec((None, None, 8, 256), lambda b, g, c: (b, g, 0, 0))],
        out_shape=[jax.ShapeDtypeStruct((R, 1024), _MXU), jax.ShapeDtypeStruct((R, 1024), F32),
                   jax.ShapeDtypeStruct((R, 512), F32), jax.ShapeDtypeStruct((R, 512), F32),
                   jax.ShapeDtypeStruct((R, 512), F32), jax.ShapeDtypeStruct((B, SSD_GROUPS, 8, 256), F32)],
        scratch_shapes=[pltpu.VMEM((128, 256), F32)])
    return _call(body, "ssd_bwd", ("arbitrary", "arbitrary", "arbitrary"), kw,
                 (dycat, ypre, u0, act, act, act, dt, dtt, hin, a_log, d_skip, norm_g), rider)


def _ssd_prep_bwd(dxs, dB, dC, ddt4, u0, udt, conv_w, conv_b, dt_bias, B, nch, rider=None):
    R = u0.shape[0]

    def body(dxs_ref, db_ref, dc_ref, ddt_ref, xs_ref, xsp_ref, bc_ref, bcp_ref, udt_ref, w0_ref, w1_ref, b0_ref, b1_ref,
             dtb_ref, dpre_ref, ddtr_ref, pgd_ref):
        c = pl.program_id(1)

        @pl.when(c == 0)
        def _():
            pgd_ref[...] = jnp.zeros_like(pgd_ref)

        keep = _real_rows(c)
        p0 = _conv_pre(xsp_ref[...], xs_ref[...], w0_ref, b0_ref, 4)
        p1 = _conv_pre(bcp_ref[...], bc_ref[...], w1_ref, b1_ref, 4)
        dpre_ref[:, :1024] = jnp.where(keep, dxs_ref[...] * _dsilu(p0), 0.0)
        dpre_ref[:, 1024:] = jnp.where(keep, jnp.concatenate([db_ref[...], dc_ref[...]], axis=1) * _dsilu(p1), 0.0)
        ddt = ddt_ref[:, 0:128] + ddt_ref[:, 128:256] + ddt_ref[:, 256:384] + ddt_ref[:, 384:512]
        ok = jnp.logical_and(keep, _lane_ids(1, 128) < SSD_HEADS)
        dr = jnp.where(ok, ddt * _sigmoid(udt_ref[...] + dtb_ref[...]), 0.0)
        ddtr_ref[...] = dr
        pgd_ref[0:1, :] += jnp.sum(dr, axis=0, keepdims=True)

    rw = lambda w: pl.BlockSpec((CH, w), lambda b, c: (b * nch + c, 0))
    row = lambda col: pl.BlockSpec((CH, 1024), lambda b, c: (b * nch + c, col))
    prev = lambda col: pl.BlockSpec((8, 1024), _prev8_map(nch, col))
    kw = dict(
        grid=(B, nch),
        in_specs=[rw(1024), rw(512), rw(512), rw(512), row(5), prev(5), row(6), prev(6), rw(128),
                  pl.BlockSpec((4, 1024), lambda b, c: (0, 0)), pl.BlockSpec((4, 1024), lambda b, c: (0, 1)),
                  pl.BlockSpec((1, 1024), lambda b, c: (0, 0)), pl.BlockSpec((1, 1024), lambda b, c: (0, 1)),
                  pl.BlockSpec((1, 128), lambda b, c: (0, 0))],
        out_specs=[rw(2048), rw(128), pl.BlockSpec((None, 8, 128), lambda b, c: (b, 0, 0))],
        out_shape=[jax.ShapeDtypeStruct((R, 2048), F32), jax.ShapeDtypeStruct((R, 128), F32),
                   jax.ShapeDtypeStruct((B, 8, 128), F32)])
    return _call(body, "ssd_prep_bwd", ("arbitrary", "arbitrary"), kw,
                 (dxs, dB, dC, ddt4, u0, u0, u0, u0, udt, conv_w, conv_w, conv_b, conv_b, dt_bias), rider)


def _conv_bwd(dpre, xin, xin_col, w, K, name, tc=1024):
    R, C = dpre.shape
    assert C % tc == 0 and xin_col % tc == 0
    nr = R // CH
    xoff = xin_col // tc

    def body(dp_ref, dpn_ref, x_ref, xp_ref, w_ref, din_ref, dw_ref):
        i = pl.program_id(1)

        @pl.when(i == 0)
        def _():
            dw_ref[...] = jnp.zeros_like(dw_ref)

        dp = dp_ref[...]
        nxt = dpn_ref[...] * (i < nr - 1).astype(F32)
        x = x_ref[...]
        xp = xp_ref[...]
        din = dp * w_ref[K - 1:K, :]
        dw_ref[K - 1:K, :] += jnp.sum(dp * x, axis=0, keepdims=True)
        dw_ref[7:8, :] += jnp.sum(dp, axis=0, keepdims=True)
        for s in range(1, K):
            din = din + _shift_up(dp, nxt, s) * w_ref[K - 1 - s:K - s, :]
            dw_ref[K - 1 - s:K - s, :] += jnp.sum(dp * _shift_down(xp, x, s), axis=0, keepdims=True)
        din_ref[...] = din.astype(din_ref.dtype)

    return pl.pallas_call(
        body, name=name, grid=(C // tc, nr),
        in_specs=[pl.BlockSpec((CH, tc), lambda j, i: (i, j)),
                  pl.BlockSpec((8, tc), lambda j, i: (jnp.minimum((i + 1) * (CH // 8), nr * (CH // 8) - 1), j)),
                  pl.BlockSpec((CH, tc), lambda j, i: (i, xoff + j)),
                  pl.BlockSpec((8, tc), lambda j, i: (jnp.maximum(i * (CH // 8) - 1, 0), xoff + j)),
                  pl.BlockSpec((K, tc), lambda j, i: (0, j))],
        out_specs=[pl.BlockSpec((CH, tc), lambda j, i: (i, j)),
                   pl.BlockSpec((8, tc), lambda j, i: (0, j))],
        out_shape=[jax.ShapeDtypeStruct((R, C), _MXU), jax.ShapeDtypeStruct((8, C), F32)],
        compiler_params=_cparams(("parallel", "arbitrary")),
    )(dpre, dpre, xin, xin, w)


_RET_LG = [float(v) for v in np.log1p(-np.exp2(-5.0 - np.arange(RET_HEADS, dtype=np.float32))).astype(np.float32)]
_RET_SCALE = RET_DK ** -0.5


def _rope_tables(nch):
    half = RET_DK // 2
    inv_freq = 1.0 / (10000.0 ** (jnp.arange(half, dtype=F32) / (half - 1)))
    pos = jnp.arange(nch * CH, dtype=F32) - PAD
    ang = pos[:, None] * inv_freq[None, :]
    return jnp.cos(ang), jnp.sin(ang)


def _rot(x, cos, sin):
    x1, x2 = x[:, :128], x[:, 128:]
    return jnp.concatenate([x1 * cos - x2 * sin, x1 * sin + x2 * cos], axis=1)


def _unrot(d, cos, sin):
    d1, d2 = d[:, :128], d[:, 128:]
    return jnp.concatenate([d1 * cos + d2 * sin, d2 * cos - d1 * sin], axis=1)


def _ret_decays(lg):
    r = _row_ids(CH, CH)
    cidx = _lane_ids(CH, CH)
    diff = (r - cidx).astype(F32)
    decay = jnp.where(r >= cidx, jnp.exp(lg * jnp.maximum(diff, 0.0)), 0.0)
    decay_t = jnp.where(cidx >= r, jnp.exp(lg * jnp.maximum(-diff, 0.0)), 0.0)
    idx = _row_ids(CH).astype(F32)
    zeta = jnp.exp(lg * (CH - 1.0 - idx))
    xi = jnp.exp(lg * (idx + 1.0))
    return decay, decay_t, zeta, xi


def _ret_fwd(u0, ycat, cos, sin, norm_g, B, nch, rider=None):
    R = u0.shape[0]

    def body(u_ref, cos_ref, sin_ref, ng_ref, ycat_in, out_ref, opre_ref, rin_ref, Rst):
        c = pl.program_id(1)

        @pl.when(c == 0)
        def _():
            Rst[...] = jnp.zeros_like(Rst)

        cos_v, sin_v = cos_ref[...], sin_ref[...]
        for h in range(RET_HEADS):
            lg = _RET_LG[h]
            cols = slice(256 * h, 256 * h + 256)
            decay, _, zeta, xi = _ret_decays(lg)
            qr = _rot(u_ref[:, cols], cos_v, sin_v)
            kr = _rot(u_ref[:, 1024 + 256 * h:1024 + 256 * h + 256], cos_v, sin_v) * _RET_SCALE
            v = u_ref[:, 2048 + 256 * h:2048 + 256 * h + 256]
            gate = u_ref[:, 3072 + 256 * h:3072 + 256 * h + 256]
            Rh = Rst[h]
            rin_ref[h] = Rh
            inner = _dot(_dot_nt(qr, kr) * decay, v)
            cross = _dot(qr, Rh) * xi
            Rst[h] = math.exp(CH * lg) * Rh + _dot((kr * zeta).T, v)
            o = inner + cross
            opre_ref[:, cols] = o
            oc = o - jnp.mean(o, axis=-1, keepdims=True)
            rr = lax.rsqrt(jnp.mean(oc * oc, axis=-1, keepdims=True) + EPS)
            out_ref[:, cols] = (_silu(gate) * (oc * rr * ng_ref[:, cols])).astype(out_ref.dtype)

    kw = dict(
        grid=(B, nch),
        in_specs=[pl.BlockSpec((CH, 4096), lambda b, c: (b * nch + c, 0)),
                  pl.BlockSpec((CH, 128), lambda b, c: (c, 0)), pl.BlockSpec((CH, 128), lambda b, c: (c, 0)),
                  pl.BlockSpec((1, 1024), lambda b, c: (0, 0)),
                  pl.BlockSpec(memory_space=pl.ANY)],
        out_specs=[pl.BlockSpec((CH, 1024), lambda b, c: (b * nch + c, 1)),
                   pl.BlockSpec((CH, 1024), lambda b, c: (b * nch + c, 0)),
                   pl.BlockSpec((None, None, RET_HEADS, 256, 256), lambda b, c: (b, c, 0, 0, 0))],
        out_shape=[jax.ShapeDtypeStruct(ycat.shape, ycat.dtype), jax.ShapeDtypeStruct((R, 1024), F32),
                   jax.ShapeDtypeStruct((B, nch, RET_HEADS, 256, 256), F32)],
        scratch_shapes=[pltpu.VMEM((RET_HEADS, 256, 256), F32)],
        input_output_aliases={4: 0})
    return _call(body, "ret_fwd", ("arbitrary", "arbitrary"), kw, (u0, cos, sin, norm_g, ycat), rider)


def _ret_bwd(dycat, u0, opre, rin, cos, sin, norm_g, B, nch, rider=None):
    R = u0.shape[0]

    def body(dy_ref, u_ref, opre_ref, rin_ref, cos_ref, sin_ref, ng_ref, du_ref, pg_ref, dR):
        @pl.when(pl.program_id(1) == 0)
        def _():
            dR[...] = jnp.zeros_like(dR)
            pg_ref[...] = jnp.zeros_like(pg_ref)

        cos_v, sin_v = cos_ref[...], sin_ref[...]
        for h in range(RET_HEADS):
            lg = _RET_LG[h]
            cols = slice(256 * h, 256 * h + 256)
            decay, decay_t, zeta, xi = _ret_decays(lg)
            qr = _rot(u_ref[:, cols], cos_v, sin_v)
            kr = _rot(u_ref[:, 1024 + 256 * h:1024 + 256 * h + 256], cos_v, sin_v) * _RET_SCALE
            v = u_ref[:, 2048 + 256 * h:2048 + 256 * h + 256]
            gate = u_ref[:, 3072 + 256 * h:3072 + 256 * h + 256]
            ng = ng_ref[:, cols]
            o = opre_ref[:, cols]
            oc = o - jnp.mean(o, axis=-1, keepdims=True)
            rr = lax.rsqrt(jnp.mean(oc * oc, axis=-1, keepdims=True) + EPS)
            ohat = oc * rr
            dout = dy_ref[:, cols]
            du_ref[:, 3072 + 256 * h:3072 + 256 * h + 256] = (dout * (ohat * ng) * _dsilu(gate)).astype(du_ref.dtype)
            don = dout * _silu(gate)
            pg_ref[0:1, cols] += jnp.sum(don * ohat, axis=0, keepdims=True)
            dohat = don * ng
            do = rr * (dohat - jnp.mean(dohat, axis=-1, keepdims=True)
                       - ohat * jnp.mean(dohat * ohat, axis=-1, keepdims=True))
            Rh = rin_ref[h]
            dRn = dR[h]
            sc_t = _dot_nt(kr, qr) * decay_t
            dv = _dot(sc_t, do) + _dot(kr * zeta, dRn)
            ds = _dot_nt(do, v) * decay
            ds_t = _dot_nt(v, do) * decay_t
            dox = do * xi
            dq = _dot(ds, kr) + _dot_nt(dox, Rh)
            dk = _dot(ds_t, qr) + zeta * _dot_nt(v, dRn)
            dR[h] = math.exp(CH * lg) * dRn + _dot(qr.T, dox)
            du_ref[:, cols] = _unrot(dq, cos_v, sin_v).astype(du_ref.dtype)
            du_ref[:, 1024 + 256 * h:1024 + 256 * h + 256] = (_unrot(dk, cos_v, sin_v) * _RET_SCALE).astype(du_ref.dtype)
            du_ref[:, 2048 + 256 * h:2048 + 256 * h + 256] = dv.astype(du_ref.dtype)

    rmap = lambda b, c: (b * nch + nch - 1 - c, 0)
    kw = dict(
        grid=(B, nch),
        in_specs=[pl.BlockSpec((CH, 1024), lambda b, c: (b * nch + nch - 1 - c, 1)),
                  pl.BlockSpec((CH, 4096), rmap), pl.BlockSpec((CH, 1024), rmap),
                  pl.BlockSpec((None, None, RET_HEADS, 256, 256), lambda b, c: (b, nch - 1 - c, 0, 0, 0)),
                  pl.BlockSpec((CH, 128), lambda b, c: (nch - 1 - c, 0)),
                  pl.BlockSpec((CH, 128), lambda b, c: (nch - 1 - c, 0)),
                  pl.BlockSpec((1, 1024), lambda b, c: (0, 0))],
        out_specs=[pl.BlockSpec((CH, 4096), rmap), pl.BlockSpec((None, 8, 1024), lambda b, c: (b, 0, 0))],
        out_shape=[jax.ShapeDtypeStruct((R, 4096), _MXU), jax.ShapeDtypeStruct((B, 8, 1024), F32)],
        scratch_shapes=[pltpu.VMEM((RET_HEADS, 256, 256), F32)])
    return _call(body, "ret_bwd", ("arbitrary", "arbitrary"), kw, (dycat, u0, opre, rin, cos, sin, norm_g), rider)


_SB_SCALE = SB_HD ** -0.5


_SB_NB = 4


def _sb_valid(qb, kb, live):
    qpos = qb * CH + jnp.bitwise_and(_row_ids(2 * CH, CH), CH - 1)
    kpos = kb * CH + _lane_ids(2 * CH, CH)
    first = PAD + (1 - live) * (1 << 24)
    return jnp.logical_and(kpos < qpos, kpos >= first)


def _sb_softplus(z):
    return jnp.maximum(z, 0.0) + jnp.log(1.0 + jnp.exp(-jnp.abs(z)))


def _stack_heads(x):
    hm = _lane_ids(1, 128) < SB_HD
    return jnp.concatenate([jnp.where(hm, x, 0.0), jnp.where(hm, 0.0, x)], axis=0)


def _unstack_heads(x2):
    return jnp.where(_lane_ids(1, 128) < SB_HD, x2[:CH], x2[CH:])


def _sb_fwd(u1, B, nch):
    R = u1.shape[0]
    Pn = nch * CH

    def body(q_ref, k_ref, v_ref, out_ref, s_ref):
        qb = pl.program_id(2)
        q2 = _stack_heads(q_ref[...] * _SB_SCALE).astype(_MXU)
        mgt = (_row_ids(CH, CH) > _lane_ids(CH, CH)).astype(F32)

        def step(i, carry):
            out2, acc = carry
            blocks = []
            for t in range(_SB_NB):
                kb = qb - _SB_NB * i - t
                live = (kb >= 0).astype(jnp.int32)
                kbc = jnp.maximum(kb, 0)
                start = pl.multiple_of(kbc * CH, CH)
                valid = _sb_valid(qb, kbc, live)
                z = _dot_nt(q2, k_ref[pl.ds(start, CH), :])
                sp = _sb_softplus(z)
                lm = jnp.where(valid, -sp, 0.0)
                blocks.append((valid, z - sp, _dot_split(lm, mgt), jnp.sum(lm, axis=1, keepdims=True), start))
            for valid, ls, loc, rs, start in blocks:
                w = jnp.where(valid, jnp.exp(ls + loc + acc), 0.0)
                out2 = out2 + _dot(w, v_ref[pl.ds(start, CH), :])
                acc = acc + rs
            return out2, acc

        trips = (qb + _SB_NB) // _SB_NB
        out2, acc = lax.fori_loop(0, trips, step, (jnp.zeros((2 * CH, 128), F32), jnp.zeros((2 * CH, 1), F32)))
        out_ref[...] = _unstack_heads(out2).astype(out_ref.dtype)
        s_ref[...] = _unstack_heads(jnp.broadcast_to(acc, (2 * CH, 128)))

    qspec = lambda off: pl.BlockSpec((CH, 128), lambda b, hp, qb: (b * nch + qb, off + hp))
    kspec = lambda off: pl.BlockSpec((Pn, 128), lambda b, hp, qb: (b, off + hp))
    return pl.pallas_call(
        body, name="sb_fwd", grid=(B, SB_HEADS // 2, nch),
        in_specs=[qspec(0), kspec(8), kspec(16)],
        out_specs=[qspec(0), qspec(0)],
        out_shape=[jax.ShapeDtypeStruct((R, 2048), _MXU), jax.ShapeDtypeStruct((R, 1024), F32)],
        compiler_params=_cparams(("parallel", "parallel", "arbitrary")),
    )(u1, u1, u1)


def _sb_bwd(dycat, u1, stot, B, nch):
    R = u1.shape[0]
    Pn = nch * CH

    def body(q_ref, k_ref, v_ref, do_ref, s_ref, dq_ref, dk_ref, dv_ref):
        qb = pl.program_id(2)

        @pl.when(qb == 0)
        def _():
            dk_ref[...] = jnp.zeros_like(dk_ref)
            dv_ref[...] = jnp.zeros_like(dv_ref)

        q2 = _stack_heads(q_ref[...] * _SB_SCALE).astype(_MXU)
        do2 = _stack_heads(do_ref[...]).astype(_MXU)
        stv = s_ref[...]
        lane = _lane_ids(1, 128)
        s2 = jnp.concatenate([jnp.sum(jnp.where(lane == 0, stv, 0.0), axis=1, keepdims=True),
                              jnp.sum(jnp.where(lane == SB_HD, stv, 0.0), axis=1, keepdims=True)], axis=0)
        rr = _row_ids(CH, CH)
        cc = _lane_ids(CH, CH)
        mle = (rr <= cc).astype(F32)
        mlt = (rr < cc).astype(F32)

        def step(i, carry):
            dq2, pacc, gacc = carry
            blocks = []
            for t in range(_SB_NB):
                kb = _SB_NB * i + t
                live = (kb <= qb).astype(jnp.int32)
                start = pl.multiple_of(jnp.minimum(kb, qb) * CH, CH)
                valid = _sb_valid(qb, jnp.minimum(kb, qb), live)
                z = _dot_nt(q2, k_ref[pl.ds(start, CH), :])
                sp = _sb_softplus(z)
                lm = jnp.where(valid, -sp, 0.0)
                blocks.append((valid, z - sp, _dot_split(lm, mle), jnp.sum(lm, axis=1, keepdims=True), start))
            stage = []
            for valid, ls, ploc, rs, start in blocks:
                w = jnp.where(valid, jnp.exp(ls + (s2 - (ploc + pacc))), 0.0)
                gg = _dot_nt(do2, v_ref[pl.ds(start, CH), :]) * w
                stage.append((valid, ls, w, gg, _dot(gg, mlt), jnp.sum(gg, axis=1, keepdims=True), start))
                pacc = pacc + rs
            for valid, ls, w, gg, gloc, gs, start in stage:
                sig = jnp.exp(ls)
                dz = jnp.where(valid, gg * (1.0 - sig) - (gloc + gacc) * sig, 0.0)
                dq2 = dq2 + _dot(dz, k_ref[pl.ds(start, CH), :])
                dk_ref[pl.ds(start, CH), :] += _dot_tn(dz, q2)
                dv_ref[pl.ds(start, CH), :] += _dot_tn(w, do2)
                gacc = gacc + gs
            return dq2, pacc, gacc

        zero = jnp.zeros((2 * CH, 1), F32)
        trips = (qb + _SB_NB) // _SB_NB
        dq2 = lax.fori_loop(0, trips, step, (jnp.zeros((2 * CH, 128), F32), zero, zero))[0]
        dq_ref[...] = (_unstack_heads(dq2) * _SB_SCALE).astype(dq_ref.dtype)

    qspec = lambda off: pl.BlockSpec((CH, 128), lambda b, hp, qb: (b * nch + qb, off + hp))
    kspec = lambda off: pl.BlockSpec((Pn, 128), lambda b, hp, qb: (b, off + hp))
    full = jax.ShapeDtypeStruct((R, 1024), F32)
    return pl.pallas_call(
        body, name="sb_bwd", grid=(B, SB_HEADS // 2, nch),
        in_specs=[qspec(0), kspec(8), kspec(16), qspec(0), qspec(0)],
        out_specs=[qspec(0), kspec(0), kspec(0)],
        out_shape=[jax.ShapeDtypeStruct((R, 1024), _MXU), full, full],
        compiler_params=_cparams(("parallel", "parallel", "arbitrary")),
    )(u1, u1, u1, dycat, stot)


def _neg_expm1(x):
    series = -(x * (1.0 + x * (0.5 + x * (1.0 / 6.0 + x * (1.0 / 24.0)))))
    return jnp.where(x > -0.05, series, 1.0 - jnp.exp(x))


def _lru_gates(x, wa_ref, ba_ref, wx_ref, bx_ref, lam_ref):
    rs, is_ = [], []
    for n in range(LRU_BLOCKS):
        xb = x[:, 128 * n:128 * n + 128]
        rs.append(_dot(xb, wa_ref[n]))
        is_.append(_dot(xb, wx_ref[n]))
    r = _sigmoid(jnp.concatenate(rs, axis=1) + ba_ref[...])
    i = _sigmoid(jnp.concatenate(is_, axis=1) + bx_ref[...])
    sp = _softplus(-lam_ref[...])
    la = -LRU_C * r * sp
    a = jnp.exp(la)
    mult = jnp.sqrt(jnp.maximum(_neg_expm1(2.0 * la), 0.0))
    return r, i, sp, a, mult


def _lru_fwd(u1, ycat, conv_w, conv_b, wa, ba, wx, bx, lam, B, nch):
    R = u1.shape[0]

    def body(x_ref, xp_ref, gate_ref, cw_ref, cb_ref, wa_ref, ba_ref, wx_ref, bx_ref, lam_ref, ycat_in,
             out_ref, hs_ref, hc):
        c = pl.program_id(1)

        @pl.when(c == 0)
        def _():
            hc[...] = jnp.zeros_like(hc)

        x = _conv_pre(xp_ref[...], x_ref[...], cw_ref, cb_ref, 4)
        r, i, sp, a, mult = _lru_gates(x, wa_ref, ba_ref, wx_ref, bx_ref, lam_ref)
        b = jnp.where(_real_rows(c), mult * (i * x), 0.0)
        rows = _row_ids(CH)
        s = 1
        while s < CH:
            a_s = jnp.where(rows >= s, pltpu.roll(a, s, axis=0), 1.0)
            b_s = jnp.where(rows >= s, pltpu.roll(b, s, axis=0), 0.0)
            b = a * b_s + b
            a = a * a_s
            s *= 2
        h = a * hc[0:1, :] + b
        hs_ref[...] = h
        hc[0:1, :] = hs_ref[CH - 1:CH, :]
        out_ref[...] = (h * _gelu(gate_ref[...])).astype(out_ref.dtype)

    row = lambda col: pl.BlockSpec((CH, 1024), lambda b, c: (b * nch + c, col))
    vec = pl.BlockSpec((1, 1024), lambda b, c: (0, 0))
    wsp = pl.BlockSpec((LRU_BLOCKS, 128, 128), lambda b, c: (0, 0, 0))
    return pl.pallas_call(
        body, name="lru_fwd", grid=(B, nch),
        in_specs=[row(4), pl.BlockSpec((8, 1024), _prev8_map(nch, 4)), row(3),
                  pl.BlockSpec((4, 1024), lambda b, c: (0, 0)), vec, wsp, vec, wsp, vec, vec,
                  pl.BlockSpec(memory_space=pl.ANY)],
        out_specs=[row(1), row(0)],
        out_shape=[jax.ShapeDtypeStruct(ycat.shape, ycat.dtype), jax.ShapeDtypeStruct((R, 1024), F32)],
        scratch_shapes=[pltpu.VMEM((8, 1024), F32)],
        input_output_aliases={10: 0},
        compiler_params=_cparams(("parallel", "arbitrary")),
    )(u1, u1, u1, conv_w, conv_b, wa, ba, wx, bx, lam, ycat)


def _lru_bwd(dycat, u1, hs, conv_w, conv_b, wa, ba, wx, bx, lam, B, nch):
    R = u1.shape[0]

    def body(dy_ref, x_ref, xp_ref, gate_ref, hs_ref, hsp_ref, cw_ref, cb_ref, wa_ref, ba_ref, wx_ref, bx_ref, lam_ref,
             dgate_ref, dxc_ref, pg_ref, dwa_ref, dwx_ref, lc):
        c = nch - 1 - pl.program_id(1)

        @pl.when(pl.program_id(1) == 0)
        def _():
            lc[...] = jnp.zeros_like(lc)
            pg_ref[...] = jnp.zeros_like(pg_ref)
            dwa_ref[...] = jnp.zeros_like(dwa_ref)
            dwx_ref[...] = jnp.zeros_like(dwx_ref)

        x = _conv_pre(xp_ref[...], x_ref[...], cw_ref, cb_ref, 4)
        r, i, sp, a, mult = _lru_gates(x, wa_ref, ba_ref, wx_ref, bx_ref, lam_ref)
        h = hs_ref[...]
        hprev = _shift_down(hsp_ref[...], h, 1)
        gate = gate_ref[...]
        dy = dy_ref[...]
        dgate_ref[...] = (dy * h * _dgelu(gate)).astype(dgate_ref.dtype)
        rows = _row_ids(CH)
        lam_t = dy * _gelu(gate) + jnp.where(rows == CH - 1, lc[0:1, :], 0.0)
        coef = jnp.where(rows < CH - 1, pltpu.roll(a, CH - 1, axis=0), 0.0)
        s = 1
        while s < CH:
            c_s = jnp.where(rows < CH - s, pltpu.roll(coef, CH - s, axis=0), 1.0)
            l_s = jnp.where(rows < CH - s, pltpu.roll(lam_t, CH - s, axis=0), 0.0)
            lam_t = coef * l_s + lam_t
            coef = coef * c_s
            s *= 2
        lc[0:1, :] = jnp.sum(jnp.where(rows == 0, a * lam_t, 0.0), axis=0, keepdims=True)
        db = jnp.where(_real_rows(c), lam_t, 0.0)
        da = db * hprev
        dmult = db * (i * x)
        di = db * mult * x
        dx = db * mult * i
        pos = mult > 0.0
        dla = da * a + jnp.where(pos, -dmult * (a * a) / jnp.where(pos, mult, 1.0), 0.0)
        dr = dla * (-LRU_C * sp)
        pg_ref[2:3, :] += jnp.sum(dla * (LRU_C * r) * _sigmoid(-lam_ref[...]), axis=0, keepdims=True)
        dpr = dr * r * (1.0 - r)
        dpi = di * i * (1.0 - i)
        pg_ref[0:1, :] += jnp.sum(dpr, axis=0, keepdims=True)
        pg_ref[1:2, :] += jnp.sum(dpi, axis=0, keepdims=True)
        dxs = []
        for n in range(LRU_BLOCKS):
            blk = slice(128 * n, 128 * n + 128)
            dxs.append(dx[:, blk] + _dot_nt(dpr[:, blk], wa_ref[n]) + _dot_nt(dpi[:, blk], wx_ref[n]))
            dwa_ref[n] += _dot_tn(x[:, blk], dpr[:, blk])
            dwx_ref[n] += _dot_tn(x[:, blk], dpi[:, blk])
        dxc_ref[...] = jnp.concatenate(dxs, axis=1)

    rmap = lambda col: (lambda b, c: (b * nch + nch - 1 - c, col))
    row = lambda col: pl.BlockSpec((CH, 1024), rmap(col))
    prev = lambda col: pl.BlockSpec(
        (8, 1024), lambda b, c: (jnp.maximum((b * nch + nch - 1 - c) * (CH // 8) - 1, 0), col))
    vec = pl.BlockSpec((1, 1024), lambda b, c: (0, 0))
    wsp = pl.BlockSpec((LRU_BLOCKS, 128, 128), lambda b, c: (0, 0, 0))
    full = jax.ShapeDtypeStruct((R, 1024), F32)
    return pl.pallas_call(
        body, name="lru_bwd", grid=(B, nch),
        in_specs=[row(1), row(4), prev(4), row(3), row(0), prev(0),
                  pl.BlockSpec((4, 1024), lambda b, c: (0, 0)), vec, wsp, vec, wsp, vec, vec],
        out_specs=[row(0), row(0), pl.BlockSpec((None, 8, 1024), lambda b, c: (b, 0, 0)),
                   pl.BlockSpec((None, LRU_BLOCKS, 128, 128), lambda b, c: (b, 0, 0, 0)),
                   pl.BlockSpec((None, LRU_BLOCKS, 128, 128), lambda b, c: (b, 0, 0, 0))],
        out_shape=[jax.ShapeDtypeStruct((R, 1024), _MXU), full, jax.ShapeDtypeStruct((B, 8, 1024), F32),
                   jax.ShapeDtypeStruct((B, LRU_BLOCKS, 128, 128), F32),
                   jax.ShapeDtypeStruct((B, LRU_BLOCKS, 128, 128), F32)],
        scratch_shapes=[pltpu.VMEM((8, 1024), F32)],
        compiler_params=_cparams(("parallel", "arbitrary")),
    )(dycat, u1, u1, u1, hs, hs, conv_w, conv_b, wa, ba, wx, bx, lam)


_FFN_TC = FFN // 2


def _ffn_specs(nch):
    nt = FFN // _FFN_TC
    row = lambda off: pl.BlockSpec((CH, _FFN_TC), lambda b, c, j: (b * nch + c, off + j))
    prev = lambda off: pl.BlockSpec(
        (8, _FFN_TC), lambda b, c, j: (jnp.maximum((b * nch + c) * (CH // 8) - 1, 0), off + j))
    wsp = lambda off: pl.BlockSpec((3, _FFN_TC), lambda b, c, j: (0, off + j))
    bsp = lambda off: pl.BlockSpec((1, _FFN_TC), lambda b, c, j: (0, off + j))
    return nt, row, [row(0), prev(0), row(nt), prev(nt), wsp(0), wsp(nt), bsp(0), bsp(nt)]


def _ffn_act_fwd(uf, conv_w, conv_b, B, nch, rider=None):
    R = uf.shape[0]
    nt, row, specs = _ffn_specs(nch)

    def body(g_ref, gp_ref, u_ref, up_ref, wg_ref, wu_ref, bg_ref, bu_ref, o_ref):
        cg = _conv_pre(gp_ref[...], g_ref[...], wg_ref, bg_ref, 3)
        cu = _conv_pre(up_ref[...], u_ref[...], wu_ref, bu_ref, 3)
        o_ref[...] = jnp.where(_real_rows(pl.program_id(1)), _silu(cg) * cu, 0.0).astype(o_ref.dtype)

    kw = dict(grid=(B, nch, nt), in_specs=specs, out_specs=[row(0)],
              out_shape=[jax.ShapeDtypeStruct((R, FFN), _MXU)])
    return _call(body, "ffn_act_fwd", ("arbitrary", "arbitrary", "arbitrary"), kw,
                 (uf, uf, uf, uf, conv_w, conv_w, conv_b, conv_b), rider)


def _ffn_act_bwd(da, uf, conv_w, conv_b, nch, name, rider=None):
    R = uf.shape[0]
    nt = FFN // _FFN_TC
    nr = R // CH
    K = 3

    def body(da_ref, dan_ref, g_ref, gp_ref, gn_ref, u_ref, up_ref, un_ref, wg_ref, wu_ref, bg_ref, bu_ref,
             dug_ref, duu_ref, dwg_ref, dwu_ref):
        i = pl.program_id(1)

        @pl.when(i == 0)
        def _():
            dwg_ref[...] = jnp.zeros_like(dwg_ref)
            dwu_ref[...] = jnp.zeros_like(dwu_ref)

        c = i % nch
        ext = CH + 8
        rows = _row_ids(ext)
        follows = (c < nch - 1).astype(jnp.int32)
        keep = jnp.logical_and(c * CH + rows >= PAD, rows < CH + 8 * follows)
        dav = jnp.where(keep, jnp.concatenate([da_ref[...], dan_ref[...]], axis=0), 0.0)

        def conv_ext(x_ref, xp_ref, xn_ref, w_ref, b_ref):
            cat = jnp.concatenate([xp_ref[...], x_ref[...], xn_ref[...]], axis=0)
            shifted = [cat[8:]] + [pltpu.roll(cat, s, axis=0)[8:] for s in range(1, K)]
            acc = shifted[0] * w_ref[K - 1:K, :] + b_ref[...]
            for s in range(1, K):
                acc = acc + shifted[s] * w_ref[K - 1 - s:K - s, :]
            return acc, shifted

        cg, gsh = conv_ext(g_ref, gp_ref, gn_ref, wg_ref, bg_ref)
        cu, ush = conv_ext(u_ref, up_ref, un_ref, wu_ref, bu_ref)
        sg = _sigmoid(cg)
        dcg = dav * cu * (sg * (1.0 + cg * (1.0 - sg)))
        dcu = dav * (cg * sg)
        for dc, xsh, w_ref, din_ref, dw_ref in ((dcg, gsh, wg_ref, dug_ref, dwg_ref), (dcu, ush, wu_ref, duu_ref, dwu_ref)):
            dp = dc[:CH]
            din = dp * w_ref[K - 1:K, :]
            dw_ref[7:8, :] += jnp.sum(dp, axis=0, keepdims=True)
            dw_ref[K - 1:K, :] += jnp.sum(dp * xsh[0][:CH], axis=0, keepdims=True)
            for s in range(1, K):
                din = din + pltpu.roll(dc, ext - s, axis=0)[:CH] * w_ref[K - 1 - s:K - s, :]
                dw_ref[K - 1 - s:K - s, :] += jnp.sum(dp * xsh[s][:CH], axis=0, keepdims=True)
            din_ref[...] = din.astype(din_ref.dtype)

    row = lambda off: pl.BlockSpec((CH, _FFN_TC), lambda j, i: (i, off + j))
    prev = lambda off: pl.BlockSpec((8, _FFN_TC), lambda j, i: (jnp.maximum(i * (CH // 8) - 1, 0), off + j))
    nxt = lambda off: pl.BlockSpec(
        (8, _FFN_TC), lambda j, i: (jnp.minimum((i + 1) * (CH // 8), nr * (CH // 8) - 1), off + j))
    wsp = lambda off: pl.BlockSpec((K, _FFN_TC), lambda j, i: (0, off + j))
    bsp = lambda off: pl.BlockSpec((1, _FFN_TC), lambda j, i: (0, off + j))
    acc = pl.BlockSpec((8, _FFN_TC), lambda j, i: (0, j))
    half = jax.ShapeDtypeStruct((R, FFN), _MXU)
    dwsh = jax.ShapeDtypeStruct((8, FFN), F32)
    kw = dict(
        grid=(nt, nr),
        in_specs=[row(0), nxt(0), row(0), prev(0), nxt(0), row(nt), prev(nt), nxt(nt), wsp(0), wsp(nt), bsp(0), bsp(nt)],
        out_specs=[row(0), row(0), acc, acc],
        out_shape=[half, half, dwsh, dwsh])
    return _call(body, name, ("arbitrary", "arbitrary"), kw,
                 (da, da, uf, uf, uf, uf, uf, uf, conv_w, conv_w, conv_b, conv_b), rider)


def _head(h, g, target, B, nch):
    R = h.shape[0]

    def body(h_ref, g_ref, t_ref, dh_ref, loss_ref, dg_ref):
        c = pl.program_id(1)

        @pl.when(c == 0)
        def _():
            dh_ref[...] = jnp.zeros_like(dh_ref)
            loss_ref[...] = jnp.zeros_like(loss_ref)
            dg_ref[...] = jnp.zeros_like(dg_ref)

        @pl.when(c > 0)
        def _():
            x = h_ref[...]
            gv = g_ref[...]
            r = lax.rsqrt(jnp.mean(x * x, axis=-1, keepdims=True) + EPS)
            xhat = x * r
            e = xhat * gv - t_ref[...]
            loss_ref[...] += 0.5 * jnp.sum(jnp.mean(e * e, axis=-1, keepdims=True), axis=0, keepdims=True)
            dy = e * (1.0 / D)
            dg_ref[0:1, :] += jnp.sum(dy * xhat, axis=0, keepdims=True)
            dx = dy * gv
            dh_ref[...] = r * (dx - xhat * jnp.mean(dx * xhat, axis=-1, keepdims=True))

    row = pl.BlockSpec((CH, D), lambda b, c: (b * nch + c, 0))
    return pl.pallas_call(
        body, name="head", grid=(B, nch),
        in_specs=[row, pl.BlockSpec((1, D), lambda b, c: (0, 0)),
                  pl.BlockSpec((CH, D), lambda b, c: (b * (nch - 1) + jnp.maximum(c - 1, 0), 0))],
        out_specs=[row, pl.BlockSpec((None, 8, 128), lambda b, c: (b, 0, 0)),
                   pl.BlockSpec((None, 8, D), lambda b, c: (b, 0, 0))],
        out_shape=[jax.ShapeDtypeStruct((R, D), F32), jax.ShapeDtypeStruct((B, 8, 128), F32),
                   jax.ShapeDtypeStruct((B, 8, D), F32)],
        compiler_params=_cparams(("parallel", "arbitrary")),
    )(h, g, target)


ADAM_LR = 0.001
ADAM_B1 = 0.9
ADAM_B2 = 0.999
ADAM_EPS = 1e-08
ADAM_WD = 0.01
ADAM_STEP = 10


def _adamw(w, g, m, v, name):
    Rr, C = w.shape
    tr = _tile(Rr, (256, 64))

    def body(w_ref, g_ref, m_ref, v_ref, d_ref, nm_ref, nv_ref):
        gv = g_ref[...]
        nm = ADAM_B1 * m_ref[...] + (1.0 - ADAM_B1) * gv
        nv = ADAM_B2 * v_ref[...] + (1.0 - ADAM_B2) * (gv * gv)
        m_hat = nm / (1.0 - ADAM_B1 ** ADAM_STEP)
        v_hat = nv / (1.0 - ADAM_B2 ** ADAM_STEP)
        d_ref[...] = -ADAM_LR * (m_hat / (jnp.sqrt(v_hat) + ADAM_EPS) + ADAM_WD * w_ref[...])
        nm_ref[...] = nm
        nv_ref[...] = nv

    spec = pl.BlockSpec((tr, C), lambda i: (i, 0))
    sh = jax.ShapeDtypeStruct((Rr, C), F32)
    return pl.pallas_call(
        body, name=name, grid=(Rr // tr,),
        in_specs=[spec] * 4, out_specs=[spec] * 3, out_shape=[sh] * 3,
        compiler_params=_cparams(("parallel",)),
    )(w, g, m, v)


_MESH = pl.DeviceIdType.MESH
_ANY = pl.BlockSpec(memory_space=pl.ANY)


def _place():
    x, y, c = lax.axis_index("x"), lax.axis_index("y"), lax.axis_index("c")
    chips = [(1 - x, y), (x, 1 - y), (1 - x, 1 - y)]
    return x, y, c, chips


def _rcopy(src, dst, ssem, rsem, dev):
    return pltpu.make_async_remote_copy(src_ref=src, dst_ref=dst, send_sem=ssem, recv_sem=rsem,
                                        device_id=dev, device_id_type=_MESH)


def _with_rider(body, kw, kind, rider):
    n_in, n_out, n_scr = len(kw["in_specs"]), len(kw["out_specs"]), len(kw.get("scratch_shapes", []))
    grid = kw["grid"]
    nsem = 4 if kind == "gather" else 2

    def new_body(*refs):
        ins, src = refs[:n_in], refs[n_in]
        outs, dst = refs[n_in + 1:n_in + 1 + n_out], refs[n_in + 1 + n_out]
        scr = refs[n_in + 2 + n_out:n_in + 2 + n_out + n_scr]
        sems = refs[n_in + 2 + n_out + n_scr:]
        first = last = None
        for axis, size in enumerate(grid):
            i = pl.program_id(axis)
            first = (i == 0) if first is None else jnp.logical_and(first, i == 0)
            last = (i == size - 1) if last is None else jnp.logical_and(last, i == size - 1)
        x, y, c, chips = _place()
        k = 2 * x + y
        sib = (x, y, 1 - c)
        if kind == "gather":
            ssem, rsem, fssem, frsem = sems
            sends = [_rcopy(src.at[c], dst.at[k, c], ssem.at[j], rsem.at[j], (cx, cy, c))
                     for j, (cx, cy) in enumerate(chips)]
        else:
            ssem, rsem = sems
            sends = [_rcopy(src.at[2 * cx + cy], dst.at[k], ssem.at[j], rsem.at[j], (cx, cy, c))
                     for j, (cx, cy) in enumerate(chips)]

        @pl.when(first)
        def _():
            for cp in sends:
                cp.start()

        body(*ins, *outs, *scr)

        @pl.when(last)
        def _():
            passed = []
            for j, (cx, cy) in enumerate(chips):
                got = dst.at[2 * cx + cy, c] if kind == "gather" else dst.at[2 * cx + cy]
                _rcopy(got, got, ssem.at[j], rsem.at[j], (cx, cy, c)).wait_recv()
                if kind == "gather":
                    fw = _rcopy(got, got, fssem.at[j], frsem.at[j], sib)
                    fw.start()
                    passed.append(fw)
            if kind == "gather":
                for j, (cx, cy) in enumerate(chips):
                    got = dst.at[2 * cx + cy, 1 - c]
                    _rcopy(got, got, fssem.at[j], frsem.at[j], sib).wait_recv()
            for cp in sends + passed:
                cp.wait_send()

    out_sh = ((4,) + rider.shape) if kind == "gather" else rider.shape
    kw = dict(kw)
    kw["in_specs"] = list(kw["in_specs"]) + [_ANY]
    kw["out_specs"] = list(kw["out_specs"]) + [_ANY]
    kw["out_shape"] = list(kw["out_shape"]) + [jax.ShapeDtypeStruct(out_sh, rider.dtype)]
    kw["scratch_shapes"] = list(kw.get("scratch_shapes", [])) + [pltpu.SemaphoreType.DMA((3,))] * nsem
    return new_body, kw


def _call(body, name, sem, kw, args, rider=None):
    if rider is not None:
        body, kw = _with_rider(body, kw, *rider)
        args = tuple(args) + (rider[1],)
    return pl.pallas_call(body, name=name, compiler_params=_cparams(sem), **kw)(*args)


def _fill_own(result, own, chip):
    return lax.dynamic_update_index_in_dim(result, own, chip, 0)


def _gather_shards(bigs, small):
    nb = len(bigs)

    def body(*refs):
        ins, outs = refs[:nb + 1], refs[nb + 1:2 * nb + 2]
        ssem, rsem, fssem, frsem = refs[2 * nb + 2:]
        x, y, c, chips = _place()
        k = 2 * x + y
        sib = (x, y, 1 - c)

        def part(a, slot, hc):
            return outs[a].at[slot] if a == nb else outs[a].at[slot, hc]

        first = []
        for a in range(nb + 1):
            src = ins[a] if a == nb else ins[a].at[c]
            for j, (cx, cy) in enumerate(chips):
                first.append(_rcopy(src, part(a, k, c), ssem.at[3 * a + j], rsem.at[3 * a + j], (cx, cy, c)))
        for cp in first:
            cp.start()
        passed = []
        for a in range(nb + 1):
            for j, (cx, cy) in enumerate(chips):
                got = part(a, 2 * cx + cy, c)
                _rcopy(got, got, ssem.at[3 * a + j], rsem.at[3 * a + j], (cx, cy, c)).wait_recv()
                if a < nb:
                    fw = _rcopy(got, got, fssem.at[3 * a + j], frsem.at[3 * a + j], sib)
                    fw.start()
                    passed.append(fw)
        for a in range(nb):
            for j, (cx, cy) in enumerate(chips):
                got = part(a, 2 * cx + cy, 1 - c)
                _rcopy(got, got, fssem.at[3 * a + j], frsem.at[3 * a + j], sib).wait_recv()
        for cp in first + passed:
            cp.wait_send()

    arrs = list(bigs) + [small]
    n = 3 * (nb + 1)
    return pl.pallas_call(
        body, name="gather_shards",
        in_specs=[_ANY] * (nb + 1), out_specs=[_ANY] * (nb + 1),
        out_shape=[jax.ShapeDtypeStruct((4,) + a.shape, a.dtype) for a in arrs],
        scratch_shapes=[pltpu.SemaphoreType.DMA((n,)), pltpu.SemaphoreType.DMA((n,)),
                        pltpu.SemaphoreType.DMA((n,)), pltpu.SemaphoreType.DMA((n,))],
    )(*arrs)


def _swap_halves(grads, name):
    na = len(grads)
    halves = [g.shape[1] // 2 for g in grads]

    def body(*refs):
        ins, outs = refs[:na], refs[na:2 * na]
        ssem, rsem = refs[2 * na:]
        x, y, c, _ = _place()
        sib = (x, y, 1 - c)
        cps = [_rcopy(ins[a].at[:, pl.ds((1 - c) * halves[a], halves[a]), :], outs[a], ssem.at[a], rsem.at[a], sib)
               for a in range(na)]
        for cp in cps:
            cp.start()
        for cp in cps:
            cp.wait()

    return pl.pallas_call(
        body, name=name,
        in_specs=[_ANY] * na, out_specs=[_ANY] * na,
        out_shape=[jax.ShapeDtypeStruct((4, g.shape[1] // 2, g.shape[2]), g.dtype) for g in grads],
        scratch_shapes=[pltpu.SemaphoreType.DMA((na,)), pltpu.SemaphoreType.DMA((na,))],
    )(*grads)


def _sum_rows(rh):
    return rh if rh <= 512 else _tile(rh, (512, 256, 128, 64, 32))


def _chip_sum(grad, recv, core, name):
    _, r, cdim = grad.shape
    rh = r // 2
    tr = _sum_rows(rh)
    nblk = rh // tr

    def body(core_ref, g_ref, r_ref, o_ref):
        o_ref[...] = (g_ref[...] + r_ref[...]).astype(o_ref.dtype)

    return pl.pallas_call(
        body, name=name,
        grid_spec=pltpu.PrefetchScalarGridSpec(
            num_scalar_prefetch=1, grid=(4, nblk),
            in_specs=[pl.BlockSpec((None, tr, cdim), lambda s, i, cr: (s, cr[0] * nblk + i, 0)),
                      pl.BlockSpec((None, tr, cdim), lambda s, i, cr: (s, i, 0))],
            out_specs=pl.BlockSpec((None, tr, cdim), lambda s, i, cr: (s, i, 0))),
        out_shape=jax.ShapeDtypeStruct((4, rh, cdim), BF16),
        compiler_params=_cparams(("parallel", "parallel")),
    )(core, grad, recv)


def _scatter_sums(sums):
    na = len(sums)

    def body(*refs):
        ins, outs = refs[:na], refs[na:2 * na]
        ssem, rsem, lsem = refs[2 * na:]
        x, y, c, chips = _place()
        k = 2 * x + y
        local = [pltpu.make_async_copy(ins[a].at[k], outs[a].at[k], lsem.at[a]) for a in range(na)]
        for cp in local:
            cp.start()
        cps = []
        for a in range(na):
            for j, (cx, cy) in enumerate(chips):
                cps.append(_rcopy(ins[a].at[2 * cx + cy], outs[a].at[k], ssem.at[3 * a + j], rsem.at[3 * a + j],
                                  (cx, cy, c)))
        for cp in cps:
            cp.start()
        for a in range(na):
            for j, (cx, cy) in enumerate(chips):
                got = outs[a].at[2 * cx + cy]
                _rcopy(got, got, ssem.at[3 * a + j], rsem.at[3 * a + j], (cx, cy, c)).wait_recv()
        for cp in cps:
            cp.wait_send()
        for cp in local:
            cp.wait()

    return pl.pallas_call(
        body, name="scatter_sums",
        in_specs=[_ANY] * na, out_specs=[_ANY] * na,
        out_shape=[jax.ShapeDtypeStruct(s.shape, s.dtype) for s in sums],
        scratch_shapes=[pltpu.SemaphoreType.DMA((3 * na,)), pltpu.SemaphoreType.DMA((3 * na,)),
                        pltpu.SemaphoreType.DMA((na,))],
    )(*sums)


def _sum_chips(parts, name):
    _, rh, cdim = parts.shape
    tr = _sum_rows(rh)

    def body(p_ref, o_ref):
        acc = p_ref[0].astype(F32)
        for j in range(1, 4):
            acc = acc + p_ref[j].astype(F32)
        o_ref[...] = acc

    return pl.pallas_call(
        body, name=name, grid=(rh // tr,),
        in_specs=[pl.BlockSpec((4, tr, cdim), lambda i: (0, i, 0))],
        out_specs=pl.BlockSpec((tr, cdim), lambda i: (i, 0)),
        out_shape=jax.ShapeDtypeStruct((rh, cdim), F32),
        compiler_params=_cparams(("parallel",)),
    )(parts)


def _join_halves(reds):
    na = len(reds)

    def body(*refs):
        ins, outs = refs[:na], refs[na:2 * na]
        ssem, rsem = refs[2 * na:]
        x, y, c, _ = _place()
        cps = [_rcopy(ins[a], outs[a], ssem.at[a], rsem.at[a], (x, y, 1 - c)) for a in range(na)]
        for cp in cps:
            cp.start()
        for cp in cps:
            cp.wait()

    return pl.pallas_call(
        body, name="join_halves",
        in_specs=[_ANY] * na, out_specs=[_ANY] * na,
        out_shape=[jax.ShapeDtypeStruct(r.shape, r.dtype) for r in reds],
        scratch_shapes=[pltpu.SemaphoreType.DMA((na,)), pltpu.SemaphoreType.DMA((na,))],
    )(*reds)


def _allreduce_small(buf):
    n = buf.shape[0]

    def body(in_ref, out_ref, recv, ssem, rsem):
        x, y, c, _ = _place()
        peers = [(x, y, 1 - c), (1 - x, y, c), (x, 1 - y, c)]
        out_ref[...] = in_ref[...]
        for r, peer in enumerate(peers):
            cp = _rcopy(out_ref, recv.at[r], ssem.at[r], rsem.at[r], peer)
            cp.start()
            cp.wait()
            out_ref[...] = out_ref[...] + recv[r]

    vm = pl.BlockSpec(memory_space=pltpu.VMEM)
    return pl.pallas_call(
        body, name="allreduce_small",
        in_specs=[vm], out_specs=vm,
        out_shape=jax.ShapeDtypeStruct(buf.shape, F32),
        scratch_shapes=[pltpu.VMEM((3, n, 128), F32), pltpu.SemaphoreType.DMA((3,)), pltpu.SemaphoreType.DMA((3,))],
        compiler_params=pltpu.CompilerParams(vmem_limit_bytes=VMEM_LIMIT),
    )(buf)


_W_NAMES = ['meta_tokens', 'l0_mix_norm', 'l0_w_in', 'l0_ssd_conv_w', 'l0_ssd_conv_b', 'l0_ssd_dt_bias', 'l0_ssd_a_log',
            'l0_ssd_d', 'l0_ssd_norm', 'l0_ret_norm', 'l0_w_out', 'l0_ffn_norm', 'l0_ffn_w_in', 'l0_ffn_conv_w',
            'l0_ffn_conv_b', 'l0_ffn_w_out', 'l1_mix_norm', 'l1_w_in', 'l1_lru_conv_w', 'l1_lru_conv_b', 'l1_lru_wa',
            'l1_lru_ba', 'l1_lru_wx', 'l1_lru_bx', 'l1_lru_lambda', 'l1_w_out', 'l1_ffn_norm', 'l1_ffn_w_in',
            'l1_ffn_conv_w', 'l1_ffn_conv_b', 'l1_ffn_w_out', 'final_norm']
_IN_NAMES = ['x'] + _W_NAMES + ['loss_target'] + ['m_' + n for n in _W_NAMES] + ['v_' + n for n in _W_NAMES]
_BIG = ['l0_w_in', 'l0_w_out', 'l0_ffn_w_in', 'l0_ffn_w_out', 'l1_w_in', 'l1_w_out', 'l1_ffn_w_in', 'l1_ffn_w_out']
_BIG_COLS = ('l0_w_in', 'l0_ffn_w_in', 'l1_w_in', 'l1_ffn_w_in')
_SMALL_SHARDED = ['meta_tokens', 'l0_ssd_conv_w', 'l0_ffn_conv_w', 'l1_lru_conv_w', 'l1_ffn_conv_w']
_SMALL = [n for n in _W_NAMES if n not in _BIG]


def _pack(arrs):
    flat = []
    for a in arrs:
        v = a.reshape(-1).astype(F32)
        flat.append(jnp.pad(v, (0, (-v.shape[0]) % 128)))
    v = jnp.concatenate(flat)
    v = jnp.pad(v, (0, (-v.shape[0]) % 1024))
    return v.reshape(-1, 128)


def _unpack(buf, shapes):
    out, row = [], 0
    for sh in shapes:
        n = int(np.prod(sh))
        rows = -(-n // 128)
        out.append(buf[row:row + rows].reshape(-1)[:n].reshape(sh))
        row += rows
    return out


def kernel(x, meta_tokens, l0_mix_norm, l0_w_in, l0_ssd_conv_w, l0_ssd_conv_b, l0_ssd_dt_bias, l0_ssd_a_log, l0_ssd_d, l0_ssd_norm, l0_ret_norm, l0_w_out, l0_ffn_norm, l0_ffn_w_in, l0_ffn_conv_w, l0_ffn_conv_b, l0_ffn_w_out, l1_mix_norm, l1_w_in, l1_lru_conv_w, l1_lru_conv_b, l1_lru_wa, l1_lru_ba, l1_lru_wx, l1_lru_bx, l1_lru_lambda, l1_w_out, l1_ffn_norm, l1_ffn_w_in, l1_ffn_conv_w, l1_ffn_conv_b, l1_ffn_w_out, final_norm, loss_target, m_meta_tokens, m_l0_mix_norm, m_l0_w_in, m_l0_ssd_conv_w, m_l0_ssd_conv_b, m_l0_ssd_dt_bias, m_l0_ssd_a_log, m_l0_ssd_d, m_l0_ssd_norm, m_l0_ret_norm, m_l0_w_out, m_l0_ffn_norm, m_l0_ffn_w_in, m_l0_ffn_conv_w, m_l0_ffn_conv_b, m_l0_ffn_w_out, m_l1_mix_norm, m_l1_w_in, m_l1_lru_conv_w, m_l1_lru_conv_b, m_l1_lru_wa, m_l1_lru_ba, m_l1_lru_wx, m_l1_lru_bx, m_l1_lru_lambda, m_l1_w_out, m_l1_ffn_norm, m_l1_ffn_w_in, m_l1_ffn_conv_w, m_l1_ffn_conv_b, m_l1_ffn_w_out, m_final_norm, v_meta_tokens, v_l0_mix_norm, v_l0_w_in, v_l0_ssd_conv_w, v_l0_ssd_conv_b, v_l0_ssd_dt_bias, v_l0_ssd_a_log, v_l0_ssd_d, v_l0_ssd_norm, v_l0_ret_norm, v_l0_w_out, v_l0_ffn_norm, v_l0_ffn_w_in, v_l0_ffn_conv_w, v_l0_ffn_conv_b, v_l0_ffn_w_out, v_l1_mix_norm, v_l1_w_in, v_l1_lru_conv_w, v_l1_lru_conv_b, v_l1_lru_wa, v_l1_lru_ba, v_l1_lru_wx, v_l1_lru_bx, v_l1_lru_lambda, v_l1_w_out, v_l1_ffn_norm, v_l1_ffn_w_in, v_l1_ffn_conv_w, v_l1_ffn_conv_b, v_l1_ffn_w_out, v_final_norm):
    args = (x, meta_tokens, l0_mix_norm, l0_w_in, l0_ssd_conv_w, l0_ssd_conv_b, l0_ssd_dt_bias, l0_ssd_a_log, l0_ssd_d, l0_ssd_norm, l0_ret_norm, l0_w_out, l0_ffn_norm, l0_ffn_w_in, l0_ffn_conv_w, l0_ffn_conv_b, l0_ffn_w_out, l1_mix_norm, l1_w_in, l1_lru_conv_w, l1_lru_conv_b, l1_lru_wa, l1_lru_ba, l1_lru_wx, l1_lru_bx, l1_lru_lambda, l1_w_out, l1_ffn_norm, l1_ffn_w_in, l1_ffn_conv_w, l1_ffn_conv_b, l1_ffn_w_out, final_norm, loss_target, m_meta_tokens, m_l0_mix_norm, m_l0_w_in, m_l0_ssd_conv_w, m_l0_ssd_conv_b, m_l0_ssd_dt_bias, m_l0_ssd_a_log, m_l0_ssd_d, m_l0_ssd_norm, m_l0_ret_norm, m_l0_w_out, m_l0_ffn_norm, m_l0_ffn_w_in, m_l0_ffn_conv_w, m_l0_ffn_conv_b, m_l0_ffn_w_out, m_l1_mix_norm, m_l1_w_in, m_l1_lru_conv_w, m_l1_lru_conv_b, m_l1_lru_wa, m_l1_lru_ba, m_l1_lru_wx, m_l1_lru_bx, m_l1_lru_lambda, m_l1_w_out, m_l1_ffn_norm, m_l1_ffn_w_in, m_l1_ffn_conv_w, m_l1_ffn_conv_b, m_l1_ffn_w_out, m_final_norm, v_meta_tokens, v_l0_mix_norm, v_l0_w_in, v_l0_ssd_conv_w, v_l0_ssd_conv_b, v_l0_ssd_dt_bias, v_l0_ssd_a_log, v_l0_ssd_d, v_l0_ssd_norm, v_l0_ret_norm, v_l0_w_out, v_l0_ffn_norm, v_l0_ffn_w_in, v_l0_ffn_conv_w, v_l0_ffn_conv_b, v_l0_ffn_w_out, v_l1_mix_norm, v_l1_w_in, v_l1_lru_conv_w, v_l1_lru_conv_b, v_l1_lru_wa, v_l1_lru_ba, v_l1_lru_wx, v_l1_lru_bx, v_l1_lru_lambda, v_l1_w_out, v_l1_ffn_norm, v_l1_ffn_w_in, v_l1_ffn_conv_w, v_l1_ffn_conv_b, v_l1_ffn_w_out, v_final_norm)
    p = dict(zip(_IN_NAMES, args))
    B, seq, _ = x.shape
    nch = (seq + CH) // CH
    Pn = nch * CH
    R = B * Pn
    chip = 2 * lax.axis_index("x") + lax.axis_index("y")
    row2 = lambda v: v.reshape(1, -1)
    pad128 = lambda v: jnp.pad(v, (0, 128 - v.shape[0])).reshape(1, 128)

    small_shapes = [p[n].shape for n in _SMALL_SHARDED]
    halved = lambda w: w.astype(_MXU).reshape(2, w.shape[0] // 2, w.shape[1])
    mine = {n: halved(p[n]) for n in _BIG}
    mine_small = _pack([p[n] for n in _SMALL_SHARDED])
    W = {}

    def set_weight(n, g):
        g = _fill_own(g, mine[n], chip)
        g = g.reshape(4, -1, g.shape[3])
        W[n] = jnp.concatenate([g[k] for k in range(4)], axis=1) if n in _BIG_COLS else g.reshape(-1, g.shape[2])

    gathered = _gather_shards([mine[n] for n in _BIG[:4]], mine_small)
    for n, g in zip(_BIG[:4], gathered[:-1]):
        set_weight(n, g)
    g_small = _fill_own(gathered[-1], mine_small, chip)
    per_chip = [_unpack(g_small[k], small_shapes) for k in range(4)]
    for i, n in enumerate(_SMALL_SHARDED):
        W[n] = jnp.concatenate([per_chip[k][i] for k in range(4)], axis=1)
    w0 = W['l0_w_in']
    w0_main = jnp.concatenate([w0[:, 3088:], w0[:, :3072]], axis=1)
    w0_dt = jnp.pad(w0[:, 3072:3088], ((0, 0), (0, 112)))
    cos, sin = _rope_tables(nch)

    meta = jnp.broadcast_to(W['meta_tokens'][None], (B, N_META, D))
    h0 = jnp.concatenate([jnp.zeros((B, PAD, D), F32), meta, x], axis=1).reshape(R, D)
    n0, n0t = _rmsnorm_fwd(h0, row2(p['l0_mix_norm']), "norm_l0_mix")
    u0 = _mm(n0, w0_main, "nn", F32, "l0_in_proj")
    udt = _mm(n0, w0_dt, "nn", F32, "l0_dt_proj")
    a_log, d_skip, dt_bias = pad128(p['l0_ssd_a_log']), pad128(p['l0_ssd_d']), pad128(p['l0_ssd_dt_bias'])
    ssd_cb = row2(p['l0_ssd_conv_b'])
    act, dt, dtt, g = _ssd_prep(u0, udt, W['l0_ssd_conv_w'], ssd_cb, dt_bias, B, nch,
                                rider=("gather", mine['l1_w_out']))
    set_weight('l1_w_out', g)
    ycat0, ypre, hin, g = _ssd_fwd(act, u0, dt, dtt, a_log, d_skip, row2(p['l0_ssd_norm']), B, nch,
                                   rider=("gather", mine['l1_ffn_w_in']))
    set_weight('l1_ffn_w_in', g)
    ycat0, opre, rin, g = _ret_fwd(u0, ycat0, cos, sin, row2(p['l0_ret_norm']), B, nch,
                                   rider=("gather", mine['l1_ffn_w_out']))
    set_weight('l1_ffn_w_out', g)
    h1 = _mm(ycat0, W['l0_w_out'], "nn", F32, "l0_out_proj", add=h0)
    n1, n1t = _rmsnorm_fwd(h1, row2(p['l0_ffn_norm']), "norm_l0_ffn")
    uf0 = _mm(n1, W['l0_ffn_w_in'], "nn", F32, "l0_ffn_in")
    f0_cb = row2(p['l0_ffn_conv_b'])
    a0, g = _ffn_act_fwd(uf0, W['l0_ffn_conv_w'], f0_cb, B, nch, rider=("gather", mine['l1_w_in']))
    set_weight('l1_w_in', g)
    h2 = _mm(a0, W['l0_ffn_w_out'], "nn", F32, "l0_ffn_out", add=h1)
    n2, n2t = _rmsnorm_fwd(h2, row2(p['l1_mix_norm']), "norm_l1_mix")
    u1 = _mm(n2, W['l1_w_in'], "nn", F32, "l1_in_proj")
    lru = (W['l1_lru_conv_w'], row2(p['l1_lru_conv_b']), p['l1_lru_wa'], row2(p['l1_lru_ba']), p['l1_lru_wx'],
           row2(p['l1_lru_bx']), row2(p['l1_lru_lambda']))
    ycat1, stot = _sb_fwd(u1, B, nch)
    ycat1, hs = _lru_fwd(u1, ycat1, *lru, B, nch)
    h3 = _mm(ycat1, W['l1_w_out'], "nn", F32, "l1_out_proj", add=h2)
    n3, n3t = _rmsnorm_fwd(h3, row2(p['l1_ffn_norm']), "norm_l1_ffn")
    uf1 = _mm(n3, W['l1_ffn_w_in'], "nn", F32, "l1_ffn_in")
    f1_cb = row2(p['l1_ffn_conv_b'])
    a1, = _ffn_act_fwd(uf1, W['l1_ffn_conv_w'], f1_cb, B, nch)
    h4 = _mm(a1, W['l1_ffn_w_out'], "nn", F32, "l1_ffn_out", add=h3)
    dh4, lossp, dgf = _head(h4, row2(p['final_norm']), p['loss_target'].reshape(B * seq, D), B, nch)
    loss = lax.psum(jnp.sum(lossp[:, 0, 0]), ("x", "y", "c"))

    G = {'final_norm': dgf[:, 0].sum(0)}

    core = lax.axis_index("c").reshape(1).astype(jnp.int32)

    def chip_sums(names, tag):
        stacked = []
        for n in names:
            g = G[n]
            if n in _BIG_COLS:
                stacked.append(g.reshape(g.shape[0], 4, g.shape[1] // 4).transpose(1, 0, 2))
            else:
                stacked.append(g.reshape(4, g.shape[0] // 4, g.shape[1]))
        theirs = _swap_halves(stacked, "swap_halves_" + tag)
        return {n: _chip_sum(g, t, core, "chip_sum_" + n) for n, g, t in zip(names, stacked, theirs)}

    parts = {}

    def set_part(n, got, sums):
        parts[n] = _fill_own(got, lax.dynamic_index_in_dim(sums[n], chip, 0, keepdims=False), chip)

    def ffn_bwd(layer, dh_out, h_in, nt_in, uf, a_act, cb, rider=None):
        pre = f"l{layer}_"
        w_in, w_out, cw = W[pre + 'ffn_w_in'], W[pre + 'ffn_w_out'], W[pre + 'ffn_conv_w']
        da = _mm(dh_out, w_out, "nt", F32, pre + "ffn_out_dgrad")
        G[pre + 'ffn_w_out'] = _mm(a_act, dh_out, "tn", F32, pre + "ffn_out_wgrad")
        dug, duu, dwg, dwu, *rode = _ffn_act_bwd(da, uf, cw, cb, nch, pre + "ffn_act_bwd", rider=rider)
        G[pre + 'ffn_conv_w'] = jnp.concatenate([dwg[:3], dwu[:3]], axis=1)
        G[pre + 'ffn_conv_b'] = jnp.concatenate([dwg[7], dwu[7]])
        dn = _mm(dug, w_in, "nt", F32, pre + "ffn_in_dgrad_g")
        dn = _mm(duu, w_in, "nt", F32, pre + "ffn_in_dgrad_u", add=dn, b_off=FFN)
        G[pre + 'ffn_w_in'] = jnp.concatenate([_mm(nt_in, dug, "nn", F32, pre + "ffn_in_wgrad_g"),
                                               _mm(nt_in, duu, "nn", F32, pre + "ffn_in_wgrad_u")], axis=1)
        dh_in, dg = _rmsnorm_bwd(h_in, row2(p[pre + 'ffn_norm']), dn, dh_out, nch, pre + "ffn_norm_bwd")
        G[pre + 'ffn_norm'] = dg[0]
        return dh_in, rode

    dh3, _ = ffn_bwd(1, dh4, h3, n3t, uf1, a1, f1_cb)
    dy1 = _mm(dh3, W['l1_w_out'], "nt", F32, "l1_out_dgrad")
    G['l1_w_out'] = _mm(ycat1, dh3, "tn", F32, "l1_out_wgrad")
    dq, dk, dv = _sb_bwd(dy1, u1, stot, B, nch)
    dgate, dxc, pgl, dwa, dwx = _lru_bwd(dy1, u1, hs, *lru, B, nch)
    dxr, dcw = _conv_bwd(dxc, u1, 4096, W['l1_lru_conv_w'], 4, "l1_lru_conv_bwd")
    pgl = pgl.sum(0)
    G['l1_lru_ba'], G['l1_lru_bx'], G['l1_lru_lambda'] = pgl[0], pgl[1], pgl[2]
    G['l1_lru_wa'], G['l1_lru_wx'] = dwa.sum(0), dwx.sum(0)
    G['l1_lru_conv_w'], G['l1_lru_conv_b'] = dcw[:4], dcw[7]
    dn, dws = None, []
    for i, piece in enumerate((dq, dk, dv, dgate, dxr)):
        dn = _mm(piece, W['l1_w_in'], "nt", F32, f"l1_in_dgrad_{i}", add=dn, b_off=1024 * i)
        dws.append(_mm(n2t, piece, "nn", F32, f"l1_in_wgrad_{i}"))
    G['l1_w_in'] = jnp.concatenate(dws, axis=1)
    dh2, dg = _rmsnorm_bwd(h2, row2(p['l1_mix_norm']), dn, dh3, nch, "l1_mix_norm_bwd")
    G['l1_mix_norm'] = dg[0]

    sums1 = chip_sums(_BIG[4:], "l1")
    dh1, (got,) = ffn_bwd(0, dh2, h1, n1t, uf0, a0, f0_cb, rider=("scatter", sums1['l1_ffn_w_in']))
    set_part('l1_ffn_w_in', got, sums1)
    dy0 = _mm(dh1, W['l0_w_out'], "nt", F32, "l0_out_dgrad")
    G['l0_w_out'] = _mm(ycat0, dh1, "tn", F32, "l0_out_wgrad")
    dz, dxs, dbm, dcm, ddt4, pgs, got = _ssd_bwd(dy0, ypre, u0, act, dt, dtt, hin, a_log, d_skip, row2(p['l0_ssd_norm']),
                                                 B, nch, rider=("scatter", sums1['l1_w_in']))
    set_part('l1_w_in', got, sums1)
    dpre, ddtr, pgd, got = _ssd_prep_bwd(dxs, dbm, dcm, ddt4, u0, udt, W['l0_ssd_conv_w'], ssd_cb, dt_bias, B, nch,
                                         rider=("scatter", sums1['l1_w_out']))
    set_part('l1_w_out', got, sums1)
    dxbc, dcw0 = _conv_bwd(dpre, u0, U0_XBC, W['l0_ssd_conv_w'], 4, "l0_ssd_conv_bwd")
    dqkvg, pgr, got = _ret_bwd(dy0, u0, opre, rin, cos, sin, row2(p['l0_ret_norm']), B, nch,
                               rider=("scatter", sums1['l1_ffn_w_out']))
    set_part('l1_ffn_w_out', got, sums1)
    pgs = pgs.sum(0)
    G['l0_ssd_norm'] = pgs[:, 0, :].reshape(-1)
    G['l0_ssd_d'] = pgs[:, 1, :128].sum(0)[:SSD_HEADS]
    G['l0_ssd_a_log'] = pgs[:, 2, :128].sum(0)[:SSD_HEADS]
    G['l0_ssd_dt_bias'] = pgd.sum(0)[0, :SSD_HEADS]
    G['l0_ssd_conv_w'], G['l0_ssd_conv_b'] = dcw0[:4], dcw0[7]
    G['l0_ret_norm'] = pgr.sum(0)[0]
    dn = _mm(dqkvg, w0_main, "nt", F32, "l0_in_dgrad_qkvg")
    dn = _mm(dz, w0_main, "nt", F32, "l0_in_dgrad_z", add=dn, b_off=U0_Z)
    dn = _mm(dxbc, w0_main, "nt", F32, "l0_in_dgrad_xbc", add=dn, b_off=U0_XBC)
    dn = _mm(ddtr, w0_dt, "nt", F32, "l0_in_dgrad_dt", add=dn)
    G['l0_w_in'] = jnp.concatenate([
        _mm(n0t, dz, "nn", F32, "l0_in_wgrad_z"), _mm(n0t, dxbc, "nn", F32, "l0_in_wgrad_xbc"),
        _mm(n0t, ddtr, "nn", F32, "l0_in_wgrad_dt")[:, :SSD_HEADS], _mm(n0t, dqkvg, "nn", F32, "l0_in_wgrad_qkvg")], axis=1)
    dh0, dg = _rmsnorm_bwd(h0, row2(p['l0_mix_norm']), dn, dh1, nch, "l0_mix_norm_bwd")
    G['l0_mix_norm'] = dg[0]
    dh0 = dh0.reshape(B, Pn, D)
    grad_x = dh0[:, CH:]
    G['meta_tokens'] = dh0[:, PAD:CH].sum(0)

    sums0 = chip_sums(_BIG[:4], "l0")
    parts.update(zip(_BIG[:4], _scatter_sums([sums0[n] for n in _BIG[:4]])))
    reds = [_sum_chips(parts[n], "sum_chips_" + n) for n in _BIG]
    grads = {}
    for n, own, other in zip(_BIG, reds, _join_halves(reds)):
        both = jnp.where(core[0] == 0, jnp.stack([own, other]), jnp.stack([other, own]))
        grads[n] = both.reshape(-1, both.shape[2])
    small_full = _unpack(_allreduce_small(_pack([G[n] for n in _SMALL])), [G[n].shape for n in _SMALL])
    for n, g in zip(_SMALL, small_full):
        if n in _SMALL_SHARDED:
            cs = g.shape[1] // 4
            g = lax.dynamic_slice_in_dim(g, chip * cs, cs, axis=1)
        grads[n] = g.reshape(p[n].shape)

    delta, new_m, new_v = {}, {}, {}
    for n in _BIG:
        delta[n], new_m[n], new_v[n] = _adamw(p[n], grads[n], p['m_' + n], p['v_' + n], "adamw_" + n)
    shapes = [p[n].shape for n in _SMALL]
    outs = _adamw(_pack([p[n] for n in _SMALL]), _pack([grads[n] for n in _SMALL]), _pack([p['m_' + n] for n in _SMALL]),
                  _pack([p['v_' + n] for n in _SMALL]), "adamw_small")
    for dst, buf in zip((delta, new_m, new_v), outs):
        for n, a in zip(_SMALL, _unpack(buf, shapes)):
            dst[n] = a
    return (loss, grad_x, *[grads[n] for n in _W_NAMES], *[delta[n] for n in _W_NAMES],
            *[new_m[n] for n in _W_NAMES], *[new_v[n] for n in _W_NAMES])
```

```python
import math

import numpy as np
import jax
import jax.numpy as jnp
from jax import lax
from jax.experimental import pallas as pl
from jax.experimental.pallas import tpu as pltpu

F32 = jnp.float32
BF16 = jnp.bfloat16
_MXU = jnp.bfloat16

D = 1024
CH = 128
N_META = 16
PAD = CH - N_META
EPS = 1e-6

SSD_HEADS = 16
SSD_HD = 64
SSD_GROUPS = 4
RET_HEADS = 4
RET_DK = 256
SB_HEADS = 16
SB_HD = 64
LRU_BLOCKS = 8
LRU_C = 8.0
FFN = 2816
U0_Z = 4096
U0_XBC = 5120

VMEM_LIMIT = 56 * 1024 * 1024


def _cparams(sem):
    return pltpu.CompilerParams(dimension_semantics=sem, vmem_limit_bytes=VMEM_LIMIT)


def _dot(a, b, dims=((1,), (0,))):
    return lax.dot_general(a.astype(_MXU), b.astype(_MXU), (dims, ((), ())), preferred_element_type=F32)


def _dot_nt(a, b):
    return _dot(a, b, ((1,), (1,)))


def _dot_tn(a, b):
    return _dot(a.T, b)


def _dot_exact(a, b):
    return lax.dot_general(a, b, (((1,), (0,)), ((), ())), preferred_element_type=F32,
                           precision=lax.Precision.HIGHEST)


def _dot_split(x, m01):
    hi = x.astype(BF16)
    lo = (x - hi.astype(F32)).astype(BF16)
    m = m01.astype(BF16)
    return jnp.dot(hi, m, preferred_element_type=F32) + jnp.dot(lo, m, preferred_element_type=F32)


def _sigmoid(x):
    return 0.5 * jnp.tanh(0.5 * x) + 0.5


def _softplus(x):
    return jnp.maximum(x, 0.0) + jnp.log1p(jnp.exp(-jnp.abs(x)))


def _silu(x):
    return x * _sigmoid(x)


def _dsilu(x):
    s = _sigmoid(x)
    return s * (1.0 + x * (1.0 - s))


_GELU_C = math.sqrt(2.0 / math.pi)


def _gelu(x):
    return 0.5 * x * (1.0 + jnp.tanh(_GELU_C * (x + 0.044715 * x * x * x)))


def _dgelu(x):
    t = jnp.tanh(_GELU_C * (x + 0.044715 * x * x * x))
    return 0.5 * (1.0 + t) + 0.5 * x * (1.0 - t * t) * _GELU_C * (1.0 + 3.0 * 0.044715 * x * x)


def _row_ids(n, cols=1):
    return lax.broadcasted_iota(jnp.int32, (n, cols), 0)


def _lane_ids(rows, n):
    return lax.broadcasted_iota(jnp.int32, (rows, n), 1)


def _real_rows(chunk):
    return chunk * CH + _row_ids(CH) >= PAD


def _shift_down(prev8, cur, s):
    cat = jnp.concatenate([prev8, cur], axis=0)
    return pltpu.roll(cat, s, axis=0)[8:]


def _shift_up(cur, next8, s):
    n = cur.shape[0]
    cat = jnp.concatenate([cur, next8], axis=0)
    return pltpu.roll(cat, n + 8 - s, axis=0)[:n]


def _conv_pre(prev8, cur, w_ref, b_ref, K):
    acc = cur * w_ref[K - 1:K, :] + b_ref[...]
    for s in range(1, K):
        acc = acc + _shift_down(prev8, cur, s) * w_ref[K - 1 - s:K - s, :]
    return acc


def _prev8_map(nch, col):
    return lambda b, c: (jnp.maximum((b * nch + c) * (CH // 8) - 1, 0), col)


def _matmul(a, b, mode, out_dtype, tm, tn, tk, name, add=None, b_off=0):
    if mode == "nn":
        (M, K), (_, N) = a.shape, b.shape
    elif mode == "nt":
        (M, K), N = a.shape, b.shape[0]
    else:
        (K, M), (_, N) = a.shape, b.shape
    tm, tn, tk = min(tm, M), min(tn, N), min(tk, K)
    assert M % tm == 0 and N % tn == 0 and K % tk == 0 and b_off % tk == 0, (name, M, N, K, tm, tn, tk)
    koff = b_off // tk
    nk = K // tk
    dims = {"nn": ((1,), (0,)), "nt": ((1,), (1,)), "tn": ((0,), (0,))}[mode]
    if mode == "tn":
        a_spec = pl.BlockSpec((tk, tm), lambda i, j, k: (k, i))
    else:
        a_spec = pl.BlockSpec((tm, tk), lambda i, j, k: (i, k))
    if mode == "nt":
        b_spec = pl.BlockSpec((tn, tk), lambda i, j, k: (j, k + koff))
    else:
        b_spec = pl.BlockSpec((tk, tn), lambda i, j, k: (k, j))
    o_spec = pl.BlockSpec((tm, tn), lambda i, j, k: (i, j))
    has_add = add is not None

    def body(a_ref, b_ref, *rest):
        if has_add:
            add_ref, o_ref, acc = rest
        else:
            o_ref, acc = rest
        k = pl.program_id(2)

        @pl.when(k == 0)
        def _():
            acc[...] = jnp.zeros_like(acc)

        acc[...] += _dot(a_ref[...], b_ref[...], dims)

        @pl.when(k == nk - 1)
        def _():
            r = acc[...]
            if has_add:
                r = r + add_ref[...].astype(F32)
            o_ref[...] = r.astype(out_dtype)

    in_specs = [a_spec, b_spec] + ([o_spec] if has_add else [])
    args = (a, b) + ((add,) if has_add else ())
    return pl.pallas_call(
        body, name=name, grid=(M // tm, N // tn, nk),
        in_specs=in_specs, out_specs=o_spec,
        out_shape=jax.ShapeDtypeStruct((M, N), out_dtype),
        scratch_shapes=[pltpu.VMEM((tm, tn), F32)],
        compiler_params=_cparams(("parallel", "parallel", "arbitrary")),
    )(*args)


def _tile(n, prefs):
    for t in prefs:
        if n % t == 0:
            return t
    return n


def _mm(a, b, mode, out_dtype, name, add=None, b_off=0):
    if mode == "tn":
        K, M = a.shape
        N = b.shape[1]
        tm, tn, tk = _tile(M, (1024, 1408, 512, 256, 128)), _tile(N, (1024, 1408, 512, 256, 128)), _tile(K, (2176, 384, 256, 128))
    else:
        M, K = a.shape
        N = b.shape[1] if mode == "nn" else b.shape[0]
        tm = _tile(M, (1088, 1024, 768, 512, 384, 256, 128))
        tn = _tile(N, (1024, 1408, 512, 256, 128))
        tk = _tile(K, (2176, 1024, 1408, 512, 256, 128))
    return _matmul(a, b, mode, out_dtype, tm, tn, tk, name, add=add, b_off=b_off)


def _rmsnorm_fwd(h, g, name):
    R = h.shape[0]
    tr = 2 * CH

    def body(h_ref, g_ref, o_ref, ot_ref):
        x = h_ref[...]
        r = lax.rsqrt(jnp.mean(x * x, axis=-1, keepdims=True) + EPS)
        y = x * r * g_ref[...]
        o_ref[...] = y.astype(o_ref.dtype)
        ot_ref[...] = y.T.astype(ot_ref.dtype)

    return pl.pallas_call(
        body, name=name, grid=(R // tr,),
        in_specs=[pl.BlockSpec((tr, D), lambda i: (i, 0)), pl.BlockSpec((1, D), lambda i: (0, 0))],
        out_specs=[pl.BlockSpec((tr, D), lambda i: (i, 0)), pl.BlockSpec((D, tr), lambda i: (0, i))],
        out_shape=[jax.ShapeDtypeStruct((R, D), _MXU), jax.ShapeDtypeStruct((D, R), _MXU)],
        compiler_params=_cparams(("parallel",)),
    )(h, g)


def _rmsnorm_bwd(h, g, dn, dres, nch, name):
    R = h.shape[0]

    def body(h_ref, g_ref, dn_ref, dres_ref, dh_ref, dg_ref):
        i = pl.program_id(0)
        x = h_ref[...]
        r = lax.rsqrt(jnp.mean(x * x, axis=-1, keepdims=True) + EPS)
        xhat = x * r
        dn_v = dn_ref[...]
        dx = dn_v * g_ref[...]
        dh = r * (dx - xhat * jnp.mean(dx * xhat, axis=-1, keepdims=True))
        dh_ref[...] = jnp.where(_real_rows(i % nch), dres_ref[...] + dh, 0.0)

        @pl.when(i == 0)
        def _():
            dg_ref[...] = jnp.zeros_like(dg_ref)

        dg_ref[...] += jnp.sum(dn_v * xhat, axis=0, keepdims=True)

    row = pl.BlockSpec((CH, D), lambda i: (i, 0))
    vec = pl.BlockSpec((1, D), lambda i: (0, 0))
    return pl.pallas_call(
        body, name=name, grid=(R // CH,),
        in_specs=[row, vec, row, row], out_specs=[row, vec],
        out_shape=[jax.ShapeDtypeStruct((R, D), F32), jax.ShapeDtypeStruct((1, D), F32)],
        compiler_params=_cparams(("arbitrary",)),
    )(h, g, dn, dres)


def _ssd_prep(u0, udt, conv_w, conv_b, dt_bias, B, nch, rider=None):
    R = u0.shape[0]

    def body(xs_ref, xsp_ref, bc_ref, bcp_ref, udt_ref, w0_ref, w1_ref, b0_ref, b1_ref, dtb_ref,
             act_ref, dt_ref, dtt_ref):
        keep = _real_rows(pl.program_id(1))
        a0 = _silu(_conv_pre(xsp_ref[...], xs_ref[...], w0_ref, b0_ref, 4))
        a1 = _silu(_conv_pre(bcp_ref[...], bc_ref[...], w1_ref, b1_ref, 4))
        act_ref[:, :1024] = jnp.where(keep, a0, 0.0)
        act_ref[:, 1024:] = jnp.where(keep, a1, 0.0)
        ok = jnp.logical_and(keep, _lane_ids(1, 128) < SSD_HEADS)
        dt = jnp.where(ok, _softplus(udt_ref[...] + dtb_ref[...]), 0.0)
        dt_ref[...] = dt
        dtt_ref[...] = dt.T

    row = lambda col: pl.BlockSpec((CH, 1024), lambda b, c: (b * nch + c, col))
    prev = lambda col: pl.BlockSpec((8, 1024), _prev8_map(nch, col))
    kw = dict(
        grid=(B, nch),
        in_specs=[row(5), prev(5), row(6), prev(6),
                  pl.BlockSpec((CH, 128), lambda b, c: (b * nch + c, 0)),
                  pl.BlockSpec((4, 1024), lambda b, c: (0, 0)), pl.BlockSpec((4, 1024), lambda b, c: (0, 1)),
                  pl.BlockSpec((1, 1024), lambda b, c: (0, 0)), pl.BlockSpec((1, 1024), lambda b, c: (0, 1)),
                  pl.BlockSpec((1, 128), lambda b, c: (0, 0))],
        out_specs=[pl.BlockSpec((CH, 2048), lambda b, c: (b * nch + c, 0)),
                   pl.BlockSpec((CH, 128), lambda b, c: (b * nch + c, 0)),
                   pl.BlockSpec((128, CH), lambda b, c: (0, b * nch + c))],
        out_shape=[jax.ShapeDtypeStruct((R, 2048), F32), jax.ShapeDtypeStruct((R, 128), F32),
                   jax.ShapeDtypeStruct((128, R), F32)])
    return _call(body, "ssd_prep", ("arbitrary", "arbitrary"), kw,
                 (u0, u0, u0, u0, udt, conv_w, conv_w, conv_b, conv_b, dt_bias), rider)


def _ssd_head_terms(h, a_vec, dt_v, dtt_v, dsk_v):
    lane = _lane_ids(1, 128)
    sub = _row_ids(128)
    r = _row_ids(CH, CH)
    cidx = _lane_ids(CH, CH)
    a_h = jnp.sum(jnp.where(lane == h, a_vec, 0.0), axis=1, keepdims=True)
    dt_col = jnp.sum(jnp.where(lane == h, dt_v, 0.0), axis=1, keepdims=True)
    dt_row = jnp.sum(jnp.where(sub == h, dtt_v, 0.0), axis=0, keepdims=True)
    cs_col = jnp.sum(jnp.where(r >= cidx, dt_row * a_h, 0.0), axis=1, keepdims=True)
    cs_row = jnp.sum(jnp.where(r <= cidx, dt_col * a_h, 0.0), axis=0, keepdims=True)
    tot = jnp.sum(dt_col * a_h, axis=0, keepdims=True)
    dsk = jnp.sum(jnp.where(lane == h, dsk_v, 0.0), axis=1, keepdims=True)
    return a_h, dt_col, cs_col, cs_row, tot, dsk


def _ssd_fwd(act, u0, dt, dtt, a_log, d_skip, norm_g, B, nch, rider=None):
    R = act.shape[0]

    def body(xs_ref, bm_ref, cm_ref, z_ref, dt_ref, dtt_ref, alog_ref, dsk_ref, ng_ref,
             out_ref, ypre_ref, hin_ref, H):
        g = pl.program_id(1)
        c = pl.program_id(2)

        @pl.when(c == 0)
        def _():
            H[...] = jnp.zeros_like(H)

        hin_ref[...] = H[...]
        a_vec = -jnp.exp(alog_ref[...])
        dt_v = dt_ref[...]
        dtt_v = dtt_ref[...]
        hm = _lane_ids(1, 128) < SSD_HD
        r = _row_ids(CH, CH)
        cidx = _lane_ids(CH, CH)
        Bm = bm_ref[...]
        Cm = cm_ref[...]
        CB = _dot_nt(Cm, Bm)
        ys = []
        for pair in range(2):
            cols = slice(128 * pair, 128 * pair + 128)
            xraw = xs_ref[:, cols]
            t = [_ssd_head_terms(4 * g + 2 * pair + j, a_vec, dt_v, dtt_v, dsk_ref[...]) for j in range(2)]
            sel = lambda f: jnp.where(hm, f(t[0]), f(t[1]))
            dtp = sel(lambda q: q[1])
            Ep = sel(lambda q: jnp.exp(q[2]))
            Wp = sel(lambda q: jnp.exp(q[4] - q[2]))
            etot = sel(lambda q: jnp.exp(q[4]))
            dsk = sel(lambda q: q[5])
            X = xraw * dtp
            ydiag = jnp.zeros((CH, 128), F32)
            for j in range(2):
                Lm = jnp.where(r >= cidx, jnp.exp(t[j][2] - t[j][3]), 0.0)
                Xh = jnp.where(hm if j == 0 else jnp.logical_not(hm), X, 0.0)
                ydiag = ydiag + _dot(CB * Lm, Xh)
            Hp = H[:, cols]
            yoff = Ep * _dot(Cm, Hp)
            S = _dot(Bm.T, X * Wp)
            H[:, cols] = etot * Hp + S
            ys.append(ydiag + yoff + xraw * dsk)
        y = jnp.concatenate(ys, axis=1)
        ypre_ref[...] = y
        yg = y * _silu(z_ref[...])
        rr = lax.rsqrt(jnp.mean(yg * yg, axis=-1, keepdims=True) + EPS)
        out_ref[...] = jnp.where(_real_rows(c), yg * rr * ng_ref[...], 0.0).astype(out_ref.dtype)

    rowb = lambda w, colf: pl.BlockSpec((CH, w), lambda b, g, c: (b * nch + c, colf(g)))
    vec = pl.BlockSpec((1, 128), lambda b, g, c: (0, 0))
    kw = dict(
        grid=(B, SSD_GROUPS, nch),
        in_specs=[rowb(256, lambda g: g), rowb(128, lambda g: 8 + g), rowb(128, lambda g: 12 + g),
                  rowb(256, lambda g: 16 + g), rowb(128, lambda g: 0),
                  pl.BlockSpec((128, CH), lambda b, g, c: (0, b * nch + c)),
                  vec, vec, pl.BlockSpec((1, 256), lambda b, g, c: (0, g))],
        out_specs=[rowb(256, lambda g: g), rowb(256, lambda g: g),
                   pl.BlockSpec((None, None, None, 128, 256), lambda b, g, c: (b, g, c, 0, 0))],
        out_shape=[jax.ShapeDtypeStruct((R, 2048), _MXU), jax.ShapeDtypeStruct((R, 1024), F32),
                   jax.ShapeDtypeStruct((B, SSD_GROUPS, nch, 128, 256), F32)],
        scratch_shapes=[pltpu.VMEM((128, 256), F32)])
    return _call(body, "ssd_fwd", ("arbitrary", "arbitrary", "arbitrary"), kw,
                 (act, act, act, u0, dt, dtt, a_log, d_skip, norm_g), rider)


def _ssd_bwd(dycat, ypre, u0, act, dt, dtt, hin, a_log, d_skip, norm_g, B, nch, rider=None):
    R = act.shape[0]

    def body(dy_ref, ypre_ref, z_ref, xs_ref, bm_ref, cm_ref, dt_ref, dtt_ref, hin_ref, alog_ref, dsk_ref, ng_ref,
             dz_ref, dxs_ref, db_ref, dc_ref, ddt_ref, pg_ref, dH):
        g = pl.program_id(1)
        c = nch - 1 - pl.program_id(2)

        @pl.when(pl.program_id(2) == 0)
        def _():
            dH[...] = jnp.zeros_like(dH)
            pg_ref[...] = jnp.zeros_like(pg_ref)

        z = z_ref[...]
        y = ypre_ref[...]
        ng = ng_ref[...]
        dout = jnp.where(_real_rows(c), dy_ref[...], 0.0)
        sz = _sigmoid(z)
        yg = y * z * sz
        rr = lax.rsqrt(jnp.mean(yg * yg, axis=-1, keepdims=True) + EPS)
        nrm = yg * rr
        pg_ref[0:1, :] += jnp.sum(dout * nrm, axis=0, keepdims=True)
        dn = dout * ng
        dyg = rr * (dn - nrm * jnp.mean(dn * nrm, axis=-1, keepdims=True))
        dy = dyg * z * sz
        dz_ref[...] = (dyg * y * (sz * (1.0 + z * (1.0 - sz)))).astype(dz_ref.dtype)

        a_vec = -jnp.exp(alog_ref[...])
        dt_v = dt_ref[...]
        dtt_v = dtt_ref[...]
        lane = _lane_ids(1, 128)
        hm = lane < SSD_HD
        r = _row_ids(CH, CH)
        cidx = _lane_ids(CH, CH)
        last = _row_ids(CH) == CH - 1
        Bm = bm_ref[...]
        Cm = cm_ref[...]
        CB = _dot_nt(Cm, Bm)
        CBT = _dot_nt(Bm, Cm)
        dB = jnp.zeros((CH, 128), F32)
        dC = jnp.zeros((CH, 128), F32)
        dcs_all = jnp.zeros((CH, 128), F32)
        dtx_all = jnp.zeros((CH, 128), F32)
        dd_row = jnp.zeros((1, 128), F32)
        dxs = []
        for pair in range(2):
            cols = slice(128 * pair, 128 * pair + 128)
            xraw = xs_ref[:, cols]
            dyp = dy[:, cols]
            heads = [4 * g + 2 * pair + j for j in range(2)]
            t = [_ssd_head_terms(heads[j], a_vec, dt_v, dtt_v, dsk_ref[...]) for j in range(2)]
            sel = lambda f: jnp.where(hm, f(t[0]), f(t[1]))
            hsum = lambda v, j: jnp.sum(jnp.where(hm if j == 0 else jnp.logical_not(hm), v, 0.0), axis=1, keepdims=True)
            dtp = sel(lambda q: q[1])
            Ep = sel(lambda q: jnp.exp(q[2]))
            Wp = sel(lambda q: jnp.exp(q[4] - q[2]))
            etot = sel(lambda q: jnp.exp(q[4]))
            dsk = sel(lambda q: q[5])
            X = xraw * dtp
            Hp = hin_ref[:, cols]
            dHn = dH[:, cols]
            dskip = jnp.sum(dyp * xraw, axis=0, keepdims=True)
            yoff = Ep * _dot(Cm, Hp)
            dE = dyp * yoff
            dC = dC + _dot_nt(dyp * Ep, Hp)
            dH[:, cols] = etot * dHn + _dot(Cm.T, dyp * Ep)
            BdS = _dot(Bm, dHn)
            dX = Wp * BdS
            ew = X * BdS * Wp
            dB = dB + _dot_nt(X * Wp, dHn)
            hh = jnp.sum(dHn * Hp, axis=0, keepdims=True) * etot
            for j in range(2):
                hmask = hm if j == 0 else jnp.logical_not(hm)
                cs_col, cs_row = t[j][2], t[j][3]
                Lm = jnp.where(r >= cidx, jnp.exp(cs_col - cs_row), 0.0)
                LmT = jnp.where(cidx >= r, jnp.exp(cs_row - cs_col), 0.0)
                dyh = jnp.where(hmask, dyp, 0.0)
                Xh = jnp.where(hmask, X, 0.0)
                dM = _dot_nt(dyh, Xh)
                dMT = _dot_nt(Xh, dyh)
                M = CB * Lm
                MT = CBT * LmT
                dX = dX + _dot(MT, dyh)
                dC = dC + _dot(dM * Lm, Bm)
                dB = dB + _dot(dMT * LmT, Cm)
                g_rows = jnp.sum(dM * M, axis=1, keepdims=True)
                g_cols = jnp.sum(dMT * MT, axis=1, keepdims=True)
                dtot = (jnp.sum(hsum(ew, j), axis=0, keepdims=True)
                        + jnp.sum(jnp.where(hmask, hh, 0.0), axis=1, keepdims=True))
                dcs = g_rows - g_cols + hsum(dE, j) - hsum(ew, j) + jnp.where(last, dtot, 0.0)
                dcs_all = dcs_all + jnp.where(lane == heads[j], dcs, 0.0)
                dtx_all = dtx_all + jnp.where(lane == heads[j], hsum(dX * xraw, j), 0.0)
                dd_row = dd_row + jnp.where(lane == heads[j],
                                            jnp.sum(jnp.where(hmask, dskip, 0.0), axis=1, keepdims=True), 0.0)
            dxs.append(dX * dtp + dyp * dsk)
        dxs_ref[...] = jnp.concatenate(dxs, axis=1)
        db_ref[...] = dB
        dc_ref[...] = dC
        dadt = _dot_exact(jnp.where(cidx >= r, 1.0, 0.0), dcs_all)
        ddt_ref[...] = dadt * a_vec + dtx_all
        pg_ref[1:2, 0:128] += dd_row
        pg_ref[2:3, 0:128] += jnp.sum(dadt * dt_v, axis=0, keepdims=True) * a_vec

    rowb = lambda w, colf: pl.BlockSpec((CH, w), lambda b, g, c: (b * nch + nch - 1 - c, colf(g)))
    vec = pl.BlockSpec((1, 128), lambda b, g, c: (0, 0))
    kw = dict(
        grid=(B, SSD_GROUPS, nch),
        in_specs=[rowb(256, lambda g: g), rowb(256, lambda g: g), rowb(256, lambda g: 16 + g), rowb(256, lambda g: g),
                  rowb(128, lambda g: 8 + g), rowb(128, lambda g: 12 + g), rowb(128, lambda g: 0),
                  pl.BlockSpec((128, CH), lambda b, g, c: (0, b * nch + nch - 1 - c)),
                  pl.BlockSpec((None, None, None, 128, 256), lambda b, g, c: (b, g, nch - 1 - c, 0, 0)),
                  vec, vec, pl.BlockSpec((1, 256), lambda b, g, c: (0, g))],
        out_specs=[rowb(256, lambda g: g), rowb(256, lambda g: g), rowb(128, lambda g: g), rowb(128, lambda g: g),
                   rowb(128, lambda g: g),
                   pl.BlockSpec((None, None, 8, 256), lambda b, g, c: (b, g, 0, 0))],
        out_shape=[jax.ShapeDtypeStruct((R, 1024), _MXU), jax.ShapeDtypeStruct((R, 1024), F32),
                   jax.ShapeDtypeStruct((R, 512), F32), jax.ShapeDtypeStruct((R, 512), F32),
                   jax.ShapeDtypeStruct((R, 512), F32), jax.ShapeDtypeStruct((B, SSD_GROUPS, 8, 256), F32)],
        scratch_shapes=[pltpu.VMEM((128, 256), F32)])
    return _call(body, "ssd_bwd", ("arbitrary", "arbitrary", "arbitrary"), kw,
                 (dycat, ypre, u0, act, act, act, dt, dtt, hin, a_log, d_skip, norm_g), rider)


def _ssd_prep_bwd(dxs, dB, dC, ddt4, u0, udt, conv_w, conv_b, dt_bias, B, nch, rider=None):
    R = u0.shape[0]

    def body(dxs_ref, db_ref, dc_ref, ddt_ref, xs_ref, xsp_ref, bc_ref, bcp_ref, udt_ref, w0_ref, w1_ref, b0_ref, b1_ref,
             dtb_ref, dpre_ref, ddtr_ref, pgd_ref):
        c = pl.program_id(1)

        @pl.when(c == 0)
        def _():
            pgd_ref[...] = jnp.zeros_like(pgd_ref)

        keep = _real_rows(c)
        p0 = _conv_pre(xsp_ref[...], xs_ref[...], w0_ref, b0_ref, 4)
        p1 = _conv_pre(bcp_ref[...], bc_ref[...], w1_ref, b1_ref, 4)
        dpre_ref[:, :1024] = jnp.where(keep, dxs_ref[...] * _dsilu(p0), 0.0)
        dpre_ref[:, 1024:] = jnp.where(keep, jnp.concatenate([db_ref[...], dc_ref[...]], axis=1) * _dsilu(p1), 0.0)
        ddt = ddt_ref[:, 0:128] + ddt_ref[:, 128:256] + ddt_ref[:, 256:384] + ddt_ref[:, 384:512]
        ok = jnp.logical_and(keep, _lane_ids(1, 128) < SSD_HEADS)
        dr = jnp.where(ok, ddt * _sigmoid(udt_ref[...] + dtb_ref[...]), 0.0)
        ddtr_ref[...] = dr
        pgd_ref[0:1, :] += jnp.sum(dr, axis=0, keepdims=True)

    rw = lambda w: pl.BlockSpec((CH, w), lambda b, c: (b * nch + c, 0))
    row = lambda col: pl.BlockSpec((CH, 1024), lambda b, c: (b * nch + c, col))
    prev = lambda col: pl.BlockSpec((8, 1024), _prev8_map(nch, col))
    kw = dict(
        grid=(B, nch),
        in_specs=[rw(1024), rw(512), rw(512), rw(512), row(5), prev(5), row(6), prev(6), rw(128),
                  pl.BlockSpec((4, 1024), lambda b, c: (0, 0)), pl.BlockSpec((4, 1024), lambda b, c: (0, 1)),
                  pl.BlockSpec((1, 1024), lambda b, c: (0, 0)), pl.BlockSpec((1, 1024), lambda b, c: (0, 1)),
                  pl.BlockSpec((1, 128), lambda b, c: (0, 0))],
        out_specs=[rw(2048), rw(128), pl.BlockSpec((None, 8, 128), lambda b, c: (b, 0, 0))],
        out_shape=[jax.ShapeDtypeStruct((R, 2048), F32), jax.ShapeDtypeStruct((R, 128), F32),
                   jax.ShapeDtypeStruct((B, 8, 128), F32)])
    return _call(body, "ssd_prep_bwd", ("arbitrary", "arbitrary"), kw,
                 (dxs, dB, dC, ddt4, u0, u0, u0, u0, udt, conv_w, conv_w, conv_b, conv_b, dt_bias), rider)


def _conv_bwd(dpre, xin, xin_col, w, K, name, tc=1024):
    R, C = dpre.shape
    assert C % tc == 0 and xin_col % tc == 0
    nr = R // CH
    xoff = xin_col // tc

    def body(dp_ref, dpn_ref, x_ref, xp_ref, w_ref, din_ref, dw_ref):
        i = pl.program_id(1)

        @pl.when(i == 0)
        def _():
            dw_ref[...] = jnp.zeros_like(dw_ref)

        dp = dp_ref[...]
        nxt = dpn_ref[...] * (i < nr - 1).astype(F32)
        x = x_ref[...]
        xp = xp_ref[...]
        din = dp * w_ref[K - 1:K, :]
        dw_ref[K - 1:K, :] += jnp.sum(dp * x, axis=0, keepdims=True)
        dw_ref[7:8, :] += jnp.sum(dp, axis=0, keepdims=True)
        for s in range(1, K):
            din = din + _shift_up(dp, nxt, s) * w_ref[K - 1 - s:K - s, :]
            dw_ref[K - 1 - s:K - s, :] += jnp.sum(dp * _shift_down(xp, x, s), axis=0, keepdims=True)
        din_ref[...] = din.astype(din_ref.dtype)

    return pl.pallas_call(
        body, name=name, grid=(C // tc, nr),
        in_specs=[pl.BlockSpec((CH, tc), lambda j, i: (i, j)),
                  pl.BlockSpec((8, tc), lambda j, i: (jnp.minimum((i + 1) * (CH // 8), nr * (CH // 8) - 1), j)),
                  pl.BlockSpec((CH, tc), lambda j, i: (i, xoff + j)),
                  pl.BlockSpec((8, tc), lambda j, i: (jnp.maximum(i * (CH // 8) - 1, 0), xoff + j)),
                  pl.BlockSpec((K, tc), lambda j, i: (0, j))],
        out_specs=[pl.BlockSpec((CH, tc), lambda j, i: (i, j)),
                   pl.BlockSpec((8, tc), lambda j, i: (0, j))],
        out_shape=[jax.ShapeDtypeStruct((R, C), _MXU), jax.ShapeDtypeStruct((8, C), F32)],
        compiler_params=_cparams(("parallel", "arbitrary")),
    )(dpre, dpre, xin, xin, w)


_RET_LG = [float(v) for v in np.log1p(-np.exp2(-5.0 - np.arange(RET_HEADS, dtype=np.float32))).astype(np.float32)]
_RET_SCALE = RET_DK ** -0.5


def _rope_tables(nch):
    half = RET_DK // 2
    inv_freq = 1.0 / (10000.0 ** (jnp.arange(half, dtype=F32) / (half - 1)))
    pos = jnp.arange(nch * CH, dtype=F32) - PAD
    ang = pos[:, None] * inv_freq[None, :]
    return jnp.cos(ang), jnp.sin(ang)


def _rot(x, cos, sin):
    x1, x2 = x[:, :128], x[:, 128:]
    return jnp.concatenate([x1 * cos - x2 * sin, x1 * sin + x2 * cos], axis=1)


def _unrot(d, cos, sin):
    d1, d2 = d[:, :128], d[:, 128:]
    return jnp.concatenate([d1 * cos + d2 * sin, d2 * cos - d1 * sin], axis=1)


def _ret_decays(lg):
    r = _row_ids(CH, CH)
    cidx = _lane_ids(CH, CH)
    diff = (r - cidx).astype(F32)
    decay = jnp.where(r >= cidx, jnp.exp(lg * jnp.maximum(diff, 0.0)), 0.0)
    decay_t = jnp.where(cidx >= r, jnp.exp(lg * jnp.maximum(-diff, 0.0)), 0.0)
    idx = _row_ids(CH).astype(F32)
    zeta = jnp.exp(lg * (CH - 1.0 - idx))
    xi = jnp.exp(lg * (idx + 1.0))
    return decay, decay_t, zeta, xi


def _ret_fwd(u0, ycat, cos, sin, norm_g, B, nch, rider=None):
    R = u0.shape[0]

    def body(u_ref, cos_ref, sin_ref, ng_ref, ycat_in, out_ref, opre_ref, rin_ref, Rst):
        c = pl.program_id(1)

        @pl.when(c == 0)
        def _():
            Rst[...] = jnp.zeros_like(Rst)

        cos_v, sin_v = cos_ref[...], sin_ref[...]
        for h in range(RET_HEADS):
            lg = _RET_LG[h]
            cols = slice(256 * h, 256 * h + 256)
            decay, _, zeta, xi = _ret_decays(lg)
            qr = _rot(u_ref[:, cols], cos_v, sin_v)
            kr = _rot(u_ref[:, 1024 + 256 * h:1024 + 256 * h + 256], cos_v, sin_v) * _RET_SCALE
            v = u_ref[:, 2048 + 256 * h:2048 + 256 * h + 256]
            gate = u_ref[:, 3072 + 256 * h:3072 + 256 * h + 256]
            Rh = Rst[h]
            rin_ref[h] = Rh
            inner = _dot(_dot_nt(qr, kr) * decay, v)
            cross = _dot(qr, Rh) * xi
            Rst[h] = math.exp(CH * lg) * Rh + _dot((kr * zeta).T, v)
            o = inner + cross
            opre_ref[:, cols] = o
            oc = o - jnp.mean(o, axis=-1, keepdims=True)
            rr = lax.rsqrt(jnp.mean(oc * oc, axis=-1, keepdims=True) + EPS)
            out_ref[:, cols] = (_silu(gate) * (oc * rr * ng_ref[:, cols])).astype(out_ref.dtype)

    kw = dict(
        grid=(B, nch),
        in_specs=[pl.BlockSpec((CH, 4096), lambda b, c: (b * nch + c, 0)),
                  pl.BlockSpec((CH, 128), lambda b, c: (c, 0)), pl.BlockSpec((CH, 128), lambda b, c: (c, 0)),
                  pl.BlockSpec((1, 1024), lambda b, c: (0, 0)),
                  pl.BlockSpec(memory_space=pl.ANY)],
        out_specs=[pl.BlockSpec((CH, 1024), lambda b, c: (b * nch + c, 1)),
                   pl.BlockSpec((CH, 1024), lambda b, c: (b * nch + c, 0)),
                   pl.BlockSpec((None, None, RET_HEADS, 256, 256), lambda b, c: (b, c, 0, 0, 0))],
        out_shape=[jax.ShapeDtypeStruct(ycat.shape, ycat.dtype), jax.ShapeDtypeStruct((R, 1024), F32),
                   jax.ShapeDtypeStruct((B, nch, RET_HEADS, 256, 256), F32)],
        scratch_shapes=[pltpu.VMEM((RET_HEADS, 256, 256), F32)],
        input_output_aliases={4: 0})
    return _call(body, "ret_fwd", ("arbitrary", "arbitrary"), kw, (u0, cos, sin, norm_g, ycat), rider)


def _ret_bwd(dycat, u0, opre, rin, cos, sin, norm_g, B, nch, rider=None):
    R = u0.shape[0]

    def body(dy_ref, u_ref, opre_ref, rin_ref, cos_ref, sin_ref, ng_ref, du_ref, pg_ref, dR):
        @pl.when(pl.program_id(1) == 0)
        def _():
            dR[...] = jnp.zeros_like(dR)
            pg_ref[...] = jnp.zeros_like(pg_ref)

        cos_v, sin_v = cos_ref[...], sin_ref[...]
        for h in range(RET_HEADS):
            lg = _RET_LG[h]
            cols = slice(256 * h, 256 * h + 256)
            decay, decay_t, zeta, xi = _ret_decays(lg)
            qr = _rot(u_ref[:, cols], cos_v, sin_v)
            kr = _rot(u_ref[:, 1024 + 256 * h:1024 + 256 * h + 256], cos_v, sin_v) * _RET_SCALE
            v = u_ref[:, 2048 + 256 * h:2048 + 256 * h + 256]
            gate = u_ref[:, 3072 + 256 * h:3072 + 256 * h + 256]
            ng = ng_ref[:, cols]
            o = opre_ref[:, cols]
            oc = o - jnp.mean(o, axis=-1, keepdims=True)
            rr = lax.rsqrt(jnp.mean(oc * oc, axis=-1, keepdims=True) + EPS)
            ohat = oc * rr
            dout = dy_ref[:, cols]
            du_ref[:, 3072 + 256 * h:3072 + 256 * h + 256] = (dout * (ohat * ng) * _dsilu(gate)).astype(du_ref.dtype)
            don = dout * _silu(gate)
            pg_ref[0:1, cols] += jnp.sum(don * ohat, axis=0, keepdims=True)
            dohat = don * ng
            do = rr * (dohat - jnp.mean(dohat, axis=-1, keepdims=True)
                       - ohat * jnp.mean(dohat * ohat, axis=-1, keepdims=True))
            Rh = rin_ref[h]
            dRn = dR[h]
            sc_t = _dot_nt(kr, qr) * decay_t
            dv = _dot(sc_t, do) + _dot(kr * zeta, dRn)
            ds = _dot_nt(do, v) * decay
            ds_t = _dot_nt(v, do) * decay_t
            dox = do * xi
            dq = _dot(ds, kr) + _dot_nt(dox, Rh)
            dk = _dot(ds_t, qr) + zeta * _dot_nt(v, dRn)
            dR[h] = math.exp(CH * lg) * dRn + _dot(qr.T, dox)
            du_ref[:, cols] = _unrot(dq, cos_v, sin_v).astype(du_ref.dtype)
            du_ref[:, 1024 + 256 * h:1024 + 256 * h + 256] = (_unrot(dk, cos_v, sin_v) * _RET_SCALE).astype(du_ref.dtype)
            du_ref[:, 2048 + 256 * h:2048 + 256 * h + 256] = dv.astype(du_ref.dtype)

    rmap = lambda b, c: (b * nch + nch - 1 - c, 0)
    kw = dict(
        grid=(B, nch),
        in_specs=[pl.BlockSpec((CH, 1024), lambda b, c: (b * nch + nch - 1 - c, 1)),
                  pl.BlockSpec((CH, 4096), rmap), pl.BlockSpec((CH, 1024), rmap),
                  pl.BlockSpec((None, None, RET_HEADS, 256, 256), lambda b, c: (b, nch - 1 - c, 0, 0, 0)),
                  pl.BlockSpec((CH, 128), lambda b, c: (nch - 1 - c, 0)),
                  pl.BlockSpec((CH, 128), lambda b, c: (nch - 1 - c, 0)),
                  pl.BlockSpec((1, 1024), lambda b, c: (0, 0))],
        out_specs=[pl.BlockSpec((CH, 4096), rmap), pl.BlockSpec((None, 8, 1024), lambda b, c: (b, 0, 0))],
        out_shape=[jax.ShapeDtypeStruct((R, 4096), _MXU), jax.ShapeDtypeStruct((B, 8, 1024), F32)],
        scratch_shapes=[pltpu.VMEM((RET_HEADS, 256, 256), F32)])
    return _call(body, "ret_bwd", ("arbitrary", "arbitrary"), kw, (dycat, u0, opre, rin, cos, sin, norm_g), rider)


_SB_SCALE = SB_HD ** -0.5


_SB_NB = 4


def _sb_valid(qb, kb, live):
    qpos = qb * CH + jnp.bitwise_and(_row_ids(2 * CH, CH), CH - 1)
    kpos = kb * CH + _lane_ids(2 * CH, CH)
    first = PAD + (1 - live) * (1 << 24)
    return jnp.logical_and(kpos < qpos, kpos >= first)


def _sb_softplus(z):
    return jnp.maximum(z, 0.0) + jnp.log(1.0 + jnp.exp(-jnp.abs(z)))


def _stack_heads(x):
    hm = _lane_ids(1, 128) < SB_HD
    return jnp.concatenate([jnp.where(hm, x, 0.0), jnp.where(hm, 0.0, x)], axis=0)


def _unstack_heads(x2):
    return jnp.where(_lane_ids(1, 128) < SB_HD, x2[:CH], x2[CH:])


def _sb_fwd(u1, B, nch, rider=None):
    R = u1.shape[0]
    Pn = nch * CH

    def body(q_ref, k_ref, v_ref, out_ref, s_ref):
        qb = pl.program_id(2)
        q2 = _stack_heads(q_ref[...] * _SB_SCALE).astype(_MXU)
        mgt = (_row_ids(CH, CH) > _lane_ids(CH, CH)).astype(F32)

        def step(i, carry):
            out2, acc = carry
            blocks = []
            for t in range(_SB_NB):
                kb = qb - _SB_NB * i - t
                live = (kb >= 0).astype(jnp.int32)
                kbc = jnp.maximum(kb, 0)
                start = pl.multiple_of(kbc * CH, CH)
                valid = _sb_valid(qb, kbc, live)
                z = _dot_nt(q2, k_ref[pl.ds(start, CH), :])
                sp = _sb_softplus(z)
                lm = jnp.where(valid, -sp, 0.0)
                blocks.append((valid, z - sp, _dot_split(lm, mgt), jnp.sum(lm, axis=1, keepdims=True), start))
            for valid, ls, loc, rs, start in blocks:
                w = jnp.where(valid, jnp.exp(ls + loc + acc), 0.0)
                out2 = out2 + _dot(w, v_ref[pl.ds(start, CH), :])
                acc = acc + rs
            return out2, acc

        trips = (qb + _SB_NB) // _SB_NB
        out2, acc = lax.fori_loop(0, trips, step, (jnp.zeros((2 * CH, 128), F32), jnp.zeros((2 * CH, 1), F32)))
        out_ref[...] = _unstack_heads(out2).astype(out_ref.dtype)
        s_ref[...] = _unstack_heads(jnp.broadcast_to(acc, (2 * CH, 128)))

    qspec = lambda off: pl.BlockSpec((CH, 128), lambda b, hp, qb: (b * nch + qb, off + hp))
    kspec = lambda off: pl.BlockSpec((Pn, 128), lambda b, hp, qb: (b, off + hp))
    kw = dict(grid=(B, SB_HEADS // 2, nch), in_specs=[qspec(0), kspec(8), kspec(16)], out_specs=[qspec(0), qspec(0)],
              out_shape=[jax.ShapeDtypeStruct((R, 2048), _MXU), jax.ShapeDtypeStruct((R, 1024), F32)])
    return _call(body, "sb_fwd", ("arbitrary", "arbitrary", "arbitrary"), kw, (u1, u1, u1), rider)


def _sb_bwd(dycat, u1, stot, B, nch, rider=None):
    R = u1.shape[0]
    Pn = nch * CH

    def body(q_ref, k_ref, v_ref, do_ref, s_ref, dq_ref, dk_ref, dv_ref):
        qb = pl.program_id(2)

        @pl.when(qb == 0)
        def _():
            dk_ref[...] = jnp.zeros_like(dk_ref)
            dv_ref[...] = jnp.zeros_like(dv_ref)

        q2 = _stack_heads(q_ref[...] * _SB_SCALE).astype(_MXU)
        do2 = _stack_heads(do_ref[...]).astype(_MXU)
        stv = s_ref[...]
        lane = _lane_ids(1, 128)
        s2 = jnp.concatenate([jnp.sum(jnp.where(lane == 0, stv, 0.0), axis=1, keepdims=True),
                              jnp.sum(jnp.where(lane == SB_HD, stv, 0.0), axis=1, keepdims=True)], axis=0)
        rr = _row_ids(CH, CH)
        cc = _lane_ids(CH, CH)
        mle = (rr <= cc).astype(F32)
        mlt = (rr < cc).astype(F32)

        def step(i, carry):
            dq2, pacc, gacc = carry
            blocks = []
            for t in range(_SB_NB):
                kb = _SB_NB * i + t
                live = (kb <= qb).astype(jnp.int32)
                start = pl.multiple_of(jnp.minimum(kb, qb) * CH, CH)
                valid = _sb_valid(qb, jnp.minimum(kb, qb), live)
                z = _dot_nt(q2, k_ref[pl.ds(start, CH), :])
                sp = _sb_softplus(z)
                lm = jnp.where(valid, -sp, 0.0)
                blocks.append((valid, z - sp, _dot_split(lm, mle), jnp.sum(lm, axis=1, keepdims=True), start))
            stage = []
            for valid, ls, ploc, rs, start in blocks:
                w = jnp.where(valid, jnp.exp(ls + (s2 - (ploc + pacc))), 0.0)
                gg = _dot_nt(do2, v_ref[pl.ds(start, CH), :]) * w
                stage.append((valid, ls, w, gg, _dot_split(gg, mlt), jnp.sum(gg, axis=1, keepdims=True), start))
                pacc = pacc + rs
            for valid, ls, w, gg, gloc, gs, start in stage:
                sig = jnp.exp(ls)
                dz = jnp.where(valid, gg * (1.0 - sig) - (gloc + gacc) * sig, 0.0)
                dq2 = dq2 + _dot(dz, k_ref[pl.ds(start, CH), :])
                dk_ref[pl.ds(start, CH), :] += _dot_tn(dz, q2)
                dv_ref[pl.ds(start, CH), :] += _dot_tn(w, do2)
                gacc = gacc + gs
            return dq2, pacc, gacc

        zero = jnp.zeros((2 * CH, 1), F32)
        trips = (qb + _SB_NB) // _SB_NB
        dq2 = lax.fori_loop(0, trips, step, (jnp.zeros((2 * CH, 128), F32), zero, zero))[0]
        dq_ref[...] = (_unstack_heads(dq2) * _SB_SCALE).astype(dq_ref.dtype)

    qspec = lambda off: pl.BlockSpec((CH, 128), lambda b, hp, qb: (b * nch + qb, off + hp))
    kspec = lambda off: pl.BlockSpec((Pn, 128), lambda b, hp, qb: (b, off + hp))
    full = jax.ShapeDtypeStruct((R, 1024), F32)
    kw = dict(grid=(B, SB_HEADS // 2, nch), in_specs=[qspec(0), kspec(8), kspec(16), qspec(0), qspec(0)],
              out_specs=[qspec(0), kspec(0), kspec(0)], out_shape=[jax.ShapeDtypeStruct((R, 1024), _MXU), full, full])
    return _call(body, "sb_bwd", ("arbitrary", "arbitrary", "arbitrary"), kw, (u1, u1, u1, dycat, stot), rider)


def _neg_expm1(x):
    series = -(x * (1.0 + x * (0.5 + x * (1.0 / 6.0 + x * (1.0 / 24.0)))))
    return jnp.where(x > -0.05, series, 1.0 - jnp.exp(x))


def _lru_gates(x, wa_ref, ba_ref, wx_ref, bx_ref, lam_ref):
    rs, is_ = [], []
    for n in range(LRU_BLOCKS):
        xb = x[:, 128 * n:128 * n + 128]
        rs.append(_dot(xb, wa_ref[n]))
        is_.append(_dot(xb, wx_ref[n]))
    r = _sigmoid(jnp.concatenate(rs, axis=1) + ba_ref[...])
    i = _sigmoid(jnp.concatenate(is_, axis=1) + bx_ref[...])
    sp = _softplus(-lam_ref[...])
    la = -LRU_C * r * sp
    a = jnp.exp(la)
    mult = jnp.sqrt(jnp.maximum(_neg_expm1(2.0 * la), 0.0))
    return r, i, sp, a, mult


def _lru_fwd(u1, ycat, conv_w, conv_b, wa, ba, wx, bx, lam, B, nch):
    R = u1.shape[0]

    def body(x_ref, xp_ref, gate_ref, cw_ref, cb_ref, wa_ref, ba_ref, wx_ref, bx_ref, lam_ref, ycat_in,
             out_ref, hs_ref, hc):
        c = pl.program_id(1)

        @pl.when(c == 0)
        def _():
            hc[...] = jnp.zeros_like(hc)

        x = _conv_pre(xp_ref[...], x_ref[...], cw_ref, cb_ref, 4)
        r, i, sp, a, mult = _lru_gates(x, wa_ref, ba_ref, wx_ref, bx_ref, lam_ref)
        b = jnp.where(_real_rows(c), mult * (i * x), 0.0)
        rows = _row_ids(CH)
        s = 1
        while s < CH:
            a_s = jnp.where(rows >= s, pltpu.roll(a, s, axis=0), 1.0)
            b_s = jnp.where(rows >= s, pltpu.roll(b, s, axis=0), 0.0)
            b = a * b_s + b
            a = a * a_s
            s *= 2
        h = a * hc[0:1, :] + b
        hs_ref[...] = h
        hc[0:1, :] = hs_ref[CH - 1:CH, :]
        out_ref[...] = (h * _gelu(gate_ref[...])).astype(out_ref.dtype)

    row = lambda col: pl.BlockSpec((CH, 1024), lambda b, c: (b * nch + c, col))
    vec = pl.BlockSpec((1, 1024), lambda b, c: (0, 0))
    wsp = pl.BlockSpec((LRU_BLOCKS, 128, 128), lambda b, c: (0, 0, 0))
    return pl.pallas_call(
        body, name="lru_fwd", grid=(B, nch),
        in_specs=[row(4), pl.BlockSpec((8, 1024), _prev8_map(nch, 4)), row(3),
                  pl.BlockSpec((4, 1024), lambda b, c: (0, 0)), vec, wsp, vec, wsp, vec, vec,
                  pl.BlockSpec(memory_space=pl.ANY)],
        out_specs=[row(1), row(0)],
        out_shape=[jax.ShapeDtypeStruct(ycat.shape, ycat.dtype), jax.ShapeDtypeStruct((R, 1024), F32)],
        scratch_shapes=[pltpu.VMEM((8, 1024), F32)],
        input_output_aliases={10: 0},
        compiler_params=_cparams(("parallel", "arbitrary")),
    )(u1, u1, u1, conv_w, conv_b, wa, ba, wx, bx, lam, ycat)


def _lru_bwd(dycat, u1, hs, conv_w, conv_b, wa, ba, wx, bx, lam, B, nch):
    R = u1.shape[0]

    def body(dy_ref, x_ref, xp_ref, gate_ref, hs_ref, hsp_ref, cw_ref, cb_ref, wa_ref, ba_ref, wx_ref, bx_ref, lam_ref,
             dgate_ref, dxc_ref, pg_ref, dwa_ref, dwx_ref, lc):
        c = nch - 1 - pl.program_id(1)

        @pl.when(pl.program_id(1) == 0)
        def _():
            lc[...] = jnp.zeros_like(lc)
            pg_ref[...] = jnp.zeros_like(pg_ref)
            dwa_ref[...] = jnp.zeros_like(dwa_ref)
            dwx_ref[...] = jnp.zeros_like(dwx_ref)

        x = _conv_pre(xp_ref[...], x_ref[...], cw_ref, cb_ref, 4)
        r, i, sp, a, mult = _lru_gates(x, wa_ref, ba_ref, wx_ref, bx_ref, lam_ref)
        h = hs_ref[...]
        hprev = _shift_down(hsp_ref[...], h, 1)
        gate = gate_ref[...]
        dy = dy_ref[...]
        dgate_ref[...] = (dy * h * _dgelu(gate)).astype(dgate_ref.dtype)
        rows = _row_ids(CH)
        lam_t = dy * _gelu(gate) + jnp.where(rows == CH - 1, lc[0:1, :], 0.0)
        coef = jnp.where(rows < CH - 1, pltpu.roll(a, CH - 1, axis=0), 0.0)
        s = 1
        while s < CH:
            c_s = jnp.where(rows < CH - s, pltpu.roll(coef, CH - s, axis=0), 1.0)
            l_s = jnp.where(rows < CH - s, pltpu.roll(lam_t, CH - s, axis=0), 0.0)
            lam_t = coef * l_s + lam_t
            coef = coef * c_s
            s *= 2
        lc[0:1, :] = jnp.sum(jnp.where(rows == 0, a * lam_t, 0.0), axis=0, keepdims=True)
        db = jnp.where(_real_rows(c), lam_t, 0.0)
        da = db * hprev
        dmult = db * (i * x)
        di = db * mult * x
        dx = db * mult * i
        pos = mult > 0.0
        dla = da * a + jnp.where(pos, -dmult * (a * a) / jnp.where(pos, mult, 1.0), 0.0)
        dr = dla * (-LRU_C * sp)
        pg_ref[2:3, :] += jnp.sum(dla * (LRU_C * r) * _sigmoid(-lam_ref[...]), axis=0, keepdims=True)
        dpr = dr * r * (1.0 - r)
        dpi = di * i * (1.0 - i)
        pg_ref[0:1, :] += jnp.sum(dpr, axis=0, keepdims=True)
        pg_ref[1:2, :] += jnp.sum(dpi, axis=0, keepdims=True)
        dxs = []
        for n in range(LRU_BLOCKS):
            blk = slice(128 * n, 128 * n + 128)
            dxs.append(dx[:, blk] + _dot_nt(dpr[:, blk], wa_ref[n]) + _dot_nt(dpi[:, blk], wx_ref[n]))
            dwa_ref[n] += _dot_tn(x[:, blk], dpr[:, blk])
            dwx_ref[n] += _dot_tn(x[:, blk], dpi[:, blk])
        dxc_ref[...] = jnp.concatenate(dxs, axis=1)

    rmap = lambda col: (lambda b, c: (b * nch + nch - 1 - c, col))
    row = lambda col: pl.BlockSpec((CH, 1024), rmap(col))
    prev = lambda col: pl.BlockSpec(
        (8, 1024), lambda b, c: (jnp.maximum((b * nch + nch - 1 - c) * (CH // 8) - 1, 0), col))
    vec = pl.BlockSpec((1, 1024), lambda b, c: (0, 0))
    wsp = pl.BlockSpec((LRU_BLOCKS, 128, 128), lambda b, c: (0, 0, 0))
    full = jax.ShapeDtypeStruct((R, 1024), F32)
    return pl.pallas_call(
        body, name="lru_bwd", grid=(B, nch),
        in_specs=[row(1), row(4), prev(4), row(3), row(0), prev(0),
                  pl.BlockSpec((4, 1024), lambda b, c: (0, 0)), vec, wsp, vec, wsp, vec, vec],
        out_specs=[row(0), row(0), pl.BlockSpec((None, 8, 1024), lambda b, c: (b, 0, 0)),
                   pl.BlockSpec((None, LRU_BLOCKS, 128, 128), lambda b, c: (b, 0, 0, 0)),
                   pl.BlockSpec((None, LRU_BLOCKS, 128, 128), lambda b, c: (b, 0, 0, 0))],
        out_shape=[jax.ShapeDtypeStruct((R, 1024), _MXU), full, jax.ShapeDtypeStruct((B, 8, 1024), F32),
                   jax.ShapeDtypeStruct((B, LRU_BLOCKS, 128, 128), F32),
                   jax.ShapeDtypeStruct((B, LRU_BLOCKS, 128, 128), F32)],
        scratch_shapes=[pltpu.VMEM((8, 1024), F32)],
        compiler_params=_cparams(("parallel", "arbitrary")),
    )(dycat, u1, u1, u1, hs, hs, conv_w, conv_b, wa, ba, wx, bx, lam)


_FFN_TC = FFN // 2


def _ffn_specs(nch):
    nt = FFN // _FFN_TC
    row = lambda off: pl.BlockSpec((CH, _FFN_TC), lambda b, c, j: (b * nch + c, off + j))
    prev = lambda off: pl.BlockSpec(
        (8, _FFN_TC), lambda b, c, j: (jnp.maximum((b * nch + c) * (CH // 8) - 1, 0), off + j))
    wsp = lambda off: pl.BlockSpec((3, _FFN_TC), lambda b, c, j: (0, off + j))
    bsp = lambda off: pl.BlockSpec((1, _FFN_TC), lambda b, c, j: (0, off + j))
    return nt, row, [row(0), prev(0), row(nt), prev(nt), wsp(0), wsp(nt), bsp(0), bsp(nt)]


def _ffn_act_fwd(uf, conv_w, conv_b, B, nch, rider=None):
    R = uf.shape[0]
    nt, row, specs = _ffn_specs(nch)

    def body(g_ref, gp_ref, u_ref, up_ref, wg_ref, wu_ref, bg_ref, bu_ref, o_ref):
        cg = _conv_pre(gp_ref[...], g_ref[...], wg_ref, bg_ref, 3)
        cu = _conv_pre(up_ref[...], u_ref[...], wu_ref, bu_ref, 3)
        o_ref[...] = jnp.where(_real_rows(pl.program_id(1)), _silu(cg) * cu, 0.0).astype(o_ref.dtype)

    kw = dict(grid=(B, nch, nt), in_specs=specs, out_specs=[row(0)],
              out_shape=[jax.ShapeDtypeStruct((R, FFN), _MXU)])
    return _call(body, "ffn_act_fwd", ("arbitrary", "arbitrary", "arbitrary"), kw,
                 (uf, uf, uf, uf, conv_w, conv_w, conv_b, conv_b), rider)


def _ffn_act_bwd(da, uf, conv_w, conv_b, nch, name, rider=None):
    R = uf.shape[0]
    nt = FFN // _FFN_TC
    nr = R // CH
    K = 3

    def body(da_ref, dan_ref, g_ref, gp_ref, gn_ref, u_ref, up_ref, un_ref, wg_ref, wu_ref, bg_ref, bu_ref,
             dug_ref, duu_ref, dwg_ref, dwu_ref):
        i = pl.program_id(1)

        @pl.when(i == 0)
        def _():
            dwg_ref[...] = jnp.zeros_like(dwg_ref)
            dwu_ref[...] = jnp.zeros_like(dwu_ref)

        c = i % nch
        ext = CH + 8
        rows = _row_ids(ext)
        follows = (c < nch - 1).astype(jnp.int32)
        keep = jnp.logical_and(c * CH + rows >= PAD, rows < CH + 8 * follows)
        dav = jnp.where(keep, jnp.concatenate([da_ref[...], dan_ref[...]], axis=0), 0.0)

        def conv_ext(x_ref, xp_ref, xn_ref, w_ref, b_ref):
            cat = jnp.concatenate([xp_ref[...], x_ref[...], xn_ref[...]], axis=0)
            shifted = [cat[8:]] + [pltpu.roll(cat, s, axis=0)[8:] for s in range(1, K)]
            acc = shifted[0] * w_ref[K - 1:K, :] + b_ref[...]
            for s in range(1, K):
                acc = acc + shifted[s] * w_ref[K - 1 - s:K - s, :]
            return acc, shifted

        cg, gsh = conv_ext(g_ref, gp_ref, gn_ref, wg_ref, bg_ref)
        cu, ush = conv_ext(u_ref, up_ref, un_ref, wu_ref, bu_ref)
        sg = _sigmoid(cg)
        dcg = dav * cu * (sg * (1.0 + cg * (1.0 - sg)))
        dcu = dav * (cg * sg)
        for dc, xsh, w_ref, din_ref, dw_ref in ((dcg, gsh, wg_ref, dug_ref, dwg_ref), (dcu, ush, wu_ref, duu_ref, dwu_ref)):
            dp = dc[:CH]
            din = dp * w_ref[K - 1:K, :]
            dw_ref[7:8, :] += jnp.sum(dp, axis=0, keepdims=True)
            dw_ref[K - 1:K, :] += jnp.sum(dp * xsh[0][:CH], axis=0, keepdims=True)
            for s in range(1, K):
                din = din + pltpu.roll(dc, ext - s, axis=0)[:CH] * w_ref[K - 1 - s:K - s, :]
                dw_ref[K - 1 - s:K - s, :] += jnp.sum(dp * xsh[s][:CH], axis=0, keepdims=True)
            din_ref[...] = din.astype(din_ref.dtype)

    row = lambda off: pl.BlockSpec((CH, _FFN_TC), lambda j, i: (i, off + j))
    prev = lambda off: pl.BlockSpec((8, _FFN_TC), lambda j, i: (jnp.maximum(i * (CH // 8) - 1, 0), off + j))
    nxt = lambda off: pl.BlockSpec(
        (8, _FFN_TC), lambda j, i: (jnp.minimum((i + 1) * (CH // 8), nr * (CH // 8) - 1), off + j))
    wsp = lambda off: pl.BlockSpec((K, _FFN_TC), lambda j, i: (0, off + j))
    bsp = lambda off: pl.BlockSpec((1, _FFN_TC), lambda j, i: (0, off + j))
    acc = pl.BlockSpec((8, _FFN_TC), lambda j, i: (0, j))
    half = jax.ShapeDtypeStruct((R, FFN), _MXU)
    dwsh = jax.ShapeDtypeStruct((8, FFN), F32)
    kw = dict(
        grid=(nt, nr),
        in_specs=[row(0), nxt(0), row(0), prev(0), nxt(0), row(nt), prev(nt), nxt(nt), wsp(0), wsp(nt), bsp(0), bsp(nt)],
        out_specs=[row(0), row(0), acc, acc],
        out_shape=[half, half, dwsh, dwsh])
    return _call(body, name, ("arbitrary", "arbitrary"), kw,
                 (da, da, uf, uf, uf, uf, uf, uf, conv_w, conv_w, conv_b, conv_b), rider)


def _head(h, g, target, B, nch):
    R = h.shape[0]

    def body(h_ref, g_ref, t_ref, dh_ref, loss_ref, dg_ref):
        c = pl.program_id(1)

        @pl.when(c == 0)
        def _():
            dh_ref[...] = jnp.zeros_like(dh_ref)
            loss_ref[...] = jnp.zeros_like(loss_ref)
            dg_ref[...] = jnp.zeros_like(dg_ref)

        @pl.when(c > 0)
        def _():
            x = h_ref[...]
            gv = g_ref[...]
            r = lax.rsqrt(jnp.mean(x * x, axis=-1, keepdims=True) + EPS)
            xhat = x * r
            e = xhat * gv - t_ref[...]
            loss_ref[...] += 0.5 * jnp.sum(jnp.mean(e * e, axis=-1, keepdims=True), axis=0, keepdims=True)
            dy = e * (1.0 / D)
            dg_ref[0:1, :] += jnp.sum(dy * xhat, axis=0, keepdims=True)
            dx = dy * gv
            dh_ref[...] = r * (dx - xhat * jnp.mean(dx * xhat, axis=-1, keepdims=True))

    row = pl.BlockSpec((CH, D), lambda b, c: (b * nch + c, 0))
    return pl.pallas_call(
        body, name="head", grid=(B, nch),
        in_specs=[row, pl.BlockSpec((1, D), lambda b, c: (0, 0)),
                  pl.BlockSpec((CH, D), lambda b, c: (b * (nch - 1) + jnp.maximum(c - 1, 0), 0))],
        out_specs=[row, pl.BlockSpec((None, 8, 128), lambda b, c: (b, 0, 0)),
                   pl.BlockSpec((None, 8, D), lambda b, c: (b, 0, 0))],
        out_shape=[jax.ShapeDtypeStruct((R, D), F32), jax.ShapeDtypeStruct((B, 8, 128), F32),
                   jax.ShapeDtypeStruct((B, 8, D), F32)],
        compiler_params=_cparams(("parallel", "arbitrary")),
    )(h, g, target)


ADAM_LR = 0.001
ADAM_B1 = 0.9
ADAM_B2 = 0.999
ADAM_EPS = 1e-08
ADAM_WD = 0.01
ADAM_STEP = 10


def _adamw(w, g, m, v, name):
    Rr, C = w.shape
    tr = _tile(Rr, (256, 64))

    def body(w_ref, g_ref, m_ref, v_ref, d_ref, nm_ref, nv_ref):
        gv = g_ref[...]
        nm = ADAM_B1 * m_ref[...] + (1.0 - ADAM_B1) * gv
        nv = ADAM_B2 * v_ref[...] + (1.0 - ADAM_B2) * (gv * gv)
        m_hat = nm / (1.0 - ADAM_B1 ** ADAM_STEP)
        v_hat = nv / (1.0 - ADAM_B2 ** ADAM_STEP)
        d_ref[...] = -ADAM_LR * (m_hat / (jnp.sqrt(v_hat) + ADAM_EPS) + ADAM_WD * w_ref[...])
        nm_ref[...] = nm
        nv_ref[...] = nv

    spec = pl.BlockSpec((tr, C), lambda i: (i, 0))
    sh = jax.ShapeDtypeStruct((Rr, C), F32)
    return pl.pallas_call(
        body, name=name, grid=(Rr // tr,),
        in_specs=[spec] * 4, out_specs=[spec] * 3, out_shape=[sh] * 3,
        compiler_params=_cparams(("parallel",)),
    )(w, g, m, v)


_MESH = pl.DeviceIdType.MESH
_ANY = pl.BlockSpec(memory_space=pl.ANY)


def _place():
    x, y, c = lax.axis_index("x"), lax.axis_index("y"), lax.axis_index("c")
    chips = [(1 - x, y), (x, 1 - y), (1 - x, 1 - y)]
    return x, y, c, chips


def _rcopy(src, dst, ssem, rsem, dev):
    return pltpu.make_async_remote_copy(src_ref=src, dst_ref=dst, send_sem=ssem, recv_sem=rsem,
                                        device_id=dev, device_id_type=_MESH)


def _with_riders(body, kw, kind, riders):
    n_in, n_out, n_scr = len(kw["in_specs"]), len(kw["out_specs"]), len(kw.get("scratch_shapes", []))
    grid = kw["grid"]
    nr = len(riders)
    nsem = 4 if kind == "gather" else 2

    def new_body(*refs):
        ins, srcs = refs[:n_in], refs[n_in:n_in + nr]
        outs, dsts = refs[n_in + nr:n_in + nr + n_out], refs[n_in + nr + n_out:n_in + 2 * nr + n_out]
        scr = refs[n_in + 2 * nr + n_out:n_in + 2 * nr + n_out + n_scr]
        sems = refs[n_in + 2 * nr + n_out + n_scr:]
        first = last = None
        for axis, size in enumerate(grid):
            i = pl.program_id(axis)
            first = (i == 0) if first is None else jnp.logical_and(first, i == 0)
            last = (i == size - 1) if last is None else jnp.logical_and(last, i == size - 1)
        x, y, c, chips = _place()
        k = 2 * x + y
        sib = (x, y, 1 - c)
        ssem, rsem = sems[:2]
        sends = []
        for a in range(nr):
            for j, (cx, cy) in enumerate(chips):
                if kind == "gather":
                    src, dst = srcs[a].at[c], dsts[a].at[k, c]
                else:
                    src, dst = srcs[a].at[2 * cx + cy], dsts[a].at[k]
                sends.append(_rcopy(src, dst, ssem.at[3 * a + j], rsem.at[3 * a + j], (cx, cy, c)))

        @pl.when(first)
        def _():
            for cp in sends:
                cp.start()

        body(*ins, *outs, *scr)

        @pl.when(last)
        def _():
            passed = []
            for a in range(nr):
                for j, (cx, cy) in enumerate(chips):
                    got = dsts[a].at[2 * cx + cy, c] if kind == "gather" else dsts[a].at[2 * cx + cy]
                    _rcopy(got, got, ssem.at[3 * a + j], rsem.at[3 * a + j], (cx, cy, c)).wait_recv()
                    if kind == "gather":
                        fw = _rcopy(got, got, sems[2].at[3 * a + j], sems[3].at[3 * a + j], sib)
                        fw.start()
                        passed.append(fw)
            if kind == "gather":
                for a in range(nr):
                    for j, (cx, cy) in enumerate(chips):
                        got = dsts[a].at[2 * cx + cy, 1 - c]
                        _rcopy(got, got, sems[2].at[3 * a + j], sems[3].at[3 * a + j], sib).wait_recv()
            for cp in sends + passed:
                cp.wait_send()

    kw = dict(kw)
    kw["in_specs"] = list(kw["in_specs"]) + [_ANY] * nr
    kw["out_specs"] = list(kw["out_specs"]) + [_ANY] * nr
    kw["out_shape"] = list(kw["out_shape"]) + [
        jax.ShapeDtypeStruct(((4,) + r.shape) if kind == "gather" else r.shape, r.dtype) for r in riders]
    kw["scratch_shapes"] = list(kw.get("scratch_shapes", [])) + [pltpu.SemaphoreType.DMA((3 * nr,))] * nsem
    return new_body, kw


def _call(body, name, sem, kw, args, rider=None):
    if rider is not None:
        body, kw = _with_riders(body, kw, *rider)
        args = tuple(args) + tuple(rider[1])
    return pl.pallas_call(body, name=name, compiler_params=_cparams(sem), **kw)(*args)


def _fill_own(result, own, chip):
    return lax.dynamic_update_index_in_dim(result, own, chip, 0)


def _gather_shards(bigs, small):
    nb = len(bigs)

    def body(*refs):
        ins, outs = refs[:nb + 1], refs[nb + 1:2 * nb + 2]
        ssem, rsem, fssem, frsem = refs[2 * nb + 2:]
        x, y, c, chips = _place()
        k = 2 * x + y
        sib = (x, y, 1 - c)

        def part(a, slot, hc):
            return outs[a].at[slot] if a == nb else outs[a].at[slot, hc]

        first = []
        for a in range(nb + 1):
            src = ins[a] if a == nb else ins[a].at[c]
            for j, (cx, cy) in enumerate(chips):
                first.append(_rcopy(src, part(a, k, c), ssem.at[3 * a + j], rsem.at[3 * a + j], (cx, cy, c)))
        for cp in first:
            cp.start()
        passed = []
        for a in range(nb + 1):
            for j, (cx, cy) in enumerate(chips):
                got = part(a, 2 * cx + cy, c)
                _rcopy(got, got, ssem.at[3 * a + j], rsem.at[3 * a + j], (cx, cy, c)).wait_recv()
                if a < nb:
                    fw = _rcopy(got, got, fssem.at[3 * a + j], frsem.at[3 * a + j], sib)
                    fw.start()
                    passed.append(fw)
        for a in range(nb):
            for j, (cx, cy) in enumerate(chips):
                got = part(a, 2 * cx + cy, 1 - c)
                _rcopy(got, got, fssem.at[3 * a + j], frsem.at[3 * a + j], sib).wait_recv()
        for cp in first + passed:
            cp.wait_send()

    arrs = list(bigs) + [small]
    n = 3 * (nb + 1)
    return pl.pallas_call(
        body, name="gather_shards",
        in_specs=[_ANY] * (nb + 1), out_specs=[_ANY] * (nb + 1),
        out_shape=[jax.ShapeDtypeStruct((4,) + a.shape, a.dtype) for a in arrs],
        scratch_shapes=[pltpu.SemaphoreType.DMA((n,)), pltpu.SemaphoreType.DMA((n,)),
                        pltpu.SemaphoreType.DMA((n,)), pltpu.SemaphoreType.DMA((n,))],
    )(*arrs)


def _swap_halves(grads, name):
    na = len(grads)
    halves = [g.shape[1] // 2 for g in grads]

    def body(*refs):
        ins, outs = refs[:na], refs[na:2 * na]
        ssem, rsem = refs[2 * na:]
        x, y, c, _ = _place()
        sib = (x, y, 1 - c)
        cps = [_rcopy(ins[a].at[:, pl.ds((1 - c) * halves[a], halves[a]), :], outs[a], ssem.at[a], rsem.at[a], sib)
               for a in range(na)]
        for cp in cps:
            cp.start()
        for cp in cps:
            cp.wait()

    return pl.pallas_call(
        body, name=name,
        in_specs=[_ANY] * na, out_specs=[_ANY] * na,
        out_shape=[jax.ShapeDtypeStruct((4, g.shape[1] // 2, g.shape[2]), g.dtype) for g in grads],
        scratch_shapes=[pltpu.SemaphoreType.DMA((na,)), pltpu.SemaphoreType.DMA((na,))],
    )(*grads)


def _sum_rows(rh):
    return rh if rh <= 512 else _tile(rh, (512, 256, 128, 64, 32))


def _chip_sum(grad, recv, core, name):
    _, r, cdim = grad.shape
    rh = r // 2
    tr = _sum_rows(rh)
    nblk = rh // tr

    def body(core_ref, g_ref, r_ref, o_ref):
        o_ref[...] = (g_ref[...] + r_ref[...]).astype(o_ref.dtype)

    return pl.pallas_call(
        body, name=name,
        grid_spec=pltpu.PrefetchScalarGridSpec(
            num_scalar_prefetch=1, grid=(4, nblk),
            in_specs=[pl.BlockSpec((None, tr, cdim), lambda s, i, cr: (s, cr[0] * nblk + i, 0)),
                      pl.BlockSpec((None, tr, cdim), lambda s, i, cr: (s, i, 0))],
            out_specs=pl.BlockSpec((None, tr, cdim), lambda s, i, cr: (s, i, 0))),
        out_shape=jax.ShapeDtypeStruct((4, rh, cdim), BF16),
        compiler_params=_cparams(("parallel", "parallel")),
    )(core, grad, recv)


def _scatter_sums(sums):
    na = len(sums)

    def body(*refs):
        ins, outs = refs[:na], refs[na:2 * na]
        ssem, rsem, lsem = refs[2 * na:]
        x, y, c, chips = _place()
        k = 2 * x + y
        local = [pltpu.make_async_copy(ins[a].at[k], outs[a].at[k], lsem.at[a]) for a in range(na)]
        for cp in local:
            cp.start()
        cps = []
        for a in range(na):
            for j, (cx, cy) in enumerate(chips):
                cps.append(_rcopy(ins[a].at[2 * cx + cy], outs[a].at[k], ssem.at[3 * a + j], rsem.at[3 * a + j],
                                  (cx, cy, c)))
        for cp in cps:
            cp.start()
        for a in range(na):
            for j, (cx, cy) in enumerate(chips):
                got = outs[a].at[2 * cx + cy]
                _rcopy(got, got, ssem.at[3 * a + j], rsem.at[3 * a + j], (cx, cy, c)).wait_recv()
        for cp in cps:
            cp.wait_send()
        for cp in local:
            cp.wait()

    return pl.pallas_call(
        body, name="scatter_sums",
        in_specs=[_ANY] * na, out_specs=[_ANY] * na,
        out_shape=[jax.ShapeDtypeStruct(s.shape, s.dtype) for s in sums],
        scratch_shapes=[pltpu.SemaphoreType.DMA((3 * na,)), pltpu.SemaphoreType.DMA((3 * na,)),
                        pltpu.SemaphoreType.DMA((na,))],
    )(*sums)


def _sum_chips(parts, name):
    _, rh, cdim = parts.shape
    tr = _sum_rows(rh)

    def body(p_ref, o_ref):
        acc = p_ref[0].astype(F32)
        for j in range(1, 4):
            acc = acc + p_ref[j].astype(F32)
        o_ref[...] = acc

    return pl.pallas_call(
        body, name=name, grid=(rh // tr,),
        in_specs=[pl.BlockSpec((4, tr, cdim), lambda i: (0, i, 0))],
        out_specs=pl.BlockSpec((tr, cdim), lambda i: (i, 0)),
        out_shape=jax.ShapeDtypeStruct((rh, cdim), F32),
        compiler_params=_cparams(("parallel",)),
    )(parts)


def _join_halves(reds):
    na = len(reds)

    def body(*refs):
        ins, outs = refs[:na], refs[na:2 * na]
        ssem, rsem = refs[2 * na:]
        x, y, c, _ = _place()
        cps = [_rcopy(ins[a], outs[a], ssem.at[a], rsem.at[a], (x, y, 1 - c)) for a in range(na)]
        for cp in cps:
            cp.start()
        for cp in cps:
            cp.wait()

    return pl.pallas_call(
        body, name="join_halves",
        in_specs=[_ANY] * na, out_specs=[_ANY] * na,
        out_shape=[jax.ShapeDtypeStruct(r.shape, r.dtype) for r in reds],
        scratch_shapes=[pltpu.SemaphoreType.DMA((na,)), pltpu.SemaphoreType.DMA((na,))],
    )(*reds)


def _allreduce_small(buf):
    n = buf.shape[0]

    def body(in_ref, out_ref, recv, ssem, rsem):
        x, y, c, _ = _place()
        peers = [(x, y, 1 - c), (1 - x, y, c), (x, 1 - y, c)]
        out_ref[...] = in_ref[...]
        for r, peer in enumerate(peers):
            cp = _rcopy(out_ref, recv.at[r], ssem.at[r], rsem.at[r], peer)
            cp.start()
            cp.wait()
            out_ref[...] = out_ref[...] + recv[r]

    vm = pl.BlockSpec(memory_space=pltpu.VMEM)
    return pl.pallas_call(
        body, name="allreduce_small",
        in_specs=[vm], out_specs=vm,
        out_shape=jax.ShapeDtypeStruct(buf.shape, F32),
        scratch_shapes=[pltpu.VMEM((3, n, 128), F32), pltpu.SemaphoreType.DMA((3,)), pltpu.SemaphoreType.DMA((3,))],
        compiler_params=pltpu.CompilerParams(vmem_limit_bytes=VMEM_LIMIT),
    )(buf)


_W_NAMES = ['meta_tokens', 'l0_mix_norm', 'l0_w_in', 'l0_ssd_conv_w', 'l0_ssd_conv_b', 'l0_ssd_dt_bias', 'l0_ssd_a_log',
            'l0_ssd_d', 'l0_ssd_norm', 'l0_ret_norm', 'l0_w_out', 'l0_ffn_norm', 'l0_ffn_w_in', 'l0_ffn_conv_w',
            'l0_ffn_conv_b', 'l0_ffn_w_out', 'l1_mix_norm', 'l1_w_in', 'l1_lru_conv_w', 'l1_lru_conv_b', 'l1_lru_wa',
            'l1_lru_ba', 'l1_lru_wx', 'l1_lru_bx', 'l1_lru_lambda', 'l1_w_out', 'l1_ffn_norm', 'l1_ffn_w_in',
            'l1_ffn_conv_w', 'l1_ffn_conv_b', 'l1_ffn_w_out', 'final_norm']
_IN_NAMES = ['x'] + _W_NAMES + ['loss_target'] + ['m_' + n for n in _W_NAMES] + ['v_' + n for n in _W_NAMES]
_BIG = ['l0_w_in', 'l0_w_out', 'l0_ffn_w_in', 'l0_ffn_w_out', 'l1_w_in', 'l1_w_out', 'l1_ffn_w_in', 'l1_ffn_w_out']
_BIG_COLS = ('l0_w_in', 'l0_ffn_w_in', 'l1_w_in', 'l1_ffn_w_in')
_SMALL_SHARDED = ['meta_tokens', 'l0_ssd_conv_w', 'l0_ffn_conv_w', 'l1_lru_conv_w', 'l1_ffn_conv_w']
_SMALL = [n for n in _W_NAMES if n not in _BIG]


def _pack(arrs):
    flat = []
    for a in arrs:
        v = a.reshape(-1).astype(F32)
        flat.append(jnp.pad(v, (0, (-v.shape[0]) % 128)))
    v = jnp.concatenate(flat)
    v = jnp.pad(v, (0, (-v.shape[0]) % 1024))
    return v.reshape(-1, 128)


def _unpack(buf, shapes):
    out, row = [], 0
    for sh in shapes:
        n = int(np.prod(sh))
        rows = -(-n // 128)
        out.append(buf[row:row + rows].reshape(-1)[:n].reshape(sh))
        row += rows
    return out


def kernel(x, meta_tokens, l0_mix_norm, l0_w_in, l0_ssd_conv_w, l0_ssd_conv_b, l0_ssd_dt_bias, l0_ssd_a_log, l0_ssd_d, l0_ssd_norm, l0_ret_norm, l0_w_out, l0_ffn_norm, l0_ffn_w_in, l0_ffn_conv_w, l0_ffn_conv_b, l0_ffn_w_out, l1_mix_norm, l1_w_in, l1_lru_conv_w, l1_lru_conv_b, l1_lru_wa, l1_lru_ba, l1_lru_wx, l1_lru_bx, l1_lru_lambda, l1_w_out, l1_ffn_norm, l1_ffn_w_in, l1_ffn_conv_w, l1_ffn_conv_b, l1_ffn_w_out, final_norm, loss_target, m_meta_tokens, m_l0_mix_norm, m_l0_w_in, m_l0_ssd_conv_w, m_l0_ssd_conv_b, m_l0_ssd_dt_bias, m_l0_ssd_a_log, m_l0_ssd_d, m_l0_ssd_norm, m_l0_ret_norm, m_l0_w_out, m_l0_ffn_norm, m_l0_ffn_w_in, m_l0_ffn_conv_w, m_l0_ffn_conv_b, m_l0_ffn_w_out, m_l1_mix_norm, m_l1_w_in, m_l1_lru_conv_w, m_l1_lru_conv_b, m_l1_lru_wa, m_l1_lru_ba, m_l1_lru_wx, m_l1_lru_bx, m_l1_lru_lambda, m_l1_w_out, m_l1_ffn_norm, m_l1_ffn_w_in, m_l1_ffn_conv_w, m_l1_ffn_conv_b, m_l1_ffn_w_out, m_final_norm, v_meta_tokens, v_l0_mix_norm, v_l0_w_in, v_l0_ssd_conv_w, v_l0_ssd_conv_b, v_l0_ssd_dt_bias, v_l0_ssd_a_log, v_l0_ssd_d, v_l0_ssd_norm, v_l0_ret_norm, v_l0_w_out, v_l0_ffn_norm, v_l0_ffn_w_in, v_l0_ffn_conv_w, v_l0_ffn_conv_b, v_l0_ffn_w_out, v_l1_mix_norm, v_l1_w_in, v_l1_lru_conv_w, v_l1_lru_conv_b, v_l1_lru_wa, v_l1_lru_ba, v_l1_lru_wx, v_l1_lru_bx, v_l1_lru_lambda, v_l1_w_out, v_l1_ffn_norm, v_l1_ffn_w_in, v_l1_ffn_conv_w, v_l1_ffn_conv_b, v_l1_ffn_w_out, v_final_norm):
    args = (x, meta_tokens, l0_mix_norm, l0_w_in, l0_ssd_conv_w, l0_ssd_conv_b, l0_ssd_dt_bias, l0_ssd_a_log, l0_ssd_d, l0_ssd_norm, l0_ret_norm, l0_w_out, l0_ffn_norm, l0_ffn_w_in, l0_ffn_conv_w, l0_ffn_conv_b, l0_ffn_w_out, l1_mix_norm, l1_w_in, l1_lru_conv_w, l1_lru_conv_b, l1_lru_wa, l1_lru_ba, l1_lru_wx, l1_lru_bx, l1_lru_lambda, l1_w_out, l1_ffn_norm, l1_ffn_w_in, l1_ffn_conv_w, l1_ffn_conv_b, l1_ffn_w_out, final_norm, loss_target, m_meta_tokens, m_l0_mix_norm, m_l0_w_in, m_l0_ssd_conv_w, m_l0_ssd_conv_b, m_l0_ssd_dt_bias, m_l0_ssd_a_log, m_l0_ssd_d, m_l0_ssd_norm, m_l0_ret_norm, m_l0_w_out, m_l0_ffn_norm, m_l0_ffn_w_in, m_l0_ffn_conv_w, m_l0_ffn_conv_b, m_l0_ffn_w_out, m_l1_mix_norm, m_l1_w_in, m_l1_lru_conv_w, m_l1_lru_conv_b, m_l1_lru_wa, m_l1_lru_ba, m_l1_lru_wx, m_l1_lru_bx, m_l1_lru_lambda, m_l1_w_out, m_l1_ffn_norm, m_l1_ffn_w_in, m_l1_ffn_conv_w, m_l1_ffn_conv_b, m_l1_ffn_w_out, m_final_norm, v_meta_tokens, v_l0_mix_norm, v_l0_w_in, v_l0_ssd_conv_w, v_l0_ssd_conv_b, v_l0_ssd_dt_bias, v_l0_ssd_a_log, v_l0_ssd_d, v_l0_ssd_norm, v_l0_ret_norm, v_l0_w_out, v_l0_ffn_norm, v_l0_ffn_w_in, v_l0_ffn_conv_w, v_l0_ffn_conv_b, v_l0_ffn_w_out, v_l1_mix_norm, v_l1_w_in, v_l1_lru_conv_w, v_l1_lru_conv_b, v_l1_lru_wa, v_l1_lru_ba, v_l1_lru_wx, v_l1_lru_bx, v_l1_lru_lambda, v_l1_w_out, v_l1_ffn_norm, v_l1_ffn_w_in, v_l1_ffn_conv_w, v_l1_ffn_conv_b, v_l1_ffn_w_out, v_final_norm)
    p = dict(zip(_IN_NAMES, args))
    B, seq, _ = x.shape
    nch = (seq + CH) // CH
    Pn = nch * CH
    R = B * Pn
    chip = 2 * lax.axis_index("x") + lax.axis_index("y")
    row2 = lambda v: v.reshape(1, -1)
    pad128 = lambda v: jnp.pad(v, (0, 128 - v.shape[0])).reshape(1, 128)

    small_shapes = [p[n].shape for n in _SMALL_SHARDED]
    halved = lambda w: w.astype(_MXU).reshape(2, w.shape[0] // 2, w.shape[1])
    mine = {n: halved(p[n]) for n in _BIG}
    mine_small = _pack([p[n] for n in _SMALL_SHARDED])
    W = {}

    def set_weight(n, g):
        g = _fill_own(g, mine[n], chip)
        g = g.reshape(4, -1, g.shape[3])
        W[n] = jnp.concatenate([g[k] for k in range(4)], axis=1) if n in _BIG_COLS else g.reshape(-1, g.shape[2])

    def gather_on(*names):
        return ("gather", [mine[n] for n in names])

    def take_weights(names, got):
        for n, g in zip(names, got):
            set_weight(n, g)

    gathered = _gather_shards([mine['l0_w_in']], mine_small)
    set_weight('l0_w_in', gathered[0])
    g_small = _fill_own(gathered[-1], mine_small, chip)
    per_chip = [_unpack(g_small[k], small_shapes) for k in range(4)]
    for i, n in enumerate(_SMALL_SHARDED):
        W[n] = jnp.concatenate([per_chip[k][i] for k in range(4)], axis=1)
    w0 = W['l0_w_in']
    w0_main = jnp.concatenate([w0[:, 3088:], w0[:, :3072]], axis=1)
    w0_dt = jnp.pad(w0[:, 3072:3088], ((0, 0), (0, 112)))
    cos, sin = _rope_tables(nch)

    meta = jnp.broadcast_to(W['meta_tokens'][None], (B, N_META, D))
    h0 = jnp.concatenate([jnp.zeros((B, PAD, D), F32), meta, x], axis=1).reshape(R, D)
    n0, n0t = _rmsnorm_fwd(h0, row2(p['l0_mix_norm']), "norm_l0_mix")
    u0 = _mm(n0, w0_main, "nn", F32, "l0_in_proj")
    udt = _mm(n0, w0_dt, "nn", F32, "l0_dt_proj")
    a_log, d_skip, dt_bias = pad128(p['l0_ssd_a_log']), pad128(p['l0_ssd_d']), pad128(p['l0_ssd_dt_bias'])
    ssd_cb = row2(p['l0_ssd_conv_b'])
    act, dt, dtt, *got = _ssd_prep(u0, udt, W['l0_ssd_conv_w'], ssd_cb, dt_bias, B, nch, rider=gather_on('l0_w_out'))
    take_weights(['l0_w_out'], got)
    ycat0, ypre, hin, *got = _ssd_fwd(act, u0, dt, dtt, a_log, d_skip, row2(p['l0_ssd_norm']), B, nch,
                                      rider=gather_on('l0_ffn_w_in'))
    take_weights(['l0_ffn_w_in'], got)
    ycat0, opre, rin, *got = _ret_fwd(u0, ycat0, cos, sin, row2(p['l0_ret_norm']), B, nch,
                                      rider=gather_on('l0_ffn_w_out'))
    take_weights(['l0_ffn_w_out'], got)
    h1 = _mm(ycat0, W['l0_w_out'], "nn", F32, "l0_out_proj", add=h0)
    n1, n1t = _rmsnorm_fwd(h1, row2(p['l0_ffn_norm']), "norm_l0_ffn")
    uf0 = _mm(n1, W['l0_ffn_w_in'], "nn", F32, "l0_ffn_in")
    f0_cb = row2(p['l0_ffn_conv_b'])
    a0, *got = _ffn_act_fwd(uf0, W['l0_ffn_conv_w'], f0_cb, B, nch, rider=gather_on('l1_w_in'))
    take_weights(['l1_w_in'], got)
    h2 = _mm(a0, W['l0_ffn_w_out'], "nn", F32, "l0_ffn_out", add=h1)
    n2, n2t = _rmsnorm_fwd(h2, row2(p['l1_mix_norm']), "norm_l1_mix")
    u1 = _mm(n2, W['l1_w_in'], "nn", F32, "l1_in_proj")
    lru = (W['l1_lru_conv_w'], row2(p['l1_lru_conv_b']), p['l1_lru_wa'], row2(p['l1_lru_ba']), p['l1_lru_wx'],
           row2(p['l1_lru_bx']), row2(p['l1_lru_lambda']))
    later = ['l1_w_out', 'l1_ffn_w_in', 'l1_ffn_w_out']
    ycat1, stot, *got = _sb_fwd(u1, B, nch, rider=gather_on(*later))
    take_weights(later, got)
    ycat1, hs = _lru_fwd(u1, ycat1, *lru, B, nch)
    h3 = _mm(ycat1, W['l1_w_out'], "nn", F32, "l1_out_proj", add=h2)
    n3, n3t = _rmsnorm_fwd(h3, row2(p['l1_ffn_norm']), "norm_l1_ffn")
    uf1 = _mm(n3, W['l1_ffn_w_in'], "nn", F32, "l1_ffn_in")
    f1_cb = row2(p['l1_ffn_conv_b'])
    a1, = _ffn_act_fwd(uf1, W['l1_ffn_conv_w'], f1_cb, B, nch)
    h4 = _mm(a1, W['l1_ffn_w_out'], "nn", F32, "l1_ffn_out", add=h3)
    dh4, lossp, dgf = _head(h4, row2(p['final_norm']), p['loss_target'].reshape(B * seq, D), B, nch)
    loss = lax.psum(jnp.sum(lossp[:, 0, 0]), ("x", "y", "c"))

    G = {'final_norm': dgf[:, 0].sum(0)}

    core = lax.axis_index("c").reshape(1).astype(jnp.int32)

    def chip_sums(names, tag):
        stacked = []
        for n in names:
            g = G[n]
            if n in _BIG_COLS:
                stacked.append(g.reshape(g.shape[0], 4, g.shape[1] // 4).transpose(1, 0, 2))
            else:
                stacked.append(g.reshape(4, g.shape[0] // 4, g.shape[1]))
        theirs = _swap_halves(stacked, "swap_halves_" + tag)
        return {n: _chip_sum(g, t, core, "chip_sum_" + n) for n, g, t in zip(names, stacked, theirs)}

    parts = {}

    def scatter_on(names, tag):
        sums = chip_sums(names, tag)
        return sums, ("scatter", [sums[n] for n in names])

    def take_parts(names, sums, got):
        for n, g in zip(names, got):
            parts[n] = _fill_own(g, lax.dynamic_index_in_dim(sums[n], chip, 0, keepdims=False), chip)

    def ffn_bwd(layer, dh_out, h_in, nt_in, uf, a_act, cb, rider=None):
        pre = f"l{layer}_"
        w_in, w_out, cw = W[pre + 'ffn_w_in'], W[pre + 'ffn_w_out'], W[pre + 'ffn_conv_w']
        da = _mm(dh_out, w_out, "nt", F32, pre + "ffn_out_dgrad")
        G[pre + 'ffn_w_out'] = _mm(a_act, dh_out, "tn", F32, pre + "ffn_out_wgrad")
        dug, duu, dwg, dwu, *rode = _ffn_act_bwd(da, uf, cw, cb, nch, pre + "ffn_act_bwd", rider=rider)
        G[pre + 'ffn_conv_w'] = jnp.concatenate([dwg[:3], dwu[:3]], axis=1)
        G[pre + 'ffn_conv_b'] = jnp.concatenate([dwg[7], dwu[7]])
        dn = _mm(dug, w_in, "nt", F32, pre + "ffn_in_dgrad_g")
        dn = _mm(duu, w_in, "nt", F32, pre + "ffn_in_dgrad_u", add=dn, b_off=FFN)
        G[pre + 'ffn_w_in'] = jnp.concatenate([_mm(nt_in, dug, "nn", F32, pre + "ffn_in_wgrad_g"),
                                               _mm(nt_in, duu, "nn", F32, pre + "ffn_in_wgrad_u")], axis=1)
        dh_in, dg = _rmsnorm_bwd(h_in, row2(p[pre + 'ffn_norm']), dn, dh_out, nch, pre + "ffn_norm_bwd")
        G[pre + 'ffn_norm'] = dg[0]
        return dh_in, rode

    dh3, _ = ffn_bwd(1, dh4, h3, n3t, uf1, a1, f1_cb)
    dy1 = _mm(dh3, W['l1_w_out'], "nt", F32, "l1_out_dgrad")
    G['l1_w_out'] = _mm(ycat1, dh3, "tn", F32, "l1_out_wgrad")
    done = ['l1_ffn_w_in', 'l1_ffn_w_out', 'l1_w_out']
    sums, rider = scatter_on(done, "a")
    dq, dk, dv, *got = _sb_bwd(dy1, u1, stot, B, nch, rider=rider)
    take_parts(done, sums, got)
    dgate, dxc, pgl, dwa, dwx = _lru_bwd(dy1, u1, hs, *lru, B, nch)
    dxr, dcw = _conv_bwd(dxc, u1, 4096, W['l1_lru_conv_w'], 4, "l1_lru_conv_bwd")
    pgl = pgl.sum(0)
    G['l1_lru_ba'], G['l1_lru_bx'], G['l1_lru_lambda'] = pgl[0], pgl[1], pgl[2]
    G['l1_lru_wa'], G['l1_lru_wx'] = dwa.sum(0), dwx.sum(0)
    G['l1_lru_conv_w'], G['l1_lru_conv_b'] = dcw[:4], dcw[7]
    dn, dws = None, []
    for i, piece in enumerate((dq, dk, dv, dgate, dxr)):
        dn = _mm(piece, W['l1_w_in'], "nt", F32, f"l1_in_dgrad_{i}", add=dn, b_off=1024 * i)
        dws.append(_mm(n2t, piece, "nn", F32, f"l1_in_wgrad_{i}"))
    G['l1_w_in'] = jnp.concatenate(dws, axis=1)
    dh2, dg = _rmsnorm_bwd(h2, row2(p['l1_mix_norm']), dn, dh3, nch, "l1_mix_norm_bwd")
    G['l1_mix_norm'] = dg[0]

    sums, rider = scatter_on(['l1_w_in'], "b")
    dh1, got = ffn_bwd(0, dh2, h1, n1t, uf0, a0, f0_cb, rider=rider)
    take_parts(['l1_w_in'], sums, got)
    dy0 = _mm(dh1, W['l0_w_out'], "nt", F32, "l0_out_dgrad")
    G['l0_w_out'] = _mm(ycat0, dh1, "tn", F32, "l0_out_wgrad")
    done = ['l0_ffn_w_in', 'l0_ffn_w_out', 'l0_w_out']
    sums, rider = scatter_on(done, "c")
    dz, dxs, dbm, dcm, ddt4, pgs, *got = _ssd_bwd(dy0, ypre, u0, act, dt, dtt, hin, a_log, d_skip,
                                                  row2(p['l0_ssd_norm']), B, nch, rider=rider)
    take_parts(done, sums, got)
    dpre, ddtr, pgd = _ssd_prep_bwd(dxs, dbm, dcm, ddt4, u0, udt, W['l0_ssd_conv_w'], ssd_cb, dt_bias, B, nch)
    dxbc, dcw0 = _conv_bwd(dpre, u0, U0_XBC, W['l0_ssd_conv_w'], 4, "l0_ssd_conv_bwd")
    dqkvg, pgr = _ret_bwd(dy0, u0, opre, rin, cos, sin, row2(p['l0_ret_norm']), B, nch)
    pgs = pgs.sum(0)
    G['l0_ssd_norm'] = pgs[:, 0, :].reshape(-1)
    G['l0_ssd_d'] = pgs[:, 1, :128].sum(0)[:SSD_HEADS]
    G['l0_ssd_a_log'] = pgs[:, 2, :128].sum(0)[:SSD_HEADS]
    G['l0_ssd_dt_bias'] = pgd.sum(0)[0, :SSD_HEADS]
    G['l0_ssd_conv_w'], G['l0_ssd_conv_b'] = dcw0[:4], dcw0[7]
    G['l0_ret_norm'] = pgr.sum(0)[0]
    dn = _mm(dqkvg, w0_main, "nt", F32, "l0_in_dgrad_qkvg")
    dn = _mm(dz, w0_main, "nt", F32, "l0_in_dgrad_z", add=dn, b_off=U0_Z)
    dn = _mm(dxbc, w0_main, "nt", F32, "l0_in_dgrad_xbc", add=dn, b_off=U0_XBC)
    dn = _mm(ddtr, w0_dt, "nt", F32, "l0_in_dgrad_dt", add=dn)
    G['l0_w_in'] = jnp.concatenate([
        _mm(n0t, dz, "nn", F32, "l0_in_wgrad_z"), _mm(n0t, dxbc, "nn", F32, "l0_in_wgrad_xbc"),
        _mm(n0t, ddtr, "nn", F32, "l0_in_wgrad_dt")[:, :SSD_HEADS], _mm(n0t, dqkvg, "nn", F32, "l0_in_wgrad_qkvg")], axis=1)
    dh0, dg = _rmsnorm_bwd(h0, row2(p['l0_mix_norm']), dn, dh1, nch, "l0_mix_norm_bwd")
    G['l0_mix_norm'] = dg[0]
    dh0 = dh0.reshape(B, Pn, D)
    grad_x = dh0[:, CH:]
    G['meta_tokens'] = dh0[:, PAD:CH].sum(0)

    sums = chip_sums(['l0_w_in'], "d")
    parts['l0_w_in'], = _scatter_sums([sums['l0_w_in']])
    reds = [_sum_chips(parts[n], "sum_chips_" + n) for n in _BIG]
    grads = {}
    for n, own, other in zip(_BIG, reds, _join_halves(reds)):
        both = jnp.where(core[0] == 0, jnp.stack([own, other]), jnp.stack([other, own]))
        grads[n] = both.reshape(-1, both.shape[2])
    small_full = _unpack(_allreduce_small(_pack([G[n] for n in _SMALL])), [G[n].shape for n in _SMALL])
    for n, g in zip(_SMALL, small_full):
        if n in _SMALL_SHARDED:
            cs = g.shape[1] // 4
            g = lax.dynamic_slice_in_dim(g, chip * cs, cs, axis=1)
        grads[n] = g.reshape(p[n].shape)

    delta, new_m, new_v = {}, {}, {}
    for n in _BIG:
        delta[n], new_m[n], new_v[n] = _adamw(p[n], grads[n], p['m_' + n], p['v_' + n], "adamw_" + n)
    shapes = [p[n].shape for n in _SMALL]
    outs = _adamw(_pack([p[n] for n in _SMALL]), _pack([grads[n] for n in _SMALL]), _pack([p['m_' + n] for n in _SMALL]),
                  _pack([p['v_' + n] for n in _SMALL]), "adamw_small")
    for dst, buf in zip((delta, new_m, new_v), outs):
        for n, a in zip(_SMALL, _unpack(buf, shapes)):
            dst[n] = a
    return (loss, grad_x, *[grads[n] for n in _W_NAMES], *[delta[n] for n in _W_NAMES],
            *[new_m[n] for n in _W_NAMES], *[new_v[n] for n in _W_NAMES])
```

```python
import math

import numpy as np
import jax
import jax.numpy as jnp
from jax import lax
from jax.experimental import pallas as pl
from jax.experimental.pallas import tpu as pltpu

F32 = jnp.float32
BF16 = jnp.bfloat16
_MXU = jnp.bfloat16

D = 1024
CH = 128
N_META = 16
PAD = CH - N_META
EPS = 1e-6

SSD_HEADS = 16
SSD_HD = 64
SSD_GROUPS = 4
RET_HEADS = 4
RET_DK = 256
SB_HEADS = 16
SB_HD = 64
LRU_BLOCKS = 8
LRU_C = 8.0
FFN = 2816
U0_Z = 4096
U0_XBC = 5120

VMEM_LIMIT = 56 * 1024 * 1024


def _cparams(sem):
    return pltpu.CompilerParams(dimension_semantics=sem, vmem_limit_bytes=VMEM_LIMIT)


def _dot(a, b, dims=((1,), (0,))):
    return lax.dot_general(a.astype(_MXU), b.astype(_MXU), (dims, ((), ())), preferred_element_type=F32)


def _dot_nt(a, b):
    return _dot(a, b, ((1,), (1,)))


def _dot_tn(a, b):
    return _dot(a.T, b)


def _dot_exact(a, b):
    return lax.dot_general(a, b, (((1,), (0,)), ((), ())), preferred_element_type=F32,
                           precision=lax.Precision.HIGHEST)


def _dot_split(x, m01):
    hi = x.astype(BF16)
    lo = (x - hi.astype(F32)).astype(BF16)
    m = m01.astype(BF16)
    return jnp.dot(hi, m, preferred_element_type=F32) + jnp.dot(lo, m, preferred_element_type=F32)


def _sigmoid(x):
    return 0.5 * jnp.tanh(0.5 * x) + 0.5


def _softplus(x):
    return jnp.maximum(x, 0.0) + jnp.log1p(jnp.exp(-jnp.abs(x)))


def _silu(x):
    return x * _sigmoid(x)


def _dsilu(x):
    s = _sigmoid(x)
    return s * (1.0 + x * (1.0 - s))


_GELU_C = math.sqrt(2.0 / math.pi)


def _gelu(x):
    return 0.5 * x * (1.0 + jnp.tanh(_GELU_C * (x + 0.044715 * x * x * x)))


def _dgelu(x):
    t = jnp.tanh(_GELU_C * (x + 0.044715 * x * x * x))
    return 0.5 * (1.0 + t) + 0.5 * x * (1.0 - t * t) * _GELU_C * (1.0 + 3.0 * 0.044715 * x * x)


def _row_ids(n, cols=1):
    return lax.broadcasted_iota(jnp.int32, (n, cols), 0)


def _lane_ids(rows, n):
    return lax.broadcasted_iota(jnp.int32, (rows, n), 1)


def _real_rows(chunk):
    return chunk * CH + _row_ids(CH) >= PAD


def _shift_down(prev8, cur, s):
    cat = jnp.concatenate([prev8, cur], axis=0)
    return pltpu.roll(cat, s, axis=0)[8:]


def _shift_up(cur, next8, s):
    n = cur.shape[0]
    cat = jnp.concatenate([cur, next8], axis=0)
    return pltpu.roll(cat, n + 8 - s, axis=0)[:n]


def _conv_pre(prev8, cur, w_ref, b_ref, K):
    acc = cur * w_ref[K - 1:K, :] + b_ref[...]
    for s in range(1, K):
        acc = acc + _shift_down(prev8, cur, s) * w_ref[K - 1 - s:K - s, :]
    return acc


def _prev8_map(nch, col):
    return lambda b, c: (jnp.maximum((b * nch + c) * (CH // 8) - 1, 0), col)


def _matmul(a, b, mode, out_dtype, tm, tn, tk, name, add=None, b_off=0):
    if mode == "nn":
        (M, K), (_, N) = a.shape, b.shape
    elif mode == "nt":
        (M, K), N = a.shape, b.shape[0]
    else:
        (K, M), (_, N) = a.shape, b.shape
    tm, tn, tk = min(tm, M), min(tn, N), min(tk, K)
    assert M % tm == 0 and N % tn == 0 and K % tk == 0 and b_off % tk == 0, (name, M, N, K, tm, tn, tk)
    koff = b_off // tk
    nk = K // tk
    dims = {"nn": ((1,), (0,)), "nt": ((1,), (1,)), "tn": ((0,), (0,))}[mode]
    if mode == "tn":
        a_spec = pl.BlockSpec((tk, tm), lambda i, j, k: (k, i))
    else:
        a_spec = pl.BlockSpec((tm, tk), lambda i, j, k: (i, k))
    if mode == "nt":
        b_spec = pl.BlockSpec((tn, tk), lambda i, j, k: (j, k + koff))
    else:
        b_spec = pl.BlockSpec((tk, tn), lambda i, j, k: (k, j))
    o_spec = pl.BlockSpec((tm, tn), lambda i, j, k: (i, j))
    has_add = add is not None

    def body(a_ref, b_ref, *rest):
        if has_add:
            add_ref, o_ref, acc = rest
        else:
            o_ref, acc = rest
        k = pl.program_id(2)

        @pl.when(k == 0)
        def _():
            acc[...] = jnp.zeros_like(acc)

        acc[...] += _dot(a_ref[...], b_ref[...], dims)

        @pl.when(k == nk - 1)
        def _():
            r = acc[...]
            if has_add:
                r = r + add_ref[...].astype(F32)
            o_ref[...] = r.astype(out_dtype)

    in_specs = [a_spec, b_spec] + ([o_spec] if has_add else [])
    args = (a, b) + ((add,) if has_add else ())
    return pl.pallas_call(
        body, name=name, grid=(M // tm, N // tn, nk),
        in_specs=in_specs, out_specs=o_spec,
        out_shape=jax.ShapeDtypeStruct((M, N), out_dtype),
        scratch_shapes=[pltpu.VMEM((tm, tn), F32)],
        compiler_params=_cparams(("parallel", "parallel", "arbitrary")),
    )(*args)


def _tile(n, prefs):
    for t in prefs:
        if n % t == 0:
            return t
    return n


def _mm(a, b, mode, out_dtype, name, add=None, b_off=0):
    if mode == "tn":
        K, M = a.shape
        N = b.shape[1]
        tm, tn, tk = _tile(M, (1024, 1408, 512, 256, 128)), _tile(N, (1024, 1408, 512, 256, 128)), _tile(K, (2176, 384, 256, 128))
    else:
        M, K = a.shape
        N = b.shape[1] if mode == "nn" else b.shape[0]
        tm = _tile(M, (1088, 1024, 768, 512, 384, 256, 128))
        tn = _tile(N, (1024, 1408, 512, 256, 128))
        tk = _tile(K, (2176, 1024, 1408, 512, 256, 128))
    return _matmul(a, b, mode, out_dtype, tm, tn, tk, name, add=add, b_off=b_off)


def _rmsnorm_fwd(h, g, name):
    R = h.shape[0]
    tr = 2 * CH

    def body(h_ref, g_ref, o_ref, ot_ref):
        x = h_ref[...]
        r = lax.rsqrt(jnp.mean(x * x, axis=-1, keepdims=True) + EPS)
        y = x * r * g_ref[...]
        o_ref[...] = y.astype(o_ref.dtype)
        ot_ref[...] = y.T.astype(ot_ref.dtype)

    return pl.pallas_call(
        body, name=name, grid=(R // tr,),
        in_specs=[pl.BlockSpec((tr, D), lambda i: (i, 0)), pl.BlockSpec((1, D), lambda i: (0, 0))],
        out_specs=[pl.BlockSpec((tr, D), lambda i: (i, 0)), pl.BlockSpec((D, tr), lambda i: (0, i))],
        out_shape=[jax.ShapeDtypeStruct((R, D), _MXU), jax.ShapeDtypeStruct((D, R), _MXU)],
        compiler_params=_cparams(("parallel",)),
    )(h, g)


def _rmsnorm_bwd(h, g, dn, dres, nch, name):
    R = h.shape[0]
    per = 4
    tr = nch * CH // per

    def body(h_ref, g_ref, dn_ref, dres_ref, dh_ref, dg_ref):
        i = pl.program_id(0)
        x = h_ref[...]
        r = lax.rsqrt(jnp.mean(x * x, axis=-1, keepdims=True) + EPS)
        xhat = x * r
        dn_v = dn_ref[...]
        dx = dn_v * g_ref[...]
        dh = r * (dx - xhat * jnp.mean(dx * xhat, axis=-1, keepdims=True))
        keep = (i % per) * tr + _row_ids(tr) >= PAD
        dh_ref[...] = jnp.where(keep, dres_ref[...] + dh, 0.0)

        @pl.when(i == 0)
        def _():
            dg_ref[...] = jnp.zeros_like(dg_ref)

        dg_ref[...] += jnp.sum(dn_v * xhat, axis=0, keepdims=True)

    row = pl.BlockSpec((tr, D), lambda i: (i, 0))
    vec = pl.BlockSpec((1, D), lambda i: (0, 0))
    return pl.pallas_call(
        body, name=name, grid=(R // tr,),
        in_specs=[row, vec, row, row], out_specs=[row, vec],
        out_shape=[jax.ShapeDtypeStruct((R, D), F32), jax.ShapeDtypeStruct((1, D), F32)],
        compiler_params=_cparams(("arbitrary",)),
    )(h, g, dn, dres)


def _ssd_prep(u0, udt, conv_w, conv_b, dt_bias, B, nch, rider=None):
    R = u0.shape[0]

    def body(xs_ref, xsp_ref, bc_ref, bcp_ref, udt_ref, w0_ref, w1_ref, b0_ref, b1_ref, dtb_ref,
             act_ref, dt_ref, dtt_ref):
        keep = _real_rows(pl.program_id(1))
        a0 = _silu(_conv_pre(xsp_ref[...], xs_ref[...], w0_ref, b0_ref, 4))
        a1 = _silu(_conv_pre(bcp_ref[...], bc_ref[...], w1_ref, b1_ref, 4))
        act_ref[:, :1024] = jnp.where(keep, a0, 0.0)
        act_ref[:, 1024:] = jnp.where(keep, a1, 0.0)
        ok = jnp.logical_and(keep, _lane_ids(1, 128) < SSD_HEADS)
        dt = jnp.where(ok, _softplus(udt_ref[...] + dtb_ref[...]), 0.0)
        dt_ref[...] = dt
        dtt_ref[...] = dt.T

    row = lambda col: pl.BlockSpec((CH, 1024), lambda b, c: (b * nch + c, col))
    prev = lambda col: pl.BlockSpec((8, 1024), _prev8_map(nch, col))
    kw = dict(
        grid=(B, nch),
        in_specs=[row(5), prev(5), row(6), prev(6),
                  pl.BlockSpec((CH, 128), lambda b, c: (b * nch + c, 0)),
                  pl.BlockSpec((4, 1024), lambda b, c: (0, 0)), pl.BlockSpec((4, 1024), lambda b, c: (0, 1)),
                  pl.BlockSpec((1, 1024), lambda b, c: (0, 0)), pl.BlockSpec((1, 1024), lambda b, c: (0, 1)),
                  pl.BlockSpec((1, 128), lambda b, c: (0, 0))],
        out_specs=[pl.BlockSpec((CH, 2048), lambda b, c: (b * nch + c, 0)),
                   pl.BlockSpec((CH, 128), lambda b, c: (b * nch + c, 0)),
                   pl.BlockSpec((128, CH), lambda b, c: (0, b * nch + c))],
        out_shape=[jax.ShapeDtypeStruct((R, 2048), F32), jax.ShapeDtypeStruct((R, 128), F32),
                   jax.ShapeDtypeStruct((128, R), F32)])
    return _call(body, "ssd_prep", ("arbitrary", "arbitrary"), kw,
                 (u0, u0, u0, u0, udt, conv_w, conv_w, conv_b, conv_b, dt_bias), rider)


def _ssd_head_terms(h, a_vec, dt_v, dtt_v, dsk_v):
    lane = _lane_ids(1, 128)
    sub = _row_ids(128)
    r = _row_ids(CH, CH)
    cidx = _lane_ids(CH, CH)
    a_h = jnp.sum(jnp.where(lane == h, a_vec, 0.0), axis=1, keepdims=True)
    dt_col = jnp.sum(jnp.where(lane == h, dt_v, 0.0), axis=1, keepdims=True)
    dt_row = jnp.sum(jnp.where(sub == h, dtt_v, 0.0), axis=0, keepdims=True)
    cs_col = jnp.sum(jnp.where(r >= cidx, dt_row * a_h, 0.0), axis=1, keepdims=True)
    cs_row = jnp.sum(jnp.where(r <= cidx, dt_col * a_h, 0.0), axis=0, keepdims=True)
    tot = jnp.sum(dt_col * a_h, axis=0, keepdims=True)
    dsk = jnp.sum(jnp.where(lane == h, dsk_v, 0.0), axis=1, keepdims=True)
    return a_h, dt_col, cs_col, cs_row, tot, dsk


def _ssd_fwd(act, u0, dt, dtt, a_log, d_skip, norm_g, B, nch, rider=None):
    R = act.shape[0]

    def body(xs_ref, bm_ref, cm_ref, z_ref, dt_ref, dtt_ref, alog_ref, dsk_ref, ng_ref,
             out_ref, ypre_ref, hin_ref, H):
        g = pl.program_id(1)
        c = pl.program_id(2)

        @pl.when(c == 0)
        def _():
            H[...] = jnp.zeros_like(H)

        hin_ref[...] = H[...]
        a_vec = -jnp.exp(alog_ref[...])
        dt_v = dt_ref[...]
        dtt_v = dtt_ref[...]
        hm = _lane_ids(1, 128) < SSD_HD
        r = _row_ids(CH, CH)
        cidx = _lane_ids(CH, CH)
        Bm = bm_ref[...]
        Cm = cm_ref[...]
        CB = _dot_nt(Cm, Bm)
        ys = []
        for pair in range(2):
            cols = slice(128 * pair, 128 * pair + 128)
            xraw = xs_ref[:, cols]
            t = [_ssd_head_terms(4 * g + 2 * pair + j, a_vec, dt_v, dtt_v, dsk_ref[...]) for j in range(2)]
            sel = lambda f: jnp.where(hm, f(t[0]), f(t[1]))
            dtp = sel(lambda q: q[1])
            Ep = sel(lambda q: jnp.exp(q[2]))
            Wp = sel(lambda q: jnp.exp(q[4] - q[2]))
            etot = sel(lambda q: jnp.exp(q[4]))
            dsk = sel(lambda q: q[5])
            X = xraw * dtp
            ydiag = jnp.zeros((CH, 128), F32)
            for j in range(2):
                Lm = jnp.where(r >= cidx, jnp.exp(t[j][2] - t[j][3]), 0.0)
                Xh = jnp.where(hm if j == 0 else jnp.logical_not(hm), X, 0.0)
                ydiag = ydiag + _dot(CB * Lm, Xh)
            Hp = H[:, cols]
            yoff = Ep * _dot(Cm, Hp)
            S = _dot(Bm.T, X * Wp)
            H[:, cols] = etot * Hp + S
            ys.append(ydiag + yoff + xraw * dsk)
        y = jnp.concatenate(ys, axis=1)
        ypre_ref[...] = y
        yg = y * _silu(z_ref[...])
        rr = lax.rsqrt(jnp.mean(yg * yg, axis=-1, keepdims=True) + EPS)
        out_ref[...] = jnp.where(_real_rows(c), yg * rr * ng_ref[...], 0.0).astype(out_ref.dtype)

    rowb = lambda w, colf: pl.BlockSpec((CH, w), lambda b, g, c: (b * nch + c, colf(g)))
    vec = pl.BlockSpec((1, 128), lambda b, g, c: (0, 0))
    kw = dict(
        grid=(B, SSD_GROUPS, nch),
        in_specs=[rowb(256, lambda g: g), rowb(128, lambda g: 8 + g), rowb(128, lambda g: 12 + g),
                  rowb(256, lambda g: 16 + g), rowb(128, lambda g: 0),
                  pl.BlockSpec((128, CH), lambda b, g, c: (0, b * nch + c)),
                  vec, vec, pl.BlockSpec((1, 256), lambda b, g, c: (0, g))],
        out_specs=[rowb(256, lambda g: g), rowb(256, lambda g: g),
                   pl.BlockSpec((None, None, None, 128, 256), lambda b, g, c: (b, g, c, 0, 0))],
        out_shape=[jax.ShapeDtypeStruct((R, 2048), _MXU), jax.ShapeDtypeStruct((R, 1024), F32),
                   jax.ShapeDtypeStruct((B, SSD_GROUPS, nch, 128, 256), F32)],
        scratch_shapes=[pltpu.VMEM((128, 256), F32)])
    return _call(body, "ssd_fwd", ("arbitrary", "arbitrary", "arbitrary"), kw,
                 (act, act, act, u0, dt, dtt, a_log, d_skip, norm_g), rider)


def _ssd_bwd(dycat, ypre, u0, act, dt, dtt, hin, a_log, d_skip, norm_g, B, nch, rider=None):
    R = act.shape[0]

    def body(dy_ref, ypre_ref, z_ref, xs_ref, bm_ref, cm_ref, dt_ref, dtt_ref, hin_ref, alog_ref, dsk_ref, ng_ref,
             dz_ref, dxs_ref, db_ref, dc_ref, ddt_ref, pg_ref, dH):
        g = pl.program_id(1)
        c = nch - 1 - pl.program_id(2)

        @pl.when(pl.program_id(2) == 0)
        def _():
            dH[...] = jnp.zeros_like(dH)
            pg_ref[...] = jnp.zeros_like(pg_ref)

        z = z_ref[...]
        y = ypre_ref[...]
        ng = ng_ref[...]
        dout = jnp.where(_real_rows(c), dy_ref[...], 0.0)
        sz = _sigmoid(z)
        yg = y * z * sz
        rr = lax.rsqrt(jnp.mean(yg * yg, axis=-1, keepdims=True) + EPS)
        nrm = yg * rr
        pg_ref[0:1, :] += jnp.sum(dout * nrm, axis=0, keepdims=True)
        dn = dout * ng
        dyg = rr * (dn - nrm * jnp.mean(dn * nrm, axis=-1, keepdims=True))
        dy = dyg * z * sz
        dz_ref[...] = (dyg * y * (sz * (1.0 + z * (1.0 - sz)))).astype(dz_ref.dtype)

        a_vec = -jnp.exp(alog_ref[...])
        dt_v = dt_ref[...]
        dtt_v = dtt_ref[...]
        lane = _lane_ids(1, 128)
        hm = lane < SSD_HD
        r = _row_ids(CH, CH)
        cidx = _lane_ids(CH, CH)
        last = _row_ids(CH) == CH - 1
        Bm = bm_ref[...]
        Cm = cm_ref[...]
        CB = _dot_nt(Cm, Bm)
        CBT = _dot_nt(Bm, Cm)
        dB = jnp.zeros((CH, 128), F32)
        dC = jnp.zeros((CH, 128), F32)
        dcs_all = jnp.zeros((CH, 128), F32)
        dtx_all = jnp.zeros((CH, 128), F32)
        dd_row = jnp.zeros((1, 128), F32)
        dxs = []
        for pair in range(2):
            cols = slice(128 * pair, 128 * pair + 128)
            xraw = xs_ref[:, cols]
            dyp = dy[:, cols]
            heads = [4 * g + 2 * pair + j for j in range(2)]
            t = [_ssd_head_terms(heads[j], a_vec, dt_v, dtt_v, dsk_ref[...]) for j in range(2)]
            sel = lambda f: jnp.where(hm, f(t[0]), f(t[1]))
            hsum = lambda v, j: jnp.sum(jnp.where(hm if j == 0 else jnp.logical_not(hm), v, 0.0), axis=1, keepdims=True)
            dtp = sel(lambda q: q[1])
            Ep = sel(lambda q: jnp.exp(q[2]))
            Wp = sel(lambda q: jnp.exp(q[4] - q[2]))
            etot = sel(lambda q: jnp.exp(q[4]))
            dsk = sel(lambda q: q[5])
            X = xraw * dtp
            Hp = hin_ref[:, cols]
            dHn = dH[:, cols]
            dskip = jnp.sum(dyp * xraw, axis=0, keepdims=True)
            yoff = Ep * _dot(Cm, Hp)
            dE = dyp * yoff
            dC = dC + _dot_nt(dyp * Ep, Hp)
            dH[:, cols] = etot * dHn + _dot(Cm.T, dyp * Ep)
            BdS = _dot(Bm, dHn)
            dX = Wp * BdS
            ew = X * BdS * Wp
            dB = dB + _dot_nt(X * Wp, dHn)
            hh = jnp.sum(dHn * Hp, axis=0, keepdims=True) * etot
            for j in range(2):
                hmask = hm if j == 0 else jnp.logical_not(hm)
                cs_col, cs_row = t[j][2], t[j][3]
                Lm = jnp.where(r >= cidx, jnp.exp(cs_col - cs_row), 0.0)
                LmT = jnp.where(cidx >= r, jnp.exp(cs_row - cs_col), 0.0)
                dyh = jnp.where(hmask, dyp, 0.0)
                Xh = jnp.where(hmask, X, 0.0)
                dM = _dot_nt(dyh, Xh)
                dMT = _dot_nt(Xh, dyh)
                M = CB * Lm
                MT = CBT * LmT
                dX = dX + _dot(MT, dyh)
                dC = dC + _dot(dM * Lm, Bm)
                dB = dB + _dot(dMT * LmT, Cm)
                g_rows = jnp.sum(dM * M, axis=1, keepdims=True)
                g_cols = jnp.sum(dMT * MT, axis=1, keepdims=True)
                dtot = (jnp.sum(hsum(ew, j), axis=0, keepdims=True)
                        + jnp.sum(jnp.where(hmask, hh, 0.0), axis=1, keepdims=True))
                dcs = g_rows - g_cols + hsum(dE, j) - hsum(ew, j) + jnp.where(last, dtot, 0.0)
                dcs_all = dcs_all + jnp.where(lane == heads[j], dcs, 0.0)
                dtx_all = dtx_all + jnp.where(lane == heads[j], hsum(dX * xraw, j), 0.0)
                dd_row = dd_row + jnp.where(lane == heads[j],
                                            jnp.sum(jnp.where(hmask, dskip, 0.0), axis=1, keepdims=True), 0.0)
            dxs.append(dX * dtp + dyp * dsk)
        dxs_ref[...] = jnp.concatenate(dxs, axis=1)
        db_ref[...] = dB
        dc_ref[...] = dC
        dadt = _dot_exact(jnp.where(cidx >= r, 1.0, 0.0), dcs_all)
        ddt_ref[...] = dadt * a_vec + dtx_all
        pg_ref[1:2, 0:128] += dd_row
        pg_ref[2:3, 0:128] += jnp.sum(dadt * dt_v, axis=0, keepdims=True) * a_vec

    rowb = lambda w, colf: pl.BlockSpec((CH, w), lambda b, g, c: (b * nch + nch - 1 - c, colf(g)))
    vec = pl.BlockSpec((1, 128), lambda b, g, c: (0, 0))
    kw = dict(
        grid=(B, SSD_GROUPS, nch),
        in_specs=[rowb(256, lambda g: g), rowb(256, lambda g: g), rowb(256, lambda g: 16 + g), rowb(256, lambda g: g),
                  rowb(128, lambda g: 8 + g), rowb(128, lambda g: 12 + g), rowb(128, lambda g: 0),
                  pl.BlockSpec((128, CH), lambda b, g, c: (0, b * nch + nch - 1 - c)),
                  pl.BlockSpec((None, None, None, 128, 256), lambda b, g, c: (b, g, nch - 1 - c, 0, 0)),
                  vec, vec, pl.BlockSpec((1, 256), lambda b, g, c: (0, g))],
        out_specs=[rowb(256, lambda g: g), rowb(256, lambda g: g), rowb(128, lambda g: g), rowb(128, lambda g: g),
                   rowb(128, lambda g: g),
                   pl.BlockSpec((None, None, 8, 256), lambda b, g, c: (b, g, 0, 0))],
        out_shape=[jax.ShapeDtypeStruct((R, 1024), _MXU), jax.ShapeDtypeStruct((R, 1024), F32),
                   jax.ShapeDtypeStruct((R, 512), F32), jax.ShapeDtypeStruct((R, 512), F32),
                   jax.ShapeDtypeStruct((R, 512), F32), jax.ShapeDtypeStruct((B, SSD_GROUPS, 8, 256), F32)],
        scratch_shapes=[pltpu.VMEM((128, 256), F32)])
    return _call(body, "ssd_bwd", ("arbitrary", "arbitrary", "arbitrary"), kw,
                 (dycat, ypre, u0, act, act, act, dt, dtt, hin, a_log, d_skip, norm_g), rider)


def _ssd_prep_bwd(dxs, dB, dC, ddt4, u0, udt, conv_w, conv_b, dt_bias, B, nch, rider=None):
    R = u0.shape[0]

    def body(dxs_ref, db_ref, dc_ref, ddt_ref, xs_ref, xsp_ref, bc_ref, bcp_ref, udt_ref, w0_ref, w1_ref, b0_ref, b1_ref,
             dtb_ref, dpre_ref, ddtr_ref, pgd_ref):
        c = pl.program_id(1)

        @pl.when(c == 0)
        def _():
            pgd_ref[...] = jnp.zeros_like(pgd_ref)

        keep = _real_rows(c)
        p0 = _conv_pre(xsp_ref[...], xs_ref[...], w0_ref, b0_ref, 4)
        p1 = _conv_pre(bcp_ref[...], bc_ref[...], w1_ref, b1_ref, 4)
        dpre_ref[:, :1024] = jnp.where(keep, dxs_ref[...] * _dsilu(p0), 0.0)
        dpre_ref[:, 1024:] = jnp.where(keep, jnp.concatenate([db_ref[...], dc_ref[...]], axis=1) * _dsilu(p1), 0.0)
        ddt = ddt_ref[:, 0:128] + ddt_ref[:, 128:256] + ddt_ref[:, 256:384] + ddt_ref[:, 384:512]
        ok = jnp.logical_and(keep, _lane_ids(1, 128) < SSD_HEADS)
        dr = jnp.where(ok, ddt * _sigmoid(udt_ref[...] + dtb_ref[...]), 0.0)
        ddtr_ref[...] = dr
        pgd_ref[0:1, :] += jnp.sum(dr, axis=0, keepdims=True)

    rw = lambda w: pl.BlockSpec((CH, w), lambda b, c: (b * nch + c, 0))
    row = lambda col: pl.BlockSpec((CH, 1024), lambda b, c: (b * nch + c, col))
    prev = lambda col: pl.BlockSpec((8, 1024), _prev8_map(nch, col))
    kw = dict(
        grid=(B, nch),
        in_specs=[rw(1024), rw(512), rw(512), rw(512), row(5), prev(5), row(6), prev(6), rw(128),
                  pl.BlockSpec((4, 1024), lambda b, c: (0, 0)), pl.BlockSpec((4, 1024), lambda b, c: (0, 1)),
                  pl.BlockSpec((1, 1024), lambda b, c: (0, 0)), pl.BlockSpec((1, 1024), lambda b, c: (0, 1)),
                  pl.BlockSpec((1, 128), lambda b, c: (0, 0))],
        out_specs=[rw(2048), rw(128), pl.BlockSpec((None, 8, 128), lambda b, c: (b, 0, 0))],
        out_shape=[jax.ShapeDtypeStruct((R, 2048), F32), jax.ShapeDtypeStruct((R, 128), F32),
                   jax.ShapeDtypeStruct((B, 8, 128), F32)])
    return _call(body, "ssd_prep_bwd", ("arbitrary", "arbitrary"), kw,
                 (dxs, dB, dC, ddt4, u0, u0, u0, u0, udt, conv_w, conv_w, conv_b, conv_b, dt_bias), rider)


def _conv_bwd(dpre, xin, xin_col, w, K, name, tc=1024):
    R, C = dpre.shape
    assert C % tc == 0 and xin_col % tc == 0
    nr = R // CH
    xoff = xin_col // tc

    def body(dp_ref, dpn_ref, x_ref, xp_ref, w_ref, din_ref, dw_ref):
        i = pl.program_id(1)

        @pl.when(i == 0)
        def _():
            dw_ref[...] = jnp.zeros_like(dw_ref)

        dp = dp_ref[...]
        nxt = dpn_ref[...] * (i < nr - 1).astype(F32)
        x = x_ref[...]
        xp = xp_ref[...]
        din = dp * w_ref[K - 1:K, :]
        dw_ref[K - 1:K, :] += jnp.sum(dp * x, axis=0, keepdims=True)
        dw_ref[7:8, :] += jnp.sum(dp, axis=0, keepdims=True)
        for s in range(1, K):
            din = din + _shift_up(dp, nxt, s) * w_ref[K - 1 - s:K - s, :]
            dw_ref[K - 1 - s:K - s, :] += jnp.sum(dp * _shift_down(xp, x, s), axis=0, keepdims=True)
        din_ref[...] = din.astype(din_ref.dtype)

    return pl.pallas_call(
        body, name=name, grid=(C // tc, nr),
        in_specs=[pl.BlockSpec((CH, tc), lambda j, i: (i, j)),
                  pl.BlockSpec((8, tc), lambda j, i: (jnp.minimum((i + 1) * (CH // 8), nr * (CH // 8) - 1), j)),
                  pl.BlockSpec((CH, tc), lambda j, i: (i, xoff + j)),
                  pl.BlockSpec((8, tc), lambda j, i: (jnp.maximum(i * (CH // 8) - 1, 0), xoff + j)),
                  pl.BlockSpec((K, tc), lambda j, i: (0, j))],
        out_specs=[pl.BlockSpec((CH, tc), lambda j, i: (i, j)),
                   pl.BlockSpec((8, tc), lambda j, i: (0, j))],
        out_shape=[jax.ShapeDtypeStruct((R, C), _MXU), jax.ShapeDtypeStruct((8, C), F32)],
        compiler_params=_cparams(("parallel", "arbitrary")),
    )(dpre, dpre, xin, xin, w)


_RET_LG = [float(v) for v in np.log1p(-np.exp2(-5.0 - np.arange(RET_HEADS, dtype=np.float32))).astype(np.float32)]
_RET_SCALE = RET_DK ** -0.5


def _rope_tables(nch):
    half = RET_DK // 2
    inv_freq = 1.0 / (10000.0 ** (jnp.arange(half, dtype=F32) / (half - 1)))
    pos = jnp.arange(nch * CH, dtype=F32) - PAD
    ang = pos[:, None] * inv_freq[None, :]
    return jnp.cos(ang), jnp.sin(ang)


def _rot(x, cos, sin):
    x1, x2 = x[:, :128], x[:, 128:]
    return jnp.concatenate([x1 * cos - x2 * sin, x1 * sin + x2 * cos], axis=1)


def _unrot(d, cos, sin):
    d1, d2 = d[:, :128], d[:, 128:]
    return jnp.concatenate([d1 * cos + d2 * sin, d2 * cos - d1 * sin], axis=1)


def _ret_decays(lg):
    r = _row_ids(CH, CH)
    cidx = _lane_ids(CH, CH)
    diff = (r - cidx).astype(F32)
    decay = jnp.where(r >= cidx, jnp.exp(lg * jnp.maximum(diff, 0.0)), 0.0)
    decay_t = jnp.where(cidx >= r, jnp.exp(lg * jnp.maximum(-diff, 0.0)), 0.0)
    idx = _row_ids(CH).astype(F32)
    zeta = jnp.exp(lg * (CH - 1.0 - idx))
    xi = jnp.exp(lg * (idx + 1.0))
    return decay, decay_t, zeta, xi


def _ret_fwd(u0, ycat, cos, sin, norm_g, B, nch, rider=None):
    R = u0.shape[0]

    def body(u_ref, cos_ref, sin_ref, ng_ref, ycat_in, out_ref, opre_ref, rin_ref, Rst):
        c = pl.program_id(1)

        @pl.when(c == 0)
        def _():
            Rst[...] = jnp.zeros_like(Rst)

        cos_v, sin_v = cos_ref[...], sin_ref[...]
        for h in range(RET_HEADS):
            lg = _RET_LG[h]
            cols = slice(256 * h, 256 * h + 256)
            decay, _, zeta, xi = _ret_decays(lg)
            qr = _rot(u_ref[:, cols], cos_v, sin_v)
            kr = _rot(u_ref[:, 1024 + 256 * h:1024 + 256 * h + 256], cos_v, sin_v) * _RET_SCALE
            v = u_ref[:, 2048 + 256 * h:2048 + 256 * h + 256]
            gate = u_ref[:, 3072 + 256 * h:3072 + 256 * h + 256]
            Rh = Rst[h]
            rin_ref[h] = Rh
            inner = _dot(_dot_nt(qr, kr) * decay, v)
            cross = _dot(qr, Rh) * xi
            Rst[h] = math.exp(CH * lg) * Rh + _dot((kr * zeta).T, v)
            o = inner + cross
            opre_ref[:, cols] = o
            oc = o - jnp.mean(o, axis=-1, keepdims=True)
            rr = lax.rsqrt(jnp.mean(oc * oc, axis=-1, keepdims=True) + EPS)
            out_ref[:, cols] = (_silu(gate) * (oc * rr * ng_ref[:, cols])).astype(out_ref.dtype)

    kw = dict(
        grid=(B, nch),
        in_specs=[pl.BlockSpec((CH, 4096), lambda b, c: (b * nch + c, 0)),
                  pl.BlockSpec((CH, 128), lambda b, c: (c, 0)), pl.BlockSpec((CH, 128), lambda b, c: (c, 0)),
                  pl.BlockSpec((1, 1024), lambda b, c: (0, 0)),
                  pl.BlockSpec(memory_space=pl.ANY)],
        out_specs=[pl.BlockSpec((CH, 1024), lambda b, c: (b * nch + c, 1)),
                   pl.BlockSpec((CH, 1024), lambda b, c: (b * nch + c, 0)),
                   pl.BlockSpec((None, None, RET_HEADS, 256, 256), lambda b, c: (b, c, 0, 0, 0))],
        out_shape=[jax.ShapeDtypeStruct(ycat.shape, ycat.dtype), jax.ShapeDtypeStruct((R, 1024), F32),
                   jax.ShapeDtypeStruct((B, nch, RET_HEADS, 256, 256), F32)],
        scratch_shapes=[pltpu.VMEM((RET_HEADS, 256, 256), F32)],
        input_output_aliases={4: 0})
    return _call(body, "ret_fwd", ("arbitrary", "arbitrary"), kw, (u0, cos, sin, norm_g, ycat), rider)


def _ret_bwd(dycat, u0, opre, rin, cos, sin, norm_g, B, nch, rider=None):
    R = u0.shape[0]

    def body(dy_ref, u_ref, opre_ref, rin_ref, cos_ref, sin_ref, ng_ref, du_ref, pg_ref, dR):
        @pl.when(pl.program_id(1) == 0)
        def _():
            dR[...] = jnp.zeros_like(dR)
            pg_ref[...] = jnp.zeros_like(pg_ref)

        cos_v, sin_v = cos_ref[...], sin_ref[...]
        for h in range(RET_HEADS):
            lg = _RET_LG[h]
            cols = slice(256 * h, 256 * h + 256)
            decay, decay_t, zeta, xi = _ret_decays(lg)
            qr = _rot(u_ref[:, cols], cos_v, sin_v)
            kr = _rot(u_ref[:, 1024 + 256 * h:1024 + 256 * h + 256], cos_v, sin_v) * _RET_SCALE
            v = u_ref[:, 2048 + 256 * h:2048 + 256 * h + 256]
            gate = u_ref[:, 3072 + 256 * h:3072 + 256 * h + 256]
            ng = ng_ref[:, cols]
            o = opre_ref[:, cols]
            oc = o - jnp.mean(o, axis=-1, keepdims=True)
            rr = lax.rsqrt(jnp.mean(oc * oc, axis=-1, keepdims=True) + EPS)
            ohat = oc * rr
            dout = dy_ref[:, cols]
            du_ref[:, 3072 + 256 * h:3072 + 256 * h + 256] = (dout * (ohat * ng) * _dsilu(gate)).astype(du_ref.dtype)
            don = dout * _silu(gate)
            pg_ref[0:1, cols] += jnp.sum(don * ohat, axis=0, keepdims=True)
            dohat = don * ng
            do = rr * (dohat - jnp.mean(dohat, axis=-1, keepdims=True)
                       - ohat * jnp.mean(dohat * ohat, axis=-1, keepdims=True))
            Rh = rin_ref[h]
            dRn = dR[h]
            sc_t = _dot_nt(kr, qr) * decay_t
            dv = _dot(sc_t, do) + _dot(kr * zeta, dRn)
            ds = _dot_nt(do, v) * decay
            ds_t = _dot_nt(v, do) * decay_t
            dox = do * xi
            dq = _dot(ds, kr) + _dot_nt(dox, Rh)
            dk = _dot(ds_t, qr) + zeta * _dot_nt(v, dRn)
            dR[h] = math.exp(CH * lg) * dRn + _dot(qr.T, dox)
            du_ref[:, cols] = _unrot(dq, cos_v, sin_v).astype(du_ref.dtype)
            du_ref[:, 1024 + 256 * h:1024 + 256 * h + 256] = (_unrot(dk, cos_v, sin_v) * _RET_SCALE).astype(du_ref.dtype)
            du_ref[:, 2048 + 256 * h:2048 + 256 * h + 256] = dv.astype(du_ref.dtype)

    rmap = lambda b, c: (b * nch + nch - 1 - c, 0)
    kw = dict(
        grid=(B, nch),
        in_specs=[pl.BlockSpec((CH, 1024), lambda b, c: (b * nch + nch - 1 - c, 1)),
                  pl.BlockSpec((CH, 4096), rmap), pl.BlockSpec((CH, 1024), rmap),
                  pl.BlockSpec((None, None, RET_HEADS, 256, 256), lambda b, c: (b, nch - 1 - c, 0, 0, 0)),
                  pl.BlockSpec((CH, 128), lambda b, c: (nch - 1 - c, 0)),
                  pl.BlockSpec((CH, 128), lambda b, c: (nch - 1 - c, 0)),
                  pl.BlockSpec((1, 1024), lambda b, c: (0, 0))],
        out_specs=[pl.BlockSpec((CH, 4096), rmap), pl.BlockSpec((None, 8, 1024), lambda b, c: (b, 0, 0))],
        out_shape=[jax.ShapeDtypeStruct((R, 4096), _MXU), jax.ShapeDtypeStruct((B, 8, 1024), F32)],
        scratch_shapes=[pltpu.VMEM((RET_HEADS, 256, 256), F32)])
    return _call(body, "ret_bwd", ("arbitrary", "arbitrary"), kw, (dycat, u0, opre, rin, cos, sin, norm_g), rider)


_SB_SCALE = SB_HD ** -0.5


_SB_NB = 4


def _sb_valid(qb, kb, live):
    qpos = qb * CH + jnp.bitwise_and(_row_ids(2 * CH, CH), CH - 1)
    kpos = kb * CH + _lane_ids(2 * CH, CH)
    first = PAD + (1 - live) * (1 << 24)
    return jnp.logical_and(kpos < qpos, kpos >= first)


def _sb_softplus(z):
    return jnp.maximum(z, 0.0) + jnp.log(1.0 + jnp.exp(-jnp.abs(z)))


def _stack_heads(x):
    hm = _lane_ids(1, 128) < SB_HD
    return jnp.concatenate([jnp.where(hm, x, 0.0), jnp.where(hm, 0.0, x)], axis=0)


def _unstack_heads(x2):
    return jnp.where(_lane_ids(1, 128) < SB_HD, x2[:CH], x2[CH:])


def _sb_fwd(u1, B, nch, rider=None):
    R = u1.shape[0]
    Pn = nch * CH

    def body(q_ref, k_ref, v_ref, out_ref, s_ref):
        qb = pl.program_id(2)
        q2 = _stack_heads(q_ref[...] * _SB_SCALE).astype(_MXU)
        mgt = (_row_ids(CH, CH) > _lane_ids(CH, CH)).astype(F32)

        def step(i, carry):
            out2, acc = carry
            blocks = []
            for t in range(_SB_NB):
                kb = qb - _SB_NB * i - t
                live = (kb >= 0).astype(jnp.int32)
                kbc = jnp.maximum(kb, 0)
                start = pl.multiple_of(kbc * CH, CH)
                valid = _sb_valid(qb, kbc, live)
                z = _dot_nt(q2, k_ref[pl.ds(start, CH), :])
                sp = _sb_softplus(z)
                lm = jnp.where(valid, -sp, 0.0)
                blocks.append((valid, z - sp, _dot_split(lm, mgt), jnp.sum(lm, axis=1, keepdims=True), start))
            for valid, ls, loc, rs, start in blocks:
                w = jnp.where(valid, jnp.exp(ls + loc + acc), 0.0)
                out2 = out2 + _dot(w, v_ref[pl.ds(start, CH), :])
                acc = acc + rs
            return out2, acc

        trips = (qb + _SB_NB) // _SB_NB
        out2, acc = lax.fori_loop(0, trips, step, (jnp.zeros((2 * CH, 128), F32), jnp.zeros((2 * CH, 1), F32)))
        out_ref[...] = _unstack_heads(out2).astype(out_ref.dtype)
        s_ref[...] = _unstack_heads(jnp.broadcast_to(acc, (2 * CH, 128)))

    qspec = lambda off: pl.BlockSpec((CH, 128), lambda b, hp, qb: (b * nch + qb, off + hp))
    kspec = lambda off: pl.BlockSpec((Pn, 128), lambda b, hp, qb: (b, off + hp))
    kw = dict(grid=(B, SB_HEADS // 2, nch), in_specs=[qspec(0), kspec(8), kspec(16)], out_specs=[qspec(0), qspec(0)],
              out_shape=[jax.ShapeDtypeStruct((R, 2048), _MXU), jax.ShapeDtypeStruct((R, 1024), F32)])
    return _call(body, "sb_fwd", ("arbitrary", "arbitrary", "arbitrary"), kw, (u1, u1, u1), rider)


def _sb_bwd(dycat, u1, stot, B, nch, rider=None):
    R = u1.shape[0]
    Pn = nch * CH

    def body(q_ref, k_ref, v_ref, do_ref, s_ref, dq_ref, dk_ref, dv_ref):
        qb = pl.program_id(2)

        @pl.when(qb == 0)
        def _():
            dk_ref[...] = jnp.zeros_like(dk_ref)
            dv_ref[...] = jnp.zeros_like(dv_ref)

        q2 = _stack_heads(q_ref[...] * _SB_SCALE)
        do2 = _stack_heads(do_ref[...])
        q2t, do2t = q2.T.astype(_MXU), do2.T.astype(_MXU)
        q2, do2 = q2.astype(_MXU), do2.astype(_MXU)
        stv = s_ref[...]
        lane = _lane_ids(1, 128)
        s2 = jnp.concatenate([jnp.sum(jnp.where(lane == 0, stv, 0.0), axis=1, keepdims=True),
                              jnp.sum(jnp.where(lane == SB_HD, stv, 0.0), axis=1, keepdims=True)], axis=0)
        rr = _row_ids(CH, CH)
        cc = _lane_ids(CH, CH)
        mle = (rr <= cc).astype(F32)
        mlt = (rr < cc).astype(F32)

        def step(i, carry):
            dq2, pacc, gacc = carry
            blocks = []
            for t in range(_SB_NB):
                kb = _SB_NB * i + t
                live = (kb <= qb).astype(jnp.int32)
                start = pl.multiple_of(jnp.minimum(kb, qb) * CH, CH)
                valid = _sb_valid(qb, jnp.minimum(kb, qb), live)
                z = _dot_nt(q2, k_ref[pl.ds(start, CH), :])
                sp = _sb_softplus(z)
                lm = jnp.where(valid, -sp, 0.0)
                blocks.append((valid, z - sp, _dot_split(lm, mle), jnp.sum(lm, axis=1, keepdims=True), start))
            stage = []
            for valid, ls, ploc, rs, start in blocks:
                w = jnp.where(valid, jnp.exp(ls + (s2 - (ploc + pacc))), 0.0)
                gg = _dot_nt(do2, v_ref[pl.ds(start, CH), :]) * w
                stage.append((valid, ls, w, gg, _dot_split(gg, mlt), jnp.sum(gg, axis=1, keepdims=True), start))
                pacc = pacc + rs
            for valid, ls, w, gg, gloc, gs, start in stage:
                sig = jnp.exp(ls)
                dz = jnp.where(valid, gg * (1.0 - sig) - (gloc + gacc) * sig, 0.0)
                dq2 = dq2 + _dot(dz, k_ref[pl.ds(start, CH), :])
                dk_ref[:, pl.ds(start, CH)] += _dot(q2t, dz)
                dv_ref[:, pl.ds(start, CH)] += _dot(do2t, w)
                gacc = gacc + gs
            return dq2, pacc, gacc

        zero = jnp.zeros((2 * CH, 1), F32)
        trips = (qb + _SB_NB) // _SB_NB
        dq2 = lax.fori_loop(0, trips, step, (jnp.zeros((2 * CH, 128), F32), zero, zero))[0]
        dq_ref[...] = (_unstack_heads(dq2) * _SB_SCALE).astype(dq_ref.dtype)

    qspec = lambda off: pl.BlockSpec((CH, 128), lambda b, hp, qb: (b * nch + qb, off + hp))
    kspec = lambda off: pl.BlockSpec((Pn, 128), lambda b, hp, qb: (b, off + hp))
    tspec = pl.BlockSpec((128, Pn), lambda b, hp, qb: (hp, b))
    full = jax.ShapeDtypeStruct((1024, R), F32)
    kw = dict(grid=(B, SB_HEADS // 2, nch), in_specs=[qspec(0), kspec(8), kspec(16), qspec(0), qspec(0)],
              out_specs=[qspec(0), tspec, tspec], out_shape=[jax.ShapeDtypeStruct((R, 1024), _MXU), full, full])
    return _call(body, "sb_bwd", ("arbitrary", "arbitrary", "arbitrary"), kw, (u1, u1, u1, dycat, stot), rider)


def _neg_expm1(x):
    series = -(x * (1.0 + x * (0.5 + x * (1.0 / 6.0 + x * (1.0 / 24.0)))))
    return jnp.where(x > -0.05, series, 1.0 - jnp.exp(x))


def _lru_gates(x, wa_ref, ba_ref, wx_ref, bx_ref, lam_ref):
    rs, is_ = [], []
    for n in range(LRU_BLOCKS):
        xb = x[:, 128 * n:128 * n + 128]
        rs.append(_dot(xb, wa_ref[n]))
        is_.append(_dot(xb, wx_ref[n]))
    r = _sigmoid(jnp.concatenate(rs, axis=1) + ba_ref[...])
    i = _sigmoid(jnp.concatenate(is_, axis=1) + bx_ref[...])
    sp = _softplus(-lam_ref[...])
    la = -LRU_C * r * sp
    a = jnp.exp(la)
    mult = jnp.sqrt(jnp.maximum(_neg_expm1(2.0 * la), 0.0))
    return r, i, sp, a, mult


def _lru_fwd(u1, ycat, conv_w, conv_b, wa, ba, wx, bx, lam, B, nch):
    R = u1.shape[0]

    def body(x_ref, xp_ref, gate_ref, cw_ref, cb_ref, wa_ref, ba_ref, wx_ref, bx_ref, lam_ref, ycat_in,
             out_ref, hs_ref, hc):
        c = pl.program_id(1)

        @pl.when(c == 0)
        def _():
            hc[...] = jnp.zeros_like(hc)

        x = _conv_pre(xp_ref[...], x_ref[...], cw_ref, cb_ref, 4)
        r, i, sp, a, mult = _lru_gates(x, wa_ref, ba_ref, wx_ref, bx_ref, lam_ref)
        b = jnp.where(_real_rows(c), mult * (i * x), 0.0)
        rows = _row_ids(CH)
        s = 1
        while s < CH:
            a_s = jnp.where(rows >= s, pltpu.roll(a, s, axis=0), 1.0)
            b_s = jnp.where(rows >= s, pltpu.roll(b, s, axis=0), 0.0)
            b = a * b_s + b
            a = a * a_s
            s *= 2
        h = a * hc[0:1, :] + b
        hs_ref[...] = h
        hc[0:1, :] = hs_ref[CH - 1:CH, :]
        out_ref[...] = (h * _gelu(gate_ref[...])).astype(out_ref.dtype)

    row = lambda col: pl.BlockSpec((CH, 1024), lambda b, c: (b * nch + c, col))
    vec = pl.BlockSpec((1, 1024), lambda b, c: (0, 0))
    wsp = pl.BlockSpec((LRU_BLOCKS, 128, 128), lambda b, c: (0, 0, 0))
    return pl.pallas_call(
        body, name="lru_fwd", grid=(B, nch),
        in_specs=[row(4), pl.BlockSpec((8, 1024), _prev8_map(nch, 4)), row(3),
                  pl.BlockSpec((4, 1024), lambda b, c: (0, 0)), vec, wsp, vec, wsp, vec, vec,
                  pl.BlockSpec(memory_space=pl.ANY)],
        out_specs=[row(1), row(0)],
        out_shape=[jax.ShapeDtypeStruct(ycat.shape, ycat.dtype), jax.ShapeDtypeStruct((R, 1024), F32)],
        scratch_shapes=[pltpu.VMEM((8, 1024), F32)],
        input_output_aliases={10: 0},
        compiler_params=_cparams(("parallel", "arbitrary")),
    )(u1, u1, u1, conv_w, conv_b, wa, ba, wx, bx, lam, ycat)


def _lru_bwd(dycat, u1, hs, conv_w, conv_b, wa, ba, wx, bx, lam, B, nch):
    R = u1.shape[0]

    def body(dy_ref, x_ref, xp_ref, gate_ref, hs_ref, hsp_ref, cw_ref, cb_ref, wa_ref, ba_ref, wx_ref, bx_ref, lam_ref,
             dgate_ref, dxc_ref, pg_ref, dwa_ref, dwx_ref, lc):
        c = nch - 1 - pl.program_id(1)

        @pl.when(pl.program_id(1) == 0)
        def _():
            lc[...] = jnp.zeros_like(lc)
            pg_ref[...] = jnp.zeros_like(pg_ref)
            dwa_ref[...] = jnp.zeros_like(dwa_ref)
            dwx_ref[...] = jnp.zeros_like(dwx_ref)

        x = _conv_pre(xp_ref[...], x_ref[...], cw_ref, cb_ref, 4)
        r, i, sp, a, mult = _lru_gates(x, wa_ref, ba_ref, wx_ref, bx_ref, lam_ref)
        h = hs_ref[...]
        hprev = _shift_down(hsp_ref[...], h, 1)
        gate = gate_ref[...]
        dy = dy_ref[...]
        dgate_ref[...] = (dy * h * _dgelu(gate)).astype(dgate_ref.dtype)
        rows = _row_ids(CH)
        lam_t = dy * _gelu(gate) + jnp.where(rows == CH - 1, lc[0:1, :], 0.0)
        coef = jnp.where(rows < CH - 1, pltpu.roll(a, CH - 1, axis=0), 0.0)
        s = 1
        while s < CH:
            c_s = jnp.where(rows < CH - s, pltpu.roll(coef, CH - s, axis=0), 1.0)
            l_s = jnp.where(rows < CH - s, pltpu.roll(lam_t, CH - s, axis=0), 0.0)
            lam_t = coef * l_s + lam_t
            coef = coef * c_s
            s *= 2
        lc[0:1, :] = jnp.sum(jnp.where(rows == 0, a * lam_t, 0.0), axis=0, keepdims=True)
        db = jnp.where(_real_rows(c), lam_t, 0.0)
        da = db * hprev
        dmult = db * (i * x)
        di = db * mult * x
        dx = db * mult * i
        pos = mult > 0.0
        dla = da * a + jnp.where(pos, -dmult * (a * a) / jnp.where(pos, mult, 1.0), 0.0)
        dr = dla * (-LRU_C * sp)
        pg_ref[2:3, :] += jnp.sum(dla * (LRU_C * r) * _sigmoid(-lam_ref[...]), axis=0, keepdims=True)
        dpr = dr * r * (1.0 - r)
        dpi = di * i * (1.0 - i)
        pg_ref[0:1, :] += jnp.sum(dpr, axis=0, keepdims=True)
        pg_ref[1:2, :] += jnp.sum(dpi, axis=0, keepdims=True)
        dxs = []
        for n in range(LRU_BLOCKS):
            blk = slice(128 * n, 128 * n + 128)
            dxs.append(dx[:, blk] + _dot_nt(dpr[:, blk], wa_ref[n]) + _dot_nt(dpi[:, blk], wx_ref[n]))
            dwa_ref[n] += _dot_tn(x[:, blk], dpr[:, blk])
            dwx_ref[n] += _dot_tn(x[:, blk], dpi[:, blk])
        dxc_ref[...] = jnp.concatenate(dxs, axis=1)

    rmap = lambda col: (lambda b, c: (b * nch + nch - 1 - c, col))
    row = lambda col: pl.BlockSpec((CH, 1024), rmap(col))
    prev = lambda col: pl.BlockSpec(
        (8, 1024), lambda b, c: (jnp.maximum((b * nch + nch - 1 - c) * (CH // 8) - 1, 0), col))
    vec = pl.BlockSpec((1, 1024), lambda b, c: (0, 0))
    wsp = pl.BlockSpec((LRU_BLOCKS, 128, 128), lambda b, c: (0, 0, 0))
    full = jax.ShapeDtypeStruct((R, 1024), F32)
    return pl.pallas_call(
        body, name="lru_bwd", grid=(B, nch),
        in_specs=[row(1), row(4), prev(4), row(3), row(0), prev(0),
                  pl.BlockSpec((4, 1024), lambda b, c: (0, 0)), vec, wsp, vec, wsp, vec, vec],
        out_specs=[row(0), row(0), pl.BlockSpec((None, 8, 1024), lambda b, c: (b, 0, 0)),
                   pl.BlockSpec((None, LRU_BLOCKS, 128, 128), lambda b, c: (b, 0, 0, 0)),
                   pl.BlockSpec((None, LRU_BLOCKS, 128, 128), lambda b, c: (b, 0, 0, 0))],
        out_shape=[jax.ShapeDtypeStruct((R, 1024), _MXU), full, jax.ShapeDtypeStruct((B, 8, 1024), F32),
                   jax.ShapeDtypeStruct((B, LRU_BLOCKS, 128, 128), F32),
                   jax.ShapeDtypeStruct((B, LRU_BLOCKS, 128, 128), F32)],
        scratch_shapes=[pltpu.VMEM((8, 1024), F32)],
        compiler_params=_cparams(("parallel", "arbitrary")),
    )(dycat, u1, u1, u1, hs, hs, conv_w, conv_b, wa, ba, wx, bx, lam)


_FFN_TC = FFN // 2


def _ffn_specs(nch):
    nt = FFN // _FFN_TC
    row = lambda off: pl.BlockSpec((CH, _FFN_TC), lambda b, c, j: (b * nch + c, off + j))
    prev = lambda off: pl.BlockSpec(
        (8, _FFN_TC), lambda b, c, j: (jnp.maximum((b * nch + c) * (CH // 8) - 1, 0), off + j))
    wsp = lambda off: pl.BlockSpec((3, _FFN_TC), lambda b, c, j: (0, off + j))
    bsp = lambda off: pl.BlockSpec((1, _FFN_TC), lambda b, c, j: (0, off + j))
    return nt, row, [row(0), prev(0), row(nt), prev(nt), wsp(0), wsp(nt), bsp(0), bsp(nt)]


def _ffn_act_fwd(uf, conv_w, conv_b, B, nch, rider=None):
    R = uf.shape[0]
    nt, row, specs = _ffn_specs(nch)

    def body(g_ref, gp_ref, u_ref, up_ref, wg_ref, wu_ref, bg_ref, bu_ref, o_ref):
        cg = _conv_pre(gp_ref[...], g_ref[...], wg_ref, bg_ref, 3)
        cu = _conv_pre(up_ref[...], u_ref[...], wu_ref, bu_ref, 3)
        o_ref[...] = jnp.where(_real_rows(pl.program_id(1)), _silu(cg) * cu, 0.0).astype(o_ref.dtype)

    kw = dict(grid=(B, nch, nt), in_specs=specs, out_specs=[row(0)],
              out_shape=[jax.ShapeDtypeStruct((R, FFN), _MXU)])
    return _call(body, "ffn_act_fwd", ("arbitrary", "arbitrary", "arbitrary"), kw,
                 (uf, uf, uf, uf, conv_w, conv_w, conv_b, conv_b), rider)


def _ffn_act_bwd(da, uf, conv_w, conv_b, nch, name, rider=None):
    R = uf.shape[0]
    nt = FFN // _FFN_TC
    nr = R // CH
    K = 3

    def body(da_ref, dan_ref, g_ref, gp_ref, gn_ref, u_ref, up_ref, un_ref, wg_ref, wu_ref, bg_ref, bu_ref,
             dug_ref, duu_ref, dwg_ref, dwu_ref):
        i = pl.program_id(1)

        @pl.when(i == 0)
        def _():
            dwg_ref[...] = jnp.zeros_like(dwg_ref)
            dwu_ref[...] = jnp.zeros_like(dwu_ref)

        c = i % nch
        ext = CH + 8
        rows = _row_ids(ext)
        follows = (c < nch - 1).astype(jnp.int32)
        keep = jnp.logical_and(c * CH + rows >= PAD, rows < CH + 8 * follows)
        dav = jnp.where(keep, jnp.concatenate([da_ref[...], dan_ref[...]], axis=0), 0.0)

        def conv_ext(x_ref, xp_ref, xn_ref, w_ref, b_ref):
            cat = jnp.concatenate([xp_ref[...], x_ref[...], xn_ref[...]], axis=0)
            shifted = [cat[8:]] + [pltpu.roll(cat, s, axis=0)[8:] for s in range(1, K)]
            acc = shifted[0] * w_ref[K - 1:K, :] + b_ref[...]
            for s in range(1, K):
                acc = acc + shifted[s] * w_ref[K - 1 - s:K - s, :]
            return acc, shifted

        cg, gsh = conv_ext(g_ref, gp_ref, gn_ref, wg_ref, bg_ref)
        cu, ush = conv_ext(u_ref, up_ref, un_ref, wu_ref, bu_ref)
        sg = _sigmoid(cg)
        dcg = dav * cu * (sg * (1.0 + cg * (1.0 - sg)))
        dcu = dav * (cg * sg)
        for dc, xsh, w_ref, din_ref, dw_ref in ((dcg, gsh, wg_ref, dug_ref, dwg_ref), (dcu, ush, wu_ref, duu_ref, dwu_ref)):
            dp = dc[:CH]
            din = dp * w_ref[K - 1:K, :]
            dw_ref[7:8, :] += jnp.sum(dp, axis=0, keepdims=True)
            dw_ref[K - 1:K, :] += jnp.sum(dp * xsh[0][:CH], axis=0, keepdims=True)
            for s in range(1, K):
                din = din + pltpu.roll(dc, ext - s, axis=0)[:CH] * w_ref[K - 1 - s:K - s, :]
                dw_ref[K - 1 - s:K - s, :] += jnp.sum(dp * xsh[s][:CH], axis=0, keepdims=True)
            din_ref[...] = din.astype(din_ref.dtype)

    row = lambda off: pl.BlockSpec((CH, _FFN_TC), lambda j, i: (i, off + j))
    prev = lambda off: pl.BlockSpec((8, _FFN_TC), lambda j, i: (jnp.maximum(i * (CH // 8) - 1, 0), off + j))
    nxt = lambda off: pl.BlockSpec(
        (8, _FFN_TC), lambda j, i: (jnp.minimum((i + 1) * (CH // 8), nr * (CH // 8) - 1), off + j))
    wsp = lambda off: pl.BlockSpec((K, _FFN_TC), lambda j, i: (0, off + j))
    bsp = lambda off: pl.BlockSpec((1, _FFN_TC), lambda j, i: (0, off + j))
    acc = pl.BlockSpec((8, _FFN_TC), lambda j, i: (0, j))
    half = jax.ShapeDtypeStruct((R, FFN), _MXU)
    dwsh = jax.ShapeDtypeStruct((8, FFN), F32)
    kw = dict(
        grid=(nt, nr),
        in_specs=[row(0), nxt(0), row(0), prev(0), nxt(0), row(nt), prev(nt), nxt(nt), wsp(0), wsp(nt), bsp(0), bsp(nt)],
        out_specs=[row(0), row(0), acc, acc],
        out_shape=[half, half, dwsh, dwsh])
    return _call(body, name, ("arbitrary", "arbitrary"), kw,
                 (da, da, uf, uf, uf, uf, uf, uf, conv_w, conv_w, conv_b, conv_b), rider)


def _head(h, g, target, B, nch):
    R = h.shape[0]

    def body(h_ref, g_ref, t_ref, dh_ref, loss_ref, dg_ref):
        c = pl.program_id(1)

        @pl.when(c == 0)
        def _():
            dh_ref[...] = jnp.zeros_like(dh_ref)
            loss_ref[...] = jnp.zeros_like(loss_ref)
            dg_ref[...] = jnp.zeros_like(dg_ref)

        @pl.when(c > 0)
        def _():
            x = h_ref[...]
            gv = g_ref[...]
            r = lax.rsqrt(jnp.mean(x * x, axis=-1, keepdims=True) + EPS)
            xhat = x * r
            e = xhat * gv - t_ref[...]
            loss_ref[...] += 0.5 * jnp.sum(jnp.mean(e * e, axis=-1, keepdims=True), axis=0, keepdims=True)
            dy = e * (1.0 / D)
            dg_ref[0:1, :] += jnp.sum(dy * xhat, axis=0, keepdims=True)
            dx = dy * gv
            dh_ref[...] = r * (dx - xhat * jnp.mean(dx * xhat, axis=-1, keepdims=True))

    row = pl.BlockSpec((CH, D), lambda b, c: (b * nch + c, 0))
    return pl.pallas_call(
        body, name="head", grid=(B, nch),
        in_specs=[row, pl.BlockSpec((1, D), lambda b, c: (0, 0)),
                  pl.BlockSpec((CH, D), lambda b, c: (b * (nch - 1) + jnp.maximum(c - 1, 0), 0))],
        out_specs=[row, pl.BlockSpec((None, 8, 128), lambda b, c: (b, 0, 0)),
                   pl.BlockSpec((None, 8, D), lambda b, c: (b, 0, 0))],
        out_shape=[jax.ShapeDtypeStruct((R, D), F32), jax.ShapeDtypeStruct((B, 8, 128), F32),
                   jax.ShapeDtypeStruct((B, 8, D), F32)],
        compiler_params=_cparams(("parallel", "arbitrary")),
    )(h, g, target)


ADAM_LR = 0.001
ADAM_B1 = 0.9
ADAM_B2 = 0.999
ADAM_EPS = 1e-08
ADAM_WD = 0.01
ADAM_STEP = 10


def _adamw(w, g, m, v, name):
    Rr, C = w.shape
    tr = _tile(Rr, (256, 64))

    def body(w_ref, g_ref, m_ref, v_ref, d_ref, nm_ref, nv_ref):
        gv = g_ref[...]
        nm = ADAM_B1 * m_ref[...] + (1.0 - ADAM_B1) * gv
        nv = ADAM_B2 * v_ref[...] + (1.0 - ADAM_B2) * (gv * gv)
        m_hat = nm / (1.0 - ADAM_B1 ** ADAM_STEP)
        v_hat = nv / (1.0 - ADAM_B2 ** ADAM_STEP)
        d_ref[...] = -ADAM_LR * (m_hat / (jnp.sqrt(v_hat) + ADAM_EPS) + ADAM_WD * w_ref[...])
        nm_ref[...] = nm
        nv_ref[...] = nv

    spec = pl.BlockSpec((tr, C), lambda i: (i, 0))
    sh = jax.ShapeDtypeStruct((Rr, C), F32)
    return pl.pallas_call(
        body, name=name, grid=(Rr // tr,),
        in_specs=[spec] * 4, out_specs=[spec] * 3, out_shape=[sh] * 3,
        compiler_params=_cparams(("parallel",)),
    )(w, g, m, v)


_MESH = pl.DeviceIdType.MESH
_ANY = pl.BlockSpec(memory_space=pl.ANY)


def _place():
    x, y, c = lax.axis_index("x"), lax.axis_index("y"), lax.axis_index("c")
    chips = [(1 - x, y), (x, 1 - y), (1 - x, 1 - y)]
    return x, y, c, chips


def _rcopy(src, dst, ssem, rsem, dev):
    return pltpu.make_async_remote_copy(src_ref=src, dst_ref=dst, send_sem=ssem, recv_sem=rsem,
                                        device_id=dev, device_id_type=_MESH)


def _with_riders(body, kw, kind, riders):
    n_in, n_out, n_scr = len(kw["in_specs"]), len(kw["out_specs"]), len(kw.get("scratch_shapes", []))
    grid = kw["grid"]
    nr = len(riders)
    nsem = 4 if kind == "gather" else 2

    def new_body(*refs):
        ins, srcs = refs[:n_in], refs[n_in:n_in + nr]
        outs, dsts = refs[n_in + nr:n_in + nr + n_out], refs[n_in + nr + n_out:n_in + 2 * nr + n_out]
        scr = refs[n_in + 2 * nr + n_out:n_in + 2 * nr + n_out + n_scr]
        sems = refs[n_in + 2 * nr + n_out + n_scr:]
        first = last = None
        for axis, size in enumerate(grid):
            i = pl.program_id(axis)
            first = (i == 0) if first is None else jnp.logical_and(first, i == 0)
            last = (i == size - 1) if last is None else jnp.logical_and(last, i == size - 1)
        x, y, c, chips = _place()
        k = 2 * x + y
        sib = (x, y, 1 - c)
        ssem, rsem = sems[:2]
        sends = []
        for a in range(nr):
            for j, (cx, cy) in enumerate(chips):
                if kind == "gather":
                    src, dst = srcs[a].at[c], dsts[a].at[k, c]
                else:
                    src, dst = srcs[a].at[2 * cx + cy], dsts[a].at[k]
                sends.append(_rcopy(src, dst, ssem.at[3 * a + j], rsem.at[3 * a + j], (cx, cy, c)))

        @pl.when(first)
        def _():
            for cp in sends:
                cp.start()

        body(*ins, *outs, *scr)

        @pl.when(last)
        def _():
            passed = []
            for a in range(nr):
                for j, (cx, cy) in enumerate(chips):
                    got = dsts[a].at[2 * cx + cy, c] if kind == "gather" else dsts[a].at[2 * cx + cy]
                    _rcopy(got, got, ssem.at[3 * a + j], rsem.at[3 * a + j], (cx, cy, c)).wait_recv()
                    if kind == "gather":
                        fw = _rcopy(got, got, sems[2].at[3 * a + j], sems[3].at[3 * a + j], sib)
                        fw.start()
                        passed.append(fw)
            if kind == "gather":
                for a in range(nr):
                    for j, (cx, cy) in enumerate(chips):
                        got = dsts[a].at[2 * cx + cy, 1 - c]
                        _rcopy(got, got, sems[2].at[3 * a + j], sems[3].at[3 * a + j], sib).wait_recv()
            for cp in sends + passed:
                cp.wait_send()

    kw = dict(kw)
    kw["in_specs"] = list(kw["in_specs"]) + [_ANY] * nr
    kw["out_specs"] = list(kw["out_specs"]) + [_ANY] * nr
    kw["out_shape"] = list(kw["out_shape"]) + [
        jax.ShapeDtypeStruct(((4,) + r.shape) if kind == "gather" else r.shape, r.dtype) for r in riders]
    kw["scratch_shapes"] = list(kw.get("scratch_shapes", [])) + [pltpu.SemaphoreType.DMA((3 * nr,))] * nsem
    return new_body, kw


def _call(body, name, sem, kw, args, rider=None):
    if rider is not None:
        body, kw = _with_riders(body, kw, *rider)
        args = tuple(args) + tuple(rider[1])
    return pl.pallas_call(body, name=name, compiler_params=_cparams(sem), **kw)(*args)


def _fill_own(result, own, chip):
    return lax.dynamic_update_index_in_dim(result, own, chip, 0)


def _gather_shards(bigs, small):
    nb = len(bigs)

    def body(*refs):
        ins, outs = refs[:nb + 1], refs[nb + 1:2 * nb + 2]
        ssem, rsem, fssem, frsem = refs[2 * nb + 2:]
        x, y, c, chips = _place()
        k = 2 * x + y
        sib = (x, y, 1 - c)

        def part(a, slot, hc):
            return outs[a].at[slot] if a == nb else outs[a].at[slot, hc]

        first = []
        for a in range(nb + 1):
            src = ins[a] if a == nb else ins[a].at[c]
            for j, (cx, cy) in enumerate(chips):
                first.append(_rcopy(src, part(a, k, c), ssem.at[3 * a + j], rsem.at[3 * a + j], (cx, cy, c)))
        for cp in first:
            cp.start()
        passed = []
        for a in range(nb + 1):
            for j, (cx, cy) in enumerate(chips):
                got = part(a, 2 * cx + cy, c)
                _rcopy(got, got, ssem.at[3 * a + j], rsem.at[3 * a + j], (cx, cy, c)).wait_recv()
                if a < nb:
                    fw = _rcopy(got, got, fssem.at[3 * a + j], frsem.at[3 * a + j], sib)
                    fw.start()
                    passed.append(fw)
        for a in range(nb):
            for j, (cx, cy) in enumerate(chips):
                got = part(a, 2 * cx + cy, 1 - c)
                _rcopy(got, got, fssem.at[3 * a + j], frsem.at[3 * a + j], sib).wait_recv()
        for cp in first + passed:
            cp.wait_send()

    arrs = list(bigs) + [small]
    n = 3 * (nb + 1)
    return pl.pallas_call(
        body, name="gather_shards",
        in_specs=[_ANY] * (nb + 1), out_specs=[_ANY] * (nb + 1),
        out_shape=[jax.ShapeDtypeStruct((4,) + a.shape, a.dtype) for a in arrs],
        scratch_shapes=[pltpu.SemaphoreType.DMA((n,)), pltpu.SemaphoreType.DMA((n,)),
                        pltpu.SemaphoreType.DMA((n,)), pltpu.SemaphoreType.DMA((n,))],
    )(*arrs)


def _swap_halves(grads, name):
    na = len(grads)
    halves = [g.shape[1] // 2 for g in grads]

    def body(*refs):
        ins, outs = refs[:na], refs[na:2 * na]
        ssem, rsem = refs[2 * na:]
        x, y, c, _ = _place()
        sib = (x, y, 1 - c)
        cps = [_rcopy(ins[a].at[:, pl.ds((1 - c) * halves[a], halves[a]), :], outs[a], ssem.at[a], rsem.at[a], sib)
               for a in range(na)]
        for cp in cps:
            cp.start()
        for cp in cps:
            cp.wait()

    return pl.pallas_call(
        body, name=name,
        in_specs=[_ANY] * na, out_specs=[_ANY] * na,
        out_shape=[jax.ShapeDtypeStruct((4, g.shape[1] // 2, g.shape[2]), g.dtype) for g in grads],
        scratch_shapes=[pltpu.SemaphoreType.DMA((na,)), pltpu.SemaphoreType.DMA((na,))],
    )(*grads)


def _sum_rows(rh):
    return rh if rh <= 512 else _tile(rh, (512, 256, 128, 64, 32))


def _chip_sum(grad, recv, core, name):
    _, r, cdim = grad.shape
    rh = r // 2
    tr = _sum_rows(rh)
    nblk = rh // tr

    def body(core_ref, g_ref, r_ref, o_ref):
        o_ref[...] = (g_ref[...] + r_ref[...]).astype(o_ref.dtype)

    return pl.pallas_call(
        body, name=name,
        grid_spec=pltpu.PrefetchScalarGridSpec(
            num_scalar_prefetch=1, grid=(4, nblk),
            in_specs=[pl.BlockSpec((None, tr, cdim), lambda s, i, cr: (s, cr[0] * nblk + i, 0)),
                      pl.BlockSpec((None, tr, cdim), lambda s, i, cr: (s, i, 0))],
            out_specs=pl.BlockSpec((None, tr, cdim), lambda s, i, cr: (s, i, 0))),
        out_shape=jax.ShapeDtypeStruct((4, rh, cdim), BF16),
        compiler_params=_cparams(("parallel", "parallel")),
    )(core, grad, recv)


def _scatter_sums(sums):
    na = len(sums)

    def body(*refs):
        ins, outs = refs[:na], refs[na:2 * na]
        ssem, rsem, lsem = refs[2 * na:]
        x, y, c, chips = _place()
        k = 2 * x + y
        local = [pltpu.make_async_copy(ins[a].at[k], outs[a].at[k], lsem.at[a]) for a in range(na)]
        for cp in local:
            cp.start()
        cps = []
        for a in range(na):
            for j, (cx, cy) in enumerate(chips):
                cps.append(_rcopy(ins[a].at[2 * cx + cy], outs[a].at[k], ssem.at[3 * a + j], rsem.at[3 * a + j],
                                  (cx, cy, c)))
        for cp in cps:
            cp.start()
        for a in range(na):
            for j, (cx, cy) in enumerate(chips):
                got = outs[a].at[2 * cx + cy]
                _rcopy(got, got, ssem.at[3 * a + j], rsem.at[3 * a + j], (cx, cy, c)).wait_recv()
        for cp in cps:
            cp.wait_send()
        for cp in local:
            cp.wait()

    return pl.pallas_call(
        body, name="scatter_sums",
        in_specs=[_ANY] * na, out_specs=[_ANY] * na,
        out_shape=[jax.ShapeDtypeStruct(s.shape, s.dtype) for s in sums],
        scratch_shapes=[pltpu.SemaphoreType.DMA((3 * na,)), pltpu.SemaphoreType.DMA((3 * na,)),
                        pltpu.SemaphoreType.DMA((na,))],
    )(*sums)


def _sum_chips(parts, name):
    _, rh, cdim = parts.shape
    tr = _sum_rows(rh)

    def body(p_ref, o_ref):
        acc = p_ref[0].astype(F32)
        for j in range(1, 4):
            acc = acc + p_ref[j].astype(F32)
        o_ref[...] = acc

    return pl.pallas_call(
        body, name=name, grid=(rh // tr,),
        in_specs=[pl.BlockSpec((4, tr, cdim), lambda i: (0, i, 0))],
        out_specs=pl.BlockSpec((tr, cdim), lambda i: (i, 0)),
        out_shape=jax.ShapeDtypeStruct((rh, cdim), F32),
        compiler_params=_cparams(("parallel",)),
    )(parts)


def _join_halves(reds):
    na = len(reds)

    def body(*refs):
        ins, outs = refs[:na], refs[na:2 * na]
        ssem, rsem = refs[2 * na:]
        x, y, c, _ = _place()
        cps = [_rcopy(ins[a], outs[a], ssem.at[a], rsem.at[a], (x, y, 1 - c)) for a in range(na)]
        for cp in cps:
            cp.start()
        for cp in cps:
            cp.wait()

    return pl.pallas_call(
        body, name="join_halves",
        in_specs=[_ANY] * na, out_specs=[_ANY] * na,
        out_shape=[jax.ShapeDtypeStruct(r.shape, r.dtype) for r in reds],
        scratch_shapes=[pltpu.SemaphoreType.DMA((na,)), pltpu.SemaphoreType.DMA((na,))],
    )(*reds)


def _allreduce_small(buf):
    n = buf.shape[0]

    def body(in_ref, out_ref, recv, ssem, rsem):
        x, y, c, _ = _place()
        peers = [(x, y, 1 - c), (1 - x, y, c), (x, 1 - y, c)]
        out_ref[...] = in_ref[...]
        for r, peer in enumerate(peers):
            cp = _rcopy(out_ref, recv.at[r], ssem.at[r], rsem.at[r], peer)
            cp.start()
            cp.wait()
            out_ref[...] = out_ref[...] + recv[r]

    vm = pl.BlockSpec(memory_space=pltpu.VMEM)
    return pl.pallas_call(
        body, name="allreduce_small",
        in_specs=[vm], out_specs=vm,
        out_shape=jax.ShapeDtypeStruct(buf.shape, F32),
        scratch_shapes=[pltpu.VMEM((3, n, 128), F32), pltpu.SemaphoreType.DMA((3,)), pltpu.SemaphoreType.DMA((3,))],
        compiler_params=pltpu.CompilerParams(vmem_limit_bytes=VMEM_LIMIT),
    )(buf)


_W_NAMES = ['meta_tokens', 'l0_mix_norm', 'l0_w_in', 'l0_ssd_conv_w', 'l0_ssd_conv_b', 'l0_ssd_dt_bias', 'l0_ssd_a_log',
            'l0_ssd_d', 'l0_ssd_norm', 'l0_ret_norm', 'l0_w_out', 'l0_ffn_norm', 'l0_ffn_w_in', 'l0_ffn_conv_w',
            'l0_ffn_conv_b', 'l0_ffn_w_out', 'l1_mix_norm', 'l1_w_in', 'l1_lru_conv_w', 'l1_lru_conv_b', 'l1_lru_wa',
            'l1_lru_ba', 'l1_lru_wx', 'l1_lru_bx', 'l1_lru_lambda', 'l1_w_out', 'l1_ffn_norm', 'l1_ffn_w_in',
            'l1_ffn_conv_w', 'l1_ffn_conv_b', 'l1_ffn_w_out', 'final_norm']
_IN_NAMES = ['x'] + _W_NAMES + ['loss_target'] + ['m_' + n for n in _W_NAMES] + ['v_' + n for n in _W_NAMES]
_BIG = ['l0_w_in', 'l0_w_out', 'l0_ffn_w_in', 'l0_ffn_w_out', 'l1_w_in', 'l1_w_out', 'l1_ffn_w_in', 'l1_ffn_w_out']
_BIG_COLS = ('l0_w_in', 'l0_ffn_w_in', 'l1_w_in', 'l1_ffn_w_in')
_SMALL_SHARDED = ['meta_tokens', 'l0_ssd_conv_w', 'l0_ffn_conv_w', 'l1_lru_conv_w', 'l1_ffn_conv_w']
_SMALL = [n for n in _W_NAMES if n not in _BIG]


def _pack(arrs):
    flat = []
    for a in arrs:
        v = a.reshape(-1).astype(F32)
        flat.append(jnp.pad(v, (0, (-v.shape[0]) % 128)))
    v = jnp.concatenate(flat)
    v = jnp.pad(v, (0, (-v.shape[0]) % 1024))
    return v.reshape(-1, 128)


def _unpack(buf, shapes):
    out, row = [], 0
    for sh in shapes:
        n = int(np.prod(sh))
        rows = -(-n // 128)
        out.append(buf[row:row + rows].reshape(-1)[:n].reshape(sh))
        row += rows
    return out


def kernel(x, meta_tokens, l0_mix_norm, l0_w_in, l0_ssd_conv_w, l0_ssd_conv_b, l0_ssd_dt_bias, l0_ssd_a_log, l0_ssd_d, l0_ssd_norm, l0_ret_norm, l0_w_out, l0_ffn_norm, l0_ffn_w_in, l0_ffn_conv_w, l0_ffn_conv_b, l0_ffn_w_out, l1_mix_norm, l1_w_in, l1_lru_conv_w, l1_lru_conv_b, l1_lru_wa, l1_lru_ba, l1_lru_wx, l1_lru_bx, l1_lru_lambda, l1_w_out, l1_ffn_norm, l1_ffn_w_in, l1_ffn_conv_w, l1_ffn_conv_b, l1_ffn_w_out, final_norm, loss_target, m_meta_tokens, m_l0_mix_norm, m_l0_w_in, m_l0_ssd_conv_w, m_l0_ssd_conv_b, m_l0_ssd_dt_bias, m_l0_ssd_a_log, m_l0_ssd_d, m_l0_ssd_norm, m_l0_ret_norm, m_l0_w_out, m_l0_ffn_norm, m_l0_ffn_w_in, m_l0_ffn_conv_w, m_l0_ffn_conv_b, m_l0_ffn_w_out, m_l1_mix_norm, m_l1_w_in, m_l1_lru_conv_w, m_l1_lru_conv_b, m_l1_lru_wa, m_l1_lru_ba, m_l1_lru_wx, m_l1_lru_bx, m_l1_lru_lambda, m_l1_w_out, m_l1_ffn_norm, m_l1_ffn_w_in, m_l1_ffn_conv_w, m_l1_ffn_conv_b, m_l1_ffn_w_out, m_final_norm, v_meta_tokens, v_l0_mix_norm, v_l0_w_in, v_l0_ssd_conv_w, v_l0_ssd_conv_b, v_l0_ssd_dt_bias, v_l0_ssd_a_log, v_l0_ssd_d, v_l0_ssd_norm, v_l0_ret_norm, v_l0_w_out, v_l0_ffn_norm, v_l0_ffn_w_in, v_l0_ffn_conv_w, v_l0_ffn_conv_b, v_l0_ffn_w_out, v_l1_mix_norm, v_l1_w_in, v_l1_lru_conv_w, v_l1_lru_conv_b, v_l1_lru_wa, v_l1_lru_ba, v_l1_lru_wx, v_l1_lru_bx, v_l1_lru_lambda, v_l1_w_out, v_l1_ffn_norm, v_l1_ffn_w_in, v_l1_ffn_conv_w, v_l1_ffn_conv_b, v_l1_ffn_w_out, v_final_norm):
    args = (x, meta_tokens, l0_mix_norm, l0_w_in, l0_ssd_conv_w, l0_ssd_conv_b, l0_ssd_dt_bias, l0_ssd_a_log, l0_ssd_d, l0_ssd_norm, l0_ret_norm, l0_w_out, l0_ffn_norm, l0_ffn_w_in, l0_ffn_conv_w, l0_ffn_conv_b, l0_ffn_w_out, l1_mix_norm, l1_w_in, l1_lru_conv_w, l1_lru_conv_b, l1_lru_wa, l1_lru_ba, l1_lru_wx, l1_lru_bx, l1_lru_lambda, l1_w_out, l1_ffn_norm, l1_ffn_w_in, l1_ffn_conv_w, l1_ffn_conv_b, l1_ffn_w_out, final_norm, loss_target, m_meta_tokens, m_l0_mix_norm, m_l0_w_in, m_l0_ssd_conv_w, m_l0_ssd_conv_b, m_l0_ssd_dt_bias, m_l0_ssd_a_log, m_l0_ssd_d, m_l0_ssd_norm, m_l0_ret_norm, m_l0_w_out, m_l0_ffn_norm, m_l0_ffn_w_in, m_l0_ffn_conv_w, m_l0_ffn_conv_b, m_l0_ffn_w_out, m_l1_mix_norm, m_l1_w_in, m_l1_lru_conv_w, m_l1_lru_conv_b, m_l1_lru_wa, m_l1_lru_ba, m_l1_lru_wx, m_l1_lru_bx, m_l1_lru_lambda, m_l1_w_out, m_l1_ffn_norm, m_l1_ffn_w_in, m_l1_ffn_conv_w, m_l1_ffn_conv_b, m_l1_ffn_w_out, m_final_norm, v_meta_tokens, v_l0_mix_norm, v_l0_w_in, v_l0_ssd_conv_w, v_l0_ssd_conv_b, v_l0_ssd_dt_bias, v_l0_ssd_a_log, v_l0_ssd_d, v_l0_ssd_norm, v_l0_ret_norm, v_l0_w_out, v_l0_ffn_norm, v_l0_ffn_w_in, v_l0_ffn_conv_w, v_l0_ffn_conv_b, v_l0_ffn_w_out, v_l1_mix_norm, v_l1_w_in, v_l1_lru_conv_w, v_l1_lru_conv_b, v_l1_lru_wa, v_l1_lru_ba, v_l1_lru_wx, v_l1_lru_bx, v_l1_lru_lambda, v_l1_w_out, v_l1_ffn_norm, v_l1_ffn_w_in, v_l1_ffn_conv_w, v_l1_ffn_conv_b, v_l1_ffn_w_out, v_final_norm)
    p = dict(zip(_IN_NAMES, args))
    B, seq, _ = x.shape
    nch = (seq + CH) // CH
    Pn = nch * CH
    R = B * Pn
    chip = 2 * lax.axis_index("x") + lax.axis_index("y")
    row2 = lambda v: v.reshape(1, -1)
    pad128 = lambda v: jnp.pad(v, (0, 128 - v.shape[0])).reshape(1, 128)

    small_shapes = [p[n].shape for n in _SMALL_SHARDED]
    halved = lambda w: w.astype(_MXU).reshape(2, w.shape[0] // 2, w.shape[1])
    mine = {n: halved(p[n]) for n in _BIG}
    mine_small = _pack([p[n] for n in _SMALL_SHARDED])
    W = {}

    def set_weight(n, g):
        g = _fill_own(g, mine[n], chip)
        g = g.reshape(4, -1, g.shape[3])
        W[n] = jnp.concatenate([g[k] for k in range(4)], axis=1) if n in _BIG_COLS else g.reshape(-1, g.shape[2])

    def gather_on(*names):
        return ("gather", [mine[n] for n in names])

    def take_weights(names, got):
        for n, g in zip(names, got):
            set_weight(n, g)

    gathered = _gather_shards([mine['l0_w_in']], mine_small)
    set_weight('l0_w_in', gathered[0])
    g_small = _fill_own(gathered[-1], mine_small, chip)
    per_chip = [_unpack(g_small[k], small_shapes) for k in range(4)]
    for i, n in enumerate(_SMALL_SHARDED):
        W[n] = jnp.concatenate([per_chip[k][i] for k in range(4)], axis=1)
    w0 = W['l0_w_in']
    w0_main = jnp.concatenate([w0[:, 3088:], w0[:, :3072]], axis=1)
    w0_dt = jnp.pad(w0[:, 3072:3088], ((0, 0), (0, 112)))
    cos, sin = _rope_tables(nch)

    meta = jnp.broadcast_to(W['meta_tokens'][None], (B, N_META, D))
    h0 = jnp.concatenate([jnp.zeros((B, PAD, D), F32), meta, x], axis=1).reshape(R, D)
    n0, n0t = _rmsnorm_fwd(h0, row2(p['l0_mix_norm']), "norm_l0_mix")
    u0 = _mm(n0, w0_main, "nn", F32, "l0_in_proj")
    udt = _mm(n0, w0_dt, "nn", F32, "l0_dt_proj")
    a_log, d_skip, dt_bias = pad128(p['l0_ssd_a_log']), pad128(p['l0_ssd_d']), pad128(p['l0_ssd_dt_bias'])
    ssd_cb = row2(p['l0_ssd_conv_b'])
    act, dt, dtt, *got = _ssd_prep(u0, udt, W['l0_ssd_conv_w'], ssd_cb, dt_bias, B, nch, rider=gather_on('l0_w_out'))
    take_weights(['l0_w_out'], got)
    ycat0, ypre, hin, *got = _ssd_fwd(act, u0, dt, dtt, a_log, d_skip, row2(p['l0_ssd_norm']), B, nch,
                                      rider=gather_on('l0_ffn_w_in'))
    take_weights(['l0_ffn_w_in'], got)
    ycat0, opre, rin, *got = _ret_fwd(u0, ycat0, cos, sin, row2(p['l0_ret_norm']), B, nch,
                                      rider=gather_on('l0_ffn_w_out'))
    take_weights(['l0_ffn_w_out'], got)
    h1 = _mm(ycat0, W['l0_w_out'], "nn", F32, "l0_out_proj", add=h0)
    n1, n1t = _rmsnorm_fwd(h1, row2(p['l0_ffn_norm']), "norm_l0_ffn")
    uf0 = _mm(n1, W['l0_ffn_w_in'], "nn", F32, "l0_ffn_in")
    f0_cb = row2(p['l0_ffn_conv_b'])
    a0, *got = _ffn_act_fwd(uf0, W['l0_ffn_conv_w'], f0_cb, B, nch, rider=gather_on('l1_w_in'))
    take_weights(['l1_w_in'], got)
    h2 = _mm(a0, W['l0_ffn_w_out'], "nn", F32, "l0_ffn_out", add=h1)
    n2, n2t = _rmsnorm_fwd(h2, row2(p['l1_mix_norm']), "norm_l1_mix")
    u1 = _mm(n2, W['l1_w_in'], "nn", F32, "l1_in_proj")
    lru = (W['l1_lru_conv_w'], row2(p['l1_lru_conv_b']), p['l1_lru_wa'], row2(p['l1_lru_ba']), p['l1_lru_wx'],
           row2(p['l1_lru_bx']), row2(p['l1_lru_lambda']))
    later = ['l1_w_out', 'l1_ffn_w_in', 'l1_ffn_w_out']
    ycat1, stot, *got = _sb_fwd(u1, B, nch, rider=gather_on(*later))
    take_weights(later, got)
    ycat1, hs = _lru_fwd(u1, ycat1, *lru, B, nch)
    h3 = _mm(ycat1, W['l1_w_out'], "nn", F32, "l1_out_proj", add=h2)
    n3, n3t = _rmsnorm_fwd(h3, row2(p['l1_ffn_norm']), "norm_l1_ffn")
    uf1 = _mm(n3, W['l1_ffn_w_in'], "nn", F32, "l1_ffn_in")
    f1_cb = row2(p['l1_ffn_conv_b'])
    a1, = _ffn_act_fwd(uf1, W['l1_ffn_conv_w'], f1_cb, B, nch)
    h4 = _mm(a1, W['l1_ffn_w_out'], "nn", F32, "l1_ffn_out", add=h3)
    dh4, lossp, dgf = _head(h4, row2(p['final_norm']), p['loss_target'].reshape(B * seq, D), B, nch)
    loss = lax.psum(jnp.sum(lossp[:, 0, 0]), ("x", "y", "c"))

    G = {'final_norm': dgf[:, 0].sum(0)}

    core = lax.axis_index("c").reshape(1).astype(jnp.int32)

    def chip_sums(names, tag):
        stacked = []
        for n in names:
            g = G[n]
            if n in _BIG_COLS:
                stacked.append(g.reshape(g.shape[0], 4, g.shape[1] // 4).transpose(1, 0, 2))
            else:
                stacked.append(g.reshape(4, g.shape[0] // 4, g.shape[1]))
        theirs = _swap_halves(stacked, "swap_halves_" + tag)
        return {n: _chip_sum(g, t, core, "chip_sum_" + n) for n, g, t in zip(names, stacked, theirs)}

    parts = {}

    def scatter_on(names, tag):
        sums = chip_sums(names, tag)
        return sums, ("scatter", [sums[n] for n in names])

    def take_parts(names, sums, got):
        for n, g in zip(names, got):
            parts[n] = _fill_own(g, lax.dynamic_index_in_dim(sums[n], chip, 0, keepdims=False), chip)

    def ffn_bwd(layer, dh_out, h_in, nt_in, uf, a_act, cb, rider=None):
        pre = f"l{layer}_"
        w_in, w_out, cw = W[pre + 'ffn_w_in'], W[pre + 'ffn_w_out'], W[pre + 'ffn_conv_w']
        da = _mm(dh_out, w_out, "nt", F32, pre + "ffn_out_dgrad")
        G[pre + 'ffn_w_out'] = _mm(a_act, dh_out, "tn", F32, pre + "ffn_out_wgrad")
        dug, duu, dwg, dwu, *rode = _ffn_act_bwd(da, uf, cw, cb, nch, pre + "ffn_act_bwd", rider=rider)
        G[pre + 'ffn_conv_w'] = jnp.concatenate([dwg[:3], dwu[:3]], axis=1)
        G[pre + 'ffn_conv_b'] = jnp.concatenate([dwg[7], dwu[7]])
        dn = _mm(dug, w_in, "nt", F32, pre + "ffn_in_dgrad_g")
        dn = _mm(duu, w_in, "nt", F32, pre + "ffn_in_dgrad_u", add=dn, b_off=FFN)
        G[pre + 'ffn_w_in'] = jnp.concatenate([_mm(nt_in, dug, "nn", F32, pre + "ffn_in_wgrad_g"),
                                               _mm(nt_in, duu, "nn", F32, pre + "ffn_in_wgrad_u")], axis=1)
        dh_in, dg = _rmsnorm_bwd(h_in, row2(p[pre + 'ffn_norm']), dn, dh_out, nch, pre + "ffn_norm_bwd")
        G[pre + 'ffn_norm'] = dg[0]
        return dh_in, rode

    dh3, _ = ffn_bwd(1, dh4, h3, n3t, uf1, a1, f1_cb)
    dy1 = _mm(dh3, W['l1_w_out'], "nt", F32, "l1_out_dgrad")
    G['l1_w_out'] = _mm(ycat1, dh3, "tn", F32, "l1_out_wgrad")
    done = ['l1_ffn_w_in', 'l1_ffn_w_out', 'l1_w_out']
    sums, rider = scatter_on(done, "a")
    dq, dkt, dvt, *got = _sb_bwd(dy1, u1, stot, B, nch, rider=rider)
    dk, dv = dkt.T, dvt.T
    take_parts(done, sums, got)
    dgate, dxc, pgl, dwa, dwx = _lru_bwd(dy1, u1, hs, *lru, B, nch)
    dxr, dcw = _conv_bwd(dxc, u1, 4096, W['l1_lru_conv_w'], 4, "l1_lru_conv_bwd")
    pgl = pgl.sum(0)
    G['l1_lru_ba'], G['l1_lru_bx'], G['l1_lru_lambda'] = pgl[0], pgl[1], pgl[2]
    G['l1_lru_wa'], G['l1_lru_wx'] = dwa.sum(0), dwx.sum(0)
    G['l1_lru_conv_w'], G['l1_lru_conv_b'] = dcw[:4], dcw[7]
    dn, dws = None, []
    for i, piece in enumerate((dq, dk, dv, dgate, dxr)):
        dn = _mm(piece, W['l1_w_in'], "nt", F32, f"l1_in_dgrad_{i}", add=dn, b_off=1024 * i)
        dws.append(_mm(n2t, piece, "nn", F32, f"l1_in_wgrad_{i}"))
    G['l1_w_in'] = jnp.concatenate(dws, axis=1)
    dh2, dg = _rmsnorm_bwd(h2, row2(p['l1_mix_norm']), dn, dh3, nch, "l1_mix_norm_bwd")
    G['l1_mix_norm'] = dg[0]

    sums, rider = scatter_on(['l1_w_in'], "b")
    dh1, got = ffn_bwd(0, dh2, h1, n1t, uf0, a0, f0_cb, rider=rider)
    take_parts(['l1_w_in'], sums, got)
    dy0 = _mm(dh1, W['l0_w_out'], "nt", F32, "l0_out_dgrad")
    G['l0_w_out'] = _mm(ycat0, dh1, "tn", F32, "l0_out_wgrad")
    done = ['l0_ffn_w_in', 'l0_ffn_w_out', 'l0_w_out']
    sums, rider = scatter_on(done, "c")
    dz, dxs, dbm, dcm, ddt4, pgs, *got = _ssd_bwd(dy0, ypre, u0, act, dt, dtt, hin, a_log, d_skip,
                                                  row2(p['l0_ssd_norm']), B, nch, rider=rider)
    take_parts(done, sums, got)
    dpre, ddtr, pgd = _ssd_prep_bwd(dxs, dbm, dcm, ddt4, u0, udt, W['l0_ssd_conv_w'], ssd_cb, dt_bias, B, nch)
    dxbc, dcw0 = _conv_bwd(dpre, u0, U0_XBC, W['l0_ssd_conv_w'], 4, "l0_ssd_conv_bwd")
    dqkvg, pgr = _ret_bwd(dy0, u0, opre, rin, cos, sin, row2(p['l0_ret_norm']), B, nch)
    pgs = pgs.sum(0)
    G['l0_ssd_norm'] = pgs[:, 0, :].reshape(-1)
    G['l0_ssd_d'] = pgs[:, 1, :128].sum(0)[:SSD_HEADS]
    G['l0_ssd_a_log'] = pgs[:, 2, :128].sum(0)[:SSD_HEADS]
    G['l0_ssd_dt_bias'] = pgd.sum(0)[0, :SSD_HEADS]
    G['l0_ssd_conv_w'], G['l0_ssd_conv_b'] = dcw0[:4], dcw0[7]
    G['l0_ret_norm'] = pgr.sum(0)[0]
    dn = _mm(dqkvg, w0_main, "nt", F32, "l0_in_dgrad_qkvg")
    dn = _mm(dz, w0_main, "nt", F32, "l0_in_dgrad_z", add=dn, b_off=U0_Z)
    dn = _mm(dxbc, w0_main, "nt", F32, "l0_in_dgrad_xbc", add=dn, b_off=U0_XBC)
    dn = _mm(ddtr, w0_dt, "nt", F32, "l0_in_dgrad_dt", add=dn)
    G['l0_w_in'] = jnp.concatenate([
        _mm(n0t, dz, "nn", F32, "l0_in_wgrad_z"), _mm(n0t, dxbc, "nn", F32, "l0_in_wgrad_xbc"),
        _mm(n0t, ddtr, "nn", F32, "l0_in_wgrad_dt")[:, :SSD_HEADS], _mm(n0t, dqkvg, "nn", F32, "l0_in_wgrad_qkvg")], axis=1)
    dh0, dg = _rmsnorm_bwd(h0, row2(p['l0_mix_norm']), dn, dh1, nch, "l0_mix_norm_bwd")
    G['l0_mix_norm'] = dg[0]
    dh0 = dh0.reshape(B, Pn, D)
    grad_x = dh0[:, CH:]
    G['meta_tokens'] = dh0[:, PAD:CH].sum(0)

    sums = chip_sums(['l0_w_in'], "d")
    parts['l0_w_in'], = _scatter_sums([sums['l0_w_in']])
    reds = [_sum_chips(parts[n], "sum_chips_" + n) for n in _BIG]
    grads = {}
    for n, own, other in zip(_BIG, reds, _join_halves(reds)):
        both = jnp.where(core[0] == 0, jnp.stack([own, other]), jnp.stack([other, own]))
        grads[n] = both.reshape(-1, both.shape[2])
    small_full = _unpack(_allreduce_small(_pack([G[n] for n in _SMALL])), [G[n].shape for n in _SMALL])
    for n, g in zip(_SMALL, small_full):
        if n in _SMALL_SHARDED:
            cs = g.shape[1] // 4
            g = lax.dynamic_slice_in_dim(g, chip * cs, cs, axis=1)
        grads[n] = g.reshape(p[n].shape)

    delta, new_m, new_v = {}, {}, {}
    for n in _BIG:
        delta[n], new_m[n], new_v[n] = _adamw(p[n], grads[n], p['m_' + n], p['v_' + n], "adamw_" + n)
    shapes = [p[n].shape for n in _SMALL]
    outs = _adamw(_pack([p[n] for n in _SMALL]), _pack([grads[n] for n in _SMALL]), _pack([p['m_' + n] for n in _SMALL]),
                  _pack([p['v_' + n] for n in _SMALL]), "adamw_small")
    for dst, buf in zip((delta, new_m, new_v), outs):
        for n, a in zip(_SMALL, _unpack(buf, shapes)):
            dst[n] = a
    return (loss, grad_x, *[grads[n] for n in _W_NAMES], *[delta[n] for n in _W_NAMES],
            *[new_m[n] for n in _W_NAMES], *[new_v[n] for n in _W_NAMES])
```

```python
import math

import numpy as np
import jax
import jax.numpy as jnp
from jax import lax
from jax.experimental import pallas as pl
from jax.experimental.pallas import tpu as pltpu

F32 = jnp.float32
BF16 = jnp.bfloat16
_MXU = jnp.bfloat16

D = 1024
CH = 128
N_META = 16
PAD = CH - N_META
EPS = 1e-6

SSD_HEADS = 16
SSD_HD = 64
SSD_GROUPS = 4
RET_HEADS = 4
RET_DK = 256
SB_HEADS = 16
SB_HD = 64
LRU_BLOCKS = 8
LRU_C = 8.0
FFN = 2816
U0_Z = 4096
U0_XBC = 5120

VMEM_LIMIT = 56 * 1024 * 1024


def _cparams(sem):
    return pltpu.CompilerParams(dimension_semantics=sem, vmem_limit_bytes=VMEM_LIMIT)


def _dot(a, b, dims=((1,), (0,))):
    return lax.dot_general(a.astype(_MXU), b.astype(_MXU), (dims, ((), ())), preferred_element_type=F32)


def _dot_nt(a, b):
    return _dot(a, b, ((1,), (1,)))


def _dot_tn(a, b):
    return _dot(a.T, b)


def _dot_exact(a, b):
    return lax.dot_general(a, b, (((1,), (0,)), ((), ())), preferred_element_type=F32,
                           precision=lax.Precision.HIGHEST)


def _dot_split(x, m01):
    hi = x.astype(BF16)
    lo = (x - hi.astype(F32)).astype(BF16)
    m = m01.astype(BF16)
    return jnp.dot(hi, m, preferred_element_type=F32) + jnp.dot(lo, m, preferred_element_type=F32)


def _sigmoid(x):
    return 0.5 * jnp.tanh(0.5 * x) + 0.5


def _softplus(x):
    return jnp.maximum(x, 0.0) + jnp.log1p(jnp.exp(-jnp.abs(x)))


def _silu(x):
    return x * _sigmoid(x)


def _dsilu(x):
    s = _sigmoid(x)
    return s * (1.0 + x * (1.0 - s))


_GELU_C = math.sqrt(2.0 / math.pi)


def _gelu(x):
    return 0.5 * x * (1.0 + jnp.tanh(_GELU_C * (x + 0.044715 * x * x * x)))


def _dgelu(x):
    t = jnp.tanh(_GELU_C * (x + 0.044715 * x * x * x))
    return 0.5 * (1.0 + t) + 0.5 * x * (1.0 - t * t) * _GELU_C * (1.0 + 3.0 * 0.044715 * x * x)


def _row_ids(n, cols=1):
    return lax.broadcasted_iota(jnp.int32, (n, cols), 0)


def _lane_ids(rows, n):
    return lax.broadcasted_iota(jnp.int32, (rows, n), 1)


def _real_rows(chunk):
    return chunk * CH + _row_ids(CH) >= PAD


def _shift_down(prev8, cur, s):
    cat = jnp.concatenate([prev8, cur], axis=0)
    return pltpu.roll(cat, s, axis=0)[8:]


def _shift_up(cur, next8, s):
    n = cur.shape[0]
    cat = jnp.concatenate([cur, next8], axis=0)
    return pltpu.roll(cat, n + 8 - s, axis=0)[:n]


def _conv_pre(prev8, cur, w_ref, b_ref, K):
    acc = cur * w_ref[K - 1:K, :] + b_ref[...]
    for s in range(1, K):
        acc = acc + _shift_down(prev8, cur, s) * w_ref[K - 1 - s:K - s, :]
    return acc


def _prev8_map(nch, col):
    return lambda b, c: (jnp.maximum((b * nch + c) * (CH // 8) - 1, 0), col)


def _matmul(a, b, mode, out_dtype, tm, tn, tk, name, add=None, b_off=0):
    if mode == "nn":
        (M, K), (_, N) = a.shape, b.shape
    elif mode == "nt":
        (M, K), N = a.shape, b.shape[0]
    else:
        (K, M), (_, N) = a.shape, b.shape
    tm, tn, tk = min(tm, M), min(tn, N), min(tk, K)
    assert M % tm == 0 and N % tn == 0 and K % tk == 0 and b_off % tk == 0, (name, M, N, K, tm, tn, tk)
    koff = b_off // tk
    nk = K // tk
    dims = {"nn": ((1,), (0,)), "nt": ((1,), (1,)), "tn": ((0,), (0,))}[mode]
    if mode == "tn":
        a_spec = pl.BlockSpec((tk, tm), lambda i, j, k: (k, i))
    else:
        a_spec = pl.BlockSpec((tm, tk), lambda i, j, k: (i, k))
    if mode == "nt":
        b_spec = pl.BlockSpec((tn, tk), lambda i, j, k: (j, k + koff))
    else:
        b_spec = pl.BlockSpec((tk, tn), lambda i, j, k: (k, j))
    o_spec = pl.BlockSpec((tm, tn), lambda i, j, k: (i, j))
    has_add = add is not None

    def body(a_ref, b_ref, *rest):
        if has_add:
            add_ref, o_ref, acc = rest
        else:
            o_ref, acc = rest
        k = pl.program_id(2)

        @pl.when(k == 0)
        def _():
            acc[...] = jnp.zeros_like(acc)

        acc[...] += _dot(a_ref[...], b_ref[...], dims)

        @pl.when(k == nk - 1)
        def _():
            r = acc[...]
            if has_add:
                r = r + add_ref[...].astype(F32)
            o_ref[...] = r.astype(out_dtype)

    in_specs = [a_spec, b_spec] + ([o_spec] if has_add else [])
    args = (a, b) + ((add,) if has_add else ())
    return pl.pallas_call(
        body, name=name, grid=(M // tm, N // tn, nk),
        in_specs=in_specs, out_specs=o_spec,
        out_shape=jax.ShapeDtypeStruct((M, N), out_dtype),
        scratch_shapes=[pltpu.VMEM((tm, tn), F32)],
        compiler_params=_cparams(("parallel", "parallel", "arbitrary")),
    )(*args)


def _tile(n, prefs):
    for t in prefs:
        if n % t == 0:
            return t
    return n


def _mm(a, b, mode, out_dtype, name, add=None, b_off=0):
    if mode == "tn":
        K, M = a.shape
        N = b.shape[1]
        tm, tn, tk = _tile(M, (1024, 1408, 512, 256, 128)), _tile(N, (1024, 1408, 512, 256, 128)), _tile(K, (2176, 384, 256, 128))
    else:
        M, K = a.shape
        N = b.shape[1] if mode == "nn" else b.shape[0]
        tm = _tile(M, (1088, 1024, 768, 512, 384, 256, 128))
        tn = _tile(N, (1024, 1408, 512, 256, 128))
        tk = _tile(K, (2176, 1024, 1408, 512, 256, 128))
    return _matmul(a, b, mode, out_dtype, tm, tn, tk, name, add=add, b_off=b_off)


def _rmsnorm_fwd(h, g, name):
    R = h.shape[0]
    tr = 2 * CH

    def body(h_ref, g_ref, o_ref, ot_ref):
        x = h_ref[...]
        r = lax.rsqrt(jnp.mean(x * x, axis=-1, keepdims=True) + EPS)
        y = x * r * g_ref[...]
        o_ref[...] = y.astype(o_ref.dtype)
        ot_ref[...] = y.T.astype(ot_ref.dtype)

    return pl.pallas_call(
        body, name=name, grid=(R // tr,),
        in_specs=[pl.BlockSpec((tr, D), lambda i: (i, 0)), pl.BlockSpec((1, D), lambda i: (0, 0))],
        out_specs=[pl.BlockSpec((tr, D), lambda i: (i, 0)), pl.BlockSpec((D, tr), lambda i: (0, i))],
        out_shape=[jax.ShapeDtypeStruct((R, D), _MXU), jax.ShapeDtypeStruct((D, R), _MXU)],
        compiler_params=_cparams(("parallel",)),
    )(h, g)


def _rmsnorm_bwd(h, g, dn, dres, nch, name):
    R = h.shape[0]
    per = 4
    tr = nch * CH // per

    def body(h_ref, g_ref, dn_ref, dres_ref, dh_ref, dg_ref):
        i = pl.program_id(0)
        x = h_ref[...]
        r = lax.rsqrt(jnp.mean(x * x, axis=-1, keepdims=True) + EPS)
        xhat = x * r
        dn_v = dn_ref[...]
        dx = dn_v * g_ref[...]
        dh = r * (dx - xhat * jnp.mean(dx * xhat, axis=-1, keepdims=True))
        keep = (i % per) * tr + _row_ids(tr) >= PAD
        dh_ref[...] = jnp.where(keep, dres_ref[...] + dh, 0.0)

        @pl.when(i == 0)
        def _():
            dg_ref[...] = jnp.zeros_like(dg_ref)

        dg_ref[...] += jnp.sum(dn_v * xhat, axis=0, keepdims=True)

    row = pl.BlockSpec((tr, D), lambda i: (i, 0))
    vec = pl.BlockSpec((1, D), lambda i: (0, 0))
    return pl.pallas_call(
        body, name=name, grid=(R // tr,),
        in_specs=[row, vec, row, row], out_specs=[row, vec],
        out_shape=[jax.ShapeDtypeStruct((R, D), F32), jax.ShapeDtypeStruct((1, D), F32)],
        compiler_params=_cparams(("arbitrary",)),
    )(h, g, dn, dres)


def _ssd_prep(u0, udt, conv_w, conv_b, dt_bias, B, nch, rider=None):
    R = u0.shape[0]

    def body(xs_ref, xsp_ref, bc_ref, bcp_ref, udt_ref, w0_ref, w1_ref, b0_ref, b1_ref, dtb_ref,
             act_ref, dt_ref, dtt_ref):
        keep = _real_rows(pl.program_id(1))
        a0 = _silu(_conv_pre(xsp_ref[...], xs_ref[...], w0_ref, b0_ref, 4))
        a1 = _silu(_conv_pre(bcp_ref[...], bc_ref[...], w1_ref, b1_ref, 4))
        act_ref[:, :1024] = jnp.where(keep, a0, 0.0)
        act_ref[:, 1024:] = jnp.where(keep, a1, 0.0)
        ok = jnp.logical_and(keep, _lane_ids(1, 128) < SSD_HEADS)
        dt = jnp.where(ok, _softplus(udt_ref[...] + dtb_ref[...]), 0.0)
        dt_ref[...] = dt
        dtt_ref[...] = dt.T

    row = lambda col: pl.BlockSpec((CH, 1024), lambda b, c: (b * nch + c, col))
    prev = lambda col: pl.BlockSpec((8, 1024), _prev8_map(nch, col))
    kw = dict(
        grid=(B, nch),
        in_specs=[row(5), prev(5), row(6), prev(6),
                  pl.BlockSpec((CH, 128), lambda b, c: (b * nch + c, 0)),
                  pl.BlockSpec((4, 1024), lambda b, c: (0, 0)), pl.BlockSpec((4, 1024), lambda b, c: (0, 1)),
                  pl.BlockSpec((1, 1024), lambda b, c: (0, 0)), pl.BlockSpec((1, 1024), lambda b, c: (0, 1)),
                  pl.BlockSpec((1, 128), lambda b, c: (0, 0))],
        out_specs=[pl.BlockSpec((CH, 2048), lambda b, c: (b * nch + c, 0)),
                   pl.BlockSpec((CH, 128), lambda b, c: (b * nch + c, 0)),
                   pl.BlockSpec((128, CH), lambda b, c: (0, b * nch + c))],
        out_shape=[jax.ShapeDtypeStruct((R, 2048), F32), jax.ShapeDtypeStruct((R, 128), F32),
                   jax.ShapeDtypeStruct((128, R), F32)])
    return _call(body, "ssd_prep", ("arbitrary", "arbitrary"), kw,
                 (u0, u0, u0, u0, udt, conv_w, conv_w, conv_b, conv_b, dt_bias), rider)


def _ssd_head_terms(h, a_vec, dt_v, dtt_v, dsk_v):
    lane = _lane_ids(1, 128)
    sub = _row_ids(128)
    r = _row_ids(CH, CH)
    cidx = _lane_ids(CH, CH)
    a_h = jnp.sum(jnp.where(lane == h, a_vec, 0.0), axis=1, keepdims=True)
    dt_col = jnp.sum(jnp.where(lane == h, dt_v, 0.0), axis=1, keepdims=True)
    dt_row = jnp.sum(jnp.where(sub == h, dtt_v, 0.0), axis=0, keepdims=True)
    cs_col = jnp.sum(jnp.where(r >= cidx, dt_row * a_h, 0.0), axis=1, keepdims=True)
    cs_row = jnp.sum(jnp.where(r <= cidx, dt_col * a_h, 0.0), axis=0, keepdims=True)
    tot = jnp.sum(dt_col * a_h, axis=0, keepdims=True)
    dsk = jnp.sum(jnp.where(lane == h, dsk_v, 0.0), axis=1, keepdims=True)
    return a_h, dt_col, cs_col, cs_row, tot, dsk


def _ssd_fwd(act, u0, dt, dtt, a_log, d_skip, norm_g, B, nch, rider=None):
    R = act.shape[0]

    def body(xs_ref, bm_ref, cm_ref, z_ref, dt_ref, dtt_ref, alog_ref, dsk_ref, ng_ref,
             out_ref, ypre_ref, hin_ref, H):
        g = pl.program_id(1)
        c = pl.program_id(2)

        @pl.when(c == 0)
        def _():
            H[...] = jnp.zeros_like(H)

        hin_ref[...] = H[...]
        a_vec = -jnp.exp(alog_ref[...])
        dt_v = dt_ref[...]
        dtt_v = dtt_ref[...]
        hm = _lane_ids(1, 128) < SSD_HD
        r = _row_ids(CH, CH)
        cidx = _lane_ids(CH, CH)
        Bm = bm_ref[...]
        Cm = cm_ref[...]
        CB = _dot_nt(Cm, Bm)
        ys = []
        for pair in range(2):
            cols = slice(128 * pair, 128 * pair + 128)
            xraw = xs_ref[:, cols]
            t = [_ssd_head_terms(4 * g + 2 * pair + j, a_vec, dt_v, dtt_v, dsk_ref[...]) for j in range(2)]
            sel = lambda f: jnp.where(hm, f(t[0]), f(t[1]))
            dtp = sel(lambda q: q[1])
            Ep = sel(lambda q: jnp.exp(q[2]))
            Wp = sel(lambda q: jnp.exp(q[4] - q[2]))
            etot = sel(lambda q: jnp.exp(q[4]))
            dsk = sel(lambda q: q[5])
            X = xraw * dtp
            ydiag = jnp.zeros((CH, 128), F32)
            for j in range(2):
                Lm = jnp.where(r >= cidx, jnp.exp(t[j][2] - t[j][3]), 0.0)
                Xh = jnp.where(hm if j == 0 else jnp.logical_not(hm), X, 0.0)
                ydiag = ydiag + _dot(CB * Lm, Xh)
            Hp = H[:, cols]
            yoff = Ep * _dot(Cm, Hp)
            S = _dot(Bm.T, X * Wp)
            H[:, cols] = etot * Hp + S
            ys.append(ydiag + yoff + xraw * dsk)
        y = jnp.concatenate(ys, axis=1)
        ypre_ref[...] = y
        yg = y * _silu(z_ref[...])
        rr = lax.rsqrt(jnp.mean(yg * yg, axis=-1, keepdims=True) + EPS)
        out_ref[...] = jnp.where(_real_rows(c), yg * rr * ng_ref[...], 0.0).astype(out_ref.dtype)

    rowb = lambda w, colf: pl.BlockSpec((CH, w), lambda b, g, c: (b * nch + c, colf(g)))
    vec = pl.BlockSpec((1, 128), lambda b, g, c: (0, 0))
    kw = dict(
        grid=(B, SSD_GROUPS, nch),
        in_specs=[rowb(256, lambda g: g), rowb(128, lambda g: 8 + g), rowb(128, lambda g: 12 + g),
                  rowb(256, lambda g: 16 + g), rowb(128, lambda g: 0),
                  pl.BlockSpec((128, CH), lambda b, g, c: (0, b * nch + c)),
                  vec, vec, pl.BlockSpec((1, 256), lambda b, g, c: (0, g))],
        out_specs=[rowb(256, lambda g: g), rowb(256, lambda g: g),
                   pl.BlockSpec((None, None, None, 128, 256), lambda b, g, c: (b, g, c, 0, 0))],
        out_shape=[jax.ShapeDtypeStruct((R, 2048), _MXU), jax.ShapeDtypeStruct((R, 1024), F32),
                   jax.ShapeDtypeStruct((B, SSD_GROUPS, nch, 128, 256), F32)],
        scratch_shapes=[pltpu.VMEM((128, 256), F32)])
    return _call(body, "ssd_fwd", ("arbitrary", "arbitrary", "arbitrary"), kw,
                 (act, act, act, u0, dt, dtt, a_log, d_skip, norm_g), rider)


def _ssd_bwd(dycat, ypre, u0, act, dt, dtt, hin, a_log, d_skip, norm_g, B, nch, rider=None):
    R = act.shape[0]

    def body(dy_ref, ypre_ref, z_ref, xs_ref, bm_ref, cm_ref, dt_ref, dtt_ref, hin_ref, alog_ref, dsk_ref, ng_ref,
             dz_ref, dxs_ref, db_ref, dc_ref, ddt_ref, pg_ref, dH):
        g = pl.program_id(1)
        c = nch - 1 - pl.program_id(2)

        @pl.when(pl.program_id(2) == 0)
        def _():
            dH[...] = jnp.zeros_like(dH)
            pg_ref[...] = jnp.zeros_like(pg_ref)

        z = z_ref[...]
        y = ypre_ref[...]
        ng = ng_ref[...]
        dout = jnp.where(_real_rows(c), dy_ref[...], 0.0)
        sz = _sigmoid(z)
        yg = y * z * sz
        rr = lax.rsqrt(jnp.mean(yg * yg, axis=-1, keepdims=True) + EPS)
        nrm = yg * rr
        pg_ref[0:1, :] += jnp.sum(dout * nrm, axis=0, keepdims=True)
        dn = dout * ng
        dyg = rr * (dn - nrm * jnp.mean(dn * nrm, axis=-1, keepdims=True))
        dy = dyg * z * sz
        dz_ref[...] = (dyg * y * (sz * (1.0 + z * (1.0 - sz)))).astype(dz_ref.dtype)

        a_vec = -jnp.exp(alog_ref[...])
        dt_v = dt_ref[...]
        dtt_v = dtt_ref[...]
        lane = _lane_ids(1, 128)
        hm = lane < SSD_HD
        r = _row_ids(CH, CH)
        cidx = _lane_ids(CH, CH)
        last = _row_ids(CH) == CH - 1
        Bm = bm_ref[...]
        Cm = cm_ref[...]
        CB = _dot_nt(Cm, Bm)
        CBT = _dot_nt(Bm, Cm)
        dB = jnp.zeros((CH, 128), F32)
        dC = jnp.zeros((CH, 128), F32)
        dcs_all = jnp.zeros((CH, 128), F32)
        dtx_all = jnp.zeros((CH, 128), F32)
        dd_row = jnp.zeros((1, 128), F32)
        dxs = []
        for pair in range(2):
            cols = slice(128 * pair, 128 * pair + 128)
            xraw = xs_ref[:, cols]
            dyp = dy[:, cols]
            heads = [4 * g + 2 * pair + j for j in range(2)]
            t = [_ssd_head_terms(heads[j], a_vec, dt_v, dtt_v, dsk_ref[...]) for j in range(2)]
            sel = lambda f: jnp.where(hm, f(t[0]), f(t[1]))
            hsum = lambda v, j: jnp.sum(jnp.where(hm if j == 0 else jnp.logical_not(hm), v, 0.0), axis=1, keepdims=True)
            dtp = sel(lambda q: q[1])
            Ep = sel(lambda q: jnp.exp(q[2]))
            Wp = sel(lambda q: jnp.exp(q[4] - q[2]))
            etot = sel(lambda q: jnp.exp(q[4]))
            dsk = sel(lambda q: q[5])
            X = xraw * dtp
            Hp = hin_ref[:, cols]
            dHn = dH[:, cols]
            dskip = jnp.sum(dyp * xraw, axis=0, keepdims=True)
            yoff = Ep * _dot(Cm, Hp)
            dE = dyp * yoff
            dC = dC + _dot_nt(dyp * Ep, Hp)
            dH[:, cols] = etot * dHn + _dot(Cm.T, dyp * Ep)
            BdS = _dot(Bm, dHn)
            dX = Wp * BdS
            ew = X * BdS * Wp
            dB = dB + _dot_nt(X * Wp, dHn)
            hh = jnp.sum(dHn * Hp, axis=0, keepdims=True) * etot
            for j in range(2):
                hmask = hm if j == 0 else jnp.logical_not(hm)
                cs_col, cs_row = t[j][2], t[j][3]
                Lm = jnp.where(r >= cidx, jnp.exp(cs_col - cs_row), 0.0)
                LmT = jnp.where(cidx >= r, jnp.exp(cs_row - cs_col), 0.0)
                dyh = jnp.where(hmask, dyp, 0.0)
                Xh = jnp.where(hmask, X, 0.0)
                dM = _dot_nt(dyh, Xh)
                dMT = _dot_nt(Xh, dyh)
                M = CB * Lm
                MT = CBT * LmT
                dX = dX + _dot(MT, dyh)
                dC = dC + _dot(dM * Lm, Bm)
                dB = dB + _dot(dMT * LmT, Cm)
                g_rows = jnp.sum(dM * M, axis=1, keepdims=True)
                g_cols = jnp.sum(dMT * MT, axis=1, keepdims=True)
                dtot = (jnp.sum(hsum(ew, j), axis=0, keepdims=True)
                        + jnp.sum(jnp.where(hmask, hh, 0.0), axis=1, keepdims=True))
                dcs = g_rows - g_cols + hsum(dE, j) - hsum(ew, j) + jnp.where(last, dtot, 0.0)
                dcs_all = dcs_all + jnp.where(lane == heads[j], dcs, 0.0)
                dtx_all = dtx_all + jnp.where(lane == heads[j], hsum(dX * xraw, j), 0.0)
                dd_row = dd_row + jnp.where(lane == heads[j],
                                            jnp.sum(jnp.where(hmask, dskip, 0.0), axis=1, keepdims=True), 0.0)
            dxs.append(dX * dtp + dyp * dsk)
        dxs_ref[...] = jnp.concatenate(dxs, axis=1)
        db_ref[...] = dB
        dc_ref[...] = dC
        dadt = _dot_exact(jnp.where(cidx >= r, 1.0, 0.0), dcs_all)
        ddt_ref[...] = dadt * a_vec + dtx_all
        pg_ref[1:2, 0:128] += dd_row
        pg_ref[2:3, 0:128] += jnp.sum(dadt * dt_v, axis=0, keepdims=True) * a_vec

    rowb = lambda w, colf: pl.BlockSpec((CH, w), lambda b, g, c: (b * nch + nch - 1 - c, colf(g)))
    vec = pl.BlockSpec((1, 128), lambda b, g, c: (0, 0))
    kw = dict(
        grid=(B, SSD_GROUPS, nch),
        in_specs=[rowb(256, lambda g: g), rowb(256, lambda g: g), rowb(256, lambda g: 16 + g), rowb(256, lambda g: g),
                  rowb(128, lambda g: 8 + g), rowb(128, lambda g: 12 + g), rowb(128, lambda g: 0),
                  pl.BlockSpec((128, CH), lambda b, g, c: (0, b * nch + nch - 1 - c)),
                  pl.BlockSpec((None, None, None, 128, 256), lambda b, g, c: (b, g, nch - 1 - c, 0, 0)),
                  vec, vec, pl.BlockSpec((1, 256), lambda b, g, c: (0, g))],
        out_specs=[rowb(256, lambda g: g), rowb(256, lambda g: g), rowb(128, lambda g: g), rowb(128, lambda g: g),
                   rowb(128, lambda g: g),
                   pl.BlockSpec((None, None, 8, 256), lambda b, g, c: (b, g, 0, 0))],
        out_shape=[jax.ShapeDtypeStruct((R, 1024), _MXU), jax.ShapeDtypeStruct((R, 1024), F32),
                   jax.ShapeDtypeStruct((R, 512), F32), jax.ShapeDtypeStruct((R, 512), F32),
                   jax.ShapeDtypeStruct((R, 512), F32), jax.ShapeDtypeStruct((B, SSD_GROUPS, 8, 256), F32)],
        scratch_shapes=[pltpu.VMEM((128, 256), F32)])
    return _call(body, "ssd_bwd", ("arbitrary", "arbitrary", "arbitrary"), kw,
                 (dycat, ypre, u0, act, act, act, dt, dtt, hin, a_log, d_skip, norm_g), rider)


def _ssd_prep_bwd(dxs, dB, dC, ddt4, u0, udt, conv_w, conv_b, dt_bias, B, nch, rider=None):
    R = u0.shape[0]

    def body(dxs_ref, db_ref, dc_ref, ddt_ref, xs_ref, xsp_ref, bc_ref, bcp_ref, udt_ref, w0_ref, w1_ref, b0_ref, b1_ref,
             dtb_ref, dpre_ref, ddtr_ref, pgd_ref):
        c = pl.program_id(1)

        @pl.when(c == 0)
        def _():
            pgd_ref[...] = jnp.zeros_like(pgd_ref)

        keep = _real_rows(c)
        p0 = _conv_pre(xsp_ref[...], xs_ref[...], w0_ref, b0_ref, 4)
        p1 = _conv_pre(bcp_ref[...], bc_ref[...], w1_ref, b1_ref, 4)
        dpre_ref[:, :1024] = jnp.where(keep, dxs_ref[...] * _dsilu(p0), 0.0)
        dpre_ref[:, 1024:] = jnp.where(keep, jnp.concatenate([db_ref[...], dc_ref[...]], axis=1) * _dsilu(p1), 0.0)
        ddt = ddt_ref[:, 0:128] + ddt_ref[:, 128:256] + ddt_ref[:, 256:384] + ddt_ref[:, 384:512]
        ok = jnp.logical_and(keep, _lane_ids(1, 128) < SSD_HEADS)
        dr = jnp.where(ok, ddt * _sigmoid(udt_ref[...] + dtb_ref[...]), 0.0)
        ddtr_ref[...] = dr
        pgd_ref[0:1, :] += jnp.sum(dr, axis=0, keepdims=True)

    rw = lambda w: pl.BlockSpec((CH, w), lambda b, c: (b * nch + c, 0))
    row = lambda col: pl.BlockSpec((CH, 1024), lambda b, c: (b * nch + c, col))
    prev = lambda col: pl.BlockSpec((8, 1024), _prev8_map(nch, col))
    kw = dict(
        grid=(B, nch),
        in_specs=[rw(1024), rw(512), rw(512), rw(512), row(5), prev(5), row(6), prev(6), rw(128),
                  pl.BlockSpec((4, 1024), lambda b, c: (0, 0)), pl.BlockSpec((4, 1024), lambda b, c: (0, 1)),
                  pl.BlockSpec((1, 1024), lambda b, c: (0, 0)), pl.BlockSpec((1, 1024), lambda b, c: (0, 1)),
                  pl.BlockSpec((1, 128), lambda b, c: (0, 0))],
        out_specs=[rw(2048), rw(128), pl.BlockSpec((None, 8, 128), lambda b, c: (b, 0, 0))],
        out_shape=[jax.ShapeDtypeStruct((R, 2048), F32), jax.ShapeDtypeStruct((R, 128), F32),
                   jax.ShapeDtypeStruct((B, 8, 128), F32)])
    return _call(body, "ssd_prep_bwd", ("arbitrary", "arbitrary"), kw,
                 (dxs, dB, dC, ddt4, u0, u0, u0, u0, udt, conv_w, conv_w, conv_b, conv_b, dt_bias), rider)


def _conv_bwd(dpre, xin, xin_col, w, K, name, tc=1024):
    R, C = dpre.shape
    assert C % tc == 0 and xin_col % tc == 0
    nr = R // CH
    xoff = xin_col // tc

    def body(dp_ref, dpn_ref, x_ref, xp_ref, w_ref, din_ref, dw_ref):
        i = pl.program_id(1)

        @pl.when(i == 0)
        def _():
            dw_ref[...] = jnp.zeros_like(dw_ref)

        dp = dp_ref[...]
        nxt = dpn_ref[...] * (i < nr - 1).astype(F32)
        x = x_ref[...]
        xp = xp_ref[...]
        din = dp * w_ref[K - 1:K, :]
        dw_ref[K - 1:K, :] += jnp.sum(dp * x, axis=0, keepdims=True)
        dw_ref[7:8, :] += jnp.sum(dp, axis=0, keepdims=True)
        for s in range(1, K):
            din = din + _shift_up(dp, nxt, s) * w_ref[K - 1 - s:K - s, :]
            dw_ref[K - 1 - s:K - s, :] += jnp.sum(dp * _shift_down(xp, x, s), axis=0, keepdims=True)
        din_ref[...] = din.astype(din_ref.dtype)

    return pl.pallas_call(
        body, name=name, grid=(C // tc, nr),
        in_specs=[pl.BlockSpec((CH, tc), lambda j, i: (i, j)),
                  pl.BlockSpec((8, tc), lambda j, i: (jnp.minimum((i + 1) * (CH // 8), nr * (CH // 8) - 1), j)),
                  pl.BlockSpec((CH, tc), lambda j, i: (i, xoff + j)),
                  pl.BlockSpec((8, tc), lambda j, i: (jnp.maximum(i * (CH // 8) - 1, 0), xoff + j)),
                  pl.BlockSpec((K, tc), lambda j, i: (0, j))],
        out_specs=[pl.BlockSpec((CH, tc), lambda j, i: (i, j)),
                   pl.BlockSpec((8, tc), lambda j, i: (0, j))],
        out_shape=[jax.ShapeDtypeStruct((R, C), _MXU), jax.ShapeDtypeStruct((8, C), F32)],
        compiler_params=_cparams(("parallel", "arbitrary")),
    )(dpre, dpre, xin, xin, w)


_RET_LG = [float(v) for v in np.log1p(-np.exp2(-5.0 - np.arange(RET_HEADS, dtype=np.float32))).astype(np.float32)]
_RET_SCALE = RET_DK ** -0.5


def _rope_tables(nch):
    half = RET_DK // 2
    inv_freq = 1.0 / (10000.0 ** (jnp.arange(half, dtype=F32) / (half - 1)))
    pos = jnp.arange(nch * CH, dtype=F32) - PAD
    ang = pos[:, None] * inv_freq[None, :]
    return jnp.cos(ang), jnp.sin(ang)


def _rot(x, cos, sin):
    x1, x2 = x[:, :128], x[:, 128:]
    return jnp.concatenate([x1 * cos - x2 * sin, x1 * sin + x2 * cos], axis=1)


def _unrot(d, cos, sin):
    d1, d2 = d[:, :128], d[:, 128:]
    return jnp.concatenate([d1 * cos + d2 * sin, d2 * cos - d1 * sin], axis=1)


def _ret_decays(lg):
    r = _row_ids(CH, CH)
    cidx = _lane_ids(CH, CH)
    diff = (r - cidx).astype(F32)
    decay = jnp.where(r >= cidx, jnp.exp(lg * jnp.maximum(diff, 0.0)), 0.0)
    decay_t = jnp.where(cidx >= r, jnp.exp(lg * jnp.maximum(-diff, 0.0)), 0.0)
    idx = _row_ids(CH).astype(F32)
    zeta = jnp.exp(lg * (CH - 1.0 - idx))
    xi = jnp.exp(lg * (idx + 1.0))
    return decay, decay_t, zeta, xi


def _ret_fwd(u0, ycat, cos, sin, norm_g, B, nch, rider=None):
    R = u0.shape[0]

    def body(u_ref, cos_ref, sin_ref, ng_ref, ycat_in, out_ref, opre_ref, rin_ref, Rst):
        c = pl.program_id(1)

        @pl.when(c == 0)
        def _():
            Rst[...] = jnp.zeros_like(Rst)

        cos_v, sin_v = cos_ref[...], sin_ref[...]
        for h in range(RET_HEADS):
            lg = _RET_LG[h]
            cols = slice(256 * h, 256 * h + 256)
            decay, _, zeta, xi = _ret_decays(lg)
            qr = _rot(u_ref[:, cols], cos_v, sin_v)
            kr = _rot(u_ref[:, 1024 + 256 * h:1024 + 256 * h + 256], cos_v, sin_v) * _RET_SCALE
            v = u_ref[:, 2048 + 256 * h:2048 + 256 * h + 256]
            gate = u_ref[:, 3072 + 256 * h:3072 + 256 * h + 256]
            Rh = Rst[h]
            rin_ref[h] = Rh
            inner = _dot(_dot_nt(qr, kr) * decay, v)
            cross = _dot(qr, Rh) * xi
            Rst[h] = math.exp(CH * lg) * Rh + _dot((kr * zeta).T, v)
            o = inner + cross
            opre_ref[:, cols] = o
            oc = o - jnp.mean(o, axis=-1, keepdims=True)
            rr = lax.rsqrt(jnp.mean(oc * oc, axis=-1, keepdims=True) + EPS)
            out_ref[:, cols] = (_silu(gate) * (oc * rr * ng_ref[:, cols])).astype(out_ref.dtype)

    kw = dict(
        grid=(B, nch),
        in_specs=[pl.BlockSpec((CH, 4096), lambda b, c: (b * nch + c, 0)),
                  pl.BlockSpec((CH, 128), lambda b, c: (c, 0)), pl.BlockSpec((CH, 128), lambda b, c: (c, 0)),
                  pl.BlockSpec((1, 1024), lambda b, c: (0, 0)),
                  pl.BlockSpec(memory_space=pl.ANY)],
        out_specs=[pl.BlockSpec((CH, 1024), lambda b, c: (b * nch + c, 1)),
                   pl.BlockSpec((CH, 1024), lambda b, c: (b * nch + c, 0)),
                   pl.BlockSpec((None, None, RET_HEADS, 256, 256), lambda b, c: (b, c, 0, 0, 0))],
        out_shape=[jax.ShapeDtypeStruct(ycat.shape, ycat.dtype), jax.ShapeDtypeStruct((R, 1024), F32),
                   jax.ShapeDtypeStruct((B, nch, RET_HEADS, 256, 256), F32)],
        scratch_shapes=[pltpu.VMEM((RET_HEADS, 256, 256), F32)],
        input_output_aliases={4: 0})
    return _call(body, "ret_fwd", ("arbitrary", "arbitrary"), kw, (u0, cos, sin, norm_g, ycat), rider)


def _ret_bwd(dycat, u0, opre, rin, cos, sin, norm_g, B, nch, rider=None):
    R = u0.shape[0]

    def body(dy_ref, u_ref, opre_ref, rin_ref, cos_ref, sin_ref, ng_ref, du_ref, pg_ref, dR):
        @pl.when(pl.program_id(1) == 0)
        def _():
            dR[...] = jnp.zeros_like(dR)
            pg_ref[...] = jnp.zeros_like(pg_ref)

        cos_v, sin_v = cos_ref[...], sin_ref[...]
        for h in range(RET_HEADS):
            lg = _RET_LG[h]
            cols = slice(256 * h, 256 * h + 256)
            decay, decay_t, zeta, xi = _ret_decays(lg)
            qr = _rot(u_ref[:, cols], cos_v, sin_v)
            kr = _rot(u_ref[:, 1024 + 256 * h:1024 + 256 * h + 256], cos_v, sin_v) * _RET_SCALE
            v = u_ref[:, 2048 + 256 * h:2048 + 256 * h + 256]
            gate = u_ref[:, 3072 + 256 * h:3072 + 256 * h + 256]
            ng = ng_ref[:, cols]
            o = opre_ref[:, cols]
            oc = o - jnp.mean(o, axis=-1, keepdims=True)
            rr = lax.rsqrt(jnp.mean(oc * oc, axis=-1, keepdims=True) + EPS)
            ohat = oc * rr
            dout = dy_ref[:, cols]
            du_ref[:, 3072 + 256 * h:3072 + 256 * h + 256] = (dout * (ohat * ng) * _dsilu(gate)).astype(du_ref.dtype)
            don = dout * _silu(gate)
            pg_ref[0:1, cols] += jnp.sum(don * ohat, axis=0, keepdims=True)
            dohat = don * ng
            do = rr * (dohat - jnp.mean(dohat, axis=-1, keepdims=True)
                       - ohat * jnp.mean(dohat * ohat, axis=-1, keepdims=True))
            Rh = rin_ref[h]
            dRn = dR[h]
            sc_t = _dot_nt(kr, qr) * decay_t
            dv = _dot(sc_t, do) + _dot(kr * zeta, dRn)
            ds = _dot_nt(do, v) * decay
            ds_t = _dot_nt(v, do) * decay_t
            dox = do * xi
            dq = _dot(ds, kr) + _dot_nt(dox, Rh)
            dk = _dot(ds_t, qr) + zeta * _dot_nt(v, dRn)
            dR[h] = math.exp(CH * lg) * dRn + _dot(qr.T, dox)
            du_ref[:, cols] = _unrot(dq, cos_v, sin_v).astype(du_ref.dtype)
            du_ref[:, 1024 + 256 * h:1024 + 256 * h + 256] = (_unrot(dk, cos_v, sin_v) * _RET_SCALE).astype(du_ref.dtype)
            du_ref[:, 2048 + 256 * h:2048 + 256 * h + 256] = dv.astype(du_ref.dtype)

    rmap = lambda b, c: (b * nch + nch - 1 - c, 0)
    kw = dict(
        grid=(B, nch),
        in_specs=[pl.BlockSpec((CH, 1024), lambda b, c: (b * nch + nch - 1 - c, 1)),
                  pl.BlockSpec((CH, 4096), rmap), pl.BlockSpec((CH, 1024), rmap),
                  pl.BlockSpec((None, None, RET_HEADS, 256, 256), lambda b, c: (b, nch - 1 - c, 0, 0, 0)),
                  pl.BlockSpec((CH, 128), lambda b, c: (nch - 1 - c, 0)),
                  pl.BlockSpec((CH, 128), lambda b, c: (nch - 1 - c, 0)),
                  pl.BlockSpec((1, 1024), lambda b, c: (0, 0))],
        out_specs=[pl.BlockSpec((CH, 4096), rmap), pl.BlockSpec((None, 8, 1024), lambda b, c: (b, 0, 0))],
        out_shape=[jax.ShapeDtypeStruct((R, 4096), _MXU), jax.ShapeDtypeStruct((B, 8, 1024), F32)],
        scratch_shapes=[pltpu.VMEM((RET_HEADS, 256, 256), F32)])
    return _call(body, "ret_bwd", ("arbitrary", "arbitrary"), kw, (dycat, u0, opre, rin, cos, sin, norm_g), rider)


_SB_SCALE = SB_HD ** -0.5


_SB_NB = 4


def _sb_valid(qb, kb, live):
    qpos = qb * CH + jnp.bitwise_and(_row_ids(2 * CH, CH), CH - 1)
    kpos = kb * CH + _lane_ids(2 * CH, CH)
    first = PAD + (1 - live) * (1 << 24)
    return jnp.logical_and(kpos < qpos, kpos >= first)


_SB_DEAD = -100.0


def _sb_alive(acc):
    return (jnp.max(acc) > _SB_DEAD).astype(jnp.int32)


def _sb_softplus(z):
    return jnp.maximum(z, 0.0) + jnp.log(1.0 + jnp.exp(-jnp.abs(z)))


def _stack_heads(x):
    hm = _lane_ids(1, 128) < SB_HD
    return jnp.concatenate([jnp.where(hm, x, 0.0), jnp.where(hm, 0.0, x)], axis=0)


def _unstack_heads(x2):
    return jnp.where(_lane_ids(1, 128) < SB_HD, x2[:CH], x2[CH:])


def _sb_fwd(u1, B, nch, rider=None):
    R = u1.shape[0]
    Pn = nch * CH

    def body(q_ref, k_ref, v_ref, out_ref):
        qb = pl.program_id(2)
        q2 = _stack_heads(q_ref[...] * _SB_SCALE).astype(_MXU)
        mgt = (_row_ids(CH, CH) > _lane_ids(CH, CH)).astype(F32)

        def step(i, carry):
            out2, acc = carry
            blocks = []
            for t in range(_SB_NB):
                kb = qb - _SB_NB * i - t
                live = (kb >= 0).astype(jnp.int32)
                kbc = jnp.maximum(kb, 0)
                start = pl.multiple_of(kbc * CH, CH)
                valid = _sb_valid(qb, kbc, live)
                z = _dot_nt(q2, k_ref[pl.ds(start, CH), :])
                sp = _sb_softplus(z)
                lm = jnp.where(valid, -sp, 0.0)
                blocks.append((valid, z - sp, _dot_split(lm, mgt), jnp.sum(lm, axis=1, keepdims=True), start))
            for valid, ls, loc, rs, start in blocks:
                w = jnp.where(valid, jnp.exp(ls + loc + acc), 0.0)
                out2 = out2 + _dot(w, v_ref[pl.ds(start, CH), :])
                acc = acc + rs
            return out2, acc

        trips = (qb + _SB_NB) // _SB_NB

        def more(c):
            return jnp.logical_and(c[0] < trips, c[1] > 0)

        def trip(c):
            out2, acc = step(c[0], c[2:])
            return c[0] + 1, _sb_alive(acc), out2, acc

        init = (jnp.int32(0), jnp.int32(1), jnp.zeros((2 * CH, 128), F32), jnp.zeros((2 * CH, 1), F32))
        out2 = lax.while_loop(more, trip, init)[2]
        out_ref[...] = _unstack_heads(out2).astype(out_ref.dtype)

    qspec = lambda off: pl.BlockSpec((CH, 128), lambda b, hp, qb: (b * nch + qb, off + hp))
    kspec = lambda off: pl.BlockSpec((Pn, 128), lambda b, hp, qb: (b, off + hp))
    kw = dict(grid=(B, SB_HEADS // 2, nch), in_specs=[qspec(0), kspec(8), kspec(16)], out_specs=[qspec(0)],
              out_shape=[jax.ShapeDtypeStruct((R, 2048), _MXU)])
    return _call(body, "sb_fwd", ("arbitrary", "arbitrary", "arbitrary"), kw, (u1, u1, u1), rider)


def _sb_bwd(dycat, u1, B, nch, rider=None):
    R = u1.shape[0]
    Pn = nch * CH

    def body(q_ref, k_ref, v_ref, do_ref, dq_ref, dk_ref, dv_ref):
        qb = pl.program_id(2)

        @pl.when(qb == 0)
        def _():
            dk_ref[...] = jnp.zeros_like(dk_ref)
            dv_ref[...] = jnp.zeros_like(dv_ref)

        q2 = _stack_heads(q_ref[...] * _SB_SCALE)
        do2 = _stack_heads(do_ref[...])
        q2t, do2t = q2.T.astype(_MXU), do2.T.astype(_MXU)
        q2, do2 = q2.astype(_MXU), do2.astype(_MXU)
        rr = _row_ids(CH, CH)
        cc = _lane_ids(CH, CH)
        mle = (rr <= cc).astype(F32)
        mlt = (rr < cc).astype(F32)
        trips = (qb + _SB_NB) // _SB_NB

        def more(c):
            return jnp.logical_and(c[0] < trips, c[1] > 0)

        def scan(c):
            acc = c[2]
            for t in range(_SB_NB):
                kb = qb - _SB_NB * c[0] - t
                kbc = jnp.maximum(kb, 0)
                z = _dot_nt(q2, k_ref[pl.ds(pl.multiple_of(kbc * CH, CH), CH), :])
                lm = jnp.where(_sb_valid(qb, kbc, (kb >= 0).astype(jnp.int32)), -_sb_softplus(z), 0.0)
                acc = acc + jnp.sum(lm, axis=1, keepdims=True)
            return c[0] + 1, _sb_alive(acc), acc

        used, _, s2 = lax.while_loop(more, scan, (jnp.int32(0), jnp.int32(1), jnp.zeros((2 * CH, 1), F32)))
        base = qb + 1 - _SB_NB * used

        def step(i, carry):
            dq2, pacc, gacc = carry
            blocks = []
            for t in range(_SB_NB):
                kb = base + _SB_NB * i + t
                live = (kb >= 0).astype(jnp.int32)
                start = pl.multiple_of(jnp.maximum(kb, 0) * CH, CH)
                valid = _sb_valid(qb, jnp.maximum(kb, 0), live)
                z = _dot_nt(q2, k_ref[pl.ds(start, CH), :])
                sp = _sb_softplus(z)
                lm = jnp.where(valid, -sp, 0.0)
                blocks.append((valid, z - sp, _dot_split(lm, mle), jnp.sum(lm, axis=1, keepdims=True), start))
            stage = []
            for valid, ls, ploc, rs, start in blocks:
                w = jnp.where(valid, jnp.exp(ls + (s2 - (ploc + pacc))), 0.0)
                gg = _dot_nt(do2, v_ref[pl.ds(start, CH), :]) * w
                stage.append((valid, ls, w, gg, _dot_split(gg, mlt), jnp.sum(gg, axis=1, keepdims=True), start))
                pacc = pacc + rs
            for valid, ls, w, gg, gloc, gs, start in stage:
                sig = jnp.exp(ls)
                dz = jnp.where(valid, gg * (1.0 - sig) - (gloc + gacc) * sig, 0.0)
                dq2 = dq2 + _dot(dz, k_ref[pl.ds(start, CH), :])
                dk_ref[:, pl.ds(start, CH)] += _dot(q2t, dz)
                dv_ref[:, pl.ds(start, CH)] += _dot(do2t, w)
                gacc = gacc + gs
            return dq2, pacc, gacc

        zero = jnp.zeros((2 * CH, 1), F32)
        dq2 = lax.fori_loop(0, used, step, (jnp.zeros((2 * CH, 128), F32), zero, zero))[0]
        dq_ref[...] = (_unstack_heads(dq2) * _SB_SCALE).astype(dq_ref.dtype)

    qspec = lambda off: pl.BlockSpec((CH, 128), lambda b, hp, qb: (b * nch + qb, off + hp))
    kspec = lambda off: pl.BlockSpec((Pn, 128), lambda b, hp, qb: (b, off + hp))
    tspec = pl.BlockSpec((128, Pn), lambda b, hp, qb: (hp, b))
    full = jax.ShapeDtypeStruct((1024, R), F32)
    kw = dict(grid=(B, SB_HEADS // 2, nch), in_specs=[qspec(0), kspec(8), kspec(16), qspec(0)],
              out_specs=[qspec(0), tspec, tspec], out_shape=[jax.ShapeDtypeStruct((R, 1024), _MXU), full, full])
    return _call(body, "sb_bwd", ("arbitrary", "arbitrary", "arbitrary"), kw, (u1, u1, u1, dycat), rider)


def _neg_expm1(x):
    series = -(x * (1.0 + x * (0.5 + x * (1.0 / 6.0 + x * (1.0 / 24.0)))))
    return jnp.where(x > -0.05, series, 1.0 - jnp.exp(x))


def _lru_gates(x, wa_ref, ba_ref, wx_ref, bx_ref, lam_ref):
    rs, is_ = [], []
    for n in range(LRU_BLOCKS):
        xb = x[:, 128 * n:128 * n + 128]
        rs.append(_dot(xb, wa_ref[n]))
        is_.append(_dot(xb, wx_ref[n]))
    r = _sigmoid(jnp.concatenate(rs, axis=1) + ba_ref[...])
    i = _sigmoid(jnp.concatenate(is_, axis=1) + bx_ref[...])
    sp = _softplus(-lam_ref[...])
    la = -LRU_C * r * sp
    a = jnp.exp(la)
    mult = jnp.sqrt(jnp.maximum(_neg_expm1(2.0 * la), 0.0))
    return r, i, sp, a, mult


def _lru_fwd(u1, ycat, conv_w, conv_b, wa, ba, wx, bx, lam, B, nch):
    R = u1.shape[0]

    def body(x_ref, xp_ref, gate_ref, cw_ref, cb_ref, wa_ref, ba_ref, wx_ref, bx_ref, lam_ref, ycat_in,
             out_ref, hs_ref, hc):
        c = pl.program_id(1)

        @pl.when(c == 0)
        def _():
            hc[...] = jnp.zeros_like(hc)

        x = _conv_pre(xp_ref[...], x_ref[...], cw_ref, cb_ref, 4)
        r, i, sp, a, mult = _lru_gates(x, wa_ref, ba_ref, wx_ref, bx_ref, lam_ref)
        b = jnp.where(_real_rows(c), mult * (i * x), 0.0)
        rows = _row_ids(CH)
        s = 1
        while s < CH:
            a_s = jnp.where(rows >= s, pltpu.roll(a, s, axis=0), 1.0)
            b_s = jnp.where(rows >= s, pltpu.roll(b, s, axis=0), 0.0)
            b = a * b_s + b
            a = a * a_s
            s *= 2
        h = a * hc[0:1, :] + b
        hs_ref[...] = h
        hc[0:1, :] = hs_ref[CH - 1:CH, :]
        out_ref[...] = (h * _gelu(gate_ref[...])).astype(out_ref.dtype)

    row = lambda col: pl.BlockSpec((CH, 1024), lambda b, c: (b * nch + c, col))
    vec = pl.BlockSpec((1, 1024), lambda b, c: (0, 0))
    wsp = pl.BlockSpec((LRU_BLOCKS, 128, 128), lambda b, c: (0, 0, 0))
    return pl.pallas_call(
        body, name="lru_fwd", grid=(B, nch),
        in_specs=[row(4), pl.BlockSpec((8, 1024), _prev8_map(nch, 4)), row(3),
                  pl.BlockSpec((4, 1024), lambda b, c: (0, 0)), vec, wsp, vec, wsp, vec, vec,
                  pl.BlockSpec(memory_space=pl.ANY)],
        out_specs=[row(1), row(0)],
        out_shape=[jax.ShapeDtypeStruct(ycat.shape, ycat.dtype), jax.ShapeDtypeStruct((R, 1024), F32)],
        scratch_shapes=[pltpu.VMEM((8, 1024), F32)],
        input_output_aliases={10: 0},
        compiler_params=_cparams(("parallel", "arbitrary")),
    )(u1, u1, u1, conv_w, conv_b, wa, ba, wx, bx, lam, ycat)


def _lru_bwd(dycat, u1, hs, conv_w, conv_b, wa, ba, wx, bx, lam, B, nch):
    R = u1.shape[0]

    def body(dy_ref, x_ref, xp_ref, gate_ref, hs_ref, hsp_ref, cw_ref, cb_ref, wa_ref, ba_ref, wx_ref, bx_ref, lam_ref,
             dgate_ref, dxc_ref, pg_ref, dwa_ref, dwx_ref, lc):
        c = nch - 1 - pl.program_id(1)

        @pl.when(pl.program_id(1) == 0)
        def _():
            lc[...] = jnp.zeros_like(lc)
            pg_ref[...] = jnp.zeros_like(pg_ref)
            dwa_ref[...] = jnp.zeros_like(dwa_ref)
            dwx_ref[...] = jnp.zeros_like(dwx_ref)

        x = _conv_pre(xp_ref[...], x_ref[...], cw_ref, cb_ref, 4)
        r, i, sp, a, mult = _lru_gates(x, wa_ref, ba_ref, wx_ref, bx_ref, lam_ref)
        h = hs_ref[...]
        hprev = _shift_down(hsp_ref[...], h, 1)
        gate = gate_ref[...]
        dy = dy_ref[...]
        dgate_ref[...] = (dy * h * _dgelu(gate)).astype(dgate_ref.dtype)
        rows = _row_ids(CH)
        lam_t = dy * _gelu(gate) + jnp.where(rows == CH - 1, lc[0:1, :], 0.0)
        coef = jnp.where(rows < CH - 1, pltpu.roll(a, CH - 1, axis=0), 0.0)
        s = 1
        while s < CH:
            c_s = jnp.where(rows < CH - s, pltpu.roll(coef, CH - s, axis=0), 1.0)
            l_s = jnp.where(rows < CH - s, pltpu.roll(lam_t, CH - s, axis=0), 0.0)
            lam_t = coef * l_s + lam_t
            coef = coef * c_s
            s *= 2
        lc[0:1, :] = jnp.sum(jnp.where(rows == 0, a * lam_t, 0.0), axis=0, keepdims=True)
        db = jnp.where(_real_rows(c), lam_t, 0.0)
        da = db * hprev
        dmult = db * (i * x)
        di = db * mult * x
        dx = db * mult * i
        pos = mult > 0.0
        dla = da * a + jnp.where(pos, -dmult * (a * a) / jnp.where(pos, mult, 1.0), 0.0)
        dr = dla * (-LRU_C * sp)
        pg_ref[2:3, :] += jnp.sum(dla * (LRU_C * r) * _sigmoid(-lam_ref[...]), axis=0, keepdims=True)
        dpr = dr * r * (1.0 - r)
        dpi = di * i * (1.0 - i)
        pg_ref[0:1, :] += jnp.sum(dpr, axis=0, keepdims=True)
        pg_ref[1:2, :] += jnp.sum(dpi, axis=0, keepdims=True)
        dxs = []
        for n in range(LRU_BLOCKS):
            blk = slice(128 * n, 128 * n + 128)
            dxs.append(dx[:, blk] + _dot_nt(dpr[:, blk], wa_ref[n]) + _dot_nt(dpi[:, blk], wx_ref[n]))
            dwa_ref[n] += _dot_tn(x[:, blk], dpr[:, blk])
            dwx_ref[n] += _dot_tn(x[:, blk], dpi[:, blk])
        dxc_ref[...] = jnp.concatenate(dxs, axis=1)

    rmap = lambda col: (lambda b, c: (b * nch + nch - 1 - c, col))
    row = lambda col: pl.BlockSpec((CH, 1024), rmap(col))
    prev = lambda col: pl.BlockSpec(
        (8, 1024), lambda b, c: (jnp.maximum((b * nch + nch - 1 - c) * (CH // 8) - 1, 0), col))
    vec = pl.BlockSpec((1, 1024), lambda b, c: (0, 0))
    wsp = pl.BlockSpec((LRU_BLOCKS, 128, 128), lambda b, c: (0, 0, 0))
    full = jax.ShapeDtypeStruct((R, 1024), F32)
    return pl.pallas_call(
        body, name="lru_bwd", grid=(B, nch),
        in_specs=[row(1), row(4), prev(4), row(3), row(0), prev(0),
                  pl.BlockSpec((4, 1024), lambda b, c: (0, 0)), vec, wsp, vec, wsp, vec, vec],
        out_specs=[row(0), row(0), pl.BlockSpec((None, 8, 1024), lambda b, c: (b, 0, 0)),
                   pl.BlockSpec((None, LRU_BLOCKS, 128, 128), lambda b, c: (b, 0, 0, 0)),
                   pl.BlockSpec((None, LRU_BLOCKS, 128, 128), lambda b, c: (b, 0, 0, 0))],
        out_shape=[jax.ShapeDtypeStruct((R, 1024), _MXU), full, jax.ShapeDtypeStruct((B, 8, 1024), F32),
                   jax.ShapeDtypeStruct((B, LRU_BLOCKS, 128, 128), F32),
                   jax.ShapeDtypeStruct((B, LRU_BLOCKS, 128, 128), F32)],
        scratch_shapes=[pltpu.VMEM((8, 1024), F32)],
        compiler_params=_cparams(("parallel", "arbitrary")),
    )(dycat, u1, u1, u1, hs, hs, conv_w, conv_b, wa, ba, wx, bx, lam)


_FFN_TC = FFN // 2


def _ffn_specs(nch):
    nt = FFN // _FFN_TC
    row = lambda off: pl.BlockSpec((CH, _FFN_TC), lambda b, c, j: (b * nch + c, off + j))
    prev = lambda off: pl.BlockSpec(
        (8, _FFN_TC), lambda b, c, j: (jnp.maximum((b * nch + c) * (CH // 8) - 1, 0), off + j))
    wsp = lambda off: pl.BlockSpec((3, _FFN_TC), lambda b, c, j: (0, off + j))
    bsp = lambda off: pl.BlockSpec((1, _FFN_TC), lambda b, c, j: (0, off + j))
    return nt, row, [row(0), prev(0), row(nt), prev(nt), wsp(0), wsp(nt), bsp(0), bsp(nt)]


def _ffn_act_fwd(uf, conv_w, conv_b, B, nch, rider=None):
    R = uf.shape[0]
    nt, row, specs = _ffn_specs(nch)

    def body(g_ref, gp_ref, u_ref, up_ref, wg_ref, wu_ref, bg_ref, bu_ref, o_ref):
        cg = _conv_pre(gp_ref[...], g_ref[...], wg_ref, bg_ref, 3)
        cu = _conv_pre(up_ref[...], u_ref[...], wu_ref, bu_ref, 3)
        o_ref[...] = jnp.where(_real_rows(pl.program_id(1)), _silu(cg) * cu, 0.0).astype(o_ref.dtype)

    kw = dict(grid=(B, nch, nt), in_specs=specs, out_specs=[row(0)],
              out_shape=[jax.ShapeDtypeStruct((R, FFN), _MXU)])
    return _call(body, "ffn_act_fwd", ("arbitrary", "arbitrary", "arbitrary"), kw,
                 (uf, uf, uf, uf, conv_w, conv_w, conv_b, conv_b), rider)


def _ffn_act_bwd(da, uf, conv_w, conv_b, nch, name, rider=None):
    R = uf.shape[0]
    nt = FFN // _FFN_TC
    nr = R // CH
    K = 3

    def body(da_ref, dan_ref, g_ref, gp_ref, gn_ref, u_ref, up_ref, un_ref, wg_ref, wu_ref, bg_ref, bu_ref,
             dug_ref, duu_ref, dwg_ref, dwu_ref):
        i = pl.program_id(1)

        @pl.when(i == 0)
        def _():
            dwg_ref[...] = jnp.zeros_like(dwg_ref)
            dwu_ref[...] = jnp.zeros_like(dwu_ref)

        c = i % nch
        ext = CH + 8
        rows = _row_ids(ext)
        follows = (c < nch - 1).astype(jnp.int32)
        keep = jnp.logical_and(c * CH + rows >= PAD, rows < CH + 8 * follows)
        dav = jnp.where(keep, jnp.concatenate([da_ref[...], dan_ref[...]], axis=0), 0.0)

        def conv_ext(x_ref, xp_ref, xn_ref, w_ref, b_ref):
            cat = jnp.concatenate([xp_ref[...], x_ref[...], xn_ref[...]], axis=0)
            shifted = [cat[8:]] + [pltpu.roll(cat, s, axis=0)[8:] for s in range(1, K)]
            acc = shifted[0] * w_ref[K - 1:K, :] + b_ref[...]
            for s in range(1, K):
                acc = acc + shifted[s] * w_ref[K - 1 - s:K - s, :]
            return acc, shifted

        cg, gsh = conv_ext(g_ref, gp_ref, gn_ref, wg_ref, bg_ref)
        cu, ush = conv_ext(u_ref, up_ref, un_ref, wu_ref, bu_ref)
        sg = _sigmoid(cg)
        dcg = dav * cu * (sg * (1.0 + cg * (1.0 - sg)))
        dcu = dav * (cg * sg)
        for dc, xsh, w_ref, din_ref, dw_ref in ((dcg, gsh, wg_ref, dug_ref, dwg_ref), (dcu, ush, wu_ref, duu_ref, dwu_ref)):
            dp = dc[:CH]
            din = dp * w_ref[K - 1:K, :]
            dw_ref[7:8, :] += jnp.sum(dp, axis=0, keepdims=True)
            dw_ref[K - 1:K, :] += jnp.sum(dp * xsh[0][:CH], axis=0, keepdims=True)
            for s in range(1, K):
                din = din + pltpu.roll(dc, ext - s, axis=0)[:CH] * w_ref[K - 1 - s:K - s, :]
                dw_ref[K - 1 - s:K - s, :] += jnp.sum(dp * xsh[s][:CH], axis=0, keepdims=True)
            din_ref[...] = din.astype(din_ref.dtype)

    row = lambda off: pl.BlockSpec((CH, _FFN_TC), lambda j, i: (i, off + j))
    prev = lambda off: pl.BlockSpec((8, _FFN_TC), lambda j, i: (jnp.maximum(i * (CH // 8) - 1, 0), off + j))
    nxt = lambda off: pl.BlockSpec(
        (8, _FFN_TC), lambda j, i: (jnp.minimum((i + 1) * (CH // 8), nr * (CH // 8) - 1), off + j))
    wsp = lambda off: pl.BlockSpec((K, _FFN_TC), lambda j, i: (0, off + j))
    bsp = lambda off: pl.BlockSpec((1, _FFN_TC), lambda j, i: (0, off + j))
    acc = pl.BlockSpec((8, _FFN_TC), lambda j, i: (0, j))
    half = jax.ShapeDtypeStruct((R, FFN), _MXU)
    dwsh = jax.ShapeDtypeStruct((8, FFN), F32)
    kw = dict(
        grid=(nt, nr),
        in_specs=[row(0), nxt(0), row(0), prev(0), nxt(0), row(nt), prev(nt), nxt(nt), wsp(0), wsp(nt), bsp(0), bsp(nt)],
        out_specs=[row(0), row(0), acc, acc],
        out_shape=[half, half, dwsh, dwsh])
    return _call(body, name, ("arbitrary", "arbitrary"), kw,
                 (da, da, uf, uf, uf, uf, uf, uf, conv_w, conv_w, conv_b, conv_b), rider)


def _head(h, g, target, B, nch):
    R = h.shape[0]

    def body(h_ref, g_ref, t_ref, dh_ref, loss_ref, dg_ref):
        c = pl.program_id(1)

        @pl.when(c == 0)
        def _():
            dh_ref[...] = jnp.zeros_like(dh_ref)
            loss_ref[...] = jnp.zeros_like(loss_ref)
            dg_ref[...] = jnp.zeros_like(dg_ref)

        @pl.when(c > 0)
        def _():
            x = h_ref[...]
            gv = g_ref[...]
            r = lax.rsqrt(jnp.mean(x * x, axis=-1, keepdims=True) + EPS)
            xhat = x * r
            e = xhat * gv - t_ref[...]
            loss_ref[...] += 0.5 * jnp.sum(jnp.mean(e * e, axis=-1, keepdims=True), axis=0, keepdims=True)
            dy = e * (1.0 / D)
            dg_ref[0:1, :] += jnp.sum(dy * xhat, axis=0, keepdims=True)
            dx = dy * gv
            dh_ref[...] = r * (dx - xhat * jnp.mean(dx * xhat, axis=-1, keepdims=True))

    row = pl.BlockSpec((CH, D), lambda b, c: (b * nch + c, 0))
    return pl.pallas_call(
        body, name="head", grid=(B, nch),
        in_specs=[row, pl.BlockSpec((1, D), lambda b, c: (0, 0)),
                  pl.BlockSpec((CH, D), lambda b, c: (b * (nch - 1) + jnp.maximum(c - 1, 0), 0))],
        out_specs=[row, pl.BlockSpec((None, 8, 128), lambda b, c: (b, 0, 0)),
                   pl.BlockSpec((None, 8, D), lambda b, c: (b, 0, 0))],
        out_shape=[jax.ShapeDtypeStruct((R, D), F32), jax.ShapeDtypeStruct((B, 8, 128), F32),
                   jax.ShapeDtypeStruct((B, 8, D), F32)],
        compiler_params=_cparams(("parallel", "arbitrary")),
    )(h, g, target)


ADAM_LR = 0.001
ADAM_B1 = 0.9
ADAM_B2 = 0.999
ADAM_EPS = 1e-08
ADAM_WD = 0.01
ADAM_STEP = 10


def _adamw(w, g, m, v, name):
    Rr, C = w.shape
    tr = _tile(Rr, (256, 64))

    def body(w_ref, g_ref, m_ref, v_ref, d_ref, nm_ref, nv_ref):
        gv = g_ref[...]
        nm = ADAM_B1 * m_ref[...] + (1.0 - ADAM_B1) * gv
        nv = ADAM_B2 * v_ref[...] + (1.0 - ADAM_B2) * (gv * gv)
        m_hat = nm / (1.0 - ADAM_B1 ** ADAM_STEP)
        v_hat = nv / (1.0 - ADAM_B2 ** ADAM_STEP)
        d_ref[...] = -ADAM_LR * (m_hat / (jnp.sqrt(v_hat) + ADAM_EPS) + ADAM_WD * w_ref[...])
        nm_ref[...] = nm
        nv_ref[...] = nv

    spec = pl.BlockSpec((tr, C), lambda i: (i, 0))
    sh = jax.ShapeDtypeStruct((Rr, C), F32)
    return pl.pallas_call(
        body, name=name, grid=(Rr // tr,),
        in_specs=[spec] * 4, out_specs=[spec] * 3, out_shape=[sh] * 3,
        compiler_params=_cparams(("parallel",)),
    )(w, g, m, v)


_MESH = pl.DeviceIdType.MESH
_ANY = pl.BlockSpec(memory_space=pl.ANY)


def _place():
    x, y, c = lax.axis_index("x"), lax.axis_index("y"), lax.axis_index("c")
    chips = [(1 - x, y), (x, 1 - y), (1 - x, 1 - y)]
    return x, y, c, chips


def _rcopy(src, dst, ssem, rsem, dev):
    return pltpu.make_async_remote_copy(src_ref=src, dst_ref=dst, send_sem=ssem, recv_sem=rsem,
                                        device_id=dev, device_id_type=_MESH)


def _with_riders(body, kw, kind, riders):
    n_in, n_out, n_scr = len(kw["in_specs"]), len(kw["out_specs"]), len(kw.get("scratch_shapes", []))
    grid = kw["grid"]
    nr = len(riders)
    nsem = 4 if kind == "gather" else 2

    def new_body(*refs):
        ins, srcs = refs[:n_in], refs[n_in:n_in + nr]
        outs, dsts = refs[n_in + nr:n_in + nr + n_out], refs[n_in + nr + n_out:n_in + 2 * nr + n_out]
        scr = refs[n_in + 2 * nr + n_out:n_in + 2 * nr + n_out + n_scr]
        sems = refs[n_in + 2 * nr + n_out + n_scr:]
        first = last = None
        for axis, size in enumerate(grid):
            i = pl.program_id(axis)
            first = (i == 0) if first is None else jnp.logical_and(first, i == 0)
            last = (i == size - 1) if last is None else jnp.logical_and(last, i == size - 1)
        x, y, c, chips = _place()
        k = 2 * x + y
        sib = (x, y, 1 - c)
        ssem, rsem = sems[:2]
        sends = []
        for a in range(nr):
            for j, (cx, cy) in enumerate(chips):
                if kind == "gather":
                    src, dst = srcs[a].at[c], dsts[a].at[k, c]
                else:
                    src, dst = srcs[a].at[2 * cx + cy], dsts[a].at[k]
                sends.append(_rcopy(src, dst, ssem.at[3 * a + j], rsem.at[3 * a + j], (cx, cy, c)))

        @pl.when(first)
        def _():
            for cp in sends:
                cp.start()

        body(*ins, *outs, *scr)

        @pl.when(last)
        def _():
            passed = []
            for a in range(nr):
                for j, (cx, cy) in enumerate(chips):
                    got = dsts[a].at[2 * cx + cy, c] if kind == "gather" else dsts[a].at[2 * cx + cy]
                    _rcopy(got, got, ssem.at[3 * a + j], rsem.at[3 * a + j], (cx, cy, c)).wait_recv()
                    if kind == "gather":
                        fw = _rcopy(got, got, sems[2].at[3 * a + j], sems[3].at[3 * a + j], sib)
                        fw.start()
                        passed.append(fw)
            if kind == "gather":
                for a in range(nr):
                    for j, (cx, cy) in enumerate(chips):
                        got = dsts[a].at[2 * cx + cy, 1 - c]
                        _rcopy(got, got, sems[2].at[3 * a + j], sems[3].at[3 * a + j], sib).wait_recv()
            for cp in sends + passed:
                cp.wait_send()

    kw = dict(kw)
    kw["in_specs"] = list(kw["in_specs"]) + [_ANY] * nr
    kw["out_specs"] = list(kw["out_specs"]) + [_ANY] * nr
    kw["out_shape"] = list(kw["out_shape"]) + [
        jax.ShapeDtypeStruct(((4,) + r.shape) if kind == "gather" else r.shape, r.dtype) for r in riders]
    kw["scratch_shapes"] = list(kw.get("scratch_shapes", [])) + [pltpu.SemaphoreType.DMA((3 * nr,))] * nsem
    return new_body, kw


def _call(body, name, sem, kw, args, rider=None):
    if rider is not None:
        body, kw = _with_riders(body, kw, *rider)
        args = tuple(args) + tuple(rider[1])
    return pl.pallas_call(body, name=name, compiler_params=_cparams(sem), **kw)(*args)


def _fill_own(result, own, chip):
    return lax.dynamic_update_index_in_dim(result, own, chip, 0)


def _gather_shards(bigs, small):
    nb = len(bigs)

    def body(*refs):
        ins, outs = refs[:nb + 1], refs[nb + 1:2 * nb + 2]
        ssem, rsem, fssem, frsem = refs[2 * nb + 2:]
        x, y, c, chips = _place()
        k = 2 * x + y
        sib = (x, y, 1 - c)

        def part(a, slot, hc):
            return outs[a].at[slot] if a == nb else outs[a].at[slot, hc]

        first = []
        for a in range(nb + 1):
            src = ins[a] if a == nb else ins[a].at[c]
            for j, (cx, cy) in enumerate(chips):
                first.append(_rcopy(src, part(a, k, c), ssem.at[3 * a + j], rsem.at[3 * a + j], (cx, cy, c)))
        for cp in first:
            cp.start()
        passed = []
        for a in range(nb + 1):
            for j, (cx, cy) in enumerate(chips):
                got = part(a, 2 * cx + cy, c)
                _rcopy(got, got, ssem.at[3 * a + j], rsem.at[3 * a + j], (cx, cy, c)).wait_recv()
                if a < nb:
                    fw = _rcopy(got, got, fssem.at[3 * a + j], frsem.at[3 * a + j], sib)
                    fw.start()
                    passed.append(fw)
        for a in range(nb):
            for j, (cx, cy) in enumerate(chips):
                got = part(a, 2 * cx + cy, 1 - c)
                _rcopy(got, got, fssem.at[3 * a + j], frsem.at[3 * a + j], sib).wait_recv()
        for cp in first + passed:
            cp.wait_send()

    arrs = list(bigs) + [small]
    n = 3 * (nb + 1)
    return pl.pallas_call(
        body, name="gather_shards",
        in_specs=[_ANY] * (nb + 1), out_specs=[_ANY] * (nb + 1),
        out_shape=[jax.ShapeDtypeStruct((4,) + a.shape, a.dtype) for a in arrs],
        scratch_shapes=[pltpu.SemaphoreType.DMA((n,)), pltpu.SemaphoreType.DMA((n,)),
                        pltpu.SemaphoreType.DMA((n,)), pltpu.SemaphoreType.DMA((n,))],
    )(*arrs)


def _swap_halves(grads, name):
    na = len(grads)
    halves = [g.shape[1] // 2 for g in grads]

    def body(*refs):
        ins, outs = refs[:na], refs[na:2 * na]
        ssem, rsem = refs[2 * na:]
        x, y, c, _ = _place()
        sib = (x, y, 1 - c)
        cps = [_rcopy(ins[a].at[:, pl.ds((1 - c) * halves[a], halves[a]), :], outs[a], ssem.at[a], rsem.at[a], sib)
               for a in range(na)]
        for cp in cps:
            cp.start()
        for cp in cps:
            cp.wait()

    return pl.pallas_call(
        body, name=name,
        in_specs=[_ANY] * na, out_specs=[_ANY] * na,
        out_shape=[jax.ShapeDtypeStruct((4, g.shape[1] // 2, g.shape[2]), g.dtype) for g in grads],
        scratch_shapes=[pltpu.SemaphoreType.DMA((na,)), pltpu.SemaphoreType.DMA((na,))],
    )(*grads)


def _sum_rows(rh):
    return rh if rh <= 512 else _tile(rh, (512, 256, 128, 64, 32))


def _chip_sum(grad, recv, core, name):
    _, r, cdim = grad.shape
    rh = r // 2
    tr = _sum_rows(rh)
    nblk = rh // tr

    def body(core_ref, g_ref, r_ref, o_ref):
        o_ref[...] = (g_ref[...] + r_ref[...]).astype(o_ref.dtype)

    return pl.pallas_call(
        body, name=name,
        grid_spec=pltpu.PrefetchScalarGridSpec(
            num_scalar_prefetch=1, grid=(4, nblk),
            in_specs=[pl.BlockSpec((None, tr, cdim), lambda s, i, cr: (s, cr[0] * nblk + i, 0)),
                      pl.BlockSpec((None, tr, cdim), lambda s, i, cr: (s, i, 0))],
            out_specs=pl.BlockSpec((None, tr, cdim), lambda s, i, cr: (s, i, 0))),
        out_shape=jax.ShapeDtypeStruct((4, rh, cdim), BF16),
        compiler_params=_cparams(("parallel", "parallel")),
    )(core, grad, recv)


def _scatter_sums(sums):
    na = len(sums)

    def body(*refs):
        ins, outs = refs[:na], refs[na:2 * na]
        ssem, rsem, lsem = refs[2 * na:]
        x, y, c, chips = _place()
        k = 2 * x + y
        local = [pltpu.make_async_copy(ins[a].at[k], outs[a].at[k], lsem.at[a]) for a in range(na)]
        for cp in local:
            cp.start()
        cps = []
        for a in range(na):
            for j, (cx, cy) in enumerate(chips):
                cps.append(_rcopy(ins[a].at[2 * cx + cy], outs[a].at[k], ssem.at[3 * a + j], rsem.at[3 * a + j],
                                  (cx, cy, c)))
        for cp in cps:
            cp.start()
        for a in range(na):
            for j, (cx, cy) in enumerate(chips):
                got = outs[a].at[2 * cx + cy]
                _rcopy(got, got, ssem.at[3 * a + j], rsem.at[3 * a + j], (cx, cy, c)).wait_recv()
        for cp in cps:
            cp.wait_send()
        for cp in local:
            cp.wait()

    return pl.pallas_call(
        body, name="scatter_sums",
        in_specs=[_ANY] * na, out_specs=[_ANY] * na,
        out_shape=[jax.ShapeDtypeStruct(s.shape, s.dtype) for s in sums],
        scratch_shapes=[pltpu.SemaphoreType.DMA((3 * na,)), pltpu.SemaphoreType.DMA((3 * na,)),
                        pltpu.SemaphoreType.DMA((na,))],
    )(*sums)


def _sum_chips(parts, name):
    _, rh, cdim = parts.shape
    tr = _sum_rows(rh)

    def body(p_ref, o_ref):
        acc = p_ref[0].astype(F32)
        for j in range(1, 4):
            acc = acc + p_ref[j].astype(F32)
        o_ref[...] = acc

    return pl.pallas_call(
        body, name=name, grid=(rh // tr,),
        in_specs=[pl.BlockSpec((4, tr, cdim), lambda i: (0, i, 0))],
        out_specs=pl.BlockSpec((tr, cdim), lambda i: (i, 0)),
        out_shape=jax.ShapeDtypeStruct((rh, cdim), F32),
        compiler_params=_cparams(("parallel",)),
    )(parts)


def _join_halves(reds):
    na = len(reds)

    def body(*refs):
        ins, outs = refs[:na], refs[na:2 * na]
        ssem, rsem = refs[2 * na:]
        x, y, c, _ = _place()
        cps = [_rcopy(ins[a], outs[a], ssem.at[a], rsem.at[a], (x, y, 1 - c)) for a in range(na)]
        for cp in cps:
            cp.start()
        for cp in cps:
            cp.wait()

    return pl.pallas_call(
        body, name="join_halves",
        in_specs=[_ANY] * na, out_specs=[_ANY] * na,
        out_shape=[jax.ShapeDtypeStruct(r.shape, r.dtype) for r in reds],
        scratch_shapes=[pltpu.SemaphoreType.DMA((na,)), pltpu.SemaphoreType.DMA((na,))],
    )(*reds)


def _allreduce_small(buf):
    n = buf.shape[0]

    def body(in_ref, out_ref, recv, ssem, rsem):
        x, y, c, _ = _place()
        peers = [(x, y, 1 - c), (1 - x, y, c), (x, 1 - y, c)]
        out_ref[...] = in_ref[...]
        for r, peer in enumerate(peers):
            cp = _rcopy(out_ref, recv.at[r], ssem.at[r], rsem.at[r], peer)
            cp.start()
            cp.wait()
            out_ref[...] = out_ref[...] + recv[r]

    vm = pl.BlockSpec(memory_space=pltpu.VMEM)
    return pl.pallas_call(
        body, name="allreduce_small",
        in_specs=[vm], out_specs=vm,
        out_shape=jax.ShapeDtypeStruct(buf.shape, F32),
        scratch_shapes=[pltpu.VMEM((3, n, 128), F32), pltpu.SemaphoreType.DMA((3,)), pltpu.SemaphoreType.DMA((3,))],
        compiler_params=pltpu.CompilerParams(vmem_limit_bytes=VMEM_LIMIT),
    )(buf)


_W_NAMES = ['meta_tokens', 'l0_mix_norm', 'l0_w_in', 'l0_ssd_conv_w', 'l0_ssd_conv_b', 'l0_ssd_dt_bias', 'l0_ssd_a_log',
            'l0_ssd_d', 'l0_ssd_norm', 'l0_ret_norm', 'l0_w_out', 'l0_ffn_norm', 'l0_ffn_w_in', 'l0_ffn_conv_w',
            'l0_ffn_conv_b', 'l0_ffn_w_out', 'l1_mix_norm', 'l1_w_in', 'l1_lru_conv_w', 'l1_lru_conv_b', 'l1_lru_wa',
            'l1_lru_ba', 'l1_lru_wx', 'l1_lru_bx', 'l1_lru_lambda', 'l1_w_out', 'l1_ffn_norm', 'l1_ffn_w_in',
            'l1_ffn_conv_w', 'l1_ffn_conv_b', 'l1_ffn_w_out', 'final_norm']
_IN_NAMES = ['x'] + _W_NAMES + ['loss_target'] + ['m_' + n for n in _W_NAMES] + ['v_' + n for n in _W_NAMES]
_BIG = ['l0_w_in', 'l0_w_out', 'l0_ffn_w_in', 'l0_ffn_w_out', 'l1_w_in', 'l1_w_out', 'l1_ffn_w_in', 'l1_ffn_w_out']
_BIG_COLS = ('l0_w_in', 'l0_ffn_w_in', 'l1_w_in', 'l1_ffn_w_in')
_SMALL_SHARDED = ['meta_tokens', 'l0_ssd_conv_w', 'l0_ffn_conv_w', 'l1_lru_conv_w', 'l1_ffn_conv_w']
_SMALL = [n for n in _W_NAMES if n not in _BIG]


def _pack(arrs):
    flat = []
    for a in arrs:
        v = a.reshape(-1).astype(F32)
        flat.append(jnp.pad(v, (0, (-v.shape[0]) % 128)))
    v = jnp.concatenate(flat)
    v = jnp.pad(v, (0, (-v.shape[0]) % 1024))
    return v.reshape(-1, 128)


def _unpack(buf, shapes):
    out, row = [], 0
    for sh in shapes:
        n = int(np.prod(sh))
        rows = -(-n // 128)
        out.append(buf[row:row + rows].reshape(-1)[:n].reshape(sh))
        row += rows
    return out


def kernel(x, meta_tokens, l0_mix_norm, l0_w_in, l0_ssd_conv_w, l0_ssd_conv_b, l0_ssd_dt_bias, l0_ssd_a_log, l0_ssd_d, l0_ssd_norm, l0_ret_norm, l0_w_out, l0_ffn_norm, l0_ffn_w_in, l0_ffn_conv_w, l0_ffn_conv_b, l0_ffn_w_out, l1_mix_norm, l1_w_in, l1_lru_conv_w, l1_lru_conv_b, l1_lru_wa, l1_lru_ba, l1_lru_wx, l1_lru_bx, l1_lru_lambda, l1_w_out, l1_ffn_norm, l1_ffn_w_in, l1_ffn_conv_w, l1_ffn_conv_b, l1_ffn_w_out, final_norm, loss_target, m_meta_tokens, m_l0_mix_norm, m_l0_w_in, m_l0_ssd_conv_w, m_l0_ssd_conv_b, m_l0_ssd_dt_bias, m_l0_ssd_a_log, m_l0_ssd_d, m_l0_ssd_norm, m_l0_ret_norm, m_l0_w_out, m_l0_ffn_norm, m_l0_ffn_w_in, m_l0_ffn_conv_w, m_l0_ffn_conv_b, m_l0_ffn_w_out, m_l1_mix_norm, m_l1_w_in, m_l1_lru_conv_w, m_l1_lru_conv_b, m_l1_lru_wa, m_l1_lru_ba, m_l1_lru_wx, m_l1_lru_bx, m_l1_lru_lambda, m_l1_w_out, m_l1_ffn_norm, m_l1_ffn_w_in, m_l1_ffn_conv_w, m_l1_ffn_conv_b, m_l1_ffn_w_out, m_final_norm, v_meta_tokens, v_l0_mix_norm, v_l0_w_in, v_l0_ssd_conv_w, v_l0_ssd_conv_b, v_l0_ssd_dt_bias, v_l0_ssd_a_log, v_l0_ssd_d, v_l0_ssd_norm, v_l0_ret_norm, v_l0_w_out, v_l0_ffn_norm, v_l0_ffn_w_in, v_l0_ffn_conv_w, v_l0_ffn_conv_b, v_l0_ffn_w_out, v_l1_mix_norm, v_l1_w_in, v_l1_lru_conv_w, v_l1_lru_conv_b, v_l1_lru_wa, v_l1_lru_ba, v_l1_lru_wx, v_l1_lru_bx, v_l1_lru_lambda, v_l1_w_out, v_l1_ffn_norm, v_l1_ffn_w_in, v_l1_ffn_conv_w, v_l1_ffn_conv_b, v_l1_ffn_w_out, v_final_norm):
    args = (x, meta_tokens, l0_mix_norm, l0_w_in, l0_ssd_conv_w, l0_ssd_conv_b, l0_ssd_dt_bias, l0_ssd_a_log, l0_ssd_d, l0_ssd_norm, l0_ret_norm, l0_w_out, l0_ffn_norm, l0_ffn_w_in, l0_ffn_conv_w, l0_ffn_conv_b, l0_ffn_w_out, l1_mix_norm, l1_w_in, l1_lru_conv_w, l1_lru_conv_b, l1_lru_wa, l1_lru_ba, l1_lru_wx, l1_lru_bx, l1_lru_lambda, l1_w_out, l1_ffn_norm, l1_ffn_w_in, l1_ffn_conv_w, l1_ffn_conv_b, l1_ffn_w_out, final_norm, loss_target, m_meta_tokens, m_l0_mix_norm, m_l0_w_in, m_l0_ssd_conv_w, m_l0_ssd_conv_b, m_l0_ssd_dt_bias, m_l0_ssd_a_log, m_l0_ssd_d, m_l0_ssd_norm, m_l0_ret_norm, m_l0_w_out, m_l0_ffn_norm, m_l0_ffn_w_in, m_l0_ffn_conv_w, m_l0_ffn_conv_b, m_l0_ffn_w_out, m_l1_mix_norm, m_l1_w_in, m_l1_lru_conv_w, m_l1_lru_conv_b, m_l1_lru_wa, m_l1_lru_ba, m_l1_lru_wx, m_l1_lru_bx, m_l1_lru_lambda, m_l1_w_out, m_l1_ffn_norm, m_l1_ffn_w_in, m_l1_ffn_conv_w, m_l1_ffn_conv_b, m_l1_ffn_w_out, m_final_norm, v_meta_tokens, v_l0_mix_norm, v_l0_w_in, v_l0_ssd_conv_w, v_l0_ssd_conv_b, v_l0_ssd_dt_bias, v_l0_ssd_a_log, v_l0_ssd_d, v_l0_ssd_norm, v_l0_ret_norm, v_l0_w_out, v_l0_ffn_norm, v_l0_ffn_w_in, v_l0_ffn_conv_w, v_l0_ffn_conv_b, v_l0_ffn_w_out, v_l1_mix_norm, v_l1_w_in, v_l1_lru_conv_w, v_l1_lru_conv_b, v_l1_lru_wa, v_l1_lru_ba, v_l1_lru_wx, v_l1_lru_bx, v_l1_lru_lambda, v_l1_w_out, v_l1_ffn_norm, v_l1_ffn_w_in, v_l1_ffn_conv_w, v_l1_ffn_conv_b, v_l1_ffn_w_out, v_final_norm)
    p = dict(zip(_IN_NAMES, args))
    B, seq, _ = x.shape
    nch = (seq + CH) // CH
    Pn = nch * CH
    R = B * Pn
    chip = 2 * lax.axis_index("x") + lax.axis_index("y")
    row2 = lambda v: v.reshape(1, -1)
    pad128 = lambda v: jnp.pad(v, (0, 128 - v.shape[0])).reshape(1, 128)

    small_shapes = [p[n].shape for n in _SMALL_SHARDED]
    halved = lambda w: w.astype(_MXU).reshape(2, w.shape[0] // 2, w.shape[1])
    mine = {n: halved(p[n]) for n in _BIG}
    mine_small = _pack([p[n] for n in _SMALL_SHARDED])
    W = {}

    def set_weight(n, g):
        g = _fill_own(g, mine[n], chip)
        g = g.reshape(4, -1, g.shape[3])
        W[n] = jnp.concatenate([g[k] for k in range(4)], axis=1) if n in _BIG_COLS else g.reshape(-1, g.shape[2])

    def gather_on(*names):
        return ("gather", [mine[n] for n in names])

    def take_weights(names, got):
        for n, g in zip(names, got):
            set_weight(n, g)

    gathered = _gather_shards([mine['l0_w_in']], mine_small)
    set_weight('l0_w_in', gathered[0])
    g_small = _fill_own(gathered[-1], mine_small, chip)
    per_chip = [_unpack(g_small[k], small_shapes) for k in range(4)]
    for i, n in enumerate(_SMALL_SHARDED):
        W[n] = jnp.concatenate([per_chip[k][i] for k in range(4)], axis=1)
    w0 = W['l0_w_in']
    w0_main = jnp.concatenate([w0[:, 3088:], w0[:, :3072]], axis=1)
    w0_dt = jnp.pad(w0[:, 3072:3088], ((0, 0), (0, 112)))
    cos, sin = _rope_tables(nch)

    meta = jnp.broadcast_to(W['meta_tokens'][None], (B, N_META, D))
    h0 = jnp.concatenate([jnp.zeros((B, PAD, D), F32), meta, x], axis=1).reshape(R, D)
    n0, n0t = _rmsnorm_fwd(h0, row2(p['l0_mix_norm']), "norm_l0_mix")
    u0 = _mm(n0, w0_main, "nn", F32, "l0_in_proj")
    udt = _mm(n0, w0_dt, "nn", F32, "l0_dt_proj")
    a_log, d_skip, dt_bias = pad128(p['l0_ssd_a_log']), pad128(p['l0_ssd_d']), pad128(p['l0_ssd_dt_bias'])
    ssd_cb = row2(p['l0_ssd_conv_b'])
    act, dt, dtt, *got = _ssd_prep(u0, udt, W['l0_ssd_conv_w'], ssd_cb, dt_bias, B, nch, rider=gather_on('l0_w_out'))
    take_weights(['l0_w_out'], got)
    ycat0, ypre, hin, *got = _ssd_fwd(act, u0, dt, dtt, a_log, d_skip, row2(p['l0_ssd_norm']), B, nch,
                                      rider=gather_on('l0_ffn_w_in'))
    take_weights(['l0_ffn_w_in'], got)
    ycat0, opre, rin, *got = _ret_fwd(u0, ycat0, cos, sin, row2(p['l0_ret_norm']), B, nch,
                                      rider=gather_on('l0_ffn_w_out'))
    take_weights(['l0_ffn_w_out'], got)
    h1 = _mm(ycat0, W['l0_w_out'], "nn", F32, "l0_out_proj", add=h0)
    n1, n1t = _rmsnorm_fwd(h1, row2(p['l0_ffn_norm']), "norm_l0_ffn")
    uf0 = _mm(n1, W['l0_ffn_w_in'], "nn", F32, "l0_ffn_in")
    f0_cb = row2(p['l0_ffn_conv_b'])
    a0, *got = _ffn_act_fwd(uf0, W['l0_ffn_conv_w'], f0_cb, B, nch, rider=gather_on('l1_w_in'))
    take_weights(['l1_w_in'], got)
    h2 = _mm(a0, W['l0_ffn_w_out'], "nn", F32, "l0_ffn_out", add=h1)
    n2, n2t = _rmsnorm_fwd(h2, row2(p['l1_mix_norm']), "norm_l1_mix")
    u1 = _mm(n2, W['l1_w_in'], "nn", F32, "l1_in_proj")
    lru = (W['l1_lru_conv_w'], row2(p['l1_lru_conv_b']), p['l1_lru_wa'], row2(p['l1_lru_ba']), p['l1_lru_wx'],
           row2(p['l1_lru_bx']), row2(p['l1_lru_lambda']))
    later = ['l1_w_out', 'l1_ffn_w_in', 'l1_ffn_w_out']
    ycat1, *got = _sb_fwd(u1, B, nch, rider=gather_on(*later))
    take_weights(later, got)
    ycat1, hs = _lru_fwd(u1, ycat1, *lru, B, nch)
    h3 = _mm(ycat1, W['l1_w_out'], "nn", F32, "l1_out_proj", add=h2)
    n3, n3t = _rmsnorm_fwd(h3, row2(p['l1_ffn_norm']), "norm_l1_ffn")
    uf1 = _mm(n3, W['l1_ffn_w_in'], "nn", F32, "l1_ffn_in")
    f1_cb = row2(p['l1_ffn_conv_b'])
    a1, = _ffn_act_fwd(uf1, W['l1_ffn_conv_w'], f1_cb, B, nch)
    h4 = _mm(a1, W['l1_ffn_w_out'], "nn", F32, "l1_ffn_out", add=h3)
    dh4, lossp, dgf = _head(h4, row2(p['final_norm']), p['loss_target'].reshape(B * seq, D), B, nch)
    loss = lax.psum(jnp.sum(lossp[:, 0, 0]), ("x", "y", "c"))

    G = {'final_norm': dgf[:, 0].sum(0)}

    core = lax.axis_index("c").reshape(1).astype(jnp.int32)

    def chip_sums(names, tag):
        stacked = []
        for n in names:
            g = G[n]
            if n in _BIG_COLS:
                stacked.append(g.reshape(g.shape[0], 4, g.shape[1] // 4).transpose(1, 0, 2))
            else:
                stacked.append(g.reshape(4, g.shape[0] // 4, g.shape[1]))
        theirs = _swap_halves(stacked, "swap_halves_" + tag)
        return {n: _chip_sum(g, t, core, "chip_sum_" + n) for n, g, t in zip(names, stacked, theirs)}

    parts = {}

    def scatter_on(names, tag):
        sums = chip_sums(names, tag)
        return sums, ("scatter", [sums[n] for n in names])

    def take_parts(names, sums, got):
        for n, g in zip(names, got):
            parts[n] = _fill_own(g, lax.dynamic_index_in_dim(sums[n], chip, 0, keepdims=False), chip)

    def ffn_bwd(layer, dh_out, h_in, nt_in, uf, a_act, cb, rider=None):
        pre = f"l{layer}_"
        w_in, w_out, cw = W[pre + 'ffn_w_in'], W[pre + 'ffn_w_out'], W[pre + 'ffn_conv_w']
        da = _mm(dh_out, w_out, "nt", F32, pre + "ffn_out_dgrad")
        G[pre + 'ffn_w_out'] = _mm(a_act, dh_out, "tn", F32, pre + "ffn_out_wgrad")
        dug, duu, dwg, dwu, *rode = _ffn_act_bwd(da, uf, cw, cb, nch, pre + "ffn_act_bwd", rider=rider)
        G[pre + 'ffn_conv_w'] = jnp.concatenate([dwg[:3], dwu[:3]], axis=1)
        G[pre + 'ffn_conv_b'] = jnp.concatenate([dwg[7], dwu[7]])
        dn = _mm(dug, w_in, "nt", F32, pre + "ffn_in_dgrad_g")
        dn = _mm(duu, w_in, "nt", F32, pre + "ffn_in_dgrad_u", add=dn, b_off=FFN)
        G[pre + 'ffn_w_in'] = jnp.concatenate([_mm(nt_in, dug, "nn", F32, pre + "ffn_in_wgrad_g"),
                                               _mm(nt_in, duu, "nn", F32, pre + "ffn_in_wgrad_u")], axis=1)
        dh_in, dg = _rmsnorm_bwd(h_in, row2(p[pre + 'ffn_norm']), dn, dh_out, nch, pre + "ffn_norm_bwd")
        G[pre + 'ffn_norm'] = dg[0]
        return dh_in, rode

    dh3, _ = ffn_bwd(1, dh4, h3, n3t, uf1, a1, f1_cb)
    dy1 = _mm(dh3, W['l1_w_out'], "nt", F32, "l1_out_dgrad")
    G['l1_w_out'] = _mm(ycat1, dh3, "tn", F32, "l1_out_wgrad")
    done = ['l1_ffn_w_in', 'l1_ffn_w_out', 'l1_w_out']
    sums, rider = scatter_on(done, "a")
    dq, dkt, dvt, *got = _sb_bwd(dy1, u1, B, nch, rider=rider)
    dk, dv = dkt.T, dvt.T
    take_parts(done, sums, got)
    dgate, dxc, pgl, dwa, dwx = _lru_bwd(dy1, u1, hs, *lru, B, nch)
    dxr, dcw = _conv_bwd(dxc, u1, 4096, W['l1_lru_conv_w'], 4, "l1_lru_conv_bwd")
    pgl = pgl.sum(0)
    G['l1_lru_ba'], G['l1_lru_bx'], G['l1_lru_lambda'] = pgl[0], pgl[1], pgl[2]
    G['l1_lru_wa'], G['l1_lru_wx'] = dwa.sum(0), dwx.sum(0)
    G['l1_lru_conv_w'], G['l1_lru_conv_b'] = dcw[:4], dcw[7]
    dn, dws = None, []
    for i, piece in enumerate((dq, dk, dv, dgate, dxr)):
        dn = _mm(piece, W['l1_w_in'], "nt", F32, f"l1_in_dgrad_{i}", add=dn, b_off=1024 * i)
        dws.append(_mm(n2t, piece, "nn", F32, f"l1_in_wgrad_{i}"))
    G['l1_w_in'] = jnp.concatenate(dws, axis=1)
    dh2, dg = _rmsnorm_bwd(h2, row2(p['l1_mix_norm']), dn, dh3, nch, "l1_mix_norm_bwd")
    G['l1_mix_norm'] = dg[0]

    sums, rider = scatter_on(['l1_w_in'], "b")
    dh1, got = ffn_bwd(0, dh2, h1, n1t, uf0, a0, f0_cb, rider=rider)
    take_parts(['l1_w_in'], sums, got)
    dy0 = _mm(dh1, W['l0_w_out'], "nt", F32, "l0_out_dgrad")
    G['l0_w_out'] = _mm(ycat0, dh1, "tn", F32, "l0_out_wgrad")
    done = ['l0_ffn_w_in', 'l0_ffn_w_out', 'l0_w_out']
    sums, rider = scatter_on(done, "c")
    dz, dxs, dbm, dcm, ddt4, pgs, *got = _ssd_bwd(dy0, ypre, u0, act, dt, dtt, hin, a_log, d_skip,
                                                  row2(p['l0_ssd_norm']), B, nch, rider=rider)
    take_parts(done, sums, got)
    dpre, ddtr, pgd = _ssd_prep_bwd(dxs, dbm, dcm, ddt4, u0, udt, W['l0_ssd_conv_w'], ssd_cb, dt_bias, B, nch)
    dxbc, dcw0 = _conv_bwd(dpre, u0, U0_XBC, W['l0_ssd_conv_w'], 4, "l0_ssd_conv_bwd")
    dqkvg, pgr = _ret_bwd(dy0, u0, opre, rin, cos, sin, row2(p['l0_ret_norm']), B, nch)
    pgs = pgs.sum(0)
    G['l0_ssd_norm'] = pgs[:, 0, :].reshape(-1)
    G['l0_ssd_d'] = pgs[:, 1, :128].sum(0)[:SSD_HEADS]
    G['l0_ssd_a_log'] = pgs[:, 2, :128].sum(0)[:SSD_HEADS]
    G['l0_ssd_dt_bias'] = pgd.sum(0)[0, :SSD_HEADS]
    G['l0_ssd_conv_w'], G['l0_ssd_conv_b'] = dcw0[:4], dcw0[7]
    G['l0_ret_norm'] = pgr.sum(0)[0]
    dn = _mm(dqkvg, w0_main, "nt", F32, "l0_in_dgrad_qkvg")
    dn = _mm(dz, w0_main, "nt", F32, "l0_in_dgrad_z", add=dn, b_off=U0_Z)
    dn = _mm(dxbc, w0_main, "nt", F32, "l0_in_dgrad_xbc", add=dn, b_off=U0_XBC)
    dn = _mm(ddtr, w0_dt, "nt", F32, "l0_in_dgrad_dt", add=dn)
    G['l0_w_in'] = jnp.concatenate([
        _mm(n0t, dz, "nn", F32, "l0_in_wgrad_z"), _mm(n0t, dxbc, "nn", F32, "l0_in_wgrad_xbc"),
        _mm(n0t, ddtr, "nn", F32, "l0_in_wgrad_dt")[:, :SSD_HEADS], _mm(n0t, dqkvg, "nn", F32, "l0_in_wgrad_qkvg")], axis=1)
    dh0, dg = _rmsnorm_bwd(h0, row2(p['l0_mix_norm']), dn, dh1, nch, "l0_mix_norm_bwd")
    G['l0_mix_norm'] = dg[0]
    dh0 = dh0.reshape(B, Pn, D)
    grad_x = dh0[:, CH:]
    G['meta_tokens'] = dh0[:, PAD:CH].sum(0)

    sums = chip_sums(['l0_w_in'], "d")
    parts['l0_w_in'], = _scatter_sums([sums['l0_w_in']])
    reds = [_sum_chips(parts[n], "sum_chips_" + n) for n in _BIG]
    grads = {}
    for n, own, other in zip(_BIG, reds, _join_halves(reds)):
        both = jnp.where(core[0] == 0, jnp.stack([own, other]), jnp.stack([other, own]))
        grads[n] = both.reshape(-1, both.shape[2])
    small_full = _unpack(_allreduce_small(_pack([G[n] for n in _SMALL])), [G[n].shape for n in _SMALL])
    for n, g in zip(_SMALL, small_full):
        if n in _SMALL_SHARDED:
            cs = g.shape[1] // 4
            g = lax.dynamic_slice_in_dim(g, chip * cs, cs, axis=1)
        grads[n] = g.reshape(p[n].shape)

    delta, new_m, new_v = {}, {}, {}
    for n in _BIG:
        delta[n], new_m[n], new_v[n] = _adamw(p[n], grads[n], p['m_' + n], p['v_' + n], "adamw_" + n)
    shapes = [p[n].shape for n in _SMALL]
    outs = _adamw(_pack([p[n] for n in _SMALL]), _pack([grads[n] for n in _SMALL]), _pack([p['m_' + n] for n in _SMALL]),
                  _pack([p['v_' + n] for n in _SMALL]), "adamw_small")
    for dst, buf in zip((delta, new_m, new_v), outs):
        for n, a in zip(_SMALL, _unpack(buf, shapes)):
            dst[n] = a
    return (loss, grad_x, *[grads[n] for n in _W_NAMES], *[delta[n] for n in _W_NAMES],
            *[new_m[n] for n in _W_NAMES], *[new_v[n] for n in _W_NAMES])
```

```python
import math

import numpy as np
import jax
import jax.numpy as jnp
from jax import lax
from jax.experimental import pallas as pl
from jax.experimental.pallas import tpu as pltpu

F32 = jnp.float32
BF16 = jnp.bfloat16
_MXU = jnp.bfloat16

D = 1024
CH = 128
N_META = 16
PAD = CH - N_META
EPS = 1e-6

SSD_HEADS = 16
SSD_HD = 64
SSD_GROUPS = 4
RET_HEADS = 4
RET_DK = 256
SB_HEADS = 16
SB_HD = 64
LRU_BLOCKS = 8
LRU_C = 8.0
FFN = 2816
U0_Z = 4096
U0_XBC = 5120

VMEM_LIMIT = 56 * 1024 * 1024


def _cparams(sem):
    return pltpu.CompilerParams(dimension_semantics=sem, vmem_limit_bytes=VMEM_LIMIT)


def _dot(a, b, dims=((1,), (0,))):
    return lax.dot_general(a.astype(_MXU), b.astype(_MXU), (dims, ((), ())), preferred_element_type=F32)


def _dot_nt(a, b):
    return _dot(a, b, ((1,), (1,)))


def _dot_tn(a, b):
    return _dot(a.T, b)


def _dot_exact(a, b):
    return lax.dot_general(a, b, (((1,), (0,)), ((), ())), preferred_element_type=F32,
                           precision=lax.Precision.HIGHEST)


def _dot_split(x, m01):
    hi = x.astype(BF16)
    lo = (x - hi.astype(F32)).astype(BF16)
    m = m01.astype(BF16)
    return jnp.dot(hi, m, preferred_element_type=F32) + jnp.dot(lo, m, preferred_element_type=F32)


def _sigmoid(x):
    return 0.5 * jnp.tanh(0.5 * x) + 0.5


def _softplus(x):
    return jnp.maximum(x, 0.0) + jnp.log1p(jnp.exp(-jnp.abs(x)))


def _silu(x):
    return x * _sigmoid(x)


def _dsilu(x):
    s = _sigmoid(x)
    return s * (1.0 + x * (1.0 - s))


_GELU_C = math.sqrt(2.0 / math.pi)


def _gelu(x):
    return 0.5 * x * (1.0 + jnp.tanh(_GELU_C * (x + 0.044715 * x * x * x)))


def _dgelu(x):
    t = jnp.tanh(_GELU_C * (x + 0.044715 * x * x * x))
    return 0.5 * (1.0 + t) + 0.5 * x * (1.0 - t * t) * _GELU_C * (1.0 + 3.0 * 0.044715 * x * x)


def _row_ids(n, cols=1):
    return lax.broadcasted_iota(jnp.int32, (n, cols), 0)


def _lane_ids(rows, n):
    return lax.broadcasted_iota(jnp.int32, (rows, n), 1)


def _real_rows(chunk):
    return chunk * CH + _row_ids(CH) >= PAD


def _shift_down(prev8, cur, s):
    cat = jnp.concatenate([prev8, cur], axis=0)
    return pltpu.roll(cat, s, axis=0)[8:]


def _shift_up(cur, next8, s):
    n = cur.shape[0]
    cat = jnp.concatenate([cur, next8], axis=0)
    return pltpu.roll(cat, n + 8 - s, axis=0)[:n]


def _conv_pre(prev8, cur, w_ref, b_ref, K):
    acc = cur * w_ref[K - 1:K, :] + b_ref[...]
    for s in range(1, K):
        acc = acc + _shift_down(prev8, cur, s) * w_ref[K - 1 - s:K - s, :]
    return acc


def _prev8_map(nch, col):
    return lambda b, c: (jnp.maximum((b * nch + c) * (CH // 8) - 1, 0), col)


def _matmul(a, b, mode, out_dtype, tm, tn, tk, name, add=None, b_off=0):
    if mode == "nn":
        (M, K), (_, N) = a.shape, b.shape
    elif mode == "nt":
        (M, K), N = a.shape, b.shape[0]
    else:
        (K, M), (_, N) = a.shape, b.shape
    tm, tn, tk = min(tm, M), min(tn, N), min(tk, K)
    assert M % tm == 0 and N % tn == 0 and K % tk == 0 and b_off % tk == 0, (name, M, N, K, tm, tn, tk)
    koff = b_off // tk
    nk = K // tk
    dims = {"nn": ((1,), (0,)), "nt": ((1,), (1,)), "tn": ((0,), (0,))}[mode]
    if mode == "tn":
        a_spec = pl.BlockSpec((tk, tm), lambda i, j, k: (k, i))
    else:
        a_spec = pl.BlockSpec((tm, tk), lambda i, j, k: (i, k))
    if mode == "nt":
        b_spec = pl.BlockSpec((tn, tk), lambda i, j, k: (j, k + koff))
    else:
        b_spec = pl.BlockSpec((tk, tn), lambda i, j, k: (k, j))
    o_spec = pl.BlockSpec((tm, tn), lambda i, j, k: (i, j))
    has_add = add is not None

    def body(a_ref, b_ref, *rest):
        if has_add:
            add_ref, o_ref, acc = rest
        else:
            o_ref, acc = rest
        k = pl.program_id(2)

        @pl.when(k == 0)
        def _():
            acc[...] = jnp.zeros_like(acc)

        acc[...] += _dot(a_ref[...], b_ref[...], dims)

        @pl.when(k == nk - 1)
        def _():
            r = acc[...]
            if has_add:
                r = r + add_ref[...].astype(F32)
            o_ref[...] = r.astype(out_dtype)

    in_specs = [a_spec, b_spec] + ([o_spec] if has_add else [])
    args = (a, b) + ((add,) if has_add else ())
    return pl.pallas_call(
        body, name=name, grid=(M // tm, N // tn, nk),
        in_specs=in_specs, out_specs=o_spec,
        out_shape=jax.ShapeDtypeStruct((M, N), out_dtype),
        scratch_shapes=[pltpu.VMEM((tm, tn), F32)],
        compiler_params=_cparams(("parallel", "parallel", "arbitrary")),
    )(*args)


def _tile(n, prefs):
    for t in prefs:
        if n % t == 0:
            return t
    return n


def _mm(a, b, mode, out_dtype, name, add=None, b_off=0):
    if mode == "tn":
        K, M = a.shape
        N = b.shape[1]
        tm, tn, tk = _tile(M, (1024, 1408, 512, 256, 128)), _tile(N, (1024, 1408, 512, 256, 128)), _tile(K, (2176, 384, 256, 128))
    else:
        M, K = a.shape
        N = b.shape[1] if mode == "nn" else b.shape[0]
        tm = _tile(M, (1088, 1024, 768, 512, 384, 256, 128))
        tn = _tile(N, (1024, 1408, 512, 256, 128))
        tk = _tile(K, (2176, 1024, 1408, 512, 256, 128))
    return _matmul(a, b, mode, out_dtype, tm, tn, tk, name, add=add, b_off=b_off)


def _rmsnorm_fwd(h, g, name):
    R = h.shape[0]
    tr = 2 * CH

    def body(h_ref, g_ref, o_ref, ot_ref):
        x = h_ref[...]
        r = lax.rsqrt(jnp.mean(x * x, axis=-1, keepdims=True) + EPS)
        y = x * r * g_ref[...]
        o_ref[...] = y.astype(o_ref.dtype)
        ot_ref[...] = y.T.astype(ot_ref.dtype)

    return pl.pallas_call(
        body, name=name, grid=(R // tr,),
        in_specs=[pl.BlockSpec((tr, D), lambda i: (i, 0)), pl.BlockSpec((1, D), lambda i: (0, 0))],
        out_specs=[pl.BlockSpec((tr, D), lambda i: (i, 0)), pl.BlockSpec((D, tr), lambda i: (0, i))],
        out_shape=[jax.ShapeDtypeStruct((R, D), _MXU), jax.ShapeDtypeStruct((D, R), _MXU)],
        compiler_params=_cparams(("parallel",)),
    )(h, g)


def _rmsnorm_bwd(h, g, dn, dres, nch, name):
    R = h.shape[0]
    per = 4
    tr = nch * CH // per

    def body(h_ref, g_ref, dn_ref, dres_ref, dh_ref, dg_ref):
        i = pl.program_id(0)
        x = h_ref[...]
        r = lax.rsqrt(jnp.mean(x * x, axis=-1, keepdims=True) + EPS)
        xhat = x * r
        dn_v = dn_ref[...]
        dx = dn_v * g_ref[...]
        dh = r * (dx - xhat * jnp.mean(dx * xhat, axis=-1, keepdims=True))
        keep = (i % per) * tr + _row_ids(tr) >= PAD
        dh_ref[...] = jnp.where(keep, dres_ref[...] + dh, 0.0)

        @pl.when(i == 0)
        def _():
            dg_ref[...] = jnp.zeros_like(dg_ref)

        dg_ref[...] += jnp.sum(dn_v * xhat, axis=0, keepdims=True)

    row = pl.BlockSpec((tr, D), lambda i: (i, 0))
    vec = pl.BlockSpec((1, D), lambda i: (0, 0))
    return pl.pallas_call(
        body, name=name, grid=(R // tr,),
        in_specs=[row, vec, row, row], out_specs=[row, vec],
        out_shape=[jax.ShapeDtypeStruct((R, D), F32), jax.ShapeDtypeStruct((1, D), F32)],
        compiler_params=_cparams(("arbitrary",)),
    )(h, g, dn, dres)


def _ssd_prep(u0, udt, conv_w, conv_b, dt_bias, B, nch, rider=None):
    R = u0.shape[0]

    def body(xs_ref, xsp_ref, bc_ref, bcp_ref, udt_ref, w0_ref, w1_ref, b0_ref, b1_ref, dtb_ref,
             act_ref, dt_ref, dtt_ref):
        keep = _real_rows(pl.program_id(1))
        a0 = _silu(_conv_pre(xsp_ref[...], xs_ref[...], w0_ref, b0_ref, 4))
        a1 = _silu(_conv_pre(bcp_ref[...], bc_ref[...], w1_ref, b1_ref, 4))
        act_ref[:, :1024] = jnp.where(keep, a0, 0.0)
        act_ref[:, 1024:] = jnp.where(keep, a1, 0.0)
        ok = jnp.logical_and(keep, _lane_ids(1, 128) < SSD_HEADS)
        dt = jnp.where(ok, _softplus(udt_ref[...] + dtb_ref[...]), 0.0)
        dt_ref[...] = dt
        dtt_ref[...] = dt.T

    row = lambda col: pl.BlockSpec((CH, 1024), lambda b, c: (b * nch + c, col))
    prev = lambda col: pl.BlockSpec((8, 1024), _prev8_map(nch, col))
    kw = dict(
        grid=(B, nch),
        in_specs=[row(5), prev(5), row(6), prev(6),
                  pl.BlockSpec((CH, 128), lambda b, c: (b * nch + c, 0)),
                  pl.BlockSpec((4, 1024), lambda b, c: (0, 0)), pl.BlockSpec((4, 1024), lambda b, c: (0, 1)),
                  pl.BlockSpec((1, 1024), lambda b, c: (0, 0)), pl.BlockSpec((1, 1024), lambda b, c: (0, 1)),
                  pl.BlockSpec((1, 128), lambda b, c: (0, 0))],
        out_specs=[pl.BlockSpec((CH, 2048), lambda b, c: (b * nch + c, 0)),
                   pl.BlockSpec((CH, 128), lambda b, c: (b * nch + c, 0)),
                   pl.BlockSpec((128, CH), lambda b, c: (0, b * nch + c))],
        out_shape=[jax.ShapeDtypeStruct((R, 2048), F32), jax.ShapeDtypeStruct((R, 128), F32),
                   jax.ShapeDtypeStruct((128, R), F32)])
    return _call(body, "ssd_prep", ("arbitrary", "arbitrary"), kw,
                 (u0, u0, u0, u0, udt, conv_w, conv_w, conv_b, conv_b, dt_bias), rider)


def _ssd_head_terms(h, a_vec, dt_v, dtt_v, dsk_v):
    lane = _lane_ids(1, 128)
    sub = _row_ids(128)
    r = _row_ids(CH, CH)
    cidx = _lane_ids(CH, CH)
    a_h = jnp.sum(jnp.where(lane == h, a_vec, 0.0), axis=1, keepdims=True)
    dt_col = jnp.sum(jnp.where(lane == h, dt_v, 0.0), axis=1, keepdims=True)
    dt_row = jnp.sum(jnp.where(sub == h, dtt_v, 0.0), axis=0, keepdims=True)
    cs_col = jnp.sum(jnp.where(r >= cidx, dt_row * a_h, 0.0), axis=1, keepdims=True)
    cs_row = jnp.sum(jnp.where(r <= cidx, dt_col * a_h, 0.0), axis=0, keepdims=True)
    tot = jnp.sum(dt_col * a_h, axis=0, keepdims=True)
    dsk = jnp.sum(jnp.where(lane == h, dsk_v, 0.0), axis=1, keepdims=True)
    return a_h, dt_col, cs_col, cs_row, tot, dsk


def _ssd_fwd(act, u0, dt, dtt, a_log, d_skip, norm_g, B, nch, rider=None):
    R = act.shape[0]

    def body(xs_ref, bm_ref, cm_ref, z_ref, dt_ref, dtt_ref, alog_ref, dsk_ref, ng_ref,
             out_ref, ypre_ref, hin_ref, H):
        g = pl.program_id(1)
        c = pl.program_id(2)

        @pl.when(c == 0)
        def _():
            H[...] = jnp.zeros_like(H)

        hin_ref[...] = H[...]
        a_vec = -jnp.exp(alog_ref[...])
        dt_v = dt_ref[...]
        dtt_v = dtt_ref[...]
        hm = _lane_ids(1, 128) < SSD_HD
        r = _row_ids(CH, CH)
        cidx = _lane_ids(CH, CH)
        Bm = bm_ref[...]
        Cm = cm_ref[...]
        CB = _dot_nt(Cm, Bm)
        ys = []
        for pair in range(2):
            cols = slice(128 * pair, 128 * pair + 128)
            xraw = xs_ref[:, cols]
            t = [_ssd_head_terms(4 * g + 2 * pair + j, a_vec, dt_v, dtt_v, dsk_ref[...]) for j in range(2)]
            sel = lambda f: jnp.where(hm, f(t[0]), f(t[1]))
            dtp = sel(lambda q: q[1])
            Ep = sel(lambda q: jnp.exp(q[2]))
            Wp = sel(lambda q: jnp.exp(q[4] - q[2]))
            etot = sel(lambda q: jnp.exp(q[4]))
            dsk = sel(lambda q: q[5])
            X = xraw * dtp
            ydiag = jnp.zeros((CH, 128), F32)
            for j in range(2):
                Lm = jnp.where(r >= cidx, jnp.exp(t[j][2] - t[j][3]), 0.0)
                Xh = jnp.where(hm if j == 0 else jnp.logical_not(hm), X, 0.0)
                ydiag = ydiag + _dot(CB * Lm, Xh)
            Hp = H[:, cols]
            yoff = Ep * _dot(Cm, Hp)
            S = _dot(Bm.T, X * Wp)
            H[:, cols] = etot * Hp + S
            ys.append(ydiag + yoff + xraw * dsk)
        y = jnp.concatenate(ys, axis=1)
        ypre_ref[...] = y
        yg = y * _silu(z_ref[...])
        rr = lax.rsqrt(jnp.mean(yg * yg, axis=-1, keepdims=True) + EPS)
        out_ref[...] = jnp.where(_real_rows(c), yg * rr * ng_ref[...], 0.0).astype(out_ref.dtype)

    rowb = lambda w, colf: pl.BlockSpec((CH, w), lambda b, g, c: (b * nch + c, colf(g)))
    vec = pl.BlockSpec((1, 128), lambda b, g, c: (0, 0))
    kw = dict(
        grid=(B, SSD_GROUPS, nch),
        in_specs=[rowb(256, lambda g: g), rowb(128, lambda g: 8 + g), rowb(128, lambda g: 12 + g),
                  rowb(256, lambda g: 16 + g), rowb(128, lambda g: 0),
                  pl.BlockSpec((128, CH), lambda b, g, c: (0, b * nch + c)),
                  vec, vec, pl.BlockSpec((1, 256), lambda b, g, c: (0, g))],
        out_specs=[rowb(256, lambda g: g), rowb(256, lambda g: g),
                   pl.BlockSpec((None, None, None, 128, 256), lambda b, g, c: (b, g, c, 0, 0))],
        out_shape=[jax.ShapeDtypeStruct((R, 2048), _MXU), jax.ShapeDtypeStruct((R, 1024), F32),
                   jax.ShapeDtypeStruct((B, SSD_GROUPS, nch, 128, 256), F32)],
        scratch_shapes=[pltpu.VMEM((128, 256), F32)])
    return _call(body, "ssd_fwd", ("arbitrary", "arbitrary", "arbitrary"), kw,
                 (act, act, act, u0, dt, dtt, a_log, d_skip, norm_g), rider)


def _ssd_bwd(dycat, ypre, u0, act, dt, dtt, hin, a_log, d_skip, norm_g, B, nch, rider=None):
    R = act.shape[0]

    def body(dy_ref, ypre_ref, z_ref, xs_ref, bm_ref, cm_ref, dt_ref, dtt_ref, hin_ref, alog_ref, dsk_ref, ng_ref,
             dz_ref, dxs_ref, db_ref, dc_ref, ddt_ref, pg_ref, dH):
        g = pl.program_id(1)
        c = nch - 1 - pl.program_id(2)

        @pl.when(pl.program_id(2) == 0)
        def _():
            dH[...] = jnp.zeros_like(dH)
            pg_ref[...] = jnp.zeros_like(pg_ref)

        z = z_ref[...]
        y = ypre_ref[...]
        ng = ng_ref[...]
        dout = jnp.where(_real_rows(c), dy_ref[...], 0.0)
        sz = _sigmoid(z)
        yg = y * z * sz
        rr = lax.rsqrt(jnp.mean(yg * yg, axis=-1, keepdims=True) + EPS)
        nrm = yg * rr
        pg_ref[0:1, :] += jnp.sum(dout * nrm, axis=0, keepdims=True)
        dn = dout * ng
        dyg = rr * (dn - nrm * jnp.mean(dn * nrm, axis=-1, keepdims=True))
        dy = dyg * z * sz
        dz_ref[...] = (dyg * y * (sz * (1.0 + z * (1.0 - sz)))).astype(dz_ref.dtype)

        a_vec = -jnp.exp(alog_ref[...])
        dt_v = dt_ref[...]
        dtt_v = dtt_ref[...]
        lane = _lane_ids(1, 128)
        hm = lane < SSD_HD
        r = _row_ids(CH, CH)
        cidx = _lane_ids(CH, CH)
        last = _row_ids(CH) == CH - 1
        Bm = bm_ref[...]
        Cm = cm_ref[...]
        CB = _dot_nt(Cm, Bm)
        CBT = _dot_nt(Bm, Cm)
        dB = jnp.zeros((CH, 128), F32)
        dC = jnp.zeros((CH, 128), F32)
        dcs_all = jnp.zeros((CH, 128), F32)
        dtx_all = jnp.zeros((CH, 128), F32)
        dd_row = jnp.zeros((1, 128), F32)
        dxs = []
        for pair in range(2):
            cols = slice(128 * pair, 128 * pair + 128)
            xraw = xs_ref[:, cols]
            dyp = dy[:, cols]
            heads = [4 * g + 2 * pair + j for j in range(2)]
            t = [_ssd_head_terms(heads[j], a_vec, dt_v, dtt_v, dsk_ref[...]) for j in range(2)]
            sel = lambda f: jnp.where(hm, f(t[0]), f(t[1]))
            hsum = lambda v, j: jnp.sum(jnp.where(hm if j == 0 else jnp.logical_not(hm), v, 0.0), axis=1, keepdims=True)
            dtp = sel(lambda q: q[1])
            Ep = sel(lambda q: jnp.exp(q[2]))
            Wp = sel(lambda q: jnp.exp(q[4] - q[2]))
            etot = sel(lambda q: jnp.exp(q[4]))
            dsk = sel(lambda q: q[5])
            X = xraw * dtp
            Hp = hin_ref[:, cols]
            dHn = dH[:, cols]
            dskip = jnp.sum(dyp * xraw, axis=0, keepdims=True)
            yoff = Ep * _dot(Cm, Hp)
            dE = dyp * yoff
            dC = dC + _dot_nt(dyp * Ep, Hp)
            dH[:, cols] = etot * dHn + _dot(Cm.T, dyp * Ep)
            BdS = _dot(Bm, dHn)
            dX = Wp * BdS
            ew = X * BdS * Wp
            dB = dB + _dot_nt(X * Wp, dHn)
            hh = jnp.sum(dHn * Hp, axis=0, keepdims=True) * etot
            for j in range(2):
                hmask = hm if j == 0 else jnp.logical_not(hm)
                cs_col, cs_row = t[j][2], t[j][3]
                Lm = jnp.where(r >= cidx, jnp.exp(cs_col - cs_row), 0.0)
                LmT = jnp.where(cidx >= r, jnp.exp(cs_row - cs_col), 0.0)
                dyh = jnp.where(hmask, dyp, 0.0)
                Xh = jnp.where(hmask, X, 0.0)
                dM = _dot_nt(dyh, Xh)
                dMT = _dot_nt(Xh, dyh)
                M = CB * Lm
                MT = CBT * LmT
                dX = dX + _dot(MT, dyh)
                dC = dC + _dot(dM * Lm, Bm)
                dB = dB + _dot(dMT * LmT, Cm)
                g_rows = jnp.sum(dM * M, axis=1, keepdims=True)
                g_cols = jnp.sum(dMT * MT, axis=1, keepdims=True)
                dtot = (jnp.sum(hsum(ew, j), axis=0, keepdims=True)
                        + jnp.sum(jnp.where(hmask, hh, 0.0), axis=1, keepdims=True))
                dcs = g_rows - g_cols + hsum(dE, j) - hsum(ew, j) + jnp.where(last, dtot, 0.0)
                dcs_all = dcs_all + jnp.where(lane == heads[j], dcs, 0.0)
                dtx_all = dtx_all + jnp.where(lane == heads[j], hsum(dX * xraw, j), 0.0)
                dd_row = dd_row + jnp.where(lane == heads[j],
                                            jnp.sum(jnp.where(hmask, dskip, 0.0), axis=1, keepdims=True), 0.0)
            dxs.append(dX * dtp + dyp * dsk)
        dxs_ref[...] = jnp.concatenate(dxs, axis=1)
        db_ref[...] = dB
        dc_ref[...] = dC
        dadt = _dot_exact(jnp.where(cidx >= r, 1.0, 0.0), dcs_all)
        ddt_ref[...] = dadt * a_vec + dtx_all
        pg_ref[1:2, 0:128] += dd_row
        pg_ref[2:3, 0:128] += jnp.sum(dadt * dt_v, axis=0, keepdims=True) * a_vec

    rowb = lambda w, colf: pl.BlockSpec((CH, w), lambda b, g, c: (b * nch + nch - 1 - c, colf(g)))
    vec = pl.BlockSpec((1, 128), lambda b, g, c: (0, 0))
    kw = dict(
        grid=(B, SSD_GROUPS, nch),
        in_specs=[rowb(256, lambda g: g), rowb(256, lambda g: g), rowb(256, lambda g: 16 + g), rowb(256, lambda g: g),
                  rowb(128, lambda g: 8 + g), rowb(128, lambda g: 12 + g), rowb(128, lambda g: 0),
                  pl.BlockSpec((128, CH), lambda b, g, c: (0, b * nch + nch - 1 - c)),
                  pl.BlockSpec((None, None, None, 128, 256), lambda b, g, c: (b, g, nch - 1 - c, 0, 0)),
                  vec, vec, pl.BlockSpec((1, 256), lambda b, g, c: (0, g))],
        out_specs=[rowb(256, lambda g: g), rowb(256, lambda g: g), rowb(128, lambda g: g), rowb(128, lambda g: g),
                   rowb(128, lambda g: g),
                   pl.BlockSpec((None, None, 8, 256), lambda b, g, c: (b, g, 0, 0))],
        out_shape=[jax.ShapeDtypeStruct((R, 1024), _MXU), jax.ShapeDtypeStruct((R, 1024), F32),
                   jax.ShapeDtypeStruct((R, 512), F32), jax.ShapeDtypeStruct((R, 512), F32),
                   jax.ShapeDtypeStruct((R, 512), F32), jax.ShapeDtypeStruct((B, SSD_GROUPS, 8, 256), F32)],
        scratch_shapes=[pltpu.VMEM((128, 256), F32)])
    return _call(body, "ssd_bwd", ("arbitrary", "arbitrary", "arbitrary"), kw,
                 (dycat, ypre, u0, act, act, act, dt, dtt, hin, a_log, d_skip, norm_g), rider)


def _ssd_prep_bwd(dxs, dB, dC, ddt4, u0, udt, conv_w, conv_b, dt_bias, B, nch, rider=None):
    R = u0.shape[0]

    def body(dxs_ref, db_ref, dc_ref, ddt_ref, xs_ref, xsp_ref, bc_ref, bcp_ref, udt_ref, w0_ref, w1_ref, b0_ref, b1_ref,
             dtb_ref, dpre_ref, ddtr_ref, pgd_ref):
        c = pl.program_id(1)

        @pl.when(c == 0)
        def _():
            pgd_ref[...] = jnp.zeros_like(pgd_ref)

        keep = _real_rows(c)
        p0 = _conv_pre(xsp_ref[...], xs_ref[...], w0_ref, b0_ref, 4)
        p1 = _conv_pre(bcp_ref[...], bc_ref[...], w1_ref, b1_ref, 4)
        dpre_ref[:, :1024] = jnp.where(keep, dxs_ref[...] * _dsilu(p0), 0.0)
        dpre_ref[:, 1024:] = jnp.where(keep, jnp.concatenate([db_ref[...], dc_ref[...]], axis=1) * _dsilu(p1), 0.0)
        ddt = ddt_ref[:, 0:128] + ddt_ref[:, 128:256] + ddt_ref[:, 256:384] + ddt_ref[:, 384:512]
        ok = jnp.logical_and(keep, _lane_ids(1, 128) < SSD_HEADS)
        dr = jnp.where(ok, ddt * _sigmoid(udt_ref[...] + dtb_ref[...]), 0.0)
        ddtr_ref[...] = dr
        pgd_ref[0:1, :] += jnp.sum(dr, axis=0, keepdims=True)

    rw = lambda w: pl.BlockSpec((CH, w), lambda b, c: (b * nch + c, 0))
    row = lambda col: pl.BlockSpec((CH, 1024), lambda b, c: (b * nch + c, col))
    prev = lambda col: pl.BlockSpec((8, 1024), _prev8_map(nch, col))
    kw = dict(
        grid=(B, nch),
        in_specs=[rw(1024), rw(512), rw(512), rw(512), row(5), prev(5), row(6), prev(6), rw(128),
                  pl.BlockSpec((4, 1024), lambda b, c: (0, 0)), pl.BlockSpec((4, 1024), lambda b, c: (0, 1)),
                  pl.BlockSpec((1, 1024), lambda b, c: (0, 0)), pl.BlockSpec((1, 1024), lambda b, c: (0, 1)),
                  pl.BlockSpec((1, 128), lambda b, c: (0, 0))],
        out_specs=[rw(2048), rw(128), pl.BlockSpec((None, 8, 128), lambda b, c: (b, 0, 0))],
        out_shape=[jax.ShapeDtypeStruct((R, 2048), F32), jax.ShapeDtypeStruct((R, 128), F32),
                   jax.ShapeDtypeStruct((B, 8, 128), F32)])
    return _call(body, "ssd_prep_bwd", ("arbitrary", "arbitrary"), kw,
                 (dxs, dB, dC, ddt4, u0, u0, u0, u0, udt, conv_w, conv_w, conv_b, conv_b, dt_bias), rider)


def _conv_bwd(dpre, xin, xin_col, w, K, name, tc=1024):
    R, C = dpre.shape
    assert C % tc == 0 and xin_col % tc == 0
    nr = R // CH
    xoff = xin_col // tc

    def body(dp_ref, dpn_ref, x_ref, xp_ref, w_ref, din_ref, dw_ref):
        i = pl.program_id(1)

        @pl.when(i == 0)
        def _():
            dw_ref[...] = jnp.zeros_like(dw_ref)

        dp = dp_ref[...]
        nxt = dpn_ref[...] * (i < nr - 1).astype(F32)
        x = x_ref[...]
        xp = xp_ref[...]
        din = dp * w_ref[K - 1:K, :]
        dw_ref[K - 1:K, :] += jnp.sum(dp * x, axis=0, keepdims=True)
        dw_ref[7:8, :] += jnp.sum(dp, axis=0, keepdims=True)
        for s in range(1, K):
            din = din + _shift_up(dp, nxt, s) * w_ref[K - 1 - s:K - s, :]
            dw_ref[K - 1 - s:K - s, :] += jnp.sum(dp * _shift_down(xp, x, s), axis=0, keepdims=True)
        din_ref[...] = din.astype(din_ref.dtype)

    return pl.pallas_call(
        body, name=name, grid=(C // tc, nr),
        in_specs=[pl.BlockSpec((CH, tc), lambda j, i: (i, j)),
                  pl.BlockSpec((8, tc), lambda j, i: (jnp.minimum((i + 1) * (CH // 8), nr * (CH // 8) - 1), j)),
                  pl.BlockSpec((CH, tc), lambda j, i: (i, xoff + j)),
                  pl.BlockSpec((8, tc), lambda j, i: (jnp.maximum(i * (CH // 8) - 1, 0), xoff + j)),
                  pl.BlockSpec((K, tc), lambda j, i: (0, j))],
        out_specs=[pl.BlockSpec((CH, tc), lambda j, i: (i, j)),
                   pl.BlockSpec((8, tc), lambda j, i: (0, j))],
        out_shape=[jax.ShapeDtypeStruct((R, C), _MXU), jax.ShapeDtypeStruct((8, C), F32)],
        compiler_params=_cparams(("parallel", "arbitrary")),
    )(dpre, dpre, xin, xin, w)


_RET_LG = [float(v) for v in np.log1p(-np.exp2(-5.0 - np.arange(RET_HEADS, dtype=np.float32))).astype(np.float32)]
_RET_SCALE = RET_DK ** -0.5


def _rope_tables(nch):
    half = RET_DK // 2
    inv_freq = 1.0 / (10000.0 ** (jnp.arange(half, dtype=F32) / (half - 1)))
    pos = jnp.arange(nch * CH, dtype=F32) - PAD
    ang = pos[:, None] * inv_freq[None, :]
    return jnp.cos(ang), jnp.sin(ang)


def _rot(x, cos, sin):
    x1, x2 = x[:, :128], x[:, 128:]
    return jnp.concatenate([x1 * cos - x2 * sin, x1 * sin + x2 * cos], axis=1)


def _unrot(d, cos, sin):
    d1, d2 = d[:, :128], d[:, 128:]
    return jnp.concatenate([d1 * cos + d2 * sin, d2 * cos - d1 * sin], axis=1)


def _ret_decays(lg):
    r = _row_ids(CH, CH)
    cidx = _lane_ids(CH, CH)
    diff = (r - cidx).astype(F32)
    decay = jnp.where(r >= cidx, jnp.exp(lg * jnp.maximum(diff, 0.0)), 0.0)
    decay_t = jnp.where(cidx >= r, jnp.exp(lg * jnp.maximum(-diff, 0.0)), 0.0)
    idx = _row_ids(CH).astype(F32)
    zeta = jnp.exp(lg * (CH - 1.0 - idx))
    xi = jnp.exp(lg * (idx + 1.0))
    return decay, decay_t, zeta, xi


def _ret_fwd(u0, ycat, cos, sin, norm_g, B, nch, rider=None):
    R = u0.shape[0]

    def body(u_ref, cos_ref, sin_ref, ng_ref, ycat_in, out_ref, opre_ref, rin_ref, Rst):
        c = pl.program_id(1)

        @pl.when(c == 0)
        def _():
            Rst[...] = jnp.zeros_like(Rst)

        cos_v, sin_v = cos_ref[...], sin_ref[...]
        for h in range(RET_HEADS):
            lg = _RET_LG[h]
            cols = slice(256 * h, 256 * h + 256)
            decay, _, zeta, xi = _ret_decays(lg)
            qr = _rot(u_ref[:, cols], cos_v, sin_v)
            kr = _rot(u_ref[:, 1024 + 256 * h:1024 + 256 * h + 256], cos_v, sin_v) * _RET_SCALE
            v = u_ref[:, 2048 + 256 * h:2048 + 256 * h + 256]
            gate = u_ref[:, 3072 + 256 * h:3072 + 256 * h + 256]
            Rh = Rst[h]
            rin_ref[h] = Rh
            inner = _dot(_dot_nt(qr, kr) * decay, v)
            cross = _dot(qr, Rh) * xi
            Rst[h] = math.exp(CH * lg) * Rh + _dot((kr * zeta).T, v)
            o = inner + cross
            opre_ref[:, cols] = o
            oc = o - jnp.mean(o, axis=-1, keepdims=True)
            rr = lax.rsqrt(jnp.mean(oc * oc, axis=-1, keepdims=True) + EPS)
            out_ref[:, cols] = (_silu(gate) * (oc * rr * ng_ref[:, cols])).astype(out_ref.dtype)

    kw = dict(
        grid=(B, nch),
        in_specs=[pl.BlockSpec((CH, 4096), lambda b, c: (b * nch + c, 0)),
                  pl.BlockSpec((CH, 128), lambda b, c: (c, 0)), pl.BlockSpec((CH, 128), lambda b, c: (c, 0)),
                  pl.BlockSpec((1, 1024), lambda b, c: (0, 0)),
                  pl.BlockSpec(memory_space=pl.ANY)],
        out_specs=[pl.BlockSpec((CH, 1024), lambda b, c: (b * nch + c, 1)),
                   pl.BlockSpec((CH, 1024), lambda b, c: (b * nch + c, 0)),
                   pl.BlockSpec((None, None, RET_HEADS, 256, 256), lambda b, c: (b, c, 0, 0, 0))],
        out_shape=[jax.ShapeDtypeStruct(ycat.shape, ycat.dtype), jax.ShapeDtypeStruct((R, 1024), F32),
                   jax.ShapeDtypeStruct((B, nch, RET_HEADS, 256, 256), F32)],
        scratch_shapes=[pltpu.VMEM((RET_HEADS, 256, 256), F32)],
        input_output_aliases={4: 0})
    return _call(body, "ret_fwd", ("arbitrary", "arbitrary"), kw, (u0, cos, sin, norm_g, ycat), rider)


def _ret_bwd(dycat, u0, opre, rin, cos, sin, norm_g, B, nch, rider=None):
    R = u0.shape[0]

    def body(dy_ref, u_ref, opre_ref, rin_ref, cos_ref, sin_ref, ng_ref, du_ref, pg_ref, dR):
        @pl.when(pl.program_id(1) == 0)
        def _():
            dR[...] = jnp.zeros_like(dR)
            pg_ref[...] = jnp.zeros_like(pg_ref)

        cos_v, sin_v = cos_ref[...], sin_ref[...]
        for h in range(RET_HEADS):
            lg = _RET_LG[h]
            cols = slice(256 * h, 256 * h + 256)
            decay, decay_t, zeta, xi = _ret_decays(lg)
            qr = _rot(u_ref[:, cols], cos_v, sin_v)
            kr = _rot(u_ref[:, 1024 + 256 * h:1024 + 256 * h + 256], cos_v, sin_v) * _RET_SCALE
            v = u_ref[:, 2048 + 256 * h:2048 + 256 * h + 256]
            gate = u_ref[:, 3072 + 256 * h:3072 + 256 * h + 256]
            ng = ng_ref[:, cols]
            o = opre_ref[:, cols]
            oc = o - jnp.mean(o, axis=-1, keepdims=True)
            rr = lax.rsqrt(jnp.mean(oc * oc, axis=-1, keepdims=True) + EPS)
            ohat = oc * rr
            dout = dy_ref[:, cols]
            du_ref[:, 3072 + 256 * h:3072 + 256 * h + 256] = (dout * (ohat * ng) * _dsilu(gate)).astype(du_ref.dtype)
            don = dout * _silu(gate)
            pg_ref[0:1, cols] += jnp.sum(don * ohat, axis=0, keepdims=True)
            dohat = don * ng
            do = rr * (dohat - jnp.mean(dohat, axis=-1, keepdims=True)
                       - ohat * jnp.mean(dohat * ohat, axis=-1, keepdims=True))
            Rh = rin_ref[h]
            dRn = dR[h]
            sc_t = _dot_nt(kr, qr) * decay_t
            dv = _dot(sc_t, do) + _dot(kr * zeta, dRn)
            ds = _dot_nt(do, v) * decay
            ds_t = _dot_nt(v, do) * decay_t
            dox = do * xi
            dq = _dot(ds, kr) + _dot_nt(dox, Rh)
            dk = _dot(ds_t, qr) + zeta * _dot_nt(v, dRn)
            dR[h] = math.exp(CH * lg) * dRn + _dot(qr.T, dox)
            du_ref[:, cols] = _unrot(dq, cos_v, sin_v).astype(du_ref.dtype)
            du_ref[:, 1024 + 256 * h:1024 + 256 * h + 256] = (_unrot(dk, cos_v, sin_v) * _RET_SCALE).astype(du_ref.dtype)
            du_ref[:, 2048 + 256 * h:2048 + 256 * h + 256] = dv.astype(du_ref.dtype)

    rmap = lambda b, c: (b * nch + nch - 1 - c, 0)
    kw = dict(
        grid=(B, nch),
        in_specs=[pl.BlockSpec((CH, 1024), lambda b, c: (b * nch + nch - 1 - c, 1)),
                  pl.BlockSpec((CH, 4096), rmap), pl.BlockSpec((CH, 1024), rmap),
                  pl.BlockSpec((None, None, RET_HEADS, 256, 256), lambda b, c: (b, nch - 1 - c, 0, 0, 0)),
                  pl.BlockSpec((CH, 128), lambda b, c: (nch - 1 - c, 0)),
                  pl.BlockSpec((CH, 128), lambda b, c: (nch - 1 - c, 0)),
                  pl.BlockSpec((1, 1024), lambda b, c: (0, 0))],
        out_specs=[pl.BlockSpec((CH, 4096), rmap), pl.BlockSpec((None, 8, 1024), lambda b, c: (b, 0, 0))],
        out_shape=[jax.ShapeDtypeStruct((R, 4096), _MXU), jax.ShapeDtypeStruct((B, 8, 1024), F32)],
        scratch_shapes=[pltpu.VMEM((RET_HEADS, 256, 256), F32)])
    return _call(body, "ret_bwd", ("arbitrary", "arbitrary"), kw, (dycat, u0, opre, rin, cos, sin, norm_g), rider)


_SB_SCALE = SB_HD ** -0.5


_SB_NB = 3


def _sb_valid(qb, kb, live):
    qpos = qb * CH + jnp.bitwise_and(_row_ids(2 * CH, CH), CH - 1)
    kpos = kb * CH + _lane_ids(2 * CH, CH)
    first = PAD + (1 - live) * (1 << 24)
    return jnp.logical_and(kpos < qpos, kpos >= first)


_SB_DEAD = -100.0


def _sb_alive(acc):
    return (jnp.max(acc) > _SB_DEAD).astype(jnp.int32)


def _sb_softplus(z):
    return jnp.maximum(z, 0.0) + jnp.log(1.0 + jnp.exp(-jnp.abs(z)))


def _stack_heads(x):
    hm = _lane_ids(1, 128) < SB_HD
    return jnp.concatenate([jnp.where(hm, x, 0.0), jnp.where(hm, 0.0, x)], axis=0)


def _unstack_heads(x2):
    return jnp.where(_lane_ids(1, 128) < SB_HD, x2[:CH], x2[CH:])


def _sb_fwd(u1, B, nch, rider=None):
    R = u1.shape[0]
    Pn = nch * CH

    def body(q_ref, k_ref, v_ref, out_ref):
        qb = pl.program_id(2)
        q2 = _stack_heads(q_ref[...] * _SB_SCALE).astype(_MXU)
        mgt = (_row_ids(CH, CH) > _lane_ids(CH, CH)).astype(F32)

        def step(i, carry):
            out2, acc = carry
            blocks = []
            for t in range(_SB_NB):
                kb = qb - _SB_NB * i - t
                live = (kb >= 0).astype(jnp.int32)
                kbc = jnp.maximum(kb, 0)
                start = pl.multiple_of(kbc * CH, CH)
                valid = _sb_valid(qb, kbc, live)
                z = _dot_nt(q2, k_ref[pl.ds(start, CH), :])
                sp = _sb_softplus(z)
                lm = jnp.where(valid, -sp, 0.0)
                blocks.append((valid, z - sp, _dot_split(lm, mgt), jnp.sum(lm, axis=1, keepdims=True), start))
            for valid, ls, loc, rs, start in blocks:
                w = jnp.where(valid, jnp.exp(ls + loc + acc), 0.0)
                out2 = out2 + _dot(w, v_ref[pl.ds(start, CH), :])
                acc = acc + rs
            return out2, acc

        trips = (qb + _SB_NB) // _SB_NB

        def more(c):
            return jnp.logical_and(c[0] < trips, c[1] > 0)

        def trip(c):
            out2, acc = step(c[0], c[2:])
            return c[0] + 1, _sb_alive(acc), out2, acc

        init = (jnp.int32(0), jnp.int32(1), jnp.zeros((2 * CH, 128), F32), jnp.zeros((2 * CH, 1), F32))
        out2 = lax.while_loop(more, trip, init)[2]
        out_ref[...] = _unstack_heads(out2).astype(out_ref.dtype)

    qspec = lambda off: pl.BlockSpec((CH, 128), lambda b, hp, qb: (b * nch + qb, off + hp))
    kspec = lambda off: pl.BlockSpec((Pn, 128), lambda b, hp, qb: (b, off + hp))
    kw = dict(grid=(B, SB_HEADS // 2, nch), in_specs=[qspec(0), kspec(8), kspec(16)], out_specs=[qspec(0)],
              out_shape=[jax.ShapeDtypeStruct((R, 2048), _MXU)])
    return _call(body, "sb_fwd", ("arbitrary", "arbitrary", "arbitrary"), kw, (u1, u1, u1), rider)


def _sb_bwd(dycat, u1, B, nch, rider=None):
    R = u1.shape[0]
    Pn = nch * CH

    def body(q_ref, k_ref, v_ref, do_ref, dq_ref, dk_ref, dv_ref):
        qb = pl.program_id(2)

        @pl.when(qb == 0)
        def _():
            dk_ref[...] = jnp.zeros_like(dk_ref)
            dv_ref[...] = jnp.zeros_like(dv_ref)

        q2 = _stack_heads(q_ref[...] * _SB_SCALE)
        do2 = _stack_heads(do_ref[...])
        q2t, do2t = q2.T.astype(_MXU), do2.T.astype(_MXU)
        q2, do2 = q2.astype(_MXU), do2.astype(_MXU)
        rr = _row_ids(CH, CH)
        cc = _lane_ids(CH, CH)
        mle = (rr <= cc).astype(F32)
        mlt = (rr < cc).astype(F32)
        trips = (qb + _SB_NB) // _SB_NB

        def more(c):
            return jnp.logical_and(c[0] < trips, c[1] > 0)

        def scan(c):
            acc = c[2]
            for t in range(_SB_NB):
                kb = qb - _SB_NB * c[0] - t
                kbc = jnp.maximum(kb, 0)
                z = _dot_nt(q2, k_ref[pl.ds(pl.multiple_of(kbc * CH, CH), CH), :])
                lm = jnp.where(_sb_valid(qb, kbc, (kb >= 0).astype(jnp.int32)), -_sb_softplus(z), 0.0)
                acc = acc + jnp.sum(lm, axis=1, keepdims=True)
            return c[0] + 1, _sb_alive(acc), acc

        used, _, s2 = lax.while_loop(more, scan, (jnp.int32(0), jnp.int32(1), jnp.zeros((2 * CH, 1), F32)))
        base = qb + 1 - _SB_NB * used

        def step(i, carry):
            dq2, pacc, gacc = carry
            blocks = []
            for t in range(_SB_NB):
                kb = base + _SB_NB * i + t
                live = (kb >= 0).astype(jnp.int32)
                start = pl.multiple_of(jnp.maximum(kb, 0) * CH, CH)
                valid = _sb_valid(qb, jnp.maximum(kb, 0), live)
                z = _dot_nt(q2, k_ref[pl.ds(start, CH), :])
                sp = _sb_softplus(z)
                lm = jnp.where(valid, -sp, 0.0)
                blocks.append((valid, z - sp, _dot_split(lm, mle), jnp.sum(lm, axis=1, keepdims=True), start))
            stage = []
            for valid, ls, ploc, rs, start in blocks:
                w = jnp.where(valid, jnp.exp(ls + (s2 - (ploc + pacc))), 0.0)
                gg = _dot_nt(do2, v_ref[pl.ds(start, CH), :]) * w
                stage.append((valid, ls, w, gg, _dot_split(gg, mlt), jnp.sum(gg, axis=1, keepdims=True), start))
                pacc = pacc + rs
            for valid, ls, w, gg, gloc, gs, start in stage:
                sig = jnp.exp(ls)
                dz = jnp.where(valid, gg * (1.0 - sig) - (gloc + gacc) * sig, 0.0)
                dq2 = dq2 + _dot(dz, k_ref[pl.ds(start, CH), :])
                dk_ref[:, pl.ds(start, CH)] += _dot(q2t, dz)
                dv_ref[:, pl.ds(start, CH)] += _dot(do2t, w)
                gacc = gacc + gs
            return dq2, pacc, gacc

        zero = jnp.zeros((2 * CH, 1), F32)
        dq2 = lax.fori_loop(0, used, step, (jnp.zeros((2 * CH, 128), F32), zero, zero))[0]
        dq_ref[...] = (_unstack_heads(dq2) * _SB_SCALE).astype(dq_ref.dtype)

    qspec = lambda off: pl.BlockSpec((CH, 128), lambda b, hp, qb: (b * nch + qb, off + hp))
    kspec = lambda off: pl.BlockSpec((Pn, 128), lambda b, hp, qb: (b, off + hp))
    tspec = pl.BlockSpec((128, Pn), lambda b, hp, qb: (hp, b))
    full = jax.ShapeDtypeStruct((1024, R), F32)
    kw = dict(grid=(B, SB_HEADS // 2, nch), in_specs=[qspec(0), kspec(8), kspec(16), qspec(0)],
              out_specs=[qspec(0), tspec, tspec], out_shape=[jax.ShapeDtypeStruct((R, 1024), _MXU), full, full])
    return _call(body, "sb_bwd", ("arbitrary", "arbitrary", "arbitrary"), kw, (u1, u1, u1, dycat), rider)


def _neg_expm1(x):
    series = -(x * (1.0 + x * (0.5 + x * (1.0 / 6.0 + x * (1.0 / 24.0)))))
    return jnp.where(x > -0.05, series, 1.0 - jnp.exp(x))


def _lru_gates(x, wa_ref, ba_ref, wx_ref, bx_ref, lam_ref):
    rs, is_ = [], []
    for n in range(LRU_BLOCKS):
        xb = x[:, 128 * n:128 * n + 128]
        rs.append(_dot(xb, wa_ref[n]))
        is_.append(_dot(xb, wx_ref[n]))
    r = _sigmoid(jnp.concatenate(rs, axis=1) + ba_ref[...])
    i = _sigmoid(jnp.concatenate(is_, axis=1) + bx_ref[...])
    sp = _softplus(-lam_ref[...])
    la = -LRU_C * r * sp
    a = jnp.exp(la)
    mult = jnp.sqrt(jnp.maximum(_neg_expm1(2.0 * la), 0.0))
    return r, i, sp, a, mult


def _lru_fwd(u1, ycat, conv_w, conv_b, wa, ba, wx, bx, lam, B, nch):
    R = u1.shape[0]

    def body(x_ref, xp_ref, gate_ref, cw_ref, cb_ref, wa_ref, ba_ref, wx_ref, bx_ref, lam_ref, ycat_in,
             out_ref, hs_ref, hc):
        c = pl.program_id(1)

        @pl.when(c == 0)
        def _():
            hc[...] = jnp.zeros_like(hc)

        x = _conv_pre(xp_ref[...], x_ref[...], cw_ref, cb_ref, 4)
        r, i, sp, a, mult = _lru_gates(x, wa_ref, ba_ref, wx_ref, bx_ref, lam_ref)
        b = jnp.where(_real_rows(c), mult * (i * x), 0.0)
        rows = _row_ids(CH)
        s = 1
        while s < CH:
            a_s = jnp.where(rows >= s, pltpu.roll(a, s, axis=0), 1.0)
            b_s = jnp.where(rows >= s, pltpu.roll(b, s, axis=0), 0.0)
            b = a * b_s + b
            a = a * a_s
            s *= 2
        h = a * hc[0:1, :] + b
        hs_ref[...] = h
        hc[0:1, :] = hs_ref[CH - 1:CH, :]
        out_ref[...] = (h * _gelu(gate_ref[...])).astype(out_ref.dtype)

    row = lambda col: pl.BlockSpec((CH, 1024), lambda b, c: (b * nch + c, col))
    vec = pl.BlockSpec((1, 1024), lambda b, c: (0, 0))
    wsp = pl.BlockSpec((LRU_BLOCKS, 128, 128), lambda b, c: (0, 0, 0))
    return pl.pallas_call(
        body, name="lru_fwd", grid=(B, nch),
        in_specs=[row(4), pl.BlockSpec((8, 1024), _prev8_map(nch, 4)), row(3),
                  pl.BlockSpec((4, 1024), lambda b, c: (0, 0)), vec, wsp, vec, wsp, vec, vec,
                  pl.BlockSpec(memory_space=pl.ANY)],
        out_specs=[row(1), row(0)],
        out_shape=[jax.ShapeDtypeStruct(ycat.shape, ycat.dtype), jax.ShapeDtypeStruct((R, 1024), F32)],
        scratch_shapes=[pltpu.VMEM((8, 1024), F32)],
        input_output_aliases={10: 0},
        compiler_params=_cparams(("parallel", "arbitrary")),
    )(u1, u1, u1, conv_w, conv_b, wa, ba, wx, bx, lam, ycat)


def _lru_bwd(dycat, u1, hs, conv_w, conv_b, wa, ba, wx, bx, lam, B, nch):
    R = u1.shape[0]

    def body(dy_ref, x_ref, xp_ref, gate_ref, hs_ref, hsp_ref, cw_ref, cb_ref, wa_ref, ba_ref, wx_ref, bx_ref, lam_ref,
             dgate_ref, dxc_ref, pg_ref, dwa_ref, dwx_ref, lc):
        c = nch - 1 - pl.program_id(1)

        @pl.when(pl.program_id(1) == 0)
        def _():
            lc[...] = jnp.zeros_like(lc)
            pg_ref[...] = jnp.zeros_like(pg_ref)
            dwa_ref[...] = jnp.zeros_like(dwa_ref)
            dwx_ref[...] = jnp.zeros_like(dwx_ref)

        x = _conv_pre(xp_ref[...], x_ref[...], cw_ref, cb_ref, 4)
        r, i, sp, a, mult = _lru_gates(x, wa_ref, ba_ref, wx_ref, bx_ref, lam_ref)
        h = hs_ref[...]
        hprev = _shift_down(hsp_ref[...], h, 1)
        gate = gate_ref[...]
        dy = dy_ref[...]
        dgate_ref[...] = (dy * h * _dgelu(gate)).astype(dgate_ref.dtype)
        rows = _row_ids(CH)
        lam_t = dy * _gelu(gate) + jnp.where(rows == CH - 1, lc[0:1, :], 0.0)
        coef = jnp.where(rows < CH - 1, pltpu.roll(a, CH - 1, axis=0), 0.0)
        s = 1
        while s < CH:
            c_s = jnp.where(rows < CH - s, pltpu.roll(coef, CH - s, axis=0), 1.0)
            l_s = jnp.where(rows < CH - s, pltpu.roll(lam_t, CH - s, axis=0), 0.0)
            lam_t = coef * l_s + lam_t
            coef = coef * c_s
            s *= 2
        lc[0:1, :] = jnp.sum(jnp.where(rows == 0, a * lam_t, 0.0), axis=0, keepdims=True)
        db = jnp.where(_real_rows(c), lam_t, 0.0)
        da = db * hprev
        dmult = db * (i * x)
        di = db * mult * x
        dx = db * mult * i
        pos = mult > 0.0
        dla = da * a + jnp.where(pos, -dmult * (a * a) / jnp.where(pos, mult, 1.0), 0.0)
        dr = dla * (-LRU_C * sp)
        pg_ref[2:3, :] += jnp.sum(dla * (LRU_C * r) * _sigmoid(-lam_ref[...]), axis=0, keepdims=True)
        dpr = dr * r * (1.0 - r)
        dpi = di * i * (1.0 - i)
        pg_ref[0:1, :] += jnp.sum(dpr, axis=0, keepdims=True)
        pg_ref[1:2, :] += jnp.sum(dpi, axis=0, keepdims=True)
        dxs = []
        for n in range(LRU_BLOCKS):
            blk = slice(128 * n, 128 * n + 128)
            dxs.append(dx[:, blk] + _dot_nt(dpr[:, blk], wa_ref[n]) + _dot_nt(dpi[:, blk], wx_ref[n]))
            dwa_ref[n] += _dot_tn(x[:, blk], dpr[:, blk])
            dwx_ref[n] += _dot_tn(x[:, blk], dpi[:, blk])
        dxc_ref[...] = jnp.concatenate(dxs, axis=1)

    rmap = lambda col: (lambda b, c: (b * nch + nch - 1 - c, col))
    row = lambda col: pl.BlockSpec((CH, 1024), rmap(col))
    prev = lambda col: pl.BlockSpec(
        (8, 1024), lambda b, c: (jnp.maximum((b * nch + nch - 1 - c) * (CH // 8) - 1, 0), col))
    vec = pl.BlockSpec((1, 1024), lambda b, c: (0, 0))
    wsp = pl.BlockSpec((LRU_BLOCKS, 128, 128), lambda b, c: (0, 0, 0))
    full = jax.ShapeDtypeStruct((R, 1024), F32)
    return pl.pallas_call(
        body, name="lru_bwd", grid=(B, nch),
        in_specs=[row(1), row(4), prev(4), row(3), row(0), prev(0),
                  pl.BlockSpec((4, 1024), lambda b, c: (0, 0)), vec, wsp, vec, wsp, vec, vec],
        out_specs=[row(0), row(0), pl.BlockSpec((None, 8, 1024), lambda b, c: (b, 0, 0)),
                   pl.BlockSpec((None, LRU_BLOCKS, 128, 128), lambda b, c: (b, 0, 0, 0)),
                   pl.BlockSpec((None, LRU_BLOCKS, 128, 128), lambda b, c: (b, 0, 0, 0))],
        out_shape=[jax.ShapeDtypeStruct((R, 1024), _MXU), full, jax.ShapeDtypeStruct((B, 8, 1024), F32),
                   jax.ShapeDtypeStruct((B, LRU_BLOCKS, 128, 128), F32),
                   jax.ShapeDtypeStruct((B, LRU_BLOCKS, 128, 128), F32)],
        scratch_shapes=[pltpu.VMEM((8, 1024), F32)],
        compiler_params=_cparams(("parallel", "arbitrary")),
    )(dycat, u1, u1, u1, hs, hs, conv_w, conv_b, wa, ba, wx, bx, lam)


_FFN_TC = FFN // 2


def _ffn_specs(nch):
    nt = FFN // _FFN_TC
    row = lambda off: pl.BlockSpec((CH, _FFN_TC), lambda b, c, j: (b * nch + c, off + j))
    prev = lambda off: pl.BlockSpec(
        (8, _FFN_TC), lambda b, c, j: (jnp.maximum((b * nch + c) * (CH // 8) - 1, 0), off + j))
    wsp = lambda off: pl.BlockSpec((3, _FFN_TC), lambda b, c, j: (0, off + j))
    bsp = lambda off: pl.BlockSpec((1, _FFN_TC), lambda b, c, j: (0, off + j))
    return nt, row, [row(0), prev(0), row(nt), prev(nt), wsp(0), wsp(nt), bsp(0), bsp(nt)]


def _ffn_act_fwd(uf, conv_w, conv_b, B, nch, rider=None):
    R = uf.shape[0]
    nt, row, specs = _ffn_specs(nch)

    def body(g_ref, gp_ref, u_ref, up_ref, wg_ref, wu_ref, bg_ref, bu_ref, o_ref):
        cg = _conv_pre(gp_ref[...], g_ref[...], wg_ref, bg_ref, 3)
        cu = _conv_pre(up_ref[...], u_ref[...], wu_ref, bu_ref, 3)
        o_ref[...] = jnp.where(_real_rows(pl.program_id(1)), _silu(cg) * cu, 0.0).astype(o_ref.dtype)

    kw = dict(grid=(B, nch, nt), in_specs=specs, out_specs=[row(0)],
              out_shape=[jax.ShapeDtypeStruct((R, FFN), _MXU)])
    return _call(body, "ffn_act_fwd", ("arbitrary", "arbitrary", "arbitrary"), kw,
                 (uf, uf, uf, uf, conv_w, conv_w, conv_b, conv_b), rider)


def _ffn_act_bwd(da, uf, conv_w, conv_b, nch, name, rider=None):
    R = uf.shape[0]
    nt = FFN // _FFN_TC
    nr = R // CH
    K = 3

    def body(da_ref, dan_ref, g_ref, gp_ref, gn_ref, u_ref, up_ref, un_ref, wg_ref, wu_ref, bg_ref, bu_ref,
             dug_ref, duu_ref, dwg_ref, dwu_ref):
        i = pl.program_id(1)

        @pl.when(i == 0)
        def _():
            dwg_ref[...] = jnp.zeros_like(dwg_ref)
            dwu_ref[...] = jnp.zeros_like(dwu_ref)

        c = i % nch
        ext = CH + 8
        rows = _row_ids(ext)
        follows = (c < nch - 1).astype(jnp.int32)
        keep = jnp.logical_and(c * CH + rows >= PAD, rows < CH + 8 * follows)
        dav = jnp.where(keep, jnp.concatenate([da_ref[...], dan_ref[...]], axis=0), 0.0)

        def conv_ext(x_ref, xp_ref, xn_ref, w_ref, b_ref):
            cat = jnp.concatenate([xp_ref[...], x_ref[...], xn_ref[...]], axis=0)
            shifted = [cat[8:]] + [pltpu.roll(cat, s, axis=0)[8:] for s in range(1, K)]
            acc = shifted[0] * w_ref[K - 1:K, :] + b_ref[...]
            for s in range(1, K):
                acc = acc + shifted[s] * w_ref[K - 1 - s:K - s, :]
            return acc, shifted

        cg, gsh = conv_ext(g_ref, gp_ref, gn_ref, wg_ref, bg_ref)
        cu, ush = conv_ext(u_ref, up_ref, un_ref, wu_ref, bu_ref)
        sg = _sigmoid(cg)
        dcg = dav * cu * (sg * (1.0 + cg * (1.0 - sg)))
        dcu = dav * (cg * sg)
        for dc, xsh, w_ref, din_ref, dw_ref in ((dcg, gsh, wg_ref, dug_ref, dwg_ref), (dcu, ush, wu_ref, duu_ref, dwu_ref)):
            dp = dc[:CH]
            din = dp * w_ref[K - 1:K, :]
            dw_ref[7:8, :] += jnp.sum(dp, axis=0, keepdims=True)
            dw_ref[K - 1:K, :] += jnp.sum(dp * xsh[0][:CH], axis=0, keepdims=True)
            for s in range(1, K):
                din = din + pltpu.roll(dc, ext - s, axis=0)[:CH] * w_ref[K - 1 - s:K - s, :]
                dw_ref[K - 1 - s:K - s, :] += jnp.sum(dp * xsh[s][:CH], axis=0, keepdims=True)
            din_ref[...] = din.astype(din_ref.dtype)

    row = lambda off: pl.BlockSpec((CH, _FFN_TC), lambda j, i: (i, off + j))
    prev = lambda off: pl.BlockSpec((8, _FFN_TC), lambda j, i: (jnp.maximum(i * (CH // 8) - 1, 0), off + j))
    nxt = lambda off: pl.BlockSpec(
        (8, _FFN_TC), lambda j, i: (jnp.minimum((i + 1) * (CH // 8), nr * (CH // 8) - 1), off + j))
    wsp = lambda off: pl.BlockSpec((K, _FFN_TC), lambda j, i: (0, off + j))
    bsp = lambda off: pl.BlockSpec((1, _FFN_TC), lambda j, i: (0, off + j))
    acc = pl.BlockSpec((8, _FFN_TC), lambda j, i: (0, j))
    half = jax.ShapeDtypeStruct((R, FFN), _MXU)
    dwsh = jax.ShapeDtypeStruct((8, FFN), F32)
    kw = dict(
        grid=(nt, nr),
        in_specs=[row(0), nxt(0), row(0), prev(0), nxt(0), row(nt), prev(nt), nxt(nt), wsp(0), wsp(nt), bsp(0), bsp(nt)],
        out_specs=[row(0), row(0), acc, acc],
        out_shape=[half, half, dwsh, dwsh])
    return _call(body, name, ("arbitrary", "arbitrary"), kw,
                 (da, da, uf, uf, uf, uf, uf, uf, conv_w, conv_w, conv_b, conv_b), rider)


def _head(h, g, target, B, nch):
    R = h.shape[0]

    def body(h_ref, g_ref, t_ref, dh_ref, loss_ref, dg_ref):
        c = pl.program_id(1)

        @pl.when(c == 0)
        def _():
            dh_ref[...] = jnp.zeros_like(dh_ref)
            loss_ref[...] = jnp.zeros_like(loss_ref)
            dg_ref[...] = jnp.zeros_like(dg_ref)

        @pl.when(c > 0)
        def _():
            x = h_ref[...]
            gv = g_ref[...]
            r = lax.rsqrt(jnp.mean(x * x, axis=-1, keepdims=True) + EPS)
            xhat = x * r
            e = xhat * gv - t_ref[...]
            loss_ref[...] += 0.5 * jnp.sum(jnp.mean(e * e, axis=-1, keepdims=True), axis=0, keepdims=True)
            dy = e * (1.0 / D)
            dg_ref[0:1, :] += jnp.sum(dy * xhat, axis=0, keepdims=True)
            dx = dy * gv
            dh_ref[...] = r * (dx - xhat * jnp.mean(dx * xhat, axis=-1, keepdims=True))

    row = pl.BlockSpec((CH, D), lambda b, c: (b * nch + c, 0))
    return pl.pallas_call(
        body, name="head", grid=(B, nch),
        in_specs=[row, pl.BlockSpec((1, D), lambda b, c: (0, 0)),
                  pl.BlockSpec((CH, D), lambda b, c: (b * (nch - 1) + jnp.maximum(c - 1, 0), 0))],
        out_specs=[row, pl.BlockSpec((None, 8, 128), lambda b, c: (b, 0, 0)),
                   pl.BlockSpec((None, 8, D), lambda b, c: (b, 0, 0))],
        out_shape=[jax.ShapeDtypeStruct((R, D), F32), jax.ShapeDtypeStruct((B, 8, 128), F32),
                   jax.ShapeDtypeStruct((B, 8, D), F32)],
        compiler_params=_cparams(("parallel", "arbitrary")),
    )(h, g, target)


ADAM_LR = 0.001
ADAM_B1 = 0.9
ADAM_B2 = 0.999
ADAM_EPS = 1e-08
ADAM_WD = 0.01
ADAM_STEP = 10


def _adamw(w, g, m, v, name):
    Rr, C = w.shape
    tr = _tile(Rr, (256, 64))

    def body(w_ref, g_ref, m_ref, v_ref, d_ref, nm_ref, nv_ref):
        gv = g_ref[...]
        nm = ADAM_B1 * m_ref[...] + (1.0 - ADAM_B1) * gv
        nv = ADAM_B2 * v_ref[...] + (1.0 - ADAM_B2) * (gv * gv)
        m_hat = nm / (1.0 - ADAM_B1 ** ADAM_STEP)
        v_hat = nv / (1.0 - ADAM_B2 ** ADAM_STEP)
        d_ref[...] = -ADAM_LR * (m_hat / (jnp.sqrt(v_hat) + ADAM_EPS) + ADAM_WD * w_ref[...])
        nm_ref[...] = nm
        nv_ref[...] = nv

    spec = pl.BlockSpec((tr, C), lambda i: (i, 0))
    sh = jax.ShapeDtypeStruct((Rr, C), F32)
    return pl.pallas_call(
        body, name=name, grid=(Rr // tr,),
        in_specs=[spec] * 4, out_specs=[spec] * 3, out_shape=[sh] * 3,
        compiler_params=_cparams(("parallel",)),
    )(w, g, m, v)


_MESH = pl.DeviceIdType.MESH
_ANY = pl.BlockSpec(memory_space=pl.ANY)


def _place():
    x, y, c = lax.axis_index("x"), lax.axis_index("y"), lax.axis_index("c")
    chips = [(1 - x, y), (x, 1 - y), (1 - x, 1 - y)]
    return x, y, c, chips


def _rcopy(src, dst, ssem, rsem, dev):
    return pltpu.make_async_remote_copy(src_ref=src, dst_ref=dst, send_sem=ssem, recv_sem=rsem,
                                        device_id=dev, device_id_type=_MESH)


def _with_riders(body, kw, kind, riders):
    n_in, n_out, n_scr = len(kw["in_specs"]), len(kw["out_specs"]), len(kw.get("scratch_shapes", []))
    grid = kw["grid"]
    nr = len(riders)
    nsem = 4 if kind == "gather" else 2

    def new_body(*refs):
        ins, srcs = refs[:n_in], refs[n_in:n_in + nr]
        outs, dsts = refs[n_in + nr:n_in + nr + n_out], refs[n_in + nr + n_out:n_in + 2 * nr + n_out]
        scr = refs[n_in + 2 * nr + n_out:n_in + 2 * nr + n_out + n_scr]
        sems = refs[n_in + 2 * nr + n_out + n_scr:]
        first = last = None
        for axis, size in enumerate(grid):
            i = pl.program_id(axis)
            first = (i == 0) if first is None else jnp.logical_and(first, i == 0)
            last = (i == size - 1) if last is None else jnp.logical_and(last, i == size - 1)
        x, y, c, chips = _place()
        k = 2 * x + y
        sib = (x, y, 1 - c)
        ssem, rsem = sems[:2]
        sends = []
        for a in range(nr):
            for j, (cx, cy) in enumerate(chips):
                if kind == "gather":
                    src, dst = srcs[a].at[c], dsts[a].at[k, c]
                else:
                    src, dst = srcs[a].at[2 * cx + cy], dsts[a].at[k]
                sends.append(_rcopy(src, dst, ssem.at[3 * a + j], rsem.at[3 * a + j], (cx, cy, c)))

        @pl.when(first)
        def _():
            for cp in sends:
                cp.start()

        body(*ins, *outs, *scr)

        @pl.when(last)
        def _():
            passed = []
            for a in range(nr):
                for j, (cx, cy) in enumerate(chips):
                    got = dsts[a].at[2 * cx + cy, c] if kind == "gather" else dsts[a].at[2 * cx + cy]
                    _rcopy(got, got, ssem.at[3 * a + j], rsem.at[3 * a + j], (cx, cy, c)).wait_recv()
                    if kind == "gather":
                        fw = _rcopy(got, got, sems[2].at[3 * a + j], sems[3].at[3 * a + j], sib)
                        fw.start()
                        passed.append(fw)
            if kind == "gather":
                for a in range(nr):
                    for j, (cx, cy) in enumerate(chips):
                        got = dsts[a].at[2 * cx + cy, 1 - c]
                        _rcopy(got, got, sems[2].at[3 * a + j], sems[3].at[3 * a + j], sib).wait_recv()
            for cp in sends + passed:
                cp.wait_send()

    kw = dict(kw)
    kw["in_specs"] = list(kw["in_specs"]) + [_ANY] * nr
    kw["out_specs"] = list(kw["out_specs"]) + [_ANY] * nr
    kw["out_shape"] = list(kw["out_shape"]) + [
        jax.ShapeDtypeStruct(((4,) + r.shape) if kind == "gather" else r.shape, r.dtype) for r in riders]
    kw["scratch_shapes"] = list(kw.get("scratch_shapes", [])) + [pltpu.SemaphoreType.DMA((3 * nr,))] * nsem
    return new_body, kw


def _call(body, name, sem, kw, args, rider=None):
    if rider is not None:
        body, kw = _with_riders(body, kw, *rider)
        args = tuple(args) + tuple(rider[1])
    return pl.pallas_call(body, name=name, compiler_params=_cparams(sem), **kw)(*args)


def _fill_own(result, own, chip):
    return lax.dynamic_update_index_in_dim(result, own, chip, 0)


def _gather_shards(bigs, small):
    nb = len(bigs)

    def body(*refs):
        ins, outs = refs[:nb + 1], refs[nb + 1:2 * nb + 2]
        ssem, rsem, fssem, frsem = refs[2 * nb + 2:]
        x, y, c, chips = _place()
        k = 2 * x + y
        sib = (x, y, 1 - c)

        def part(a, slot, hc):
            return outs[a].at[slot] if a == nb else outs[a].at[slot, hc]

        first = []
        for a in range(nb + 1):
            src = ins[a] if a == nb else ins[a].at[c]
            for j, (cx, cy) in enumerate(chips):
                first.append(_rcopy(src, part(a, k, c), ssem.at[3 * a + j], rsem.at[3 * a + j], (cx, cy, c)))
        for cp in first:
            cp.start()
        passed = []
        for a in range(nb + 1):
            for j, (cx, cy) in enumerate(chips):
                got = part(a, 2 * cx + cy, c)
                _rcopy(got, got, ssem.at[3 * a + j], rsem.at[3 * a + j], (cx, cy, c)).wait_recv()
                if a < nb:
                    fw = _rcopy(got, got, fssem.at[3 * a + j], frsem.at[3 * a + j], sib)
                    fw.start()
                    passed.append(fw)
        for a in range(nb):
            for j, (cx, cy) in enumerate(chips):
                got = part(a, 2 * cx + cy, 1 - c)
                _rcopy(got, got, fssem.at[3 * a + j], frsem.at[3 * a + j], sib).wait_recv()
        for cp in first + passed:
            cp.wait_send()

    arrs = list(bigs) + [small]
    n = 3 * (nb + 1)
    return pl.pallas_call(
        body, name="gather_shards",
        in_specs=[_ANY] * (nb + 1), out_specs=[_ANY] * (nb + 1),
        out_shape=[jax.ShapeDtypeStruct((4,) + a.shape, a.dtype) for a in arrs],
        scratch_shapes=[pltpu.SemaphoreType.DMA((n,)), pltpu.SemaphoreType.DMA((n,)),
                        pltpu.SemaphoreType.DMA((n,)), pltpu.SemaphoreType.DMA((n,))],
    )(*arrs)


def _swap_halves(grads, name):
    na = len(grads)
    halves = [g.shape[1] // 2 for g in grads]

    def body(*refs):
        ins, outs = refs[:na], refs[na:2 * na]
        ssem, rsem = refs[2 * na:]
        x, y, c, _ = _place()
        sib = (x, y, 1 - c)
        cps = [_rcopy(ins[a].at[:, pl.ds((1 - c) * halves[a], halves[a]), :], outs[a], ssem.at[a], rsem.at[a], sib)
               for a in range(na)]
        for cp in cps:
            cp.start()
        for cp in cps:
            cp.wait()

    return pl.pallas_call(
        body, name=name,
        in_specs=[_ANY] * na, out_specs=[_ANY] * na,
        out_shape=[jax.ShapeDtypeStruct((4, g.shape[1] // 2, g.shape[2]), g.dtype) for g in grads],
        scratch_shapes=[pltpu.SemaphoreType.DMA((na,)), pltpu.SemaphoreType.DMA((na,))],
    )(*grads)


def _sum_rows(rh):
    return rh if rh <= 512 else _tile(rh, (512, 256, 128, 64, 32))


def _chip_sum(grad, recv, core, name):
    _, r, cdim = grad.shape
    rh = r // 2
    tr = _sum_rows(rh)
    nblk = rh // tr

    def body(core_ref, g_ref, r_ref, o_ref):
        o_ref[...] = (g_ref[...] + r_ref[...]).astype(o_ref.dtype)

    return pl.pallas_call(
        body, name=name,
        grid_spec=pltpu.PrefetchScalarGridSpec(
            num_scalar_prefetch=1, grid=(4, nblk),
            in_specs=[pl.BlockSpec((None, tr, cdim), lambda s, i, cr: (s, cr[0] * nblk + i, 0)),
                      pl.BlockSpec((None, tr, cdim), lambda s, i, cr: (s, i, 0))],
            out_specs=pl.BlockSpec((None, tr, cdim), lambda s, i, cr: (s, i, 0))),
        out_shape=jax.ShapeDtypeStruct((4, rh, cdim), BF16),
        compiler_params=_cparams(("parallel", "parallel")),
    )(core, grad, recv)


def _scatter_sums(sums):
    na = len(sums)

    def body(*refs):
        ins, outs = refs[:na], refs[na:2 * na]
        ssem, rsem, lsem = refs[2 * na:]
        x, y, c, chips = _place()
        k = 2 * x + y
        local = [pltpu.make_async_copy(ins[a].at[k], outs[a].at[k], lsem.at[a]) for a in range(na)]
        for cp in local:
            cp.start()
        cps = []
        for a in range(na):
            for j, (cx, cy) in enumerate(chips):
                cps.append(_rcopy(ins[a].at[2 * cx + cy], outs[a].at[k], ssem.at[3 * a + j], rsem.at[3 * a + j],
                                  (cx, cy, c)))
        for cp in cps:
            cp.start()
        for a in range(na):
            for j, (cx, cy) in enumerate(chips):
                got = outs[a].at[2 * cx + cy]
                _rcopy(got, got, ssem.at[3 * a + j], rsem.at[3 * a + j], (cx, cy, c)).wait_recv()
        for cp in cps:
            cp.wait_send()
        for cp in local:
            cp.wait()

    return pl.pallas_call(
        body, name="scatter_sums",
        in_specs=[_ANY] * na, out_specs=[_ANY] * na,
        out_shape=[jax.ShapeDtypeStruct(s.shape, s.dtype) for s in sums],
        scratch_shapes=[pltpu.SemaphoreType.DMA((3 * na,)), pltpu.SemaphoreType.DMA((3 * na,)),
                        pltpu.SemaphoreType.DMA((na,))],
    )(*sums)


def _sum_chips(parts, name):
    _, rh, cdim = parts.shape
    tr = _sum_rows(rh)

    def body(p_ref, o_ref):
        acc = p_ref[0].astype(F32)
        for j in range(1, 4):
            acc = acc + p_ref[j].astype(F32)
        o_ref[...] = acc

    return pl.pallas_call(
        body, name=name, grid=(rh // tr,),
        in_specs=[pl.BlockSpec((4, tr, cdim), lambda i: (0, i, 0))],
        out_specs=pl.BlockSpec((tr, cdim), lambda i: (i, 0)),
        out_shape=jax.ShapeDtypeStruct((rh, cdim), F32),
        compiler_params=_cparams(("parallel",)),
    )(parts)


def _join_halves(reds):
    na = len(reds)

    def body(*refs):
        ins, outs = refs[:na], refs[na:2 * na]
        ssem, rsem = refs[2 * na:]
        x, y, c, _ = _place()
        cps = [_rcopy(ins[a], outs[a], ssem.at[a], rsem.at[a], (x, y, 1 - c)) for a in range(na)]
        for cp in cps:
            cp.start()
        for cp in cps:
            cp.wait()

    return pl.pallas_call(
        body, name="join_halves",
        in_specs=[_ANY] * na, out_specs=[_ANY] * na,
        out_shape=[jax.ShapeDtypeStruct(r.shape, r.dtype) for r in reds],
        scratch_shapes=[pltpu.SemaphoreType.DMA((na,)), pltpu.SemaphoreType.DMA((na,))],
    )(*reds)


def _allreduce_small(buf):
    n = buf.shape[0]

    def body(in_ref, out_ref, recv, ssem, rsem):
        x, y, c, _ = _place()
        peers = [(x, y, 1 - c), (1 - x, y, c), (x, 1 - y, c)]
        out_ref[...] = in_ref[...]
        for r, peer in enumerate(peers):
            cp = _rcopy(out_ref, recv.at[r], ssem.at[r], rsem.at[r], peer)
            cp.start()
            cp.wait()
            out_ref[...] = out_ref[...] + recv[r]

    vm = pl.BlockSpec(memory_space=pltpu.VMEM)
    return pl.pallas_call(
        body, name="allreduce_small",
        in_specs=[vm], out_specs=vm,
        out_shape=jax.ShapeDtypeStruct(buf.shape, F32),
        scratch_shapes=[pltpu.VMEM((3, n, 128), F32), pltpu.SemaphoreType.DMA((3,)), pltpu.SemaphoreType.DMA((3,))],
        compiler_params=pltpu.CompilerParams(vmem_limit_bytes=VMEM_LIMIT),
    )(buf)


_W_NAMES = ['meta_tokens', 'l0_mix_norm', 'l0_w_in', 'l0_ssd_conv_w', 'l0_ssd_conv_b', 'l0_ssd_dt_bias', 'l0_ssd_a_log',
            'l0_ssd_d', 'l0_ssd_norm', 'l0_ret_norm', 'l0_w_out', 'l0_ffn_norm', 'l0_ffn_w_in', 'l0_ffn_conv_w',
            'l0_ffn_conv_b', 'l0_ffn_w_out', 'l1_mix_norm', 'l1_w_in', 'l1_lru_conv_w', 'l1_lru_conv_b', 'l1_lru_wa',
            'l1_lru_ba', 'l1_lru_wx', 'l1_lru_bx', 'l1_lru_lambda', 'l1_w_out', 'l1_ffn_norm', 'l1_ffn_w_in',
            'l1_ffn_conv_w', 'l1_ffn_conv_b', 'l1_ffn_w_out', 'final_norm']
_IN_NAMES = ['x'] + _W_NAMES + ['loss_target'] + ['m_' + n for n in _W_NAMES] + ['v_' + n for n in _W_NAMES]
_BIG = ['l0_w_in', 'l0_w_out', 'l0_ffn_w_in', 'l0_ffn_w_out', 'l1_w_in', 'l1_w_out', 'l1_ffn_w_in', 'l1_ffn_w_out']
_BIG_COLS = ('l0_w_in', 'l0_ffn_w_in', 'l1_w_in', 'l1_ffn_w_in')
_SMALL_SHARDED = ['meta_tokens', 'l0_ssd_conv_w', 'l0_ffn_conv_w', 'l1_lru_conv_w', 'l1_ffn_conv_w']
_SMALL = [n for n in _W_NAMES if n not in _BIG]


def _pack(arrs):
    flat = []
    for a in arrs:
        v = a.reshape(-1).astype(F32)
        flat.append(jnp.pad(v, (0, (-v.shape[0]) % 128)))
    v = jnp.concatenate(flat)
    v = jnp.pad(v, (0, (-v.shape[0]) % 1024))
    return v.reshape(-1, 128)


def _unpack(buf, shapes):
    out, row = [], 0
    for sh in shapes:
        n = int(np.prod(sh))
        rows = -(-n // 128)
        out.append(buf[row:row + rows].reshape(-1)[:n].reshape(sh))
        row += rows
    return out


def kernel(x, meta_tokens, l0_mix_norm, l0_w_in, l0_ssd_conv_w, l0_ssd_conv_b, l0_ssd_dt_bias, l0_ssd_a_log, l0_ssd_d, l0_ssd_norm, l0_ret_norm, l0_w_out, l0_ffn_norm, l0_ffn_w_in, l0_ffn_conv_w, l0_ffn_conv_b, l0_ffn_w_out, l1_mix_norm, l1_w_in, l1_lru_conv_w, l1_lru_conv_b, l1_lru_wa, l1_lru_ba, l1_lru_wx, l1_lru_bx, l1_lru_lambda, l1_w_out, l1_ffn_norm, l1_ffn_w_in, l1_ffn_conv_w, l1_ffn_conv_b, l1_ffn_w_out, final_norm, loss_target, m_meta_tokens, m_l0_mix_norm, m_l0_w_in, m_l0_ssd_conv_w, m_l0_ssd_conv_b, m_l0_ssd_dt_bias, m_l0_ssd_a_log, m_l0_ssd_d, m_l0_ssd_norm, m_l0_ret_norm, m_l0_w_out, m_l0_ffn_norm, m_l0_ffn_w_in, m_l0_ffn_conv_w, m_l0_ffn_conv_b, m_l0_ffn_w_out, m_l1_mix_norm, m_l1_w_in, m_l1_lru_conv_w, m_l1_lru_conv_b, m_l1_lru_wa, m_l1_lru_ba, m_l1_lru_wx, m_l1_lru_bx, m_l1_lru_lambda, m_l1_w_out, m_l1_ffn_norm, m_l1_ffn_w_in, m_l1_ffn_conv_w, m_l1_ffn_conv_b, m_l1_ffn_w_out, m_final_norm, v_meta_tokens, v_l0_mix_norm, v_l0_w_in, v_l0_ssd_conv_w, v_l0_ssd_conv_b, v_l0_ssd_dt_bias, v_l0_ssd_a_log, v_l0_ssd_d, v_l0_ssd_norm, v_l0_ret_norm, v_l0_w_out, v_l0_ffn_norm, v_l0_ffn_w_in, v_l0_ffn_conv_w, v_l0_ffn_conv_b, v_l0_ffn_w_out, v_l1_mix_norm, v_l1_w_in, v_l1_lru_conv_w, v_l1_lru_conv_b, v_l1_lru_wa, v_l1_lru_ba, v_l1_lru_wx, v_l1_lru_bx, v_l1_lru_lambda, v_l1_w_out, v_l1_ffn_norm, v_l1_ffn_w_in, v_l1_ffn_conv_w, v_l1_ffn_conv_b, v_l1_ffn_w_out, v_final_norm):
    args = (x, meta_tokens, l0_mix_norm, l0_w_in, l0_ssd_conv_w, l0_ssd_conv_b, l0_ssd_dt_bias, l0_ssd_a_log, l0_ssd_d, l0_ssd_norm, l0_ret_norm, l0_w_out, l0_ffn_norm, l0_ffn_w_in, l0_ffn_conv_w, l0_ffn_conv_b, l0_ffn_w_out, l1_mix_norm, l1_w_in, l1_lru_conv_w, l1_lru_conv_b, l1_lru_wa, l1_lru_ba, l1_lru_wx, l1_lru_bx, l1_lru_lambda, l1_w_out, l1_ffn_norm, l1_ffn_w_in, l1_ffn_conv_w, l1_ffn_conv_b, l1_ffn_w_out, final_norm, loss_target, m_meta_tokens, m_l0_mix_norm, m_l0_w_in, m_l0_ssd_conv_w, m_l0_ssd_conv_b, m_l0_ssd_dt_bias, m_l0_ssd_a_log, m_l0_ssd_d, m_l0_ssd_norm, m_l0_ret_norm, m_l0_w_out, m_l0_ffn_norm, m_l0_ffn_w_in, m_l0_ffn_conv_w, m_l0_ffn_conv_b, m_l0_ffn_w_out, m_l1_mix_norm, m_l1_w_in, m_l1_lru_conv_w, m_l1_lru_conv_b, m_l1_lru_wa, m_l1_lru_ba, m_l1_lru_wx, m_l1_lru_bx, m_l1_lru_lambda, m_l1_w_out, m_l1_ffn_norm, m_l1_ffn_w_in, m_l1_ffn_conv_w, m_l1_ffn_conv_b, m_l1_ffn_w_out, m_final_norm, v_meta_tokens, v_l0_mix_norm, v_l0_w_in, v_l0_ssd_conv_w, v_l0_ssd_conv_b, v_l0_ssd_dt_bias, v_l0_ssd_a_log, v_l0_ssd_d, v_l0_ssd_norm, v_l0_ret_norm, v_l0_w_out, v_l0_ffn_norm, v_l0_ffn_w_in, v_l0_ffn_conv_w, v_l0_ffn_conv_b, v_l0_ffn_w_out, v_l1_mix_norm, v_l1_w_in, v_l1_lru_conv_w, v_l1_lru_conv_b, v_l1_lru_wa, v_l1_lru_ba, v_l1_lru_wx, v_l1_lru_bx, v_l1_lru_lambda, v_l1_w_out, v_l1_ffn_norm, v_l1_ffn_w_in, v_l1_ffn_conv_w, v_l1_ffn_conv_b, v_l1_ffn_w_out, v_final_norm)
    p = dict(zip(_IN_NAMES, args))
    B, seq, _ = x.shape
    nch = (seq + CH) // CH
    Pn = nch * CH
    R = B * Pn
    chip = 2 * lax.axis_index("x") + lax.axis_index("y")
    row2 = lambda v: v.reshape(1, -1)
    pad128 = lambda v: jnp.pad(v, (0, 128 - v.shape[0])).reshape(1, 128)

    small_shapes = [p[n].shape for n in _SMALL_SHARDED]
    halved = lambda w: w.astype(_MXU).reshape(2, w.shape[0] // 2, w.shape[1])
    mine = {n: halved(p[n]) for n in _BIG}
    mine_small = _pack([p[n] for n in _SMALL_SHARDED])
    W = {}

    def set_weight(n, g):
        g = _fill_own(g, mine[n], chip)
        g = g.reshape(4, -1, g.shape[3])
        W[n] = jnp.concatenate([g[k] for k in range(4)], axis=1) if n in _BIG_COLS else g.reshape(-1, g.shape[2])

    def gather_on(*names):
        return ("gather", [mine[n] for n in names])

    def take_weights(names, got):
        for n, g in zip(names, got):
            set_weight(n, g)

    gathered = _gather_shards([mine['l0_w_in']], mine_small)
    set_weight('l0_w_in', gathered[0])
    g_small = _fill_own(gathered[-1], mine_small, chip)
    per_chip = [_unpack(g_small[k], small_shapes) for k in range(4)]
    for i, n in enumerate(_SMALL_SHARDED):
        W[n] = jnp.concatenate([per_chip[k][i] for k in range(4)], axis=1)
    w0 = W['l0_w_in']
    w0_main = jnp.concatenate([w0[:, 3088:], w0[:, :3072]], axis=1)
    w0_dt = jnp.pad(w0[:, 3072:3088], ((0, 0), (0, 112)))
    cos, sin = _rope_tables(nch)

    meta = jnp.broadcast_to(W['meta_tokens'][None], (B, N_META, D))
    h0 = jnp.concatenate([jnp.zeros((B, PAD, D), F32), meta, x], axis=1).reshape(R, D)
    n0, n0t = _rmsnorm_fwd(h0, row2(p['l0_mix_norm']), "norm_l0_mix")
    u0 = _mm(n0, w0_main, "nn", F32, "l0_in_proj")
    udt = _mm(n0, w0_dt, "nn", F32, "l0_dt_proj")
    a_log, d_skip, dt_bias = pad128(p['l0_ssd_a_log']), pad128(p['l0_ssd_d']), pad128(p['l0_ssd_dt_bias'])
    ssd_cb = row2(p['l0_ssd_conv_b'])
    act, dt, dtt, *got = _ssd_prep(u0, udt, W['l0_ssd_conv_w'], ssd_cb, dt_bias, B, nch, rider=gather_on('l0_w_out'))
    take_weights(['l0_w_out'], got)
    ycat0, ypre, hin, *got = _ssd_fwd(act, u0, dt, dtt, a_log, d_skip, row2(p['l0_ssd_norm']), B, nch,
                                      rider=gather_on('l0_ffn_w_in'))
    take_weights(['l0_ffn_w_in'], got)
    ycat0, opre, rin, *got = _ret_fwd(u0, ycat0, cos, sin, row2(p['l0_ret_norm']), B, nch,
                                      rider=gather_on('l0_ffn_w_out'))
    take_weights(['l0_ffn_w_out'], got)
    h1 = _mm(ycat0, W['l0_w_out'], "nn", F32, "l0_out_proj", add=h0)
    n1, n1t = _rmsnorm_fwd(h1, row2(p['l0_ffn_norm']), "norm_l0_ffn")
    uf0 = _mm(n1, W['l0_ffn_w_in'], "nn", F32, "l0_ffn_in")
    f0_cb = row2(p['l0_ffn_conv_b'])
    a0, *got = _ffn_act_fwd(uf0, W['l0_ffn_conv_w'], f0_cb, B, nch, rider=gather_on('l1_w_in'))
    take_weights(['l1_w_in'], got)
    h2 = _mm(a0, W['l0_ffn_w_out'], "nn", F32, "l0_ffn_out", add=h1)
    n2, n2t = _rmsnorm_fwd(h2, row2(p['l1_mix_norm']), "norm_l1_mix")
    u1 = _mm(n2, W['l1_w_in'], "nn", F32, "l1_in_proj")
    lru = (W['l1_lru_conv_w'], row2(p['l1_lru_conv_b']), p['l1_lru_wa'], row2(p['l1_lru_ba']), p['l1_lru_wx'],
           row2(p['l1_lru_bx']), row2(p['l1_lru_lambda']))
    later = ['l1_w_out', 'l1_ffn_w_in', 'l1_ffn_w_out']
    ycat1, *got = _sb_fwd(u1, B, nch, rider=gather_on(*later))
    take_weights(later, got)
    ycat1, hs = _lru_fwd(u1, ycat1, *lru, B, nch)
    h3 = _mm(ycat1, W['l1_w_out'], "nn", F32, "l1_out_proj", add=h2)
    n3, n3t = _rmsnorm_fwd(h3, row2(p['l1_ffn_norm']), "norm_l1_ffn")
    uf1 = _mm(n3, W['l1_ffn_w_in'], "nn", F32, "l1_ffn_in")
    f1_cb = row2(p['l1_ffn_conv_b'])
    a1, = _ffn_act_fwd(uf1, W['l1_ffn_conv_w'], f1_cb, B, nch)
    h4 = _mm(a1, W['l1_ffn_w_out'], "nn", F32, "l1_ffn_out", add=h3)
    dh4, lossp, dgf = _head(h4, row2(p['final_norm']), p['loss_target'].reshape(B * seq, D), B, nch)
    loss = lax.psum(jnp.sum(lossp[:, 0, 0]), ("x", "y", "c"))

    G = {'final_norm': dgf[:, 0].sum(0)}

    core = lax.axis_index("c").reshape(1).astype(jnp.int32)

    def chip_sums(names, tag):
        stacked = []
        for n in names:
            g = G[n]
            if n in _BIG_COLS:
                stacked.append(g.reshape(g.shape[0], 4, g.shape[1] // 4).transpose(1, 0, 2))
            else:
                stacked.append(g.reshape(4, g.shape[0] // 4, g.shape[1]))
        theirs = _swap_halves(stacked, "swap_halves_" + tag)
        return {n: _chip_sum(g, t, core, "chip_sum_" + n) for n, g, t in zip(names, stacked, theirs)}

    parts = {}

    def scatter_on(names, tag):
        sums = chip_sums(names, tag)
        return sums, ("scatter", [sums[n] for n in names])

    def take_parts(names, sums, got):
        for n, g in zip(names, got):
            parts[n] = _fill_own(g, lax.dynamic_index_in_dim(sums[n], chip, 0, keepdims=False), chip)

    def ffn_bwd(layer, dh_out, h_in, nt_in, uf, a_act, cb, rider=None):
        pre = f"l{layer}_"
        w_in, w_out, cw = W[pre + 'ffn_w_in'], W[pre + 'ffn_w_out'], W[pre + 'ffn_conv_w']
        da = _mm(dh_out, w_out, "nt", F32, pre + "ffn_out_dgrad")
        G[pre + 'ffn_w_out'] = _mm(a_act, dh_out, "tn", F32, pre + "ffn_out_wgrad")
        dug, duu, dwg, dwu, *rode = _ffn_act_bwd(da, uf, cw, cb, nch, pre + "ffn_act_bwd", rider=rider)
        G[pre + 'ffn_conv_w'] = jnp.concatenate([dwg[:3], dwu[:3]], axis=1)
        G[pre + 'ffn_conv_b'] = jnp.concatenate([dwg[7], dwu[7]])
        dn = _mm(dug, w_in, "nt", F32, pre + "ffn_in_dgrad_g")
        dn = _mm(duu, w_in, "nt", F32, pre + "ffn_in_dgrad_u", add=dn, b_off=FFN)
        G[pre + 'ffn_w_in'] = jnp.concatenate([_mm(nt_in, dug, "nn", F32, pre + "ffn_in_wgrad_g"),
                                               _mm(nt_in, duu, "nn", F32, pre + "ffn_in_wgrad_u")], axis=1)
        dh_in, dg = _rmsnorm_bwd(h_in, row2(p[pre + 'ffn_norm']), dn, dh_out, nch, pre + "ffn_norm_bwd")
        G[pre + 'ffn_norm'] = dg[0]
        return dh_in, rode

    dh3, _ = ffn_bwd(1, dh4, h3, n3t, uf1, a1, f1_cb)
    dy1 = _mm(dh3, W['l1_w_out'], "nt", F32, "l1_out_dgrad")
    G['l1_w_out'] = _mm(ycat1, dh3, "tn", F32, "l1_out_wgrad")
    done = ['l1_ffn_w_in', 'l1_ffn_w_out', 'l1_w_out']
    sums, rider = scatter_on(done, "a")
    dq, dkt, dvt, *got = _sb_bwd(dy1, u1, B, nch, rider=rider)
    dk, dv = dkt.T, dvt.T
    take_parts(done, sums, got)
    dgate, dxc, pgl, dwa, dwx = _lru_bwd(dy1, u1, hs, *lru, B, nch)
    dxr, dcw = _conv_bwd(dxc, u1, 4096, W['l1_lru_conv_w'], 4, "l1_lru_conv_bwd")
    pgl = pgl.sum(0)
    G['l1_lru_ba'], G['l1_lru_bx'], G['l1_lru_lambda'] = pgl[0], pgl[1], pgl[2]
    G['l1_lru_wa'], G['l1_lru_wx'] = dwa.sum(0), dwx.sum(0)
    G['l1_lru_conv_w'], G['l1_lru_conv_b'] = dcw[:4], dcw[7]
    dn, dws = None, []
    for i, piece in enumerate((dq, dk, dv, dgate, dxr)):
        dn = _mm(piece, W['l1_w_in'], "nt", F32, f"l1_in_dgrad_{i}", add=dn, b_off=1024 * i)
        dws.append(_mm(n2t, piece, "nn", F32, f"l1_in_wgrad_{i}"))
    G['l1_w_in'] = jnp.concatenate(dws, axis=1)
    dh2, dg = _rmsnorm_bwd(h2, row2(p['l1_mix_norm']), dn, dh3, nch, "l1_mix_norm_bwd")
    G['l1_mix_norm'] = dg[0]

    sums, rider = scatter_on(['l1_w_in'], "b")
    dh1, got = ffn_bwd(0, dh2, h1, n1t, uf0, a0, f0_cb, rider=rider)
    take_parts(['l1_w_in'], sums, got)
    dy0 = _mm(dh1, W['l0_w_out'], "nt", F32, "l0_out_dgrad")
    G['l0_w_out'] = _mm(ycat0, dh1, "tn", F32, "l0_out_wgrad")
    done = ['l0_ffn_w_in', 'l0_ffn_w_out', 'l0_w_out']
    sums, rider = scatter_on(done, "c")
    dz, dxs, dbm, dcm, ddt4, pgs, *got = _ssd_bwd(dy0, ypre, u0, act, dt, dtt, hin, a_log, d_skip,
                                                  row2(p['l0_ssd_norm']), B, nch, rider=rider)
    take_parts(done, sums, got)
    dpre, ddtr, pgd = _ssd_prep_bwd(dxs, dbm, dcm, ddt4, u0, udt, W['l0_ssd_conv_w'], ssd_cb, dt_bias, B, nch)
    dxbc, dcw0 = _conv_bwd(dpre, u0, U0_XBC, W['l0_ssd_conv_w'], 4, "l0_ssd_conv_bwd")
    dqkvg, pgr = _ret_bwd(dy0, u0, opre, rin, cos, sin, row2(p['l0_ret_norm']), B, nch)
    pgs = pgs.sum(0)
    G['l0_ssd_norm'] = pgs[:, 0, :].reshape(-1)
    G['l0_ssd_d'] = pgs[:, 1, :128].sum(0)[:SSD_HEADS]
    G['l0_ssd_a_log'] = pgs[:, 2, :128].sum(0)[:SSD_HEADS]
    G['l0_ssd_dt_bias'] = pgd.sum(0)[0, :SSD_HEADS]
    G['l0_ssd_conv_w'], G['l0_ssd_conv_b'] = dcw0[:4], dcw0[7]
    G['l0_ret_norm'] = pgr.sum(0)[0]
    dn = _mm(dqkvg, w0_main, "nt", F32, "l0_in_dgrad_qkvg")
    dn = _mm(dz, w0_main, "nt", F32, "l0_in_dgrad_z", add=dn, b_off=U0_Z)
    dn = _mm(dxbc, w0_main, "nt", F32, "l0_in_dgrad_xbc", add=dn, b_off=U0_XBC)
    dn = _mm(ddtr, w0_dt, "nt", F32, "l0_in_dgrad_dt", add=dn)
    G['l0_w_in'] = jnp.concatenate([
        _mm(n0t, dz, "nn", F32, "l0_in_wgrad_z"), _mm(n0t, dxbc, "nn", F32, "l0_in_wgrad_xbc"),
        _mm(n0t, ddtr, "nn", F32, "l0_in_wgrad_dt")[:, :SSD_HEADS], _mm(n0t, dqkvg, "nn", F32, "l0_in_wgrad_qkvg")], axis=1)
    dh0, dg = _rmsnorm_bwd(h0, row2(p['l0_mix_norm']), dn, dh1, nch, "l0_mix_norm_bwd")
    G['l0_mix_norm'] = dg[0]
    dh0 = dh0.reshape(B, Pn, D)
    grad_x = dh0[:, CH:]
    G['meta_tokens'] = dh0[:, PAD:CH].sum(0)

    sums = chip_sums(['l0_w_in'], "d")
    parts['l0_w_in'], = _scatter_sums([sums['l0_w_in']])
    reds = [_sum_chips(parts[n], "sum_chips_" + n) for n in _BIG]
    grads = {}
    for n, own, other in zip(_BIG, reds, _join_halves(reds)):
        both = jnp.where(core[0] == 0, jnp.stack([own, other]), jnp.stack([other, own]))
        grads[n] = both.reshape(-1, both.shape[2])
    small_full = _unpack(_allreduce_small(_pack([G[n] for n in _SMALL])), [G[n].shape for n in _SMALL])
    for n, g in zip(_SMALL, small_full):
        if n in _SMALL_SHARDED:
            cs = g.shape[1] // 4
            g = lax.dynamic_slice_in_dim(g, chip * cs, cs, axis=1)
        grads[n] = g.reshape(p[n].shape)

    delta, new_m, new_v = {}, {}, {}
    for n in _BIG:
        delta[n], new_m[n], new_v[n] = _adamw(p[n], grads[n], p['m_' + n], p['v_' + n], "adamw_" + n)
    shapes = [p[n].shape for n in _SMALL]
    outs = _adamw(_pack([p[n] for n in _SMALL]), _pack([grads[n] for n in _SMALL]), _pack([p['m_' + n] for n in _SMALL]),
                  _pack([p['v_' + n] for n in _SMALL]), "adamw_small")
    for dst, buf in zip((delta, new_m, new_v), outs):
        for n, a in zip(_SMALL, _unpack(buf, shapes)):
            dst[n] = a
    return (loss, grad_x, *[grads[n] for n in _W_NAMES], *[delta[n] for n in _W_NAMES],
            *[new_m[n] for n in _W_NAMES], *[new_v[n] for n in _W_NAMES])
```

```python
import math

import numpy as np
import jax
import jax.numpy as jnp
from jax import lax
from jax.experimental import pallas as pl
from jax.experimental.pallas import tpu as pltpu

F32 = jnp.float32
BF16 = jnp.bfloat16
_MXU = jnp.bfloat16

D = 1024
CH = 128
N_META = 16
PAD = CH - N_META
EPS = 1e-6

SSD_HEADS = 16
SSD_HD = 64
SSD_GROUPS = 4
RET_HEADS = 4
RET_DK = 256
SB_HEADS = 16
SB_HD = 64
LRU_BLOCKS = 8
LRU_C = 8.0
FFN = 2816
U0_Z = 4096
U0_XBC = 5120

VMEM_LIMIT = 56 * 1024 * 1024


def _cparams(sem):
    return pltpu.CompilerParams(dimension_semantics=sem, vmem_limit_bytes=VMEM_LIMIT)


def _dot(a, b, dims=((1,), (0,))):
    return lax.dot_general(a.astype(_MXU), b.astype(_MXU), (dims, ((), ())), preferred_element_type=F32)


def _dot_nt(a, b):
    return _dot(a, b, ((1,), (1,)))


def _dot_tn(a, b):
    return _dot(a.T, b)


def _dot_exact(a, b):
    return lax.dot_general(a, b, (((1,), (0,)), ((), ())), preferred_element_type=F32,
                           precision=lax.Precision.HIGHEST)


def _dot_split(x, m01):
    hi = x.astype(BF16)
    lo = (x - hi.astype(F32)).astype(BF16)
    m = m01.astype(BF16)
    return jnp.dot(hi, m, preferred_element_type=F32) + jnp.dot(lo, m, preferred_element_type=F32)


def _sigmoid(x):
    return 0.5 * jnp.tanh(0.5 * x) + 0.5


def _softplus(x):
    return jnp.maximum(x, 0.0) + jnp.log1p(jnp.exp(-jnp.abs(x)))


def _silu(x):
    return x * _sigmoid(x)


def _dsilu(x):
    s = _sigmoid(x)
    return s * (1.0 + x * (1.0 - s))


_GELU_C = math.sqrt(2.0 / math.pi)


def _gelu(x):
    return 0.5 * x * (1.0 + jnp.tanh(_GELU_C * (x + 0.044715 * x * x * x)))


def _dgelu(x):
    t = jnp.tanh(_GELU_C * (x + 0.044715 * x * x * x))
    return 0.5 * (1.0 + t) + 0.5 * x * (1.0 - t * t) * _GELU_C * (1.0 + 3.0 * 0.044715 * x * x)


def _row_ids(n, cols=1):
    return lax.broadcasted_iota(jnp.int32, (n, cols), 0)


def _lane_ids(rows, n):
    return lax.broadcasted_iota(jnp.int32, (rows, n), 1)


def _real_rows(chunk):
    return chunk * CH + _row_ids(CH) >= PAD


def _shift_down(prev8, cur, s):
    cat = jnp.concatenate([prev8, cur], axis=0)
    return pltpu.roll(cat, s, axis=0)[8:]


def _shift_up(cur, next8, s):
    n = cur.shape[0]
    cat = jnp.concatenate([cur, next8], axis=0)
    return pltpu.roll(cat, n + 8 - s, axis=0)[:n]


def _conv_pre(prev8, cur, w_ref, b_ref, K):
    acc = cur * w_ref[K - 1:K, :] + b_ref[...]
    for s in range(1, K):
        acc = acc + _shift_down(prev8, cur, s) * w_ref[K - 1 - s:K - s, :]
    return acc


def _prev8_map(nch, col):
    return lambda b, c: (jnp.maximum((b * nch + c) * (CH // 8) - 1, 0), col)


def _matmul(a, b, mode, out_dtype, tm, tn, tk, name, add=None, b_off=0):
    if mode == "nn":
        (M, K), (_, N) = a.shape, b.shape
    elif mode == "nt":
        (M, K), N = a.shape, b.shape[0]
    else:
        (K, M), (_, N) = a.shape, b.shape
    tm, tn, tk = min(tm, M), min(tn, N), min(tk, K)
    assert M % tm == 0 and N % tn == 0 and K % tk == 0 and b_off % tk == 0, (name, M, N, K, tm, tn, tk)
    koff = b_off // tk
    nk = K // tk
    dims = {"nn": ((1,), (0,)), "nt": ((1,), (1,)), "tn": ((0,), (0,))}[mode]
    if mode == "tn":
        a_spec = pl.BlockSpec((tk, tm), lambda i, j, k: (k, i))
    else:
        a_spec = pl.BlockSpec((tm, tk), lambda i, j, k: (i, k))
    if mode == "nt":
        b_spec = pl.BlockSpec((tn, tk), lambda i, j, k: (j, k + koff))
    else:
        b_spec = pl.BlockSpec((tk, tn), lambda i, j, k: (k, j))
    o_spec = pl.BlockSpec((tm, tn), lambda i, j, k: (i, j))
    has_add = add is not None

    def body(a_ref, b_ref, *rest):
        if has_add:
            add_ref, o_ref, acc = rest
        else:
            o_ref, acc = rest
        k = pl.program_id(2)

        @pl.when(k == 0)
        def _():
            acc[...] = jnp.zeros_like(acc)

        acc[...] += _dot(a_ref[...], b_ref[...], dims)

        @pl.when(k == nk - 1)
        def _():
            r = acc[...]
            if has_add:
                r = r + add_ref[...].astype(F32)
            o_ref[...] = r.astype(out_dtype)

    in_specs = [a_spec, b_spec] + ([o_spec] if has_add else [])
    args = (a, b) + ((add,) if has_add else ())
    return pl.pallas_call(
        body, name=name, grid=(M // tm, N // tn, nk),
        in_specs=in_specs, out_specs=o_spec,
        out_shape=jax.ShapeDtypeStruct((M, N), out_dtype),
        scratch_shapes=[pltpu.VMEM((tm, tn), F32)],
        compiler_params=_cparams(("parallel", "parallel", "arbitrary")),
    )(*args)


def _tile(n, prefs):
    for t in prefs:
        if n % t == 0:
            return t
    return n


def _mm(a, b, mode, out_dtype, name, add=None, b_off=0):
    if mode == "tn":
        K, M = a.shape
        N = b.shape[1]
        tm, tn, tk = _tile(M, (1024, 1408, 512, 256, 128)), _tile(N, (1024, 1408, 512, 256, 128)), _tile(K, (2176, 384, 256, 128))
    else:
        M, K = a.shape
        N = b.shape[1] if mode == "nn" else b.shape[0]
        tm = _tile(M, (1088, 1024, 768, 512, 384, 256, 128))
        tn = _tile(N, (1024, 1408, 512, 256, 128))
        tk = _tile(K, (2176, 1024, 1408, 512, 256, 128))
    return _matmul(a, b, mode, out_dtype, tm, tn, tk, name, add=add, b_off=b_off)


def _rmsnorm_fwd(h, g, name):
    R = h.shape[0]
    tr = 2 * CH

    def body(h_ref, g_ref, o_ref, ot_ref):
        x = h_ref[...]
        r = lax.rsqrt(jnp.mean(x * x, axis=-1, keepdims=True) + EPS)
        y = x * r * g_ref[...]
        o_ref[...] = y.astype(o_ref.dtype)
        ot_ref[...] = y.T.astype(ot_ref.dtype)

    return pl.pallas_call(
        body, name=name, grid=(R // tr,),
        in_specs=[pl.BlockSpec((tr, D), lambda i: (i, 0)), pl.BlockSpec((1, D), lambda i: (0, 0))],
        out_specs=[pl.BlockSpec((tr, D), lambda i: (i, 0)), pl.BlockSpec((D, tr), lambda i: (0, i))],
        out_shape=[jax.ShapeDtypeStruct((R, D), _MXU), jax.ShapeDtypeStruct((D, R), _MXU)],
        compiler_params=_cparams(("parallel",)),
    )(h, g)


def _rmsnorm_bwd(h, g, dn, dres, nch, name):
    R = h.shape[0]
    per = 4
    tr = nch * CH // per

    def body(h_ref, g_ref, dn_ref, dres_ref, dh_ref, dg_ref):
        i = pl.program_id(0)
        x = h_ref[...]
        r = lax.rsqrt(jnp.mean(x * x, axis=-1, keepdims=True) + EPS)
        xhat = x * r
        dn_v = dn_ref[...]
        dx = dn_v * g_ref[...]
        dh = r * (dx - xhat * jnp.mean(dx * xhat, axis=-1, keepdims=True))
        keep = (i % per) * tr + _row_ids(tr) >= PAD
        dh_ref[...] = jnp.where(keep, dres_ref[...] + dh, 0.0)

        @pl.when(i == 0)
        def _():
            dg_ref[...] = jnp.zeros_like(dg_ref)

        dg_ref[...] += jnp.sum(dn_v * xhat, axis=0, keepdims=True)

    row = pl.BlockSpec((tr, D), lambda i: (i, 0))
    vec = pl.BlockSpec((1, D), lambda i: (0, 0))
    return pl.pallas_call(
        body, name=name, grid=(R // tr,),
        in_specs=[row, vec, row, row], out_specs=[row, vec],
        out_shape=[jax.ShapeDtypeStruct((R, D), F32), jax.ShapeDtypeStruct((1, D), F32)],
        compiler_params=_cparams(("arbitrary",)),
    )(h, g, dn, dres)


def _ssd_prep(u0, udt, conv_w, conv_b, dt_bias, B, nch, rider=None):
    R = u0.shape[0]

    def body(xs_ref, xsp_ref, bc_ref, bcp_ref, udt_ref, w0_ref, w1_ref, b0_ref, b1_ref, dtb_ref,
             act_ref, dt_ref, dtt_ref):
        keep = _real_rows(pl.program_id(1))
        a0 = _silu(_conv_pre(xsp_ref[...], xs_ref[...], w0_ref, b0_ref, 4))
        a1 = _silu(_conv_pre(bcp_ref[...], bc_ref[...], w1_ref, b1_ref, 4))
        act_ref[:, :1024] = jnp.where(keep, a0, 0.0)
        act_ref[:, 1024:] = jnp.where(keep, a1, 0.0)
        ok = jnp.logical_and(keep, _lane_ids(1, 128) < SSD_HEADS)
        dt = jnp.where(ok, _softplus(udt_ref[...] + dtb_ref[...]), 0.0)
        dt_ref[...] = dt
        dtt_ref[...] = dt.T

    row = lambda col: pl.BlockSpec((CH, 1024), lambda b, c: (b * nch + c, col))
    prev = lambda col: pl.BlockSpec((8, 1024), _prev8_map(nch, col))
    kw = dict(
        grid=(B, nch),
        in_specs=[row(5), prev(5), row(6), prev(6),
                  pl.BlockSpec((CH, 128), lambda b, c: (b * nch + c, 0)),
                  pl.BlockSpec((4, 1024), lambda b, c: (0, 0)), pl.BlockSpec((4, 1024), lambda b, c: (0, 1)),
                  pl.BlockSpec((1, 1024), lambda b, c: (0, 0)), pl.BlockSpec((1, 1024), lambda b, c: (0, 1)),
                  pl.BlockSpec((1, 128), lambda b, c: (0, 0))],
        out_specs=[pl.BlockSpec((CH, 2048), lambda b, c: (b * nch + c, 0)),
                   pl.BlockSpec((CH, 128), lambda b, c: (b * nch + c, 0)),
                   pl.BlockSpec((128, CH), lambda b, c: (0, b * nch + c))],
        out_shape=[jax.ShapeDtypeStruct((R, 2048), F32), jax.ShapeDtypeStruct((R, 128), F32),
                   jax.ShapeDtypeStruct((128, R), F32)])
    return _call(body, "ssd_prep", ("arbitrary", "arbitrary"), kw,
                 (u0, u0, u0, u0, udt, conv_w, conv_w, conv_b, conv_b, dt_bias), rider)


def _ssd_head_terms(h, a_vec, dt_v, dtt_v, dsk_v):
    lane = _lane_ids(1, 128)
    sub = _row_ids(128)
    r = _row_ids(CH, CH)
    cidx = _lane_ids(CH, CH)
    a_h = jnp.sum(jnp.where(lane == h, a_vec, 0.0), axis=1, keepdims=True)
    dt_col = jnp.sum(jnp.where(lane == h, dt_v, 0.0), axis=1, keepdims=True)
    dt_row = jnp.sum(jnp.where(sub == h, dtt_v, 0.0), axis=0, keepdims=True)
    cs_col = jnp.sum(jnp.where(r >= cidx, dt_row * a_h, 0.0), axis=1, keepdims=True)
    cs_row = jnp.sum(jnp.where(r <= cidx, dt_col * a_h, 0.0), axis=0, keepdims=True)
    tot = jnp.sum(dt_col * a_h, axis=0, keepdims=True)
    dsk = jnp.sum(jnp.where(lane == h, dsk_v, 0.0), axis=1, keepdims=True)
    return a_h, dt_col, cs_col, cs_row, tot, dsk


def _ssd_fwd(act, u0, dt, dtt, a_log, d_skip, norm_g, B, nch, rider=None):
    R = act.shape[0]

    def body(xs_ref, bm_ref, cm_ref, z_ref, dt_ref, dtt_ref, alog_ref, dsk_ref, ng_ref,
             out_ref, ypre_ref, hin_ref, H):
        g = pl.program_id(1)
        c = pl.program_id(2)

        @pl.when(c == 0)
        def _():
            H[...] = jnp.zeros_like(H)

        hin_ref[...] = H[...]
        a_vec = -jnp.exp(alog_ref[...])
        dt_v = dt_ref[...]
        dtt_v = dtt_ref[...]
        hm = _lane_ids(1, 128) < SSD_HD
        r = _row_ids(CH, CH)
        cidx = _lane_ids(CH, CH)
        Bm = bm_ref[...]
        Cm = cm_ref[...]
        CB = _dot_nt(Cm, Bm)
        ys = []
        for pair in range(2):
            cols = slice(128 * pair, 128 * pair + 128)
            xraw = xs_ref[:, cols]
            t = [_ssd_head_terms(4 * g + 2 * pair + j, a_vec, dt_v, dtt_v, dsk_ref[...]) for j in range(2)]
            sel = lambda f: jnp.where(hm, f(t[0]), f(t[1]))
            dtp = sel(lambda q: q[1])
            Ep = sel(lambda q: jnp.exp(q[2]))
            Wp = sel(lambda q: jnp.exp(q[4] - q[2]))
            etot = sel(lambda q: jnp.exp(q[4]))
            dsk = sel(lambda q: q[5])
            X = xraw * dtp
            ydiag = jnp.zeros((CH, 128), F32)
            for j in range(2):
                Lm = jnp.where(r >= cidx, jnp.exp(t[j][2] - t[j][3]), 0.0)
                Xh = jnp.where(hm if j == 0 else jnp.logical_not(hm), X, 0.0)
                ydiag = ydiag + _dot(CB * Lm, Xh)
            Hp = H[:, cols]
            yoff = Ep * _dot(Cm, Hp)
            S = _dot(Bm.T, X * Wp)
            H[:, cols] = etot * Hp + S
            ys.append(ydiag + yoff + xraw * dsk)
        y = jnp.concatenate(ys, axis=1)
        ypre_ref[...] = y
        yg = y * _silu(z_ref[...])
        rr = lax.rsqrt(jnp.mean(yg * yg, axis=-1, keepdims=True) + EPS)
        out_ref[...] = jnp.where(_real_rows(c), yg * rr * ng_ref[...], 0.0).astype(out_ref.dtype)

    rowb = lambda w, colf: pl.BlockSpec((CH, w), lambda b, g, c: (b * nch + c, colf(g)))
    vec = pl.BlockSpec((1, 128), lambda b, g, c: (0, 0))
    kw = dict(
        grid=(B, SSD_GROUPS, nch),
        in_specs=[rowb(256, lambda g: g), rowb(128, lambda g: 8 + g), rowb(128, lambda g: 12 + g),
                  rowb(256, lambda g: 16 + g), rowb(128, lambda g: 0),
                  pl.BlockSpec((128, CH), lambda b, g, c: (0, b * nch + c)),
                  vec, vec, pl.BlockSpec((1, 256), lambda b, g, c: (0, g))],
        out_specs=[rowb(256, lambda g: g), rowb(256, lambda g: g),
                   pl.BlockSpec((None, None, None, 128, 256), lambda b, g, c: (b, g, c, 0, 0))],
        out_shape=[jax.ShapeDtypeStruct((R, 2048), _MXU), jax.ShapeDtypeStruct((R, 1024), F32),
                   jax.ShapeDtypeStruct((B, SSD_GROUPS, nch, 128, 256), F32)],
        scratch_shapes=[pltpu.VMEM((128, 256), F32)])
    return _call(body, "ssd_fwd", ("arbitrary", "arbitrary", "arbitrary"), kw,
                 (act, act, act, u0, dt, dtt, a_log, d_skip, norm_g), rider)


def _ssd_bwd(dycat, ypre, u0, act, dt, dtt, hin, a_log, d_skip, norm_g, B, nch, rider=None):
    R = act.shape[0]

    def body(dy_ref, ypre_ref, z_ref, xs_ref, bm_ref, cm_ref, dt_ref, dtt_ref, hin_ref, alog_ref, dsk_ref, ng_ref,
             dz_ref, dxs_ref, db_ref, dc_ref, ddt_ref, pg_ref, dH):
        g = pl.program_id(1)
        c = nch - 1 - pl.program_id(2)

        @pl.when(pl.program_id(2) == 0)
        def _():
            dH[...] = jnp.zeros_like(dH)
            pg_ref[...] = jnp.zeros_like(pg_ref)

        z = z_ref[...]
        y = ypre_ref[...]
        ng = ng_ref[...]
        dout = jnp.where(_real_rows(c), dy_ref[...], 0.0)
        sz = _sigmoid(z)
        yg = y * z * sz
        rr = lax.rsqrt(jnp.mean(yg * yg, axis=-1, keepdims=True) + EPS)
        nrm = yg * rr
        pg_ref[0:1, :] += jnp.sum(dout * nrm, axis=0, keepdims=True)
        dn = dout * ng
        dyg = rr * (dn - nrm * jnp.mean(dn * nrm, axis=-1, keepdims=True))
        dy = dyg * z * sz
        dz_ref[...] = (dyg * y * (sz * (1.0 + z * (1.0 - sz)))).astype(dz_ref.dtype)

        a_vec = -jnp.exp(alog_ref[...])
        dt_v = dt_ref[...]
        dtt_v = dtt_ref[...]
        lane = _lane_ids(1, 128)
        hm = lane < SSD_HD
        r = _row_ids(CH, CH)
        cidx = _lane_ids(CH, CH)
        last = _row_ids(CH) == CH - 1
        Bm = bm_ref[...]
        Cm = cm_ref[...]
        CB = _dot_nt(Cm, Bm)
        CBT = _dot_nt(Bm, Cm)
        dB = jnp.zeros((CH, 128), F32)
        dC = jnp.zeros((CH, 128), F32)
        dcs_all = jnp.zeros((CH, 128), F32)
        dtx_all = jnp.zeros((CH, 128), F32)
        dd_row = jnp.zeros((1, 128), F32)
        dxs = []
        for pair in range(2):
            cols = slice(128 * pair, 128 * pair + 128)
            xraw = xs_ref[:, cols]
            dyp = dy[:, cols]
            heads = [4 * g + 2 * pair + j for j in range(2)]
            t = [_ssd_head_terms(heads[j], a_vec, dt_v, dtt_v, dsk_ref[...]) for j in range(2)]
            sel = lambda f: jnp.where(hm, f(t[0]), f(t[1]))
            hsum = lambda v, j: jnp.sum(jnp.where(hm if j == 0 else jnp.logical_not(hm), v, 0.0), axis=1, keepdims=True)
            dtp = sel(lambda q: q[1])
            Ep = sel(lambda q: jnp.exp(q[2]))
            Wp = sel(lambda q: jnp.exp(q[4] - q[2]))
            etot = sel(lambda q: jnp.exp(q[4]))
            dsk = sel(lambda q: q[5])
            X = xraw * dtp
            Hp = hin_ref[:, cols]
            dHn = dH[:, cols]
            dskip = jnp.sum(dyp * xraw, axis=0, keepdims=True)
            yoff = Ep * _dot(Cm, Hp)
            dE = dyp * yoff
            dC = dC + _dot_nt(dyp * Ep, Hp)
            dH[:, cols] = etot * dHn + _dot(Cm.T, dyp * Ep)
            BdS = _dot(Bm, dHn)
            dX = Wp * BdS
            ew = X * BdS * Wp
            dB = dB + _dot_nt(X * Wp, dHn)
            hh = jnp.sum(dHn * Hp, axis=0, keepdims=True) * etot
            for j in range(2):
                hmask = hm if j == 0 else jnp.logical_not(hm)
                cs_col, cs_row = t[j][2], t[j][3]
                Lm = jnp.where(r >= cidx, jnp.exp(cs_col - cs_row), 0.0)
                LmT = jnp.where(cidx >= r, jnp.exp(cs_row - cs_col), 0.0)
                dyh = jnp.where(hmask, dyp, 0.0)
                Xh = jnp.where(hmask, X, 0.0)
                dM = _dot_nt(dyh, Xh)
                dMT = _dot_nt(Xh, dyh)
                M = CB * Lm
                MT = CBT * LmT
                dX = dX + _dot(MT, dyh)
                dC = dC + _dot(dM * Lm, Bm)
                dB = dB + _dot(dMT * LmT, Cm)
                g_rows = jnp.sum(dM * M, axis=1, keepdims=True)
                g_cols = jnp.sum(dMT * MT, axis=1, keepdims=True)
                dtot = (jnp.sum(hsum(ew, j), axis=0, keepdims=True)
                        + jnp.sum(jnp.where(hmask, hh, 0.0), axis=1, keepdims=True))
                dcs = g_rows - g_cols + hsum(dE, j) - hsum(ew, j) + jnp.where(last, dtot, 0.0)
                dcs_all = dcs_all + jnp.where(lane == heads[j], dcs, 0.0)
                dtx_all = dtx_all + jnp.where(lane == heads[j], hsum(dX * xraw, j), 0.0)
                dd_row = dd_row + jnp.where(lane == heads[j],
                                            jnp.sum(jnp.where(hmask, dskip, 0.0), axis=1, keepdims=True), 0.0)
            dxs.append(dX * dtp + dyp * dsk)
        dxs_ref[...] = jnp.concatenate(dxs, axis=1)
        db_ref[...] = dB
        dc_ref[...] = dC
        dadt = _dot_exact(jnp.where(cidx >= r, 1.0, 0.0), dcs_all)
        ddt_ref[...] = dadt * a_vec + dtx_all
        pg_ref[1:2, 0:128] += dd_row
        pg_ref[2:3, 0:128] += jnp.sum(dadt * dt_v, axis=0, keepdims=True) * a_vec

    rowb = lambda w, colf: pl.BlockSpec((CH, w), lambda b, g, c: (b * nch + nch - 1 - c, colf(g)))
    vec = pl.BlockSpec((1, 128), lambda b, g, c: (0, 0))
    kw = dict(
        grid=(B, SSD_GROUPS, nch),
        in_specs=[rowb(256, lambda g: g), rowb(256, lambda g: g), rowb(256, lambda g: 16 + g), rowb(256, lambda g: g),
                  rowb(128, lambda g: 8 + g), rowb(128, lambda g: 12 + g), rowb(128, lambda g: 0),
                  pl.BlockSpec((128, CH), lambda b, g, c: (0, b * nch + nch - 1 - c)),
                  pl.BlockSpec((None, None, None, 128, 256), lambda b, g, c: (b, g, nch - 1 - c, 0, 0)),
                  vec, vec, pl.BlockSpec((1, 256), lambda b, g, c: (0, g))],
        out_specs=[rowb(256, lambda g: g), rowb(256, lambda g: g), rowb(128, lambda g: g), rowb(128, lambda g: g),
                   rowb(128, lambda g: g),
                   pl.BlockSpec((None, None, 8, 256), lambda b, g, c: (b, g, 0, 0))],
        out_shape=[jax.ShapeDtypeStruct((R, 1024), _MXU), jax.ShapeDtypeStruct((R, 1024), F32),
                   jax.ShapeDtypeStruct((R, 512), F32), jax.ShapeDtypeStruct((R, 512), F32),
                   jax.ShapeDtypeStruct((R, 512), F32), jax.ShapeDtypeStruct((B, SSD_GROUPS, 8, 256), F32)],
        scratch_shapes=[pltpu.VMEM((128, 256), F32)])
    return _call(body, "ssd_bwd", ("arbitrary", "arbitrary", "arbitrary"), kw,
                 (dycat, ypre, u0, act, act, act, dt, dtt, hin, a_log, d_skip, norm_g), rider)


def _ssd_prep_bwd(dxs, dB, dC, ddt4, u0, udt, conv_w, conv_b, dt_bias, B, nch, rider=None):
    R = u0.shape[0]

    def body(dxs_ref, db_ref, dc_ref, ddt_ref, xs_ref, xsp_ref, bc_ref, bcp_ref, udt_ref, w0_ref, w1_ref, b0_ref, b1_ref,
             dtb_ref, dpre_ref, ddtr_ref, pgd_ref):
        c = pl.program_id(1)

        @pl.when(c == 0)
        def _():
            pgd_ref[...] = jnp.zeros_like(pgd_ref)

        keep = _real_rows(c)
        p0 = _conv_pre(xsp_ref[...], xs_ref[...], w0_ref, b0_ref, 4)
        p1 = _conv_pre(bcp_ref[...], bc_ref[...], w1_ref, b1_ref, 4)
        dpre_ref[:, :1024] = jnp.where(keep, dxs_ref[...] * _dsilu(p0), 0.0)
        dpre_ref[:, 1024:] = jnp.where(keep, jnp.concatenate([db_ref[...], dc_ref[...]], axis=1) * _dsilu(p1), 0.0)
        ddt = ddt_ref[:, 0:128] + ddt_ref[:, 128:256] + ddt_ref[:, 256:384] + ddt_ref[:, 384:512]
        ok = jnp.logical_and(keep, _lane_ids(1, 128) < SSD_HEADS)
        dr = jnp.where(ok, ddt * _sigmoid(udt_ref[...] + dtb_ref[...]), 0.0)
        ddtr_ref[...] = dr
        pgd_ref[0:1, :] += jnp.sum(dr, axis=0, keepdims=True)

    rw = lambda w: pl.BlockSpec((CH, w), lambda b, c: (b * nch + c, 0))
    row = lambda col: pl.BlockSpec((CH, 1024), lambda b, c: (b * nch + c, col))
    prev = lambda col: pl.BlockSpec((8, 1024), _prev8_map(nch, col))
    kw = dict(
        grid=(B, nch),
        in_specs=[rw(1024), rw(512), rw(512), rw(512), row(5), prev(5), row(6), prev(6), rw(128),
                  pl.BlockSpec((4, 1024), lambda b, c: (0, 0)), pl.BlockSpec((4, 1024), lambda b, c: (0, 1)),
                  pl.BlockSpec((1, 1024), lambda b, c: (0, 0)), pl.BlockSpec((1, 1024), lambda b, c: (0, 1)),
                  pl.BlockSpec((1, 128), lambda b, c: (0, 0))],
        out_specs=[rw(2048), rw(128), pl.BlockSpec((None, 8, 128), lambda b, c: (b, 0, 0))],
        out_shape=[jax.ShapeDtypeStruct((R, 2048), F32), jax.ShapeDtypeStruct((R, 128), F32),
                   jax.ShapeDtypeStruct((B, 8, 128), F32)])
    return _call(body, "ssd_prep_bwd", ("arbitrary", "arbitrary"), kw,
                 (dxs, dB, dC, ddt4, u0, u0, u0, u0, udt, conv_w, conv_w, conv_b, conv_b, dt_bias), rider)


def _conv_bwd(dpre, xin, xin_col, w, K, name, tc=1024):
    R, C = dpre.shape
    assert C % tc == 0 and xin_col % tc == 0
    nr = R // CH
    xoff = xin_col // tc

    def body(dp_ref, dpn_ref, x_ref, xp_ref, w_ref, din_ref, dw_ref):
        i = pl.program_id(1)

        @pl.when(i == 0)
        def _():
            dw_ref[...] = jnp.zeros_like(dw_ref)

        dp = dp_ref[...]
        nxt = dpn_ref[...] * (i < nr - 1).astype(F32)
        x = x_ref[...]
        xp = xp_ref[...]
        din = dp * w_ref[K - 1:K, :]
        dw_ref[K - 1:K, :] += jnp.sum(dp * x, axis=0, keepdims=True)
        dw_ref[7:8, :] += jnp.sum(dp, axis=0, keepdims=True)
        for s in range(1, K):
            din = din + _shift_up(dp, nxt, s) * w_ref[K - 1 - s:K - s, :]
            dw_ref[K - 1 - s:K - s, :] += jnp.sum(dp * _shift_down(xp, x, s), axis=0, keepdims=True)
        din_ref[...] = din.astype(din_ref.dtype)

    return pl.pallas_call(
        body, name=name, grid=(C // tc, nr),
        in_specs=[pl.BlockSpec((CH, tc), lambda j, i: (i, j)),
                  pl.BlockSpec((8, tc), lambda j, i: (jnp.minimum((i + 1) * (CH // 8), nr * (CH // 8) - 1), j)),
                  pl.BlockSpec((CH, tc), lambda j, i: (i, xoff + j)),
                  pl.BlockSpec((8, tc), lambda j, i: (jnp.maximum(i * (CH // 8) - 1, 0), xoff + j)),
                  pl.BlockSpec((K, tc), lambda j, i: (0, j))],
        out_specs=[pl.BlockSpec((CH, tc), lambda j, i: (i, j)),
                   pl.BlockSpec((8, tc), lambda j, i: (0, j))],
        out_shape=[jax.ShapeDtypeStruct((R, C), _MXU), jax.ShapeDtypeStruct((8, C), F32)],
        compiler_params=_cparams(("parallel", "arbitrary")),
    )(dpre, dpre, xin, xin, w)


_RET_LG = [float(v) for v in np.log1p(-np.exp2(-5.0 - np.arange(RET_HEADS, dtype=np.float32))).astype(np.float32)]
_RET_SCALE = RET_DK ** -0.5


def _rope_tables(nch):
    half = RET_DK // 2
    inv_freq = 1.0 / (10000.0 ** (jnp.arange(half, dtype=F32) / (half - 1)))
    pos = jnp.arange(nch * CH, dtype=F32) - PAD
    ang = pos[:, None] * inv_freq[None, :]
    return jnp.cos(ang), jnp.sin(ang)


def _rot(x, cos, sin):
    x1, x2 = x[:, :128], x[:, 128:]
    return jnp.concatenate([x1 * cos - x2 * sin, x1 * sin + x2 * cos], axis=1)


def _unrot(d, cos, sin):
    d1, d2 = d[:, :128], d[:, 128:]
    return jnp.concatenate([d1 * cos + d2 * sin, d2 * cos - d1 * sin], axis=1)


def _ret_decays(lg):
    r = _row_ids(CH, CH)
    cidx = _lane_ids(CH, CH)
    diff = (r - cidx).astype(F32)
    decay = jnp.where(r >= cidx, jnp.exp(lg * jnp.maximum(diff, 0.0)), 0.0)
    decay_t = jnp.where(cidx >= r, jnp.exp(lg * jnp.maximum(-diff, 0.0)), 0.0)
    idx = _row_ids(CH).astype(F32)
    zeta = jnp.exp(lg * (CH - 1.0 - idx))
    xi = jnp.exp(lg * (idx + 1.0))
    return decay, decay_t, zeta, xi


def _ret_fwd(u0, ycat, cos, sin, norm_g, B, nch, rider=None):
    R = u0.shape[0]

    def body(u_ref, cos_ref, sin_ref, ng_ref, ycat_in, out_ref, opre_ref, rin_ref, Rst):
        c = pl.program_id(1)

        @pl.when(c == 0)
        def _():
            Rst[...] = jnp.zeros_like(Rst)

        cos_v, sin_v = cos_ref[...], sin_ref[...]
        for h in range(RET_HEADS):
            lg = _RET_LG[h]
            cols = slice(256 * h, 256 * h + 256)
            decay, _, zeta, xi = _ret_decays(lg)
            qr = _rot(u_ref[:, cols], cos_v, sin_v)
            kr = _rot(u_ref[:, 1024 + 256 * h:1024 + 256 * h + 256], cos_v, sin_v) * _RET_SCALE
            v = u_ref[:, 2048 + 256 * h:2048 + 256 * h + 256]
            gate = u_ref[:, 3072 + 256 * h:3072 + 256 * h + 256]
            Rh = Rst[h]
            rin_ref[h] = Rh
            inner = _dot(_dot_nt(qr, kr) * decay, v)
            cross = _dot(qr, Rh) * xi
            Rst[h] = math.exp(CH * lg) * Rh + _dot((kr * zeta).T, v)
            o = inner + cross
            opre_ref[:, cols] = o
            oc = o - jnp.mean(o, axis=-1, keepdims=True)
            rr = lax.rsqrt(jnp.mean(oc * oc, axis=-1, keepdims=True) + EPS)
            out_ref[:, cols] = (_silu(gate) * (oc * rr * ng_ref[:, cols])).astype(out_ref.dtype)

    kw = dict(
        grid=(B, nch),
        in_specs=[pl.BlockSpec((CH, 4096), lambda b, c: (b * nch + c, 0)),
                  pl.BlockSpec((CH, 128), lambda b, c: (c, 0)), pl.BlockSpec((CH, 128), lambda b, c: (c, 0)),
                  pl.BlockSpec((1, 1024), lambda b, c: (0, 0)),
                  pl.BlockSpec(memory_space=pl.ANY)],
        out_specs=[pl.BlockSpec((CH, 1024), lambda b, c: (b * nch + c, 1)),
                   pl.BlockSpec((CH, 1024), lambda b, c: (b * nch + c, 0)),
                   pl.BlockSpec((None, None, RET_HEADS, 256, 256), lambda b, c: (b, c, 0, 0, 0))],
        out_shape=[jax.ShapeDtypeStruct(ycat.shape, ycat.dtype), jax.ShapeDtypeStruct((R, 1024), F32),
                   jax.ShapeDtypeStruct((B, nch, RET_HEADS, 256, 256), F32)],
        scratch_shapes=[pltpu.VMEM((RET_HEADS, 256, 256), F32)],
        input_output_aliases={4: 0})
    return _call(body, "ret_fwd", ("arbitrary", "arbitrary"), kw, (u0, cos, sin, norm_g, ycat), rider)


def _ret_bwd(dycat, u0, opre, rin, cos, sin, norm_g, B, nch, rider=None):
    R = u0.shape[0]

    def body(dy_ref, u_ref, opre_ref, rin_ref, cos_ref, sin_ref, ng_ref, du_ref, pg_ref, dR):
        @pl.when(pl.program_id(1) == 0)
        def _():
            dR[...] = jnp.zeros_like(dR)
            pg_ref[...] = jnp.zeros_like(pg_ref)

        cos_v, sin_v = cos_ref[...], sin_ref[...]
        for h in range(RET_HEADS):
            lg = _RET_LG[h]
            cols = slice(256 * h, 256 * h + 256)
            decay, decay_t, zeta, xi = _ret_decays(lg)
            qr = _rot(u_ref[:, cols], cos_v, sin_v)
            kr = _rot(u_ref[:, 1024 + 256 * h:1024 + 256 * h + 256], cos_v, sin_v) * _RET_SCALE
            v = u_ref[:, 2048 + 256 * h:2048 + 256 * h + 256]
            gate = u_ref[:, 3072 + 256 * h:3072 + 256 * h + 256]
            ng = ng_ref[:, cols]
            o = opre_ref[:, cols]
            oc = o - jnp.mean(o, axis=-1, keepdims=True)
            rr = lax.rsqrt(jnp.mean(oc * oc, axis=-1, keepdims=True) + EPS)
            ohat = oc * rr
            dout = dy_ref[:, cols]
            du_ref[:, 3072 + 256 * h:3072 + 256 * h + 256] = (dout * (ohat * ng) * _dsilu(gate)).astype(du_ref.dtype)
            don = dout * _silu(gate)
            pg_ref[0:1, cols] += jnp.sum(don * ohat, axis=0, keepdims=True)
            dohat = don * ng
            do = rr * (dohat - jnp.mean(dohat, axis=-1, keepdims=True)
                       - ohat * jnp.mean(dohat * ohat, axis=-1, keepdims=True))
            Rh = rin_ref[h]
            dRn = dR[h]
            sc_t = _dot_nt(kr, qr) * decay_t
            dv = _dot(sc_t, do) + _dot(kr * zeta, dRn)
            ds = _dot_nt(do, v) * decay
            ds_t = _dot_nt(v, do) * decay_t
            dox = do * xi
            dq = _dot(ds, kr) + _dot_nt(dox, Rh)
            dk = _dot(ds_t, qr) + zeta * _dot_nt(v, dRn)
            dR[h] = math.exp(CH * lg) * dRn + _dot(qr.T, dox)
            du_ref[:, cols] = _unrot(dq, cos_v, sin_v).astype(du_ref.dtype)
            du_ref[:, 1024 + 256 * h:1024 + 256 * h + 256] = (_unrot(dk, cos_v, sin_v) * _RET_SCALE).astype(du_ref.dtype)
            du_ref[:, 2048 + 256 * h:2048 + 256 * h + 256] = dv.astype(du_ref.dtype)

    rmap = lambda b, c: (b * nch + nch - 1 - c, 0)
    kw = dict(
        grid=(B, nch),
        in_specs=[pl.BlockSpec((CH, 1024), lambda b, c: (b * nch + nch - 1 - c, 1)),
                  pl.BlockSpec((CH, 4096), rmap), pl.BlockSpec((CH, 1024), rmap),
                  pl.BlockSpec((None, None, RET_HEADS, 256, 256), lambda b, c: (b, nch - 1 - c, 0, 0, 0)),
                  pl.BlockSpec((CH, 128), lambda b, c: (nch - 1 - c, 0)),
                  pl.BlockSpec((CH, 128), lambda b, c: (nch - 1 - c, 0)),
                  pl.BlockSpec((1, 1024), lambda b, c: (0, 0))],
        out_specs=[pl.BlockSpec((CH, 4096), rmap), pl.BlockSpec((None, 8, 1024), lambda b, c: (b, 0, 0))],
        out_shape=[jax.ShapeDtypeStruct((R, 4096), _MXU), jax.ShapeDtypeStruct((B, 8, 1024), F32)],
        scratch_shapes=[pltpu.VMEM((RET_HEADS, 256, 256), F32)])
    return _call(body, "ret_bwd", ("arbitrary", "arbitrary"), kw, (dycat, u0, opre, rin, cos, sin, norm_g), rider)


_SB_SCALE = SB_HD ** -0.5


_SB_NB = 2


def _sb_valid(qb, kb, live):
    qpos = qb * CH + jnp.bitwise_and(_row_ids(2 * CH, CH), CH - 1)
    kpos = kb * CH + _lane_ids(2 * CH, CH)
    first = PAD + (1 - live) * (1 << 24)
    return jnp.logical_and(kpos < qpos, kpos >= first)


_SB_DEAD = -90.0


def _sb_alive(acc):
    return (jnp.max(acc) > _SB_DEAD).astype(jnp.int32)


def _sb_softplus(z):
    return jnp.maximum(z, 0.0) + jnp.log(1.0 + jnp.exp(-jnp.abs(z)))


def _stack_heads(x):
    hm = _lane_ids(1, 128) < SB_HD
    return jnp.concatenate([jnp.where(hm, x, 0.0), jnp.where(hm, 0.0, x)], axis=0)


def _unstack_heads(x2):
    return jnp.where(_lane_ids(1, 128) < SB_HD, x2[:CH], x2[CH:])


def _sb_fwd(u1, B, nch, rider=None):
    R = u1.shape[0]
    Pn = nch * CH

    def body(q_ref, k_ref, v_ref, out_ref):
        qb = pl.program_id(2)
        q2 = _stack_heads(q_ref[...] * _SB_SCALE).astype(_MXU)
        mgt = (_row_ids(CH, CH) > _lane_ids(CH, CH)).astype(F32)

        def step(i, carry):
            out2, acc = carry
            blocks = []
            for t in range(_SB_NB):
                kb = qb - _SB_NB * i - t
                live = (kb >= 0).astype(jnp.int32)
                kbc = jnp.maximum(kb, 0)
                start = pl.multiple_of(kbc * CH, CH)
                valid = _sb_valid(qb, kbc, live)
                z = _dot_nt(q2, k_ref[pl.ds(start, CH), :])
                sp = _sb_softplus(z)
                lm = jnp.where(valid, -sp, 0.0)
                blocks.append((valid, z - sp, _dot_split(lm, mgt), jnp.sum(lm, axis=1, keepdims=True), start))
            for valid, ls, loc, rs, start in blocks:
                w = jnp.where(valid, jnp.exp(ls + loc + acc), 0.0)
                out2 = out2 + _dot(w, v_ref[pl.ds(start, CH), :])
                acc = acc + rs
            return out2, acc

        trips = (qb + _SB_NB) // _SB_NB

        def more(c):
            return jnp.logical_and(c[0] < trips, c[1] > 0)

        def trip(c):
            out2, acc = step(c[0], c[2:])
            return c[0] + 1, _sb_alive(acc), out2, acc

        init = (jnp.int32(0), jnp.int32(1), jnp.zeros((2 * CH, 128), F32), jnp.zeros((2 * CH, 1), F32))
        out2 = lax.while_loop(more, trip, init)[2]
        out_ref[...] = _unstack_heads(out2).astype(out_ref.dtype)

    qspec = lambda off: pl.BlockSpec((CH, 128), lambda b, hp, qb: (b * nch + qb, off + hp))
    kspec = lambda off: pl.BlockSpec((Pn, 128), lambda b, hp, qb: (b, off + hp))
    kw = dict(grid=(B, SB_HEADS // 2, nch), in_specs=[qspec(0), kspec(8), kspec(16)], out_specs=[qspec(0)],
              out_shape=[jax.ShapeDtypeStruct((R, 2048), _MXU)])
    return _call(body, "sb_fwd", ("arbitrary", "arbitrary", "arbitrary"), kw, (u1, u1, u1), rider)


def _sb_bwd(dycat, u1, B, nch, rider=None):
    R = u1.shape[0]
    Pn = nch * CH

    def body(q_ref, k_ref, v_ref, do_ref, dq_ref, dk_ref, dv_ref):
        qb = pl.program_id(2)

        @pl.when(qb == 0)
        def _():
            dk_ref[...] = jnp.zeros_like(dk_ref)
            dv_ref[...] = jnp.zeros_like(dv_ref)

        q2 = _stack_heads(q_ref[...] * _SB_SCALE)
        do2 = _stack_heads(do_ref[...])
        q2t, do2t = q2.T.astype(_MXU), do2.T.astype(_MXU)
        q2, do2 = q2.astype(_MXU), do2.astype(_MXU)
        rr = _row_ids(CH, CH)
        cc = _lane_ids(CH, CH)
        mle = (rr <= cc).astype(F32)
        mlt = (rr < cc).astype(F32)
        trips = (qb + _SB_NB) // _SB_NB

        def more(c):
            return jnp.logical_and(c[0] < trips, c[1] > 0)

        def scan(c):
            acc = c[2]
            for t in range(_SB_NB):
                kb = qb - _SB_NB * c[0] - t
                kbc = jnp.maximum(kb, 0)
                z = _dot_nt(q2, k_ref[pl.ds(pl.multiple_of(kbc * CH, CH), CH), :])
                lm = jnp.where(_sb_valid(qb, kbc, (kb >= 0).astype(jnp.int32)), -_sb_softplus(z), 0.0)
                acc = acc + jnp.sum(lm, axis=1, keepdims=True)
            return c[0] + 1, _sb_alive(acc), acc

        used, _, s2 = lax.while_loop(more, scan, (jnp.int32(0), jnp.int32(1), jnp.zeros((2 * CH, 1), F32)))
        base = qb + 1 - _SB_NB * used

        def step(i, carry):
            dq2, pacc, gacc = carry
            blocks = []
            for t in range(_SB_NB):
                kb = base + _SB_NB * i + t
                live = (kb >= 0).astype(jnp.int32)
                start = pl.multiple_of(jnp.maximum(kb, 0) * CH, CH)
                valid = _sb_valid(qb, jnp.maximum(kb, 0), live)
                z = _dot_nt(q2, k_ref[pl.ds(start, CH), :])
                sp = _sb_softplus(z)
                lm = jnp.where(valid, -sp, 0.0)
                blocks.append((valid, z - sp, _dot_split(lm, mle), jnp.sum(lm, axis=1, keepdims=True), start))
            stage = []
            for valid, ls, ploc, rs, start in blocks:
                w = jnp.where(valid, jnp.exp(ls + (s2 - (ploc + pacc))), 0.0)
                gg = _dot_nt(do2, v_ref[pl.ds(start, CH), :]) * w
                stage.append((valid, ls, w, gg, _dot_split(gg, mlt), jnp.sum(gg, axis=1, keepdims=True), start))
                pacc = pacc + rs
            for valid, ls, w, gg, gloc, gs, start in stage:
                sig = jnp.exp(ls)
                dz = jnp.where(valid, gg * (1.0 - sig) - (gloc + gacc) * sig, 0.0)
                dq2 = dq2 + _dot(dz, k_ref[pl.ds(start, CH), :])
                dk_ref[:, pl.ds(start, CH)] += _dot(q2t, dz)
                dv_ref[:, pl.ds(start, CH)] += _dot(do2t, w)
                gacc = gacc + gs
            return dq2, pacc, gacc

        zero = jnp.zeros((2 * CH, 1), F32)
        dq2 = lax.fori_loop(0, used, step, (jnp.zeros((2 * CH, 128), F32), zero, zero))[0]
        dq_ref[...] = (_unstack_heads(dq2) * _SB_SCALE).astype(dq_ref.dtype)

    qspec = lambda off: pl.BlockSpec((CH, 128), lambda b, hp, qb: (b * nch + qb, off + hp))
    kspec = lambda off: pl.BlockSpec((Pn, 128), lambda b, hp, qb: (b, off + hp))
    tspec = pl.BlockSpec((128, Pn), lambda b, hp, qb: (hp, b))
    full = jax.ShapeDtypeStruct((1024, R), F32)
    kw = dict(grid=(B, SB_HEADS // 2, nch), in_specs=[qspec(0), kspec(8), kspec(16), qspec(0)],
              out_specs=[qspec(0), tspec, tspec], out_shape=[jax.ShapeDtypeStruct((R, 1024), _MXU), full, full])
    return _call(body, "sb_bwd", ("arbitrary", "arbitrary", "arbitrary"), kw, (u1, u1, u1, dycat), rider)


def _neg_expm1(x):
    series = -(x * (1.0 + x * (0.5 + x * (1.0 / 6.0 + x * (1.0 / 24.0)))))
    return jnp.where(x > -0.05, series, 1.0 - jnp.exp(x))


def _lru_gates(x, wa_ref, ba_ref, wx_ref, bx_ref, lam_ref):
    rs, is_ = [], []
    for n in range(LRU_BLOCKS):
        xb = x[:, 128 * n:128 * n + 128]
        rs.append(_dot(xb, wa_ref[n]))
        is_.append(_dot(xb, wx_ref[n]))
    r = _sigmoid(jnp.concatenate(rs, axis=1) + ba_ref[...])
    i = _sigmoid(jnp.concatenate(is_, axis=1) + bx_ref[...])
    sp = _softplus(-lam_ref[...])
    la = -LRU_C * r * sp
    a = jnp.exp(la)
    mult = jnp.sqrt(jnp.maximum(_neg_expm1(2.0 * la), 0.0))
    return r, i, sp, a, mult


def _lru_fwd(u1, ycat, conv_w, conv_b, wa, ba, wx, bx, lam, B, nch):
    R = u1.shape[0]

    def body(x_ref, xp_ref, gate_ref, cw_ref, cb_ref, wa_ref, ba_ref, wx_ref, bx_ref, lam_ref, ycat_in,
             out_ref, hs_ref, hc):
        c = pl.program_id(1)

        @pl.when(c == 0)
        def _():
            hc[...] = jnp.zeros_like(hc)

        x = _conv_pre(xp_ref[...], x_ref[...], cw_ref, cb_ref, 4)
        r, i, sp, a, mult = _lru_gates(x, wa_ref, ba_ref, wx_ref, bx_ref, lam_ref)
        b = jnp.where(_real_rows(c), mult * (i * x), 0.0)
        rows = _row_ids(CH)
        s = 1
        while s < CH:
            a_s = jnp.where(rows >= s, pltpu.roll(a, s, axis=0), 1.0)
            b_s = jnp.where(rows >= s, pltpu.roll(b, s, axis=0), 0.0)
            b = a * b_s + b
            a = a * a_s
            s *= 2
        h = a * hc[0:1, :] + b
        hs_ref[...] = h
        hc[0:1, :] = hs_ref[CH - 1:CH, :]
        out_ref[...] = (h * _gelu(gate_ref[...])).astype(out_ref.dtype)

    row = lambda col: pl.BlockSpec((CH, 1024), lambda b, c: (b * nch + c, col))
    vec = pl.BlockSpec((1, 1024), lambda b, c: (0, 0))
    wsp = pl.BlockSpec((LRU_BLOCKS, 128, 128), lambda b, c: (0, 0, 0))
    return pl.pallas_call(
        body, name="lru_fwd", grid=(B, nch),
        in_specs=[row(4), pl.BlockSpec((8, 1024), _prev8_map(nch, 4)), row(3),
                  pl.BlockSpec((4, 1024), lambda b, c: (0, 0)), vec, wsp, vec, wsp, vec, vec,
                  pl.BlockSpec(memory_space=pl.ANY)],
        out_specs=[row(1), row(0)],
        out_shape=[jax.ShapeDtypeStruct(ycat.shape, ycat.dtype), jax.ShapeDtypeStruct((R, 1024), F32)],
        scratch_shapes=[pltpu.VMEM((8, 1024), F32)],
        input_output_aliases={10: 0},
        compiler_params=_cparams(("parallel", "arbitrary")),
    )(u1, u1, u1, conv_w, conv_b, wa, ba, wx, bx, lam, ycat)


def _lru_bwd(dycat, u1, hs, conv_w, conv_b, wa, ba, wx, bx, lam, B, nch):
    R = u1.shape[0]

    def body(dy_ref, x_ref, xp_ref, gate_ref, hs_ref, hsp_ref, cw_ref, cb_ref, wa_ref, ba_ref, wx_ref, bx_ref, lam_ref,
             dgate_ref, dxc_ref, pg_ref, dwa_ref, dwx_ref, lc):
        c = nch - 1 - pl.program_id(1)

        @pl.when(pl.program_id(1) == 0)
        def _():
            lc[...] = jnp.zeros_like(lc)
            pg_ref[...] = jnp.zeros_like(pg_ref)
            dwa_ref[...] = jnp.zeros_like(dwa_ref)
            dwx_ref[...] = jnp.zeros_like(dwx_ref)

        x = _conv_pre(xp_ref[...], x_ref[...], cw_ref, cb_ref, 4)
        r, i, sp, a, mult = _lru_gates(x, wa_ref, ba_ref, wx_ref, bx_ref, lam_ref)
        h = hs_ref[...]
        hprev = _shift_down(hsp_ref[...], h, 1)
        gate = gate_ref[...]
        dy = dy_ref[...]
        dgate_ref[...] = (dy * h * _dgelu(gate)).astype(dgate_ref.dtype)
        rows = _row_ids(CH)
        lam_t = dy * _gelu(gate) + jnp.where(rows == CH - 1, lc[0:1, :], 0.0)
        coef = jnp.where(rows < CH - 1, pltpu.roll(a, CH - 1, axis=0), 0.0)
        s = 1
        while s < CH:
            c_s = jnp.where(rows < CH - s, pltpu.roll(coef, CH - s, axis=0), 1.0)
            l_s = jnp.where(rows < CH - s, pltpu.roll(lam_t, CH - s, axis=0), 0.0)
            lam_t = coef * l_s + lam_t
            coef = coef * c_s
            s *= 2
        lc[0:1, :] = jnp.sum(jnp.where(rows == 0, a * lam_t, 0.0), axis=0, keepdims=True)
        db = jnp.where(_real_rows(c), lam_t, 0.0)
        da = db * hprev
        dmult = db * (i * x)
        di = db * mult * x
        dx = db * mult * i
        pos = mult > 0.0
        dla = da * a + jnp.where(pos, -dmult * (a * a) / jnp.where(pos, mult, 1.0), 0.0)
        dr = dla * (-LRU_C * sp)
        pg_ref[2:3, :] += jnp.sum(dla * (LRU_C * r) * _sigmoid(-lam_ref[...]), axis=0, keepdims=True)
        dpr = dr * r * (1.0 - r)
        dpi = di * i * (1.0 - i)
        pg_ref[0:1, :] += jnp.sum(dpr, axis=0, keepdims=True)
        pg_ref[1:2, :] += jnp.sum(dpi, axis=0, keepdims=True)
        dxs = []
        for n in range(LRU_BLOCKS):
            blk = slice(128 * n, 128 * n + 128)
            dxs.append(dx[:, blk] + _dot_nt(dpr[:, blk], wa_ref[n]) + _dot_nt(dpi[:, blk], wx_ref[n]))
            dwa_ref[n] += _dot_tn(x[:, blk], dpr[:, blk])
            dwx_ref[n] += _dot_tn(x[:, blk], dpi[:, blk])
        dxc_ref[...] = jnp.concatenate(dxs, axis=1)

    rmap = lambda col: (lambda b, c: (b * nch + nch - 1 - c, col))
    row = lambda col: pl.BlockSpec((CH, 1024), rmap(col))
    prev = lambda col: pl.BlockSpec(
        (8, 1024), lambda b, c: (jnp.maximum((b * nch + nch - 1 - c) * (CH // 8) - 1, 0), col))
    vec = pl.BlockSpec((1, 1024), lambda b, c: (0, 0))
    wsp = pl.BlockSpec((LRU_BLOCKS, 128, 128), lambda b, c: (0, 0, 0))
    full = jax.ShapeDtypeStruct((R, 1024), F32)
    return pl.pallas_call(
        body, name="lru_bwd", grid=(B, nch),
        in_specs=[row(1), row(4), prev(4), row(3), row(0), prev(0),
                  pl.BlockSpec((4, 1024), lambda b, c: (0, 0)), vec, wsp, vec, wsp, vec, vec],
        out_specs=[row(0), row(0), pl.BlockSpec((None, 8, 1024), lambda b, c: (b, 0, 0)),
                   pl.BlockSpec((None, LRU_BLOCKS, 128, 128), lambda b, c: (b, 0, 0, 0)),
                   pl.BlockSpec((None, LRU_BLOCKS, 128, 128), lambda b, c: (b, 0, 0, 0))],
        out_shape=[jax.ShapeDtypeStruct((R, 1024), _MXU), full, jax.ShapeDtypeStruct((B, 8, 1024), F32),
                   jax.ShapeDtypeStruct((B, LRU_BLOCKS, 128, 128), F32),
                   jax.ShapeDtypeStruct((B, LRU_BLOCKS, 128, 128), F32)],
        scratch_shapes=[pltpu.VMEM((8, 1024), F32)],
        compiler_params=_cparams(("parallel", "arbitrary")),
    )(dycat, u1, u1, u1, hs, hs, conv_w, conv_b, wa, ba, wx, bx, lam)


_FFN_TC = FFN // 2


def _ffn_specs(nch):
    nt = FFN // _FFN_TC
    row = lambda off: pl.BlockSpec((CH, _FFN_TC), lambda b, c, j: (b * nch + c, off + j))
    prev = lambda off: pl.BlockSpec(
        (8, _FFN_TC), lambda b, c, j: (jnp.maximum((b * nch + c) * (CH // 8) - 1, 0), off + j))
    wsp = lambda off: pl.BlockSpec((3, _FFN_TC), lambda b, c, j: (0, off + j))
    bsp = lambda off: pl.BlockSpec((1, _FFN_TC), lambda b, c, j: (0, off + j))
    return nt, row, [row(0), prev(0), row(nt), prev(nt), wsp(0), wsp(nt), bsp(0), bsp(nt)]


def _ffn_act_fwd(uf, conv_w, conv_b, B, nch, rider=None):
    R = uf.shape[0]
    nt, row, specs = _ffn_specs(nch)

    def body(g_ref, gp_ref, u_ref, up_ref, wg_ref, wu_ref, bg_ref, bu_ref, o_ref):
        cg = _conv_pre(gp_ref[...], g_ref[...], wg_ref, bg_ref, 3)
        cu = _conv_pre(up_ref[...], u_ref[...], wu_ref, bu_ref, 3)
        o_ref[...] = jnp.where(_real_rows(pl.program_id(1)), _silu(cg) * cu, 0.0).astype(o_ref.dtype)

    kw = dict(grid=(B, nch, nt), in_specs=specs, out_specs=[row(0)],
              out_shape=[jax.ShapeDtypeStruct((R, FFN), _MXU)])
    return _call(body, "ffn_act_fwd", ("arbitrary", "arbitrary", "arbitrary"), kw,
                 (uf, uf, uf, uf, conv_w, conv_w, conv_b, conv_b), rider)


def _ffn_act_bwd(da, uf, conv_w, conv_b, nch, name, rider=None):
    R = uf.shape[0]
    nt = FFN // _FFN_TC
    nr = R // CH
    K = 3

    def body(da_ref, dan_ref, g_ref, gp_ref, gn_ref, u_ref, up_ref, un_ref, wg_ref, wu_ref, bg_ref, bu_ref,
             dug_ref, duu_ref, dwg_ref, dwu_ref):
        i = pl.program_id(1)

        @pl.when(i == 0)
        def _():
            dwg_ref[...] = jnp.zeros_like(dwg_ref)
            dwu_ref[...] = jnp.zeros_like(dwu_ref)

        c = i % nch
        ext = CH + 8
        rows = _row_ids(ext)
        follows = (c < nch - 1).astype(jnp.int32)
        keep = jnp.logical_and(c * CH + rows >= PAD, rows < CH + 8 * follows)
        dav = jnp.where(keep, jnp.concatenate([da_ref[...], dan_ref[...]], axis=0), 0.0)

        def conv_ext(x_ref, xp_ref, xn_ref, w_ref, b_ref):
            cat = jnp.concatenate([xp_ref[...], x_ref[...], xn_ref[...]], axis=0)
            shifted = [cat[8:]] + [pltpu.roll(cat, s, axis=0)[8:] for s in range(1, K)]
            acc = shifted[0] * w_ref[K - 1:K, :] + b_ref[...]
            for s in range(1, K):
                acc = acc + shifted[s] * w_ref[K - 1 - s:K - s, :]
            return acc, shifted

        cg, gsh = conv_ext(g_ref, gp_ref, gn_ref, wg_ref, bg_ref)
        cu, ush = conv_ext(u_ref, up_ref, un_ref, wu_ref, bu_ref)
        sg = _sigmoid(cg)
        dcg = dav * cu * (sg * (1.0 + cg * (1.0 - sg)))
        dcu = dav * (cg * sg)
        for dc, xsh, w_ref, din_ref, dw_ref in ((dcg, gsh, wg_ref, dug_ref, dwg_ref), (dcu, ush, wu_ref, duu_ref, dwu_ref)):
            dp = dc[:CH]
            din = dp * w_ref[K - 1:K, :]
            dw_ref[7:8, :] += jnp.sum(dp, axis=0, keepdims=True)
            dw_ref[K - 1:K, :] += jnp.sum(dp * xsh[0][:CH], axis=0, keepdims=True)
            for s in range(1, K):
                din = din + pltpu.roll(dc, ext - s, axis=0)[:CH] * w_ref[K - 1 - s:K - s, :]
                dw_ref[K - 1 - s:K - s, :] += jnp.sum(dp * xsh[s][:CH], axis=0, keepdims=True)
            din_ref[...] = din.astype(din_ref.dtype)

    row = lambda off: pl.BlockSpec((CH, _FFN_TC), lambda j, i: (i, off + j))
    prev = lambda off: pl.BlockSpec((8, _FFN_TC), lambda j, i: (jnp.maximum(i * (CH // 8) - 1, 0), off + j))
    nxt = lambda off: pl.BlockSpec(
        (8, _FFN_TC), lambda j, i: (jnp.minimum((i + 1) * (CH // 8), nr * (CH // 8) - 1), off + j))
    wsp = lambda off: pl.BlockSpec((K, _FFN_TC), lambda j, i: (0, off + j))
    bsp = lambda off: pl.BlockSpec((1, _FFN_TC), lambda j, i: (0, off + j))
    acc = pl.BlockSpec((8, _FFN_TC), lambda j, i: (0, j))
    half = jax.ShapeDtypeStruct((R, FFN), _MXU)
    dwsh = jax.ShapeDtypeStruct((8, FFN), F32)
    kw = dict(
        grid=(nt, nr),
        in_specs=[row(0), nxt(0), row(0), prev(0), nxt(0), row(nt), prev(nt), nxt(nt), wsp(0), wsp(nt), bsp(0), bsp(nt)],
        out_specs=[row(0), row(0), acc, acc],
        out_shape=[half, half, dwsh, dwsh])
    return _call(body, name, ("arbitrary", "arbitrary"), kw,
                 (da, da, uf, uf, uf, uf, uf, uf, conv_w, conv_w, conv_b, conv_b), rider)


def _head(h, g, target, B, nch):
    R = h.shape[0]

    def body(h_ref, g_ref, t_ref, dh_ref, loss_ref, dg_ref):
        c = pl.program_id(1)

        @pl.when(c == 0)
        def _():
            dh_ref[...] = jnp.zeros_like(dh_ref)
            loss_ref[...] = jnp.zeros_like(loss_ref)
            dg_ref[...] = jnp.zeros_like(dg_ref)

        @pl.when(c > 0)
        def _():
            x = h_ref[...]
            gv = g_ref[...]
            r = lax.rsqrt(jnp.mean(x * x, axis=-1, keepdims=True) + EPS)
            xhat = x * r
            e = xhat * gv - t_ref[...]
            loss_ref[...] += 0.5 * jnp.sum(jnp.mean(e * e, axis=-1, keepdims=True), axis=0, keepdims=True)
            dy = e * (1.0 / D)
            dg_ref[0:1, :] += jnp.sum(dy * xhat, axis=0, keepdims=True)
            dx = dy * gv
            dh_ref[...] = r * (dx - xhat * jnp.mean(dx * xhat, axis=-1, keepdims=True))

    row = pl.BlockSpec((CH, D), lambda b, c: (b * nch + c, 0))
    return pl.pallas_call(
        body, name="head", grid=(B, nch),
        in_specs=[row, pl.BlockSpec((1, D), lambda b, c: (0, 0)),
                  pl.BlockSpec((CH, D), lambda b, c: (b * (nch - 1) + jnp.maximum(c - 1, 0), 0))],
        out_specs=[row, pl.BlockSpec((None, 8, 128), lambda b, c: (b, 0, 0)),
                   pl.BlockSpec((None, 8, D), lambda b, c: (b, 0, 0))],
        out_shape=[jax.ShapeDtypeStruct((R, D), F32), jax.ShapeDtypeStruct((B, 8, 128), F32),
                   jax.ShapeDtypeStruct((B, 8, D), F32)],
        compiler_params=_cparams(("parallel", "arbitrary")),
    )(h, g, target)


ADAM_LR = 0.001
ADAM_B1 = 0.9
ADAM_B2 = 0.999
ADAM_EPS = 1e-08
ADAM_WD = 0.01
ADAM_STEP = 10


def _adamw(w, g, m, v, name):
    Rr, C = w.shape
    tr = _tile(Rr, (256, 64))

    def body(w_ref, g_ref, m_ref, v_ref, d_ref, nm_ref, nv_ref):
        gv = g_ref[...]
        nm = ADAM_B1 * m_ref[...] + (1.0 - ADAM_B1) * gv
        nv = ADAM_B2 * v_ref[...] + (1.0 - ADAM_B2) * (gv * gv)
        m_hat = nm / (1.0 - ADAM_B1 ** ADAM_STEP)
        v_hat = nv / (1.0 - ADAM_B2 ** ADAM_STEP)
        d_ref[...] = -ADAM_LR * (m_hat / (jnp.sqrt(v_hat) + ADAM_EPS) + ADAM_WD * w_ref[...])
        nm_ref[...] = nm
        nv_ref[...] = nv

    spec = pl.BlockSpec((tr, C), lambda i: (i, 0))
    sh = jax.ShapeDtypeStruct((Rr, C), F32)
    return pl.pallas_call(
        body, name=name, grid=(Rr // tr,),
        in_specs=[spec] * 4, out_specs=[spec] * 3, out_shape=[sh] * 3,
        compiler_params=_cparams(("parallel",)),
    )(w, g, m, v)


_MESH = pl.DeviceIdType.MESH
_ANY = pl.BlockSpec(memory_space=pl.ANY)


def _place():
    x, y, c = lax.axis_index("x"), lax.axis_index("y"), lax.axis_index("c")
    chips = [(1 - x, y), (x, 1 - y), (1 - x, 1 - y)]
    return x, y, c, chips


def _rcopy(src, dst, ssem, rsem, dev):
    return pltpu.make_async_remote_copy(src_ref=src, dst_ref=dst, send_sem=ssem, recv_sem=rsem,
                                        device_id=dev, device_id_type=_MESH)


def _with_riders(body, kw, kind, riders):
    n_in, n_out, n_scr = len(kw["in_specs"]), len(kw["out_specs"]), len(kw.get("scratch_shapes", []))
    grid = kw["grid"]
    nr = len(riders)
    nsem = 4 if kind == "gather" else 2

    def new_body(*refs):
        ins, srcs = refs[:n_in], refs[n_in:n_in + nr]
        outs, dsts = refs[n_in + nr:n_in + nr + n_out], refs[n_in + nr + n_out:n_in + 2 * nr + n_out]
        scr = refs[n_in + 2 * nr + n_out:n_in + 2 * nr + n_out + n_scr]
        sems = refs[n_in + 2 * nr + n_out + n_scr:]
        first = last = None
        for axis, size in enumerate(grid):
            i = pl.program_id(axis)
            first = (i == 0) if first is None else jnp.logical_and(first, i == 0)
            last = (i == size - 1) if last is None else jnp.logical_and(last, i == size - 1)
        x, y, c, chips = _place()
        k = 2 * x + y
        sib = (x, y, 1 - c)
        ssem, rsem = sems[:2]
        sends = []
        for a in range(nr):
            for j, (cx, cy) in enumerate(chips):
                if kind == "gather":
                    src, dst = srcs[a].at[c], dsts[a].at[k, c]
                else:
                    src, dst = srcs[a].at[2 * cx + cy], dsts[a].at[k]
                sends.append(_rcopy(src, dst, ssem.at[3 * a + j], rsem.at[3 * a + j], (cx, cy, c)))

        @pl.when(first)
        def _():
            for cp in sends:
                cp.start()

        body(*ins, *outs, *scr)

        @pl.when(last)
        def _():
            passed = []
            for a in range(nr):
                for j, (cx, cy) in enumerate(chips):
                    got = dsts[a].at[2 * cx + cy, c] if kind == "gather" else dsts[a].at[2 * cx + cy]
                    _rcopy(got, got, ssem.at[3 * a + j], rsem.at[3 * a + j], (cx, cy, c)).wait_recv()
                    if kind == "gather":
                        fw = _rcopy(got, got, sems[2].at[3 * a + j], sems[3].at[3 * a + j], sib)
                        fw.start()
                        passed.append(fw)
            if kind == "gather":
                for a in range(nr):
                    for j, (cx, cy) in enumerate(chips):
                        got = dsts[a].at[2 * cx + cy, 1 - c]
                        _rcopy(got, got, sems[2].at[3 * a + j], sems[3].at[3 * a + j], sib).wait_recv()
            for cp in sends + passed:
                cp.wait_send()

    kw = dict(kw)
    kw["in_specs"] = list(kw["in_specs"]) + [_ANY] * nr
    kw["out_specs"] = list(kw["out_specs"]) + [_ANY] * nr
    kw["out_shape"] = list(kw["out_shape"]) + [
        jax.ShapeDtypeStruct(((4,) + r.shape) if kind == "gather" else r.shape, r.dtype) for r in riders]
    kw["scratch_shapes"] = list(kw.get("scratch_shapes", [])) + [pltpu.SemaphoreType.DMA((3 * nr,))] * nsem
    return new_body, kw


def _call(body, name, sem, kw, args, rider=None):
    if rider is not None:
        body, kw = _with_riders(body, kw, *rider)
        args = tuple(args) + tuple(rider[1])
    return pl.pallas_call(body, name=name, compiler_params=_cparams(sem), **kw)(*args)


def _fill_own(result, own, chip):
    return lax.dynamic_update_index_in_dim(result, own, chip, 0)


def _gather_shards(bigs, small):
    nb = len(bigs)

    def body(*refs):
        ins, outs = refs[:nb + 1], refs[nb + 1:2 * nb + 2]
        ssem, rsem, fssem, frsem = refs[2 * nb + 2:]
        x, y, c, chips = _place()
        k = 2 * x + y
        sib = (x, y, 1 - c)

        def part(a, slot, hc):
            return outs[a].at[slot] if a == nb else outs[a].at[slot, hc]

        first = []
        for a in range(nb + 1):
            src = ins[a] if a == nb else ins[a].at[c]
            for j, (cx, cy) in enumerate(chips):
                first.append(_rcopy(src, part(a, k, c), ssem.at[3 * a + j], rsem.at[3 * a + j], (cx, cy, c)))
        for cp in first:
            cp.start()
        passed = []
        for a in range(nb + 1):
            for j, (cx, cy) in enumerate(chips):
                got = part(a, 2 * cx + cy, c)
                _rcopy(got, got, ssem.at[3 * a + j], rsem.at[3 * a + j], (cx, cy, c)).wait_recv()
                if a < nb:
                    fw = _rcopy(got, got, fssem.at[3 * a + j], frsem.at[3 * a + j], sib)
                    fw.start()
                    passed.append(fw)
        for a in range(nb):
            for j, (cx, cy) in enumerate(chips):
                got = part(a, 2 * cx + cy, 1 - c)
                _rcopy(got, got, fssem.at[3 * a + j], frsem.at[3 * a + j], sib).wait_recv()
        for cp in first + passed:
            cp.wait_send()

    arrs = list(bigs) + [small]
    n = 3 * (nb + 1)
    return pl.pallas_call(
        body, name="gather_shards",
        in_specs=[_ANY] * (nb + 1), out_specs=[_ANY] * (nb + 1),
        out_shape=[jax.ShapeDtypeStruct((4,) + a.shape, a.dtype) for a in arrs],
        scratch_shapes=[pltpu.SemaphoreType.DMA((n,)), pltpu.SemaphoreType.DMA((n,)),
                        pltpu.SemaphoreType.DMA((n,)), pltpu.SemaphoreType.DMA((n,))],
    )(*arrs)


def _swap_halves(grads, name):
    na = len(grads)
    halves = [g.shape[1] // 2 for g in grads]

    def body(*refs):
        ins, outs = refs[:na], refs[na:2 * na]
        ssem, rsem = refs[2 * na:]
        x, y, c, _ = _place()
        sib = (x, y, 1 - c)
        cps = [_rcopy(ins[a].at[:, pl.ds((1 - c) * halves[a], halves[a]), :], outs[a], ssem.at[a], rsem.at[a], sib)
               for a in range(na)]
        for cp in cps:
            cp.start()
        for cp in cps:
            cp.wait()

    return pl.pallas_call(
        body, name=name,
        in_specs=[_ANY] * na, out_specs=[_ANY] * na,
        out_shape=[jax.ShapeDtypeStruct((4, g.shape[1] // 2, g.shape[2]), g.dtype) for g in grads],
        scratch_shapes=[pltpu.SemaphoreType.DMA((na,)), pltpu.SemaphoreType.DMA((na,))],
    )(*grads)


def _sum_rows(rh):
    return rh if rh <= 512 else _tile(rh, (512, 256, 128, 64, 32))


def _chip_sum(grad, recv, core, name):
    _, r, cdim = grad.shape
    rh = r // 2
    tr = _sum_rows(rh)
    nblk = rh // tr

    def body(core_ref, g_ref, r_ref, o_ref):
        o_ref[...] = (g_ref[...] + r_ref[...]).astype(o_ref.dtype)

    return pl.pallas_call(
        body, name=name,
        grid_spec=pltpu.PrefetchScalarGridSpec(
            num_scalar_prefetch=1, grid=(4, nblk),
            in_specs=[pl.BlockSpec((None, tr, cdim), lambda s, i, cr: (s, cr[0] * nblk + i, 0)),
                      pl.BlockSpec((None, tr, cdim), lambda s, i, cr: (s, i, 0))],
            out_specs=pl.BlockSpec((None, tr, cdim), lambda s, i, cr: (s, i, 0))),
        out_shape=jax.ShapeDtypeStruct((4, rh, cdim), BF16),
        compiler_params=_cparams(("parallel", "parallel")),
    )(core, grad, recv)


def _scatter_sums(sums):
    na = len(sums)

    def body(*refs):
        ins, outs = refs[:na], refs[na:2 * na]
        ssem, rsem, lsem = refs[2 * na:]
        x, y, c, chips = _place()
        k = 2 * x + y
        local = [pltpu.make_async_copy(ins[a].at[k], outs[a].at[k], lsem.at[a]) for a in range(na)]
        for cp in local:
            cp.start()
        cps = []
        for a in range(na):
            for j, (cx, cy) in enumerate(chips):
                cps.append(_rcopy(ins[a].at[2 * cx + cy], outs[a].at[k], ssem.at[3 * a + j], rsem.at[3 * a + j],
                                  (cx, cy, c)))
        for cp in cps:
            cp.start()
        for a in range(na):
            for j, (cx, cy) in enumerate(chips):
                got = outs[a].at[2 * cx + cy]
                _rcopy(got, got, ssem.at[3 * a + j], rsem.at[3 * a + j], (cx, cy, c)).wait_recv()
        for cp in cps:
            cp.wait_send()
        for cp in local:
            cp.wait()

    return pl.pallas_call(
        body, name="scatter_sums",
        in_specs=[_ANY] * na, out_specs=[_ANY] * na,
        out_shape=[jax.ShapeDtypeStruct(s.shape, s.dtype) for s in sums],
        scratch_shapes=[pltpu.SemaphoreType.DMA((3 * na,)), pltpu.SemaphoreType.DMA((3 * na,)),
                        pltpu.SemaphoreType.DMA((na,))],
    )(*sums)


def _sum_chips(parts, name):
    _, rh, cdim = parts.shape
    tr = _sum_rows(rh)

    def body(p_ref, o_ref):
        acc = p_ref[0].astype(F32)
        for j in range(1, 4):
            acc = acc + p_ref[j].astype(F32)
        o_ref[...] = acc

    return pl.pallas_call(
        body, name=name, grid=(rh // tr,),
        in_specs=[pl.BlockSpec((4, tr, cdim), lambda i: (0, i, 0))],
        out_specs=pl.BlockSpec((tr, cdim), lambda i: (i, 0)),
        out_shape=jax.ShapeDtypeStruct((rh, cdim), F32),
        compiler_params=_cparams(("parallel",)),
    )(parts)


def _join_halves(reds):
    na = len(reds)

    def body(*refs):
        ins, outs = refs[:na], refs[na:2 * na]
        ssem, rsem = refs[2 * na:]
        x, y, c, _ = _place()
        cps = [_rcopy(ins[a], outs[a], ssem.at[a], rsem.at[a], (x, y, 1 - c)) for a in range(na)]
        for cp in cps:
            cp.start()
        for cp in cps:
            cp.wait()

    return pl.pallas_call(
        body, name="join_halves",
        in_specs=[_ANY] * na, out_specs=[_ANY] * na,
        out_shape=[jax.ShapeDtypeStruct(r.shape, r.dtype) for r in reds],
        scratch_shapes=[pltpu.SemaphoreType.DMA((na,)), pltpu.SemaphoreType.DMA((na,))],
    )(*reds)


def _allreduce_small(buf):
    n = buf.shape[0]

    def body(in_ref, out_ref, recv, ssem, rsem):
        x, y, c, _ = _place()
        peers = [(x, y, 1 - c), (1 - x, y, c), (x, 1 - y, c)]
        out_ref[...] = in_ref[...]
        for r, peer in enumerate(peers):
            cp = _rcopy(out_ref, recv.at[r], ssem.at[r], rsem.at[r], peer)
            cp.start()
            cp.wait()
            out_ref[...] = out_ref[...] + recv[r]

    vm = pl.BlockSpec(memory_space=pltpu.VMEM)
    return pl.pallas_call(
        body, name="allreduce_small",
        in_specs=[vm], out_specs=vm,
        out_shape=jax.ShapeDtypeStruct(buf.shape, F32),
        scratch_shapes=[pltpu.VMEM((3, n, 128), F32), pltpu.SemaphoreType.DMA((3,)), pltpu.SemaphoreType.DMA((3,))],
        compiler_params=pltpu.CompilerParams(vmem_limit_bytes=VMEM_LIMIT),
    )(buf)


_W_NAMES = ['meta_tokens', 'l0_mix_norm', 'l0_w_in', 'l0_ssd_conv_w', 'l0_ssd_conv_b', 'l0_ssd_dt_bias', 'l0_ssd_a_log',
            'l0_ssd_d', 'l0_ssd_norm', 'l0_ret_norm', 'l0_w_out', 'l0_ffn_norm', 'l0_ffn_w_in', 'l0_ffn_conv_w',
            'l0_ffn_conv_b', 'l0_ffn_w_out', 'l1_mix_norm', 'l1_w_in', 'l1_lru_conv_w', 'l1_lru_conv_b', 'l1_lru_wa',
            'l1_lru_ba', 'l1_lru_wx', 'l1_lru_bx', 'l1_lru_lambda', 'l1_w_out', 'l1_ffn_norm', 'l1_ffn_w_in',
            'l1_ffn_conv_w', 'l1_ffn_conv_b', 'l1_ffn_w_out', 'final_norm']
_IN_NAMES = ['x'] + _W_NAMES + ['loss_target'] + ['m_' + n for n in _W_NAMES] + ['v_' + n for n in _W_NAMES]
_BIG = ['l0_w_in', 'l0_w_out', 'l0_ffn_w_in', 'l0_ffn_w_out', 'l1_w_in', 'l1_w_out', 'l1_ffn_w_in', 'l1_ffn_w_out']
_BIG_COLS = ('l0_w_in', 'l0_ffn_w_in', 'l1_w_in', 'l1_ffn_w_in')
_SMALL_SHARDED = ['meta_tokens', 'l0_ssd_conv_w', 'l0_ffn_conv_w', 'l1_lru_conv_w', 'l1_ffn_conv_w']
_SMALL = [n for n in _W_NAMES if n not in _BIG]


def _pack(arrs):
    flat = []
    for a in arrs:
        v = a.reshape(-1).astype(F32)
        flat.append(jnp.pad(v, (0, (-v.shape[0]) % 128)))
    v = jnp.concatenate(flat)
    v = jnp.pad(v, (0, (-v.shape[0]) % 1024))
    return v.reshape(-1, 128)


def _unpack(buf, shapes):
    out, row = [], 0
    for sh in shapes:
        n = int(np.prod(sh))
        rows = -(-n // 128)
        out.append(buf[row:row + rows].reshape(-1)[:n].reshape(sh))
        row += rows
    return out


def kernel(x, meta_tokens, l0_mix_norm, l0_w_in, l0_ssd_conv_w, l0_ssd_conv_b, l0_ssd_dt_bias, l0_ssd_a_log, l0_ssd_d, l0_ssd_norm, l0_ret_norm, l0_w_out, l0_ffn_norm, l0_ffn_w_in, l0_ffn_conv_w, l0_ffn_conv_b, l0_ffn_w_out, l1_mix_norm, l1_w_in, l1_lru_conv_w, l1_lru_conv_b, l1_lru_wa, l1_lru_ba, l1_lru_wx, l1_lru_bx, l1_lru_lambda, l1_w_out, l1_ffn_norm, l1_ffn_w_in, l1_ffn_conv_w, l1_ffn_conv_b, l1_ffn_w_out, final_norm, loss_target, m_meta_tokens, m_l0_mix_norm, m_l0_w_in, m_l0_ssd_conv_w, m_l0_ssd_conv_b, m_l0_ssd_dt_bias, m_l0_ssd_a_log, m_l0_ssd_d, m_l0_ssd_norm, m_l0_ret_norm, m_l0_w_out, m_l0_ffn_norm, m_l0_ffn_w_in, m_l0_ffn_conv_w, m_l0_ffn_conv_b, m_l0_ffn_w_out, m_l1_mix_norm, m_l1_w_in, m_l1_lru_conv_w, m_l1_lru_conv_b, m_l1_lru_wa, m_l1_lru_ba, m_l1_lru_wx, m_l1_lru_bx, m_l1_lru_lambda, m_l1_w_out, m_l1_ffn_norm, m_l1_ffn_w_in, m_l1_ffn_conv_w, m_l1_ffn_conv_b, m_l1_ffn_w_out, m_final_norm, v_meta_tokens, v_l0_mix_norm, v_l0_w_in, v_l0_ssd_conv_w, v_l0_ssd_conv_b, v_l0_ssd_dt_bias, v_l0_ssd_a_log, v_l0_ssd_d, v_l0_ssd_norm, v_l0_ret_norm, v_l0_w_out, v_l0_ffn_norm, v_l0_ffn_w_in, v_l0_ffn_conv_w, v_l0_ffn_conv_b, v_l0_ffn_w_out, v_l1_mix_norm, v_l1_w_in, v_l1_lru_conv_w, v_l1_lru_conv_b, v_l1_lru_wa, v_l1_lru_ba, v_l1_lru_wx, v_l1_lru_bx, v_l1_lru_lambda, v_l1_w_out, v_l1_ffn_norm, v_l1_ffn_w_in, v_l1_ffn_conv_w, v_l1_ffn_conv_b, v_l1_ffn_w_out, v_final_norm):
    args = (x, meta_tokens, l0_mix_norm, l0_w_in, l0_ssd_conv_w, l0_ssd_conv_b, l0_ssd_dt_bias, l0_ssd_a_log, l0_ssd_d, l0_ssd_norm, l0_ret_norm, l0_w_out, l0_ffn_norm, l0_ffn_w_in, l0_ffn_conv_w, l0_ffn_conv_b, l0_ffn_w_out, l1_mix_norm, l1_w_in, l1_lru_conv_w, l1_lru_conv_b, l1_lru_wa, l1_lru_ba, l1_lru_wx, l1_lru_bx, l1_lru_lambda, l1_w_out, l1_ffn_norm, l1_ffn_w_in, l1_ffn_conv_w, l1_ffn_conv_b, l1_ffn_w_out, final_norm, loss_target, m_meta_tokens, m_l0_mix_norm, m_l0_w_in, m_l0_ssd_conv_w, m_l0_ssd_conv_b, m_l0_ssd_dt_bias, m_l0_ssd_a_log, m_l0_ssd_d, m_l0_ssd_norm, m_l0_ret_norm, m_l0_w_out, m_l0_ffn_norm, m_l0_ffn_w_in, m_l0_ffn_conv_w, m_l0_ffn_conv_b, m_l0_ffn_w_out, m_l1_mix_norm, m_l1_w_in, m_l1_lru_conv_w, m_l1_lru_conv_b, m_l1_lru_wa, m_l1_lru_ba, m_l1_lru_wx, m_l1_lru_bx, m_l1_lru_lambda, m_l1_w_out, m_l1_ffn_norm, m_l1_ffn_w_in, m_l1_ffn_conv_w, m_l1_ffn_conv_b, m_l1_ffn_w_out, m_final_norm, v_meta_tokens, v_l0_mix_norm, v_l0_w_in, v_l0_ssd_conv_w, v_l0_ssd_conv_b, v_l0_ssd_dt_bias, v_l0_ssd_a_log, v_l0_ssd_d, v_l0_ssd_norm, v_l0_ret_norm, v_l0_w_out, v_l0_ffn_norm, v_l0_ffn_w_in, v_l0_ffn_conv_w, v_l0_ffn_conv_b, v_l0_ffn_w_out, v_l1_mix_norm, v_l1_w_in, v_l1_lru_conv_w, v_l1_lru_conv_b, v_l1_lru_wa, v_l1_lru_ba, v_l1_lru_wx, v_l1_lru_bx, v_l1_lru_lambda, v_l1_w_out, v_l1_ffn_norm, v_l1_ffn_w_in, v_l1_ffn_conv_w, v_l1_ffn_conv_b, v_l1_ffn_w_out, v_final_norm)
    p = dict(zip(_IN_NAMES, args))
    B, seq, _ = x.shape
    nch = (seq + CH) // CH
    Pn = nch * CH
    R = B * Pn
    chip = 2 * lax.axis_index("x") + lax.axis_index("y")
    row2 = lambda v: v.reshape(1, -1)
    pad128 = lambda v: jnp.pad(v, (0, 128 - v.shape[0])).reshape(1, 128)

    small_shapes = [p[n].shape for n in _SMALL_SHARDED]
    halved = lambda w: w.astype(_MXU).reshape(2, w.shape[0] // 2, w.shape[1])
    mine = {n: halved(p[n]) for n in _BIG}
    mine_small = _pack([p[n] for n in _SMALL_SHARDED])
    W = {}

    def set_weight(n, g):
        g = _fill_own(g, mine[n], chip)
        g = g.reshape(4, -1, g.shape[3])
        W[n] = jnp.concatenate([g[k] for k in range(4)], axis=1) if n in _BIG_COLS else g.reshape(-1, g.shape[2])

    def gather_on(*names):
        return ("gather", [mine[n] for n in names])

    def take_weights(names, got):
        for n, g in zip(names, got):
            set_weight(n, g)

    gathered = _gather_shards([mine['l0_w_in']], mine_small)
    set_weight('l0_w_in', gathered[0])
    g_small = _fill_own(gathered[-1], mine_small, chip)
    per_chip = [_unpack(g_small[k], small_shapes) for k in range(4)]
    for i, n in enumerate(_SMALL_SHARDED):
        W[n] = jnp.concatenate([per_chip[k][i] for k in range(4)], axis=1)
    w0 = W['l0_w_in']
    w0_main = jnp.concatenate([w0[:, 3088:], w0[:, :3072]], axis=1)
    w0_dt = jnp.pad(w0[:, 3072:3088], ((0, 0), (0, 112)))
    cos, sin = _rope_tables(nch)

    meta = jnp.broadcast_to(W['meta_tokens'][None], (B, N_META, D))
    h0 = jnp.concatenate([jnp.zeros((B, PAD, D), F32), meta, x], axis=1).reshape(R, D)
    n0, n0t = _rmsnorm_fwd(h0, row2(p['l0_mix_norm']), "norm_l0_mix")
    u0 = _mm(n0, w0_main, "nn", F32, "l0_in_proj")
    udt = _mm(n0, w0_dt, "nn", F32, "l0_dt_proj")
    a_log, d_skip, dt_bias = pad128(p['l0_ssd_a_log']), pad128(p['l0_ssd_d']), pad128(p['l0_ssd_dt_bias'])
    ssd_cb = row2(p['l0_ssd_conv_b'])
    act, dt, dtt, *got = _ssd_prep(u0, udt, W['l0_ssd_conv_w'], ssd_cb, dt_bias, B, nch, rider=gather_on('l0_w_out'))
    take_weights(['l0_w_out'], got)
    ycat0, ypre, hin, *got = _ssd_fwd(act, u0, dt, dtt, a_log, d_skip, row2(p['l0_ssd_norm']), B, nch,
                                      rider=gather_on('l0_ffn_w_in'))
    take_weights(['l0_ffn_w_in'], got)
    ycat0, opre, rin, *got = _ret_fwd(u0, ycat0, cos, sin, row2(p['l0_ret_norm']), B, nch,
                                      rider=gather_on('l0_ffn_w_out'))
    take_weights(['l0_ffn_w_out'], got)
    h1 = _mm(ycat0, W['l0_w_out'], "nn", F32, "l0_out_proj", add=h0)
    n1, n1t = _rmsnorm_fwd(h1, row2(p['l0_ffn_norm']), "norm_l0_ffn")
    uf0 = _mm(n1, W['l0_ffn_w_in'], "nn", F32, "l0_ffn_in")
    f0_cb = row2(p['l0_ffn_conv_b'])
    a0, *got = _ffn_act_fwd(uf0, W['l0_ffn_conv_w'], f0_cb, B, nch, rider=gather_on('l1_w_in'))
    take_weights(['l1_w_in'], got)
    h2 = _mm(a0, W['l0_ffn_w_out'], "nn", F32, "l0_ffn_out", add=h1)
    n2, n2t = _rmsnorm_fwd(h2, row2(p['l1_mix_norm']), "norm_l1_mix")
    u1 = _mm(n2, W['l1_w_in'], "nn", F32, "l1_in_proj")
    lru = (W['l1_lru_conv_w'], row2(p['l1_lru_conv_b']), p['l1_lru_wa'], row2(p['l1_lru_ba']), p['l1_lru_wx'],
           row2(p['l1_lru_bx']), row2(p['l1_lru_lambda']))
    later = ['l1_w_out', 'l1_ffn_w_in', 'l1_ffn_w_out']
    ycat1, *got = _sb_fwd(u1, B, nch, rider=gather_on(*later))
    take_weights(later, got)
    ycat1, hs = _lru_fwd(u1, ycat1, *lru, B, nch)
    h3 = _mm(ycat1, W['l1_w_out'], "nn", F32, "l1_out_proj", add=h2)
    n3, n3t = _rmsnorm_fwd(h3, row2(p['l1_ffn_norm']), "norm_l1_ffn")
    uf1 = _mm(n3, W['l1_ffn_w_in'], "nn", F32, "l1_ffn_in")
    f1_cb = row2(p['l1_ffn_conv_b'])
    a1, = _ffn_act_fwd(uf1, W['l1_ffn_conv_w'], f1_cb, B, nch)
    h4 = _mm(a1, W['l1_ffn_w_out'], "nn", F32, "l1_ffn_out", add=h3)
    dh4, lossp, dgf = _head(h4, row2(p['final_norm']), p['loss_target'].reshape(B * seq, D), B, nch)
    loss = lax.psum(jnp.sum(lossp[:, 0, 0]), ("x", "y", "c"))

    G = {'final_norm': dgf[:, 0].sum(0)}

    core = lax.axis_index("c").reshape(1).astype(jnp.int32)

    def chip_sums(names, tag):
        stacked = []
        for n in names:
            g = G[n]
            if n in _BIG_COLS:
                stacked.append(g.reshape(g.shape[0], 4, g.shape[1] // 4).transpose(1, 0, 2))
            else:
                stacked.append(g.reshape(4, g.shape[0] // 4, g.shape[1]))
        theirs = _swap_halves(stacked, "swap_halves_" + tag)
        return {n: _chip_sum(g, t, core, "chip_sum_" + n) for n, g, t in zip(names, stacked, theirs)}

    parts = {}

    def scatter_on(names, tag):
        sums = chip_sums(names, tag)
        return sums, ("scatter", [sums[n] for n in names])

    def take_parts(names, sums, got):
        for n, g in zip(names, got):
            parts[n] = _fill_own(g, lax.dynamic_index_in_dim(sums[n], chip, 0, keepdims=False), chip)

    def ffn_bwd(layer, dh_out, h_in, nt_in, uf, a_act, cb, rider=None):
        pre = f"l{layer}_"
        w_in, w_out, cw = W[pre + 'ffn_w_in'], W[pre + 'ffn_w_out'], W[pre + 'ffn_conv_w']
        da = _mm(dh_out, w_out, "nt", F32, pre + "ffn_out_dgrad")
        G[pre + 'ffn_w_out'] = _mm(a_act, dh_out, "tn", F32, pre + "ffn_out_wgrad")
        dug, duu, dwg, dwu, *rode = _ffn_act_bwd(da, uf, cw, cb, nch, pre + "ffn_act_bwd", rider=rider)
        G[pre + 'ffn_conv_w'] = jnp.concatenate([dwg[:3], dwu[:3]], axis=1)
        G[pre + 'ffn_conv_b'] = jnp.concatenate([dwg[7], dwu[7]])
        dn = _mm(dug, w_in, "nt", F32, pre + "ffn_in_dgrad_g")
        dn = _mm(duu, w_in, "nt", F32, pre + "ffn_in_dgrad_u", add=dn, b_off=FFN)
        G[pre + 'ffn_w_in'] = jnp.concatenate([_mm(nt_in, dug, "nn", F32, pre + "ffn_in_wgrad_g"),
                                               _mm(nt_in, duu, "nn", F32, pre + "ffn_in_wgrad_u")], axis=1)
        dh_in, dg = _rmsnorm_bwd(h_in, row2(p[pre + 'ffn_norm']), dn, dh_out, nch, pre + "ffn_norm_bwd")
        G[pre + 'ffn_norm'] = dg[0]
        return dh_in, rode

    dh3, _ = ffn_bwd(1, dh4, h3, n3t, uf1, a1, f1_cb)
    dy1 = _mm(dh3, W['l1_w_out'], "nt", F32, "l1_out_dgrad")
    G['l1_w_out'] = _mm(ycat1, dh3, "tn", F32, "l1_out_wgrad")
    done = ['l1_ffn_w_in', 'l1_ffn_w_out', 'l1_w_out']
    sums, rider = scatter_on(done, "a")
    dq, dkt, dvt, *got = _sb_bwd(dy1, u1, B, nch, rider=rider)
    dk, dv = dkt.T, dvt.T
    take_parts(done, sums, got)
    dgate, dxc, pgl, dwa, dwx = _lru_bwd(dy1, u1, hs, *lru, B, nch)
    dxr, dcw = _conv_bwd(dxc, u1, 4096, W['l1_lru_conv_w'], 4, "l1_lru_conv_bwd")
    pgl = pgl.sum(0)
    G['l1_lru_ba'], G['l1_lru_bx'], G['l1_lru_lambda'] = pgl[0], pgl[1], pgl[2]
    G['l1_lru_wa'], G['l1_lru_wx'] = dwa.sum(0), dwx.sum(0)
    G['l1_lru_conv_w'], G['l1_lru_conv_b'] = dcw[:4], dcw[7]
    dn, dws = None, []
    for i, piece in enumerate((dq, dk, dv, dgate, dxr)):
        dn = _mm(piece, W['l1_w_in'], "nt", F32, f"l1_in_dgrad_{i}", add=dn, b_off=1024 * i)
        dws.append(_mm(n2t, piece, "nn", F32, f"l1_in_wgrad_{i}"))
    G['l1_w_in'] = jnp.concatenate(dws, axis=1)
    dh2, dg = _rmsnorm_bwd(h2, row2(p['l1_mix_norm']), dn, dh3, nch, "l1_mix_norm_bwd")
    G['l1_mix_norm'] = dg[0]

    sums, rider = scatter_on(['l1_w_in'], "b")
    dh1, got = ffn_bwd(0, dh2, h1, n1t, uf0, a0, f0_cb, rider=rider)
    take_parts(['l1_w_in'], sums, got)
    dy0 = _mm(dh1, W['l0_w_out'], "nt", F32, "l0_out_dgrad")
    G['l0_w_out'] = _mm(ycat0, dh1, "tn", F32, "l0_out_wgrad")
    done = ['l0_ffn_w_in', 'l0_ffn_w_out', 'l0_w_out']
    sums, rider = scatter_on(done, "c")
    dz, dxs, dbm, dcm, ddt4, pgs, *got = _ssd_bwd(dy0, ypre, u0, act, dt, dtt, hin, a_log, d_skip,
                                                  row2(p['l0_ssd_norm']), B, nch, rider=rider)
    take_parts(done, sums, got)
    dpre, ddtr, pgd = _ssd_prep_bwd(dxs, dbm, dcm, ddt4, u0, udt, W['l0_ssd_conv_w'], ssd_cb, dt_bias, B, nch)
    dxbc, dcw0 = _conv_bwd(dpre, u0, U0_XBC, W['l0_ssd_conv_w'], 4, "l0_ssd_conv_bwd")
    dqkvg, pgr = _ret_bwd(dy0, u0, opre, rin, cos, sin, row2(p['l0_ret_norm']), B, nch)
    pgs = pgs.sum(0)
    G['l0_ssd_norm'] = pgs[:, 0, :].reshape(-1)
    G['l0_ssd_d'] = pgs[:, 1, :128].sum(0)[:SSD_HEADS]
    G['l0_ssd_a_log'] = pgs[:, 2, :128].sum(0)[:SSD_HEADS]
    G['l0_ssd_dt_bias'] = pgd.sum(0)[0, :SSD_HEADS]
    G['l0_ssd_conv_w'], G['l0_ssd_conv_b'] = dcw0[:4], dcw0[7]
    G['l0_ret_norm'] = pgr.sum(0)[0]
    dn = _mm(dqkvg, w0_main, "nt", F32, "l0_in_dgrad_qkvg")
    dn = _mm(dz, w0_main, "nt", F32, "l0_in_dgrad_z", add=dn, b_off=U0_Z)
    dn = _mm(dxbc, w0_main, "nt", F32, "l0_in_dgrad_xbc", add=dn, b_off=U0_XBC)
    dn = _mm(ddtr, w0_dt, "nt", F32, "l0_in_dgrad_dt", add=dn)
    G['l0_w_in'] = jnp.concatenate([
        _mm(n0t, dz, "nn", F32, "l0_in_wgrad_z"), _mm(n0t, dxbc, "nn", F32, "l0_in_wgrad_xbc"),
        _mm(n0t, ddtr, "nn", F32, "l0_in_wgrad_dt")[:, :SSD_HEADS], _mm(n0t, dqkvg, "nn", F32, "l0_in_wgrad_qkvg")], axis=1)
    dh0, dg = _rmsnorm_bwd(h0, row2(p['l0_mix_norm']), dn, dh1, nch, "l0_mix_norm_bwd")
    G['l0_mix_norm'] = dg[0]
    dh0 = dh0.reshape(B, Pn, D)
    grad_x = dh0[:, CH:]
    G['meta_tokens'] = dh0[:, PAD:CH].sum(0)

    sums = chip_sums(['l0_w_in'], "d")
    parts['l0_w_in'], = _scatter_sums([sums['l0_w_in']])
    reds = [_sum_chips(parts[n], "sum_chips_" + n) for n in _BIG]
    grads = {}
    for n, own, other in zip(_BIG, reds, _join_halves(reds)):
        both = jnp.where(core[0] == 0, jnp.stack([own, other]), jnp.stack([other, own]))
        grads[n] = both.reshape(-1, both.shape[2])
    small_full = _unpack(_allreduce_small(_pack([G[n] for n in _SMALL])), [G[n].shape for n in _SMALL])
    for n, g in zip(_SMALL, small_full):
        if n in _SMALL_SHARDED:
            cs = g.shape[1] // 4
            g = lax.dynamic_slice_in_dim(g, chip * cs, cs, axis=1)
        grads[n] = g.reshape(p[n].shape)

    delta, new_m, new_v = {}, {}, {}
    for n in _BIG:
        delta[n], new_m[n], new_v[n] = _adamw(p[n], grads[n], p['m_' + n], p['v_' + n], "adamw_" + n)
    shapes = [p[n].shape for n in _SMALL]
    outs = _adamw(_pack([p[n] for n in _SMALL]), _pack([grads[n] for n in _SMALL]), _pack([p['m_' + n] for n in _SMALL]),
                  _pack([p['v_' + n] for n in _SMALL]), "adamw_small")
    for dst, buf in zip((delta, new_m, new_v), outs):
        for n, a in zip(_SMALL, _unpack(buf, shapes)):
            dst[n] = a
    return (loss, grad_x, *[grads[n] for n in _W_NAMES], *[delta[n] for n in _W_NAMES],
            *[new_m[n] for n in _W_NAMES], *[new_v[n] for n in _W_NAMES])
```

```python
import math

import numpy as np
import jax
import jax.numpy as jnp
from jax import lax
from jax.experimental import pallas as pl
from jax.experimental.pallas import tpu as pltpu

F32 = jnp.float32
BF16 = jnp.bfloat16
_MXU = jnp.bfloat16

D = 1024
CH = 128
N_META = 16
PAD = CH - N_META
EPS = 1e-6

SSD_HEADS = 16
SSD_HD = 64
SSD_GROUPS = 4
RET_HEADS = 4
RET_DK = 256
SB_HEADS = 16
SB_HD = 64
LRU_BLOCKS = 8
LRU_C = 8.0
FFN = 2816
U0_Z = 4096
U0_XBC = 5120

VMEM_LIMIT = 56 * 1024 * 1024


def _cparams(sem):
    return pltpu.CompilerParams(dimension_semantics=sem, vmem_limit_bytes=VMEM_LIMIT)


def _dot(a, b, dims=((1,), (0,))):
    return lax.dot_general(a.astype(_MXU), b.astype(_MXU), (dims, ((), ())), preferred_element_type=F32)


def _dot_nt(a, b):
    return _dot(a, b, ((1,), (1,)))


def _dot_tn(a, b):
    return _dot(a.T, b)


def _dot_exact(a, b):
    return lax.dot_general(a, b, (((1,), (0,)), ((), ())), preferred_element_type=F32,
                           precision=lax.Precision.HIGHEST)


def _dot_split(x, m01):
    hi = x.astype(BF16)
    lo = (x - hi.astype(F32)).astype(BF16)
    m = m01.astype(BF16)
    return jnp.dot(hi, m, preferred_element_type=F32) + jnp.dot(lo, m, preferred_element_type=F32)


def _sigmoid(x):
    return 0.5 * jnp.tanh(0.5 * x) + 0.5


def _softplus(x):
    return jnp.maximum(x, 0.0) + jnp.log1p(jnp.exp(-jnp.abs(x)))


def _silu(x):
    return x * _sigmoid(x)


def _dsilu(x):
    s = _sigmoid(x)
    return s * (1.0 + x * (1.0 - s))


_GELU_C = math.sqrt(2.0 / math.pi)


def _gelu(x):
    return 0.5 * x * (1.0 + jnp.tanh(_GELU_C * (x + 0.044715 * x * x * x)))


def _dgelu(x):
    t = jnp.tanh(_GELU_C * (x + 0.044715 * x * x * x))
    return 0.5 * (1.0 + t) + 0.5 * x * (1.0 - t * t) * _GELU_C * (1.0 + 3.0 * 0.044715 * x * x)


def _row_ids(n, cols=1):
    return lax.broadcasted_iota(jnp.int32, (n, cols), 0)


def _lane_ids(rows, n):
    return lax.broadcasted_iota(jnp.int32, (rows, n), 1)


def _real_rows(chunk):
    return chunk * CH + _row_ids(CH) >= PAD


def _shift_down(prev8, cur, s):
    cat = jnp.concatenate([prev8, cur], axis=0)
    return pltpu.roll(cat, s, axis=0)[8:]


def _shift_up(cur, next8, s):
    n = cur.shape[0]
    cat = jnp.concatenate([cur, next8], axis=0)
    return pltpu.roll(cat, n + 8 - s, axis=0)[:n]


def _conv_pre(prev8, cur, w_ref, b_ref, K):
    acc = cur * w_ref[K - 1:K, :] + b_ref[...]
    for s in range(1, K):
        acc = acc + _shift_down(prev8, cur, s) * w_ref[K - 1 - s:K - s, :]
    return acc


def _prev8_map(nch, col):
    return lambda b, c: (jnp.maximum((b * nch + c) * (CH // 8) - 1, 0), col)


def _matmul(a, b, mode, out_dtype, tm, tn, tk, name, add=None, b_off=0):
    if mode == "nn":
        (M, K), (_, N) = a.shape, b.shape
    elif mode == "nt":
        (M, K), N = a.shape, b.shape[0]
    else:
        (K, M), (_, N) = a.shape, b.shape
    tm, tn, tk = min(tm, M), min(tn, N), min(tk, K)
    assert M % tm == 0 and N % tn == 0 and K % tk == 0 and b_off % tk == 0, (name, M, N, K, tm, tn, tk)
    koff = b_off // tk
    nk = K // tk
    dims = {"nn": ((1,), (0,)), "nt": ((1,), (1,)), "tn": ((0,), (0,))}[mode]
    if mode == "tn":
        a_spec = pl.BlockSpec((tk, tm), lambda i, j, k: (k, i))
    else:
        a_spec = pl.BlockSpec((tm, tk), lambda i, j, k: (i, k))
    if mode == "nt":
        b_spec = pl.BlockSpec((tn, tk), lambda i, j, k: (j, k + koff))
    else:
        b_spec = pl.BlockSpec((tk, tn), lambda i, j, k: (k, j))
    o_spec = pl.BlockSpec((tm, tn), lambda i, j, k: (i, j))
    has_add = add is not None

    def body(a_ref, b_ref, *rest):
        if has_add:
            add_ref, o_ref, acc = rest
        else:
            o_ref, acc = rest
        k = pl.program_id(2)

        @pl.when(k == 0)
        def _():
            acc[...] = jnp.zeros_like(acc)

        acc[...] += _dot(a_ref[...], b_ref[...], dims)

        @pl.when(k == nk - 1)
        def _():
            r = acc[...]
            if has_add:
                r = r + add_ref[...].astype(F32)
            o_ref[...] = r.astype(out_dtype)

    in_specs = [a_spec, b_spec] + ([o_spec] if has_add else [])
    args = (a, b) + ((add,) if has_add else ())
    return pl.pallas_call(
        body, name=name, grid=(M // tm, N // tn, nk),
        in_specs=in_specs, out_specs=o_spec,
        out_shape=jax.ShapeDtypeStruct((M, N), out_dtype),
        scratch_shapes=[pltpu.VMEM((tm, tn), F32)],
        compiler_params=_cparams(("parallel", "parallel", "arbitrary")),
    )(*args)


def _tile(n, prefs):
    for t in prefs:
        if n % t == 0:
            return t
    return n


def _mm(a, b, mode, out_dtype, name, add=None, b_off=0):
    if mode == "tn":
        K, M = a.shape
        N = b.shape[1]
        tm, tn, tk = _tile(M, (1024, 1408, 512, 256, 128)), _tile(N, (1024, 1408, 512, 256, 128)), _tile(K, (2176, 384, 256, 128))
    else:
        M, K = a.shape
        N = b.shape[1] if mode == "nn" else b.shape[0]
        tm = _tile(M, (1088, 1024, 768, 512, 384, 256, 128))
        tn = _tile(N, (1024, 1408, 512, 256, 128))
        tk = _tile(K, (2176, 1024, 1408, 512, 256, 128))
    return _matmul(a, b, mode, out_dtype, tm, tn, tk, name, add=add, b_off=b_off)


def _rmsnorm_fwd(h, g, name):
    R = h.shape[0]
    tr = 2 * CH

    def body(h_ref, g_ref, o_ref, ot_ref):
        x = h_ref[...]
        r = lax.rsqrt(jnp.mean(x * x, axis=-1, keepdims=True) + EPS)
        y = x * r * g_ref[...]
        o_ref[...] = y.astype(o_ref.dtype)
        ot_ref[...] = y.T.astype(ot_ref.dtype)

    return pl.pallas_call(
        body, name=name, grid=(R // tr,),
        in_specs=[pl.BlockSpec((tr, D), lambda i: (i, 0)), pl.BlockSpec((1, D), lambda i: (0, 0))],
        out_specs=[pl.BlockSpec((tr, D), lambda i: (i, 0)), pl.BlockSpec((D, tr), lambda i: (0, i))],
        out_shape=[jax.ShapeDtypeStruct((R, D), _MXU), jax.ShapeDtypeStruct((D, R), _MXU)],
        compiler_params=_cparams(("parallel",)),
    )(h, g)


def _rmsnorm_bwd(h, g, dn, dres, nch, name):
    R = h.shape[0]
    per = 4
    tr = nch * CH // per

    def body(h_ref, g_ref, dn_ref, dres_ref, dh_ref, dg_ref):
        i = pl.program_id(0)
        x = h_ref[...]
        r = lax.rsqrt(jnp.mean(x * x, axis=-1, keepdims=True) + EPS)
        xhat = x * r
        dn_v = dn_ref[...]
        dx = dn_v * g_ref[...]
        dh = r * (dx - xhat * jnp.mean(dx * xhat, axis=-1, keepdims=True))
        keep = (i % per) * tr + _row_ids(tr) >= PAD
        dh_ref[...] = jnp.where(keep, dres_ref[...] + dh, 0.0)

        @pl.when(i == 0)
        def _():
            dg_ref[...] = jnp.zeros_like(dg_ref)

        dg_ref[...] += jnp.sum(dn_v * xhat, axis=0, keepdims=True)

    row = pl.BlockSpec((tr, D), lambda i: (i, 0))
    vec = pl.BlockSpec((1, D), lambda i: (0, 0))
    return pl.pallas_call(
        body, name=name, grid=(R // tr,),
        in_specs=[row, vec, row, row], out_specs=[row, vec],
        out_shape=[jax.ShapeDtypeStruct((R, D), F32), jax.ShapeDtypeStruct((1, D), F32)],
        compiler_params=_cparams(("arbitrary",)),
    )(h, g, dn, dres)


def _ssd_prep(u0, udt, conv_w, conv_b, dt_bias, B, nch, rider=None):
    R = u0.shape[0]

    def body(xs_ref, xsp_ref, bc_ref, bcp_ref, udt_ref, w0_ref, w1_ref, b0_ref, b1_ref, dtb_ref,
             act_ref, dt_ref, dtt_ref):
        keep = _real_rows(pl.program_id(1))
        a0 = _silu(_conv_pre(xsp_ref[...], xs_ref[...], w0_ref, b0_ref, 4))
        a1 = _silu(_conv_pre(bcp_ref[...], bc_ref[...], w1_ref, b1_ref, 4))
        act_ref[:, :1024] = jnp.where(keep, a0, 0.0)
        act_ref[:, 1024:] = jnp.where(keep, a1, 0.0)
        ok = jnp.logical_and(keep, _lane_ids(1, 128) < SSD_HEADS)
        dt = jnp.where(ok, _softplus(udt_ref[...] + dtb_ref[...]), 0.0)
        dt_ref[...] = dt
        dtt_ref[...] = dt.T

    row = lambda col: pl.BlockSpec((CH, 1024), lambda b, c: (b * nch + c, col))
    prev = lambda col: pl.BlockSpec((8, 1024), _prev8_map(nch, col))
    kw = dict(
        grid=(B, nch),
        in_specs=[row(5), prev(5), row(6), prev(6),
                  pl.BlockSpec((CH, 128), lambda b, c: (b * nch + c, 0)),
                  pl.BlockSpec((4, 1024), lambda b, c: (0, 0)), pl.BlockSpec((4, 1024), lambda b, c: (0, 1)),
                  pl.BlockSpec((1, 1024), lambda b, c: (0, 0)), pl.BlockSpec((1, 1024), lambda b, c: (0, 1)),
                  pl.BlockSpec((1, 128), lambda b, c: (0, 0))],
        out_specs=[pl.BlockSpec((CH, 2048), lambda b, c: (b * nch + c, 0)),
                   pl.BlockSpec((CH, 128), lambda b, c: (b * nch + c, 0)),
                   pl.BlockSpec((128, CH), lambda b, c: (0, b * nch + c))],
        out_shape=[jax.ShapeDtypeStruct((R, 2048), F32), jax.ShapeDtypeStruct((R, 128), F32),
                   jax.ShapeDtypeStruct((128, R), F32)])
    return _call(body, "ssd_prep", ("arbitrary", "arbitrary"), kw,
                 (u0, u0, u0, u0, udt, conv_w, conv_w, conv_b, conv_b, dt_bias), rider)


def _ssd_head_terms(h, a_vec, dt_v, dtt_v, dsk_v):
    lane = _lane_ids(1, 128)
    sub = _row_ids(128)
    r = _row_ids(CH, CH)
    cidx = _lane_ids(CH, CH)
    a_h = jnp.sum(jnp.where(lane == h, a_vec, 0.0), axis=1, keepdims=True)
    dt_col = jnp.sum(jnp.where(lane == h, dt_v, 0.0), axis=1, keepdims=True)
    dt_row = jnp.sum(jnp.where(sub == h, dtt_v, 0.0), axis=0, keepdims=True)
    cs_col = jnp.sum(jnp.where(r >= cidx, dt_row * a_h, 0.0), axis=1, keepdims=True)
    cs_row = jnp.sum(jnp.where(r <= cidx, dt_col * a_h, 0.0), axis=0, keepdims=True)
    tot = jnp.sum(dt_col * a_h, axis=0, keepdims=True)
    dsk = jnp.sum(jnp.where(lane == h, dsk_v, 0.0), axis=1, keepdims=True)
    return a_h, dt_col, cs_col, cs_row, tot, dsk


def _ssd_fwd(act, u0, dt, dtt, a_log, d_skip, norm_g, B, nch, rider=None):
    R = act.shape[0]

    def body(xs_ref, bm_ref, cm_ref, z_ref, dt_ref, dtt_ref, alog_ref, dsk_ref, ng_ref,
             out_ref, ypre_ref, hin_ref, H):
        g = pl.program_id(1)
        c = pl.program_id(2)

        @pl.when(c == 0)
        def _():
            H[...] = jnp.zeros_like(H)

        hin_ref[...] = H[...]
        a_vec = -jnp.exp(alog_ref[...])
        dt_v = dt_ref[...]
        dtt_v = dtt_ref[...]
        hm = _lane_ids(1, 128) < SSD_HD
        r = _row_ids(CH, CH)
        cidx = _lane_ids(CH, CH)
        Bm = bm_ref[...]
        Cm = cm_ref[...]
        CB = _dot_nt(Cm, Bm)
        ys = []
        for pair in range(2):
            cols = slice(128 * pair, 128 * pair + 128)
            xraw = xs_ref[:, cols]
            t = [_ssd_head_terms(4 * g + 2 * pair + j, a_vec, dt_v, dtt_v, dsk_ref[...]) for j in range(2)]
            sel = lambda f: jnp.where(hm, f(t[0]), f(t[1]))
            dtp = sel(lambda q: q[1])
            Ep = sel(lambda q: jnp.exp(q[2]))
            Wp = sel(lambda q: jnp.exp(q[4] - q[2]))
            etot = sel(lambda q: jnp.exp(q[4]))
            dsk = sel(lambda q: q[5])
            X = xraw * dtp
            ydiag = jnp.zeros((CH, 128), F32)
            for j in range(2):
                Lm = jnp.where(r >= cidx, jnp.exp(t[j][2] - t[j][3]), 0.0)
                Xh = jnp.where(hm if j == 0 else jnp.logical_not(hm), X, 0.0)
                ydiag = ydiag + _dot(CB * Lm, Xh)
            Hp = H[:, cols]
            yoff = Ep * _dot(Cm, Hp)
            S = _dot(Bm.T, X * Wp)
            H[:, cols] = etot * Hp + S
            ys.append(ydiag + yoff + xraw * dsk)
        y = jnp.concatenate(ys, axis=1)
        ypre_ref[...] = y
        yg = y * _silu(z_ref[...])
        rr = lax.rsqrt(jnp.mean(yg * yg, axis=-1, keepdims=True) + EPS)
        out_ref[...] = jnp.where(_real_rows(c), yg * rr * ng_ref[...], 0.0).astype(out_ref.dtype)

    rowb = lambda w, colf: pl.BlockSpec((CH, w), lambda b, g, c: (b * nch + c, colf(g)))
    vec = pl.BlockSpec((1, 128), lambda b, g, c: (0, 0))
    kw = dict(
        grid=(B, SSD_GROUPS, nch),
        in_specs=[rowb(256, lambda g: g), rowb(128, lambda g: 8 + g), rowb(128, lambda g: 12 + g),
                  rowb(256, lambda g: 16 + g), rowb(128, lambda g: 0),
                  pl.BlockSpec((128, CH), lambda b, g, c: (0, b * nch + c)),
                  vec, vec, pl.BlockSpec((1, 256), lambda b, g, c: (0, g))],
        out_specs=[rowb(256, lambda g: g), rowb(256, lambda g: g),
                   pl.BlockSpec((None, None, None, 128, 256), lambda b, g, c: (b, g, c, 0, 0))],
        out_shape=[jax.ShapeDtypeStruct((R, 2048), _MXU), jax.ShapeDtypeStruct((R, 1024), F32),
                   jax.ShapeDtypeStruct((B, SSD_GROUPS, nch, 128, 256), F32)],
        scratch_shapes=[pltpu.VMEM((128, 256), F32)])
    return _call(body, "ssd_fwd", ("arbitrary", "arbitrary", "arbitrary"), kw,
                 (act, act, act, u0, dt, dtt, a_log, d_skip, norm_g), rider)


def _ssd_bwd(dycat, ypre, u0, act, dt, dtt, hin, a_log, d_skip, norm_g, B, nch, rider=None):
    R = act.shape[0]

    def body(dy_ref, ypre_ref, z_ref, xs_ref, bm_ref, cm_ref, dt_ref, dtt_ref, hin_ref, alog_ref, dsk_ref, ng_ref,
             dz_ref, dxs_ref, db_ref, dc_ref, ddt_ref, pg_ref, dH):
        g = pl.program_id(1)
        c = nch - 1 - pl.program_id(2)

        @pl.when(pl.program_id(2) == 0)
        def _():
            dH[...] = jnp.zeros_like(dH)
            pg_ref[...] = jnp.zeros_like(pg_ref)

        z = z_ref[...]
        y = ypre_ref[...]
        ng = ng_ref[...]
        dout = jnp.where(_real_rows(c), dy_ref[...], 0.0)
        sz = _sigmoid(z)
        yg = y * z * sz
        rr = lax.rsqrt(jnp.mean(yg * yg, axis=-1, keepdims=True) + EPS)
        nrm = yg * rr
        pg_ref[0:1, :] += jnp.sum(dout * nrm, axis=0, keepdims=True)
        dn = dout * ng
        dyg = rr * (dn - nrm * jnp.mean(dn * nrm, axis=-1, keepdims=True))
        dy = dyg * z * sz
        dz_ref[...] = (dyg * y * (sz * (1.0 + z * (1.0 - sz)))).astype(dz_ref.dtype)

        a_vec = -jnp.exp(alog_ref[...])
        dt_v = dt_ref[...]
        dtt_v = dtt_ref[...]
        lane = _lane_ids(1, 128)
        hm = lane < SSD_HD
        r = _row_ids(CH, CH)
        cidx = _lane_ids(CH, CH)
        last = _row_ids(CH) == CH - 1
        Bm = bm_ref[...]
        Cm = cm_ref[...]
        CB = _dot_nt(Cm, Bm)
        CBT = _dot_nt(Bm, Cm)
        dB = jnp.zeros((CH, 128), F32)
        dC = jnp.zeros((CH, 128), F32)
        dcs_all = jnp.zeros((CH, 128), F32)
        dtx_all = jnp.zeros((CH, 128), F32)
        dd_row = jnp.zeros((1, 128), F32)
        dxs = []
        for pair in range(2):
            cols = slice(128 * pair, 128 * pair + 128)
            xraw = xs_ref[:, cols]
            dyp = dy[:, cols]
            heads = [4 * g + 2 * pair + j for j in range(2)]
            t = [_ssd_head_terms(heads[j], a_vec, dt_v, dtt_v, dsk_ref[...]) for j in range(2)]
            sel = lambda f: jnp.where(hm, f(t[0]), f(t[1]))
            hsum = lambda v, j: jnp.sum(jnp.where(hm if j == 0 else jnp.logical_not(hm), v, 0.0), axis=1, keepdims=True)
            dtp = sel(lambda q: q[1])
            Ep = sel(lambda q: jnp.exp(q[2]))
            Wp = sel(lambda q: jnp.exp(q[4] - q[2]))
            etot = sel(lambda q: jnp.exp(q[4]))
            dsk = sel(lambda q: q[5])
            X = xraw * dtp
            Hp = hin_ref[:, cols]
            dHn = dH[:, cols]
            dskip = jnp.sum(dyp * xraw, axis=0, keepdims=True)
            yoff = Ep * _dot(Cm, Hp)
            dE = dyp * yoff
            dC = dC + _dot_nt(dyp * Ep, Hp)
            dH[:, cols] = etot * dHn + _dot(Cm.T, dyp * Ep)
            BdS = _dot(Bm, dHn)
            dX = Wp * BdS
            ew = X * BdS * Wp
            dB = dB + _dot_nt(X * Wp, dHn)
            hh = jnp.sum(dHn * Hp, axis=0, keepdims=True) * etot
            for j in range(2):
                hmask = hm if j == 0 else jnp.logical_not(hm)
                cs_col, cs_row = t[j][2], t[j][3]
                Lm = jnp.where(r >= cidx, jnp.exp(cs_col - cs_row), 0.0)
                LmT = jnp.where(cidx >= r, jnp.exp(cs_row - cs_col), 0.0)
                dyh = jnp.where(hmask, dyp, 0.0)
                Xh = jnp.where(hmask, X, 0.0)
                dM = _dot_nt(dyh, Xh)
                dMT = _dot_nt(Xh, dyh)
                M = CB * Lm
                MT = CBT * LmT
                dX = dX + _dot(MT, dyh)
                dC = dC + _dot(dM * Lm, Bm)
                dB = dB + _dot(dMT * LmT, Cm)
                g_rows = jnp.sum(dM * M, axis=1, keepdims=True)
                g_cols = jnp.sum(dMT * MT, axis=1, keepdims=True)
                dtot = (jnp.sum(hsum(ew, j), axis=0, keepdims=True)
                        + jnp.sum(jnp.where(hmask, hh, 0.0), axis=1, keepdims=True))
                dcs = g_rows - g_cols + hsum(dE, j) - hsum(ew, j) + jnp.where(last, dtot, 0.0)
                dcs_all = dcs_all + jnp.where(lane == heads[j], dcs, 0.0)
                dtx_all = dtx_all + jnp.where(lane == heads[j], hsum(dX * xraw, j), 0.0)
                dd_row = dd_row + jnp.where(lane == heads[j],
                                            jnp.sum(jnp.where(hmask, dskip, 0.0), axis=1, keepdims=True), 0.0)
            dxs.append(dX * dtp + dyp * dsk)
        dxs_ref[...] = jnp.concatenate(dxs, axis=1)
        db_ref[...] = dB
        dc_ref[...] = dC
        dadt = _dot_exact(jnp.where(cidx >= r, 1.0, 0.0), dcs_all)
        ddt_ref[...] = dadt * a_vec + dtx_all
        pg_ref[1:2, 0:128] += dd_row
        pg_ref[2:3, 0:128] += jnp.sum(dadt * dt_v, axis=0, keepdims=True) * a_vec

    rowb = lambda w, colf: pl.BlockSpec((CH, w), lambda b, g, c: (b * nch + nch - 1 - c, colf(g)))
    vec = pl.BlockSpec((1, 128), lambda b, g, c: (0, 0))
    kw = dict(
        grid=(B, SSD_GROUPS, nch),
        in_specs=[rowb(256, lambda g: g), rowb(256, lambda g: g), rowb(256, lambda g: 16 + g), rowb(256, lambda g: g),
                  rowb(128, lambda g: 8 + g), rowb(128, lambda g: 12 + g), rowb(128, lambda g: 0),
                  pl.BlockSpec((128, CH), lambda b, g, c: (0, b * nch + nch - 1 - c)),
                  pl.BlockSpec((None, None, None, 128, 256), lambda b, g, c: (b, g, nch - 1 - c, 0, 0)),
                  vec, vec, pl.BlockSpec((1, 256), lambda b, g, c: (0, g))],
        out_specs=[rowb(256, lambda g: g), rowb(256, lambda g: g), rowb(128, lambda g: g), rowb(128, lambda g: g),
                   rowb(128, lambda g: g),
                   pl.BlockSpec((None, None, 8, 256), lambda b, g, c: (b, g, 0, 0))],
        out_shape=[jax.ShapeDtypeStruct((R, 1024), _MXU), jax.ShapeDtypeStruct((R, 1024), F32),
                   jax.ShapeDtypeStruct((R, 512), F32), jax.ShapeDtypeStruct((R, 512), F32),
                   jax.ShapeDtypeStruct((R, 512), F32), jax.ShapeDtypeStruct((B, SSD_GROUPS, 8, 256), F32)],
        scratch_shapes=[pltpu.VMEM((128, 256), F32)])
    return _call(body, "ssd_bwd", ("arbitrary", "arbitrary", "arbitrary"), kw,
                 (dycat, ypre, u0, act, act, act, dt, dtt, hin, a_log, d_skip, norm_g), rider)


def _ssd_prep_bwd(dxs, dB, dC, ddt4, u0, udt, conv_w, conv_b, dt_bias, B, nch, rider=None):
    R = u0.shape[0]

    def body(dxs_ref, db_ref, dc_ref, ddt_ref, xs_ref, xsp_ref, bc_ref, bcp_ref, udt_ref, w0_ref, w1_ref, b0_ref, b1_ref,
             dtb_ref, dpre_ref, ddtr_ref, pgd_ref):
        c = pl.program_id(1)

        @pl.when(c == 0)
        def _():
            pgd_ref[...] = jnp.zeros_like(pgd_ref)

        keep = _real_rows(c)
        p0 = _conv_pre(xsp_ref[...], xs_ref[...], w0_ref, b0_ref, 4)
        p1 = _conv_pre(bcp_ref[...], bc_ref[...], w1_ref, b1_ref, 4)
        dpre_ref[:, :1024] = jnp.where(keep, dxs_ref[...] * _dsilu(p0), 0.0)
        dpre_ref[:, 1024:] = jnp.where(keep, jnp.concatenate([db_ref[...], dc_ref[...]], axis=1) * _dsilu(p1), 0.0)
        ddt = ddt_ref[:, 0:128] + ddt_ref[:, 128:256] + ddt_ref[:, 256:384] + ddt_ref[:, 384:512]
        ok = jnp.logical_and(keep, _lane_ids(1, 128) < SSD_HEADS)
        dr = jnp.where(ok, ddt * _sigmoid(udt_ref[...] + dtb_ref[...]), 0.0)
        ddtr_ref[...] = dr
        pgd_ref[0:1, :] += jnp.sum(dr, axis=0, keepdims=True)

    rw = lambda w: pl.BlockSpec((CH, w), lambda b, c: (b * nch + c, 0))
    row = lambda col: pl.BlockSpec((CH, 1024), lambda b, c: (b * nch + c, col))
    prev = lambda col: pl.BlockSpec((8, 1024), _prev8_map(nch, col))
    kw = dict(
        grid=(B, nch),
        in_specs=[rw(1024), rw(512), rw(512), rw(512), row(5), prev(5), row(6), prev(6), rw(128),
                  pl.BlockSpec((4, 1024), lambda b, c: (0, 0)), pl.BlockSpec((4, 1024), lambda b, c: (0, 1)),
                  pl.BlockSpec((1, 1024), lambda b, c: (0, 0)), pl.BlockSpec((1, 1024), lambda b, c: (0, 1)),
                  pl.BlockSpec((1, 128), lambda b, c: (0, 0))],
        out_specs=[rw(2048), rw(128), pl.BlockSpec((None, 8, 128), lambda b, c: (b, 0, 0))],
        out_shape=[jax.ShapeDtypeStruct((R, 2048), F32), jax.ShapeDtypeStruct((R, 128), F32),
                   jax.ShapeDtypeStruct((B, 8, 128), F32)])
    return _call(body, "ssd_prep_bwd", ("arbitrary", "arbitrary"), kw,
                 (dxs, dB, dC, ddt4, u0, u0, u0, u0, udt, conv_w, conv_w, conv_b, conv_b, dt_bias), rider)


def _conv_bwd(dpre, xin, xin_col, w, K, name, tc=1024):
    R, C = dpre.shape
    assert C % tc == 0 and xin_col % tc == 0
    nr = R // CH
    xoff = xin_col // tc

    def body(dp_ref, dpn_ref, x_ref, xp_ref, w_ref, din_ref, dw_ref):
        i = pl.program_id(1)

        @pl.when(i == 0)
        def _():
            dw_ref[...] = jnp.zeros_like(dw_ref)

        dp = dp_ref[...]
        nxt = dpn_ref[...] * (i < nr - 1).astype(F32)
        x = x_ref[...]
        xp = xp_ref[...]
        din = dp * w_ref[K - 1:K, :]
        dw_ref[K - 1:K, :] += jnp.sum(dp * x, axis=0, keepdims=True)
        dw_ref[7:8, :] += jnp.sum(dp, axis=0, keepdims=True)
        for s in range(1, K):
            din = din + _shift_up(dp, nxt, s) * w_ref[K - 1 - s:K - s, :]
            dw_ref[K - 1 - s:K - s, :] += jnp.sum(dp * _shift_down(xp, x, s), axis=0, keepdims=True)
        din_ref[...] = din.astype(din_ref.dtype)

    return pl.pallas_call(
        body, name=name, grid=(C // tc, nr),
        in_specs=[pl.BlockSpec((CH, tc), lambda j, i: (i, j)),
                  pl.BlockSpec((8, tc), lambda j, i: (jnp.minimum((i + 1) * (CH // 8), nr * (CH // 8) - 1), j)),
                  pl.BlockSpec((CH, tc), lambda j, i: (i, xoff + j)),
                  pl.BlockSpec((8, tc), lambda j, i: (jnp.maximum(i * (CH // 8) - 1, 0), xoff + j)),
                  pl.BlockSpec((K, tc), lambda j, i: (0, j))],
        out_specs=[pl.BlockSpec((CH, tc), lambda j, i: (i, j)),
                   pl.BlockSpec((8, tc), lambda j, i: (0, j))],
        out_shape=[jax.ShapeDtypeStruct((R, C), _MXU), jax.ShapeDtypeStruct((8, C), F32)],
        compiler_params=_cparams(("parallel", "arbitrary")),
    )(dpre, dpre, xin, xin, w)


_RET_LG = [float(v) for v in np.log1p(-np.exp2(-5.0 - np.arange(RET_HEADS, dtype=np.float32))).astype(np.float32)]
_RET_SCALE = RET_DK ** -0.5


def _rope_tables(nch):
    half = RET_DK // 2
    inv_freq = 1.0 / (10000.0 ** (jnp.arange(half, dtype=F32) / (half - 1)))
    pos = jnp.arange(nch * CH, dtype=F32) - PAD
    ang = pos[:, None] * inv_freq[None, :]
    return jnp.cos(ang), jnp.sin(ang)


def _rot(x, cos, sin):
    x1, x2 = x[:, :128], x[:, 128:]
    return jnp.concatenate([x1 * cos - x2 * sin, x1 * sin + x2 * cos], axis=1)


def _unrot(d, cos, sin):
    d1, d2 = d[:, :128], d[:, 128:]
    return jnp.concatenate([d1 * cos + d2 * sin, d2 * cos - d1 * sin], axis=1)


def _ret_decays(lg):
    r = _row_ids(CH, CH)
    cidx = _lane_ids(CH, CH)
    diff = (r - cidx).astype(F32)
    decay = jnp.where(r >= cidx, jnp.exp(lg * jnp.maximum(diff, 0.0)), 0.0)
    decay_t = jnp.where(cidx >= r, jnp.exp(lg * jnp.maximum(-diff, 0.0)), 0.0)
    idx = _row_ids(CH).astype(F32)
    zeta = jnp.exp(lg * (CH - 1.0 - idx))
    xi = jnp.exp(lg * (idx + 1.0))
    return decay, decay_t, zeta, xi


def _ret_fwd(u0, ycat, cos, sin, norm_g, B, nch, rider=None):
    R = u0.shape[0]

    def body(u_ref, cos_ref, sin_ref, ng_ref, ycat_in, out_ref, opre_ref, rin_ref, Rst):
        c = pl.program_id(1)

        @pl.when(c == 0)
        def _():
            Rst[...] = jnp.zeros_like(Rst)

        cos_v, sin_v = cos_ref[...], sin_ref[...]
        for h in range(RET_HEADS):
            lg = _RET_LG[h]
            cols = slice(256 * h, 256 * h + 256)
            decay, _, zeta, xi = _ret_decays(lg)
            qr = _rot(u_ref[:, cols], cos_v, sin_v)
            kr = _rot(u_ref[:, 1024 + 256 * h:1024 + 256 * h + 256], cos_v, sin_v) * _RET_SCALE
            v = u_ref[:, 2048 + 256 * h:2048 + 256 * h + 256]
            gate = u_ref[:, 3072 + 256 * h:3072 + 256 * h + 256]
            Rh = Rst[h]
            rin_ref[h] = Rh
            inner = _dot(_dot_nt(qr, kr) * decay, v)
            cross = _dot(qr, Rh) * xi
            Rst[h] = math.exp(CH * lg) * Rh + _dot((kr * zeta).T, v)
            o = inner + cross
            opre_ref[:, cols] = o
            oc = o - jnp.mean(o, axis=-1, keepdims=True)
            rr = lax.rsqrt(jnp.mean(oc * oc, axis=-1, keepdims=True) + EPS)
            out_ref[:, cols] = (_silu(gate) * (oc * rr * ng_ref[:, cols])).astype(out_ref.dtype)

    kw = dict(
        grid=(B, nch),
        in_specs=[pl.BlockSpec((CH, 4096), lambda b, c: (b * nch + c, 0)),
                  pl.BlockSpec((CH, 128), lambda b, c: (c, 0)), pl.BlockSpec((CH, 128), lambda b, c: (c, 0)),
                  pl.BlockSpec((1, 1024), lambda b, c: (0, 0)),
                  pl.BlockSpec(memory_space=pl.ANY)],
        out_specs=[pl.BlockSpec((CH, 1024), lambda b, c: (b * nch + c, 1)),
                   pl.BlockSpec((CH, 1024), lambda b, c: (b * nch + c, 0)),
                   pl.BlockSpec((None, None, RET_HEADS, 256, 256), lambda b, c: (b, c, 0, 0, 0))],
        out_shape=[jax.ShapeDtypeStruct(ycat.shape, ycat.dtype), jax.ShapeDtypeStruct((R, 1024), F32),
                   jax.ShapeDtypeStruct((B, nch, RET_HEADS, 256, 256), F32)],
        scratch_shapes=[pltpu.VMEM((RET_HEADS, 256, 256), F32)],
        input_output_aliases={4: 0})
    return _call(body, "ret_fwd", ("arbitrary", "arbitrary"), kw, (u0, cos, sin, norm_g, ycat), rider)


def _ret_bwd(dycat, u0, opre, rin, cos, sin, norm_g, B, nch, rider=None):
    R = u0.shape[0]

    def body(dy_ref, u_ref, opre_ref, rin_ref, cos_ref, sin_ref, ng_ref, du_ref, pg_ref, dR):
        @pl.when(pl.program_id(1) == 0)
        def _():
            dR[...] = jnp.zeros_like(dR)
            pg_ref[...] = jnp.zeros_like(pg_ref)

        cos_v, sin_v = cos_ref[...], sin_ref[...]
        for h in range(RET_HEADS):
            lg = _RET_LG[h]
            cols = slice(256 * h, 256 * h + 256)
            decay, decay_t, zeta, xi = _ret_decays(lg)
            qr = _rot(u_ref[:, cols], cos_v, sin_v)
            kr = _rot(u_ref[:, 1024 + 256 * h:1024 + 256 * h + 256], cos_v, sin_v) * _RET_SCALE
            v = u_ref[:, 2048 + 256 * h:2048 + 256 * h + 256]
            gate = u_ref[:, 3072 + 256 * h:3072 + 256 * h + 256]
            ng = ng_ref[:, cols]
            o = opre_ref[:, cols]
            oc = o - jnp.mean(o, axis=-1, keepdims=True)
            rr = lax.rsqrt(jnp.mean(oc * oc, axis=-1, keepdims=True) + EPS)
            ohat = oc * rr
            dout = dy_ref[:, cols]
            du_ref[:, 3072 + 256 * h:3072 + 256 * h + 256] = (dout * (ohat * ng) * _dsilu(gate)).astype(du_ref.dtype)
            don = dout * _silu(gate)
            pg_ref[0:1, cols] += jnp.sum(don * ohat, axis=0, keepdims=True)
            dohat = don * ng
            do = rr * (dohat - jnp.mean(dohat, axis=-1, keepdims=True)
                       - ohat * jnp.mean(dohat * ohat, axis=-1, keepdims=True))
            Rh = rin_ref[h]
            dRn = dR[h]
            sc_t = _dot_nt(kr, qr) * decay_t
            dv = _dot(sc_t, do) + _dot(kr * zeta, dRn)
            ds = _dot_nt(do, v) * decay
            ds_t = _dot_nt(v, do) * decay_t
            dox = do * xi
            dq = _dot(ds, kr) + _dot_nt(dox, Rh)
            dk = _dot(ds_t, qr) + zeta * _dot_nt(v, dRn)
            dR[h] = math.exp(CH * lg) * dRn + _dot(qr.T, dox)
            du_ref[:, cols] = _unrot(dq, cos_v, sin_v).astype(du_ref.dtype)
            du_ref[:, 1024 + 256 * h:1024 + 256 * h + 256] = (_unrot(dk, cos_v, sin_v) * _RET_SCALE).astype(du_ref.dtype)
            du_ref[:, 2048 + 256 * h:2048 + 256 * h + 256] = dv.astype(du_ref.dtype)

    rmap = lambda b, c: (b * nch + nch - 1 - c, 0)
    kw = dict(
        grid=(B, nch),
        in_specs=[pl.BlockSpec((CH, 1024), lambda b, c: (b * nch + nch - 1 - c, 1)),
                  pl.BlockSpec((CH, 4096), rmap), pl.BlockSpec((CH, 1024), rmap),
                  pl.BlockSpec((None, None, RET_HEADS, 256, 256), lambda b, c: (b, nch - 1 - c, 0, 0, 0)),
                  pl.BlockSpec((CH, 128), lambda b, c: (nch - 1 - c, 0)),
                  pl.BlockSpec((CH, 128), lambda b, c: (nch - 1 - c, 0)),
                  pl.BlockSpec((1, 1024), lambda b, c: (0, 0))],
        out_specs=[pl.BlockSpec((CH, 4096), rmap), pl.BlockSpec((None, 8, 1024), lambda b, c: (b, 0, 0))],
        out_shape=[jax.ShapeDtypeStruct((R, 4096), _MXU), jax.ShapeDtypeStruct((B, 8, 1024), F32)],
        scratch_shapes=[pltpu.VMEM((RET_HEADS, 256, 256), F32)])
    return _call(body, "ret_bwd", ("arbitrary", "arbitrary"), kw, (dycat, u0, opre, rin, cos, sin, norm_g), rider)


_SB_SCALE = SB_HD ** -0.5


_SB_NB = 3


def _sb_valid(qb, kb, live):
    qpos = qb * CH + jnp.bitwise_and(_row_ids(2 * CH, CH), CH - 1)
    kpos = kb * CH + _lane_ids(2 * CH, CH)
    first = PAD + (1 - live) * (1 << 24)
    return jnp.logical_and(kpos < qpos, kpos >= first)


_SB_DEAD = -100.0


def _sb_alive(acc):
    return (jnp.max(acc) > _SB_DEAD).astype(jnp.int32)


def _sb_softplus(z):
    return jnp.maximum(z, 0.0) + jnp.log(1.0 + jnp.exp(-jnp.abs(z)))


def _stack_heads(x):
    hm = _lane_ids(1, 128) < SB_HD
    return jnp.concatenate([jnp.where(hm, x, 0.0), jnp.where(hm, 0.0, x)], axis=0)


def _unstack_heads(x2):
    return jnp.where(_lane_ids(1, 128) < SB_HD, x2[:CH], x2[CH:])


def _sb_fwd(u1, B, nch, rider=None):
    R = u1.shape[0]
    Pn = nch * CH

    def body(q_ref, k_ref, v_ref, out_ref):
        qb = pl.program_id(2)
        q2 = _stack_heads(q_ref[...] * _SB_SCALE).astype(_MXU)
        mgt = (_row_ids(CH, CH) > _lane_ids(CH, CH)).astype(F32)

        def step(i, carry):
            out2, acc = carry
            blocks = []
            for t in range(_SB_NB):
                kb = qb - _SB_NB * i - t
                live = (kb >= 0).astype(jnp.int32)
                kbc = jnp.maximum(kb, 0)
                start = pl.multiple_of(kbc * CH, CH)
                valid = _sb_valid(qb, kbc, live)
                z = _dot_nt(q2, k_ref[pl.ds(start, CH), :])
                sp = _sb_softplus(z)
                lm = jnp.where(valid, -sp, 0.0)
                blocks.append((valid, z - sp, _dot_split(lm, mgt), jnp.sum(lm, axis=1, keepdims=True), start))
            for valid, ls, loc, rs, start in blocks:
                w = jnp.where(valid, jnp.exp(ls + loc + acc), 0.0)
                out2 = out2 + _dot(w, v_ref[pl.ds(start, CH), :])
                acc = acc + rs
            return out2, acc

        trips = (qb + _SB_NB) // _SB_NB

        def more(c):
            return jnp.logical_and(c[0] < trips, c[1] > 0)

        def trip(c):
            out2, acc = step(c[0], c[2:])
            return c[0] + 1, _sb_alive(acc), out2, acc

        init = (jnp.int32(0), jnp.int32(1), jnp.zeros((2 * CH, 128), F32), jnp.zeros((2 * CH, 1), F32))
        out2 = lax.while_loop(more, trip, init)[2]
        out_ref[...] = _unstack_heads(out2).astype(out_ref.dtype)

    qspec = lambda off: pl.BlockSpec((CH, 128), lambda b, hp, qb: (b * nch + qb, off + hp))
    kspec = lambda off: pl.BlockSpec((Pn, 128), lambda b, hp, qb: (b, off + hp))
    kw = dict(grid=(B, SB_HEADS // 2, nch), in_specs=[qspec(0), kspec(8), kspec(16)], out_specs=[qspec(0)],
              out_shape=[jax.ShapeDtypeStruct((R, 2048), _MXU)])
    return _call(body, "sb_fwd", ("arbitrary", "arbitrary", "arbitrary"), kw, (u1, u1, u1), rider)


def _sb_bwd(dycat, u1, B, nch, rider=None):
    R = u1.shape[0]
    Pn = nch * CH

    def body(q_ref, k_ref, v_ref, do_ref, dq_ref, dk_ref, dv_ref):
        qb = pl.program_id(2)

        @pl.when(qb == 0)
        def _():
            dk_ref[...] = jnp.zeros_like(dk_ref)
            dv_ref[...] = jnp.zeros_like(dv_ref)

        q2 = _stack_heads(q_ref[...] * _SB_SCALE)
        do2 = _stack_heads(do_ref[...])
        q2t, do2t = q2.T.astype(_MXU), do2.T.astype(_MXU)
        q2, do2 = q2.astype(_MXU), do2.astype(_MXU)
        rr = _row_ids(CH, CH)
        cc = _lane_ids(CH, CH)
        mle = (rr <= cc).astype(F32)
        mlt = (rr < cc).astype(F32)
        trips = (qb + _SB_NB) // _SB_NB

        def more(c):
            return jnp.logical_and(c[0] < trips, c[1] > 0)

        def scan(c):
            acc = c[2]
            for t in range(_SB_NB):
                kb = qb - _SB_NB * c[0] - t
                kbc = jnp.maximum(kb, 0)
                z = _dot_nt(q2, k_ref[pl.ds(pl.multiple_of(kbc * CH, CH), CH), :])
                lm = jnp.where(_sb_valid(qb, kbc, (kb >= 0).astype(jnp.int32)), -_sb_softplus(z), 0.0)
                acc = acc + jnp.sum(lm, axis=1, keepdims=True)
            return c[0] + 1, _sb_alive(acc), acc

        used, _, s2 = lax.while_loop(more, scan, (jnp.int32(0), jnp.int32(1), jnp.zeros((2 * CH, 1), F32)))
        base = qb + 1 - _SB_NB * used

        def step(i, carry):
            dq2, pacc, gacc = carry
            blocks = []
            for t in range(_SB_NB):
                kb = base + _SB_NB * i + t
                live = (kb >= 0).astype(jnp.int32)
                start = pl.multiple_of(jnp.maximum(kb, 0) * CH, CH)
                valid = _sb_valid(qb, jnp.maximum(kb, 0), live)
                z = _dot_nt(q2, k_ref[pl.ds(start, CH), :])
                sp = _sb_softplus(z)
                lm = jnp.where(valid, -sp, 0.0)
                blocks.append((valid, z - sp, _dot_split(lm, mle), jnp.sum(lm, axis=1, keepdims=True), start))
            stage = []
            for valid, ls, ploc, rs, start in blocks:
                w = jnp.where(valid, jnp.exp(ls + (s2 - (ploc + pacc))), 0.0)
                gg = _dot_nt(do2, v_ref[pl.ds(start, CH), :]) * w
                stage.append((valid, ls, w, gg, _dot_split(gg, mlt), jnp.sum(gg, axis=1, keepdims=True), start))
                pacc = pacc + rs
            for valid, ls, w, gg, gloc, gs, start in stage:
                sig = jnp.exp(ls)
                dz = jnp.where(valid, gg * (1.0 - sig) - (gloc + gacc) * sig, 0.0)
                dq2 = dq2 + _dot(dz, k_ref[pl.ds(start, CH), :])
                dk_ref[:, pl.ds(start, CH)] += _dot(q2t, dz)
                dv_ref[:, pl.ds(start, CH)] += _dot(do2t, w)
                gacc = gacc + gs
            return dq2, pacc, gacc

        zero = jnp.zeros((2 * CH, 1), F32)
        dq2 = lax.fori_loop(0, used, step, (jnp.zeros((2 * CH, 128), F32), zero, zero))[0]
        dq_ref[...] = (_unstack_heads(dq2) * _SB_SCALE).astype(dq_ref.dtype)

    qspec = lambda off: pl.BlockSpec((CH, 128), lambda b, hp, qb: (b * nch + qb, off + hp))
    kspec = lambda off: pl.BlockSpec((Pn, 128), lambda b, hp, qb: (b, off + hp))
    tspec = pl.BlockSpec((128, Pn), lambda b, hp, qb: (hp, b))
    full = jax.ShapeDtypeStruct((1024, R), F32)
    kw = dict(grid=(B, SB_HEADS // 2, nch), in_specs=[qspec(0), kspec(8), kspec(16), qspec(0)],
              out_specs=[qspec(0), tspec, tspec], out_shape=[jax.ShapeDtypeStruct((R, 1024), _MXU), full, full])
    return _call(body, "sb_bwd", ("arbitrary", "arbitrary", "arbitrary"), kw, (u1, u1, u1, dycat), rider)


def _neg_expm1(x):
    series = -(x * (1.0 + x * (0.5 + x * (1.0 / 6.0 + x * (1.0 / 24.0)))))
    return jnp.where(x > -0.05, series, 1.0 - jnp.exp(x))


def _lru_gates(x, wa_ref, ba_ref, wx_ref, bx_ref, lam_ref):
    rs, is_ = [], []
    for n in range(LRU_BLOCKS):
        xb = x[:, 128 * n:128 * n + 128]
        rs.append(_dot(xb, wa_ref[n]))
        is_.append(_dot(xb, wx_ref[n]))
    r = _sigmoid(jnp.concatenate(rs, axis=1) + ba_ref[...])
    i = _sigmoid(jnp.concatenate(is_, axis=1) + bx_ref[...])
    sp = _softplus(-lam_ref[...])
    la = -LRU_C * r * sp
    a = jnp.exp(la)
    mult = jnp.sqrt(jnp.maximum(_neg_expm1(2.0 * la), 0.0))
    return r, i, sp, a, mult


def _lru_fwd(u1, ycat, conv_w, conv_b, wa, ba, wx, bx, lam, B, nch):
    R = u1.shape[0]

    def body(x_ref, xp_ref, gate_ref, cw_ref, cb_ref, wa_ref, ba_ref, wx_ref, bx_ref, lam_ref, ycat_in,
             out_ref, hs_ref, hc):
        c = pl.program_id(1)

        @pl.when(c == 0)
        def _():
            hc[...] = jnp.zeros_like(hc)

        x = _conv_pre(xp_ref[...], x_ref[...], cw_ref, cb_ref, 4)
        r, i, sp, a, mult = _lru_gates(x, wa_ref, ba_ref, wx_ref, bx_ref, lam_ref)
        b = jnp.where(_real_rows(c), mult * (i * x), 0.0)
        rows = _row_ids(CH)
        s = 1
        while s < CH:
            a_s = jnp.where(rows >= s, pltpu.roll(a, s, axis=0), 1.0)
            b_s = jnp.where(rows >= s, pltpu.roll(b, s, axis=0), 0.0)
            b = a * b_s + b
            a = a * a_s
            s *= 2
        h = a * hc[0:1, :] + b
        hs_ref[...] = h
        hc[0:1, :] = hs_ref[CH - 1:CH, :]
        out_ref[...] = (h * _gelu(gate_ref[...])).astype(out_ref.dtype)

    row = lambda col: pl.BlockSpec((CH, 1024), lambda b, c: (b * nch + c, col))
    vec = pl.BlockSpec((1, 1024), lambda b, c: (0, 0))
    wsp = pl.BlockSpec((LRU_BLOCKS, 128, 128), lambda b, c: (0, 0, 0))
    return pl.pallas_call(
        body, name="lru_fwd", grid=(B, nch),
        in_specs=[row(4), pl.BlockSpec((8, 1024), _prev8_map(nch, 4)), row(3),
                  pl.BlockSpec((4, 1024), lambda b, c: (0, 0)), vec, wsp, vec, wsp, vec, vec,
                  pl.BlockSpec(memory_space=pl.ANY)],
        out_specs=[row(1), row(0)],
        out_shape=[jax.ShapeDtypeStruct(ycat.shape, ycat.dtype), jax.ShapeDtypeStruct((R, 1024), F32)],
        scratch_shapes=[pltpu.VMEM((8, 1024), F32)],
        input_output_aliases={10: 0},
        compiler_params=_cparams(("parallel", "arbitrary")),
    )(u1, u1, u1, conv_w, conv_b, wa, ba, wx, bx, lam, ycat)


def _lru_bwd(dycat, u1, hs, conv_w, conv_b, wa, ba, wx, bx, lam, B, nch):
    R = u1.shape[0]

    def body(dy_ref, x_ref, xp_ref, gate_ref, hs_ref, hsp_ref, cw_ref, cb_ref, wa_ref, ba_ref, wx_ref, bx_ref, lam_ref,
             dgate_ref, dxc_ref, pg_ref, dwa_ref, dwx_ref, lc):
        c = nch - 1 - pl.program_id(1)

        @pl.when(pl.program_id(1) == 0)
        def _():
            lc[...] = jnp.zeros_like(lc)
            pg_ref[...] = jnp.zeros_like(pg_ref)
            dwa_ref[...] = jnp.zeros_like(dwa_ref)
            dwx_ref[...] = jnp.zeros_like(dwx_ref)

        x = _conv_pre(xp_ref[...], x_ref[...], cw_ref, cb_ref, 4)
        r, i, sp, a, mult = _lru_gates(x, wa_ref, ba_ref, wx_ref, bx_ref, lam_ref)
        h = hs_ref[...]
        hprev = _shift_down(hsp_ref[...], h, 1)
        gate = gate_ref[...]
        dy = dy_ref[...]
        dgate_ref[...] = (dy * h * _dgelu(gate)).astype(dgate_ref.dtype)
        rows = _row_ids(CH)
        lam_t = dy * _gelu(gate) + jnp.where(rows == CH - 1, lc[0:1, :], 0.0)
        coef = jnp.where(rows < CH - 1, pltpu.roll(a, CH - 1, axis=0), 0.0)
        s = 1
        while s < CH:
            c_s = jnp.where(rows < CH - s, pltpu.roll(coef, CH - s, axis=0), 1.0)
            l_s = jnp.where(rows < CH - s, pltpu.roll(lam_t, CH - s, axis=0), 0.0)
            lam_t = coef * l_s + lam_t
            coef = coef * c_s
            s *= 2
        lc[0:1, :] = jnp.sum(jnp.where(rows == 0, a * lam_t, 0.0), axis=0, keepdims=True)
        db = jnp.where(_real_rows(c), lam_t, 0.0)
        da = db * hprev
        dmult = db * (i * x)
        di = db * mult * x
        dx = db * mult * i
        pos = mult > 0.0
        dla = da * a + jnp.where(pos, -dmult * (a * a) / jnp.where(pos, mult, 1.0), 0.0)
        dr = dla * (-LRU_C * sp)
        pg_ref[2:3, :] += jnp.sum(dla * (LRU_C * r) * _sigmoid(-lam_ref[...]), axis=0, keepdims=True)
        dpr = dr * r * (1.0 - r)
        dpi = di * i * (1.0 - i)
        pg_ref[0:1, :] += jnp.sum(dpr, axis=0, keepdims=True)
        pg_ref[1:2, :] += jnp.sum(dpi, axis=0, keepdims=True)
        dxs = []
        for n in range(LRU_BLOCKS):
            blk = slice(128 * n, 128 * n + 128)
            dxs.append(dx[:, blk] + _dot_nt(dpr[:, blk], wa_ref[n]) + _dot_nt(dpi[:, blk], wx_ref[n]))
            dwa_ref[n] += _dot_tn(x[:, blk], dpr[:, blk])
            dwx_ref[n] += _dot_tn(x[:, blk], dpi[:, blk])
        dxc_ref[...] = jnp.concatenate(dxs, axis=1)

    rmap = lambda col: (lambda b, c: (b * nch + nch - 1 - c, col))
    row = lambda col: pl.BlockSpec((CH, 1024), rmap(col))
    prev = lambda col: pl.BlockSpec(
        (8, 1024), lambda b, c: (jnp.maximum((b * nch + nch - 1 - c) * (CH // 8) - 1, 0), col))
    vec = pl.BlockSpec((1, 1024), lambda b, c: (0, 0))
    wsp = pl.BlockSpec((LRU_BLOCKS, 128, 128), lambda b, c: (0, 0, 0))
    full = jax.ShapeDtypeStruct((R, 1024), F32)
    return pl.pallas_call(
        body, name="lru_bwd", grid=(B, nch),
        in_specs=[row(1), row(4), prev(4), row(3), row(0), prev(0),
                  pl.BlockSpec((4, 1024), lambda b, c: (0, 0)), vec, wsp, vec, wsp, vec, vec],
        out_specs=[row(0), row(0), pl.BlockSpec((None, 8, 1024), lambda b, c: (b, 0, 0)),
                   pl.BlockSpec((None, LRU_BLOCKS, 128, 128), lambda b, c: (b, 0, 0, 0)),
                   pl.BlockSpec((None, LRU_BLOCKS, 128, 128), lambda b, c: (b, 0, 0, 0))],
        out_shape=[jax.ShapeDtypeStruct((R, 1024), _MXU), full, jax.ShapeDtypeStruct((B, 8, 1024), F32),
                   jax.ShapeDtypeStruct((B, LRU_BLOCKS, 128, 128), F32),
                   jax.ShapeDtypeStruct((B, LRU_BLOCKS, 128, 128), F32)],
        scratch_shapes=[pltpu.VMEM((8, 1024), F32)],
        compiler_params=_cparams(("parallel", "arbitrary")),
    )(dycat, u1, u1, u1, hs, hs, conv_w, conv_b, wa, ba, wx, bx, lam)


_FFN_TC = FFN // 2


def _ffn_specs(nch):
    nt = FFN // _FFN_TC
    row = lambda off: pl.BlockSpec((CH, _FFN_TC), lambda b, c, j: (b * nch + c, off + j))
    prev = lambda off: pl.BlockSpec(
        (8, _FFN_TC), lambda b, c, j: (jnp.maximum((b * nch + c) * (CH // 8) - 1, 0), off + j))
    wsp = lambda off: pl.BlockSpec((3, _FFN_TC), lambda b, c, j: (0, off + j))
    bsp = lambda off: pl.BlockSpec((1, _FFN_TC), lambda b, c, j: (0, off + j))
    return nt, row, [row(0), prev(0), row(nt), prev(nt), wsp(0), wsp(nt), bsp(0), bsp(nt)]


def _ffn_act_fwd(uf, conv_w, conv_b, B, nch, rider=None):
    R = uf.shape[0]
    nt, row, specs = _ffn_specs(nch)

    def body(g_ref, gp_ref, u_ref, up_ref, wg_ref, wu_ref, bg_ref, bu_ref, o_ref):
        cg = _conv_pre(gp_ref[...], g_ref[...], wg_ref, bg_ref, 3)
        cu = _conv_pre(up_ref[...], u_ref[...], wu_ref, bu_ref, 3)
        o_ref[...] = jnp.where(_real_rows(pl.program_id(1)), _silu(cg) * cu, 0.0).astype(o_ref.dtype)

    kw = dict(grid=(B, nch, nt), in_specs=specs, out_specs=[row(0)],
              out_shape=[jax.ShapeDtypeStruct((R, FFN), _MXU)])
    return _call(body, "ffn_act_fwd", ("arbitrary", "arbitrary", "arbitrary"), kw,
                 (uf, uf, uf, uf, conv_w, conv_w, conv_b, conv_b), rider)


def _ffn_act_bwd(da, uf, conv_w, conv_b, nch, name, rider=None):
    R = uf.shape[0]
    nt = FFN // _FFN_TC
    nr = R // CH
    K = 3

    def body(da_ref, dan_ref, g_ref, gp_ref, gn_ref, u_ref, up_ref, un_ref, wg_ref, wu_ref, bg_ref, bu_ref,
             dug_ref, duu_ref, dwg_ref, dwu_ref):
        i = pl.program_id(1)

        @pl.when(i == 0)
        def _():
            dwg_ref[...] = jnp.zeros_like(dwg_ref)
            dwu_ref[...] = jnp.zeros_like(dwu_ref)

        c = i % nch
        ext = CH + 8
        rows = _row_ids(ext)
        follows = (c < nch - 1).astype(jnp.int32)
        keep = jnp.logical_and(c * CH + rows >= PAD, rows < CH + 8 * follows)
        dav = jnp.where(keep, jnp.concatenate([da_ref[...], dan_ref[...]], axis=0), 0.0)

        def conv_ext(x_ref, xp_ref, xn_ref, w_ref, b_ref):
            cat = jnp.concatenate([xp_ref[...], x_ref[...], xn_ref[...]], axis=0)
            shifted = [cat[8:]] + [pltpu.roll(cat, s, axis=0)[8:] for s in range(1, K)]
            acc = shifted[0] * w_ref[K - 1:K, :] + b_ref[...]
            for s in range(1, K):
                acc = acc + shifted[s] * w_ref[K - 1 - s:K - s, :]
            return acc, shifted

        cg, gsh = conv_ext(g_ref, gp_ref, gn_ref, wg_ref, bg_ref)
        cu, ush = conv_ext(u_ref, up_ref, un_ref, wu_ref, bu_ref)
        sg = _sigmoid(cg)
        dcg = dav * cu * (sg * (1.0 + cg * (1.0 - sg)))
        dcu = dav * (cg * sg)
        for dc, xsh, w_ref, din_ref, dw_ref in ((dcg, gsh, wg_ref, dug_ref, dwg_ref), (dcu, ush, wu_ref, duu_ref, dwu_ref)):
            dp = dc[:CH]
            din = dp * w_ref[K - 1:K, :]
            dw_ref[7:8, :] += jnp.sum(dp, axis=0, keepdims=True)
            dw_ref[K - 1:K, :] += jnp.sum(dp * xsh[0][:CH], axis=0, keepdims=True)
            for s in range(1, K):
                din = din + pltpu.roll(dc, ext - s, axis=0)[:CH] * w_ref[K - 1 - s:K - s, :]
                dw_ref[K - 1 - s:K - s, :] += jnp.sum(dp * xsh[s][:CH], axis=0, keepdims=True)
            din_ref[...] = din.astype(din_ref.dtype)

    row = lambda off: pl.BlockSpec((CH, _FFN_TC), lambda j, i: (i, off + j))
    prev = lambda off: pl.BlockSpec((8, _FFN_TC), lambda j, i: (jnp.maximum(i * (CH // 8) - 1, 0), off + j))
    nxt = lambda off: pl.BlockSpec(
        (8, _FFN_TC), lambda j, i: (jnp.minimum((i + 1) * (CH // 8), nr * (CH // 8) - 1), off + j))
    wsp = lambda off: pl.BlockSpec((K, _FFN_TC), lambda j, i: (0, off + j))
    bsp = lambda off: pl.BlockSpec((1, _FFN_TC), lambda j, i: (0, off + j))
    acc = pl.BlockSpec((8, _FFN_TC), lambda j, i: (0, j))
    half = jax.ShapeDtypeStruct((R, FFN), _MXU)
    dwsh = jax.ShapeDtypeStruct((8, FFN), F32)
    kw = dict(
        grid=(nt, nr),
        in_specs=[row(0), nxt(0), row(0), prev(0), nxt(0), row(nt), prev(nt), nxt(nt), wsp(0), wsp(nt), bsp(0), bsp(nt)],
        out_specs=[row(0), row(0), acc, acc],
        out_shape=[half, half, dwsh, dwsh])
    return _call(body, name, ("arbitrary", "arbitrary"), kw,
                 (da, da, uf, uf, uf, uf, uf, uf, conv_w, conv_w, conv_b, conv_b), rider)


def _head(h, g, target, B, nch):
    R = h.shape[0]

    def body(h_ref, g_ref, t_ref, dh_ref, loss_ref, dg_ref):
        c = pl.program_id(1)

        @pl.when(c == 0)
        def _():
            dh_ref[...] = jnp.zeros_like(dh_ref)
            loss_ref[...] = jnp.zeros_like(loss_ref)
            dg_ref[...] = jnp.zeros_like(dg_ref)

        @pl.when(c > 0)
        def _():
            x = h_ref[...]
            gv = g_ref[...]
            r = lax.rsqrt(jnp.mean(x * x, axis=-1, keepdims=True) + EPS)
            xhat = x * r
            e = xhat * gv - t_ref[...]
            loss_ref[...] += 0.5 * jnp.sum(jnp.mean(e * e, axis=-1, keepdims=True), axis=0, keepdims=True)
            dy = e * (1.0 / D)
            dg_ref[0:1, :] += jnp.sum(dy * xhat, axis=0, keepdims=True)
            dx = dy * gv
            dh_ref[...] = r * (dx - xhat * jnp.mean(dx * xhat, axis=-1, keepdims=True))

    row = pl.BlockSpec((CH, D), lambda b, c: (b * nch + c, 0))
    return pl.pallas_call(
        body, name="head", grid=(B, nch),
        in_specs=[row, pl.BlockSpec((1, D), lambda b, c: (0, 0)),
                  pl.BlockSpec((CH, D), lambda b, c: (b * (nch - 1) + jnp.maximum(c - 1, 0), 0))],
        out_specs=[row, pl.BlockSpec((None, 8, 128), lambda b, c: (b, 0, 0)),
                   pl.BlockSpec((None, 8, D), lambda b, c: (b, 0, 0))],
        out_shape=[jax.ShapeDtypeStruct((R, D), F32), jax.ShapeDtypeStruct((B, 8, 128), F32),
                   jax.ShapeDtypeStruct((B, 8, D), F32)],
        compiler_params=_cparams(("parallel", "arbitrary")),
    )(h, g, target)


ADAM_LR = 0.001
ADAM_B1 = 0.9
ADAM_B2 = 0.999
ADAM_EPS = 1e-08
ADAM_WD = 0.01
ADAM_STEP = 10


def _adamw(w, g, m, v, name):
    Rr, C = w.shape
    tr = _tile(Rr, (256, 64))

    def body(w_ref, g_ref, m_ref, v_ref, d_ref, nm_ref, nv_ref):
        gv = g_ref[...]
        nm = ADAM_B1 * m_ref[...] + (1.0 - ADAM_B1) * gv
        nv = ADAM_B2 * v_ref[...] + (1.0 - ADAM_B2) * (gv * gv)
        m_hat = nm / (1.0 - ADAM_B1 ** ADAM_STEP)
        v_hat = nv / (1.0 - ADAM_B2 ** ADAM_STEP)
        d_ref[...] = -ADAM_LR * (m_hat / (jnp.sqrt(v_hat) + ADAM_EPS) + ADAM_WD * w_ref[...])
        nm_ref[...] = nm
        nv_ref[...] = nv

    spec = pl.BlockSpec((tr, C), lambda i: (i, 0))
    sh = jax.ShapeDtypeStruct((Rr, C), F32)
    return pl.pallas_call(
        body, name=name, grid=(Rr // tr,),
        in_specs=[spec] * 4, out_specs=[spec] * 3, out_shape=[sh] * 3,
        compiler_params=_cparams(("parallel",)),
    )(w, g, m, v)


_MESH = pl.DeviceIdType.MESH
_ANY = pl.BlockSpec(memory_space=pl.ANY)


def _place():
    x, y, c = lax.axis_index("x"), lax.axis_index("y"), lax.axis_index("c")
    chips = [(1 - x, y), (x, 1 - y), (1 - x, 1 - y)]
    return x, y, c, chips


def _rcopy(src, dst, ssem, rsem, dev):
    return pltpu.make_async_remote_copy(src_ref=src, dst_ref=dst, send_sem=ssem, recv_sem=rsem,
                                        device_id=dev, device_id_type=_MESH)


def _with_riders(body, kw, kind, riders):
    n_in, n_out, n_scr = len(kw["in_specs"]), len(kw["out_specs"]), len(kw.get("scratch_shapes", []))
    grid = kw["grid"]
    nr = len(riders)
    nsem = 4 if kind == "gather" else 2

    def new_body(*refs):
        ins, srcs = refs[:n_in], refs[n_in:n_in + nr]
        outs, dsts = refs[n_in + nr:n_in + nr + n_out], refs[n_in + nr + n_out:n_in + 2 * nr + n_out]
        scr = refs[n_in + 2 * nr + n_out:n_in + 2 * nr + n_out + n_scr]
        sems = refs[n_in + 2 * nr + n_out + n_scr:]
        first = last = None
        for axis, size in enumerate(grid):
            i = pl.program_id(axis)
            first = (i == 0) if first is None else jnp.logical_and(first, i == 0)
            last = (i == size - 1) if last is None else jnp.logical_and(last, i == size - 1)
        x, y, c, chips = _place()
        k = 2 * x + y
        sib = (x, y, 1 - c)
        ssem, rsem = sems[:2]
        sends = []
        for a in range(nr):
            for j, (cx, cy) in enumerate(chips):
                if kind == "gather":
                    src, dst = srcs[a].at[c], dsts[a].at[k, c]
                else:
                    src, dst = srcs[a].at[2 * cx + cy], dsts[a].at[k]
                sends.append(_rcopy(src, dst, ssem.at[3 * a + j], rsem.at[3 * a + j], (cx, cy, c)))

        @pl.when(first)
        def _():
            for cp in sends:
                cp.start()

        body(*ins, *outs, *scr)

        @pl.when(last)
        def _():
            passed = []
            for a in range(nr):
                for j, (cx, cy) in enumerate(chips):
                    got = dsts[a].at[2 * cx + cy, c] if kind == "gather" else dsts[a].at[2 * cx + cy]
                    _rcopy(got, got, ssem.at[3 * a + j], rsem.at[3 * a + j], (cx, cy, c)).wait_recv()
                    if kind == "gather":
                        fw = _rcopy(got, got, sems[2].at[3 * a + j], sems[3].at[3 * a + j], sib)
                        fw.start()
                        passed.append(fw)
            if kind == "gather":
                for a in range(nr):
                    for j, (cx, cy) in enumerate(chips):
                        got = dsts[a].at[2 * cx + cy, 1 - c]
                        _rcopy(got, got, sems[2].at[3 * a + j], sems[3].at[3 * a + j], sib).wait_recv()
            for cp in sends + passed:
                cp.wait_send()

    kw = dict(kw)
    kw["in_specs"] = list(kw["in_specs"]) + [_ANY] * nr
    kw["out_specs"] = list(kw["out_specs"]) + [_ANY] * nr
    kw["out_shape"] = list(kw["out_shape"]) + [
        jax.ShapeDtypeStruct(((4,) + r.shape) if kind == "gather" else r.shape, r.dtype) for r in riders]
    kw["scratch_shapes"] = list(kw.get("scratch_shapes", [])) + [pltpu.SemaphoreType.DMA((3 * nr,))] * nsem
    return new_body, kw


def _call(body, name, sem, kw, args, rider=None):
    if rider is not None:
        body, kw = _with_riders(body, kw, *rider)
        args = tuple(args) + tuple(rider[1])
    return pl.pallas_call(body, name=name, compiler_params=_cparams(sem), **kw)(*args)


def _fill_own(result, own, chip):
    return lax.dynamic_update_index_in_dim(result, own, chip, 0)


def _gather_shards(bigs, small):
    nb = len(bigs)

    def body(*refs):
        ins, outs = refs[:nb + 1], refs[nb + 1:2 * nb + 2]
        ssem, rsem, fssem, frsem = refs[2 * nb + 2:]
        x, y, c, chips = _place()
        k = 2 * x + y
        sib = (x, y, 1 - c)

        def part(a, slot, hc):
            return outs[a].at[slot] if a == nb else outs[a].at[slot, hc]

        first = []
        for a in range(nb + 1):
            src = ins[a] if a == nb else ins[a].at[c]
            for j, (cx, cy) in enumerate(chips):
                first.append(_rcopy(src, part(a, k, c), ssem.at[3 * a + j], rsem.at[3 * a + j], (cx, cy, c)))
        for cp in first:
            cp.start()
        passed = []
        for a in range(nb + 1):
            for j, (cx, cy) in enumerate(chips):
                got = part(a, 2 * cx + cy, c)
                _rcopy(got, got, ssem.at[3 * a + j], rsem.at[3 * a + j], (cx, cy, c)).wait_recv()
                if a < nb:
                    fw = _rcopy(got, got, fssem.at[3 * a + j], frsem.at[3 * a + j], sib)
                    fw.start()
                    passed.append(fw)
        for a in range(nb):
            for j, (cx, cy) in enumerate(chips):
                got = part(a, 2 * cx + cy, 1 - c)
                _rcopy(got, got, fssem.at[3 * a + j], frsem.at[3 * a + j], sib).wait_recv()
        for cp in first + passed:
            cp.wait_send()

    arrs = list(bigs) + [small]
    n = 3 * (nb + 1)
    return pl.pallas_call(
        body, name="gather_shards",
        in_specs=[_ANY] * (nb + 1), out_specs=[_ANY] * (nb + 1),
        out_shape=[jax.ShapeDtypeStruct((4,) + a.shape, a.dtype) for a in arrs],
        scratch_shapes=[pltpu.SemaphoreType.DMA((n,)), pltpu.SemaphoreType.DMA((n,)),
                        pltpu.SemaphoreType.DMA((n,)), pltpu.SemaphoreType.DMA((n,))],
    )(*arrs)


def _swap_halves(grads, name):
    na = len(grads)
    halves = [g.shape[1] // 2 for g in grads]

    def body(*refs):
        ins, outs = refs[:na], refs[na:2 * na]
        ssem, rsem = refs[2 * na:]
        x, y, c, _ = _place()
        sib = (x, y, 1 - c)
        cps = [_rcopy(ins[a].at[:, pl.ds((1 - c) * halves[a], halves[a]), :], outs[a], ssem.at[a], rsem.at[a], sib)
               for a in range(na)]
        for cp in cps:
            cp.start()
        for cp in cps:
            cp.wait()

    return pl.pallas_call(
        body, name=name,
        in_specs=[_ANY] * na, out_specs=[_ANY] * na,
        out_shape=[jax.ShapeDtypeStruct((4, g.shape[1] // 2, g.shape[2]), g.dtype) for g in grads],
        scratch_shapes=[pltpu.SemaphoreType.DMA((na,)), pltpu.SemaphoreType.DMA((na,))],
    )(*grads)


def _sum_rows(rh):
    return rh if rh <= 512 else _tile(rh, (512, 256, 128, 64, 32))


def _chip_sum(grad, recv, core, name):
    _, r, cdim = grad.shape
    rh = r // 2
    tr = _sum_rows(rh)
    nblk = rh // tr

    def body(core_ref, g_ref, r_ref, o_ref):
        o_ref[...] = (g_ref[...] + r_ref[...]).astype(o_ref.dtype)

    return pl.pallas_call(
        body, name=name,
        grid_spec=pltpu.PrefetchScalarGridSpec(
            num_scalar_prefetch=1, grid=(4, nblk),
            in_specs=[pl.BlockSpec((None, tr, cdim), lambda s, i, cr: (s, cr[0] * nblk + i, 0)),
                      pl.BlockSpec((None, tr, cdim), lambda s, i, cr: (s, i, 0))],
            out_specs=pl.BlockSpec((None, tr, cdim), lambda s, i, cr: (s, i, 0))),
        out_shape=jax.ShapeDtypeStruct((4, rh, cdim), BF16),
        compiler_params=_cparams(("parallel", "parallel")),
    )(core, grad, recv)


def _scatter_sums(sums):
    na = len(sums)

    def body(*refs):
        ins, outs = refs[:na], refs[na:2 * na]
        ssem, rsem, lsem = refs[2 * na:]
        x, y, c, chips = _place()
        k = 2 * x + y
        local = [pltpu.make_async_copy(ins[a].at[k], outs[a].at[k], lsem.at[a]) for a in range(na)]
        for cp in local:
            cp.start()
        cps = []
        for a in range(na):
            for j, (cx, cy) in enumerate(chips):
                cps.append(_rcopy(ins[a].at[2 * cx + cy], outs[a].at[k], ssem.at[3 * a + j], rsem.at[3 * a + j],
                                  (cx, cy, c)))
        for cp in cps:
            cp.start()
        for a in range(na):
            for j, (cx, cy) in enumerate(chips):
                got = outs[a].at[2 * cx + cy]
                _rcopy(got, got, ssem.at[3 * a + j], rsem.at[3 * a + j], (cx, cy, c)).wait_recv()
        for cp in cps:
            cp.wait_send()
        for cp in local:
            cp.wait()

    return pl.pallas_call(
        body, name="scatter_sums",
        in_specs=[_ANY] * na, out_specs=[_ANY] * na,
        out_shape=[jax.ShapeDtypeStruct(s.shape, s.dtype) for s in sums],
        scratch_shapes=[pltpu.SemaphoreType.DMA((3 * na,)), pltpu.SemaphoreType.DMA((3 * na,)),
                        pltpu.SemaphoreType.DMA((na,))],
    )(*sums)


def _sum_chips(parts, name):
    _, rh, cdim = parts.shape
    tr = _sum_rows(rh)

    def body(p_ref, o_ref):
        acc = p_ref[0].astype(F32)
        for j in range(1, 4):
            acc = acc + p_ref[j].astype(F32)
        o_ref[...] = acc

    return pl.pallas_call(
        body, name=name, grid=(rh // tr,),
        in_specs=[pl.BlockSpec((4, tr, cdim), lambda i: (0, i, 0))],
        out_specs=pl.BlockSpec((tr, cdim), lambda i: (i, 0)),
        out_shape=jax.ShapeDtypeStruct((rh, cdim), F32),
        compiler_params=_cparams(("parallel",)),
    )(parts)


def _join_halves(reds):
    na = len(reds)

    def body(*refs):
        ins, outs = refs[:na], refs[na:2 * na]
        ssem, rsem = refs[2 * na:]
        x, y, c, _ = _place()
        cps = [_rcopy(ins[a], outs[a], ssem.at[a], rsem.at[a], (x, y, 1 - c)) for a in range(na)]
        for cp in cps:
            cp.start()
        for cp in cps:
            cp.wait()

    return pl.pallas_call(
        body, name="join_halves",
        in_specs=[_ANY] * na, out_specs=[_ANY] * na,
        out_shape=[jax.ShapeDtypeStruct(r.shape, r.dtype) for r in reds],
        scratch_shapes=[pltpu.SemaphoreType.DMA((na,)), pltpu.SemaphoreType.DMA((na,))],
    )(*reds)


def _allreduce_small(buf):
    n = buf.shape[0]

    def body(in_ref, out_ref, recv, ssem, rsem):
        x, y, c, _ = _place()
        peers = [(x, y, 1 - c), (1 - x, y, c), (x, 1 - y, c)]
        out_ref[...] = in_ref[...]
        for r, peer in enumerate(peers):
            cp = _rcopy(out_ref, recv.at[r], ssem.at[r], rsem.at[r], peer)
            cp.start()
            cp.wait()
            out_ref[...] = out_ref[...] + recv[r]

    vm = pl.BlockSpec(memory_space=pltpu.VMEM)
    return pl.pallas_call(
        body, name="allreduce_small",
        in_specs=[vm], out_specs=vm,
        out_shape=jax.ShapeDtypeStruct(buf.shape, F32),
        scratch_shapes=[pltpu.VMEM((3, n, 128), F32), pltpu.SemaphoreType.DMA((3,)), pltpu.SemaphoreType.DMA((3,))],
        compiler_params=pltpu.CompilerParams(vmem_limit_bytes=VMEM_LIMIT),
    )(buf)


_W_NAMES = ['meta_tokens', 'l0_mix_norm', 'l0_w_in', 'l0_ssd_conv_w', 'l0_ssd_conv_b', 'l0_ssd_dt_bias', 'l0_ssd_a_log',
            'l0_ssd_d', 'l0_ssd_norm', 'l0_ret_norm', 'l0_w_out', 'l0_ffn_norm', 'l0_ffn_w_in', 'l0_ffn_conv_w',
            'l0_ffn_conv_b', 'l0_ffn_w_out', 'l1_mix_norm', 'l1_w_in', 'l1_lru_conv_w', 'l1_lru_conv_b', 'l1_lru_wa',
            'l1_lru_ba', 'l1_lru_wx', 'l1_lru_bx', 'l1_lru_lambda', 'l1_w_out', 'l1_ffn_norm', 'l1_ffn_w_in',
            'l1_ffn_conv_w', 'l1_ffn_conv_b', 'l1_ffn_w_out', 'final_norm']
_IN_NAMES = ['x'] + _W_NAMES + ['loss_target'] + ['m_' + n for n in _W_NAMES] + ['v_' + n for n in _W_NAMES]
_BIG = ['l0_w_in', 'l0_w_out', 'l0_ffn_w_in', 'l0_ffn_w_out', 'l1_w_in', 'l1_w_out', 'l1_ffn_w_in', 'l1_ffn_w_out']
_BIG_COLS = ('l0_w_in', 'l0_ffn_w_in', 'l1_w_in', 'l1_ffn_w_in')
_SMALL_SHARDED = ['meta_tokens', 'l0_ssd_conv_w', 'l0_ffn_conv_w', 'l1_lru_conv_w', 'l1_ffn_conv_w']
_SMALL = [n for n in _W_NAMES if n not in _BIG]


def _pack(arrs):
    flat = []
    for a in arrs:
        v = a.reshape(-1).astype(F32)
        flat.append(jnp.pad(v, (0, (-v.shape[0]) % 128)))
    v = jnp.concatenate(flat)
    v = jnp.pad(v, (0, (-v.shape[0]) % 1024))
    return v.reshape(-1, 128)


def _unpack(buf, shapes):
    out, row = [], 0
    for sh in shapes:
        n = int(np.prod(sh))
        rows = -(-n // 128)
        out.append(buf[row:row + rows].reshape(-1)[:n].reshape(sh))
        row += rows
    return out


def kernel(x, meta_tokens, l0_mix_norm, l0_w_in, l0_ssd_conv_w, l0_ssd_conv_b, l0_ssd_dt_bias, l0_ssd_a_log, l0_ssd_d, l0_ssd_norm, l0_ret_norm, l0_w_out, l0_ffn_norm, l0_ffn_w_in, l0_ffn_conv_w, l0_ffn_conv_b, l0_ffn_w_out, l1_mix_norm, l1_w_in, l1_lru_conv_w, l1_lru_conv_b, l1_lru_wa, l1_lru_ba, l1_lru_wx, l1_lru_bx, l1_lru_lambda, l1_w_out, l1_ffn_norm, l1_ffn_w_in, l1_ffn_conv_w, l1_ffn_conv_b, l1_ffn_w_out, final_norm, loss_target, m_meta_tokens, m_l0_mix_norm, m_l0_w_in, m_l0_ssd_conv_w, m_l0_ssd_conv_b, m_l0_ssd_dt_bias, m_l0_ssd_a_log, m_l0_ssd_d, m_l0_ssd_norm, m_l0_ret_norm, m_l0_w_out, m_l0_ffn_norm, m_l0_ffn_w_in, m_l0_ffn_conv_w, m_l0_ffn_conv_b, m_l0_ffn_w_out, m_l1_mix_norm, m_l1_w_in, m_l1_lru_conv_w, m_l1_lru_conv_b, m_l1_lru_wa, m_l1_lru_ba, m_l1_lru_wx, m_l1_lru_bx, m_l1_lru_lambda, m_l1_w_out, m_l1_ffn_norm, m_l1_ffn_w_in, m_l1_ffn_conv_w, m_l1_ffn_conv_b, m_l1_ffn_w_out, m_final_norm, v_meta_tokens, v_l0_mix_norm, v_l0_w_in, v_l0_ssd_conv_w, v_l0_ssd_conv_b, v_l0_ssd_dt_bias, v_l0_ssd_a_log, v_l0_ssd_d, v_l0_ssd_norm, v_l0_ret_norm, v_l0_w_out, v_l0_ffn_norm, v_l0_ffn_w_in, v_l0_ffn_conv_w, v_l0_ffn_conv_b, v_l0_ffn_w_out, v_l1_mix_norm, v_l1_w_in, v_l1_lru_conv_w, v_l1_lru_conv_b, v_l1_lru_wa, v_l1_lru_ba, v_l1_lru_wx, v_l1_lru_bx, v_l1_lru_lambda, v_l1_w_out, v_l1_ffn_norm, v_l1_ffn_w_in, v_l1_ffn_conv_w, v_l1_ffn_conv_b, v_l1_ffn_w_out, v_final_norm):
    args = (x, meta_tokens, l0_mix_norm, l0_w_in, l0_ssd_conv_w, l0_ssd_conv_b, l0_ssd_dt_bias, l0_ssd_a_log, l0_ssd_d, l0_ssd_norm, l0_ret_norm, l0_w_out, l0_ffn_norm, l0_ffn_w_in, l0_ffn_conv_w, l0_ffn_conv_b, l0_ffn_w_out, l1_mix_norm, l1_w_in, l1_lru_conv_w, l1_lru_conv_b, l1_lru_wa, l1_lru_ba, l1_lru_wx, l1_lru_bx, l1_lru_lambda, l1_w_out, l1_ffn_norm, l1_ffn_w_in, l1_ffn_conv_w, l1_ffn_conv_b, l1_ffn_w_out, final_norm, loss_target, m_meta_tokens, m_l0_mix_norm, m_l0_w_in, m_l0_ssd_conv_w, m_l0_ssd_conv_b, m_l0_ssd_dt_bias, m_l0_ssd_a_log, m_l0_ssd_d, m_l0_ssd_norm, m_l0_ret_norm, m_l0_w_out, m_l0_ffn_norm, m_l0_ffn_w_in, m_l0_ffn_conv_w, m_l0_ffn_conv_b, m_l0_ffn_w_out, m_l1_mix_norm, m_l1_w_in, m_l1_lru_conv_w, m_l1_lru_conv_b, m_l1_lru_wa, m_l1_lru_ba, m_l1_lru_wx, m_l1_lru_bx, m_l1_lru_lambda, m_l1_w_out, m_l1_ffn_norm, m_l1_ffn_w_in, m_l1_ffn_conv_w, m_l1_ffn_conv_b, m_l1_ffn_w_out, m_final_norm, v_meta_tokens, v_l0_mix_norm, v_l0_w_in, v_l0_ssd_conv_w, v_l0_ssd_conv_b, v_l0_ssd_dt_bias, v_l0_ssd_a_log, v_l0_ssd_d, v_l0_ssd_norm, v_l0_ret_norm, v_l0_w_out, v_l0_ffn_norm, v_l0_ffn_w_in, v_l0_ffn_conv_w, v_l0_ffn_conv_b, v_l0_ffn_w_out, v_l1_mix_norm, v_l1_w_in, v_l1_lru_conv_w, v_l1_lru_conv_b, v_l1_lru_wa, v_l1_lru_ba, v_l1_lru_wx, v_l1_lru_bx, v_l1_lru_lambda, v_l1_w_out, v_l1_ffn_norm, v_l1_ffn_w_in, v_l1_ffn_conv_w, v_l1_ffn_conv_b, v_l1_ffn_w_out, v_final_norm)
    p = dict(zip(_IN_NAMES, args))
    B, seq, _ = x.shape
    nch = (seq + CH) // CH
    Pn = nch * CH
    R = B * Pn
    chip = 2 * lax.axis_index("x") + lax.axis_index("y")
    row2 = lambda v: v.reshape(1, -1)
    pad128 = lambda v: jnp.pad(v, (0, 128 - v.shape[0])).reshape(1, 128)

    small_shapes = [p[n].shape for n in _SMALL_SHARDED]
    halved = lambda w: w.astype(_MXU).reshape(2, w.shape[0] // 2, w.shape[1])
    mine = {n: halved(p[n]) for n in _BIG}
    mine_small = _pack([p[n] for n in _SMALL_SHARDED])
    W = {}

    def set_weight(n, g):
        g = _fill_own(g, mine[n], chip)
        g = g.reshape(4, -1, g.shape[3])
        W[n] = jnp.concatenate([g[k] for k in range(4)], axis=1) if n in _BIG_COLS else g.reshape(-1, g.shape[2])

    def gather_on(*names):
        return ("gather", [mine[n] for n in names])

    def take_weights(names, got):
        for n, g in zip(names, got):
            set_weight(n, g)

    gathered = _gather_shards([mine['l0_w_in']], mine_small)
    set_weight('l0_w_in', gathered[0])
    g_small = _fill_own(gathered[-1], mine_small, chip)
    per_chip = [_unpack(g_small[k], small_shapes) for k in range(4)]
    for i, n in enumerate(_SMALL_SHARDED):
        W[n] = jnp.concatenate([per_chip[k][i] for k in range(4)], axis=1)
    w0 = W['l0_w_in']
    w0_main = jnp.concatenate([w0[:, 3088:], w0[:, :3072]], axis=1)
    w0_dt = jnp.pad(w0[:, 3072:3088], ((0, 0), (0, 112)))
    cos, sin = _rope_tables(nch)

    meta = jnp.broadcast_to(W['meta_tokens'][None], (B, N_META, D))
    h0 = jnp.concatenate([jnp.zeros((B, PAD, D), F32), meta, x], axis=1).reshape(R, D)
    n0, n0t = _rmsnorm_fwd(h0, row2(p['l0_mix_norm']), "norm_l0_mix")
    u0 = _mm(n0, w0_main, "nn", F32, "l0_in_proj")
    udt = _mm(n0, w0_dt, "nn", F32, "l0_dt_proj")
    a_log, d_skip, dt_bias = pad128(p['l0_ssd_a_log']), pad128(p['l0_ssd_d']), pad128(p['l0_ssd_dt_bias'])
    ssd_cb = row2(p['l0_ssd_conv_b'])
    act, dt, dtt, *got = _ssd_prep(u0, udt, W['l0_ssd_conv_w'], ssd_cb, dt_bias, B, nch, rider=gather_on('l0_w_out'))
    take_weights(['l0_w_out'], got)
    ycat0, ypre, hin, *got = _ssd_fwd(act, u0, dt, dtt, a_log, d_skip, row2(p['l0_ssd_norm']), B, nch,
                                      rider=gather_on('l0_ffn_w_in'))
    take_weights(['l0_ffn_w_in'], got)
    ycat0, opre, rin, *got = _ret_fwd(u0, ycat0, cos, sin, row2(p['l0_ret_norm']), B, nch,
                                      rider=gather_on('l0_ffn_w_out'))
    take_weights(['l0_ffn_w_out'], got)
    h1 = _mm(ycat0, W['l0_w_out'], "nn", F32, "l0_out_proj", add=h0)
    n1, n1t = _rmsnorm_fwd(h1, row2(p['l0_ffn_norm']), "norm_l0_ffn")
    uf0 = _mm(n1, W['l0_ffn_w_in'], "nn", F32, "l0_ffn_in")
    f0_cb = row2(p['l0_ffn_conv_b'])
    a0, *got = _ffn_act_fwd(uf0, W['l0_ffn_conv_w'], f0_cb, B, nch, rider=gather_on('l1_w_in'))
    take_weights(['l1_w_in'], got)
    h2 = _mm(a0, W['l0_ffn_w_out'], "nn", F32, "l0_ffn_out", add=h1)
    n2, n2t = _rmsnorm_fwd(h2, row2(p['l1_mix_norm']), "norm_l1_mix")
    u1 = _mm(n2, W['l1_w_in'], "nn", F32, "l1_in_proj")
    lru = (W['l1_lru_conv_w'], row2(p['l1_lru_conv_b']), p['l1_lru_wa'], row2(p['l1_lru_ba']), p['l1_lru_wx'],
           row2(p['l1_lru_bx']), row2(p['l1_lru_lambda']))
    later = ['l1_w_out', 'l1_ffn_w_in', 'l1_ffn_w_out']
    ycat1, *got = _sb_fwd(u1, B, nch, rider=gather_on(*later))
    take_weights(later, got)
    ycat1, hs = _lru_fwd(u1, ycat1, *lru, B, nch)
    h3 = _mm(ycat1, W['l1_w_out'], "nn", F32, "l1_out_proj", add=h2)
    n3, n3t = _rmsnorm_fwd(h3, row2(p['l1_ffn_norm']), "norm_l1_ffn")
    uf1 = _mm(n3, W['l1_ffn_w_in'], "nn", F32, "l1_ffn_in")
    f1_cb = row2(p['l1_ffn_conv_b'])
    a1, = _ffn_act_fwd(uf1, W['l1_ffn_conv_w'], f1_cb, B, nch)
    h4 = _mm(a1, W['l1_ffn_w_out'], "nn", F32, "l1_ffn_out", add=h3)
    dh4, lossp, dgf = _head(h4, row2(p['final_norm']), p['loss_target'].reshape(B * seq, D), B, nch)
    loss = lax.psum(jnp.sum(lossp[:, 0, 0]), ("x", "y", "c"))

    G = {'final_norm': dgf[:, 0].sum(0)}

    core = lax.axis_index("c").reshape(1).astype(jnp.int32)

    def col_shards(pieces):
        edges = np.cumsum([0] + [q.shape[1] for q in pieces])
        cs = int(edges[-1]) // 4
        shards = []
        for k in range(4):
            lo, hi = k * cs, (k + 1) * cs
            cut = [q[:, max(lo - e0, 0):min(hi - e0, q.shape[1])]
                   for q, e0, e1 in zip(pieces, edges[:-1], edges[1:]) if e0 < hi and e1 > lo]
            shards.append(cut[0] if len(cut) == 1 else jnp.concatenate(cut, axis=1))
        return jnp.stack(shards)

    def chip_sums(names, tag):
        stacked = [G[n] if n in _BIG_COLS else G[n].reshape(4, G[n].shape[0] // 4, G[n].shape[1]) for n in names]
        theirs = _swap_halves(stacked, "swap_halves_" + tag)
        return {n: _chip_sum(g, t, core, "chip_sum_" + n) for n, g, t in zip(names, stacked, theirs)}

    parts = {}

    def scatter_on(names, tag):
        sums = chip_sums(names, tag)
        return sums, ("scatter", [sums[n] for n in names])

    def take_parts(names, sums, got):
        for n, g in zip(names, got):
            parts[n] = _fill_own(g, lax.dynamic_index_in_dim(sums[n], chip, 0, keepdims=False), chip)

    def ffn_bwd(layer, dh_out, h_in, nt_in, uf, a_act, cb, rider=None):
        pre = f"l{layer}_"
        w_in, w_out, cw = W[pre + 'ffn_w_in'], W[pre + 'ffn_w_out'], W[pre + 'ffn_conv_w']
        da = _mm(dh_out, w_out, "nt", F32, pre + "ffn_out_dgrad")
        G[pre + 'ffn_w_out'] = _mm(a_act, dh_out, "tn", F32, pre + "ffn_out_wgrad")
        dug, duu, dwg, dwu, *rode = _ffn_act_bwd(da, uf, cw, cb, nch, pre + "ffn_act_bwd", rider=rider)
        G[pre + 'ffn_conv_w'] = jnp.concatenate([dwg[:3], dwu[:3]], axis=1)
        G[pre + 'ffn_conv_b'] = jnp.concatenate([dwg[7], dwu[7]])
        dn = _mm(dug, w_in, "nt", F32, pre + "ffn_in_dgrad_g")
        dn = _mm(duu, w_in, "nt", F32, pre + "ffn_in_dgrad_u", add=dn, b_off=FFN)
        G[pre + 'ffn_w_in'] = col_shards([_mm(nt_in, dug, "nn", F32, pre + "ffn_in_wgrad_g"),
                                          _mm(nt_in, duu, "nn", F32, pre + "ffn_in_wgrad_u")])
        dh_in, dg = _rmsnorm_bwd(h_in, row2(p[pre + 'ffn_norm']), dn, dh_out, nch, pre + "ffn_norm_bwd")
        G[pre + 'ffn_norm'] = dg[0]
        return dh_in, rode

    dh3, _ = ffn_bwd(1, dh4, h3, n3t, uf1, a1, f1_cb)
    dy1 = _mm(dh3, W['l1_w_out'], "nt", F32, "l1_out_dgrad")
    G['l1_w_out'] = _mm(ycat1, dh3, "tn", F32, "l1_out_wgrad")
    done = ['l1_ffn_w_in', 'l1_ffn_w_out', 'l1_w_out']
    sums, rider = scatter_on(done, "a")
    dq, dkt, dvt, *got = _sb_bwd(dy1, u1, B, nch, rider=rider)
    dk, dv = dkt.T, dvt.T
    take_parts(done, sums, got)
    dgate, dxc, pgl, dwa, dwx = _lru_bwd(dy1, u1, hs, *lru, B, nch)
    dxr, dcw = _conv_bwd(dxc, u1, 4096, W['l1_lru_conv_w'], 4, "l1_lru_conv_bwd")
    pgl = pgl.sum(0)
    G['l1_lru_ba'], G['l1_lru_bx'], G['l1_lru_lambda'] = pgl[0], pgl[1], pgl[2]
    G['l1_lru_wa'], G['l1_lru_wx'] = dwa.sum(0), dwx.sum(0)
    G['l1_lru_conv_w'], G['l1_lru_conv_b'] = dcw[:4], dcw[7]
    dn, dws = None, []
    for i, piece in enumerate((dq, dk, dv, dgate, dxr)):
        dn = _mm(piece, W['l1_w_in'], "nt", F32, f"l1_in_dgrad_{i}", add=dn, b_off=1024 * i)
        dws.append(_mm(n2t, piece, "nn", F32, f"l1_in_wgrad_{i}"))
    G['l1_w_in'] = col_shards(dws)
    dh2, dg = _rmsnorm_bwd(h2, row2(p['l1_mix_norm']), dn, dh3, nch, "l1_mix_norm_bwd")
    G['l1_mix_norm'] = dg[0]

    dh1, _ = ffn_bwd(0, dh2, h1, n1t, uf0, a0, f0_cb)
    dy0 = _mm(dh1, W['l0_w_out'], "nt", F32, "l0_out_dgrad")
    G['l0_w_out'] = _mm(ycat0, dh1, "tn", F32, "l0_out_wgrad")
    done = ['l1_w_in', 'l0_ffn_w_in', 'l0_ffn_w_out', 'l0_w_out']
    sums, rider = scatter_on(done, "b")
    dz, dxs, dbm, dcm, ddt4, pgs, *got = _ssd_bwd(dy0, ypre, u0, act, dt, dtt, hin, a_log, d_skip,
                                                  row2(p['l0_ssd_norm']), B, nch, rider=rider)
    take_parts(done, sums, got)
    dpre, ddtr, pgd = _ssd_prep_bwd(dxs, dbm, dcm, ddt4, u0, udt, W['l0_ssd_conv_w'], ssd_cb, dt_bias, B, nch)
    dxbc, dcw0 = _conv_bwd(dpre, u0, U0_XBC, W['l0_ssd_conv_w'], 4, "l0_ssd_conv_bwd")
    dqkvg, pgr = _ret_bwd(dy0, u0, opre, rin, cos, sin, row2(p['l0_ret_norm']), B, nch)
    pgs = pgs.sum(0)
    G['l0_ssd_norm'] = pgs[:, 0, :].reshape(-1)
    G['l0_ssd_d'] = pgs[:, 1, :128].sum(0)[:SSD_HEADS]
    G['l0_ssd_a_log'] = pgs[:, 2, :128].sum(0)[:SSD_HEADS]
    G['l0_ssd_dt_bias'] = pgd.sum(0)[0, :SSD_HEADS]
    G['l0_ssd_conv_w'], G['l0_ssd_conv_b'] = dcw0[:4], dcw0[7]
    G['l0_ret_norm'] = pgr.sum(0)[0]
    dn = _mm(dqkvg, w0_main, "nt", F32, "l0_in_dgrad_qkvg")
    dn = _mm(dz, w0_main, "nt", F32, "l0_in_dgrad_z", add=dn, b_off=U0_Z)
    dn = _mm(dxbc, w0_main, "nt", F32, "l0_in_dgrad_xbc", add=dn, b_off=U0_XBC)
    dn = _mm(ddtr, w0_dt, "nt", F32, "l0_in_dgrad_dt", add=dn)
    G['l0_w_in'] = col_shards([
        _mm(n0t, dz, "nn", F32, "l0_in_wgrad_z"), _mm(n0t, dxbc, "nn", F32, "l0_in_wgrad_xbc"),
        _mm(n0t, ddtr, "nn", F32, "l0_in_wgrad_dt")[:, :SSD_HEADS], _mm(n0t, dqkvg, "nn", F32, "l0_in_wgrad_qkvg")])
    dh0, dg = _rmsnorm_bwd(h0, row2(p['l0_mix_norm']), dn, dh1, nch, "l0_mix_norm_bwd")
    G['l0_mix_norm'] = dg[0]
    dh0 = dh0.reshape(B, Pn, D)
    grad_x = dh0[:, CH:]
    G['meta_tokens'] = dh0[:, PAD:CH].sum(0)

    sums = chip_sums(['l0_w_in'], "d")
    parts['l0_w_in'], = _scatter_sums([sums['l0_w_in']])
    reds = [_sum_chips(parts[n], "sum_chips_" + n) for n in _BIG]
    grads = {}
    for n, own, other in zip(_BIG, reds, _join_halves(reds)):
        both = jnp.where(core[0] == 0, jnp.stack([own, other]), jnp.stack([other, own]))
        grads[n] = both.reshape(-1, both.shape[2])
    small_full = _unpack(_allreduce_small(_pack([G[n] for n in _SMALL])), [G[n].shape for n in _SMALL])
    for n, g in zip(_SMALL, small_full):
        if n in _SMALL_SHARDED:
            cs = g.shape[1] // 4
            g = lax.dynamic_slice_in_dim(g, chip * cs, cs, axis=1)
        grads[n] = g.reshape(p[n].shape)

    delta, new_m, new_v = {}, {}, {}
    for n in _BIG:
        delta[n], new_m[n], new_v[n] = _adamw(p[n], grads[n], p['m_' + n], p['v_' + n], "adamw_" + n)
    shapes = [p[n].shape for n in _SMALL]
    outs = _adamw(_pack([p[n] for n in _SMALL]), _pack([grads[n] for n in _SMALL]), _pack([p['m_' + n] for n in _SMALL]),
                  _pack([p['v_' + n] for n in _SMALL]), "adamw_small")
    for dst, buf in zip((delta, new_m, new_v), outs):
        for n, a in zip(_SMALL, _unpack(buf, shapes)):
            dst[n] = a
    return (loss, grad_x, *[grads[n] for n in _W_NAMES], *[delta[n] for n in _W_NAMES],
            *[new_m[n] for n in _W_NAMES], *[new_v[n] for n in _W_NAMES])
```

```python
import math

import numpy as np
import jax
import jax.numpy as jnp
from jax import lax
from jax.experimental import pallas as pl
from jax.experimental.pallas import tpu as pltpu

F32 = jnp.float32
BF16 = jnp.bfloat16
_MXU = jnp.bfloat16

D = 1024
CH = 128
N_META = 16
PAD = CH - N_META
EPS = 1e-6

SSD_HEADS = 16
SSD_HD = 64
SSD_GROUPS = 4
RET_HEADS = 4
RET_DK = 256
SB_HEADS = 16
SB_HD = 64
LRU_BLOCKS = 8
LRU_C = 8.0
FFN = 2816
U0_Z = 4096
U0_XBC = 5120

VMEM_LIMIT = 56 * 1024 * 1024


def _cparams(sem):
    return pltpu.CompilerParams(dimension_semantics=sem, vmem_limit_bytes=VMEM_LIMIT)


def _dot(a, b, dims=((1,), (0,))):
    return lax.dot_general(a.astype(_MXU), b.astype(_MXU), (dims, ((), ())), preferred_element_type=F32)


def _dot_nt(a, b):
    return _dot(a, b, ((1,), (1,)))


def _dot_tn(a, b):
    return _dot(a.T, b)


def _dot_exact(a, b):
    return lax.dot_general(a, b, (((1,), (0,)), ((), ())), preferred_element_type=F32,
                           precision=lax.Precision.HIGHEST)


def _dot_split(x, m01):
    hi = x.astype(BF16)
    lo = (x - hi.astype(F32)).astype(BF16)
    m = m01.astype(BF16)
    return jnp.dot(hi, m, preferred_element_type=F32) + jnp.dot(lo, m, preferred_element_type=F32)


def _sigmoid(x):
    return 0.5 * jnp.tanh(0.5 * x) + 0.5


def _softplus(x):
    return jnp.maximum(x, 0.0) + jnp.log1p(jnp.exp(-jnp.abs(x)))


def _silu(x):
    return x * _sigmoid(x)


def _dsilu(x):
    s = _sigmoid(x)
    return s * (1.0 + x * (1.0 - s))


_GELU_C = math.sqrt(2.0 / math.pi)


def _gelu(x):
    return 0.5 * x * (1.0 + jnp.tanh(_GELU_C * (x + 0.044715 * x * x * x)))


def _dgelu(x):
    t = jnp.tanh(_GELU_C * (x + 0.044715 * x * x * x))
    return 0.5 * (1.0 + t) + 0.5 * x * (1.0 - t * t) * _GELU_C * (1.0 + 3.0 * 0.044715 * x * x)


def _row_ids(n, cols=1):
    return lax.broadcasted_iota(jnp.int32, (n, cols), 0)


def _lane_ids(rows, n):
    return lax.broadcasted_iota(jnp.int32, (rows, n), 1)


def _real_rows(chunk):
    return chunk * CH + _row_ids(CH) >= PAD


def _shift_down(prev8, cur, s):
    cat = jnp.concatenate([prev8, cur], axis=0)
    return pltpu.roll(cat, s, axis=0)[8:]


def _shift_up(cur, next8, s):
    n = cur.shape[0]
    cat = jnp.concatenate([cur, next8], axis=0)
    return pltpu.roll(cat, n + 8 - s, axis=0)[:n]


def _conv_pre(prev8, cur, w_ref, b_ref, K):
    acc = cur * w_ref[K - 1:K, :] + b_ref[...]
    for s in range(1, K):
        acc = acc + _shift_down(prev8, cur, s) * w_ref[K - 1 - s:K - s, :]
    return acc


def _prev8_map(nch, col):
    return lambda b, c: (jnp.maximum((b * nch + c) * (CH // 8) - 1, 0), col)


def _matmul(a, b, mode, out_dtype, tm, tn, tk, name, add=None, b_off=0):
    if mode == "nn":
        (M, K), (_, N) = a.shape, b.shape
    elif mode == "nt":
        (M, K), N = a.shape, b.shape[0]
    else:
        (K, M), (_, N) = a.shape, b.shape
    tm, tn, tk = min(tm, M), min(tn, N), min(tk, K)
    assert M % tm == 0 and N % tn == 0 and K % tk == 0 and b_off % tk == 0, (name, M, N, K, tm, tn, tk)
    koff = b_off // tk
    nk = K // tk
    dims = {"nn": ((1,), (0,)), "nt": ((1,), (1,)), "tn": ((0,), (0,))}[mode]
    if mode == "tn":
        a_spec = pl.BlockSpec((tk, tm), lambda i, j, k: (k, i))
    else:
        a_spec = pl.BlockSpec((tm, tk), lambda i, j, k: (i, k))
    if mode == "nt":
        b_spec = pl.BlockSpec((tn, tk), lambda i, j, k: (j, k + koff))
    else:
        b_spec = pl.BlockSpec((tk, tn), lambda i, j, k: (k, j))
    o_spec = pl.BlockSpec((tm, tn), lambda i, j, k: (i, j))
    has_add = add is not None

    def body(a_ref, b_ref, *rest):
        if has_add:
            add_ref, o_ref, acc = rest
        else:
            o_ref, acc = rest
        k = pl.program_id(2)

        @pl.when(k == 0)
        def _():
            acc[...] = jnp.zeros_like(acc)

        acc[...] += _dot(a_ref[...], b_ref[...], dims)

        @pl.when(k == nk - 1)
        def _():
            r = acc[...]
            if has_add:
                r = r + add_ref[...].astype(F32)
            o_ref[...] = r.astype(out_dtype)

    in_specs = [a_spec, b_spec] + ([o_spec] if has_add else [])
    args = (a, b) + ((add,) if has_add else ())
    return pl.pallas_call(
        body, name=name, grid=(M // tm, N // tn, nk),
        in_specs=in_specs, out_specs=o_spec,
        out_shape=jax.ShapeDtypeStruct((M, N), out_dtype),
        scratch_shapes=[pltpu.VMEM((tm, tn), F32)],
        compiler_params=_cparams(("parallel", "parallel", "arbitrary")),
    )(*args)


def _tile(n, prefs):
    for t in prefs:
        if n % t == 0:
            return t
    return n


def _mm(a, b, mode, out_dtype, name, add=None, b_off=0):
    if mode == "tn":
        K, M = a.shape
        N = b.shape[1]
        tm, tn, tk = _tile(M, (1024, 1408, 512, 256, 128)), _tile(N, (1024, 1408, 512, 256, 128)), _tile(K, (2176, 384, 256, 128))
    else:
        M, K = a.shape
        N = b.shape[1] if mode == "nn" else b.shape[0]
        tm = _tile(M, (1088, 1024, 768, 512, 384, 256, 128))
        tn = _tile(N, (1024, 1408, 512, 256, 128))
        tk = _tile(K, (2176, 1024, 1408, 512, 256, 128))
    return _matmul(a, b, mode, out_dtype, tm, tn, tk, name, add=add, b_off=b_off)


def _rmsnorm_fwd(h, g, name):
    R = h.shape[0]
    tr = 2 * CH

    def body(h_ref, g_ref, o_ref, ot_ref):
        x = h_ref[...]
        r = lax.rsqrt(jnp.mean(x * x, axis=-1, keepdims=True) + EPS)
        y = x * r * g_ref[...]
        o_ref[...] = y.astype(o_ref.dtype)
        ot_ref[...] = y.T.astype(ot_ref.dtype)

    return pl.pallas_call(
        body, name=name, grid=(R // tr,),
        in_specs=[pl.BlockSpec((tr, D), lambda i: (i, 0)), pl.BlockSpec((1, D), lambda i: (0, 0))],
        out_specs=[pl.BlockSpec((tr, D), lambda i: (i, 0)), pl.BlockSpec((D, tr), lambda i: (0, i))],
        out_shape=[jax.ShapeDtypeStruct((R, D), _MXU), jax.ShapeDtypeStruct((D, R), _MXU)],
        compiler_params=_cparams(("parallel",)),
    )(h, g)


def _rmsnorm_bwd(h, g, dn, dres, nch, name):
    R = h.shape[0]
    per = 4
    tr = nch * CH // per

    def body(h_ref, g_ref, dn_ref, dres_ref, dh_ref, dg_ref):
        i = pl.program_id(0)
        x = h_ref[...]
        r = lax.rsqrt(jnp.mean(x * x, axis=-1, keepdims=True) + EPS)
        xhat = x * r
        dn_v = dn_ref[...]
        dx = dn_v * g_ref[...]
        dh = r * (dx - xhat * jnp.mean(dx * xhat, axis=-1, keepdims=True))
        keep = (i % per) * tr + _row_ids(tr) >= PAD
        dh_ref[...] = jnp.where(keep, dres_ref[...] + dh, 0.0)

        @pl.when(i == 0)
        def _():
            dg_ref[...] = jnp.zeros_like(dg_ref)

        dg_ref[...] += jnp.sum(dn_v * xhat, axis=0, keepdims=True)

    row = pl.BlockSpec((tr, D), lambda i: (i, 0))
    vec = pl.BlockSpec((1, D), lambda i: (0, 0))
    return pl.pallas_call(
        body, name=name, grid=(R // tr,),
        in_specs=[row, vec, row, row], out_specs=[row, vec],
        out_shape=[jax.ShapeDtypeStruct((R, D), F32), jax.ShapeDtypeStruct((1, D), F32)],
        compiler_params=_cparams(("arbitrary",)),
    )(h, g, dn, dres)


def _ssd_prep(u0, udt, conv_w, conv_b, dt_bias, B, nch, rider=None):
    R = u0.shape[0]

    def body(xs_ref, xsp_ref, bc_ref, bcp_ref, udt_ref, w0_ref, w1_ref, b0_ref, b1_ref, dtb_ref,
             act_ref, dt_ref, dtt_ref):
        keep = _real_rows(pl.program_id(1))
        a0 = _silu(_conv_pre(xsp_ref[...], xs_ref[...], w0_ref, b0_ref, 4))
        a1 = _silu(_conv_pre(bcp_ref[...], bc_ref[...], w1_ref, b1_ref, 4))
        act_ref[:, :1024] = jnp.where(keep, a0, 0.0)
        act_ref[:, 1024:] = jnp.where(keep, a1, 0.0)
        ok = jnp.logical_and(keep, _lane_ids(1, 128) < SSD_HEADS)
        dt = jnp.where(ok, _softplus(udt_ref[...] + dtb_ref[...]), 0.0)
        dt_ref[...] = dt
        dtt_ref[...] = dt.T

    row = lambda col: pl.BlockSpec((CH, 1024), lambda b, c: (b * nch + c, col))
    prev = lambda col: pl.BlockSpec((8, 1024), _prev8_map(nch, col))
    kw = dict(
        grid=(B, nch),
        in_specs=[row(5), prev(5), row(6), prev(6),
                  pl.BlockSpec((CH, 128), lambda b, c: (b * nch + c, 0)),
                  pl.BlockSpec((4, 1024), lambda b, c: (0, 0)), pl.BlockSpec((4, 1024), lambda b, c: (0, 1)),
                  pl.BlockSpec((1, 1024), lambda b, c: (0, 0)), pl.BlockSpec((1, 1024), lambda b, c: (0, 1)),
                  pl.BlockSpec((1, 128), lambda b, c: (0, 0))],
        out_specs=[pl.BlockSpec((CH, 2048), lambda b, c: (b * nch + c, 0)),
                   pl.BlockSpec((CH, 128), lambda b, c: (b * nch + c, 0)),
                   pl.BlockSpec((128, CH), lambda b, c: (0, b * nch + c))],
        out_shape=[jax.ShapeDtypeStruct((R, 2048), F32), jax.ShapeDtypeStruct((R, 128), F32),
                   jax.ShapeDtypeStruct((128, R), F32)])
    return _call(body, "ssd_prep", ("arbitrary", "arbitrary"), kw,
                 (u0, u0, u0, u0, udt, conv_w, conv_w, conv_b, conv_b, dt_bias), rider)


def _ssd_head_terms(h, a_vec, dt_v, dtt_v, dsk_v):
    lane = _lane_ids(1, 128)
    sub = _row_ids(128)
    r = _row_ids(CH, CH)
    cidx = _lane_ids(CH, CH)
    a_h = jnp.sum(jnp.where(lane == h, a_vec, 0.0), axis=1, keepdims=True)
    dt_col = jnp.sum(jnp.where(lane == h, dt_v, 0.0), axis=1, keepdims=True)
    dt_row = jnp.sum(jnp.where(sub == h, dtt_v, 0.0), axis=0, keepdims=True)
    cs_col = jnp.sum(jnp.where(r >= cidx, dt_row * a_h, 0.0), axis=1, keepdims=True)
    cs_row = jnp.sum(jnp.where(r <= cidx, dt_col * a_h, 0.0), axis=0, keepdims=True)
    tot = jnp.sum(dt_col * a_h, axis=0, keepdims=True)
    dsk = jnp.sum(jnp.where(lane == h, dsk_v, 0.0), axis=1, keepdims=True)
    return a_h, dt_col, cs_col, cs_row, tot, dsk


def _ssd_fwd(act, u0, dt, dtt, a_log, d_skip, norm_g, B, nch, rider=None):
    R = act.shape[0]

    def body(xs_ref, bm_ref, cm_ref, z_ref, dt_ref, dtt_ref, alog_ref, dsk_ref, ng_ref,
             out_ref, ypre_ref, hin_ref, H):
        g = pl.program_id(1)
        c = pl.program_id(2)

        @pl.when(c == 0)
        def _():
            H[...] = jnp.zeros_like(H)

        hin_ref[...] = H[...]
        a_vec = -jnp.exp(alog_ref[...])
        dt_v = dt_ref[...]
        dtt_v = dtt_ref[...]
        hm = _lane_ids(1, 128) < SSD_HD
        r = _row_ids(CH, CH)
        cidx = _lane_ids(CH, CH)
        Bm = bm_ref[...]
        Cm = cm_ref[...]
        CB = _dot_nt(Cm, Bm)
        ys = []
        for pair in range(2):
            cols = slice(128 * pair, 128 * pair + 128)
            xraw = xs_ref[:, cols]
            t = [_ssd_head_terms(4 * g + 2 * pair + j, a_vec, dt_v, dtt_v, dsk_ref[...]) for j in range(2)]
            sel = lambda f: jnp.where(hm, f(t[0]), f(t[1]))
            dtp = sel(lambda q: q[1])
            Ep = sel(lambda q: jnp.exp(q[2]))
            Wp = sel(lambda q: jnp.exp(q[4] - q[2]))
            etot = sel(lambda q: jnp.exp(q[4]))
            dsk = sel(lambda q: q[5])
            X = xraw * dtp
            ydiag = jnp.zeros((CH, 128), F32)
            for j in range(2):
                Lm = jnp.where(r >= cidx, jnp.exp(t[j][2] - t[j][3]), 0.0)
                Xh = jnp.where(hm if j == 0 else jnp.logical_not(hm), X, 0.0)
                ydiag = ydiag + _dot(CB * Lm, Xh)
            Hp = H[:, cols]
            yoff = Ep * _dot(Cm, Hp)
            S = _dot(Bm.T, X * Wp)
            H[:, cols] = etot * Hp + S
            ys.append(ydiag + yoff + xraw * dsk)
        y = jnp.concatenate(ys, axis=1)
        ypre_ref[...] = y
        yg = y * _silu(z_ref[...])
        rr = lax.rsqrt(jnp.mean(yg * yg, axis=-1, keepdims=True) + EPS)
        out_ref[...] = jnp.where(_real_rows(c), yg * rr * ng_ref[...], 0.0).astype(out_ref.dtype)

    rowb = lambda w, colf: pl.BlockSpec((CH, w), lambda b, g, c: (b * nch + c, colf(g)))
    vec = pl.BlockSpec((1, 128), lambda b, g, c: (0, 0))
    kw = dict(
        grid=(B, SSD_GROUPS, nch),
        in_specs=[rowb(256, lambda g: g), rowb(128, lambda g: 8 + g), rowb(128, lambda g: 12 + g),
                  rowb(256, lambda g: 16 + g), rowb(128, lambda g: 0),
                  pl.BlockSpec((128, CH), lambda b, g, c: (0, b * nch + c)),
                  vec, vec, pl.BlockSpec((1, 256), lambda b, g, c: (0, g))],
        out_specs=[rowb(256, lambda g: g), rowb(256, lambda g: g),
                   pl.BlockSpec((None, None, None, 128, 256), lambda b, g, c: (b, g, c, 0, 0))],
        out_shape=[jax.ShapeDtypeStruct((R, 2048), _MXU), jax.ShapeDtypeStruct((R, 1024), F32),
                   jax.ShapeDtypeStruct((B, SSD_GROUPS, nch, 128, 256), F32)],
        scratch_shapes=[pltpu.VMEM((128, 256), F32)])
    return _call(body, "ssd_fwd", ("arbitrary", "arbitrary", "arbitrary"), kw,
                 (act, act, act, u0, dt, dtt, a_log, d_skip, norm_g), rider)


def _ssd_bwd(dycat, ypre, u0, act, dt, dtt, hin, a_log, d_skip, norm_g, B, nch, rider=None):
    R = act.shape[0]

    def body(dy_ref, ypre_ref, z_ref, xs_ref, bm_ref, cm_ref, dt_ref, dtt_ref, hin_ref, alog_ref, dsk_ref, ng_ref,
             dz_ref, dxs_ref, db_ref, dc_ref, ddt_ref, pg_ref, dH):
        g = pl.program_id(1)
        c = nch - 1 - pl.program_id(2)

        @pl.when(pl.program_id(2) == 0)
        def _():
            dH[...] = jnp.zeros_like(dH)
            pg_ref[...] = jnp.zeros_like(pg_ref)

        z = z_ref[...]
        y = ypre_ref[...]
        ng = ng_ref[...]
        dout = jnp.where(_real_rows(c), dy_ref[...], 0.0)
        sz = _sigmoid(z)
        yg = y * z * sz
        rr = lax.rsqrt(jnp.mean(yg * yg, axis=-1, keepdims=True) + EPS)
        nrm = yg * rr
        pg_ref[0:1, :] += jnp.sum(dout * nrm, axis=0, keepdims=True)
        dn = dout * ng
        dyg = rr * (dn - nrm * jnp.mean(dn * nrm, axis=-1, keepdims=True))
        dy = dyg * z * sz
        dz_ref[...] = (dyg * y * (sz * (1.0 + z * (1.0 - sz)))).astype(dz_ref.dtype)

        a_vec = -jnp.exp(alog_ref[...])
        dt_v = dt_ref[...]
        dtt_v = dtt_ref[...]
        lane = _lane_ids(1, 128)
        hm = lane < SSD_HD
        r = _row_ids(CH, CH)
        cidx = _lane_ids(CH, CH)
        last = _row_ids(CH) == CH - 1
        Bm = bm_ref[...]
        Cm = cm_ref[...]
        CB = _dot_nt(Cm, Bm)
        CBT = _dot_nt(Bm, Cm)
        dB = jnp.zeros((CH, 128), F32)
        dC = jnp.zeros((CH, 128), F32)
        dcs_all = jnp.zeros((CH, 128), F32)
        dtx_all = jnp.zeros((CH, 128), F32)
        dd_row = jnp.zeros((1, 128), F32)
        dxs = []
        for pair in range(2):
            cols = slice(128 * pair, 128 * pair + 128)
            xraw = xs_ref[:, cols]
            dyp = dy[:, cols]
            heads = [4 * g + 2 * pair + j for j in range(2)]
            t = [_ssd_head_terms(heads[j], a_vec, dt_v, dtt_v, dsk_ref[...]) for j in range(2)]
            sel = lambda f: jnp.where(hm, f(t[0]), f(t[1]))
            hsum = lambda v, j: jnp.sum(jnp.where(hm if j == 0 else jnp.logical_not(hm), v, 0.0), axis=1, keepdims=True)
            dtp = sel(lambda q: q[1])
            Ep = sel(lambda q: jnp.exp(q[2]))
            Wp = sel(lambda q: jnp.exp(q[4] - q[2]))
            etot = sel(lambda q: jnp.exp(q[4]))
            dsk = sel(lambda q: q[5])
            X = xraw * dtp
            Hp = hin_ref[:, cols]
            dHn = dH[:, cols]
            dskip = jnp.sum(dyp * xraw, axis=0, keepdims=True)
            yoff = Ep * _dot(Cm, Hp)
            dE = dyp * yoff
            dC = dC + _dot_nt(dyp * Ep, Hp)
            dH[:, cols] = etot * dHn + _dot(Cm.T, dyp * Ep)
            BdS = _dot(Bm, dHn)
            dX = Wp * BdS
            ew = X * BdS * Wp
            dB = dB + _dot_nt(X * Wp, dHn)
            hh = jnp.sum(dHn * Hp, axis=0, keepdims=True) * etot
            for j in range(2):
                hmask = hm if j == 0 else jnp.logical_not(hm)
                cs_col, cs_row = t[j][2], t[j][3]
                Lm = jnp.where(r >= cidx, jnp.exp(cs_col - cs_row), 0.0)
                LmT = jnp.where(cidx >= r, jnp.exp(cs_row - cs_col), 0.0)
                dyh = jnp.where(hmask, dyp, 0.0)
                Xh = jnp.where(hmask, X, 0.0)
                dM = _dot_nt(dyh, Xh)
                dMT = _dot_nt(Xh, dyh)
                M = CB * Lm
                MT = CBT * LmT
                dX = dX + _dot(MT, dyh)
                dC = dC + _dot(dM * Lm, Bm)
                dB = dB + _dot(dMT * LmT, Cm)
                g_rows = jnp.sum(dM * M, axis=1, keepdims=True)
                g_cols = jnp.sum(dMT * MT, axis=1, keepdims=True)
                dtot = (jnp.sum(hsum(ew, j), axis=0, keepdims=True)
                        + jnp.sum(jnp.where(hmask, hh, 0.0), axis=1, keepdims=True))
                dcs = g_rows - g_cols + hsum(dE, j) - hsum(ew, j) + jnp.where(last, dtot, 0.0)
                dcs_all = dcs_all + jnp.where(lane == heads[j], dcs, 0.0)
                dtx_all = dtx_all + jnp.where(lane == heads[j], hsum(dX * xraw, j), 0.0)
                dd_row = dd_row + jnp.where(lane == heads[j],
                                            jnp.sum(jnp.where(hmask, dskip, 0.0), axis=1, keepdims=True), 0.0)
            dxs.append(dX * dtp + dyp * dsk)
        dxs_ref[...] = jnp.concatenate(dxs, axis=1)
        db_ref[...] = dB
        dc_ref[...] = dC
        dadt = _dot_exact(jnp.where(cidx >= r, 1.0, 0.0), dcs_all)
        ddt_ref[...] = dadt * a_vec + dtx_all
        pg_ref[1:2, 0:128] += dd_row
        pg_ref[2:3, 0:128] += jnp.sum(dadt * dt_v, axis=0, keepdims=True) * a_vec

    rowb = lambda w, colf: pl.BlockSpec((CH, w), lambda b, g, c: (b * nch + nch - 1 - c, colf(g)))
    vec = pl.BlockSpec((1, 128), lambda b, g, c: (0, 0))
    kw = dict(
        grid=(B, SSD_GROUPS, nch),
        in_specs=[rowb(256, lambda g: g), rowb(256, lambda g: g), rowb(256, lambda g: 16 + g), rowb(256, lambda g: g),
                  rowb(128, lambda g: 8 + g), rowb(128, lambda g: 12 + g), rowb(128, lambda g: 0),
                  pl.BlockSpec((128, CH), lambda b, g, c: (0, b * nch + nch - 1 - c)),
                  pl.BlockSpec((None, None, None, 128, 256), lambda b, g, c: (b, g, nch - 1 - c, 0, 0)),
                  vec, vec, pl.BlockSpec((1, 256), lambda b, g, c: (0, g))],
        out_specs=[rowb(256, lambda g: g), rowb(256, lambda g: g), rowb(128, lambda g: g), rowb(128, lambda g: g),
                   rowb(128, lambda g: g),
                   pl.BlockSpec((None, None, 8, 256), lambda b, g, c: (b, g, 0, 0))],
        out_shape=[jax.ShapeDtypeStruct((R, 1024), _MXU), jax.ShapeDtypeStruct((R, 1024), F32),
                   jax.ShapeDtypeStruct((R, 512), F32), jax.ShapeDtypeStruct((R, 512), F32),
                   jax.ShapeDtypeStruct((R, 512), F32), jax.ShapeDtypeStruct((B, SSD_GROUPS, 8, 256), F32)],
        scratch_shapes=[pltpu.VMEM((128, 256), F32)])
    return _call(body, "ssd_bwd", ("arbitrary", "arbitrary", "arbitrary"), kw,
                 (dycat, ypre, u0, act, act, act, dt, dtt, hin, a_log, d_skip, norm_g), rider)


def _ssd_prep_bwd(dxs, dB, dC, ddt4, u0, udt, conv_w, conv_b, dt_bias, B, nch, rider=None):
    R = u0.shape[0]

    def body(dxs_ref, db_ref, dc_ref, ddt_ref, xs_ref, xsp_ref, bc_ref, bcp_ref, udt_ref, w0_ref, w1_ref, b0_ref, b1_ref,
             dtb_ref, dpre_ref, ddtr_ref, pgd_ref):
        c = pl.program_id(1)

        @pl.when(c == 0)
        def _():
            pgd_ref[...] = jnp.zeros_like(pgd_ref)

        keep = _real_rows(c)
        p0 = _conv_pre(xsp_ref[...], xs_ref[...], w0_ref, b0_ref, 4)
        p1 = _conv_pre(bcp_ref[...], bc_ref[...], w1_ref, b1_ref, 4)
        dpre_ref[:, :1024] = jnp.where(keep, dxs_ref[...] * _dsilu(p0), 0.0)
        dpre_ref[:, 1024:] = jnp.where(keep, jnp.concatenate([db_ref[...], dc_ref[...]], axis=1) * _dsilu(p1), 0.0)
        ddt = ddt_ref[:, 0:128] + ddt_ref[:, 128:256] + ddt_ref[:, 256:384] + ddt_ref[:, 384:512]
        ok = jnp.logical_and(keep, _lane_ids(1, 128) < SSD_HEADS)
        dr = jnp.where(ok, ddt * _sigmoid(udt_ref[...] + dtb_ref[...]), 0.0)
        ddtr_ref[...] = dr
        pgd_ref[0:1, :] += jnp.sum(dr, axis=0, keepdims=True)

    rw = lambda w: pl.BlockSpec((CH, w), lambda b, c: (b * nch + c, 0))
    row = lambda col: pl.BlockSpec((CH, 1024), lambda b, c: (b * nch + c, col))
    prev = lambda col: pl.BlockSpec((8, 1024), _prev8_map(nch, col))
    kw = dict(
        grid=(B, nch),
        in_specs=[rw(1024), rw(512), rw(512), rw(512), row(5), prev(5), row(6), prev(6), rw(128),
                  pl.BlockSpec((4, 1024), lambda b, c: (0, 0)), pl.BlockSpec((4, 1024), lambda b, c: (0, 1)),
                  pl.BlockSpec((1, 1024), lambda b, c: (0, 0)), pl.BlockSpec((1, 1024), lambda b, c: (0, 1)),
                  pl.BlockSpec((1, 128), lambda b, c: (0, 0))],
        out_specs=[rw(2048), rw(128), pl.BlockSpec((None, 8, 128), lambda b, c: (b, 0, 0))],
        out_shape=[jax.ShapeDtypeStruct((R, 2048), F32), jax.ShapeDtypeStruct((R, 128), F32),
                   jax.ShapeDtypeStruct((B, 8, 128), F32)])
    return _call(body, "ssd_prep_bwd", ("arbitrary", "arbitrary"), kw,
                 (dxs, dB, dC, ddt4, u0, u0, u0, u0, udt, conv_w, conv_w, conv_b, conv_b, dt_bias), rider)


def _conv_bwd(dpre, xin, xin_col, w, K, name, tc=1024):
    R, C = dpre.shape
    assert C % tc == 0 and xin_col % tc == 0
    nr = R // CH
    xoff = xin_col // tc

    def body(dp_ref, dpn_ref, x_ref, xp_ref, w_ref, din_ref, dw_ref):
        i = pl.program_id(1)

        @pl.when(i == 0)
        def _():
            dw_ref[...] = jnp.zeros_like(dw_ref)

        dp = dp_ref[...]
        nxt = dpn_ref[...] * (i < nr - 1).astype(F32)
        x = x_ref[...]
        xp = xp_ref[...]
        din = dp * w_ref[K - 1:K, :]
        dw_ref[K - 1:K, :] += jnp.sum(dp * x, axis=0, keepdims=True)
        dw_ref[7:8, :] += jnp.sum(dp, axis=0, keepdims=True)
        for s in range(1, K):
            din = din + _shift_up(dp, nxt, s) * w_ref[K - 1 - s:K - s, :]
            dw_ref[K - 1 - s:K - s, :] += jnp.sum(dp * _shift_down(xp, x, s), axis=0, keepdims=True)
        din_ref[...] = din.astype(din_ref.dtype)

    return pl.pallas_call(
        body, name=name, grid=(C // tc, nr),
        in_specs=[pl.BlockSpec((CH, tc), lambda j, i: (i, j)),
                  pl.BlockSpec((8, tc), lambda j, i: (jnp.minimum((i + 1) * (CH // 8), nr * (CH // 8) - 1), j)),
                  pl.BlockSpec((CH, tc), lambda j, i: (i, xoff + j)),
                  pl.BlockSpec((8, tc), lambda j, i: (jnp.maximum(i * (CH // 8) - 1, 0), xoff + j)),
                  pl.BlockSpec((K, tc), lambda j, i: (0, j))],
        out_specs=[pl.BlockSpec((CH, tc), lambda j, i: (i, j)),
                   pl.BlockSpec((8, tc), lambda j, i: (0, j))],
        out_shape=[jax.ShapeDtypeStruct((R, C), _MXU), jax.ShapeDtypeStruct((8, C), F32)],
        compiler_params=_cparams(("parallel", "arbitrary")),
    )(dpre, dpre, xin, xin, w)


_RET_LG = [float(v) for v in np.log1p(-np.exp2(-5.0 - np.arange(RET_HEADS, dtype=np.float32))).astype(np.float32)]
_RET_SCALE = RET_DK ** -0.5


def _rope_tables(nch):
    half = RET_DK // 2
    inv_freq = 1.0 / (10000.0 ** (jnp.arange(half, dtype=F32) / (half - 1)))
    pos = jnp.arange(nch * CH, dtype=F32) - PAD
    ang = pos[:, None] * inv_freq[None, :]
    return jnp.cos(ang), jnp.sin(ang)


def _rot(x, cos, sin):
    x1, x2 = x[:, :128], x[:, 128:]
    return jnp.concatenate([x1 * cos - x2 * sin, x1 * sin + x2 * cos], axis=1)


def _unrot(d, cos, sin):
    d1, d2 = d[:, :128], d[:, 128:]
    return jnp.concatenate([d1 * cos + d2 * sin, d2 * cos - d1 * sin], axis=1)


def _ret_decays(lg):
    r = _row_ids(CH, CH)
    cidx = _lane_ids(CH, CH)
    diff = (r - cidx).astype(F32)
    decay = jnp.where(r >= cidx, jnp.exp(lg * jnp.maximum(diff, 0.0)), 0.0)
    decay_t = jnp.where(cidx >= r, jnp.exp(lg * jnp.maximum(-diff, 0.0)), 0.0)
    idx = _row_ids(CH).astype(F32)
    zeta = jnp.exp(lg * (CH - 1.0 - idx))
    xi = jnp.exp(lg * (idx + 1.0))
    return decay, decay_t, zeta, xi


def _ret_fwd(u0, ycat, cos, sin, norm_g, B, nch, rider=None):
    R = u0.shape[0]

    def body(u_ref, cos_ref, sin_ref, ng_ref, ycat_in, out_ref, opre_ref, rin_ref, Rst):
        c = pl.program_id(1)

        @pl.when(c == 0)
        def _():
            Rst[...] = jnp.zeros_like(Rst)

        cos_v, sin_v = cos_ref[...], sin_ref[...]
        for h in range(RET_HEADS):
            lg = _RET_LG[h]
            cols = slice(256 * h, 256 * h + 256)
            decay, _, zeta, xi = _ret_decays(lg)
            qr = _rot(u_ref[:, cols], cos_v, sin_v)
            kr = _rot(u_ref[:, 1024 + 256 * h:1024 + 256 * h + 256], cos_v, sin_v) * _RET_SCALE
            v = u_ref[:, 2048 + 256 * h:2048 + 256 * h + 256]
            gate = u_ref[:, 3072 + 256 * h:3072 + 256 * h + 256]
            Rh = Rst[h]
            rin_ref[h] = Rh
            inner = _dot(_dot_nt(qr, kr) * decay, v)
            cross = _dot(qr, Rh) * xi
            Rst[h] = math.exp(CH * lg) * Rh + _dot((kr * zeta).T, v)
            o = inner + cross
            opre_ref[:, cols] = o
            oc = o - jnp.mean(o, axis=-1, keepdims=True)
            rr = lax.rsqrt(jnp.mean(oc * oc, axis=-1, keepdims=True) + EPS)
            out_ref[:, cols] = (_silu(gate) * (oc * rr * ng_ref[:, cols])).astype(out_ref.dtype)

    kw = dict(
        grid=(B, nch),
        in_specs=[pl.BlockSpec((CH, 4096), lambda b, c: (b * nch + c, 0)),
                  pl.BlockSpec((CH, 128), lambda b, c: (c, 0)), pl.BlockSpec((CH, 128), lambda b, c: (c, 0)),
                  pl.BlockSpec((1, 1024), lambda b, c: (0, 0)),
                  pl.BlockSpec(memory_space=pl.ANY)],
        out_specs=[pl.BlockSpec((CH, 1024), lambda b, c: (b * nch + c, 1)),
                   pl.BlockSpec((CH, 1024), lambda b, c: (b * nch + c, 0)),
                   pl.BlockSpec((None, None, RET_HEADS, 256, 256), lambda b, c: (b, c, 0, 0, 0))],
        out_shape=[jax.ShapeDtypeStruct(ycat.shape, ycat.dtype), jax.ShapeDtypeStruct((R, 1024), F32),
                   jax.ShapeDtypeStruct((B, nch, RET_HEADS, 256, 256), F32)],
        scratch_shapes=[pltpu.VMEM((RET_HEADS, 256, 256), F32)],
        input_output_aliases={4: 0})
    return _call(body, "ret_fwd", ("arbitrary", "arbitrary"), kw, (u0, cos, sin, norm_g, ycat), rider)


def _ret_bwd(dycat, u0, opre, rin, cos, sin, norm_g, B, nch, rider=None):
    R = u0.shape[0]

    def body(dy_ref, u_ref, opre_ref, rin_ref, cos_ref, sin_ref, ng_ref, du_ref, pg_ref, dR):
        @pl.when(pl.program_id(1) == 0)
        def _():
            dR[...] = jnp.zeros_like(dR)
            pg_ref[...] = jnp.zeros_like(pg_ref)

        cos_v, sin_v = cos_ref[...], sin_ref[...]
        for h in range(RET_HEADS):
            lg = _RET_LG[h]
            cols = slice(256 * h, 256 * h + 256)
            decay, decay_t, zeta, xi = _ret_decays(lg)
            qr = _rot(u_ref[:, cols], cos_v, sin_v)
            kr = _rot(u_ref[:, 1024 + 256 * h:1024 + 256 * h + 256], cos_v, sin_v) * _RET_SCALE
            v = u_ref[:, 2048 + 256 * h:2048 + 256 * h + 256]
            gate = u_ref[:, 3072 + 256 * h:3072 + 256 * h + 256]
            ng = ng_ref[:, cols]
            o = opre_ref[:, cols]
            oc = o - jnp.mean(o, axis=-1, keepdims=True)
            rr = lax.rsqrt(jnp.mean(oc * oc, axis=-1, keepdims=True) + EPS)
            ohat = oc * rr
            dout = dy_ref[:, cols]
            du_ref[:, 3072 + 256 * h:3072 + 256 * h + 256] = (dout * (ohat * ng) * _dsilu(gate)).astype(du_ref.dtype)
            don = dout * _silu(gate)
            pg_ref[0:1, cols] += jnp.sum(don * ohat, axis=0, keepdims=True)
            dohat = don * ng
            do = rr * (dohat - jnp.mean(dohat, axis=-1, keepdims=True)
                       - ohat * jnp.mean(dohat * ohat, axis=-1, keepdims=True))
            Rh = rin_ref[h]
            dRn = dR[h]
            sc_t = _dot_nt(kr, qr) * decay_t
            dv = _dot(sc_t, do) + _dot(kr * zeta, dRn)
            ds = _dot_nt(do, v) * decay
            ds_t = _dot_nt(v, do) * decay_t
            dox = do * xi
            dq = _dot(ds, kr) + _dot_nt(dox, Rh)
            dk = _dot(ds_t, qr) + zeta * _dot_nt(v, dRn)
            dR[h] = math.exp(CH * lg) * dRn + _dot(qr.T, dox)
            du_ref[:, cols] = _unrot(dq, cos_v, sin_v).astype(du_ref.dtype)
            du_ref[:, 1024 + 256 * h:1024 + 256 * h + 256] = (_unrot(dk, cos_v, sin_v) * _RET_SCALE).astype(du_ref.dtype)
            du_ref[:, 2048 + 256 * h:2048 + 256 * h + 256] = dv.astype(du_ref.dtype)

    rmap = lambda b, c: (b * nch + nch - 1 - c, 0)
    kw = dict(
        grid=(B, nch),
        in_specs=[pl.BlockSpec((CH, 1024), lambda b, c: (b * nch + nch - 1 - c, 1)),
                  pl.BlockSpec((CH, 4096), rmap), pl.BlockSpec((CH, 1024), rmap),
                  pl.BlockSpec((None, None, RET_HEADS, 256, 256), lambda b, c: (b, nch - 1 - c, 0, 0, 0)),
                  pl.BlockSpec((CH, 128), lambda b, c: (nch - 1 - c, 0)),
                  pl.BlockSpec((CH, 128), lambda b, c: (nch - 1 - c, 0)),
                  pl.BlockSpec((1, 1024), lambda b, c: (0, 0))],
        out_specs=[pl.BlockSpec((CH, 4096), rmap), pl.BlockSpec((None, 8, 1024), lambda b, c: (b, 0, 0))],
        out_shape=[jax.ShapeDtypeStruct((R, 4096), _MXU), jax.ShapeDtypeStruct((B, 8, 1024), F32)],
        scratch_shapes=[pltpu.VMEM((RET_HEADS, 256, 256), F32)])
    return _call(body, "ret_bwd", ("arbitrary", "arbitrary"), kw, (dycat, u0, opre, rin, cos, sin, norm_g), rider)


_SB_SCALE = SB_HD ** -0.5


_SB_NB = 3


def _sb_valid(qb, kb, live):
    qpos = qb * CH + jnp.bitwise_and(_row_ids(2 * CH, CH), CH - 1)
    kpos = kb * CH + _lane_ids(2 * CH, CH)
    first = PAD + (1 - live) * (1 << 24)
    return jnp.logical_and(kpos < qpos, kpos >= first)


_SB_DEAD = -100.0


def _sb_alive(acc):
    return (jnp.max(acc) > _SB_DEAD).astype(jnp.int32)


def _sb_softplus(z):
    return jnp.maximum(z, 0.0) + jnp.log(1.0 + jnp.exp(-jnp.abs(z)))


def _stack_heads(x):
    hm = _lane_ids(1, 128) < SB_HD
    return jnp.concatenate([jnp.where(hm, x, 0.0), jnp.where(hm, 0.0, x)], axis=0)


def _unstack_heads(x2):
    return jnp.where(_lane_ids(1, 128) < SB_HD, x2[:CH], x2[CH:])


def _sb_fwd(u1, B, nch, rider=None):
    R = u1.shape[0]
    Pn = nch * CH

    def body(q_ref, k_ref, v_ref, out_ref):
        qb = pl.program_id(2)
        q2 = _stack_heads(q_ref[...] * _SB_SCALE).astype(_MXU)
        mgt = (_row_ids(CH, CH) > _lane_ids(CH, CH)).astype(F32)

        def step(i, carry):
            out2, acc = carry
            blocks = []
            for t in range(_SB_NB):
                kb = qb - _SB_NB * i - t
                live = (kb >= 0).astype(jnp.int32)
                kbc = jnp.maximum(kb, 0)
                start = pl.multiple_of(kbc * CH, CH)
                valid = _sb_valid(qb, kbc, live)
                z = _dot_nt(q2, k_ref[pl.ds(start, CH), :])
                sp = _sb_softplus(z)
                lm = jnp.where(valid, -sp, 0.0)
                blocks.append((valid, z - sp, _dot_split(lm, mgt), jnp.sum(lm, axis=1, keepdims=True), start))
            for valid, ls, loc, rs, start in blocks:
                w = jnp.where(valid, jnp.exp(ls + loc + acc), 0.0)
                out2 = out2 + _dot(w, v_ref[pl.ds(start, CH), :])
                acc = acc + rs
            return out2, acc

        trips = (qb + _SB_NB) // _SB_NB

        def more(c):
            return jnp.logical_and(c[0] < trips, c[1] > 0)

        def trip(c):
            out2, acc = step(c[0], c[2:])
            return c[0] + 1, _sb_alive(acc), out2, acc

        init = (jnp.int32(0), jnp.int32(1), jnp.zeros((2 * CH, 128), F32), jnp.zeros((2 * CH, 1), F32))
        out2 = lax.while_loop(more, trip, init)[2]
        out_ref[...] = _unstack_heads(out2).astype(out_ref.dtype)

    qspec = lambda off: pl.BlockSpec((CH, 128), lambda b, hp, qb: (b * nch + qb, off + hp))
    kspec = lambda off: pl.BlockSpec((Pn, 128), lambda b, hp, qb: (b, off + hp))
    kw = dict(grid=(B, SB_HEADS // 2, nch), in_specs=[qspec(0), kspec(8), kspec(16)], out_specs=[qspec(0)],
              out_shape=[jax.ShapeDtypeStruct((R, 2048), _MXU)])
    return _call(body, "sb_fwd", ("arbitrary", "arbitrary", "arbitrary"), kw, (u1, u1, u1), rider)


def _sb_bwd(dycat, u1, B, nch, rider=None):
    R = u1.shape[0]
    Pn = nch * CH

    def body(q_ref, k_ref, v_ref, do_ref, dq_ref, dk_ref, dv_ref):
        qb = pl.program_id(2)

        @pl.when(qb == 0)
        def _():
            dk_ref[...] = jnp.zeros_like(dk_ref)
            dv_ref[...] = jnp.zeros_like(dv_ref)

        q2 = _stack_heads(q_ref[...] * _SB_SCALE)
        do2 = _stack_heads(do_ref[...])
        q2t, do2t = q2.T.astype(_MXU), do2.T.astype(_MXU)
        q2, do2 = q2.astype(_MXU), do2.astype(_MXU)
        rr = _row_ids(CH, CH)
        cc = _lane_ids(CH, CH)
        mle = (rr <= cc).astype(F32)
        mlt = (rr < cc).astype(F32)
        trips = (qb + _SB_NB) // _SB_NB

        def more(c):
            return jnp.logical_and(c[0] < trips, c[1] > 0)

        def scan(c):
            acc = c[2]
            for t in range(_SB_NB):
                kb = qb - _SB_NB * c[0] - t
                kbc = jnp.maximum(kb, 0)
                z = _dot_nt(q2, k_ref[pl.ds(pl.multiple_of(kbc * CH, CH), CH), :])
                lm = jnp.where(_sb_valid(qb, kbc, (kb >= 0).astype(jnp.int32)), -_sb_softplus(z), 0.0)
                acc = acc + jnp.sum(lm, axis=1, keepdims=True)
            return c[0] + 1, _sb_alive(acc), acc

        used, _, s2 = lax.while_loop(more, scan, (jnp.int32(0), jnp.int32(1), jnp.zeros((2 * CH, 1), F32)))
        base = qb + 1 - _SB_NB * used

        def step(i, carry):
            dq2, pacc, gacc = carry
            blocks = []
            for t in range(_SB_NB):
                kb = base + _SB_NB * i + t
                live = (kb >= 0).astype(jnp.int32)
                start = pl.multiple_of(jnp.maximum(kb, 0) * CH, CH)
                valid = _sb_valid(qb, jnp.maximum(kb, 0), live)
                z = _dot_nt(q2, k_ref[pl.ds(start, CH), :])
                sp = _sb_softplus(z)
                lm = jnp.where(valid, -sp, 0.0)
                blocks.append((valid, z - sp, _dot_split(lm, mle), jnp.sum(lm, axis=1, keepdims=True), start))
            stage = []
            for valid, ls, ploc, rs, start in blocks:
                w = jnp.where(valid, jnp.exp(ls + (s2 - (ploc + pacc))), 0.0)
                gg = _dot_nt(do2, v_ref[pl.ds(start, CH), :]) * w
                stage.append((valid, ls, w, gg, _dot_split(gg, mlt), jnp.sum(gg, axis=1, keepdims=True), start))
                pacc = pacc + rs
            for valid, ls, w, gg, gloc, gs, start in stage:
                sig = jnp.exp(ls)
                dz = jnp.where(valid, gg * (1.0 - sig) - (gloc + gacc) * sig, 0.0)
                dq2 = dq2 + _dot(dz, k_ref[pl.ds(start, CH), :])
                dk_ref[:, pl.ds(start, CH)] += _dot(q2t, dz)
                dv_ref[:, pl.ds(start, CH)] += _dot(do2t, w)
                gacc = gacc + gs
            return dq2, pacc, gacc

        zero = jnp.zeros((2 * CH, 1), F32)
        dq2 = lax.fori_loop(0, used, step, (jnp.zeros((2 * CH, 128), F32), zero, zero))[0]
        dq_ref[...] = (_unstack_heads(dq2) * _SB_SCALE).astype(dq_ref.dtype)

    qspec = lambda off: pl.BlockSpec((CH, 128), lambda b, hp, qb: (b * nch + qb, off + hp))
    kspec = lambda off: pl.BlockSpec((Pn, 128), lambda b, hp, qb: (b, off + hp))
    tspec = pl.BlockSpec((128, Pn), lambda b, hp, qb: (hp, b))
    full = jax.ShapeDtypeStruct((1024, R), F32)
    kw = dict(grid=(B, SB_HEADS // 2, nch), in_specs=[qspec(0), kspec(8), kspec(16), qspec(0)],
              out_specs=[qspec(0), tspec, tspec], out_shape=[jax.ShapeDtypeStruct((R, 1024), _MXU), full, full])
    return _call(body, "sb_bwd", ("arbitrary", "arbitrary", "arbitrary"), kw, (u1, u1, u1, dycat), rider)


def _neg_expm1(x):
    series = -(x * (1.0 + x * (0.5 + x * (1.0 / 6.0 + x * (1.0 / 24.0)))))
    return jnp.where(x > -0.05, series, 1.0 - jnp.exp(x))


def _lru_gates(x, wa_ref, ba_ref, wx_ref, bx_ref, lam_ref):
    rs, is_ = [], []
    for n in range(LRU_BLOCKS):
        xb = x[:, 128 * n:128 * n + 128]
        rs.append(_dot(xb, wa_ref[n]))
        is_.append(_dot(xb, wx_ref[n]))
    r = _sigmoid(jnp.concatenate(rs, axis=1) + ba_ref[...])
    i = _sigmoid(jnp.concatenate(is_, axis=1) + bx_ref[...])
    sp = _softplus(-lam_ref[...])
    la = -LRU_C * r * sp
    a = jnp.exp(la)
    mult = jnp.sqrt(jnp.maximum(_neg_expm1(2.0 * la), 0.0))
    return r, i, sp, a, mult


def _lru_fwd(u1, ycat, conv_w, conv_b, wa, ba, wx, bx, lam, B, nch):
    R = u1.shape[0]

    def body(x_ref, xp_ref, gate_ref, cw_ref, cb_ref, wa_ref, ba_ref, wx_ref, bx_ref, lam_ref, ycat_in,
             out_ref, hs_ref, hc):
        c = pl.program_id(1)

        @pl.when(c == 0)
        def _():
            hc[...] = jnp.zeros_like(hc)

        x = _conv_pre(xp_ref[...], x_ref[...], cw_ref, cb_ref, 4)
        r, i, sp, a, mult = _lru_gates(x, wa_ref, ba_ref, wx_ref, bx_ref, lam_ref)
        b = jnp.where(_real_rows(c), mult * (i * x), 0.0)
        rows = _row_ids(CH)
        s = 1
        while s < CH:
            a_s = jnp.where(rows >= s, pltpu.roll(a, s, axis=0), 1.0)
            b_s = jnp.where(rows >= s, pltpu.roll(b, s, axis=0), 0.0)
            b = a * b_s + b
            a = a * a_s
            s *= 2
        h = a * hc[0:1, :] + b
        hs_ref[...] = h
        hc[0:1, :] = hs_ref[CH - 1:CH, :]
        out_ref[...] = (h * _gelu(gate_ref[...])).astype(out_ref.dtype)

    row = lambda col: pl.BlockSpec((CH, 1024), lambda b, c: (b * nch + c, col))
    vec = pl.BlockSpec((1, 1024), lambda b, c: (0, 0))
    wsp = pl.BlockSpec((LRU_BLOCKS, 128, 128), lambda b, c: (0, 0, 0))
    return pl.pallas_call(
        body, name="lru_fwd", grid=(B, nch),
        in_specs=[row(4), pl.BlockSpec((8, 1024), _prev8_map(nch, 4)), row(3),
                  pl.BlockSpec((4, 1024), lambda b, c: (0, 0)), vec, wsp, vec, wsp, vec, vec,
                  pl.BlockSpec(memory_space=pl.ANY)],
        out_specs=[row(1), row(0)],
        out_shape=[jax.ShapeDtypeStruct(ycat.shape, ycat.dtype), jax.ShapeDtypeStruct((R, 1024), F32)],
        scratch_shapes=[pltpu.VMEM((8, 1024), F32)],
        input_output_aliases={10: 0},
        compiler_params=_cparams(("parallel", "arbitrary")),
    )(u1, u1, u1, conv_w, conv_b, wa, ba, wx, bx, lam, ycat)


def _lru_bwd(dycat, u1, hs, conv_w, conv_b, wa, ba, wx, bx, lam, B, nch):
    R = u1.shape[0]

    def body(dy_ref, x_ref, xp_ref, gate_ref, hs_ref, hsp_ref, cw_ref, cb_ref, wa_ref, ba_ref, wx_ref, bx_ref, lam_ref,
             dgate_ref, dxc_ref, pg_ref, dwa_ref, dwx_ref, lc):
        c = nch - 1 - pl.program_id(1)

        @pl.when(pl.program_id(1) == 0)
        def _():
            lc[...] = jnp.zeros_like(lc)
            pg_ref[...] = jnp.zeros_like(pg_ref)
            dwa_ref[...] = jnp.zeros_like(dwa_ref)
            dwx_ref[...] = jnp.zeros_like(dwx_ref)

        x = _conv_pre(xp_ref[...], x_ref[...], cw_ref, cb_ref, 4)
        r, i, sp, a, mult = _lru_gates(x, wa_ref, ba_ref, wx_ref, bx_ref, lam_ref)
        h = hs_ref[...]
        hprev = _shift_down(hsp_ref[...], h, 1)
        gate = gate_ref[...]
        dy = dy_ref[...]
        dgate_ref[...] = (dy * h * _dgelu(gate)).astype(dgate_ref.dtype)
        rows = _row_ids(CH)
        lam_t = dy * _gelu(gate) + jnp.where(rows == CH - 1, lc[0:1, :], 0.0)
        coef = jnp.where(rows < CH - 1, pltpu.roll(a, CH - 1, axis=0), 0.0)
        s = 1
        while s < CH:
            c_s = jnp.where(rows < CH - s, pltpu.roll(coef, CH - s, axis=0), 1.0)
            l_s = jnp.where(rows < CH - s, pltpu.roll(lam_t, CH - s, axis=0), 0.0)
            lam_t = coef * l_s + lam_t
            coef = coef * c_s
            s *= 2
        lc[0:1, :] = jnp.sum(jnp.where(rows == 0, a * lam_t, 0.0), axis=0, keepdims=True)
        db = jnp.where(_real_rows(c), lam_t, 0.0)
        da = db * hprev
        dmult = db * (i * x)
        di = db * mult * x
        dx = db * mult * i
        pos = mult > 0.0
        dla = da * a + jnp.where(pos, -dmult * (a * a) / jnp.where(pos, mult, 1.0), 0.0)
        dr = dla * (-LRU_C * sp)
        pg_ref[2:3, :] += jnp.sum(dla * (LRU_C * r) * _sigmoid(-lam_ref[...]), axis=0, keepdims=True)
        dpr = dr * r * (1.0 - r)
        dpi = di * i * (1.0 - i)
        pg_ref[0:1, :] += jnp.sum(dpr, axis=0, keepdims=True)
        pg_ref[1:2, :] += jnp.sum(dpi, axis=0, keepdims=True)
        dxs = []
        for n in range(LRU_BLOCKS):
            blk = slice(128 * n, 128 * n + 128)
            dxs.append(dx[:, blk] + _dot_nt(dpr[:, blk], wa_ref[n]) + _dot_nt(dpi[:, blk], wx_ref[n]))
            dwa_ref[n] += _dot_tn(x[:, blk], dpr[:, blk])
            dwx_ref[n] += _dot_tn(x[:, blk], dpi[:, blk])
        dxc_ref[...] = jnp.concatenate(dxs, axis=1)

    rmap = lambda col: (lambda b, c: (b * nch + nch - 1 - c, col))
    row = lambda col: pl.BlockSpec((CH, 1024), rmap(col))
    prev = lambda col: pl.BlockSpec(
        (8, 1024), lambda b, c: (jnp.maximum((b * nch + nch - 1 - c) * (CH // 8) - 1, 0), col))
    vec = pl.BlockSpec((1, 1024), lambda b, c: (0, 0))
    wsp = pl.BlockSpec((LRU_BLOCKS, 128, 128), lambda b, c: (0, 0, 0))
    full = jax.ShapeDtypeStruct((R, 1024), F32)
    return pl.pallas_call(
        body, name="lru_bwd", grid=(B, nch),
        in_specs=[row(1), row(4), prev(4), row(3), row(0), prev(0),
                  pl.BlockSpec((4, 1024), lambda b, c: (0, 0)), vec, wsp, vec, wsp, vec, vec],
        out_specs=[row(0), row(0), pl.BlockSpec((None, 8, 1024), lambda b, c: (b, 0, 0)),
                   pl.BlockSpec((None, LRU_BLOCKS, 128, 128), lambda b, c: (b, 0, 0, 0)),
                   pl.BlockSpec((None, LRU_BLOCKS, 128, 128), lambda b, c: (b, 0, 0, 0))],
        out_shape=[jax.ShapeDtypeStruct((R, 1024), _MXU), full, jax.ShapeDtypeStruct((B, 8, 1024), F32),
                   jax.ShapeDtypeStruct((B, LRU_BLOCKS, 128, 128), F32),
                   jax.ShapeDtypeStruct((B, LRU_BLOCKS, 128, 128), F32)],
        scratch_shapes=[pltpu.VMEM((8, 1024), F32)],
        compiler_params=_cparams(("parallel", "arbitrary")),
    )(dycat, u1, u1, u1, hs, hs, conv_w, conv_b, wa, ba, wx, bx, lam)


_FFN_TC = FFN // 2


def _ffn_specs(nch):
    nt = FFN // _FFN_TC
    row = lambda off: pl.BlockSpec((CH, _FFN_TC), lambda b, c, j: (b * nch + c, off + j))
    prev = lambda off: pl.BlockSpec(
        (8, _FFN_TC), lambda b, c, j: (jnp.maximum((b * nch + c) * (CH // 8) - 1, 0), off + j))
    wsp = lambda off: pl.BlockSpec((3, _FFN_TC), lambda b, c, j: (0, off + j))
    bsp = lambda off: pl.BlockSpec((1, _FFN_TC), lambda b, c, j: (0, off + j))
    return nt, row, [row(0), prev(0), row(nt), prev(nt), wsp(0), wsp(nt), bsp(0), bsp(nt)]


def _ffn_act_fwd(uf, conv_w, conv_b, B, nch, rider=None):
    R = uf.shape[0]
    nt, row, specs = _ffn_specs(nch)

    def body(g_ref, gp_ref, u_ref, up_ref, wg_ref, wu_ref, bg_ref, bu_ref, o_ref):
        cg = _conv_pre(gp_ref[...], g_ref[...], wg_ref, bg_ref, 3)
        cu = _conv_pre(up_ref[...], u_ref[...], wu_ref, bu_ref, 3)
        o_ref[...] = jnp.where(_real_rows(pl.program_id(1)), _silu(cg) * cu, 0.0).astype(o_ref.dtype)

    kw = dict(grid=(B, nch, nt), in_specs=specs, out_specs=[row(0)],
              out_shape=[jax.ShapeDtypeStruct((R, FFN), _MXU)])
    return _call(body, "ffn_act_fwd", ("arbitrary", "arbitrary", "arbitrary"), kw,
                 (uf, uf, uf, uf, conv_w, conv_w, conv_b, conv_b), rider)


def _ffn_act_bwd(da, uf, conv_w, conv_b, nch, name, rider=None):
    R = uf.shape[0]
    nt = FFN // _FFN_TC
    nr = R // CH
    K = 3

    def body(da_ref, dan_ref, g_ref, gp_ref, gn_ref, u_ref, up_ref, un_ref, wg_ref, wu_ref, bg_ref, bu_ref,
             dug_ref, duu_ref, dwg_ref, dwu_ref):
        i = pl.program_id(1)

        @pl.when(i == 0)
        def _():
            dwg_ref[...] = jnp.zeros_like(dwg_ref)
            dwu_ref[...] = jnp.zeros_like(dwu_ref)

        c = i % nch
        ext = CH + 8
        rows = _row_ids(ext)
        follows = (c < nch - 1).astype(jnp.int32)
        keep = jnp.logical_and(c * CH + rows >= PAD, rows < CH + 8 * follows)
        dav = jnp.where(keep, jnp.concatenate([da_ref[...], dan_ref[...]], axis=0), 0.0)

        def conv_ext(x_ref, xp_ref, xn_ref, w_ref, b_ref):
            cat = jnp.concatenate([xp_ref[...], x_ref[...], xn_ref[...]], axis=0)
            shifted = [cat[8:]] + [pltpu.roll(cat, s, axis=0)[8:] for s in range(1, K)]
            acc = shifted[0] * w_ref[K - 1:K, :] + b_ref[...]
            for s in range(1, K):
                acc = acc + shifted[s] * w_ref[K - 1 - s:K - s, :]
            return acc, shifted

        cg, gsh = conv_ext(g_ref, gp_ref, gn_ref, wg_ref, bg_ref)
        cu, ush = conv_ext(u_ref, up_ref, un_ref, wu_ref, bu_ref)
        sg = _sigmoid(cg)
        dcg = dav * cu * (sg * (1.0 + cg * (1.0 - sg)))
        dcu = dav * (cg * sg)
        for dc, xsh, w_ref, din_ref, dw_ref in ((dcg, gsh, wg_ref, dug_ref, dwg_ref), (dcu, ush, wu_ref, duu_ref, dwu_ref)):
            dp = dc[:CH]
            din = dp * w_ref[K - 1:K, :]
            dw_ref[7:8, :] += jnp.sum(dp, axis=0, keepdims=True)
            dw_ref[K - 1:K, :] += jnp.sum(dp * xsh[0][:CH], axis=0, keepdims=True)
            for s in range(1, K):
                din = din + pltpu.roll(dc, ext - s, axis=0)[:CH] * w_ref[K - 1 - s:K - s, :]
                dw_ref[K - 1 - s:K - s, :] += jnp.sum(dp * xsh[s][:CH], axis=0, keepdims=True)
            din_ref[...] = din.astype(din_ref.dtype)

    row = lambda off: pl.BlockSpec((CH, _FFN_TC), lambda j, i: (i, off + j))
    prev = lambda off: pl.BlockSpec((8, _FFN_TC), lambda j, i: (jnp.maximum(i * (CH // 8) - 1, 0), off + j))
    nxt = lambda off: pl.BlockSpec(
        (8, _FFN_TC), lambda j, i: (jnp.minimum((i + 1) * (CH // 8), nr * (CH // 8) - 1), off + j))
    wsp = lambda off: pl.BlockSpec((K, _FFN_TC), lambda j, i: (0, off + j))
    bsp = lambda off: pl.BlockSpec((1, _FFN_TC), lambda j, i: (0, off + j))
    acc = pl.BlockSpec((8, _FFN_TC), lambda j, i: (0, j))
    half = jax.ShapeDtypeStruct((R, FFN), _MXU)
    dwsh = jax.ShapeDtypeStruct((8, FFN), F32)
    kw = dict(
        grid=(nt, nr),
        in_specs=[row(0), nxt(0), row(0), prev(0), nxt(0), row(nt), prev(nt), nxt(nt), wsp(0), wsp(nt), bsp(0), bsp(nt)],
        out_specs=[row(0), row(0), acc, acc],
        out_shape=[half, half, dwsh, dwsh])
    return _call(body, name, ("arbitrary", "arbitrary"), kw,
                 (da, da, uf, uf, uf, uf, uf, uf, conv_w, conv_w, conv_b, conv_b), rider)


def _head(h, g, target, B, nch):
    R = h.shape[0]

    def body(h_ref, g_ref, t_ref, dh_ref, loss_ref, dg_ref):
        c = pl.program_id(1)

        @pl.when(c == 0)
        def _():
            dh_ref[...] = jnp.zeros_like(dh_ref)
            loss_ref[...] = jnp.zeros_like(loss_ref)
            dg_ref[...] = jnp.zeros_like(dg_ref)

        @pl.when(c > 0)
        def _():
            x = h_ref[...]
            gv = g_ref[...]
            r = lax.rsqrt(jnp.mean(x * x, axis=-1, keepdims=True) + EPS)
            xhat = x * r
            e = xhat * gv - t_ref[...]
            loss_ref[...] += 0.5 * jnp.sum(jnp.mean(e * e, axis=-1, keepdims=True), axis=0, keepdims=True)
            dy = e * (1.0 / D)
            dg_ref[0:1, :] += jnp.sum(dy * xhat, axis=0, keepdims=True)
            dx = dy * gv
            dh_ref[...] = r * (dx - xhat * jnp.mean(dx * xhat, axis=-1, keepdims=True))

    row = pl.BlockSpec((CH, D), lambda b, c: (b * nch + c, 0))
    return pl.pallas_call(
        body, name="head", grid=(B, nch),
        in_specs=[row, pl.BlockSpec((1, D), lambda b, c: (0, 0)),
                  pl.BlockSpec((CH, D), lambda b, c: (b * (nch - 1) + jnp.maximum(c - 1, 0), 0))],
        out_specs=[row, pl.BlockSpec((None, 8, 128), lambda b, c: (b, 0, 0)),
                   pl.BlockSpec((None, 8, D), lambda b, c: (b, 0, 0))],
        out_shape=[jax.ShapeDtypeStruct((R, D), F32), jax.ShapeDtypeStruct((B, 8, 128), F32),
                   jax.ShapeDtypeStruct((B, 8, D), F32)],
        compiler_params=_cparams(("parallel", "arbitrary")),
    )(h, g, target)


ADAM_LR = 0.001
ADAM_B1 = 0.9
ADAM_B2 = 0.999
ADAM_EPS = 1e-08
ADAM_WD = 0.01
ADAM_STEP = 10


def _adamw(w, g, m, v, name):
    Rr, C = w.shape
    tr = _tile(Rr, (256, 64))

    def body(w_ref, g_ref, m_ref, v_ref, d_ref, nm_ref, nv_ref):
        gv = g_ref[...]
        nm = ADAM_B1 * m_ref[...] + (1.0 - ADAM_B1) * gv
        nv = ADAM_B2 * v_ref[...] + (1.0 - ADAM_B2) * (gv * gv)
        m_hat = nm / (1.0 - ADAM_B1 ** ADAM_STEP)
        v_hat = nv / (1.0 - ADAM_B2 ** ADAM_STEP)
        d_ref[...] = -ADAM_LR * (m_hat / (jnp.sqrt(v_hat) + ADAM_EPS) + ADAM_WD * w_ref[...])
        nm_ref[...] = nm
        nv_ref[...] = nv

    spec = pl.BlockSpec((tr, C), lambda i: (i, 0))
    sh = jax.ShapeDtypeStruct((Rr, C), F32)
    return pl.pallas_call(
        body, name=name, grid=(Rr // tr,),
        in_specs=[spec] * 4, out_specs=[spec] * 3, out_shape=[sh] * 3,
        compiler_params=_cparams(("parallel",)),
    )(w, g, m, v)


_MESH = pl.DeviceIdType.MESH
_ANY = pl.BlockSpec(memory_space=pl.ANY)


def _place():
    x, y, c = lax.axis_index("x"), lax.axis_index("y"), lax.axis_index("c")
    chips = [(1 - x, y), (x, 1 - y), (1 - x, 1 - y)]
    return x, y, c, chips


def _rcopy(src, dst, ssem, rsem, dev):
    return pltpu.make_async_remote_copy(src_ref=src, dst_ref=dst, send_sem=ssem, recv_sem=rsem,
                                        device_id=dev, device_id_type=_MESH)


def _with_riders(body, kw, kind, riders):
    n_in, n_out, n_scr = len(kw["in_specs"]), len(kw["out_specs"]), len(kw.get("scratch_shapes", []))
    grid = kw["grid"]
    nr = len(riders)
    nsem = 4 if kind == "gather" else 2

    def new_body(*refs):
        ins, srcs = refs[:n_in], refs[n_in:n_in + nr]
        outs, dsts = refs[n_in + nr:n_in + nr + n_out], refs[n_in + nr + n_out:n_in + 2 * nr + n_out]
        scr = refs[n_in + 2 * nr + n_out:n_in + 2 * nr + n_out + n_scr]
        sems = refs[n_in + 2 * nr + n_out + n_scr:]
        first = last = None
        for axis, size in enumerate(grid):
            i = pl.program_id(axis)
            first = (i == 0) if first is None else jnp.logical_and(first, i == 0)
            last = (i == size - 1) if last is None else jnp.logical_and(last, i == size - 1)
        x, y, c, chips = _place()
        k = 2 * x + y
        sib = (x, y, 1 - c)
        ssem, rsem = sems[:2]
        sends = []
        for a in range(nr):
            for j, (cx, cy) in enumerate(chips):
                if kind == "gather":
                    src, dst = srcs[a].at[c], dsts[a].at[k, c]
                else:
                    src, dst = srcs[a].at[2 * cx + cy], dsts[a].at[k]
                sends.append(_rcopy(src, dst, ssem.at[3 * a + j], rsem.at[3 * a + j], (cx, cy, c)))

        @pl.when(first)
        def _():
            for cp in sends:
                cp.start()

        body(*ins, *outs, *scr)

        @pl.when(last)
        def _():
            passed = []
            for a in range(nr):
                for j, (cx, cy) in enumerate(chips):
                    got = dsts[a].at[2 * cx + cy, c] if kind == "gather" else dsts[a].at[2 * cx + cy]
                    _rcopy(got, got, ssem.at[3 * a + j], rsem.at[3 * a + j], (cx, cy, c)).wait_recv()
                    if kind == "gather":
                        fw = _rcopy(got, got, sems[2].at[3 * a + j], sems[3].at[3 * a + j], sib)
                        fw.start()
                        passed.append(fw)
            if kind == "gather":
                for a in range(nr):
                    for j, (cx, cy) in enumerate(chips):
                        got = dsts[a].at[2 * cx + cy, 1 - c]
                        _rcopy(got, got, sems[2].at[3 * a + j], sems[3].at[3 * a + j], sib).wait_recv()
            for cp in sends + passed:
                cp.wait_send()

    kw = dict(kw)
    kw["in_specs"] = list(kw["in_specs"]) + [_ANY] * nr
    kw["out_specs"] = list(kw["out_specs"]) + [_ANY] * nr
    kw["out_shape"] = list(kw["out_shape"]) + [
        jax.ShapeDtypeStruct(((4,) + r.shape) if kind == "gather" else r.shape, r.dtype) for r in riders]
    kw["scratch_shapes"] = list(kw.get("scratch_shapes", [])) + [pltpu.SemaphoreType.DMA((3 * nr,))] * nsem
    return new_body, kw


def _call(body, name, sem, kw, args, rider=None):
    if rider is not None:
        body, kw = _with_riders(body, kw, *rider)
        args = tuple(args) + tuple(rider[1])
    return pl.pallas_call(body, name=name, compiler_params=_cparams(sem), **kw)(*args)


def _fill_own(result, own, chip):
    return lax.dynamic_update_index_in_dim(result, own, chip, 0)


def _gather_shards(bigs, small):
    nb = len(bigs)

    def body(*refs):
        ins, outs = refs[:nb + 1], refs[nb + 1:2 * nb + 2]
        ssem, rsem, fssem, frsem = refs[2 * nb + 2:]
        x, y, c, chips = _place()
        k = 2 * x + y
        sib = (x, y, 1 - c)

        def part(a, slot, hc):
            return outs[a].at[slot] if a == nb else outs[a].at[slot, hc]

        first = []
        for a in range(nb + 1):
            src = ins[a] if a == nb else ins[a].at[c]
            for j, (cx, cy) in enumerate(chips):
                first.append(_rcopy(src, part(a, k, c), ssem.at[3 * a + j], rsem.at[3 * a + j], (cx, cy, c)))
        for cp in first:
            cp.start()
        passed = []
        for a in range(nb + 1):
            for j, (cx, cy) in enumerate(chips):
                got = part(a, 2 * cx + cy, c)
                _rcopy(got, got, ssem.at[3 * a + j], rsem.at[3 * a + j], (cx, cy, c)).wait_recv()
                if a < nb:
                    fw = _rcopy(got, got, fssem.at[3 * a + j], frsem.at[3 * a + j], sib)
                    fw.start()
                    passed.append(fw)
        for a in range(nb):
            for j, (cx, cy) in enumerate(chips):
                got = part(a, 2 * cx + cy, 1 - c)
                _rcopy(got, got, fssem.at[3 * a + j], frsem.at[3 * a + j], sib).wait_recv()
        for cp in first + passed:
            cp.wait_send()

    arrs = list(bigs) + [small]
    n = 3 * (nb + 1)
    return pl.pallas_call(
        body, name="gather_shards",
        in_specs=[_ANY] * (nb + 1), out_specs=[_ANY] * (nb + 1),
        out_shape=[jax.ShapeDtypeStruct((4,) + a.shape, a.dtype) for a in arrs],
        scratch_shapes=[pltpu.SemaphoreType.DMA((n,)), pltpu.SemaphoreType.DMA((n,)),
                        pltpu.SemaphoreType.DMA((n,)), pltpu.SemaphoreType.DMA((n,))],
    )(*arrs)


def _swap_halves(grads, name):
    na = len(grads)
    halves = [g.shape[1] // 2 for g in grads]

    def body(*refs):
        ins, outs = refs[:na], refs[na:2 * na]
        ssem, rsem = refs[2 * na:]
        x, y, c, _ = _place()
        sib = (x, y, 1 - c)
        cps = [_rcopy(ins[a].at[:, pl.ds((1 - c) * halves[a], halves[a]), :], outs[a], ssem.at[a], rsem.at[a], sib)
               for a in range(na)]
        for cp in cps:
            cp.start()
        for cp in cps:
            cp.wait()

    return pl.pallas_call(
        body, name=name,
        in_specs=[_ANY] * na, out_specs=[_ANY] * na,
        out_shape=[jax.ShapeDtypeStruct((4, g.shape[1] // 2, g.shape[2]), g.dtype) for g in grads],
        scratch_shapes=[pltpu.SemaphoreType.DMA((na,)), pltpu.SemaphoreType.DMA((na,))],
    )(*grads)


def _sum_rows(rh):
    return rh if rh <= 512 else _tile(rh, (512, 256, 128, 64, 32))


def _chip_sum(grad, recv, core, name):
    _, r, cdim = grad.shape
    rh = r // 2
    tr = _sum_rows(rh)
    nblk = rh // tr

    def body(core_ref, g_ref, r_ref, o_ref):
        o_ref[...] = (g_ref[...] + r_ref[...]).astype(o_ref.dtype)

    return pl.pallas_call(
        body, name=name,
        grid_spec=pltpu.PrefetchScalarGridSpec(
            num_scalar_prefetch=1, grid=(4, nblk),
            in_specs=[pl.BlockSpec((None, tr, cdim), lambda s, i, cr: (s, cr[0] * nblk + i, 0)),
                      pl.BlockSpec((None, tr, cdim), lambda s, i, cr: (s, i, 0))],
            out_specs=pl.BlockSpec((None, tr, cdim), lambda s, i, cr: (s, i, 0))),
        out_shape=jax.ShapeDtypeStruct((4, rh, cdim), BF16),
        compiler_params=_cparams(("parallel", "parallel")),
    )(core, grad, recv)


def _scatter_sums(sums):
    na = len(sums)

    def body(*refs):
        ins, outs = refs[:na], refs[na:2 * na]
        ssem, rsem, lsem = refs[2 * na:]
        x, y, c, chips = _place()
        k = 2 * x + y
        local = [pltpu.make_async_copy(ins[a].at[k], outs[a].at[k], lsem.at[a]) for a in range(na)]
        for cp in local:
            cp.start()
        cps = []
        for a in range(na):
            for j, (cx, cy) in enumerate(chips):
                cps.append(_rcopy(ins[a].at[2 * cx + cy], outs[a].at[k], ssem.at[3 * a + j], rsem.at[3 * a + j],
                                  (cx, cy, c)))
        for cp in cps:
            cp.start()
        for a in range(na):
            for j, (cx, cy) in enumerate(chips):
                got = outs[a].at[2 * cx + cy]
                _rcopy(got, got, ssem.at[3 * a + j], rsem.at[3 * a + j], (cx, cy, c)).wait_recv()
        for cp in cps:
            cp.wait_send()
        for cp in local:
            cp.wait()

    return pl.pallas_call(
        body, name="scatter_sums",
        in_specs=[_ANY] * na, out_specs=[_ANY] * na,
        out_shape=[jax.ShapeDtypeStruct(s.shape, s.dtype) for s in sums],
        scratch_shapes=[pltpu.SemaphoreType.DMA((3 * na,)), pltpu.SemaphoreType.DMA((3 * na,)),
                        pltpu.SemaphoreType.DMA((na,))],
    )(*sums)


def _sum_chips(parts, name):
    _, rh, cdim = parts.shape
    tr = _sum_rows(rh)

    def body(p_ref, o_ref):
        acc = p_ref[0].astype(F32)
        for j in range(1, 4):
            acc = acc + p_ref[j].astype(F32)
        o_ref[...] = acc

    return pl.pallas_call(
        body, name=name, grid=(rh // tr,),
        in_specs=[pl.BlockSpec((4, tr, cdim), lambda i: (0, i, 0))],
        out_specs=pl.BlockSpec((tr, cdim), lambda i: (i, 0)),
        out_shape=jax.ShapeDtypeStruct((rh, cdim), F32),
        compiler_params=_cparams(("parallel",)),
    )(parts)


def _join_halves(reds):
    na = len(reds)

    def body(*refs):
        ins, outs = refs[:na], refs[na:2 * na]
        ssem, rsem = refs[2 * na:]
        x, y, c, _ = _place()
        cps = [_rcopy(ins[a], outs[a], ssem.at[a], rsem.at[a], (x, y, 1 - c)) for a in range(na)]
        for cp in cps:
            cp.start()
        for cp in cps:
            cp.wait()

    return pl.pallas_call(
        body, name="join_halves",
        in_specs=[_ANY] * na, out_specs=[_ANY] * na,
        out_shape=[jax.ShapeDtypeStruct(r.shape, r.dtype) for r in reds],
        scratch_shapes=[pltpu.SemaphoreType.DMA((na,)), pltpu.SemaphoreType.DMA((na,))],
    )(*reds)


def _allreduce_small(buf):
    n = buf.shape[0]

    def body(in_ref, out_ref, recv, ssem, rsem):
        x, y, c, _ = _place()
        peers = [(x, y, 1 - c), (1 - x, y, c), (x, 1 - y, c)]
        out_ref[...] = in_ref[...]
        for r, peer in enumerate(peers):
            cp = _rcopy(out_ref, recv.at[r], ssem.at[r], rsem.at[r], peer)
            cp.start()
            cp.wait()
            out_ref[...] = out_ref[...] + recv[r]

    vm = pl.BlockSpec(memory_space=pltpu.VMEM)
    return pl.pallas_call(
        body, name="allreduce_small",
        in_specs=[vm], out_specs=vm,
        out_shape=jax.ShapeDtypeStruct(buf.shape, F32),
        scratch_shapes=[pltpu.VMEM((3, n, 128), F32), pltpu.SemaphoreType.DMA((3,)), pltpu.SemaphoreType.DMA((3,))],
        compiler_params=pltpu.CompilerParams(vmem_limit_bytes=VMEM_LIMIT),
    )(buf)


_W_NAMES = ['meta_tokens', 'l0_mix_norm', 'l0_w_in', 'l0_ssd_conv_w', 'l0_ssd_conv_b', 'l0_ssd_dt_bias', 'l0_ssd_a_log',
            'l0_ssd_d', 'l0_ssd_norm', 'l0_ret_norm', 'l0_w_out', 'l0_ffn_norm', 'l0_ffn_w_in', 'l0_ffn_conv_w',
            'l0_ffn_conv_b', 'l0_ffn_w_out', 'l1_mix_norm', 'l1_w_in', 'l1_lru_conv_w', 'l1_lru_conv_b', 'l1_lru_wa',
            'l1_lru_ba', 'l1_lru_wx', 'l1_lru_bx', 'l1_lru_lambda', 'l1_w_out', 'l1_ffn_norm', 'l1_ffn_w_in',
            'l1_ffn_conv_w', 'l1_ffn_conv_b', 'l1_ffn_w_out', 'final_norm']
_IN_NAMES = ['x'] + _W_NAMES + ['loss_target'] + ['m_' + n for n in _W_NAMES] + ['v_' + n for n in _W_NAMES]
_BIG = ['l0_w_in', 'l0_w_out', 'l0_ffn_w_in', 'l0_ffn_w_out', 'l1_w_in', 'l1_w_out', 'l1_ffn_w_in', 'l1_ffn_w_out']
_BIG_COLS = ('l0_w_in', 'l0_ffn_w_in', 'l1_w_in', 'l1_ffn_w_in')
_SMALL_SHARDED = ['meta_tokens', 'l0_ssd_conv_w', 'l0_ffn_conv_w', 'l1_lru_conv_w', 'l1_ffn_conv_w']
_SMALL = [n for n in _W_NAMES if n not in _BIG]


def _pack(arrs):
    flat = []
    for a in arrs:
        v = a.reshape(-1).astype(F32)
        flat.append(jnp.pad(v, (0, (-v.shape[0]) % 128)))
    v = jnp.concatenate(flat)
    v = jnp.pad(v, (0, (-v.shape[0]) % 1024))
    return v.reshape(-1, 128)


def _unpack(buf, shapes):
    out, row = [], 0
    for sh in shapes:
        n = int(np.prod(sh))
        rows = -(-n // 128)
        out.append(buf[row:row + rows].reshape(-1)[:n].reshape(sh))
        row += rows
    return out


def kernel(x, meta_tokens, l0_mix_norm, l0_w_in, l0_ssd_conv_w, l0_ssd_conv_b, l0_ssd_dt_bias, l0_ssd_a_log, l0_ssd_d, l0_ssd_norm, l0_ret_norm, l0_w_out, l0_ffn_norm, l0_ffn_w_in, l0_ffn_conv_w, l0_ffn_conv_b, l0_ffn_w_out, l1_mix_norm, l1_w_in, l1_lru_conv_w, l1_lru_conv_b, l1_lru_wa, l1_lru_ba, l1_lru_wx, l1_lru_bx, l1_lru_lambda, l1_w_out, l1_ffn_norm, l1_ffn_w_in, l1_ffn_conv_w, l1_ffn_conv_b, l1_ffn_w_out, final_norm, loss_target, m_meta_tokens, m_l0_mix_norm, m_l0_w_in, m_l0_ssd_conv_w, m_l0_ssd_conv_b, m_l0_ssd_dt_bias, m_l0_ssd_a_log, m_l0_ssd_d, m_l0_ssd_norm, m_l0_ret_norm, m_l0_w_out, m_l0_ffn_norm, m_l0_ffn_w_in, m_l0_ffn_conv_w, m_l0_ffn_conv_b, m_l0_ffn_w_out, m_l1_mix_norm, m_l1_w_in, m_l1_lru_conv_w, m_l1_lru_conv_b, m_l1_lru_wa, m_l1_lru_ba, m_l1_lru_wx, m_l1_lru_bx, m_l1_lru_lambda, m_l1_w_out, m_l1_ffn_norm, m_l1_ffn_w_in, m_l1_ffn_conv_w, m_l1_ffn_conv_b, m_l1_ffn_w_out, m_final_norm, v_meta_tokens, v_l0_mix_norm, v_l0_w_in, v_l0_ssd_conv_w, v_l0_ssd_conv_b, v_l0_ssd_dt_bias, v_l0_ssd_a_log, v_l0_ssd_d, v_l0_ssd_norm, v_l0_ret_norm, v_l0_w_out, v_l0_ffn_norm, v_l0_ffn_w_in, v_l0_ffn_conv_w, v_l0_ffn_conv_b, v_l0_ffn_w_out, v_l1_mix_norm, v_l1_w_in, v_l1_lru_conv_w, v_l1_lru_conv_b, v_l1_lru_wa, v_l1_lru_ba, v_l1_lru_wx, v_l1_lru_bx, v_l1_lru_lambda, v_l1_w_out, v_l1_ffn_norm, v_l1_ffn_w_in, v_l1_ffn_conv_w, v_l1_ffn_conv_b, v_l1_ffn_w_out, v_final_norm):
    args = (x, meta_tokens, l0_mix_norm, l0_w_in, l0_ssd_conv_w, l0_ssd_conv_b, l0_ssd_dt_bias, l0_ssd_a_log, l0_ssd_d, l0_ssd_norm, l0_ret_norm, l0_w_out, l0_ffn_norm, l0_ffn_w_in, l0_ffn_conv_w, l0_ffn_conv_b, l0_ffn_w_out, l1_mix_norm, l1_w_in, l1_lru_conv_w, l1_lru_conv_b, l1_lru_wa, l1_lru_ba, l1_lru_wx, l1_lru_bx, l1_lru_lambda, l1_w_out, l1_ffn_norm, l1_ffn_w_in, l1_ffn_conv_w, l1_ffn_conv_b, l1_ffn_w_out, final_norm, loss_target, m_meta_tokens, m_l0_mix_norm, m_l0_w_in, m_l0_ssd_conv_w, m_l0_ssd_conv_b, m_l0_ssd_dt_bias, m_l0_ssd_a_log, m_l0_ssd_d, m_l0_ssd_norm, m_l0_ret_norm, m_l0_w_out, m_l0_ffn_norm, m_l0_ffn_w_in, m_l0_ffn_conv_w, m_l0_ffn_conv_b, m_l0_ffn_w_out, m_l1_mix_norm, m_l1_w_in, m_l1_lru_conv_w, m_l1_lru_conv_b, m_l1_lru_wa, m_l1_lru_ba, m_l1_lru_wx, m_l1_lru_bx, m_l1_lru_lambda, m_l1_w_out, m_l1_ffn_norm, m_l1_ffn_w_in, m_l1_ffn_conv_w, m_l1_ffn_conv_b, m_l1_ffn_w_out, m_final_norm, v_meta_tokens, v_l0_mix_norm, v_l0_w_in, v_l0_ssd_conv_w, v_l0_ssd_conv_b, v_l0_ssd_dt_bias, v_l0_ssd_a_log, v_l0_ssd_d, v_l0_ssd_norm, v_l0_ret_norm, v_l0_w_out, v_l0_ffn_norm, v_l0_ffn_w_in, v_l0_ffn_conv_w, v_l0_ffn_conv_b, v_l0_ffn_w_out, v_l1_mix_norm, v_l1_w_in, v_l1_lru_conv_w, v_l1_lru_conv_b, v_l1_lru_wa, v_l1_lru_ba, v_l1_lru_wx, v_l1_lru_bx, v_l1_lru_lambda, v_l1_w_out, v_l1_ffn_norm, v_l1_ffn_w_in, v_l1_ffn_conv_w, v_l1_ffn_conv_b, v_l1_ffn_w_out, v_final_norm)
    p = dict(zip(_IN_NAMES, args))
    B, seq, _ = x.shape
    nch = (seq + CH) // CH
    Pn = nch * CH
    R = B * Pn
    chip = 2 * lax.axis_index("x") + lax.axis_index("y")
    row2 = lambda v: v.reshape(1, -1)
    pad128 = lambda v: jnp.pad(v, (0, 128 - v.shape[0])).reshape(1, 128)

    small_shapes = [p[n].shape for n in _SMALL_SHARDED]
    halved = lambda w: w.astype(_MXU).reshape(2, w.shape[0] // 2, w.shape[1])
    mine = {n: halved(p[n]) for n in _BIG}
    mine_small = _pack([p[n] for n in _SMALL_SHARDED])
    W = {}

    def set_weight(n, g):
        g = _fill_own(g, mine[n], chip)
        g = g.reshape(4, -1, g.shape[3])
        W[n] = jnp.concatenate([g[k] for k in range(4)], axis=1) if n in _BIG_COLS else g.reshape(-1, g.shape[2])

    def gather_on(*names):
        return ("gather", [mine[n] for n in names])

    def take_weights(names, got):
        for n, g in zip(names, got):
            set_weight(n, g)

    gathered = _gather_shards([mine['l0_w_in']], mine_small)
    set_weight('l0_w_in', gathered[0])
    g_small = _fill_own(gathered[-1], mine_small, chip)
    per_chip = [_unpack(g_small[k], small_shapes) for k in range(4)]
    for i, n in enumerate(_SMALL_SHARDED):
        W[n] = jnp.concatenate([per_chip[k][i] for k in range(4)], axis=1)
    w0 = W['l0_w_in']
    w0_main = jnp.concatenate([w0[:, 3088:], w0[:, :3072]], axis=1)
    w0_dt = jnp.pad(w0[:, 3072:3088], ((0, 0), (0, 112)))
    cos, sin = _rope_tables(nch)

    meta = jnp.broadcast_to(W['meta_tokens'][None], (B, N_META, D))
    h0 = jnp.concatenate([jnp.zeros((B, PAD, D), F32), meta, x], axis=1).reshape(R, D)
    n0, n0t = _rmsnorm_fwd(h0, row2(p['l0_mix_norm']), "norm_l0_mix")
    u0 = _mm(n0, w0_main, "nn", F32, "l0_in_proj")
    udt = _mm(n0, w0_dt, "nn", F32, "l0_dt_proj")
    a_log, d_skip, dt_bias = pad128(p['l0_ssd_a_log']), pad128(p['l0_ssd_d']), pad128(p['l0_ssd_dt_bias'])
    ssd_cb = row2(p['l0_ssd_conv_b'])
    act, dt, dtt, *got = _ssd_prep(u0, udt, W['l0_ssd_conv_w'], ssd_cb, dt_bias, B, nch, rider=gather_on('l0_w_out'))
    take_weights(['l0_w_out'], got)
    ycat0, ypre, hin, *got = _ssd_fwd(act, u0, dt, dtt, a_log, d_skip, row2(p['l0_ssd_norm']), B, nch,
                                      rider=gather_on('l0_ffn_w_in'))
    take_weights(['l0_ffn_w_in'], got)
    ycat0, opre, rin, *got = _ret_fwd(u0, ycat0, cos, sin, row2(p['l0_ret_norm']), B, nch,
                                      rider=gather_on('l0_ffn_w_out'))
    take_weights(['l0_ffn_w_out'], got)
    h1 = _mm(ycat0, W['l0_w_out'], "nn", F32, "l0_out_proj", add=h0)
    n1, n1t = _rmsnorm_fwd(h1, row2(p['l0_ffn_norm']), "norm_l0_ffn")
    uf0 = _mm(n1, W['l0_ffn_w_in'], "nn", F32, "l0_ffn_in")
    f0_cb = row2(p['l0_ffn_conv_b'])
    a0, *got = _ffn_act_fwd(uf0, W['l0_ffn_conv_w'], f0_cb, B, nch, rider=gather_on('l1_w_in'))
    take_weights(['l1_w_in'], got)
    h2 = _mm(a0, W['l0_ffn_w_out'], "nn", F32, "l0_ffn_out", add=h1)
    n2, n2t = _rmsnorm_fwd(h2, row2(p['l1_mix_norm']), "norm_l1_mix")
    u1 = _mm(n2, W['l1_w_in'], "nn", F32, "l1_in_proj")
    lru = (W['l1_lru_conv_w'], row2(p['l1_lru_conv_b']), p['l1_lru_wa'], row2(p['l1_lru_ba']), p['l1_lru_wx'],
           row2(p['l1_lru_bx']), row2(p['l1_lru_lambda']))
    later = ['l1_w_out', 'l1_ffn_w_in', 'l1_ffn_w_out']
    ycat1, *got = _sb_fwd(u1, B, nch, rider=gather_on(*later))
    take_weights(later, got)
    ycat1, hs = _lru_fwd(u1, ycat1, *lru, B, nch)
    h3 = _mm(ycat1, W['l1_w_out'], "nn", F32, "l1_out_proj", add=h2)
    n3, n3t = _rmsnorm_fwd(h3, row2(p['l1_ffn_norm']), "norm_l1_ffn")
    uf1 = _mm(n3, W['l1_ffn_w_in'], "nn", F32, "l1_ffn_in")
    f1_cb = row2(p['l1_ffn_conv_b'])
    a1, = _ffn_act_fwd(uf1, W['l1_ffn_conv_w'], f1_cb, B, nch)
    h4 = _mm(a1, W['l1_ffn_w_out'], "nn", F32, "l1_ffn_out", add=h3)
    dh4, lossp, dgf = _head(h4, row2(p['final_norm']), p['loss_target'].reshape(B * seq, D), B, nch)
    loss = lax.psum(jnp.sum(lossp[:, 0, 0]), ("x", "y", "c"))

    G = {'final_norm': dgf[:, 0].sum(0)}

    core = lax.axis_index("c").reshape(1).astype(jnp.int32)

    def col_shards(pieces):
        edges = np.cumsum([0] + [q.shape[1] for q in pieces])
        cs = int(edges[-1]) // 4
        shards = []
        for k in range(4):
            lo, hi = k * cs, (k + 1) * cs
            cut = [q[:, max(lo - e0, 0):min(hi - e0, q.shape[1])]
                   for q, e0, e1 in zip(pieces, edges[:-1], edges[1:]) if e0 < hi and e1 > lo]
            shards.append(cut[0] if len(cut) == 1 else jnp.concatenate(cut, axis=1))
        return jnp.stack(shards)

    def chip_sums(names, tag):
        stacked = [G[n] if n in _BIG_COLS else G[n].reshape(4, G[n].shape[0] // 4, G[n].shape[1]) for n in names]
        theirs = _swap_halves(stacked, "swap_halves_" + tag)
        return {n: _chip_sum(g, t, core, "chip_sum_" + n) for n, g, t in zip(names, stacked, theirs)}

    parts = {}

    def scatter_on(names, tag):
        sums = chip_sums(names, tag)
        return sums, ("scatter", [sums[n] for n in names])

    def take_parts(names, sums, got):
        for n, g in zip(names, got):
            parts[n] = _fill_own(g, lax.dynamic_index_in_dim(sums[n], chip, 0, keepdims=False), chip)

    def ffn_bwd(layer, dh_out, h_in, nt_in, uf, a_act, cb, rider=None):
        pre = f"l{layer}_"
        w_in, w_out, cw = W[pre + 'ffn_w_in'], W[pre + 'ffn_w_out'], W[pre + 'ffn_conv_w']
        da = _mm(dh_out, w_out, "nt", F32, pre + "ffn_out_dgrad")
        G[pre + 'ffn_w_out'] = _mm(a_act, dh_out, "tn", F32, pre + "ffn_out_wgrad")
        dug, duu, dwg, dwu, *rode = _ffn_act_bwd(da, uf, cw, cb, nch, pre + "ffn_act_bwd", rider=rider)
        G[pre + 'ffn_conv_w'] = jnp.concatenate([dwg[:3], dwu[:3]], axis=1)
        G[pre + 'ffn_conv_b'] = jnp.concatenate([dwg[7], dwu[7]])
        dn = _mm(dug, w_in, "nt", F32, pre + "ffn_in_dgrad_g")
        dn = _mm(duu, w_in, "nt", F32, pre + "ffn_in_dgrad_u", add=dn, b_off=FFN)
        G[pre + 'ffn_w_in'] = col_shards([_mm(nt_in, dug, "nn", F32, pre + "ffn_in_wgrad_g"),
                                          _mm(nt_in, duu, "nn", F32, pre + "ffn_in_wgrad_u")])
        dh_in, dg = _rmsnorm_bwd(h_in, row2(p[pre + 'ffn_norm']), dn, dh_out, nch, pre + "ffn_norm_bwd")
        G[pre + 'ffn_norm'] = dg[0]
        return dh_in, rode

    dh3, _ = ffn_bwd(1, dh4, h3, n3t, uf1, a1, f1_cb)
    dy1 = _mm(dh3, W['l1_w_out'], "nt", F32, "l1_out_dgrad")
    G['l1_w_out'] = _mm(ycat1, dh3, "tn", F32, "l1_out_wgrad")
    done = ['l1_ffn_w_in', 'l1_ffn_w_out', 'l1_w_out']
    sums, rider = scatter_on(done, "a")
    dq, dkt, dvt, *got = _sb_bwd(dy1, u1, B, nch, rider=rider)
    dk, dv = dkt.T, dvt.T
    take_parts(done, sums, got)
    dgate, dxc, pgl, dwa, dwx = _lru_bwd(dy1, u1, hs, *lru, B, nch)
    dxr, dcw = _conv_bwd(dxc, u1, 4096, W['l1_lru_conv_w'], 4, "l1_lru_conv_bwd")
    pgl = pgl.sum(0)
    G['l1_lru_ba'], G['l1_lru_bx'], G['l1_lru_lambda'] = pgl[0], pgl[1], pgl[2]
    G['l1_lru_wa'], G['l1_lru_wx'] = dwa.sum(0), dwx.sum(0)
    G['l1_lru_conv_w'], G['l1_lru_conv_b'] = dcw[:4], dcw[7]
    du1 = jnp.concatenate([piece.astype(_MXU) for piece in (dq, dk, dv, dgate, dxr)], axis=1)
    dn = _mm(du1, W['l1_w_in'], "nt", F32, "l1_in_dgrad")
    G['l1_w_in'] = col_shards([_mm(n2t, du1, "nn", F32, "l1_in_wgrad")])
    dh2, dg = _rmsnorm_bwd(h2, row2(p['l1_mix_norm']), dn, dh3, nch, "l1_mix_norm_bwd")
    G['l1_mix_norm'] = dg[0]

    dh1, _ = ffn_bwd(0, dh2, h1, n1t, uf0, a0, f0_cb)
    dy0 = _mm(dh1, W['l0_w_out'], "nt", F32, "l0_out_dgrad")
    G['l0_w_out'] = _mm(ycat0, dh1, "tn", F32, "l0_out_wgrad")
    done = ['l1_w_in', 'l0_ffn_w_in', 'l0_ffn_w_out', 'l0_w_out']
    sums, rider = scatter_on(done, "b")
    dz, dxs, dbm, dcm, ddt4, pgs, *got = _ssd_bwd(dy0, ypre, u0, act, dt, dtt, hin, a_log, d_skip,
                                                  row2(p['l0_ssd_norm']), B, nch, rider=rider)
    take_parts(done, sums, got)
    dpre, ddtr, pgd = _ssd_prep_bwd(dxs, dbm, dcm, ddt4, u0, udt, W['l0_ssd_conv_w'], ssd_cb, dt_bias, B, nch)
    dxbc, dcw0 = _conv_bwd(dpre, u0, U0_XBC, W['l0_ssd_conv_w'], 4, "l0_ssd_conv_bwd")
    dqkvg, pgr = _ret_bwd(dy0, u0, opre, rin, cos, sin, row2(p['l0_ret_norm']), B, nch)
    pgs = pgs.sum(0)
    G['l0_ssd_norm'] = pgs[:, 0, :].reshape(-1)
    G['l0_ssd_d'] = pgs[:, 1, :128].sum(0)[:SSD_HEADS]
    G['l0_ssd_a_log'] = pgs[:, 2, :128].sum(0)[:SSD_HEADS]
    G['l0_ssd_dt_bias'] = pgd.sum(0)[0, :SSD_HEADS]
    G['l0_ssd_conv_w'], G['l0_ssd_conv_b'] = dcw0[:4], dcw0[7]
    G['l0_ret_norm'] = pgr.sum(0)[0]
    dn = _mm(dqkvg, w0_main, "nt", F32, "l0_in_dgrad_qkvg")
    dn = _mm(dz, w0_main, "nt", F32, "l0_in_dgrad_z", add=dn, b_off=U0_Z)
    dn = _mm(dxbc, w0_main, "nt", F32, "l0_in_dgrad_xbc", add=dn, b_off=U0_XBC)
    dn = _mm(ddtr, w0_dt, "nt", F32, "l0_in_dgrad_dt", add=dn)
    G['l0_w_in'] = col_shards([
        _mm(n0t, dz, "nn", F32, "l0_in_wgrad_z"), _mm(n0t, dxbc, "nn", F32, "l0_in_wgrad_xbc"),
        _mm(n0t, ddtr, "nn", F32, "l0_in_wgrad_dt")[:, :SSD_HEADS], _mm(n0t, dqkvg, "nn", F32, "l0_in_wgrad_qkvg")])
    dh0, dg = _rmsnorm_bwd(h0, row2(p['l0_mix_norm']), dn, dh1, nch, "l0_mix_norm_bwd")
    G['l0_mix_norm'] = dg[0]
    dh0 = dh0.reshape(B, Pn, D)
    grad_x = dh0[:, CH:]
    G['meta_tokens'] = dh0[:, PAD:CH].sum(0)

    sums = chip_sums(['l0_w_in'], "d")
    parts['l0_w_in'], = _scatter_sums([sums['l0_w_in']])
    reds = [_sum_chips(parts[n], "sum_chips_" + n) for n in _BIG]
    grads = {}
    for n, own, other in zip(_BIG, reds, _join_halves(reds)):
        both = jnp.where(core[0] == 0, jnp.stack([own, other]), jnp.stack([other, own]))
        grads[n] = both.reshape(-1, both.shape[2])
    small_full = _unpack(_allreduce_small(_pack([G[n] for n in _SMALL])), [G[n].shape for n in _SMALL])
    for n, g in zip(_SMALL, small_full):
        if n in _SMALL_SHARDED:
            cs = g.shape[1] // 4
            g = lax.dynamic_slice_in_dim(g, chip * cs, cs, axis=1)
        grads[n] = g.reshape(p[n].shape)

    delta, new_m, new_v = {}, {}, {}
    for n in _BIG:
        delta[n], new_m[n], new_v[n] = _adamw(p[n], grads[n], p['m_' + n], p['v_' + n], "adamw_" + n)
    shapes = [p[n].shape for n in _SMALL]
    outs = _adamw(_pack([p[n] for n in _SMALL]), _pack([grads[n] for n in _SMALL]), _pack([p['m_' + n] for n in _SMALL]),
                  _pack([p['v_' + n] for n in _SMALL]), "adamw_small")
    for dst, buf in zip((delta, new_m, new_v), outs):
        for n, a in zip(_SMALL, _unpack(buf, shapes)):
            dst[n] = a
    return (loss, grad_x, *[grads[n] for n in _W_NAMES], *[delta[n] for n in _W_NAMES],
            *[new_m[n] for n in _W_NAMES], *[new_v[n] for n in _W_NAMES])
```

```python
import math

import numpy as np
import jax
import jax.numpy as jnp
from jax import lax
from jax.experimental import pallas as pl
from jax.experimental.pallas import tpu as pltpu

F32 = jnp.float32
BF16 = jnp.bfloat16
_MXU = jnp.bfloat16

D = 1024
CH = 128
N_META = 16
PAD = CH - N_META
EPS = 1e-6

SSD_HEADS = 16
SSD_HD = 64
SSD_GROUPS = 4
RET_HEADS = 4
RET_DK = 256
SB_HEADS = 16
SB_HD = 64
LRU_BLOCKS = 8
LRU_C = 8.0
FFN = 2816
U0_Z = 4096
U0_XBC = 5120

VMEM_LIMIT = 56 * 1024 * 1024


def _cparams(sem):
    return pltpu.CompilerParams(dimension_semantics=sem, vmem_limit_bytes=VMEM_LIMIT)


def _dot(a, b, dims=((1,), (0,))):
    return lax.dot_general(a.astype(_MXU), b.astype(_MXU), (dims, ((), ())), preferred_element_type=F32)


def _dot_nt(a, b):
    return _dot(a, b, ((1,), (1,)))


def _dot_tn(a, b):
    return _dot(a.T, b)


def _dot_exact(a, b):
    return lax.dot_general(a, b, (((1,), (0,)), ((), ())), preferred_element_type=F32,
                           precision=lax.Precision.HIGHEST)


def _dot_split(x, m01):
    hi = x.astype(BF16)
    lo = (x - hi.astype(F32)).astype(BF16)
    m = m01.astype(BF16)
    return jnp.dot(hi, m, preferred_element_type=F32) + jnp.dot(lo, m, preferred_element_type=F32)


def _sigmoid(x):
    return 0.5 * jnp.tanh(0.5 * x) + 0.5


def _softplus(x):
    return jnp.maximum(x, 0.0) + jnp.log1p(jnp.exp(-jnp.abs(x)))


def _silu(x):
    return x * _sigmoid(x)


def _dsilu(x):
    s = _sigmoid(x)
    return s * (1.0 + x * (1.0 - s))


_GELU_C = math.sqrt(2.0 / math.pi)


def _gelu(x):
    return 0.5 * x * (1.0 + jnp.tanh(_GELU_C * (x + 0.044715 * x * x * x)))


def _dgelu(x):
    t = jnp.tanh(_GELU_C * (x + 0.044715 * x * x * x))
    return 0.5 * (1.0 + t) + 0.5 * x * (1.0 - t * t) * _GELU_C * (1.0 + 3.0 * 0.044715 * x * x)


def _row_ids(n, cols=1):
    return lax.broadcasted_iota(jnp.int32, (n, cols), 0)


def _lane_ids(rows, n):
    return lax.broadcasted_iota(jnp.int32, (rows, n), 1)


def _real_rows(chunk):
    return chunk * CH + _row_ids(CH) >= PAD


def _shift_down(prev8, cur, s):
    cat = jnp.concatenate([prev8, cur], axis=0)
    return pltpu.roll(cat, s, axis=0)[8:]


def _shift_up(cur, next8, s):
    n = cur.shape[0]
    cat = jnp.concatenate([cur, next8], axis=0)
    return pltpu.roll(cat, n + 8 - s, axis=0)[:n]


def _conv_pre(prev8, cur, w_ref, b_ref, K):
    acc = cur * w_ref[K - 1:K, :] + b_ref[...]
    for s in range(1, K):
        acc = acc + _shift_down(prev8, cur, s) * w_ref[K - 1 - s:K - s, :]
    return acc


def _prev8_map(nch, col):
    return lambda b, c: (jnp.maximum((b * nch + c) * (CH // 8) - 1, 0), col)


def _matmul(a, b, mode, out_dtype, tm, tn, tk, name, add=None, b_off=0):
    if mode == "nn":
        (M, K), (_, N) = a.shape, b.shape
    elif mode == "nt":
        (M, K), N = a.shape, b.shape[0]
    else:
        (K, M), (_, N) = a.shape, b.shape
    tm, tn, tk = min(tm, M), min(tn, N), min(tk, K)
    assert M % tm == 0 and N % tn == 0 and K % tk == 0 and b_off % tk == 0, (name, M, N, K, tm, tn, tk)
    koff = b_off // tk
    nk = K // tk
    dims = {"nn": ((1,), (0,)), "nt": ((1,), (1,)), "tn": ((0,), (0,))}[mode]
    if mode == "tn":
        a_spec = pl.BlockSpec((tk, tm), lambda i, j, k: (k, i))
    else:
        a_spec = pl.BlockSpec((tm, tk), lambda i, j, k: (i, k))
    if mode == "nt":
        b_spec = pl.BlockSpec((tn, tk), lambda i, j, k: (j, k + koff))
    else:
        b_spec = pl.BlockSpec((tk, tn), lambda i, j, k: (k, j))
    o_spec = pl.BlockSpec((tm, tn), lambda i, j, k: (i, j))
    has_add = add is not None

    def body(a_ref, b_ref, *rest):
        if has_add:
            add_ref, o_ref, acc = rest
        else:
            o_ref, acc = rest
        k = pl.program_id(2)

        @pl.when(k == 0)
        def _():
            acc[...] = jnp.zeros_like(acc)

        acc[...] += _dot(a_ref[...], b_ref[...], dims)

        @pl.when(k == nk - 1)
        def _():
            r = acc[...]
            if has_add:
                r = r + add_ref[...].astype(F32)
            o_ref[...] = r.astype(out_dtype)

    in_specs = [a_spec, b_spec] + ([o_spec] if has_add else [])
    args = (a, b) + ((add,) if has_add else ())
    return pl.pallas_call(
        body, name=name, grid=(M // tm, N // tn, nk),
        in_specs=in_specs, out_specs=o_spec,
        out_shape=jax.ShapeDtypeStruct((M, N), out_dtype),
        scratch_shapes=[pltpu.VMEM((tm, tn), F32)],
        compiler_params=_cparams(("parallel", "parallel", "arbitrary")),
    )(*args)


def _tile(n, prefs):
    for t in prefs:
        if n % t == 0:
            return t
    return n


def _mm(a, b, mode, out_dtype, name, add=None, b_off=0):
    if mode == "tn":
        K, M = a.shape
        N = b.shape[1]
        tm, tn, tk = _tile(M, (1024, 1408, 512, 256, 128)), _tile(N, (1024, 1408, 512, 256, 128)), _tile(K, (2176, 384, 256, 128))
    else:
        M, K = a.shape
        N = b.shape[1] if mode == "nn" else b.shape[0]
        tm = _tile(M, (1088, 1024, 768, 512, 384, 256, 128))
        tn = _tile(N, (1024, 1408, 512, 256, 128))
        tk = _tile(K, (2176, 1024, 1408, 512, 256, 128))
    return _matmul(a, b, mode, out_dtype, tm, tn, tk, name, add=add, b_off=b_off)


def _rmsnorm_fwd(h, g, name):
    R = h.shape[0]
    tr = 2 * CH

    def body(h_ref, g_ref, o_ref, ot_ref):
        x = h_ref[...]
        r = lax.rsqrt(jnp.mean(x * x, axis=-1, keepdims=True) + EPS)
        y = x * r * g_ref[...]
        o_ref[...] = y.astype(o_ref.dtype)
        ot_ref[...] = y.T.astype(ot_ref.dtype)

    return pl.pallas_call(
        body, name=name, grid=(R // tr,),
        in_specs=[pl.BlockSpec((tr, D), lambda i: (i, 0)), pl.BlockSpec((1, D), lambda i: (0, 0))],
        out_specs=[pl.BlockSpec((tr, D), lambda i: (i, 0)), pl.BlockSpec((D, tr), lambda i: (0, i))],
        out_shape=[jax.ShapeDtypeStruct((R, D), _MXU), jax.ShapeDtypeStruct((D, R), _MXU)],
        compiler_params=_cparams(("parallel",)),
    )(h, g)


def _rmsnorm_bwd(h, g, dn, dres, nch, name):
    R = h.shape[0]
    per = 4
    tr = nch * CH // per

    def body(h_ref, g_ref, dn_ref, dres_ref, dh_ref, dg_ref):
        i = pl.program_id(0)
        x = h_ref[...]
        r = lax.rsqrt(jnp.mean(x * x, axis=-1, keepdims=True) + EPS)
        xhat = x * r
        dn_v = dn_ref[...]
        dx = dn_v * g_ref[...]
        dh = r * (dx - xhat * jnp.mean(dx * xhat, axis=-1, keepdims=True))
        keep = (i % per) * tr + _row_ids(tr) >= PAD
        dh_ref[...] = jnp.where(keep, dres_ref[...] + dh, 0.0)

        @pl.when(i == 0)
        def _():
            dg_ref[...] = jnp.zeros_like(dg_ref)

        dg_ref[...] += jnp.sum(dn_v * xhat, axis=0, keepdims=True)

    row = pl.BlockSpec((tr, D), lambda i: (i, 0))
    vec = pl.BlockSpec((1, D), lambda i: (0, 0))
    return pl.pallas_call(
        body, name=name, grid=(R // tr,),
        in_specs=[row, vec, row, row], out_specs=[row, vec],
        out_shape=[jax.ShapeDtypeStruct((R, D), F32), jax.ShapeDtypeStruct((1, D), F32)],
        compiler_params=_cparams(("arbitrary",)),
    )(h, g, dn, dres)


def _ssd_prep(u0, udt, conv_w, conv_b, dt_bias, B, nch, rider=None):
    R = u0.shape[0]

    def body(xs_ref, xsp_ref, bc_ref, bcp_ref, udt_ref, w0_ref, w1_ref, b0_ref, b1_ref, dtb_ref,
             act_ref, dt_ref, dtt_ref):
        keep = _real_rows(pl.program_id(1))
        a0 = _silu(_conv_pre(xsp_ref[...], xs_ref[...], w0_ref, b0_ref, 4))
        a1 = _silu(_conv_pre(bcp_ref[...], bc_ref[...], w1_ref, b1_ref, 4))
        act_ref[:, :1024] = jnp.where(keep, a0, 0.0)
        act_ref[:, 1024:] = jnp.where(keep, a1, 0.0)
        ok = jnp.logical_and(keep, _lane_ids(1, 128) < SSD_HEADS)
        dt = jnp.where(ok, _softplus(udt_ref[...] + dtb_ref[...]), 0.0)
        dt_ref[...] = dt
        dtt_ref[...] = dt.T

    row = lambda col: pl.BlockSpec((CH, 1024), lambda b, c: (b * nch + c, col))
    prev = lambda col: pl.BlockSpec((8, 1024), _prev8_map(nch, col))
    kw = dict(
        grid=(B, nch),
        in_specs=[row(5), prev(5), row(6), prev(6),
                  pl.BlockSpec((CH, 128), lambda b, c: (b * nch + c, 0)),
                  pl.BlockSpec((4, 1024), lambda b, c: (0, 0)), pl.BlockSpec((4, 1024), lambda b, c: (0, 1)),
                  pl.BlockSpec((1, 1024), lambda b, c: (0, 0)), pl.BlockSpec((1, 1024), lambda b, c: (0, 1)),
                  pl.BlockSpec((1, 128), lambda b, c: (0, 0))],
        out_specs=[pl.BlockSpec((CH, 2048), lambda b, c: (b * nch + c, 0)),
                   pl.BlockSpec((CH, 128), lambda b, c: (b * nch + c, 0)),
                   pl.BlockSpec((128, CH), lambda b, c: (0, b * nch + c))],
        out_shape=[jax.ShapeDtypeStruct((R, 2048), F32), jax.ShapeDtypeStruct((R, 128), F32),
                   jax.ShapeDtypeStruct((128, R), F32)])
    return _call(body, "ssd_prep", ("arbitrary", "arbitrary"), kw,
                 (u0, u0, u0, u0, udt, conv_w, conv_w, conv_b, conv_b, dt_bias), rider)


def _ssd_head_terms(h, a_vec, dt_v, dtt_v, dsk_v):
    lane = _lane_ids(1, 128)
    sub = _row_ids(128)
    r = _row_ids(CH, CH)
    cidx = _lane_ids(CH, CH)
    a_h = jnp.sum(jnp.where(lane == h, a_vec, 0.0), axis=1, keepdims=True)
    dt_col = jnp.sum(jnp.where(lane == h, dt_v, 0.0), axis=1, keepdims=True)
    dt_row = jnp.sum(jnp.where(sub == h, dtt_v, 0.0), axis=0, keepdims=True)
    cs_col = jnp.sum(jnp.where(r >= cidx, dt_row * a_h, 0.0), axis=1, keepdims=True)
    cs_row = jnp.sum(jnp.where(r <= cidx, dt_col * a_h, 0.0), axis=0, keepdims=True)
    tot = jnp.sum(dt_col * a_h, axis=0, keepdims=True)
    dsk = jnp.sum(jnp.where(lane == h, dsk_v, 0.0), axis=1, keepdims=True)
    return a_h, dt_col, cs_col, cs_row, tot, dsk


def _ssd_fwd(act, u0, dt, dtt, a_log, d_skip, norm_g, B, nch, rider=None):
    R = act.shape[0]

    def body(xs_ref, bm_ref, cm_ref, z_ref, dt_ref, dtt_ref, alog_ref, dsk_ref, ng_ref,
             out_ref, ypre_ref, hin_ref, H):
        g = pl.program_id(1)
        c = pl.program_id(2)

        @pl.when(c == 0)
        def _():
            H[...] = jnp.zeros_like(H)

        hin_ref[...] = H[...]
        a_vec = -jnp.exp(alog_ref[...])
        dt_v = dt_ref[...]
        dtt_v = dtt_ref[...]
        hm = _lane_ids(1, 128) < SSD_HD
        r = _row_ids(CH, CH)
        cidx = _lane_ids(CH, CH)
        Bm = bm_ref[...]
        Cm = cm_ref[...]
        CB = _dot_nt(Cm, Bm)
        ys = []
        for pair in range(2):
            cols = slice(128 * pair, 128 * pair + 128)
            xraw = xs_ref[:, cols]
            t = [_ssd_head_terms(4 * g + 2 * pair + j, a_vec, dt_v, dtt_v, dsk_ref[...]) for j in range(2)]
            sel = lambda f: jnp.where(hm, f(t[0]), f(t[1]))
            dtp = sel(lambda q: q[1])
            Ep = sel(lambda q: jnp.exp(q[2]))
            Wp = sel(lambda q: jnp.exp(q[4] - q[2]))
            etot = sel(lambda q: jnp.exp(q[4]))
            dsk = sel(lambda q: q[5])
            X = xraw * dtp
            ydiag = jnp.zeros((CH, 128), F32)
            for j in range(2):
                Lm = jnp.where(r >= cidx, jnp.exp(t[j][2] - t[j][3]), 0.0)
                Xh = jnp.where(hm if j == 0 else jnp.logical_not(hm), X, 0.0)
                ydiag = ydiag + _dot(CB * Lm, Xh)
            Hp = H[:, cols]
            yoff = Ep * _dot(Cm, Hp)
            S = _dot(Bm.T, X * Wp)
            H[:, cols] = etot * Hp + S
            ys.append(ydiag + yoff + xraw * dsk)
        y = jnp.concatenate(ys, axis=1)
        ypre_ref[...] = y
        yg = y * _silu(z_ref[...])
        rr = lax.rsqrt(jnp.mean(yg * yg, axis=-1, keepdims=True) + EPS)
        out_ref[...] = jnp.where(_real_rows(c), yg * rr * ng_ref[...], 0.0).astype(out_ref.dtype)

    rowb = lambda w, colf: pl.BlockSpec((CH, w), lambda b, g, c: (b * nch + c, colf(g)))
    vec = pl.BlockSpec((1, 128), lambda b, g, c: (0, 0))
    kw = dict(
        grid=(B, SSD_GROUPS, nch),
        in_specs=[rowb(256, lambda g: g), rowb(128, lambda g: 8 + g), rowb(128, lambda g: 12 + g),
                  rowb(256, lambda g: 16 + g), rowb(128, lambda g: 0),
                  pl.BlockSpec((128, CH), lambda b, g, c: (0, b * nch + c)),
                  vec, vec, pl.BlockSpec((1, 256), lambda b, g, c: (0, g))],
        out_specs=[rowb(256, lambda g: g), rowb(256, lambda g: g),
                   pl.BlockSpec((None, None, None, 128, 256), lambda b, g, c: (b, g, c, 0, 0))],
        out_shape=[jax.ShapeDtypeStruct((R, 2048), _MXU), jax.ShapeDtypeStruct((R, 1024), F32),
                   jax.ShapeDtypeStruct((B, SSD_GROUPS, nch, 128, 256), F32)],
        scratch_shapes=[pltpu.VMEM((128, 256), F32)])
    return _call(body, "ssd_fwd", ("arbitrary", "arbitrary", "arbitrary"), kw,
                 (act, act, act, u0, dt, dtt, a_log, d_skip, norm_g), rider)


def _ssd_bwd(dycat, ypre, u0, act, dt, dtt, hin, a_log, d_skip, norm_g, B, nch, rider=None):
    R = act.shape[0]

    def body(dy_ref, ypre_ref, z_ref, xs_ref, bm_ref, cm_ref, dt_ref, dtt_ref, hin_ref, alog_ref, dsk_ref, ng_ref,
             dz_ref, dxs_ref, db_ref, dc_ref, ddt_ref, pg_ref, dH):
        g = pl.program_id(1)
        c = nch - 1 - pl.program_id(2)

        @pl.when(pl.program_id(2) == 0)
        def _():
            dH[...] = jnp.zeros_like(dH)
            pg_ref[...] = jnp.zeros_like(pg_ref)

        z = z_ref[...]
        y = ypre_ref[...]
        ng = ng_ref[...]
        dout = jnp.where(_real_rows(c), dy_ref[...], 0.0)
        sz = _sigmoid(z)
        yg = y * z * sz
        rr = lax.rsqrt(jnp.mean(yg * yg, axis=-1, keepdims=True) + EPS)
        nrm = yg * rr
        pg_ref[0:1, :] += jnp.sum(dout * nrm, axis=0, keepdims=True)
        dn = dout * ng
        dyg = rr * (dn - nrm * jnp.mean(dn * nrm, axis=-1, keepdims=True))
        dy = dyg * z * sz
        dz_ref[...] = (dyg * y * (sz * (1.0 + z * (1.0 - sz)))).astype(dz_ref.dtype)

        a_vec = -jnp.exp(alog_ref[...])
        dt_v = dt_ref[...]
        dtt_v = dtt_ref[...]
        lane = _lane_ids(1, 128)
        hm = lane < SSD_HD
        r = _row_ids(CH, CH)
        cidx = _lane_ids(CH, CH)
        last = _row_ids(CH) == CH - 1
        Bm = bm_ref[...]
        Cm = cm_ref[...]
        CB = _dot_nt(Cm, Bm)
        CBT = _dot_nt(Bm, Cm)
        dB = jnp.zeros((CH, 128), F32)
        dC = jnp.zeros((CH, 128), F32)
        dcs_all = jnp.zeros((CH, 128), F32)
        dtx_all = jnp.zeros((CH, 128), F32)
        dd_row = jnp.zeros((1, 128), F32)
        dxs = []
        for pair in range(2):
            cols = slice(128 * pair, 128 * pair + 128)
            xraw = xs_ref[:, cols]
            dyp = dy[:, cols]
            heads = [4 * g + 2 * pair + j for j in range(2)]
            t = [_ssd_head_terms(heads[j], a_vec, dt_v, dtt_v, dsk_ref[...]) for j in range(2)]
            sel = lambda f: jnp.where(hm, f(t[0]), f(t[1]))
            hsum = lambda v, j: jnp.sum(jnp.where(hm if j == 0 else jnp.logical_not(hm), v, 0.0), axis=1, keepdims=True)
            dtp = sel(lambda q: q[1])
            Ep = sel(lambda q: jnp.exp(q[2]))
            Wp = sel(lambda q: jnp.exp(q[4] - q[2]))
            etot = sel(lambda q: jnp.exp(q[4]))
            dsk = sel(lambda q: q[5])
            X = xraw * dtp
            Hp = hin_ref[:, cols]
            dHn = dH[:, cols]
            dskip = jnp.sum(dyp * xraw, axis=0, keepdims=True)
            yoff = Ep * _dot(Cm, Hp)
            dE = dyp * yoff
            dC = dC + _dot_nt(dyp * Ep, Hp)
            dH[:, cols] = etot * dHn + _dot(Cm.T, dyp * Ep)
            BdS = _dot(Bm, dHn)
            dX = Wp * BdS
            ew = X * BdS * Wp
            dB = dB + _dot_nt(X * Wp, dHn)
            hh = jnp.sum(dHn * Hp, axis=0, keepdims=True) * etot
            for j in range(2):
                hmask = hm if j == 0 else jnp.logical_not(hm)
                cs_col, cs_row = t[j][2], t[j][3]
                Lm = jnp.where(r >= cidx, jnp.exp(cs_col - cs_row), 0.0)
                LmT = jnp.where(cidx >= r, jnp.exp(cs_row - cs_col), 0.0)
                dyh = jnp.where(hmask, dyp, 0.0)
                Xh = jnp.where(hmask, X, 0.0)
                dM = _dot_nt(dyh, Xh)
                dMT = _dot_nt(Xh, dyh)
                M = CB * Lm
                MT = CBT * LmT
                dX = dX + _dot(MT, dyh)
                dC = dC + _dot(dM * Lm, Bm)
                dB = dB + _dot(dMT * LmT, Cm)
                g_rows = jnp.sum(dM * M, axis=1, keepdims=True)
                g_cols = jnp.sum(dMT * MT, axis=1, keepdims=True)
                dtot = (jnp.sum(hsum(ew, j), axis=0, keepdims=True)
                        + jnp.sum(jnp.where(hmask, hh, 0.0), axis=1, keepdims=True))
                dcs = g_rows - g_cols + hsum(dE, j) - hsum(ew, j) + jnp.where(last, dtot, 0.0)
                dcs_all = dcs_all + jnp.where(lane == heads[j], dcs, 0.0)
                dtx_all = dtx_all + jnp.where(lane == heads[j], hsum(dX * xraw, j), 0.0)
                dd_row = dd_row + jnp.where(lane == heads[j],
                                            jnp.sum(jnp.where(hmask, dskip, 0.0), axis=1, keepdims=True), 0.0)
            dxs.append(dX * dtp + dyp * dsk)
        dxs_ref[...] = jnp.concatenate(dxs, axis=1)
        db_ref[...] = dB
        dc_ref[...] = dC
        dadt = _dot_exact(jnp.where(cidx >= r, 1.0, 0.0), dcs_all)
        ddt_ref[...] = dadt * a_vec + dtx_all
        pg_ref[1:2, 0:128] += dd_row
        pg_ref[2:3, 0:128] += jnp.sum(dadt * dt_v, axis=0, keepdims=True) * a_vec

    rowb = lambda w, colf: pl.BlockSpec((CH, w), lambda b, g, c: (b * nch + nch - 1 - c, colf(g)))
    vec = pl.BlockSpec((1, 128), lambda b, g, c: (0, 0))
    kw = dict(
        grid=(B, SSD_GROUPS, nch),
        in_specs=[rowb(256, lambda g: g), rowb(256, lambda g: g), rowb(256, lambda g: 16 + g), rowb(256, lambda g: g),
                  rowb(128, lambda g: 8 + g), rowb(128, lambda g: 12 + g), rowb(128, lambda g: 0),
                  pl.BlockSpec((128, CH), lambda b, g, c: (0, b * nch + nch - 1 - c)),
                  pl.BlockSpec((None, None, None, 128, 256), lambda b, g, c: (b, g, nch - 1 - c, 0, 0)),
                  vec, vec, pl.BlockSpec((1, 256), lambda b, g, c: (0, g))],
        out_specs=[rowb(256, lambda g: g), rowb(256, lambda g: g), rowb(128, lambda g: g), rowb(128, lambda g: g),
                   rowb(128, lambda g: g),
                   pl.BlockSpec((None, None, 8, 256), lambda b, g, c: (b, g, 0, 0))],
        out_shape=[jax.ShapeDtypeStruct((R, 1024), _MXU), jax.ShapeDtypeStruct((R, 1024), F32),
                   jax.ShapeDtypeStruct((R, 512), F32), jax.ShapeDtypeStruct((R, 512), F32),
                   jax.ShapeDtypeStruct((R, 512), F32), jax.ShapeDtypeStruct((B, SSD_GROUPS, 8, 256), F32)],
        scratch_shapes=[pltpu.VMEM((128, 256), F32)])
    return _call(body, "ssd_bwd", ("arbitrary", "arbitrary", "arbitrary"), kw,
                 (dycat, ypre, u0, act, act, act, dt, dtt, hin, a_log, d_skip, norm_g), rider)


def _ssd_prep_bwd(dxs, dB, dC, ddt4, u0, udt, conv_w, conv_b, dt_bias, B, nch, rider=None):
    R = u0.shape[0]

    def body(dxs_ref, db_ref, dc_ref, ddt_ref, xs_ref, xsp_ref, bc_ref, bcp_ref, udt_ref, w0_ref, w1_ref, b0_ref, b1_ref,
             dtb_ref, dpre_ref, ddtr_ref, pgd_ref):
        c = pl.program_id(1)

        @pl.when(c == 0)
        def _():
            pgd_ref[...] = jnp.zeros_like(pgd_ref)

        keep = _real_rows(c)
        p0 = _conv_pre(xsp_ref[...], xs_ref[...], w0_ref, b0_ref, 4)
        p1 = _conv_pre(bcp_ref[...], bc_ref[...], w1_ref, b1_ref, 4)
        dpre_ref[:, :1024] = jnp.where(keep, dxs_ref[...] * _dsilu(p0), 0.0)
        dpre_ref[:, 1024:] = jnp.where(keep, jnp.concatenate([db_ref[...], dc_ref[...]], axis=1) * _dsilu(p1), 0.0)
        ddt = ddt_ref[:, 0:128] + ddt_ref[:, 128:256] + ddt_ref[:, 256:384] + ddt_ref[:, 384:512]
        ok = jnp.logical_and(keep, _lane_ids(1, 128) < SSD_HEADS)
        dr = jnp.where(ok, ddt * _sigmoid(udt_ref[...] + dtb_ref[...]), 0.0)
        ddtr_ref[...] = dr
        pgd_ref[0:1, :] += jnp.sum(dr, axis=0, keepdims=True)

    rw = lambda w: pl.BlockSpec((CH, w), lambda b, c: (b * nch + c, 0))
    row = lambda col: pl.BlockSpec((CH, 1024), lambda b, c: (b * nch + c, col))
    prev = lambda col: pl.BlockSpec((8, 1024), _prev8_map(nch, col))
    kw = dict(
        grid=(B, nch),
        in_specs=[rw(1024), rw(512), rw(512), rw(512), row(5), prev(5), row(6), prev(6), rw(128),
                  pl.BlockSpec((4, 1024), lambda b, c: (0, 0)), pl.BlockSpec((4, 1024), lambda b, c: (0, 1)),
                  pl.BlockSpec((1, 1024), lambda b, c: (0, 0)), pl.BlockSpec((1, 1024), lambda b, c: (0, 1)),
                  pl.BlockSpec((1, 128), lambda b, c: (0, 0))],
        out_specs=[rw(2048), rw(128), pl.BlockSpec((None, 8, 128), lambda b, c: (b, 0, 0))],
        out_shape=[jax.ShapeDtypeStruct((R, 2048), F32), jax.ShapeDtypeStruct((R, 128), F32),
                   jax.ShapeDtypeStruct((B, 8, 128), F32)])
    return _call(body, "ssd_prep_bwd", ("arbitrary", "arbitrary"), kw,
                 (dxs, dB, dC, ddt4, u0, u0, u0, u0, udt, conv_w, conv_w, conv_b, conv_b, dt_bias), rider)


def _conv_bwd(dpre, xin, xin_col, w, K, name, tc=1024):
    R, C = dpre.shape
    assert C % tc == 0 and xin_col % tc == 0
    nr = R // CH
    xoff = xin_col // tc

    def body(dp_ref, dpn_ref, x_ref, xp_ref, w_ref, din_ref, dw_ref):
        i = pl.program_id(1)

        @pl.when(i == 0)
        def _():
            dw_ref[...] = jnp.zeros_like(dw_ref)

        dp = dp_ref[...]
        nxt = dpn_ref[...] * (i < nr - 1).astype(F32)
        x = x_ref[...]
        xp = xp_ref[...]
        din = dp * w_ref[K - 1:K, :]
        dw_ref[K - 1:K, :] += jnp.sum(dp * x, axis=0, keepdims=True)
        dw_ref[7:8, :] += jnp.sum(dp, axis=0, keepdims=True)
        for s in range(1, K):
            din = din + _shift_up(dp, nxt, s) * w_ref[K - 1 - s:K - s, :]
            dw_ref[K - 1 - s:K - s, :] += jnp.sum(dp * _shift_down(xp, x, s), axis=0, keepdims=True)
        din_ref[...] = din.astype(din_ref.dtype)

    return pl.pallas_call(
        body, name=name, grid=(C // tc, nr),
        in_specs=[pl.BlockSpec((CH, tc), lambda j, i: (i, j)),
                  pl.BlockSpec((8, tc), lambda j, i: (jnp.minimum((i + 1) * (CH // 8), nr * (CH // 8) - 1), j)),
                  pl.BlockSpec((CH, tc), lambda j, i: (i, xoff + j)),
                  pl.BlockSpec((8, tc), lambda j, i: (jnp.maximum(i * (CH // 8) - 1, 0), xoff + j)),
                  pl.BlockSpec((K, tc), lambda j, i: (0, j))],
        out_specs=[pl.BlockSpec((CH, tc), lambda j, i: (i, j)),
                   pl.BlockSpec((8, tc), lambda j, i: (0, j))],
        out_shape=[jax.ShapeDtypeStruct((R, C), _MXU), jax.ShapeDtypeStruct((8, C), F32)],
        compiler_params=_cparams(("parallel", "arbitrary")),
    )(dpre, dpre, xin, xin, w)


_RET_LG = [float(v) for v in np.log1p(-np.exp2(-5.0 - np.arange(RET_HEADS, dtype=np.float32))).astype(np.float32)]
_RET_SCALE = RET_DK ** -0.5


def _rope_tables(nch):
    half = RET_DK // 2
    inv_freq = 1.0 / (10000.0 ** (jnp.arange(half, dtype=F32) / (half - 1)))
    pos = jnp.arange(nch * CH, dtype=F32) - PAD
    ang = pos[:, None] * inv_freq[None, :]
    return jnp.cos(ang), jnp.sin(ang)


def _rot(x, cos, sin):
    x1, x2 = x[:, :128], x[:, 128:]
    return jnp.concatenate([x1 * cos - x2 * sin, x1 * sin + x2 * cos], axis=1)


def _unrot(d, cos, sin):
    d1, d2 = d[:, :128], d[:, 128:]
    return jnp.concatenate([d1 * cos + d2 * sin, d2 * cos - d1 * sin], axis=1)


def _ret_decays(lg):
    r = _row_ids(CH, CH)
    cidx = _lane_ids(CH, CH)
    diff = (r - cidx).astype(F32)
    decay = jnp.where(r >= cidx, jnp.exp(lg * jnp.maximum(diff, 0.0)), 0.0)
    decay_t = jnp.where(cidx >= r, jnp.exp(lg * jnp.maximum(-diff, 0.0)), 0.0)
    idx = _row_ids(CH).astype(F32)
    zeta = jnp.exp(lg * (CH - 1.0 - idx))
    xi = jnp.exp(lg * (idx + 1.0))
    return decay, decay_t, zeta, xi


def _ret_fwd(u0, ycat, cos, sin, norm_g, B, nch, rider=None):
    R = u0.shape[0]

    def body(u_ref, cos_ref, sin_ref, ng_ref, ycat_in, out_ref, opre_ref, rin_ref, Rst):
        c = pl.program_id(1)

        @pl.when(c == 0)
        def _():
            Rst[...] = jnp.zeros_like(Rst)

        cos_v, sin_v = cos_ref[...], sin_ref[...]
        for h in range(RET_HEADS):
            lg = _RET_LG[h]
            cols = slice(256 * h, 256 * h + 256)
            decay, _, zeta, xi = _ret_decays(lg)
            qr = _rot(u_ref[:, cols], cos_v, sin_v)
            kr = _rot(u_ref[:, 1024 + 256 * h:1024 + 256 * h + 256], cos_v, sin_v) * _RET_SCALE
            v = u_ref[:, 2048 + 256 * h:2048 + 256 * h + 256]
            gate = u_ref[:, 3072 + 256 * h:3072 + 256 * h + 256]
            Rh = Rst[h]
            rin_ref[h] = Rh
            inner = _dot(_dot_nt(qr, kr) * decay, v)
            cross = _dot(qr, Rh) * xi
            Rst[h] = math.exp(CH * lg) * Rh + _dot((kr * zeta).T, v)
            o = inner + cross
            opre_ref[:, cols] = o
            oc = o - jnp.mean(o, axis=-1, keepdims=True)
            rr = lax.rsqrt(jnp.mean(oc * oc, axis=-1, keepdims=True) + EPS)
            out_ref[:, cols] = (_silu(gate) * (oc * rr * ng_ref[:, cols])).astype(out_ref.dtype)

    kw = dict(
        grid=(B, nch),
        in_specs=[pl.BlockSpec((CH, 4096), lambda b, c: (b * nch + c, 0)),
                  pl.BlockSpec((CH, 128), lambda b, c: (c, 0)), pl.BlockSpec((CH, 128), lambda b, c: (c, 0)),
                  pl.BlockSpec((1, 1024), lambda b, c: (0, 0)),
                  pl.BlockSpec(memory_space=pl.ANY)],
        out_specs=[pl.BlockSpec((CH, 1024), lambda b, c: (b * nch + c, 1)),
                   pl.BlockSpec((CH, 1024), lambda b, c: (b * nch + c, 0)),
                   pl.BlockSpec((None, None, RET_HEADS, 256, 256), lambda b, c: (b, c, 0, 0, 0))],
        out_shape=[jax.ShapeDtypeStruct(ycat.shape, ycat.dtype), jax.ShapeDtypeStruct((R, 1024), F32),
                   jax.ShapeDtypeStruct((B, nch, RET_HEADS, 256, 256), F32)],
        scratch_shapes=[pltpu.VMEM((RET_HEADS, 256, 256), F32)],
        input_output_aliases={4: 0})
    return _call(body, "ret_fwd", ("arbitrary", "arbitrary"), kw, (u0, cos, sin, norm_g, ycat), rider)


def _ret_bwd(dycat, u0, opre, rin, cos, sin, norm_g, B, nch, rider=None):
    R = u0.shape[0]

    def body(dy_ref, u_ref, opre_ref, rin_ref, cos_ref, sin_ref, ng_ref, du_ref, pg_ref, dR):
        @pl.when(pl.program_id(1) == 0)
        def _():
            dR[...] = jnp.zeros_like(dR)
            pg_ref[...] = jnp.zeros_like(pg_ref)

        cos_v, sin_v = cos_ref[...], sin_ref[...]
        for h in range(RET_HEADS):
            lg = _RET_LG[h]
            cols = slice(256 * h, 256 * h + 256)
            decay, decay_t, zeta, xi = _ret_decays(lg)
            qr = _rot(u_ref[:, cols], cos_v, sin_v)
            kr = _rot(u_ref[:, 1024 + 256 * h:1024 + 256 * h + 256], cos_v, sin_v) * _RET_SCALE
            v = u_ref[:, 2048 + 256 * h:2048 + 256 * h + 256]
            gate = u_ref[:, 3072 + 256 * h:3072 + 256 * h + 256]
            ng = ng_ref[:, cols]
            o = opre_ref[:, cols]
            oc = o - jnp.mean(o, axis=-1, keepdims=True)
            rr = lax.rsqrt(jnp.mean(oc * oc, axis=-1, keepdims=True) + EPS)
            ohat = oc * rr
            dout = dy_ref[:, cols]
            du_ref[:, 3072 + 256 * h:3072 + 256 * h + 256] = (dout * (ohat * ng) * _dsilu(gate)).astype(du_ref.dtype)
            don = dout * _silu(gate)
            pg_ref[0:1, cols] += jnp.sum(don * ohat, axis=0, keepdims=True)
            dohat = don * ng
            do = rr * (dohat - jnp.mean(dohat, axis=-1, keepdims=True)
                       - ohat * jnp.mean(dohat * ohat, axis=-1, keepdims=True))
            Rh = rin_ref[h]
            dRn = dR[h]
            sc_t = _dot_nt(kr, qr) * decay_t
            dv = _dot(sc_t, do) + _dot(kr * zeta, dRn)
            ds = _dot_nt(do, v) * decay
            ds_t = _dot_nt(v, do) * decay_t
            dox = do * xi
            dq = _dot(ds, kr) + _dot_nt(dox, Rh)
            dk = _dot(ds_t, qr) + zeta * _dot_nt(v, dRn)
            dR[h] = math.exp(CH * lg) * dRn + _dot(qr.T, dox)
            du_ref[:, cols] = _unrot(dq, cos_v, sin_v).astype(du_ref.dtype)
            du_ref[:, 1024 + 256 * h:1024 + 256 * h + 256] = (_unrot(dk, cos_v, sin_v) * _RET_SCALE).astype(du_ref.dtype)
            du_ref[:, 2048 + 256 * h:2048 + 256 * h + 256] = dv.astype(du_ref.dtype)

    rmap = lambda b, c: (b * nch + nch - 1 - c, 0)
    kw = dict(
        grid=(B, nch),
        in_specs=[pl.BlockSpec((CH, 1024), lambda b, c: (b * nch + nch - 1 - c, 1)),
                  pl.BlockSpec((CH, 4096), rmap), pl.BlockSpec((CH, 1024), rmap),
                  pl.BlockSpec((None, None, RET_HEADS, 256, 256), lambda b, c: (b, nch - 1 - c, 0, 0, 0)),
                  pl.BlockSpec((CH, 128), lambda b, c: (nch - 1 - c, 0)),
                  pl.BlockSpec((CH, 128), lambda b, c: (nch - 1 - c, 0)),
                  pl.BlockSpec((1, 1024), lambda b, c: (0, 0))],
        out_specs=[pl.BlockSpec((CH, 4096), rmap), pl.BlockSpec((None, 8, 1024), lambda b, c: (b, 0, 0))],
        out_shape=[jax.ShapeDtypeStruct((R, 4096), _MXU), jax.ShapeDtypeStruct((B, 8, 1024), F32)],
        scratch_shapes=[pltpu.VMEM((RET_HEADS, 256, 256), F32)])
    return _call(body, "ret_bwd", ("arbitrary", "arbitrary"), kw, (dycat, u0, opre, rin, cos, sin, norm_g), rider)


_SB_SCALE = SB_HD ** -0.5


_SB_NB = 3


def _sb_valid(qb, kb, live):
    qpos = qb * CH + jnp.bitwise_and(_row_ids(2 * CH, CH), CH - 1)
    kpos = kb * CH + _lane_ids(2 * CH, CH)
    first = PAD + (1 - live) * (1 << 24)
    return jnp.logical_and(kpos < qpos, kpos >= first)


_SB_DEAD = -100.0
_SB_OFF = -1e30


def _sb_alive(acc):
    return (jnp.max(acc) > _SB_DEAD).astype(jnp.int32)


def _sb_softplus(z):
    return jnp.maximum(z, 0.0) + jnp.log(1.0 + jnp.exp(-jnp.abs(z)))


def _stack_heads(x):
    hm = _lane_ids(1, 128) < SB_HD
    return jnp.concatenate([jnp.where(hm, x, 0.0), jnp.where(hm, 0.0, x)], axis=0)


def _unstack_heads(x2):
    return jnp.where(_lane_ids(1, 128) < SB_HD, x2[:CH], x2[CH:])


def _sb_fwd(u1, B, nch, rider=None):
    R = u1.shape[0]
    Pn = nch * CH

    def body(q_ref, k_ref, v_ref, out_ref):
        qb = pl.program_id(2)
        q2 = _stack_heads(q_ref[...] * _SB_SCALE).astype(_MXU)
        mgt = (_row_ids(CH, CH) > _lane_ids(CH, CH)).astype(F32)

        def step(i, carry):
            out2, acc = carry
            blocks = []
            for t in range(_SB_NB):
                kb = qb - _SB_NB * i - t
                live = (kb >= 0).astype(jnp.int32)
                kbc = jnp.maximum(kb, 0)
                start = pl.multiple_of(kbc * CH, CH)
                valid = _sb_valid(qb, kbc, live)
                z = _dot_nt(q2, k_ref[pl.ds(start, CH), :])
                sp = _sb_softplus(z)
                lm = jnp.where(valid, -sp, 0.0)
                blocks.append((valid, z - sp, _dot_split(lm, mgt), jnp.sum(lm, axis=1, keepdims=True), start))
            for valid, ls, loc, rs, start in blocks:
                w = jnp.where(valid, jnp.exp(ls + loc + acc), 0.0)
                out2 = out2 + _dot(w, v_ref[pl.ds(start, CH), :])
                acc = acc + rs
            return out2, acc

        trips = (qb + _SB_NB) // _SB_NB

        def more(c):
            return jnp.logical_and(c[0] < trips, c[1] > 0)

        def trip(c):
            out2, acc = step(c[0], c[2:])
            return c[0] + 1, _sb_alive(acc), out2, acc

        init = (jnp.int32(0), jnp.int32(1), jnp.zeros((2 * CH, 128), F32), jnp.zeros((2 * CH, 1), F32))
        out2 = lax.while_loop(more, trip, init)[2]
        out_ref[...] = _unstack_heads(out2).astype(out_ref.dtype)

    qspec = lambda off: pl.BlockSpec((CH, 128), lambda b, hp, qb: (b * nch + qb, off + hp))
    kspec = lambda off: pl.BlockSpec((Pn, 128), lambda b, hp, qb: (b, off + hp))
    kw = dict(grid=(B, SB_HEADS // 2, nch), in_specs=[qspec(0), kspec(8), kspec(16)], out_specs=[qspec(0)],
              out_shape=[jax.ShapeDtypeStruct((R, 2048), _MXU)])
    return _call(body, "sb_fwd", ("arbitrary", "arbitrary", "arbitrary"), kw, (u1, u1, u1), rider)


def _sb_bwd(dycat, u1, B, nch, rider=None):
    R = u1.shape[0]
    Pn = nch * CH

    def body(q_ref, k_ref, v_ref, do_ref, dq_ref, dk_ref, dv_ref, lm_scr, ls_scr):
        qb = pl.program_id(2)

        @pl.when(qb == 0)
        def _():
            dk_ref[...] = jnp.zeros_like(dk_ref)
            dv_ref[...] = jnp.zeros_like(dv_ref)

        q2 = _stack_heads(q_ref[...] * _SB_SCALE)
        do2 = _stack_heads(do_ref[...])
        q2t, do2t = q2.T.astype(_MXU), do2.T.astype(_MXU)
        q2, do2 = q2.astype(_MXU), do2.astype(_MXU)
        rr = _row_ids(CH, CH)
        cc = _lane_ids(CH, CH)
        mle = (rr <= cc).astype(F32)
        mlt = (rr < cc).astype(F32)
        trips = (qb + _SB_NB) // _SB_NB

        def more(c):
            return jnp.logical_and(c[0] < trips, c[1] > 0)

        def scan(c):
            acc = c[2]
            for t in range(_SB_NB):
                kb = qb - _SB_NB * c[0] - t
                kbc = jnp.maximum(kb, 0)
                valid = _sb_valid(qb, kbc, (kb >= 0).astype(jnp.int32))
                z = _dot_nt(q2, k_ref[pl.ds(pl.multiple_of(kbc * CH, CH), CH), :])
                sp = _sb_softplus(z)
                lm = jnp.where(valid, -sp, 0.0)
                lm_scr[c[0] * _SB_NB + t] = lm
                ls_scr[c[0] * _SB_NB + t] = jnp.where(valid, z - sp, _SB_OFF)
                acc = acc + jnp.sum(lm, axis=1, keepdims=True)
            return c[0] + 1, _sb_alive(acc), acc

        used, _, s2 = lax.while_loop(more, scan, (jnp.int32(0), jnp.int32(1), jnp.zeros((2 * CH, 1), F32)))
        base = qb + 1 - _SB_NB * used

        def step(i, carry):
            dq2, pacc, gacc = carry
            blocks = []
            for t in range(_SB_NB):
                kb = base + _SB_NB * i + t
                start = pl.multiple_of(jnp.maximum(kb, 0) * CH, CH)
                slot = (used - 1 - i) * _SB_NB + (_SB_NB - 1 - t)
                lm = lm_scr[slot]
                blocks.append((ls_scr[slot], _dot_split(lm, mle), jnp.sum(lm, axis=1, keepdims=True), start))
            stage = []
            for ls, ploc, rs, start in blocks:
                w = jnp.exp(ls + (s2 - (ploc + pacc)))
                gg = _dot_nt(do2, v_ref[pl.ds(start, CH), :]) * w
                stage.append((ls, w, gg, _dot_split(gg, mlt), jnp.sum(gg, axis=1, keepdims=True), start))
                pacc = pacc + rs
            for ls, w, gg, gloc, gs, start in stage:
                sig = jnp.exp(ls)
                dz = gg * (1.0 - sig) - (gloc + gacc) * sig
                dq2 = dq2 + _dot(dz, k_ref[pl.ds(start, CH), :])
                dk_ref[:, pl.ds(start, CH)] += _dot(q2t, dz)
                dv_ref[:, pl.ds(start, CH)] += _dot(do2t, w)
                gacc = gacc + gs
            return dq2, pacc, gacc

        zero = jnp.zeros((2 * CH, 1), F32)
        dq2 = lax.fori_loop(0, used, step, (jnp.zeros((2 * CH, 128), F32), zero, zero))[0]
        dq_ref[...] = (_unstack_heads(dq2) * _SB_SCALE).astype(dq_ref.dtype)

    qspec = lambda off: pl.BlockSpec((CH, 128), lambda b, hp, qb: (b * nch + qb, off + hp))
    kspec = lambda off: pl.BlockSpec((Pn, 128), lambda b, hp, qb: (b, off + hp))
    tspec = pl.BlockSpec((128, Pn), lambda b, hp, qb: (hp, b))
    full = jax.ShapeDtypeStruct((1024, R), F32)
    slots = (nch - 1 + _SB_NB) // _SB_NB * _SB_NB
    kw = dict(grid=(B, SB_HEADS // 2, nch), in_specs=[qspec(0), kspec(8), kspec(16), qspec(0)],
              out_specs=[qspec(0), tspec, tspec], out_shape=[jax.ShapeDtypeStruct((R, 1024), _MXU), full, full],
              scratch_shapes=[pltpu.VMEM((slots, 2 * CH, CH), F32)] * 2)
    return _call(body, "sb_bwd", ("arbitrary", "arbitrary", "arbitrary"), kw, (u1, u1, u1, dycat), rider)


def _neg_expm1(x):
    series = -(x * (1.0 + x * (0.5 + x * (1.0 / 6.0 + x * (1.0 / 24.0)))))
    return jnp.where(x > -0.05, series, 1.0 - jnp.exp(x))


def _lru_gates(x, wa_ref, ba_ref, wx_ref, bx_ref, lam_ref):
    rs, is_ = [], []
    for n in range(LRU_BLOCKS):
        xb = x[:, 128 * n:128 * n + 128]
        rs.append(_dot(xb, wa_ref[n]))
        is_.append(_dot(xb, wx_ref[n]))
    r = _sigmoid(jnp.concatenate(rs, axis=1) + ba_ref[...])
    i = _sigmoid(jnp.concatenate(is_, axis=1) + bx_ref[...])
    sp = _softplus(-lam_ref[...])
    la = -LRU_C * r * sp
    a = jnp.exp(la)
    mult = jnp.sqrt(jnp.maximum(_neg_expm1(2.0 * la), 0.0))
    return r, i, sp, a, mult


def _lru_fwd(u1, ycat, conv_w, conv_b, wa, ba, wx, bx, lam, B, nch):
    R = u1.shape[0]

    def body(x_ref, xp_ref, gate_ref, cw_ref, cb_ref, wa_ref, ba_ref, wx_ref, bx_ref, lam_ref, ycat_in,
             out_ref, hs_ref, hc):
        c = pl.program_id(1)

        @pl.when(c == 0)
        def _():
            hc[...] = jnp.zeros_like(hc)

        x = _conv_pre(xp_ref[...], x_ref[...], cw_ref, cb_ref, 4)
        r, i, sp, a, mult = _lru_gates(x, wa_ref, ba_ref, wx_ref, bx_ref, lam_ref)
        b = jnp.where(_real_rows(c), mult * (i * x), 0.0)
        rows = _row_ids(CH)
        s = 1
        while s < CH:
            a_s = jnp.where(rows >= s, pltpu.roll(a, s, axis=0), 1.0)
            b_s = jnp.where(rows >= s, pltpu.roll(b, s, axis=0), 0.0)
            b = a * b_s + b
            a = a * a_s
            s *= 2
        h = a * hc[0:1, :] + b
        hs_ref[...] = h
        hc[0:1, :] = hs_ref[CH - 1:CH, :]
        out_ref[...] = (h * _gelu(gate_ref[...])).astype(out_ref.dtype)

    row = lambda col: pl.BlockSpec((CH, 1024), lambda b, c: (b * nch + c, col))
    vec = pl.BlockSpec((1, 1024), lambda b, c: (0, 0))
    wsp = pl.BlockSpec((LRU_BLOCKS, 128, 128), lambda b, c: (0, 0, 0))
    return pl.pallas_call(
        body, name="lru_fwd", grid=(B, nch),
        in_specs=[row(4), pl.BlockSpec((8, 1024), _prev8_map(nch, 4)), row(3),
                  pl.BlockSpec((4, 1024), lambda b, c: (0, 0)), vec, wsp, vec, wsp, vec, vec,
                  pl.BlockSpec(memory_space=pl.ANY)],
        out_specs=[row(1), row(0)],
        out_shape=[jax.ShapeDtypeStruct(ycat.shape, ycat.dtype), jax.ShapeDtypeStruct((R, 1024), F32)],
        scratch_shapes=[pltpu.VMEM((8, 1024), F32)],
        input_output_aliases={10: 0},
        compiler_params=_cparams(("parallel", "arbitrary")),
    )(u1, u1, u1, conv_w, conv_b, wa, ba, wx, bx, lam, ycat)


def _lru_bwd(dycat, u1, hs, conv_w, conv_b, wa, ba, wx, bx, lam, B, nch):
    R = u1.shape[0]

    def body(dy_ref, x_ref, xp_ref, gate_ref, hs_ref, hsp_ref, cw_ref, cb_ref, wa_ref, ba_ref, wx_ref, bx_ref, lam_ref,
             dgate_ref, dxc_ref, pg_ref, dwa_ref, dwx_ref, lc):
        c = nch - 1 - pl.program_id(1)

        @pl.when(pl.program_id(1) == 0)
        def _():
            lc[...] = jnp.zeros_like(lc)
            pg_ref[...] = jnp.zeros_like(pg_ref)
            dwa_ref[...] = jnp.zeros_like(dwa_ref)
            dwx_ref[...] = jnp.zeros_like(dwx_ref)

        x = _conv_pre(xp_ref[...], x_ref[...], cw_ref, cb_ref, 4)
        r, i, sp, a, mult = _lru_gates(x, wa_ref, ba_ref, wx_ref, bx_ref, lam_ref)
        h = hs_ref[...]
        hprev = _shift_down(hsp_ref[...], h, 1)
        gate = gate_ref[...]
        dy = dy_ref[...]
        dgate_ref[...] = (dy * h * _dgelu(gate)).astype(dgate_ref.dtype)
        rows = _row_ids(CH)
        lam_t = dy * _gelu(gate) + jnp.where(rows == CH - 1, lc[0:1, :], 0.0)
        coef = jnp.where(rows < CH - 1, pltpu.roll(a, CH - 1, axis=0), 0.0)
        s = 1
        while s < CH:
            c_s = jnp.where(rows < CH - s, pltpu.roll(coef, CH - s, axis=0), 1.0)
            l_s = jnp.where(rows < CH - s, pltpu.roll(lam_t, CH - s, axis=0), 0.0)
            lam_t = coef * l_s + lam_t
            coef = coef * c_s
            s *= 2
        lc[0:1, :] = jnp.sum(jnp.where(rows == 0, a * lam_t, 0.0), axis=0, keepdims=True)
        db = jnp.where(_real_rows(c), lam_t, 0.0)
        da = db * hprev
        dmult = db * (i * x)
        di = db * mult * x
        dx = db * mult * i
        pos = mult > 0.0
        dla = da * a + jnp.where(pos, -dmult * (a * a) / jnp.where(pos, mult, 1.0), 0.0)
        dr = dla * (-LRU_C * sp)
        pg_ref[2:3, :] += jnp.sum(dla * (LRU_C * r) * _sigmoid(-lam_ref[...]), axis=0, keepdims=True)
        dpr = dr * r * (1.0 - r)
        dpi = di * i * (1.0 - i)
        pg_ref[0:1, :] += jnp.sum(dpr, axis=0, keepdims=True)
        pg_ref[1:2, :] += jnp.sum(dpi, axis=0, keepdims=True)
        dxs = []
        for n in range(LRU_BLOCKS):
            blk = slice(128 * n, 128 * n + 128)
            dxs.append(dx[:, blk] + _dot_nt(dpr[:, blk], wa_ref[n]) + _dot_nt(dpi[:, blk], wx_ref[n]))
            dwa_ref[n] += _dot_tn(x[:, blk], dpr[:, blk])
            dwx_ref[n] += _dot_tn(x[:, blk], dpi[:, blk])
        dxc_ref[...] = jnp.concatenate(dxs, axis=1)

    rmap = lambda col: (lambda b, c: (b * nch + nch - 1 - c, col))
    row = lambda col: pl.BlockSpec((CH, 1024), rmap(col))
    prev = lambda col: pl.BlockSpec(
        (8, 1024), lambda b, c: (jnp.maximum((b * nch + nch - 1 - c) * (CH // 8) - 1, 0), col))
    vec = pl.BlockSpec((1, 1024), lambda b, c: (0, 0))
    wsp = pl.BlockSpec((LRU_BLOCKS, 128, 128), lambda b, c: (0, 0, 0))
    full = jax.ShapeDtypeStruct((R, 1024), F32)
    return pl.pallas_call(
        body, name="lru_bwd", grid=(B, nch),
        in_specs=[row(1), row(4), prev(4), row(3), row(0), prev(0),
                  pl.BlockSpec((4, 1024), lambda b, c: (0, 0)), vec, wsp, vec, wsp, vec, vec],
        out_specs=[row(0), row(0), pl.BlockSpec((None, 8, 1024), lambda b, c: (b, 0, 0)),
                   pl.BlockSpec((None, LRU_BLOCKS, 128, 128), lambda b, c: (b, 0, 0, 0)),
                   pl.BlockSpec((None, LRU_BLOCKS, 128, 128), lambda b, c: (b, 0, 0, 0))],
        out_shape=[jax.ShapeDtypeStruct((R, 1024), _MXU), full, jax.ShapeDtypeStruct((B, 8, 1024), F32),
                   jax.ShapeDtypeStruct((B, LRU_BLOCKS, 128, 128), F32),
                   jax.ShapeDtypeStruct((B, LRU_BLOCKS, 128, 128), F32)],
        scratch_shapes=[pltpu.VMEM((8, 1024), F32)],
        compiler_params=_cparams(("parallel", "arbitrary")),
    )(dycat, u1, u1, u1, hs, hs, conv_w, conv_b, wa, ba, wx, bx, lam)


_FFN_TC = FFN // 2


def _ffn_specs(nch):
    nt = FFN // _FFN_TC
    row = lambda off: pl.BlockSpec((CH, _FFN_TC), lambda b, c, j: (b * nch + c, off + j))
    prev = lambda off: pl.BlockSpec(
        (8, _FFN_TC), lambda b, c, j: (jnp.maximum((b * nch + c) * (CH // 8) - 1, 0), off + j))
    wsp = lambda off: pl.BlockSpec((3, _FFN_TC), lambda b, c, j: (0, off + j))
    bsp = lambda off: pl.BlockSpec((1, _FFN_TC), lambda b, c, j: (0, off + j))
    return nt, row, [row(0), prev(0), row(nt), prev(nt), wsp(0), wsp(nt), bsp(0), bsp(nt)]


def _ffn_act_fwd(uf, conv_w, conv_b, B, nch, rider=None):
    R = uf.shape[0]
    nt, row, specs = _ffn_specs(nch)

    def body(g_ref, gp_ref, u_ref, up_ref, wg_ref, wu_ref, bg_ref, bu_ref, o_ref):
        cg = _conv_pre(gp_ref[...], g_ref[...], wg_ref, bg_ref, 3)
        cu = _conv_pre(up_ref[...], u_ref[...], wu_ref, bu_ref, 3)
        o_ref[...] = jnp.where(_real_rows(pl.program_id(1)), _silu(cg) * cu, 0.0).astype(o_ref.dtype)

    kw = dict(grid=(B, nch, nt), in_specs=specs, out_specs=[row(0)],
              out_shape=[jax.ShapeDtypeStruct((R, FFN), _MXU)])
    return _call(body, "ffn_act_fwd", ("arbitrary", "arbitrary", "arbitrary"), kw,
                 (uf, uf, uf, uf, conv_w, conv_w, conv_b, conv_b), rider)


def _ffn_act_bwd(da, uf, conv_w, conv_b, nch, name, rider=None):
    R = uf.shape[0]
    nt = FFN // _FFN_TC
    nr = R // CH
    K = 3

    def body(da_ref, dan_ref, g_ref, gp_ref, gn_ref, u_ref, up_ref, un_ref, wg_ref, wu_ref, bg_ref, bu_ref,
             dug_ref, duu_ref, dwg_ref, dwu_ref):
        i = pl.program_id(1)

        @pl.when(i == 0)
        def _():
            dwg_ref[...] = jnp.zeros_like(dwg_ref)
            dwu_ref[...] = jnp.zeros_like(dwu_ref)

        c = i % nch
        ext = CH + 8
        rows = _row_ids(ext)
        follows = (c < nch - 1).astype(jnp.int32)
        keep = jnp.logical_and(c * CH + rows >= PAD, rows < CH + 8 * follows)
        dav = jnp.where(keep, jnp.concatenate([da_ref[...], dan_ref[...]], axis=0), 0.0)

        def conv_ext(x_ref, xp_ref, xn_ref, w_ref, b_ref):
            cat = jnp.concatenate([xp_ref[...], x_ref[...], xn_ref[...]], axis=0)
            shifted = [cat[8:]] + [pltpu.roll(cat, s, axis=0)[8:] for s in range(1, K)]
            acc = shifted[0] * w_ref[K - 1:K, :] + b_ref[...]
            for s in range(1, K):
                acc = acc + shifted[s] * w_ref[K - 1 - s:K - s, :]
            return acc, shifted

        cg, gsh = conv_ext(g_ref, gp_ref, gn_ref, wg_ref, bg_ref)
        cu, ush = conv_ext(u_ref, up_ref, un_ref, wu_ref, bu_ref)
        sg = _sigmoid(cg)
        dcg = dav * cu * (sg * (1.0 + cg * (1.0 - sg)))
        dcu = dav * (cg * sg)
        for dc, xsh, w_ref, din_ref, dw_ref in ((dcg, gsh, wg_ref, dug_ref, dwg_ref), (dcu, ush, wu_ref, duu_ref, dwu_ref)):
            dp = dc[:CH]
            din = dp * w_ref[K - 1:K, :]
            dw_ref[7:8, :] += jnp.sum(dp, axis=0, keepdims=True)
            dw_ref[K - 1:K, :] += jnp.sum(dp * xsh[0][:CH], axis=0, keepdims=True)
            for s in range(1, K):
                din = din + pltpu.roll(dc, ext - s, axis=0)[:CH] * w_ref[K - 1 - s:K - s, :]
                dw_ref[K - 1 - s:K - s, :] += jnp.sum(dp * xsh[s][:CH], axis=0, keepdims=True)
            din_ref[...] = din.astype(din_ref.dtype)

    row = lambda off: pl.BlockSpec((CH, _FFN_TC), lambda j, i: (i, off + j))
    prev = lambda off: pl.BlockSpec((8, _FFN_TC), lambda j, i: (jnp.maximum(i * (CH // 8) - 1, 0), off + j))
    nxt = lambda off: pl.BlockSpec(
        (8, _FFN_TC), lambda j, i: (jnp.minimum((i + 1) * (CH // 8), nr * (CH // 8) - 1), off + j))
    wsp = lambda off: pl.BlockSpec((K, _FFN_TC), lambda j, i: (0, off + j))
    bsp = lambda off: pl.BlockSpec((1, _FFN_TC), lambda j, i: (0, off + j))
    acc = pl.BlockSpec((8, _FFN_TC), lambda j, i: (0, j))
    half = jax.ShapeDtypeStruct((R, FFN), _MXU)
    dwsh = jax.ShapeDtypeStruct((8, FFN), F32)
    kw = dict(
        grid=(nt, nr),
        in_specs=[row(0), nxt(0), row(0), prev(0), nxt(0), row(nt), prev(nt), nxt(nt), wsp(0), wsp(nt), bsp(0), bsp(nt)],
        out_specs=[row(0), row(0), acc, acc],
        out_shape=[half, half, dwsh, dwsh])
    return _call(body, name, ("arbitrary", "arbitrary"), kw,
                 (da, da, uf, uf, uf, uf, uf, uf, conv_w, conv_w, conv_b, conv_b), rider)


def _head(h, g, target, B, nch):
    R = h.shape[0]

    def body(h_ref, g_ref, t_ref, dh_ref, loss_ref, dg_ref):
        c = pl.program_id(1)

        @pl.when(c == 0)
        def _():
            dh_ref[...] = jnp.zeros_like(dh_ref)
            loss_ref[...] = jnp.zeros_like(loss_ref)
            dg_ref[...] = jnp.zeros_like(dg_ref)

        @pl.when(c > 0)
        def _():
            x = h_ref[...]
            gv = g_ref[...]
            r = lax.rsqrt(jnp.mean(x * x, axis=-1, keepdims=True) + EPS)
            xhat = x * r
            e = xhat * gv - t_ref[...]
            loss_ref[...] += 0.5 * jnp.sum(jnp.mean(e * e, axis=-1, keepdims=True), axis=0, keepdims=True)
            dy = e * (1.0 / D)
            dg_ref[0:1, :] += jnp.sum(dy * xhat, axis=0, keepdims=True)
            dx = dy * gv
            dh_ref[...] = r * (dx - xhat * jnp.mean(dx * xhat, axis=-1, keepdims=True))

    row = pl.BlockSpec((CH, D), lambda b, c: (b * nch + c, 0))
    return pl.pallas_call(
        body, name="head", grid=(B, nch),
        in_specs=[row, pl.BlockSpec((1, D), lambda b, c: (0, 0)),
                  pl.BlockSpec((CH, D), lambda b, c: (b * (nch - 1) + jnp.maximum(c - 1, 0), 0))],
        out_specs=[row, pl.BlockSpec((None, 8, 128), lambda b, c: (b, 0, 0)),
                   pl.BlockSpec((None, 8, D), lambda b, c: (b, 0, 0))],
        out_shape=[jax.ShapeDtypeStruct((R, D), F32), jax.ShapeDtypeStruct((B, 8, 128), F32),
                   jax.ShapeDtypeStruct((B, 8, D), F32)],
        compiler_params=_cparams(("parallel", "arbitrary")),
    )(h, g, target)


ADAM_LR = 0.001
ADAM_B1 = 0.9
ADAM_B2 = 0.999
ADAM_EPS = 1e-08
ADAM_WD = 0.01
ADAM_STEP = 10


def _adamw(w, g, m, v, name):
    Rr, C = w.shape
    tr = _tile(Rr, (256, 64))

    def body(w_ref, g_ref, m_ref, v_ref, d_ref, nm_ref, nv_ref):
        gv = g_ref[...]
        nm = ADAM_B1 * m_ref[...] + (1.0 - ADAM_B1) * gv
        nv = ADAM_B2 * v_ref[...] + (1.0 - ADAM_B2) * (gv * gv)
        m_hat = nm / (1.0 - ADAM_B1 ** ADAM_STEP)
        v_hat = nv / (1.0 - ADAM_B2 ** ADAM_STEP)
        d_ref[...] = -ADAM_LR * (m_hat / (jnp.sqrt(v_hat) + ADAM_EPS) + ADAM_WD * w_ref[...])
        nm_ref[...] = nm
        nv_ref[...] = nv

    spec = pl.BlockSpec((tr, C), lambda i: (i, 0))
    sh = jax.ShapeDtypeStruct((Rr, C), F32)
    return pl.pallas_call(
        body, name=name, grid=(Rr // tr,),
        in_specs=[spec] * 4, out_specs=[spec] * 3, out_shape=[sh] * 3,
        compiler_params=_cparams(("parallel",)),
    )(w, g, m, v)


_MESH = pl.DeviceIdType.MESH
_ANY = pl.BlockSpec(memory_space=pl.ANY)


def _place():
    x, y, c = lax.axis_index("x"), lax.axis_index("y"), lax.axis_index("c")
    chips = [(1 - x, y), (x, 1 - y), (1 - x, 1 - y)]
    return x, y, c, chips


def _rcopy(src, dst, ssem, rsem, dev):
    return pltpu.make_async_remote_copy(src_ref=src, dst_ref=dst, send_sem=ssem, recv_sem=rsem,
                                        device_id=dev, device_id_type=_MESH)


def _with_riders(body, kw, kind, riders):
    n_in, n_out, n_scr = len(kw["in_specs"]), len(kw["out_specs"]), len(kw.get("scratch_shapes", []))
    grid = kw["grid"]
    nr = len(riders)
    nsem = 4 if kind == "gather" else 2

    def new_body(*refs):
        ins, srcs = refs[:n_in], refs[n_in:n_in + nr]
        outs, dsts = refs[n_in + nr:n_in + nr + n_out], refs[n_in + nr + n_out:n_in + 2 * nr + n_out]
        scr = refs[n_in + 2 * nr + n_out:n_in + 2 * nr + n_out + n_scr]
        sems = refs[n_in + 2 * nr + n_out + n_scr:]
        first = last = None
        for axis, size in enumerate(grid):
            i = pl.program_id(axis)
            first = (i == 0) if first is None else jnp.logical_and(first, i == 0)
            last = (i == size - 1) if last is None else jnp.logical_and(last, i == size - 1)
        x, y, c, chips = _place()
        k = 2 * x + y
        sib = (x, y, 1 - c)
        ssem, rsem = sems[:2]
        sends = []
        for a in range(nr):
            for j, (cx, cy) in enumerate(chips):
                if kind == "gather":
                    src, dst = srcs[a].at[c], dsts[a].at[k, c]
                else:
                    src, dst = srcs[a].at[2 * cx + cy], dsts[a].at[k]
                sends.append(_rcopy(src, dst, ssem.at[3 * a + j], rsem.at[3 * a + j], (cx, cy, c)))

        @pl.when(first)
        def _():
            for cp in sends:
                cp.start()

        body(*ins, *outs, *scr)

        @pl.when(last)
        def _():
            passed = []
            for a in range(nr):
                for j, (cx, cy) in enumerate(chips):
                    got = dsts[a].at[2 * cx + cy, c] if kind == "gather" else dsts[a].at[2 * cx + cy]
                    _rcopy(got, got, ssem.at[3 * a + j], rsem.at[3 * a + j], (cx, cy, c)).wait_recv()
                    if kind == "gather":
                        fw = _rcopy(got, got, sems[2].at[3 * a + j], sems[3].at[3 * a + j], sib)
                        fw.start()
                        passed.append(fw)
            if kind == "gather":
                for a in range(nr):
                    for j, (cx, cy) in enumerate(chips):
                        got = dsts[a].at[2 * cx + cy, 1 - c]
                        _rcopy(got, got, sems[2].at[3 * a + j], sems[3].at[3 * a + j], sib).wait_recv()
            for cp in sends + passed:
                cp.wait_send()

    kw = dict(kw)
    kw["in_specs"] = list(kw["in_specs"]) + [_ANY] * nr
    kw["out_specs"] = list(kw["out_specs"]) + [_ANY] * nr
    kw["out_shape"] = list(kw["out_shape"]) + [
        jax.ShapeDtypeStruct(((4,) + r.shape) if kind == "gather" else r.shape, r.dtype) for r in riders]
    kw["scratch_shapes"] = list(kw.get("scratch_shapes", [])) + [pltpu.SemaphoreType.DMA((3 * nr,))] * nsem
    return new_body, kw


def _call(body, name, sem, kw, args, rider=None):
    if rider is not None:
        body, kw = _with_riders(body, kw, *rider)
        args = tuple(args) + tuple(rider[1])
    return pl.pallas_call(body, name=name, compiler_params=_cparams(sem), **kw)(*args)


def _fill_own(result, own, chip):
    return lax.dynamic_update_index_in_dim(result, own, chip, 0)


def _gather_shards(bigs, small):
    nb = len(bigs)

    def body(*refs):
        ins, outs = refs[:nb + 1], refs[nb + 1:2 * nb + 2]
        ssem, rsem, fssem, frsem = refs[2 * nb + 2:]
        x, y, c, chips = _place()
        k = 2 * x + y
        sib = (x, y, 1 - c)

        def part(a, slot, hc):
            return outs[a].at[slot] if a == nb else outs[a].at[slot, hc]

        first = []
        for a in range(nb + 1):
            src = ins[a] if a == nb else ins[a].at[c]
            for j, (cx, cy) in enumerate(chips):
                first.append(_rcopy(src, part(a, k, c), ssem.at[3 * a + j], rsem.at[3 * a + j], (cx, cy, c)))
        for cp in first:
            cp.start()
        passed = []
        for a in range(nb + 1):
            for j, (cx, cy) in enumerate(chips):
                got = part(a, 2 * cx + cy, c)
                _rcopy(got, got, ssem.at[3 * a + j], rsem.at[3 * a + j], (cx, cy, c)).wait_recv()
                if a < nb:
                    fw = _rcopy(got, got, fssem.at[3 * a + j], frsem.at[3 * a + j], sib)
                    fw.start()
                    passed.append(fw)
        for a in range(nb):
            for j, (cx, cy) in enumerate(chips):
                got = part(a, 2 * cx + cy, 1 - c)
                _rcopy(got, got, fssem.at[3 * a + j], frsem.at[3 * a + j], sib).wait_recv()
        for cp in first + passed:
            cp.wait_send()

    arrs = list(bigs) + [small]
    n = 3 * (nb + 1)
    return pl.pallas_call(
        body, name="gather_shards",
        in_specs=[_ANY] * (nb + 1), out_specs=[_ANY] * (nb + 1),
        out_shape=[jax.ShapeDtypeStruct((4,) + a.shape, a.dtype) for a in arrs],
        scratch_shapes=[pltpu.SemaphoreType.DMA((n,)), pltpu.SemaphoreType.DMA((n,)),
                        pltpu.SemaphoreType.DMA((n,)), pltpu.SemaphoreType.DMA((n,))],
    )(*arrs)


def _swap_halves(grads, name):
    na = len(grads)
    halves = [g.shape[1] // 2 for g in grads]

    def body(*refs):
        ins, outs = refs[:na], refs[na:2 * na]
        ssem, rsem = refs[2 * na:]
        x, y, c, _ = _place()
        sib = (x, y, 1 - c)
        cps = [_rcopy(ins[a].at[:, pl.ds((1 - c) * halves[a], halves[a]), :], outs[a], ssem.at[a], rsem.at[a], sib)
               for a in range(na)]
        for cp in cps:
            cp.start()
        for cp in cps:
            cp.wait()

    return pl.pallas_call(
        body, name=name,
        in_specs=[_ANY] * na, out_specs=[_ANY] * na,
        out_shape=[jax.ShapeDtypeStruct((4, g.shape[1] // 2, g.shape[2]), g.dtype) for g in grads],
        scratch_shapes=[pltpu.SemaphoreType.DMA((na,)), pltpu.SemaphoreType.DMA((na,))],
    )(*grads)


def _sum_rows(rh):
    return rh if rh <= 512 else _tile(rh, (512, 256, 128, 64, 32))


def _chip_sum(grad, recv, core, name):
    _, r, cdim = grad.shape
    rh = r // 2
    tr = _sum_rows(rh)
    nblk = rh // tr

    def body(core_ref, g_ref, r_ref, o_ref):
        o_ref[...] = (g_ref[...] + r_ref[...]).astype(o_ref.dtype)

    return pl.pallas_call(
        body, name=name,
        grid_spec=pltpu.PrefetchScalarGridSpec(
            num_scalar_prefetch=1, grid=(4, nblk),
            in_specs=[pl.BlockSpec((None, tr, cdim), lambda s, i, cr: (s, cr[0] * nblk + i, 0)),
                      pl.BlockSpec((None, tr, cdim), lambda s, i, cr: (s, i, 0))],
            out_specs=pl.BlockSpec((None, tr, cdim), lambda s, i, cr: (s, i, 0))),
        out_shape=jax.ShapeDtypeStruct((4, rh, cdim), BF16),
        compiler_params=_cparams(("parallel", "parallel")),
    )(core, grad, recv)


def _scatter_sums(sums):
    na = len(sums)

    def body(*refs):
        ins, outs = refs[:na], refs[na:2 * na]
        ssem, rsem, lsem = refs[2 * na:]
        x, y, c, chips = _place()
        k = 2 * x + y
        local = [pltpu.make_async_copy(ins[a].at[k], outs[a].at[k], lsem.at[a]) for a in range(na)]
        for cp in local:
            cp.start()
        cps = []
        for a in range(na):
            for j, (cx, cy) in enumerate(chips):
                cps.append(_rcopy(ins[a].at[2 * cx + cy], outs[a].at[k], ssem.at[3 * a + j], rsem.at[3 * a + j],
                                  (cx, cy, c)))
        for cp in cps:
            cp.start()
        for a in range(na):
            for j, (cx, cy) in enumerate(chips):
                got = outs[a].at[2 * cx + cy]
                _rcopy(got, got, ssem.at[3 * a + j], rsem.at[3 * a + j], (cx, cy, c)).wait_recv()
        for cp in cps:
            cp.wait_send()
        for cp in local:
            cp.wait()

    return pl.pallas_call(
        body, name="scatter_sums",
        in_specs=[_ANY] * na, out_specs=[_ANY] * na,
        out_shape=[jax.ShapeDtypeStruct(s.shape, s.dtype) for s in sums],
        scratch_shapes=[pltpu.SemaphoreType.DMA((3 * na,)), pltpu.SemaphoreType.DMA((3 * na,)),
                        pltpu.SemaphoreType.DMA((na,))],
    )(*sums)


def _sum_chips(parts, name):
    _, rh, cdim = parts.shape
    tr = _sum_rows(rh)

    def body(p_ref, o_ref):
        acc = p_ref[0].astype(F32)
        for j in range(1, 4):
            acc = acc + p_ref[j].astype(F32)
        o_ref[...] = acc

    return pl.pallas_call(
        body, name=name, grid=(rh // tr,),
        in_specs=[pl.BlockSpec((4, tr, cdim), lambda i: (0, i, 0))],
        out_specs=pl.BlockSpec((tr, cdim), lambda i: (i, 0)),
        out_shape=jax.ShapeDtypeStruct((rh, cdim), F32),
        compiler_params=_cparams(("parallel",)),
    )(parts)


def _join_halves(reds):
    na = len(reds)

    def body(*refs):
        ins, outs = refs[:na], refs[na:2 * na]
        ssem, rsem = refs[2 * na:]
        x, y, c, _ = _place()
        cps = [_rcopy(ins[a], outs[a], ssem.at[a], rsem.at[a], (x, y, 1 - c)) for a in range(na)]
        for cp in cps:
            cp.start()
        for cp in cps:
            cp.wait()

    return pl.pallas_call(
        body, name="join_halves",
        in_specs=[_ANY] * na, out_specs=[_ANY] * na,
        out_shape=[jax.ShapeDtypeStruct(r.shape, r.dtype) for r in reds],
        scratch_shapes=[pltpu.SemaphoreType.DMA((na,)), pltpu.SemaphoreType.DMA((na,))],
    )(*reds)


def _allreduce_small(buf):
    n = buf.shape[0]

    def body(in_ref, out_ref, recv, ssem, rsem):
        x, y, c, _ = _place()
        peers = [(x, y, 1 - c), (1 - x, y, c), (x, 1 - y, c)]
        out_ref[...] = in_ref[...]
        for r, peer in enumerate(peers):
            cp = _rcopy(out_ref, recv.at[r], ssem.at[r], rsem.at[r], peer)
            cp.start()
            cp.wait()
            out_ref[...] = out_ref[...] + recv[r]

    vm = pl.BlockSpec(memory_space=pltpu.VMEM)
    return pl.pallas_call(
        body, name="allreduce_small",
        in_specs=[vm], out_specs=vm,
        out_shape=jax.ShapeDtypeStruct(buf.shape, F32),
        scratch_shapes=[pltpu.VMEM((3, n, 128), F32), pltpu.SemaphoreType.DMA((3,)), pltpu.SemaphoreType.DMA((3,))],
        compiler_params=pltpu.CompilerParams(vmem_limit_bytes=VMEM_LIMIT),
    )(buf)


_W_NAMES = ['meta_tokens', 'l0_mix_norm', 'l0_w_in', 'l0_ssd_conv_w', 'l0_ssd_conv_b', 'l0_ssd_dt_bias', 'l0_ssd_a_log',
            'l0_ssd_d', 'l0_ssd_norm', 'l0_ret_norm', 'l0_w_out', 'l0_ffn_norm', 'l0_ffn_w_in', 'l0_ffn_conv_w',
            'l0_ffn_conv_b', 'l0_ffn_w_out', 'l1_mix_norm', 'l1_w_in', 'l1_lru_conv_w', 'l1_lru_conv_b', 'l1_lru_wa',
            'l1_lru_ba', 'l1_lru_wx', 'l1_lru_bx', 'l1_lru_lambda', 'l1_w_out', 'l1_ffn_norm', 'l1_ffn_w_in',
            'l1_ffn_conv_w', 'l1_ffn_conv_b', 'l1_ffn_w_out', 'final_norm']
_IN_NAMES = ['x'] + _W_NAMES + ['loss_target'] + ['m_' + n for n in _W_NAMES] + ['v_' + n for n in _W_NAMES]
_BIG = ['l0_w_in', 'l0_w_out', 'l0_ffn_w_in', 'l0_ffn_w_out', 'l1_w_in', 'l1_w_out', 'l1_ffn_w_in', 'l1_ffn_w_out']
_BIG_COLS = ('l0_w_in', 'l0_ffn_w_in', 'l1_w_in', 'l1_ffn_w_in')
_SMALL_SHARDED = ['meta_tokens', 'l0_ssd_conv_w', 'l0_ffn_conv_w', 'l1_lru_conv_w', 'l1_ffn_conv_w']
_SMALL = [n for n in _W_NAMES if n not in _BIG]


def _pack(arrs):
    flat = []
    for a in arrs:
        v = a.reshape(-1).astype(F32)
        flat.append(jnp.pad(v, (0, (-v.shape[0]) % 128)))
    v = jnp.concatenate(flat)
    v = jnp.pad(v, (0, (-v.shape[0]) % 1024))
    return v.reshape(-1, 128)


def _unpack(buf, shapes):
    out, row = [], 0
    for sh in shapes:
        n = int(np.prod(sh))
        rows = -(-n // 128)
        out.append(buf[row:row + rows].reshape(-1)[:n].reshape(sh))
        row += rows
    return out


def kernel(x, meta_tokens, l0_mix_norm, l0_w_in, l0_ssd_conv_w, l0_ssd_conv_b, l0_ssd_dt_bias, l0_ssd_a_log, l0_ssd_d, l0_ssd_norm, l0_ret_norm, l0_w_out, l0_ffn_norm, l0_ffn_w_in, l0_ffn_conv_w, l0_ffn_conv_b, l0_ffn_w_out, l1_mix_norm, l1_w_in, l1_lru_conv_w, l1_lru_conv_b, l1_lru_wa, l1_lru_ba, l1_lru_wx, l1_lru_bx, l1_lru_lambda, l1_w_out, l1_ffn_norm, l1_ffn_w_in, l1_ffn_conv_w, l1_ffn_conv_b, l1_ffn_w_out, final_norm, loss_target, m_meta_tokens, m_l0_mix_norm, m_l0_w_in, m_l0_ssd_conv_w, m_l0_ssd_conv_b, m_l0_ssd_dt_bias, m_l0_ssd_a_log, m_l0_ssd_d, m_l0_ssd_norm, m_l0_ret_norm, m_l0_w_out, m_l0_ffn_norm, m_l0_ffn_w_in, m_l0_ffn_conv_w, m_l0_ffn_conv_b, m_l0_ffn_w_out, m_l1_mix_norm, m_l1_w_in, m_l1_lru_conv_w, m_l1_lru_conv_b, m_l1_lru_wa, m_l1_lru_ba, m_l1_lru_wx, m_l1_lru_bx, m_l1_lru_lambda, m_l1_w_out, m_l1_ffn_norm, m_l1_ffn_w_in, m_l1_ffn_conv_w, m_l1_ffn_conv_b, m_l1_ffn_w_out, m_final_norm, v_meta_tokens, v_l0_mix_norm, v_l0_w_in, v_l0_ssd_conv_w, v_l0_ssd_conv_b, v_l0_ssd_dt_bias, v_l0_ssd_a_log, v_l0_ssd_d, v_l0_ssd_norm, v_l0_ret_norm, v_l0_w_out, v_l0_ffn_norm, v_l0_ffn_w_in, v_l0_ffn_conv_w, v_l0_ffn_conv_b, v_l0_ffn_w_out, v_l1_mix_norm, v_l1_w_in, v_l1_lru_conv_w, v_l1_lru_conv_b, v_l1_lru_wa, v_l1_lru_ba, v_l1_lru_wx, v_l1_lru_bx, v_l1_lru_lambda, v_l1_w_out, v_l1_ffn_norm, v_l1_ffn_w_in, v_l1_ffn_conv_w, v_l1_ffn_conv_b, v_l1_ffn_w_out, v_final_norm):
    args = (x, meta_tokens, l0_mix_norm, l0_w_in, l0_ssd_conv_w, l0_ssd_conv_b, l0_ssd_dt_bias, l0_ssd_a_log, l0_ssd_d, l0_ssd_norm, l0_ret_norm, l0_w_out, l0_ffn_norm, l0_ffn_w_in, l0_ffn_conv_w, l0_ffn_conv_b, l0_ffn_w_out, l1_mix_norm, l1_w_in, l1_lru_conv_w, l1_lru_conv_b, l1_lru_wa, l1_lru_ba, l1_lru_wx, l1_lru_bx, l1_lru_lambda, l1_w_out, l1_ffn_norm, l1_ffn_w_in, l1_ffn_conv_w, l1_ffn_conv_b, l1_ffn_w_out, final_norm, loss_target, m_meta_tokens, m_l0_mix_norm, m_l0_w_in, m_l0_ssd_conv_w, m_l0_ssd_conv_b, m_l0_ssd_dt_bias, m_l0_ssd_a_log, m_l0_ssd_d, m_l0_ssd_norm, m_l0_ret_norm, m_l0_w_out, m_l0_ffn_norm, m_l0_ffn_w_in, m_l0_ffn_conv_w, m_l0_ffn_conv_b, m_l0_ffn_w_out, m_l1_mix_norm, m_l1_w_in, m_l1_lru_conv_w, m_l1_lru_conv_b, m_l1_lru_wa, m_l1_lru_ba, m_l1_lru_wx, m_l1_lru_bx, m_l1_lru_lambda, m_l1_w_out, m_l1_ffn_norm, m_l1_ffn_w_in, m_l1_ffn_conv_w, m_l1_ffn_conv_b, m_l1_ffn_w_out, m_final_norm, v_meta_tokens, v_l0_mix_norm, v_l0_w_in, v_l0_ssd_conv_w, v_l0_ssd_conv_b, v_l0_ssd_dt_bias, v_l0_ssd_a_log, v_l0_ssd_d, v_l0_ssd_norm, v_l0_ret_norm, v_l0_w_out, v_l0_ffn_norm, v_l0_ffn_w_in, v_l0_ffn_conv_w, v_l0_ffn_conv_b, v_l0_ffn_w_out, v_l1_mix_norm, v_l1_w_in, v_l1_lru_conv_w, v_l1_lru_conv_b, v_l1_lru_wa, v_l1_lru_ba, v_l1_lru_wx, v_l1_lru_bx, v_l1_lru_lambda, v_l1_w_out, v_l1_ffn_norm, v_l1_ffn_w_in, v_l1_ffn_conv_w, v_l1_ffn_conv_b, v_l1_ffn_w_out, v_final_norm)
    p = dict(zip(_IN_NAMES, args))
    B, seq, _ = x.shape
    nch = (seq + CH) // CH
    Pn = nch * CH
    R = B * Pn
    chip = 2 * lax.axis_index("x") + lax.axis_index("y")
    row2 = lambda v: v.reshape(1, -1)
    pad128 = lambda v: jnp.pad(v, (0, 128 - v.shape[0])).reshape(1, 128)

    small_shapes = [p[n].shape for n in _SMALL_SHARDED]
    halved = lambda w: w.astype(_MXU).reshape(2, w.shape[0] // 2, w.shape[1])
    mine = {n: halved(p[n]) for n in _BIG}
    mine_small = _pack([p[n] for n in _SMALL_SHARDED])
    W = {}

    def set_weight(n, g):
        g = _fill_own(g, mine[n], chip)
        g = g.reshape(4, -1, g.shape[3])
        W[n] = jnp.concatenate([g[k] for k in range(4)], axis=1) if n in _BIG_COLS else g.reshape(-1, g.shape[2])

    def gather_on(*names):
        return ("gather", [mine[n] for n in names])

    def take_weights(names, got):
        for n, g in zip(names, got):
            set_weight(n, g)

    gathered = _gather_shards([mine['l0_w_in']], mine_small)
    set_weight('l0_w_in', gathered[0])
    g_small = _fill_own(gathered[-1], mine_small, chip)
    per_chip = [_unpack(g_small[k], small_shapes) for k in range(4)]
    for i, n in enumerate(_SMALL_SHARDED):
        W[n] = jnp.concatenate([per_chip[k][i] for k in range(4)], axis=1)
    w0 = W['l0_w_in']
    w0_main = jnp.concatenate([w0[:, 3088:], w0[:, :3072]], axis=1)
    w0_dt = jnp.pad(w0[:, 3072:3088], ((0, 0), (0, 112)))
    cos, sin = _rope_tables(nch)

    meta = jnp.broadcast_to(W['meta_tokens'][None], (B, N_META, D))
    h0 = jnp.concatenate([jnp.zeros((B, PAD, D), F32), meta, x], axis=1).reshape(R, D)
    n0, n0t = _rmsnorm_fwd(h0, row2(p['l0_mix_norm']), "norm_l0_mix")
    u0 = _mm(n0, w0_main, "nn", F32, "l0_in_proj")
    udt = _mm(n0, w0_dt, "nn", F32, "l0_dt_proj")
    a_log, d_skip, dt_bias = pad128(p['l0_ssd_a_log']), pad128(p['l0_ssd_d']), pad128(p['l0_ssd_dt_bias'])
    ssd_cb = row2(p['l0_ssd_conv_b'])
    act, dt, dtt, *got = _ssd_prep(u0, udt, W['l0_ssd_conv_w'], ssd_cb, dt_bias, B, nch, rider=gather_on('l0_w_out'))
    take_weights(['l0_w_out'], got)
    ycat0, ypre, hin, *got = _ssd_fwd(act, u0, dt, dtt, a_log, d_skip, row2(p['l0_ssd_norm']), B, nch,
                                      rider=gather_on('l0_ffn_w_in'))
    take_weights(['l0_ffn_w_in'], got)
    ycat0, opre, rin, *got = _ret_fwd(u0, ycat0, cos, sin, row2(p['l0_ret_norm']), B, nch,
                                      rider=gather_on('l0_ffn_w_out'))
    take_weights(['l0_ffn_w_out'], got)
    h1 = _mm(ycat0, W['l0_w_out'], "nn", F32, "l0_out_proj", add=h0)
    n1, n1t = _rmsnorm_fwd(h1, row2(p['l0_ffn_norm']), "norm_l0_ffn")
    uf0 = _mm(n1, W['l0_ffn_w_in'], "nn", F32, "l0_ffn_in")
    f0_cb = row2(p['l0_ffn_conv_b'])
    a0, *got = _ffn_act_fwd(uf0, W['l0_ffn_conv_w'], f0_cb, B, nch, rider=gather_on('l1_w_in'))
    take_weights(['l1_w_in'], got)
    h2 = _mm(a0, W['l0_ffn_w_out'], "nn", F32, "l0_ffn_out", add=h1)
    n2, n2t = _rmsnorm_fwd(h2, row2(p['l1_mix_norm']), "norm_l1_mix")
    u1 = _mm(n2, W['l1_w_in'], "nn", F32, "l1_in_proj")
    lru = (W['l1_lru_conv_w'], row2(p['l1_lru_conv_b']), p['l1_lru_wa'], row2(p['l1_lru_ba']), p['l1_lru_wx'],
           row2(p['l1_lru_bx']), row2(p['l1_lru_lambda']))
    later = ['l1_w_out', 'l1_ffn_w_in', 'l1_ffn_w_out']
    ycat1, *got = _sb_fwd(u1, B, nch, rider=gather_on(*later))
    take_weights(later, got)
    ycat1, hs = _lru_fwd(u1, ycat1, *lru, B, nch)
    h3 = _mm(ycat1, W['l1_w_out'], "nn", F32, "l1_out_proj", add=h2)
    n3, n3t = _rmsnorm_fwd(h3, row2(p['l1_ffn_norm']), "norm_l1_ffn")
    uf1 = _mm(n3, W['l1_ffn_w_in'], "nn", F32, "l1_ffn_in")
    f1_cb = row2(p['l1_ffn_conv_b'])
    a1, = _ffn_act_fwd(uf1, W['l1_ffn_conv_w'], f1_cb, B, nch)
    h4 = _mm(a1, W['l1_ffn_w_out'], "nn", F32, "l1_ffn_out", add=h3)
    dh4, lossp, dgf = _head(h4, row2(p['final_norm']), p['loss_target'].reshape(B * seq, D), B, nch)
    loss = lax.psum(jnp.sum(lossp[:, 0, 0]), ("x", "y", "c"))

    G = {'final_norm': dgf[:, 0].sum(0)}

    core = lax.axis_index("c").reshape(1).astype(jnp.int32)

    def col_shards(pieces):
        edges = np.cumsum([0] + [q.shape[1] for q in pieces])
        cs = int(edges[-1]) // 4
        shards = []
        for k in range(4):
            lo, hi = k * cs, (k + 1) * cs
            cut = [q[:, max(lo - e0, 0):min(hi - e0, q.shape[1])]
                   for q, e0, e1 in zip(pieces, edges[:-1], edges[1:]) if e0 < hi and e1 > lo]
            shards.append(cut[0] if len(cut) == 1 else jnp.concatenate(cut, axis=1))
        return jnp.stack(shards)

    def chip_sums(names, tag):
        stacked = [G[n] if n in _BIG_COLS else G[n].reshape(4, G[n].shape[0] // 4, G[n].shape[1]) for n in names]
        theirs = _swap_halves(stacked, "swap_halves_" + tag)
        return {n: _chip_sum(g, t, core, "chip_sum_" + n) for n, g, t in zip(names, stacked, theirs)}

    parts = {}

    def scatter_on(names, tag):
        sums = chip_sums(names, tag)
        return sums, ("scatter", [sums[n] for n in names])

    def take_parts(names, sums, got):
        for n, g in zip(names, got):
            parts[n] = _fill_own(g, lax.dynamic_index_in_dim(sums[n], chip, 0, keepdims=False), chip)

    def ffn_bwd(layer, dh_out, h_in, nt_in, uf, a_act, cb, rider=None):
        pre = f"l{layer}_"
        w_in, w_out, cw = W[pre + 'ffn_w_in'], W[pre + 'ffn_w_out'], W[pre + 'ffn_conv_w']
        da = _mm(dh_out, w_out, "nt", F32, pre + "ffn_out_dgrad")
        G[pre + 'ffn_w_out'] = _mm(a_act, dh_out, "tn", F32, pre + "ffn_out_wgrad")
        dug, duu, dwg, dwu, *rode = _ffn_act_bwd(da, uf, cw, cb, nch, pre + "ffn_act_bwd", rider=rider)
        G[pre + 'ffn_conv_w'] = jnp.concatenate([dwg[:3], dwu[:3]], axis=1)
        G[pre + 'ffn_conv_b'] = jnp.concatenate([dwg[7], dwu[7]])
        dn = _mm(dug, w_in, "nt", F32, pre + "ffn_in_dgrad_g")
        dn = _mm(duu, w_in, "nt", F32, pre + "ffn_in_dgrad_u", add=dn, b_off=FFN)
        G[pre + 'ffn_w_in'] = col_shards([_mm(nt_in, dug, "nn", F32, pre + "ffn_in_wgrad_g"),
                                          _mm(nt_in, duu, "nn", F32, pre + "ffn_in_wgrad_u")])
        dh_in, dg = _rmsnorm_bwd(h_in, row2(p[pre + 'ffn_norm']), dn, dh_out, nch, pre + "ffn_norm_bwd")
        G[pre + 'ffn_norm'] = dg[0]
        return dh_in, rode

    dh3, _ = ffn_bwd(1, dh4, h3, n3t, uf1, a1, f1_cb)
    dy1 = _mm(dh3, W['l1_w_out'], "nt", F32, "l1_out_dgrad")
    G['l1_w_out'] = _mm(ycat1, dh3, "tn", F32, "l1_out_wgrad")
    done = ['l1_ffn_w_in', 'l1_ffn_w_out', 'l1_w_out']
    sums, rider = scatter_on(done, "a")
    dq, dkt, dvt, *got = _sb_bwd(dy1, u1, B, nch, rider=rider)
    dk, dv = dkt.T, dvt.T
    take_parts(done, sums, got)
    dgate, dxc, pgl, dwa, dwx = _lru_bwd(dy1, u1, hs, *lru, B, nch)
    dxr, dcw = _conv_bwd(dxc, u1, 4096, W['l1_lru_conv_w'], 4, "l1_lru_conv_bwd")
    pgl = pgl.sum(0)
    G['l1_lru_ba'], G['l1_lru_bx'], G['l1_lru_lambda'] = pgl[0], pgl[1], pgl[2]
    G['l1_lru_wa'], G['l1_lru_wx'] = dwa.sum(0), dwx.sum(0)
    G['l1_lru_conv_w'], G['l1_lru_conv_b'] = dcw[:4], dcw[7]
    du1 = jnp.concatenate([piece.astype(_MXU) for piece in (dq, dk, dv, dgate, dxr)], axis=1)
    dn = _mm(du1, W['l1_w_in'], "nt", F32, "l1_in_dgrad")
    G['l1_w_in'] = col_shards([_mm(n2t, du1, "nn", F32, "l1_in_wgrad")])
    dh2, dg = _rmsnorm_bwd(h2, row2(p['l1_mix_norm']), dn, dh3, nch, "l1_mix_norm_bwd")
    G['l1_mix_norm'] = dg[0]

    dh1, _ = ffn_bwd(0, dh2, h1, n1t, uf0, a0, f0_cb)
    dy0 = _mm(dh1, W['l0_w_out'], "nt", F32, "l0_out_dgrad")
    G['l0_w_out'] = _mm(ycat0, dh1, "tn", F32, "l0_out_wgrad")
    done = ['l1_w_in', 'l0_ffn_w_in', 'l0_ffn_w_out', 'l0_w_out']
    sums, rider = scatter_on(done, "b")
    dz, dxs, dbm, dcm, ddt4, pgs, *got = _ssd_bwd(dy0, ypre, u0, act, dt, dtt, hin, a_log, d_skip,
                                                  row2(p['l0_ssd_norm']), B, nch, rider=rider)
    take_parts(done, sums, got)
    dpre, ddtr, pgd = _ssd_prep_bwd(dxs, dbm, dcm, ddt4, u0, udt, W['l0_ssd_conv_w'], ssd_cb, dt_bias, B, nch)
    dxbc, dcw0 = _conv_bwd(dpre, u0, U0_XBC, W['l0_ssd_conv_w'], 4, "l0_ssd_conv_bwd")
    dqkvg, pgr = _ret_bwd(dy0, u0, opre, rin, cos, sin, row2(p['l0_ret_norm']), B, nch)
    pgs = pgs.sum(0)
    G['l0_ssd_norm'] = pgs[:, 0, :].reshape(-1)
    G['l0_ssd_d'] = pgs[:, 1, :128].sum(0)[:SSD_HEADS]
    G['l0_ssd_a_log'] = pgs[:, 2, :128].sum(0)[:SSD_HEADS]
    G['l0_ssd_dt_bias'] = pgd.sum(0)[0, :SSD_HEADS]
    G['l0_ssd_conv_w'], G['l0_ssd_conv_b'] = dcw0[:4], dcw0[7]
    G['l0_ret_norm'] = pgr.sum(0)[0]
    dn = _mm(dqkvg, w0_main, "nt", F32, "l0_in_dgrad_qkvg")
    dn = _mm(dz, w0_main, "nt", F32, "l0_in_dgrad_z", add=dn, b_off=U0_Z)
    dn = _mm(dxbc, w0_main, "nt", F32, "l0_in_dgrad_xbc", add=dn, b_off=U0_XBC)
    dn = _mm(ddtr, w0_dt, "nt", F32, "l0_in_dgrad_dt", add=dn)
    G['l0_w_in'] = col_shards([
        _mm(n0t, dz, "nn", F32, "l0_in_wgrad_z"), _mm(n0t, dxbc, "nn", F32, "l0_in_wgrad_xbc"),
        _mm(n0t, ddtr, "nn", F32, "l0_in_wgrad_dt")[:, :SSD_HEADS], _mm(n0t, dqkvg, "nn", F32, "l0_in_wgrad_qkvg")])
    dh0, dg = _rmsnorm_bwd(h0, row2(p['l0_mix_norm']), dn, dh1, nch, "l0_mix_norm_bwd")
    G['l0_mix_norm'] = dg[0]
    dh0 = dh0.reshape(B, Pn, D)
    grad_x = dh0[:, CH:]
    G['meta_tokens'] = dh0[:, PAD:CH].sum(0)

    sums = chip_sums(['l0_w_in'], "d")
    parts['l0_w_in'], = _scatter_sums([sums['l0_w_in']])
    reds = [_sum_chips(parts[n], "sum_chips_" + n) for n in _BIG]
    grads = {}
    for n, own, other in zip(_BIG, reds, _join_halves(reds)):
        both = jnp.where(core[0] == 0, jnp.stack([own, other]), jnp.stack([other, own]))
        grads[n] = both.reshape(-1, both.shape[2])
    small_full = _unpack(_allreduce_small(_pack([G[n] for n in _SMALL])), [G[n].shape for n in _SMALL])
    for n, g in zip(_SMALL, small_full):
        if n in _SMALL_SHARDED:
            cs = g.shape[1] // 4
            g = lax.dynamic_slice_in_dim(g, chip * cs, cs, axis=1)
        grads[n] = g.reshape(p[n].shape)

    delta, new_m, new_v = {}, {}, {}
    for n in _BIG:
        delta[n], new_m[n], new_v[n] = _adamw(p[n], grads[n], p['m_' + n], p['v_' + n], "adamw_" + n)
    shapes = [p[n].shape for n in _SMALL]
    outs = _adamw(_pack([p[n] for n in _SMALL]), _pack([grads[n] for n in _SMALL]), _pack([p['m_' + n] for n in _SMALL]),
                  _pack([p['v_' + n] for n in _SMALL]), "adamw_small")
    for dst, buf in zip((delta, new_m, new_v), outs):
        for n, a in zip(_SMALL, _unpack(buf, shapes)):
            dst[n] = a
    return (loss, grad_x, *[grads[n] for n in _W_NAMES], *[delta[n] for n in _W_NAMES],
            *[new_m[n] for n in _W_NAMES], *[new_v[n] for n in _W_NAMES])
```

```python
import math

import numpy as np
import jax
import jax.numpy as jnp
from jax import lax
from jax.experimental import pallas as pl
from jax.experimental.pallas import tpu as pltpu

F32 = jnp.float32
BF16 = jnp.bfloat16
_MXU = jnp.bfloat16

D = 1024
CH = 128
N_META = 16
PAD = CH - N_META
EPS = 1e-6

SSD_HEADS = 16
SSD_HD = 64
SSD_GROUPS = 4
RET_HEADS = 4
RET_DK = 256
SB_HEADS = 16
SB_HD = 64
LRU_BLOCKS = 8
LRU_C = 8.0
FFN = 2816
U0_Z = 4096
U0_XBC = 5120

VMEM_LIMIT = 56 * 1024 * 1024


def _cparams(sem):
    return pltpu.CompilerParams(dimension_semantics=sem, vmem_limit_bytes=VMEM_LIMIT)


def _dot(a, b, dims=((1,), (0,))):
    return lax.dot_general(a.astype(_MXU), b.astype(_MXU), (dims, ((), ())), preferred_element_type=F32)


def _dot_nt(a, b):
    return _dot(a, b, ((1,), (1,)))


def _dot_tn(a, b):
    return _dot(a.T, b)


def _dot_exact(a, b):
    return lax.dot_general(a, b, (((1,), (0,)), ((), ())), preferred_element_type=F32,
                           precision=lax.Precision.HIGHEST)


def _dot_split(x, m01):
    hi = x.astype(BF16)
    lo = (x - hi.astype(F32)).astype(BF16)
    m = m01.astype(BF16)
    return jnp.dot(hi, m, preferred_element_type=F32) + jnp.dot(lo, m, preferred_element_type=F32)


def _sigmoid(x):
    return 0.5 * jnp.tanh(0.5 * x) + 0.5


def _softplus(x):
    return jnp.maximum(x, 0.0) + jnp.log1p(jnp.exp(-jnp.abs(x)))


def _silu(x):
    return x * _sigmoid(x)


def _dsilu(x):
    s = _sigmoid(x)
    return s * (1.0 + x * (1.0 - s))


_GELU_C = math.sqrt(2.0 / math.pi)


def _gelu(x):
    return 0.5 * x * (1.0 + jnp.tanh(_GELU_C * (x + 0.044715 * x * x * x)))


def _dgelu(x):
    t = jnp.tanh(_GELU_C * (x + 0.044715 * x * x * x))
    return 0.5 * (1.0 + t) + 0.5 * x * (1.0 - t * t) * _GELU_C * (1.0 + 3.0 * 0.044715 * x * x)


def _row_ids(n, cols=1):
    return lax.broadcasted_iota(jnp.int32, (n, cols), 0)


def _lane_ids(rows, n):
    return lax.broadcasted_iota(jnp.int32, (rows, n), 1)


def _real_rows(chunk):
    return chunk * CH + _row_ids(CH) >= PAD


def _shift_down(prev8, cur, s):
    cat = jnp.concatenate([prev8, cur], axis=0)
    return pltpu.roll(cat, s, axis=0)[8:]


def _shift_up(cur, next8, s):
    n = cur.shape[0]
    cat = jnp.concatenate([cur, next8], axis=0)
    return pltpu.roll(cat, n + 8 - s, axis=0)[:n]


def _conv_pre(prev8, cur, w_ref, b_ref, K):
    acc = cur * w_ref[K - 1:K, :] + b_ref[...]
    for s in range(1, K):
        acc = acc + _shift_down(prev8, cur, s) * w_ref[K - 1 - s:K - s, :]
    return acc


def _prev8_map(nch, col):
    return lambda b, c: (jnp.maximum((b * nch + c) * (CH // 8) - 1, 0), col)


def _matmul(a, b, mode, out_dtype, tm, tn, tk, name, add=None, b_off=0):
    if mode == "nn":
        (M, K), (_, N) = a.shape, b.shape
    elif mode == "nt":
        (M, K), N = a.shape, b.shape[0]
    else:
        (K, M), (_, N) = a.shape, b.shape
    tm, tn, tk = min(tm, M), min(tn, N), min(tk, K)
    assert M % tm == 0 and N % tn == 0 and K % tk == 0 and b_off % tk == 0, (name, M, N, K, tm, tn, tk)
    koff = b_off // tk
    nk = K // tk
    dims = {"nn": ((1,), (0,)), "nt": ((1,), (1,)), "tn": ((0,), (0,))}[mode]
    if mode == "tn":
        a_spec = pl.BlockSpec((tk, tm), lambda i, j, k: (k, i))
    else:
        a_spec = pl.BlockSpec((tm, tk), lambda i, j, k: (i, k))
    if mode == "nt":
        b_spec = pl.BlockSpec((tn, tk), lambda i, j, k: (j, k + koff))
    else:
        b_spec = pl.BlockSpec((tk, tn), lambda i, j, k: (k, j))
    o_spec = pl.BlockSpec((tm, tn), lambda i, j, k: (i, j))
    has_add = add is not None

    def body(a_ref, b_ref, *rest):
        if has_add:
            add_ref, o_ref, acc = rest
        else:
            o_ref, acc = rest
        k = pl.program_id(2)

        @pl.when(k == 0)
        def _():
            acc[...] = jnp.zeros_like(acc)

        acc[...] += _dot(a_ref[...], b_ref[...], dims)

        @pl.when(k == nk - 1)
        def _():
            r = acc[...]
            if has_add:
                r = r + add_ref[...].astype(F32)
            o_ref[...] = r.astype(out_dtype)

    in_specs = [a_spec, b_spec] + ([o_spec] if has_add else [])
    args = (a, b) + ((add,) if has_add else ())
    return pl.pallas_call(
        body, name=name, grid=(M // tm, N // tn, nk),
        in_specs=in_specs, out_specs=o_spec,
        out_shape=jax.ShapeDtypeStruct((M, N), out_dtype),
        scratch_shapes=[pltpu.VMEM((tm, tn), F32)],
        compiler_params=_cparams(("parallel", "parallel", "arbitrary")),
    )(*args)


def _tile(n, prefs):
    for t in prefs:
        if n % t == 0:
            return t
    return n


def _mm(a, b, mode, out_dtype, name, add=None, b_off=0):
    if mode == "tn":
        K, M = a.shape
        N = b.shape[1]
        tm, tn, tk = _tile(M, (1024, 1408, 512, 256, 128)), _tile(N, (1024, 1408, 512, 256, 128)), _tile(K, (2176, 384, 256, 128))
    else:
        M, K = a.shape
        N = b.shape[1] if mode == "nn" else b.shape[0]
        tm = _tile(M, (1088, 1024, 768, 512, 384, 256, 128))
        tn = _tile(N, (1024, 1408, 512, 256, 128))
        tk = _tile(K, (2176, 1024, 1408, 512, 256, 128))
    return _matmul(a, b, mode, out_dtype, tm, tn, tk, name, add=add, b_off=b_off)


def _rmsnorm_fwd(h, g, name):
    R = h.shape[0]
    tr = 2 * CH

    def body(h_ref, g_ref, o_ref, ot_ref):
        x = h_ref[...]
        r = lax.rsqrt(jnp.mean(x * x, axis=-1, keepdims=True) + EPS)
        y = x * r * g_ref[...]
        o_ref[...] = y.astype(o_ref.dtype)
        ot_ref[...] = y.T.astype(ot_ref.dtype)

    return pl.pallas_call(
        body, name=name, grid=(R // tr,),
        in_specs=[pl.BlockSpec((tr, D), lambda i: (i, 0)), pl.BlockSpec((1, D), lambda i: (0, 0))],
        out_specs=[pl.BlockSpec((tr, D), lambda i: (i, 0)), pl.BlockSpec((D, tr), lambda i: (0, i))],
        out_shape=[jax.ShapeDtypeStruct((R, D), _MXU), jax.ShapeDtypeStruct((D, R), _MXU)],
        compiler_params=_cparams(("parallel",)),
    )(h, g)


def _rmsnorm_bwd(h, g, dn, dres, nch, name):
    R = h.shape[0]
    per = 4
    tr = nch * CH // per

    def body(h_ref, g_ref, dn_ref, dres_ref, dh_ref, dg_ref):
        i = pl.program_id(0)
        x = h_ref[...]
        r = lax.rsqrt(jnp.mean(x * x, axis=-1, keepdims=True) + EPS)
        xhat = x * r
        dn_v = dn_ref[...]
        dx = dn_v * g_ref[...]
        dh = r * (dx - xhat * jnp.mean(dx * xhat, axis=-1, keepdims=True))
        keep = (i % per) * tr + _row_ids(tr) >= PAD
        dh_ref[...] = jnp.where(keep, dres_ref[...] + dh, 0.0)

        @pl.when(i == 0)
        def _():
            dg_ref[...] = jnp.zeros_like(dg_ref)

        dg_ref[...] += jnp.sum(dn_v * xhat, axis=0, keepdims=True)

    row = pl.BlockSpec((tr, D), lambda i: (i, 0))
    vec = pl.BlockSpec((1, D), lambda i: (0, 0))
    return pl.pallas_call(
        body, name=name, grid=(R // tr,),
        in_specs=[row, vec, row, row], out_specs=[row, vec],
        out_shape=[jax.ShapeDtypeStruct((R, D), F32), jax.ShapeDtypeStruct((1, D), F32)],
        compiler_params=_cparams(("arbitrary",)),
    )(h, g, dn, dres)


def _ssd_prep(u0, udt, conv_w, conv_b, dt_bias, B, nch, rider=None):
    R = u0.shape[0]

    def body(xs_ref, xsp_ref, bc_ref, bcp_ref, udt_ref, w0_ref, w1_ref, b0_ref, b1_ref, dtb_ref,
             act_ref, dt_ref, dtt_ref):
        keep = _real_rows(pl.program_id(1))
        a0 = _silu(_conv_pre(xsp_ref[...], xs_ref[...], w0_ref, b0_ref, 4))
        a1 = _silu(_conv_pre(bcp_ref[...], bc_ref[...], w1_ref, b1_ref, 4))
        act_ref[:, :1024] = jnp.where(keep, a0, 0.0)
        act_ref[:, 1024:] = jnp.where(keep, a1, 0.0)
        ok = jnp.logical_and(keep, _lane_ids(1, 128) < SSD_HEADS)
        dt = jnp.where(ok, _softplus(udt_ref[...] + dtb_ref[...]), 0.0)
        dt_ref[...] = dt
        dtt_ref[...] = dt.T

    row = lambda col: pl.BlockSpec((CH, 1024), lambda b, c: (b * nch + c, col))
    prev = lambda col: pl.BlockSpec((8, 1024), _prev8_map(nch, col))
    kw = dict(
        grid=(B, nch),
        in_specs=[row(5), prev(5), row(6), prev(6),
                  pl.BlockSpec((CH, 128), lambda b, c: (b * nch + c, 0)),
                  pl.BlockSpec((4, 1024), lambda b, c: (0, 0)), pl.BlockSpec((4, 1024), lambda b, c: (0, 1)),
                  pl.BlockSpec((1, 1024), lambda b, c: (0, 0)), pl.BlockSpec((1, 1024), lambda b, c: (0, 1)),
                  pl.BlockSpec((1, 128), lambda b, c: (0, 0))],
        out_specs=[pl.BlockSpec((CH, 2048), lambda b, c: (b * nch + c, 0)),
                   pl.BlockSpec((CH, 128), lambda b, c: (b * nch + c, 0)),
                   pl.BlockSpec((128, CH), lambda b, c: (0, b * nch + c))],
        out_shape=[jax.ShapeDtypeStruct((R, 2048), F32), jax.ShapeDtypeStruct((R, 128), F32),
                   jax.ShapeDtypeStruct((128, R), F32)])
    return _call(body, "ssd_prep", ("arbitrary", "arbitrary"), kw,
                 (u0, u0, u0, u0, udt, conv_w, conv_w, conv_b, conv_b, dt_bias), rider)


def _ssd_head_terms(h, a_vec, dt_v, dtt_v, dsk_v):
    lane = _lane_ids(1, 128)
    sub = _row_ids(128)
    r = _row_ids(CH, CH)
    cidx = _lane_ids(CH, CH)
    a_h = jnp.sum(jnp.where(lane == h, a_vec, 0.0), axis=1, keepdims=True)
    dt_col = jnp.sum(jnp.where(lane == h, dt_v, 0.0), axis=1, keepdims=True)
    dt_row = jnp.sum(jnp.where(sub == h, dtt_v, 0.0), axis=0, keepdims=True)
    cs_col = jnp.sum(jnp.where(r >= cidx, dt_row * a_h, 0.0), axis=1, keepdims=True)
    cs_row = jnp.sum(jnp.where(r <= cidx, dt_col * a_h, 0.0), axis=0, keepdims=True)
    tot = jnp.sum(dt_col * a_h, axis=0, keepdims=True)
    dsk = jnp.sum(jnp.where(lane == h, dsk_v, 0.0), axis=1, keepdims=True)
    return a_h, dt_col, cs_col, cs_row, tot, dsk


def _ssd_fwd(act, u0, dt, dtt, a_log, d_skip, norm_g, B, nch, rider=None):
    R = act.shape[0]

    def body(xs_ref, bm_ref, cm_ref, z_ref, dt_ref, dtt_ref, alog_ref, dsk_ref, ng_ref,
             out_ref, ypre_ref, hin_ref, H):
        g = pl.program_id(1)
        c = pl.program_id(2)

        @pl.when(c == 0)
        def _():
            H[...] = jnp.zeros_like(H)

        hin_ref[...] = H[...]
        a_vec = -jnp.exp(alog_ref[...])
        dt_v = dt_ref[...]
        dtt_v = dtt_ref[...]
        hm = _lane_ids(1, 128) < SSD_HD
        r = _row_ids(CH, CH)
        cidx = _lane_ids(CH, CH)
        Bm = bm_ref[...]
        Cm = cm_ref[...]
        CB = _dot_nt(Cm, Bm)
        ys = []
        for pair in range(2):
            cols = slice(128 * pair, 128 * pair + 128)
            xraw = xs_ref[:, cols]
            t = [_ssd_head_terms(4 * g + 2 * pair + j, a_vec, dt_v, dtt_v, dsk_ref[...]) for j in range(2)]
            sel = lambda f: jnp.where(hm, f(t[0]), f(t[1]))
            dtp = sel(lambda q: q[1])
            Ep = sel(lambda q: jnp.exp(q[2]))
            Wp = sel(lambda q: jnp.exp(q[4] - q[2]))
            etot = sel(lambda q: jnp.exp(q[4]))
            dsk = sel(lambda q: q[5])
            X = xraw * dtp
            ydiag = jnp.zeros((CH, 128), F32)
            for j in range(2):
                Lm = jnp.where(r >= cidx, jnp.exp(t[j][2] - t[j][3]), 0.0)
                Xh = jnp.where(hm if j == 0 else jnp.logical_not(hm), X, 0.0)
                ydiag = ydiag + _dot(CB * Lm, Xh)
            Hp = H[:, cols]
            yoff = Ep * _dot(Cm, Hp)
            S = _dot(Bm.T, X * Wp)
            H[:, cols] = etot * Hp + S
            ys.append(ydiag + yoff + xraw * dsk)
        y = jnp.concatenate(ys, axis=1)
        ypre_ref[...] = y
        yg = y * _silu(z_ref[...])
        rr = lax.rsqrt(jnp.mean(yg * yg, axis=-1, keepdims=True) + EPS)
        out_ref[...] = jnp.where(_real_rows(c), yg * rr * ng_ref[...], 0.0).astype(out_ref.dtype)

    rowb = lambda w, colf: pl.BlockSpec((CH, w), lambda b, g, c: (b * nch + c, colf(g)))
    vec = pl.BlockSpec((1, 128), lambda b, g, c: (0, 0))
    kw = dict(
        grid=(B, SSD_GROUPS, nch),
        in_specs=[rowb(256, lambda g: g), rowb(128, lambda g: 8 + g), rowb(128, lambda g: 12 + g),
                  rowb(256, lambda g: 16 + g), rowb(128, lambda g: 0),
                  pl.BlockSpec((128, CH), lambda b, g, c: (0, b * nch + c)),
                  vec, vec, pl.BlockSpec((1, 256), lambda b, g, c: (0, g))],
        out_specs=[rowb(256, lambda g: g), rowb(256, lambda g: g),
                   pl.BlockSpec((None, None, None, 128, 256), lambda b, g, c: (b, g, c, 0, 0))],
        out_shape=[jax.ShapeDtypeStruct((R, 2048), _MXU), jax.ShapeDtypeStruct((R, 1024), F32),
                   jax.ShapeDtypeStruct((B, SSD_GROUPS, nch, 128, 256), F32)],
        scratch_shapes=[pltpu.VMEM((128, 256), F32)])
    return _call(body, "ssd_fwd", ("arbitrary", "arbitrary", "arbitrary"), kw,
                 (act, act, act, u0, dt, dtt, a_log, d_skip, norm_g), rider)


def _ssd_bwd(dycat, ypre, u0, act, dt, dtt, hin, a_log, d_skip, norm_g, B, nch, rider=None):
    R = act.shape[0]

    def body(dy_ref, ypre_ref, z_ref, xs_ref, bm_ref, cm_ref, dt_ref, dtt_ref, hin_ref, alog_ref, dsk_ref, ng_ref,
             dz_ref, dxs_ref, db_ref, dc_ref, ddt_ref, pg_ref, dH):
        g = pl.program_id(1)
        c = nch - 1 - pl.program_id(2)

        @pl.when(pl.program_id(2) == 0)
        def _():
            dH[...] = jnp.zeros_like(dH)
            pg_ref[...] = jnp.zeros_like(pg_ref)

        z = z_ref[...]
        y = ypre_ref[...]
        ng = ng_ref[...]
        dout = jnp.where(_real_rows(c), dy_ref[...], 0.0)
        sz = _sigmoid(z)
        yg = y * z * sz
        rr = lax.rsqrt(jnp.mean(yg * yg, axis=-1, keepdims=True) + EPS)
        nrm = yg * rr
        pg_ref[0:1, :] += jnp.sum(dout * nrm, axis=0, keepdims=True)
        dn = dout * ng
        dyg = rr * (dn - nrm * jnp.mean(dn * nrm, axis=-1, keepdims=True))
        dy = dyg * z * sz
        dz_ref[...] = (dyg * y * (sz * (1.0 + z * (1.0 - sz)))).astype(dz_ref.dtype)

        a_vec = -jnp.exp(alog_ref[...])
        dt_v = dt_ref[...]
        dtt_v = dtt_ref[...]
        lane = _lane_ids(1, 128)
        hm = lane < SSD_HD
        r = _row_ids(CH, CH)
        cidx = _lane_ids(CH, CH)
        last = _row_ids(CH) == CH - 1
        Bm = bm_ref[...]
        Cm = cm_ref[...]
        CB = _dot_nt(Cm, Bm)
        CBT = _dot_nt(Bm, Cm)
        dB = jnp.zeros((CH, 128), F32)
        dC = jnp.zeros((CH, 128), F32)
        dcs_all = jnp.zeros((CH, 128), F32)
        dtx_all = jnp.zeros((CH, 128), F32)
        dd_row = jnp.zeros((1, 128), F32)
        dxs = []
        for pair in range(2):
            cols = slice(128 * pair, 128 * pair + 128)
            xraw = xs_ref[:, cols]
            dyp = dy[:, cols]
            heads = [4 * g + 2 * pair + j for j in range(2)]
            t = [_ssd_head_terms(heads[j], a_vec, dt_v, dtt_v, dsk_ref[...]) for j in range(2)]
            sel = lambda f: jnp.where(hm, f(t[0]), f(t[1]))
            hsum = lambda v, j: jnp.sum(jnp.where(hm if j == 0 else jnp.logical_not(hm), v, 0.0), axis=1, keepdims=True)
            dtp = sel(lambda q: q[1])
            Ep = sel(lambda q: jnp.exp(q[2]))
            Wp = sel(lambda q: jnp.exp(q[4] - q[2]))
            etot = sel(lambda q: jnp.exp(q[4]))
            dsk = sel(lambda q: q[5])
            X = xraw * dtp
            Hp = hin_ref[:, cols]
            dHn = dH[:, cols]
            dskip = jnp.sum(dyp * xraw, axis=0, keepdims=True)
            yoff = Ep * _dot(Cm, Hp)
            dE = dyp * yoff
            dC = dC + _dot_nt(dyp * Ep, Hp)
            dH[:, cols] = etot * dHn + _dot(Cm.T, dyp * Ep)
            BdS = _dot(Bm, dHn)
            dX = Wp * BdS
            ew = X * BdS * Wp
            dB = dB + _dot_nt(X * Wp, dHn)
            hh = jnp.sum(dHn * Hp, axis=0, keepdims=True) * etot
            for j in range(2):
                hmask = hm if j == 0 else jnp.logical_not(hm)
                cs_col, cs_row = t[j][2], t[j][3]
                Lm = jnp.where(r >= cidx, jnp.exp(cs_col - cs_row), 0.0)
                LmT = jnp.where(cidx >= r, jnp.exp(cs_row - cs_col), 0.0)
                dyh = jnp.where(hmask, dyp, 0.0)
                Xh = jnp.where(hmask, X, 0.0)
                dM = _dot_nt(dyh, Xh)
                dMT = _dot_nt(Xh, dyh)
                M = CB * Lm
                MT = CBT * LmT
                dX = dX + _dot(MT, dyh)
                dC = dC + _dot(dM * Lm, Bm)
                dB = dB + _dot(dMT * LmT, Cm)
                g_rows = jnp.sum(dM * M, axis=1, keepdims=True)
                g_cols = jnp.sum(dMT * MT, axis=1, keepdims=True)
                dtot = (jnp.sum(hsum(ew, j), axis=0, keepdims=True)
                        + jnp.sum(jnp.where(hmask, hh, 0.0), axis=1, keepdims=True))
                dcs = g_rows - g_cols + hsum(dE, j) - hsum(ew, j) + jnp.where(last, dtot, 0.0)
                dcs_all = dcs_all + jnp.where(lane == heads[j], dcs, 0.0)
                dtx_all = dtx_all + jnp.where(lane == heads[j], hsum(dX * xraw, j), 0.0)
                dd_row = dd_row + jnp.where(lane == heads[j],
                                            jnp.sum(jnp.where(hmask, dskip, 0.0), axis=1, keepdims=True), 0.0)
            dxs.append(dX * dtp + dyp * dsk)
        dxs_ref[...] = jnp.concatenate(dxs, axis=1)
        db_ref[...] = dB
        dc_ref[...] = dC
        dadt = _dot_exact(jnp.where(cidx >= r, 1.0, 0.0), dcs_all)
        ddt_ref[...] = dadt * a_vec + dtx_all
        pg_ref[1:2, 0:128] += dd_row
        pg_ref[2:3, 0:128] += jnp.sum(dadt * dt_v, axis=0, keepdims=True) * a_vec

    rowb = lambda w, colf: pl.BlockSpec((CH, w), lambda b, g, c: (b * nch + nch - 1 - c, colf(g)))
    vec = pl.BlockSpec((1, 128), lambda b, g, c: (0, 0))
    kw = dict(
        grid=(B, SSD_GROUPS, nch),
        in_specs=[rowb(256, lambda g: g), rowb(256, lambda g: g), rowb(256, lambda g: 16 + g), rowb(256, lambda g: g),
                  rowb(128, lambda g: 8 + g), rowb(128, lambda g: 12 + g), rowb(128, lambda g: 0),
                  pl.BlockSpec((128, CH), lambda b, g, c: (0, b * nch + nch - 1 - c)),
                  pl.BlockSpec((None, None, None, 128, 256), lambda b, g, c: (b, g, nch - 1 - c, 0, 0)),
                  vec, vec, pl.BlockSpec((1, 256), lambda b, g, c: (0, g))],
        out_specs=[rowb(256, lambda g: g), rowb(256, lambda g: g), rowb(128, lambda g: g), rowb(128, lambda g: g),
                   rowb(128, lambda g: g),
                   pl.BlockSpec((None, None, 8, 256), lambda b, g, c: (b, g, 0, 0))],
        out_shape=[jax.ShapeDtypeStruct((R, 1024), _MXU), jax.ShapeDtypeStruct((R, 1024), F32),
                   jax.ShapeDtypeStruct((R, 512), F32), jax.ShapeDtypeStruct((R, 512), F32),
                   jax.ShapeDtypeStruct((R, 512), F32), jax.ShapeDtypeStruct((B, SSD_GROUPS, 8, 256), F32)],
        scratch_shapes=[pltpu.VMEM((128, 256), F32)])
    return _call(body, "ssd_bwd", ("arbitrary", "arbitrary", "arbitrary"), kw,
                 (dycat, ypre, u0, act, act, act, dt, dtt, hin, a_log, d_skip, norm_g), rider)


def _ssd_prep_bwd(dxs, dB, dC, ddt4, u0, udt, conv_w, conv_b, dt_bias, B, nch, rider=None):
    R = u0.shape[0]

    def body(dxs_ref, db_ref, dc_ref, ddt_ref, xs_ref, xsp_ref, bc_ref, bcp_ref, udt_ref, w0_ref, w1_ref, b0_ref, b1_ref,
             dtb_ref, dpre_ref, ddtr_ref, pgd_ref):
        c = pl.program_id(1)

        @pl.when(c == 0)
        def _():
            pgd_ref[...] = jnp.zeros_like(pgd_ref)

        keep = _real_rows(c)
        p0 = _conv_pre(xsp_ref[...], xs_ref[...], w0_ref, b0_ref, 4)
        p1 = _conv_pre(bcp_ref[...], bc_ref[...], w1_ref, b1_ref, 4)
        dpre_ref[:, :1024] = jnp.where(keep, dxs_ref[...] * _dsilu(p0), 0.0)
        dpre_ref[:, 1024:] = jnp.where(keep, jnp.concatenate([db_ref[...], dc_ref[...]], axis=1) * _dsilu(p1), 0.0)
        ddt = ddt_ref[:, 0:128] + ddt_ref[:, 128:256] + ddt_ref[:, 256:384] + ddt_ref[:, 384:512]
        ok = jnp.logical_and(keep, _lane_ids(1, 128) < SSD_HEADS)
        dr = jnp.where(ok, ddt * _sigmoid(udt_ref[...] + dtb_ref[...]), 0.0)
        ddtr_ref[...] = dr
        pgd_ref[0:1, :] += jnp.sum(dr, axis=0, keepdims=True)

    rw = lambda w: pl.BlockSpec((CH, w), lambda b, c: (b * nch + c, 0))
    row = lambda col: pl.BlockSpec((CH, 1024), lambda b, c: (b * nch + c, col))
    prev = lambda col: pl.BlockSpec((8, 1024), _prev8_map(nch, col))
    kw = dict(
        grid=(B, nch),
        in_specs=[rw(1024), rw(512), rw(512), rw(512), row(5), prev(5), row(6), prev(6), rw(128),
                  pl.BlockSpec((4, 1024), lambda b, c: (0, 0)), pl.BlockSpec((4, 1024), lambda b, c: (0, 1)),
                  pl.BlockSpec((1, 1024), lambda b, c: (0, 0)), pl.BlockSpec((1, 1024), lambda b, c: (0, 1)),
                  pl.BlockSpec((1, 128), lambda b, c: (0, 0))],
        out_specs=[rw(2048), rw(128), pl.BlockSpec((None, 8, 128), lambda b, c: (b, 0, 0))],
        out_shape=[jax.ShapeDtypeStruct((R, 2048), F32), jax.ShapeDtypeStruct((R, 128), F32),
                   jax.ShapeDtypeStruct((B, 8, 128), F32)])
    return _call(body, "ssd_prep_bwd", ("arbitrary", "arbitrary"), kw,
                 (dxs, dB, dC, ddt4, u0, u0, u0, u0, udt, conv_w, conv_w, conv_b, conv_b, dt_bias), rider)


def _conv_bwd(dpre, xin, xin_col, w, K, name, tc=1024):
    R, C = dpre.shape
    assert C % tc == 0 and xin_col % tc == 0
    nr = R // CH
    xoff = xin_col // tc

    def body(dp_ref, dpn_ref, x_ref, xp_ref, w_ref, din_ref, dw_ref):
        i = pl.program_id(1)

        @pl.when(i == 0)
        def _():
            dw_ref[...] = jnp.zeros_like(dw_ref)

        dp = dp_ref[...]
        nxt = dpn_ref[...] * (i < nr - 1).astype(F32)
        x = x_ref[...]
        xp = xp_ref[...]
        din = dp * w_ref[K - 1:K, :]
        dw_ref[K - 1:K, :] += jnp.sum(dp * x, axis=0, keepdims=True)
        dw_ref[7:8, :] += jnp.sum(dp, axis=0, keepdims=True)
        for s in range(1, K):
            din = din + _shift_up(dp, nxt, s) * w_ref[K - 1 - s:K - s, :]
            dw_ref[K - 1 - s:K - s, :] += jnp.sum(dp * _shift_down(xp, x, s), axis=0, keepdims=True)
        din_ref[...] = din.astype(din_ref.dtype)

    return pl.pallas_call(
        body, name=name, grid=(C // tc, nr),
        in_specs=[pl.BlockSpec((CH, tc), lambda j, i: (i, j)),
                  pl.BlockSpec((8, tc), lambda j, i: (jnp.minimum((i + 1) * (CH // 8), nr * (CH // 8) - 1), j)),
                  pl.BlockSpec((CH, tc), lambda j, i: (i, xoff + j)),
                  pl.BlockSpec((8, tc), lambda j, i: (jnp.maximum(i * (CH // 8) - 1, 0), xoff + j)),
                  pl.BlockSpec((K, tc), lambda j, i: (0, j))],
        out_specs=[pl.BlockSpec((CH, tc), lambda j, i: (i, j)),
                   pl.BlockSpec((8, tc), lambda j, i: (0, j))],
        out_shape=[jax.ShapeDtypeStruct((R, C), _MXU), jax.ShapeDtypeStruct((8, C), F32)],
        compiler_params=_cparams(("parallel", "arbitrary")),
    )(dpre, dpre, xin, xin, w)


_RET_LG = [float(v) for v in np.log1p(-np.exp2(-5.0 - np.arange(RET_HEADS, dtype=np.float32))).astype(np.float32)]
_RET_SCALE = RET_DK ** -0.5


def _rope_tables(nch):
    half = RET_DK // 2
    inv_freq = 1.0 / (10000.0 ** (jnp.arange(half, dtype=F32) / (half - 1)))
    pos = jnp.arange(nch * CH, dtype=F32) - PAD
    ang = pos[:, None] * inv_freq[None, :]
    return jnp.cos(ang), jnp.sin(ang)


def _rot(x, cos, sin):
    x1, x2 = x[:, :128], x[:, 128:]
    return jnp.concatenate([x1 * cos - x2 * sin, x1 * sin + x2 * cos], axis=1)


def _unrot(d, cos, sin):
    d1, d2 = d[:, :128], d[:, 128:]
    return jnp.concatenate([d1 * cos + d2 * sin, d2 * cos - d1 * sin], axis=1)


def _ret_decays(lg):
    r = _row_ids(CH, CH)
    cidx = _lane_ids(CH, CH)
    diff = (r - cidx).astype(F32)
    decay = jnp.where(r >= cidx, jnp.exp(lg * jnp.maximum(diff, 0.0)), 0.0)
    decay_t = jnp.where(cidx >= r, jnp.exp(lg * jnp.maximum(-diff, 0.0)), 0.0)
    idx = _row_ids(CH).astype(F32)
    zeta = jnp.exp(lg * (CH - 1.0 - idx))
    xi = jnp.exp(lg * (idx + 1.0))
    return decay, decay_t, zeta, xi


def _ret_fwd(u0, ycat, cos, sin, norm_g, B, nch, rider=None):
    R = u0.shape[0]

    def body(u_ref, cos_ref, sin_ref, ng_ref, ycat_in, out_ref, opre_ref, rin_ref, Rst):
        c = pl.program_id(1)

        @pl.when(c == 0)
        def _():
            Rst[...] = jnp.zeros_like(Rst)

        cos_v, sin_v = cos_ref[...], sin_ref[...]
        for h in range(RET_HEADS):
            lg = _RET_LG[h]
            cols = slice(256 * h, 256 * h + 256)
            decay, _, zeta, xi = _ret_decays(lg)
            qr = _rot(u_ref[:, cols], cos_v, sin_v)
            kr = _rot(u_ref[:, 1024 + 256 * h:1024 + 256 * h + 256], cos_v, sin_v) * _RET_SCALE
            v = u_ref[:, 2048 + 256 * h:2048 + 256 * h + 256]
            gate = u_ref[:, 3072 + 256 * h:3072 + 256 * h + 256]
            Rh = Rst[h]
            rin_ref[h] = Rh
            inner = _dot(_dot_nt(qr, kr) * decay, v)
            cross = _dot(qr, Rh) * xi
            Rst[h] = math.exp(CH * lg) * Rh + _dot((kr * zeta).T, v)
            o = inner + cross
            opre_ref[:, cols] = o
            oc = o - jnp.mean(o, axis=-1, keepdims=True)
            rr = lax.rsqrt(jnp.mean(oc * oc, axis=-1, keepdims=True) + EPS)
            out_ref[:, cols] = (_silu(gate) * (oc * rr * ng_ref[:, cols])).astype(out_ref.dtype)

    kw = dict(
        grid=(B, nch),
        in_specs=[pl.BlockSpec((CH, 4096), lambda b, c: (b * nch + c, 0)),
                  pl.BlockSpec((CH, 128), lambda b, c: (c, 0)), pl.BlockSpec((CH, 128), lambda b, c: (c, 0)),
                  pl.BlockSpec((1, 1024), lambda b, c: (0, 0)),
                  pl.BlockSpec(memory_space=pl.ANY)],
        out_specs=[pl.BlockSpec((CH, 1024), lambda b, c: (b * nch + c, 1)),
                   pl.BlockSpec((CH, 1024), lambda b, c: (b * nch + c, 0)),
                   pl.BlockSpec((None, None, RET_HEADS, 256, 256), lambda b, c: (b, c, 0, 0, 0))],
        out_shape=[jax.ShapeDtypeStruct(ycat.shape, ycat.dtype), jax.ShapeDtypeStruct((R, 1024), F32),
                   jax.ShapeDtypeStruct((B, nch, RET_HEADS, 256, 256), F32)],
        scratch_shapes=[pltpu.VMEM((RET_HEADS, 256, 256), F32)],
        input_output_aliases={4: 0})
    return _call(body, "ret_fwd", ("arbitrary", "arbitrary"), kw, (u0, cos, sin, norm_g, ycat), rider)


def _ret_bwd(dycat, u0, opre, rin, cos, sin, norm_g, B, nch, rider=None):
    R = u0.shape[0]

    def body(dy_ref, u_ref, opre_ref, rin_ref, cos_ref, sin_ref, ng_ref, du_ref, pg_ref, dR):
        @pl.when(pl.program_id(1) == 0)
        def _():
            dR[...] = jnp.zeros_like(dR)
            pg_ref[...] = jnp.zeros_like(pg_ref)

        cos_v, sin_v = cos_ref[...], sin_ref[...]
        for h in range(RET_HEADS):
            lg = _RET_LG[h]
            cols = slice(256 * h, 256 * h + 256)
            decay, decay_t, zeta, xi = _ret_decays(lg)
            qr = _rot(u_ref[:, cols], cos_v, sin_v)
            kr = _rot(u_ref[:, 1024 + 256 * h:1024 + 256 * h + 256], cos_v, sin_v) * _RET_SCALE
            v = u_ref[:, 2048 + 256 * h:2048 + 256 * h + 256]
            gate = u_ref[:, 3072 + 256 * h:3072 + 256 * h + 256]
            ng = ng_ref[:, cols]
            o = opre_ref[:, cols]
            oc = o - jnp.mean(o, axis=-1, keepdims=True)
            rr = lax.rsqrt(jnp.mean(oc * oc, axis=-1, keepdims=True) + EPS)
            ohat = oc * rr
            dout = dy_ref[:, cols]
            du_ref[:, 3072 + 256 * h:3072 + 256 * h + 256] = (dout * (ohat * ng) * _dsilu(gate)).astype(du_ref.dtype)
            don = dout * _silu(gate)
            pg_ref[0:1, cols] += jnp.sum(don * ohat, axis=0, keepdims=True)
            dohat = don * ng
            do = rr * (dohat - jnp.mean(dohat, axis=-1, keepdims=True)
                       - ohat * jnp.mean(dohat * ohat, axis=-1, keepdims=True))
            Rh = rin_ref[h]
            dRn = dR[h]
            sc_t = _dot_nt(kr, qr) * decay_t
            dv = _dot(sc_t, do) + _dot(kr * zeta, dRn)
            ds = _dot_nt(do, v) * decay
            ds_t = _dot_nt(v, do) * decay_t
            dox = do * xi
            dq = _dot(ds, kr) + _dot_nt(dox, Rh)
            dk = _dot(ds_t, qr) + zeta * _dot_nt(v, dRn)
            dR[h] = math.exp(CH * lg) * dRn + _dot(qr.T, dox)
            du_ref[:, cols] = _unrot(dq, cos_v, sin_v).astype(du_ref.dtype)
            du_ref[:, 1024 + 256 * h:1024 + 256 * h + 256] = (_unrot(dk, cos_v, sin_v) * _RET_SCALE).astype(du_ref.dtype)
            du_ref[:, 2048 + 256 * h:2048 + 256 * h + 256] = dv.astype(du_ref.dtype)

    rmap = lambda b, c: (b * nch + nch - 1 - c, 0)
    kw = dict(
        grid=(B, nch),
        in_specs=[pl.BlockSpec((CH, 1024), lambda b, c: (b * nch + nch - 1 - c, 1)),
                  pl.BlockSpec((CH, 4096), rmap), pl.BlockSpec((CH, 1024), rmap),
                  pl.BlockSpec((None, None, RET_HEADS, 256, 256), lambda b, c: (b, nch - 1 - c, 0, 0, 0)),
                  pl.BlockSpec((CH, 128), lambda b, c: (nch - 1 - c, 0)),
                  pl.BlockSpec((CH, 128), lambda b, c: (nch - 1 - c, 0)),
                  pl.BlockSpec((1, 1024), lambda b, c: (0, 0))],
        out_specs=[pl.BlockSpec((CH, 4096), rmap), pl.BlockSpec((None, 8, 1024), lambda b, c: (b, 0, 0))],
        out_shape=[jax.ShapeDtypeStruct((R, 4096), _MXU), jax.ShapeDtypeStruct((B, 8, 1024), F32)],
        scratch_shapes=[pltpu.VMEM((RET_HEADS, 256, 256), F32)])
    return _call(body, "ret_bwd", ("arbitrary", "arbitrary"), kw, (dycat, u0, opre, rin, cos, sin, norm_g), rider)


_SB_SCALE = SB_HD ** -0.5


_SB_NB = 3


def _sb_valid(qb, kb, live):
    qpos = qb * CH + jnp.bitwise_and(_row_ids(2 * CH, CH), CH - 1)
    kpos = kb * CH + _lane_ids(2 * CH, CH)
    first = PAD + (1 - live) * (1 << 24)
    return jnp.logical_and(kpos < qpos, kpos >= first)


_SB_DEAD = -100.0
_SB_OFF = -1e30


def _sb_alive(acc):
    return (jnp.max(acc) > _SB_DEAD).astype(jnp.int32)


def _sb_softplus(z):
    return jnp.maximum(z, 0.0) + jnp.log(1.0 + jnp.exp(-jnp.abs(z)))


def _stack_heads(x):
    hm = _lane_ids(1, 128) < SB_HD
    return jnp.concatenate([jnp.where(hm, x, 0.0), jnp.where(hm, 0.0, x)], axis=0)


def _unstack_heads(x2):
    return jnp.where(_lane_ids(1, 128) < SB_HD, x2[:CH], x2[CH:])


def _sb_fwd(u1, B, nch, rider=None):
    R = u1.shape[0]
    Pn = nch * CH

    def body(q_ref, k_ref, v_ref, out_ref):
        qb = pl.program_id(2)
        q2 = _stack_heads(q_ref[...] * _SB_SCALE).astype(_MXU)
        mgt = (_row_ids(CH, CH) > _lane_ids(CH, CH)).astype(F32)

        def step(i, carry):
            out2, acc = carry
            blocks = []
            for t in range(_SB_NB):
                kb = qb - _SB_NB * i - t
                live = (kb >= 0).astype(jnp.int32)
                kbc = jnp.maximum(kb, 0)
                start = pl.multiple_of(kbc * CH, CH)
                valid = _sb_valid(qb, kbc, live)
                z = _dot_nt(q2, k_ref[pl.ds(start, CH), :])
                sp = _sb_softplus(z)
                lm = jnp.where(valid, -sp, 0.0)
                blocks.append((valid, z - sp, _dot_split(lm, mgt), jnp.sum(lm, axis=1, keepdims=True), start))
            for valid, ls, loc, rs, start in blocks:
                w = jnp.where(valid, jnp.exp(ls + loc + acc), 0.0)
                out2 = out2 + _dot(w, v_ref[pl.ds(start, CH), :])
                acc = acc + rs
            return out2, acc

        trips = (qb + _SB_NB) // _SB_NB

        def more(c):
            return jnp.logical_and(c[0] < trips, c[1] > 0)

        def trip(c):
            out2, acc = step(c[0], c[2:])
            return c[0] + 1, _sb_alive(acc), out2, acc

        init = (jnp.int32(0), jnp.int32(1), jnp.zeros((2 * CH, 128), F32), jnp.zeros((2 * CH, 1), F32))
        out2 = lax.while_loop(more, trip, init)[2]
        out_ref[...] = _unstack_heads(out2).astype(out_ref.dtype)

    qspec = lambda off: pl.BlockSpec((CH, 128), lambda b, hp, qb: (b * nch + qb, off + hp))
    kspec = lambda off: pl.BlockSpec((Pn, 128), lambda b, hp, qb: (b, off + hp))
    kw = dict(grid=(B, SB_HEADS // 2, nch), in_specs=[qspec(0), kspec(8), kspec(16)], out_specs=[qspec(0)],
              out_shape=[jax.ShapeDtypeStruct((R, 2048), _MXU)])
    return _call(body, "sb_fwd", ("arbitrary", "arbitrary", "arbitrary"), kw, (u1, u1, u1), rider)


def _sb_bwd(dycat, u1, B, nch, rider=None):
    R = u1.shape[0]
    Pn = nch * CH

    def body(q_ref, k_ref, v_ref, do_ref, dq_ref, dk_ref, dv_ref, lm_scr, ls_scr):
        qb = pl.program_id(2)

        @pl.when(qb == 0)
        def _():
            dk_ref[...] = jnp.zeros_like(dk_ref)
            dv_ref[...] = jnp.zeros_like(dv_ref)

        q2 = _stack_heads(q_ref[...] * _SB_SCALE)
        do2 = _stack_heads(do_ref[...])
        q2t, do2t = q2.T.astype(_MXU), do2.T.astype(_MXU)
        q2, do2 = q2.astype(_MXU), do2.astype(_MXU)
        rr = _row_ids(CH, CH)
        cc = _lane_ids(CH, CH)
        mle = (rr <= cc).astype(F32)
        mlt = (rr < cc).astype(F32)
        trips = (qb + _SB_NB) // _SB_NB

        def more(c):
            return jnp.logical_and(c[0] < trips, c[1] > 0)

        def scan(c):
            acc = c[2]
            for t in range(_SB_NB):
                kb = qb - _SB_NB * c[0] - t
                kbc = jnp.maximum(kb, 0)
                valid = _sb_valid(qb, kbc, (kb >= 0).astype(jnp.int32))
                z = _dot_nt(q2, k_ref[pl.ds(pl.multiple_of(kbc * CH, CH), CH), :])
                sp = _sb_softplus(z)
                lm = jnp.where(valid, -sp, 0.0)
                lm_scr[c[0] * _SB_NB + t] = lm
                ls_scr[c[0] * _SB_NB + t] = jnp.where(valid, z - sp, _SB_OFF)
                acc = acc + jnp.sum(lm, axis=1, keepdims=True)
            return c[0] + 1, _sb_alive(acc), acc

        used, _, s2 = lax.while_loop(more, scan, (jnp.int32(0), jnp.int32(1), jnp.zeros((2 * CH, 1), F32)))
        base = qb + 1 - _SB_NB * used

        def step(i, carry):
            dq2, pacc, gacc = carry
            blocks = []
            for t in range(_SB_NB):
                kb = base + _SB_NB * i + t
                start = pl.multiple_of(jnp.maximum(kb, 0) * CH, CH)
                slot = (used - 1 - i) * _SB_NB + (_SB_NB - 1 - t)
                lm = lm_scr[slot]
                blocks.append((ls_scr[slot], _dot_split(lm, mle), jnp.sum(lm, axis=1, keepdims=True), start))
            stage = []
            for ls, ploc, rs, start in blocks:
                w = jnp.exp(ls + (s2 - (ploc + pacc)))
                gg = _dot_nt(do2, v_ref[pl.ds(start, CH), :]) * w
                stage.append((ls, w, gg, _dot_split(gg, mlt), jnp.sum(gg, axis=1, keepdims=True), start))
                pacc = pacc + rs
            for ls, w, gg, gloc, gs, start in stage:
                sig = jnp.exp(ls)
                dz = gg * (1.0 - sig) - (gloc + gacc) * sig
                dq2 = dq2 + _dot(dz, k_ref[pl.ds(start, CH), :])
                dk_ref[:, pl.ds(start, CH)] += _dot(q2t, dz)
                dv_ref[:, pl.ds(start, CH)] += _dot(do2t, w)
                gacc = gacc + gs
            return dq2, pacc, gacc

        zero = jnp.zeros((2 * CH, 1), F32)
        dq2 = lax.fori_loop(0, used, step, (jnp.zeros((2 * CH, 128), F32), zero, zero))[0]
        dq_ref[...] = (_unstack_heads(dq2) * _SB_SCALE).astype(dq_ref.dtype)

    qspec = lambda off: pl.BlockSpec((CH, 128), lambda b, hp, qb: (b * nch + qb, off + hp))
    kspec = lambda off: pl.BlockSpec((Pn, 128), lambda b, hp, qb: (b, off + hp))
    tspec = pl.BlockSpec((128, Pn), lambda b, hp, qb: (hp, b))
    full = jax.ShapeDtypeStruct((1024, R), F32)
    slots = (nch - 1 + _SB_NB) // _SB_NB * _SB_NB
    kw = dict(grid=(B, SB_HEADS // 2, nch), in_specs=[qspec(0), kspec(8), kspec(16), qspec(0)],
              out_specs=[qspec(0), tspec, tspec], out_shape=[jax.ShapeDtypeStruct((R, 1024), _MXU), full, full],
              scratch_shapes=[pltpu.VMEM((slots, 2 * CH, CH), F32)] * 2)
    return _call(body, "sb_bwd", ("arbitrary", "arbitrary", "arbitrary"), kw, (u1, u1, u1, dycat), rider)


def _neg_expm1(x):
    series = -(x * (1.0 + x * (0.5 + x * (1.0 / 6.0 + x * (1.0 / 24.0)))))
    return jnp.where(x > -0.05, series, 1.0 - jnp.exp(x))


def _lru_gates(x, wa_ref, ba_ref, wx_ref, bx_ref, lam_ref):
    rs, is_ = [], []
    for n in range(LRU_BLOCKS):
        xb = x[:, 128 * n:128 * n + 128]
        rs.append(_dot(xb, wa_ref[n]))
        is_.append(_dot(xb, wx_ref[n]))
    r = _sigmoid(jnp.concatenate(rs, axis=1) + ba_ref[...])
    i = _sigmoid(jnp.concatenate(is_, axis=1) + bx_ref[...])
    sp = _softplus(-lam_ref[...])
    la = -LRU_C * r * sp
    a = jnp.exp(la)
    mult = jnp.sqrt(jnp.maximum(_neg_expm1(2.0 * la), 0.0))
    return r, i, sp, a, mult


def _lru_fwd(u1, ycat, conv_w, conv_b, wa, ba, wx, bx, lam, B, nch):
    R = u1.shape[0]

    def body(x_ref, xp_ref, gate_ref, cw_ref, cb_ref, wa_ref, ba_ref, wx_ref, bx_ref, lam_ref, ycat_in,
             out_ref, hs_ref, hc):
        c = pl.program_id(1)

        @pl.when(c == 0)
        def _():
            hc[...] = jnp.zeros_like(hc)

        x = _conv_pre(xp_ref[...], x_ref[...], cw_ref, cb_ref, 4)
        r, i, sp, a, mult = _lru_gates(x, wa_ref, ba_ref, wx_ref, bx_ref, lam_ref)
        b = jnp.where(_real_rows(c), mult * (i * x), 0.0)
        rows = _row_ids(CH)
        s = 1
        while s < CH:
            a_s = jnp.where(rows >= s, pltpu.roll(a, s, axis=0), 1.0)
            b_s = jnp.where(rows >= s, pltpu.roll(b, s, axis=0), 0.0)
            b = a * b_s + b
            a = a * a_s
            s *= 2
        h = a * hc[0:1, :] + b
        hs_ref[...] = h
        hc[0:1, :] = hs_ref[CH - 1:CH, :]
        out_ref[...] = (h * _gelu(gate_ref[...])).astype(out_ref.dtype)

    row = lambda col: pl.BlockSpec((CH, 1024), lambda b, c: (b * nch + c, col))
    vec = pl.BlockSpec((1, 1024), lambda b, c: (0, 0))
    wsp = pl.BlockSpec((LRU_BLOCKS, 128, 128), lambda b, c: (0, 0, 0))
    return pl.pallas_call(
        body, name="lru_fwd", grid=(B, nch),
        in_specs=[row(4), pl.BlockSpec((8, 1024), _prev8_map(nch, 4)), row(3),
                  pl.BlockSpec((4, 1024), lambda b, c: (0, 0)), vec, wsp, vec, wsp, vec, vec,
                  pl.BlockSpec(memory_space=pl.ANY)],
        out_specs=[row(1), row(0)],
        out_shape=[jax.ShapeDtypeStruct(ycat.shape, ycat.dtype), jax.ShapeDtypeStruct((R, 1024), F32)],
        scratch_shapes=[pltpu.VMEM((8, 1024), F32)],
        input_output_aliases={10: 0},
        compiler_params=_cparams(("parallel", "arbitrary")),
    )(u1, u1, u1, conv_w, conv_b, wa, ba, wx, bx, lam, ycat)


def _lru_bwd(dycat, u1, hs, conv_w, conv_b, wa, ba, wx, bx, lam, B, nch):
    R = u1.shape[0]

    def body(dy_ref, x_ref, xp_ref, gate_ref, hs_ref, hsp_ref, cw_ref, cb_ref, wa_ref, ba_ref, wx_ref, bx_ref, lam_ref,
             dgate_ref, dxc_ref, pg_ref, dwa_ref, dwx_ref, lc):
        c = nch - 1 - pl.program_id(1)

        @pl.when(pl.program_id(1) == 0)
        def _():
            lc[...] = jnp.zeros_like(lc)
            pg_ref[...] = jnp.zeros_like(pg_ref)
            dwa_ref[...] = jnp.zeros_like(dwa_ref)
            dwx_ref[...] = jnp.zeros_like(dwx_ref)

        x = _conv_pre(xp_ref[...], x_ref[...], cw_ref, cb_ref, 4)
        r, i, sp, a, mult = _lru_gates(x, wa_ref, ba_ref, wx_ref, bx_ref, lam_ref)
        h = hs_ref[...]
        hprev = _shift_down(hsp_ref[...], h, 1)
        gate = gate_ref[...]
        dy = dy_ref[...]
        dgate_ref[...] = (dy * h * _dgelu(gate)).astype(dgate_ref.dtype)
        rows = _row_ids(CH)
        lam_t = dy * _gelu(gate) + jnp.where(rows == CH - 1, lc[0:1, :], 0.0)
        coef = jnp.where(rows < CH - 1, pltpu.roll(a, CH - 1, axis=0), 0.0)
        s = 1
        while s < CH:
            c_s = jnp.where(rows < CH - s, pltpu.roll(coef, CH - s, axis=0), 1.0)
            l_s = jnp.where(rows < CH - s, pltpu.roll(lam_t, CH - s, axis=0), 0.0)
            lam_t = coef * l_s + lam_t
            coef = coef * c_s
            s *= 2
        lc[0:1, :] = jnp.sum(jnp.where(rows == 0, a * lam_t, 0.0), axis=0, keepdims=True)
        db = jnp.where(_real_rows(c), lam_t, 0.0)
        da = db * hprev
        dmult = db * (i * x)
        di = db * mult * x
        dx = db * mult * i
        pos = mult > 0.0
        dla = da * a + jnp.where(pos, -dmult * (a * a) / jnp.where(pos, mult, 1.0), 0.0)
        dr = dla * (-LRU_C * sp)
        pg_ref[2:3, :] += jnp.sum(dla * (LRU_C * r) * _sigmoid(-lam_ref[...]), axis=0, keepdims=True)
        dpr = dr * r * (1.0 - r)
        dpi = di * i * (1.0 - i)
        pg_ref[0:1, :] += jnp.sum(dpr, axis=0, keepdims=True)
        pg_ref[1:2, :] += jnp.sum(dpi, axis=0, keepdims=True)
        dxs = []
        for n in range(LRU_BLOCKS):
            blk = slice(128 * n, 128 * n + 128)
            dxs.append(dx[:, blk] + _dot_nt(dpr[:, blk], wa_ref[n]) + _dot_nt(dpi[:, blk], wx_ref[n]))
            dwa_ref[n] += _dot_tn(x[:, blk], dpr[:, blk])
            dwx_ref[n] += _dot_tn(x[:, blk], dpi[:, blk])
        dxc_ref[...] = jnp.concatenate(dxs, axis=1)

    rmap = lambda col: (lambda b, c: (b * nch + nch - 1 - c, col))
    row = lambda col: pl.BlockSpec((CH, 1024), rmap(col))
    prev = lambda col: pl.BlockSpec(
        (8, 1024), lambda b, c: (jnp.maximum((b * nch + nch - 1 - c) * (CH // 8) - 1, 0), col))
    vec = pl.BlockSpec((1, 1024), lambda b, c: (0, 0))
    wsp = pl.BlockSpec((LRU_BLOCKS, 128, 128), lambda b, c: (0, 0, 0))
    full = jax.ShapeDtypeStruct((R, 1024), F32)
    return pl.pallas_call(
        body, name="lru_bwd", grid=(B, nch),
        in_specs=[row(1), row(4), prev(4), row(3), row(0), prev(0),
                  pl.BlockSpec((4, 1024), lambda b, c: (0, 0)), vec, wsp, vec, wsp, vec, vec],
        out_specs=[row(0), row(0), pl.BlockSpec((None, 8, 1024), lambda b, c: (b, 0, 0)),
                   pl.BlockSpec((None, LRU_BLOCKS, 128, 128), lambda b, c: (b, 0, 0, 0)),
                   pl.BlockSpec((None, LRU_BLOCKS, 128, 128), lambda b, c: (b, 0, 0, 0))],
        out_shape=[jax.ShapeDtypeStruct((R, 1024), _MXU), full, jax.ShapeDtypeStruct((B, 8, 1024), F32),
                   jax.ShapeDtypeStruct((B, LRU_BLOCKS, 128, 128), F32),
                   jax.ShapeDtypeStruct((B, LRU_BLOCKS, 128, 128), F32)],
        scratch_shapes=[pltpu.VMEM((8, 1024), F32)],
        compiler_params=_cparams(("parallel", "arbitrary")),
    )(dycat, u1, u1, u1, hs, hs, conv_w, conv_b, wa, ba, wx, bx, lam)


_FFN_TC = FFN


def _ffn_specs(nch):
    nt = FFN // _FFN_TC
    row = lambda off: pl.BlockSpec((CH, _FFN_TC), lambda b, c, j: (b * nch + c, off + j))
    prev = lambda off: pl.BlockSpec(
        (8, _FFN_TC), lambda b, c, j: (jnp.maximum((b * nch + c) * (CH // 8) - 1, 0), off + j))
    wsp = lambda off: pl.BlockSpec((3, _FFN_TC), lambda b, c, j: (0, off + j))
    bsp = lambda off: pl.BlockSpec((1, _FFN_TC), lambda b, c, j: (0, off + j))
    return nt, row, [row(0), prev(0), row(nt), prev(nt), wsp(0), wsp(nt), bsp(0), bsp(nt)]


def _ffn_act_fwd(uf, conv_w, conv_b, B, nch, rider=None):
    R = uf.shape[0]
    nt, row, specs = _ffn_specs(nch)

    def body(g_ref, gp_ref, u_ref, up_ref, wg_ref, wu_ref, bg_ref, bu_ref, o_ref):
        cg = _conv_pre(gp_ref[...], g_ref[...], wg_ref, bg_ref, 3)
        cu = _conv_pre(up_ref[...], u_ref[...], wu_ref, bu_ref, 3)
        o_ref[...] = jnp.where(_real_rows(pl.program_id(1)), _silu(cg) * cu, 0.0).astype(o_ref.dtype)

    kw = dict(grid=(B, nch, nt), in_specs=specs, out_specs=[row(0)],
              out_shape=[jax.ShapeDtypeStruct((R, FFN), _MXU)])
    return _call(body, "ffn_act_fwd", ("arbitrary", "arbitrary", "arbitrary"), kw,
                 (uf, uf, uf, uf, conv_w, conv_w, conv_b, conv_b), rider)


def _ffn_act_bwd(da, uf, conv_w, conv_b, nch, name, rider=None):
    R = uf.shape[0]
    nt = FFN // _FFN_TC
    nr = R // CH
    K = 3

    def body(da_ref, dan_ref, g_ref, gp_ref, gn_ref, u_ref, up_ref, un_ref, wg_ref, wu_ref, bg_ref, bu_ref,
             dug_ref, duu_ref, dwg_ref, dwu_ref):
        i = pl.program_id(1)

        @pl.when(i == 0)
        def _():
            dwg_ref[...] = jnp.zeros_like(dwg_ref)
            dwu_ref[...] = jnp.zeros_like(dwu_ref)

        c = i % nch
        ext = CH + 8
        rows = _row_ids(ext)
        follows = (c < nch - 1).astype(jnp.int32)
        keep = jnp.logical_and(c * CH + rows >= PAD, rows < CH + 8 * follows)
        dav = jnp.where(keep, jnp.concatenate([da_ref[...], dan_ref[...]], axis=0), 0.0)

        def conv_ext(x_ref, xp_ref, xn_ref, w_ref, b_ref):
            cat = jnp.concatenate([xp_ref[...], x_ref[...], xn_ref[...]], axis=0)
            shifted = [cat[8:]] + [pltpu.roll(cat, s, axis=0)[8:] for s in range(1, K)]
            acc = shifted[0] * w_ref[K - 1:K, :] + b_ref[...]
            for s in range(1, K):
                acc = acc + shifted[s] * w_ref[K - 1 - s:K - s, :]
            return acc, shifted

        cg, gsh = conv_ext(g_ref, gp_ref, gn_ref, wg_ref, bg_ref)
        cu, ush = conv_ext(u_ref, up_ref, un_ref, wu_ref, bu_ref)
        sg = _sigmoid(cg)
        dcg = dav * cu * (sg * (1.0 + cg * (1.0 - sg)))
        dcu = dav * (cg * sg)
        for dc, xsh, w_ref, din_ref, dw_ref in ((dcg, gsh, wg_ref, dug_ref, dwg_ref), (dcu, ush, wu_ref, duu_ref, dwu_ref)):
            dp = dc[:CH]
            din = dp * w_ref[K - 1:K, :]
            dw_ref[7:8, :] += jnp.sum(dp, axis=0, keepdims=True)
            dw_ref[K - 1:K, :] += jnp.sum(dp * xsh[0][:CH], axis=0, keepdims=True)
            for s in range(1, K):
                din = din + pltpu.roll(dc, ext - s, axis=0)[:CH] * w_ref[K - 1 - s:K - s, :]
                dw_ref[K - 1 - s:K - s, :] += jnp.sum(dp * xsh[s][:CH], axis=0, keepdims=True)
            din_ref[...] = din.astype(din_ref.dtype)

    row = lambda off: pl.BlockSpec((CH, _FFN_TC), lambda j, i: (i, off + j))
    prev = lambda off: pl.BlockSpec((8, _FFN_TC), lambda j, i: (jnp.maximum(i * (CH // 8) - 1, 0), off + j))
    nxt = lambda off: pl.BlockSpec(
        (8, _FFN_TC), lambda j, i: (jnp.minimum((i + 1) * (CH // 8), nr * (CH // 8) - 1), off + j))
    wsp = lambda off: pl.BlockSpec((K, _FFN_TC), lambda j, i: (0, off + j))
    bsp = lambda off: pl.BlockSpec((1, _FFN_TC), lambda j, i: (0, off + j))
    acc = pl.BlockSpec((8, _FFN_TC), lambda j, i: (0, j))
    half = jax.ShapeDtypeStruct((R, FFN), _MXU)
    dwsh = jax.ShapeDtypeStruct((8, FFN), F32)
    kw = dict(
        grid=(nt, nr),
        in_specs=[row(0), nxt(0), row(0), prev(0), nxt(0), row(nt), prev(nt), nxt(nt), wsp(0), wsp(nt), bsp(0), bsp(nt)],
        out_specs=[row(0), row(0), acc, acc],
        out_shape=[half, half, dwsh, dwsh])
    return _call(body, name, ("arbitrary", "arbitrary"), kw,
                 (da, da, uf, uf, uf, uf, uf, uf, conv_w, conv_w, conv_b, conv_b), rider)


def _head(h, g, target, B, nch):
    R = h.shape[0]

    def body(h_ref, g_ref, t_ref, dh_ref, loss_ref, dg_ref):
        c = pl.program_id(1)

        @pl.when(c == 0)
        def _():
            dh_ref[...] = jnp.zeros_like(dh_ref)
            loss_ref[...] = jnp.zeros_like(loss_ref)
            dg_ref[...] = jnp.zeros_like(dg_ref)

        @pl.when(c > 0)
        def _():
            x = h_ref[...]
            gv = g_ref[...]
            r = lax.rsqrt(jnp.mean(x * x, axis=-1, keepdims=True) + EPS)
            xhat = x * r
            e = xhat * gv - t_ref[...]
            loss_ref[...] += 0.5 * jnp.sum(jnp.mean(e * e, axis=-1, keepdims=True), axis=0, keepdims=True)
            dy = e * (1.0 / D)
            dg_ref[0:1, :] += jnp.sum(dy * xhat, axis=0, keepdims=True)
            dx = dy * gv
            dh_ref[...] = r * (dx - xhat * jnp.mean(dx * xhat, axis=-1, keepdims=True))

    row = pl.BlockSpec((CH, D), lambda b, c: (b * nch + c, 0))
    return pl.pallas_call(
        body, name="head", grid=(B, nch),
        in_specs=[row, pl.BlockSpec((1, D), lambda b, c: (0, 0)),
                  pl.BlockSpec((CH, D), lambda b, c: (b * (nch - 1) + jnp.maximum(c - 1, 0), 0))],
        out_specs=[row, pl.BlockSpec((None, 8, 128), lambda b, c: (b, 0, 0)),
                   pl.BlockSpec((None, 8, D), lambda b, c: (b, 0, 0))],
        out_shape=[jax.ShapeDtypeStruct((R, D), F32), jax.ShapeDtypeStruct((B, 8, 128), F32),
                   jax.ShapeDtypeStruct((B, 8, D), F32)],
        compiler_params=_cparams(("parallel", "arbitrary")),
    )(h, g, target)


ADAM_LR = 0.001
ADAM_B1 = 0.9
ADAM_B2 = 0.999
ADAM_EPS = 1e-08
ADAM_WD = 0.01
ADAM_STEP = 10


def _adamw(w, g, m, v, name):
    Rr, C = w.shape
    tr = _tile(Rr, (256, 64))

    def body(w_ref, g_ref, m_ref, v_ref, d_ref, nm_ref, nv_ref):
        gv = g_ref[...]
        nm = ADAM_B1 * m_ref[...] + (1.0 - ADAM_B1) * gv
        nv = ADAM_B2 * v_ref[...] + (1.0 - ADAM_B2) * (gv * gv)
        m_hat = nm / (1.0 - ADAM_B1 ** ADAM_STEP)
        v_hat = nv / (1.0 - ADAM_B2 ** ADAM_STEP)
        d_ref[...] = -ADAM_LR * (m_hat / (jnp.sqrt(v_hat) + ADAM_EPS) + ADAM_WD * w_ref[...])
        nm_ref[...] = nm
        nv_ref[...] = nv

    spec = pl.BlockSpec((tr, C), lambda i: (i, 0))
    sh = jax.ShapeDtypeStruct((Rr, C), F32)
    return pl.pallas_call(
        body, name=name, grid=(Rr // tr,),
        in_specs=[spec] * 4, out_specs=[spec] * 3, out_shape=[sh] * 3,
        compiler_params=_cparams(("parallel",)),
    )(w, g, m, v)


_MESH = pl.DeviceIdType.MESH
_ANY = pl.BlockSpec(memory_space=pl.ANY)


def _place():
    x, y, c = lax.axis_index("x"), lax.axis_index("y"), lax.axis_index("c")
    chips = [(1 - x, y), (x, 1 - y), (1 - x, 1 - y)]
    return x, y, c, chips


def _rcopy(src, dst, ssem, rsem, dev):
    return pltpu.make_async_remote_copy(src_ref=src, dst_ref=dst, send_sem=ssem, recv_sem=rsem,
                                        device_id=dev, device_id_type=_MESH)


def _with_riders(body, kw, kind, riders):
    n_in, n_out, n_scr = len(kw["in_specs"]), len(kw["out_specs"]), len(kw.get("scratch_shapes", []))
    grid = kw["grid"]
    nr = len(riders)
    nsem = 4 if kind == "gather" else 2

    def new_body(*refs):
        ins, srcs = refs[:n_in], refs[n_in:n_in + nr]
        outs, dsts = refs[n_in + nr:n_in + nr + n_out], refs[n_in + nr + n_out:n_in + 2 * nr + n_out]
        scr = refs[n_in + 2 * nr + n_out:n_in + 2 * nr + n_out + n_scr]
        sems = refs[n_in + 2 * nr + n_out + n_scr:]
        first = last = None
        for axis, size in enumerate(grid):
            i = pl.program_id(axis)
            first = (i == 0) if first is None else jnp.logical_and(first, i == 0)
            last = (i == size - 1) if last is None else jnp.logical_and(last, i == size - 1)
        x, y, c, chips = _place()
        k = 2 * x + y
        sib = (x, y, 1 - c)
        ssem, rsem = sems[:2]
        sends = []
        for a in range(nr):
            for j, (cx, cy) in enumerate(chips):
                if kind == "gather":
                    src, dst = srcs[a].at[c], dsts[a].at[k, c]
                else:
                    src, dst = srcs[a].at[2 * cx + cy], dsts[a].at[k]
                sends.append(_rcopy(src, dst, ssem.at[3 * a + j], rsem.at[3 * a + j], (cx, cy, c)))

        @pl.when(first)
        def _():
            for cp in sends:
                cp.start()

        body(*ins, *outs, *scr)

        @pl.when(last)
        def _():
            passed = []
            for a in range(nr):
                for j, (cx, cy) in enumerate(chips):
                    got = dsts[a].at[2 * cx + cy, c] if kind == "gather" else dsts[a].at[2 * cx + cy]
                    _rcopy(got, got, ssem.at[3 * a + j], rsem.at[3 * a + j], (cx, cy, c)).wait_recv()
                    if kind == "gather":
                        fw = _rcopy(got, got, sems[2].at[3 * a + j], sems[3].at[3 * a + j], sib)
                        fw.start()
                        passed.append(fw)
            if kind == "gather":
                for a in range(nr):
                    for j, (cx, cy) in enumerate(chips):
                        got = dsts[a].at[2 * cx + cy, 1 - c]
                        _rcopy(got, got, sems[2].at[3 * a + j], sems[3].at[3 * a + j], sib).wait_recv()
            for cp in sends + passed:
                cp.wait_send()

    kw = dict(kw)
    kw["in_specs"] = list(kw["in_specs"]) + [_ANY] * nr
    kw["out_specs"] = list(kw["out_specs"]) + [_ANY] * nr
    kw["out_shape"] = list(kw["out_shape"]) + [
        jax.ShapeDtypeStruct(((4,) + r.shape) if kind == "gather" else r.shape, r.dtype) for r in riders]
    kw["scratch_shapes"] = list(kw.get("scratch_shapes", [])) + [pltpu.SemaphoreType.DMA((3 * nr,))] * nsem
    return new_body, kw


def _call(body, name, sem, kw, args, rider=None):
    if rider is not None:
        body, kw = _with_riders(body, kw, *rider)
        args = tuple(args) + tuple(rider[1])
    return pl.pallas_call(body, name=name, compiler_params=_cparams(sem), **kw)(*args)


def _fill_own(result, own, chip):
    return lax.dynamic_update_index_in_dim(result, own, chip, 0)


def _gather_shards(bigs, small):
    nb = len(bigs)

    def body(*refs):
        ins, outs = refs[:nb + 1], refs[nb + 1:2 * nb + 2]
        ssem, rsem, fssem, frsem = refs[2 * nb + 2:]
        x, y, c, chips = _place()
        k = 2 * x + y
        sib = (x, y, 1 - c)

        def part(a, slot, hc):
            return outs[a].at[slot] if a == nb else outs[a].at[slot, hc]

        first = []
        for a in range(nb + 1):
            src = ins[a] if a == nb else ins[a].at[c]
            for j, (cx, cy) in enumerate(chips):
                first.append(_rcopy(src, part(a, k, c), ssem.at[3 * a + j], rsem.at[3 * a + j], (cx, cy, c)))
        for cp in first:
            cp.start()
        passed = []
        for a in range(nb + 1):
            for j, (cx, cy) in enumerate(chips):
                got = part(a, 2 * cx + cy, c)
                _rcopy(got, got, ssem.at[3 * a + j], rsem.at[3 * a + j], (cx, cy, c)).wait_recv()
                if a < nb:
                    fw = _rcopy(got, got, fssem.at[3 * a + j], frsem.at[3 * a + j], sib)
                    fw.start()
                    passed.append(fw)
        for a in range(nb):
            for j, (cx, cy) in enumerate(chips):
                got = part(a, 2 * cx + cy, 1 - c)
                _rcopy(got, got, fssem.at[3 * a + j], frsem.at[3 * a + j], sib).wait_recv()
        for cp in first + passed:
            cp.wait_send()

    arrs = list(bigs) + [small]
    n = 3 * (nb + 1)
    return pl.pallas_call(
        body, name="gather_shards",
        in_specs=[_ANY] * (nb + 1), out_specs=[_ANY] * (nb + 1),
        out_shape=[jax.ShapeDtypeStruct((4,) + a.shape, a.dtype) for a in arrs],
        scratch_shapes=[pltpu.SemaphoreType.DMA((n,)), pltpu.SemaphoreType.DMA((n,)),
                        pltpu.SemaphoreType.DMA((n,)), pltpu.SemaphoreType.DMA((n,))],
    )(*arrs)


def _swap_halves(grads, name):
    na = len(grads)
    halves = [g.shape[1] // 2 for g in grads]

    def body(*refs):
        ins, outs = refs[:na], refs[na:2 * na]
        ssem, rsem = refs[2 * na:]
        x, y, c, _ = _place()
        sib = (x, y, 1 - c)
        cps = [_rcopy(ins[a].at[:, pl.ds((1 - c) * halves[a], halves[a]), :], outs[a], ssem.at[a], rsem.at[a], sib)
               for a in range(na)]
        for cp in cps:
            cp.start()
        for cp in cps:
            cp.wait()

    return pl.pallas_call(
        body, name=name,
        in_specs=[_ANY] * na, out_specs=[_ANY] * na,
        out_shape=[jax.ShapeDtypeStruct((4, g.shape[1] // 2, g.shape[2]), g.dtype) for g in grads],
        scratch_shapes=[pltpu.SemaphoreType.DMA((na,)), pltpu.SemaphoreType.DMA((na,))],
    )(*grads)


def _sum_rows(rh):
    return rh if rh <= 512 else _tile(rh, (512, 256, 128, 64, 32))


def _chip_sum(grad, recv, core, name):
    _, r, cdim = grad.shape
    rh = r // 2
    tr = _sum_rows(rh)
    nblk = rh // tr

    def body(core_ref, g_ref, r_ref, o_ref):
        o_ref[...] = (g_ref[...] + r_ref[...]).astype(o_ref.dtype)

    return pl.pallas_call(
        body, name=name,
        grid_spec=pltpu.PrefetchScalarGridSpec(
            num_scalar_prefetch=1, grid=(4, nblk),
            in_specs=[pl.BlockSpec((None, tr, cdim), lambda s, i, cr: (s, cr[0] * nblk + i, 0)),
                      pl.BlockSpec((None, tr, cdim), lambda s, i, cr: (s, i, 0))],
            out_specs=pl.BlockSpec((None, tr, cdim), lambda s, i, cr: (s, i, 0))),
        out_shape=jax.ShapeDtypeStruct((4, rh, cdim), BF16),
        compiler_params=_cparams(("parallel", "parallel")),
    )(core, grad, recv)


def _scatter_sums(sums):
    na = len(sums)

    def body(*refs):
        ins, outs = refs[:na], refs[na:2 * na]
        ssem, rsem, lsem = refs[2 * na:]
        x, y, c, chips = _place()
        k = 2 * x + y
        local = [pltpu.make_async_copy(ins[a].at[k], outs[a].at[k], lsem.at[a]) for a in range(na)]
        for cp in local:
            cp.start()
        cps = []
        for a in range(na):
            for j, (cx, cy) in enumerate(chips):
                cps.append(_rcopy(ins[a].at[2 * cx + cy], outs[a].at[k], ssem.at[3 * a + j], rsem.at[3 * a + j],
                                  (cx, cy, c)))
        for cp in cps:
            cp.start()
        for a in range(na):
            for j, (cx, cy) in enumerate(chips):
                got = outs[a].at[2 * cx + cy]
                _rcopy(got, got, ssem.at[3 * a + j], rsem.at[3 * a + j], (cx, cy, c)).wait_recv()
        for cp in cps:
            cp.wait_send()
        for cp in local:
            cp.wait()

    return pl.pallas_call(
        body, name="scatter_sums",
        in_specs=[_ANY] * na, out_specs=[_ANY] * na,
        out_shape=[jax.ShapeDtypeStruct(s.shape, s.dtype) for s in sums],
        scratch_shapes=[pltpu.SemaphoreType.DMA((3 * na,)), pltpu.SemaphoreType.DMA((3 * na,)),
                        pltpu.SemaphoreType.DMA((na,))],
    )(*sums)


def _sum_chips(parts, name):
    _, rh, cdim = parts.shape
    tr = _sum_rows(rh)

    def body(p_ref, o_ref):
        acc = p_ref[0].astype(F32)
        for j in range(1, 4):
            acc = acc + p_ref[j].astype(F32)
        o_ref[...] = acc

    return pl.pallas_call(
        body, name=name, grid=(rh // tr,),
        in_specs=[pl.BlockSpec((4, tr, cdim), lambda i: (0, i, 0))],
        out_specs=pl.BlockSpec((tr, cdim), lambda i: (i, 0)),
        out_shape=jax.ShapeDtypeStruct((rh, cdim), F32),
        compiler_params=_cparams(("parallel",)),
    )(parts)


def _join_halves(reds):
    na = len(reds)

    def body(*refs):
        ins, outs = refs[:na], refs[na:2 * na]
        ssem, rsem = refs[2 * na:]
        x, y, c, _ = _place()
        cps = [_rcopy(ins[a], outs[a], ssem.at[a], rsem.at[a], (x, y, 1 - c)) for a in range(na)]
        for cp in cps:
            cp.start()
        for cp in cps:
            cp.wait()

    return pl.pallas_call(
        body, name="join_halves",
        in_specs=[_ANY] * na, out_specs=[_ANY] * na,
        out_shape=[jax.ShapeDtypeStruct(r.shape, r.dtype) for r in reds],
        scratch_shapes=[pltpu.SemaphoreType.DMA((na,)), pltpu.SemaphoreType.DMA((na,))],
    )(*reds)


def _allreduce_small(buf):
    n = buf.shape[0]

    def body(in_ref, out_ref, recv, ssem, rsem):
        x, y, c, _ = _place()
        peers = [(x, y, 1 - c), (1 - x, y, c), (x, 1 - y, c)]
        out_ref[...] = in_ref[...]
        for r, peer in enumerate(peers):
            cp = _rcopy(out_ref, recv.at[r], ssem.at[r], rsem.at[r], peer)
            cp.start()
            cp.wait()
            out_ref[...] = out_ref[...] + recv[r]

    vm = pl.BlockSpec(memory_space=pltpu.VMEM)
    return pl.pallas_call(
        body, name="allreduce_small",
        in_specs=[vm], out_specs=vm,
        out_shape=jax.ShapeDtypeStruct(buf.shape, F32),
        scratch_shapes=[pltpu.VMEM((3, n, 128), F32), pltpu.SemaphoreType.DMA((3,)), pltpu.SemaphoreType.DMA((3,))],
        compiler_params=pltpu.CompilerParams(vmem_limit_bytes=VMEM_LIMIT),
    )(buf)


_W_NAMES = ['meta_tokens', 'l0_mix_norm', 'l0_w_in', 'l0_ssd_conv_w', 'l0_ssd_conv_b', 'l0_ssd_dt_bias', 'l0_ssd_a_log',
            'l0_ssd_d', 'l0_ssd_norm', 'l0_ret_norm', 'l0_w_out', 'l0_ffn_norm', 'l0_ffn_w_in', 'l0_ffn_conv_w',
            'l0_ffn_conv_b', 'l0_ffn_w_out', 'l1_mix_norm', 'l1_w_in', 'l1_lru_conv_w', 'l1_lru_conv_b', 'l1_lru_wa',
            'l1_lru_ba', 'l1_lru_wx', 'l1_lru_bx', 'l1_lru_lambda', 'l1_w_out', 'l1_ffn_norm', 'l1_ffn_w_in',
            'l1_ffn_conv_w', 'l1_ffn_conv_b', 'l1_ffn_w_out', 'final_norm']
_IN_NAMES = ['x'] + _W_NAMES + ['loss_target'] + ['m_' + n for n in _W_NAMES] + ['v_' + n for n in _W_NAMES]
_BIG = ['l0_w_in', 'l0_w_out', 'l0_ffn_w_in', 'l0_ffn_w_out', 'l1_w_in', 'l1_w_out', 'l1_ffn_w_in', 'l1_ffn_w_out']
_BIG_COLS = ('l0_w_in', 'l0_ffn_w_in', 'l1_w_in', 'l1_ffn_w_in')
_SMALL_SHARDED = ['meta_tokens', 'l0_ssd_conv_w', 'l0_ffn_conv_w', 'l1_lru_conv_w', 'l1_ffn_conv_w']
_SMALL = [n for n in _W_NAMES if n not in _BIG]


def _pack(arrs):
    flat = []
    for a in arrs:
        v = a.reshape(-1).astype(F32)
        flat.append(jnp.pad(v, (0, (-v.shape[0]) % 128)))
    v = jnp.concatenate(flat)
    v = jnp.pad(v, (0, (-v.shape[0]) % 1024))
    return v.reshape(-1, 128)


def _unpack(buf, shapes):
    out, row = [], 0
    for sh in shapes:
        n = int(np.prod(sh))
        rows = -(-n // 128)
        out.append(buf[row:row + rows].reshape(-1)[:n].reshape(sh))
        row += rows
    return out


def kernel(x, meta_tokens, l0_mix_norm, l0_w_in, l0_ssd_conv_w, l0_ssd_conv_b, l0_ssd_dt_bias, l0_ssd_a_log, l0_ssd_d, l0_ssd_norm, l0_ret_norm, l0_w_out, l0_ffn_norm, l0_ffn_w_in, l0_ffn_conv_w, l0_ffn_conv_b, l0_ffn_w_out, l1_mix_norm, l1_w_in, l1_lru_conv_w, l1_lru_conv_b, l1_lru_wa, l1_lru_ba, l1_lru_wx, l1_lru_bx, l1_lru_lambda, l1_w_out, l1_ffn_norm, l1_ffn_w_in, l1_ffn_conv_w, l1_ffn_conv_b, l1_ffn_w_out, final_norm, loss_target, m_meta_tokens, m_l0_mix_norm, m_l0_w_in, m_l0_ssd_conv_w, m_l0_ssd_conv_b, m_l0_ssd_dt_bias, m_l0_ssd_a_log, m_l0_ssd_d, m_l0_ssd_norm, m_l0_ret_norm, m_l0_w_out, m_l0_ffn_norm, m_l0_ffn_w_in, m_l0_ffn_conv_w, m_l0_ffn_conv_b, m_l0_ffn_w_out, m_l1_mix_norm, m_l1_w_in, m_l1_lru_conv_w, m_l1_lru_conv_b, m_l1_lru_wa, m_l1_lru_ba, m_l1_lru_wx, m_l1_lru_bx, m_l1_lru_lambda, m_l1_w_out, m_l1_ffn_norm, m_l1_ffn_w_in, m_l1_ffn_conv_w, m_l1_ffn_conv_b, m_l1_ffn_w_out, m_final_norm, v_meta_tokens, v_l0_mix_norm, v_l0_w_in, v_l0_ssd_conv_w, v_l0_ssd_conv_b, v_l0_ssd_dt_bias, v_l0_ssd_a_log, v_l0_ssd_d, v_l0_ssd_norm, v_l0_ret_norm, v_l0_w_out, v_l0_ffn_norm, v_l0_ffn_w_in, v_l0_ffn_conv_w, v_l0_ffn_conv_b, v_l0_ffn_w_out, v_l1_mix_norm, v_l1_w_in, v_l1_lru_conv_w, v_l1_lru_conv_b, v_l1_lru_wa, v_l1_lru_ba, v_l1_lru_wx, v_l1_lru_bx, v_l1_lru_lambda, v_l1_w_out, v_l1_ffn_norm, v_l1_ffn_w_in, v_l1_ffn_conv_w, v_l1_ffn_conv_b, v_l1_ffn_w_out, v_final_norm):
    args = (x, meta_tokens, l0_mix_norm, l0_w_in, l0_ssd_conv_w, l0_ssd_conv_b, l0_ssd_dt_bias, l0_ssd_a_log, l0_ssd_d, l0_ssd_norm, l0_ret_norm, l0_w_out, l0_ffn_norm, l0_ffn_w_in, l0_ffn_conv_w, l0_ffn_conv_b, l0_ffn_w_out, l1_mix_norm, l1_w_in, l1_lru_conv_w, l1_lru_conv_b, l1_lru_wa, l1_lru_ba, l1_lru_wx, l1_lru_bx, l1_lru_lambda, l1_w_out, l1_ffn_norm, l1_ffn_w_in, l1_ffn_conv_w, l1_ffn_conv_b, l1_ffn_w_out, final_norm, loss_target, m_meta_tokens, m_l0_mix_norm, m_l0_w_in, m_l0_ssd_conv_w, m_l0_ssd_conv_b, m_l0_ssd_dt_bias, m_l0_ssd_a_log, m_l0_ssd_d, m_l0_ssd_norm, m_l0_ret_norm, m_l0_w_out, m_l0_ffn_norm, m_l0_ffn_w_in, m_l0_ffn_conv_w, m_l0_ffn_conv_b, m_l0_ffn_w_out, m_l1_mix_norm, m_l1_w_in, m_l1_lru_conv_w, m_l1_lru_conv_b, m_l1_lru_wa, m_l1_lru_ba, m_l1_lru_wx, m_l1_lru_bx, m_l1_lru_lambda, m_l1_w_out, m_l1_ffn_norm, m_l1_ffn_w_in, m_l1_ffn_conv_w, m_l1_ffn_conv_b, m_l1_ffn_w_out, m_final_norm, v_meta_tokens, v_l0_mix_norm, v_l0_w_in, v_l0_ssd_conv_w, v_l0_ssd_conv_b, v_l0_ssd_dt_bias, v_l0_ssd_a_log, v_l0_ssd_d, v_l0_ssd_norm, v_l0_ret_norm, v_l0_w_out, v_l0_ffn_norm, v_l0_ffn_w_in, v_l0_ffn_conv_w, v_l0_ffn_conv_b, v_l0_ffn_w_out, v_l1_mix_norm, v_l1_w_in, v_l1_lru_conv_w, v_l1_lru_conv_b, v_l1_lru_wa, v_l1_lru_ba, v_l1_lru_wx, v_l1_lru_bx, v_l1_lru_lambda, v_l1_w_out, v_l1_ffn_norm, v_l1_ffn_w_in, v_l1_ffn_conv_w, v_l1_ffn_conv_b, v_l1_ffn_w_out, v_final_norm)
    p = dict(zip(_IN_NAMES, args))
    B, seq, _ = x.shape
    nch = (seq + CH) // CH
    Pn = nch * CH
    R = B * Pn
    chip = 2 * lax.axis_index("x") + lax.axis_index("y")
    row2 = lambda v: v.reshape(1, -1)
    pad128 = lambda v: jnp.pad(v, (0, 128 - v.shape[0])).reshape(1, 128)

    small_shapes = [p[n].shape for n in _SMALL_SHARDED]
    halved = lambda w: w.astype(_MXU).reshape(2, w.shape[0] // 2, w.shape[1])
    mine = {n: halved(p[n]) for n in _BIG}
    mine_small = _pack([p[n] for n in _SMALL_SHARDED])
    W = {}

    def set_weight(n, g):
        g = _fill_own(g, mine[n], chip)
        g = g.reshape(4, -1, g.shape[3])
        W[n] = jnp.concatenate([g[k] for k in range(4)], axis=1) if n in _BIG_COLS else g.reshape(-1, g.shape[2])

    def gather_on(*names):
        return ("gather", [mine[n] for n in names])

    def take_weights(names, got):
        for n, g in zip(names, got):
            set_weight(n, g)

    gathered = _gather_shards([mine['l0_w_in']], mine_small)
    set_weight('l0_w_in', gathered[0])
    g_small = _fill_own(gathered[-1], mine_small, chip)
    per_chip = [_unpack(g_small[k], small_shapes) for k in range(4)]
    for i, n in enumerate(_SMALL_SHARDED):
        W[n] = jnp.concatenate([per_chip[k][i] for k in range(4)], axis=1)
    w0 = W['l0_w_in']
    w0_main = jnp.concatenate([w0[:, 3088:], w0[:, :3072]], axis=1)
    w0_dt = jnp.pad(w0[:, 3072:3088], ((0, 0), (0, 112)))
    cos, sin = _rope_tables(nch)

    meta = jnp.broadcast_to(W['meta_tokens'][None], (B, N_META, D))
    h0 = jnp.concatenate([jnp.zeros((B, PAD, D), F32), meta, x], axis=1).reshape(R, D)
    n0, n0t = _rmsnorm_fwd(h0, row2(p['l0_mix_norm']), "norm_l0_mix")
    u0 = _mm(n0, w0_main, "nn", F32, "l0_in_proj")
    udt = _mm(n0, w0_dt, "nn", F32, "l0_dt_proj")
    a_log, d_skip, dt_bias = pad128(p['l0_ssd_a_log']), pad128(p['l0_ssd_d']), pad128(p['l0_ssd_dt_bias'])
    ssd_cb = row2(p['l0_ssd_conv_b'])
    act, dt, dtt, *got = _ssd_prep(u0, udt, W['l0_ssd_conv_w'], ssd_cb, dt_bias, B, nch, rider=gather_on('l0_w_out'))
    take_weights(['l0_w_out'], got)
    ycat0, ypre, hin, *got = _ssd_fwd(act, u0, dt, dtt, a_log, d_skip, row2(p['l0_ssd_norm']), B, nch,
                                      rider=gather_on('l0_ffn_w_in'))
    take_weights(['l0_ffn_w_in'], got)
    ycat0, opre, rin, *got = _ret_fwd(u0, ycat0, cos, sin, row2(p['l0_ret_norm']), B, nch,
                                      rider=gather_on('l0_ffn_w_out'))
    take_weights(['l0_ffn_w_out'], got)
    h1 = _mm(ycat0, W['l0_w_out'], "nn", F32, "l0_out_proj", add=h0)
    n1, n1t = _rmsnorm_fwd(h1, row2(p['l0_ffn_norm']), "norm_l0_ffn")
    uf0 = _mm(n1, W['l0_ffn_w_in'], "nn", F32, "l0_ffn_in")
    f0_cb = row2(p['l0_ffn_conv_b'])
    a0, *got = _ffn_act_fwd(uf0, W['l0_ffn_conv_w'], f0_cb, B, nch, rider=gather_on('l1_w_in'))
    take_weights(['l1_w_in'], got)
    h2 = _mm(a0, W['l0_ffn_w_out'], "nn", F32, "l0_ffn_out", add=h1)
    n2, n2t = _rmsnorm_fwd(h2, row2(p['l1_mix_norm']), "norm_l1_mix")
    u1 = _mm(n2, W['l1_w_in'], "nn", F32, "l1_in_proj")
    lru = (W['l1_lru_conv_w'], row2(p['l1_lru_conv_b']), p['l1_lru_wa'], row2(p['l1_lru_ba']), p['l1_lru_wx'],
           row2(p['l1_lru_bx']), row2(p['l1_lru_lambda']))
    later = ['l1_w_out', 'l1_ffn_w_in', 'l1_ffn_w_out']
    ycat1, *got = _sb_fwd(u1, B, nch, rider=gather_on(*later))
    take_weights(later, got)
    ycat1, hs = _lru_fwd(u1, ycat1, *lru, B, nch)
    h3 = _mm(ycat1, W['l1_w_out'], "nn", F32, "l1_out_proj", add=h2)
    n3, n3t = _rmsnorm_fwd(h3, row2(p['l1_ffn_norm']), "norm_l1_ffn")
    uf1 = _mm(n3, W['l1_ffn_w_in'], "nn", F32, "l1_ffn_in")
    f1_cb = row2(p['l1_ffn_conv_b'])
    a1, = _ffn_act_fwd(uf1, W['l1_ffn_conv_w'], f1_cb, B, nch)
    h4 = _mm(a1, W['l1_ffn_w_out'], "nn", F32, "l1_ffn_out", add=h3)
    dh4, lossp, dgf = _head(h4, row2(p['final_norm']), p['loss_target'].reshape(B * seq, D), B, nch)
    loss = lax.psum(jnp.sum(lossp[:, 0, 0]), ("x", "y", "c"))

    G = {'final_norm': dgf[:, 0].sum(0)}

    core = lax.axis_index("c").reshape(1).astype(jnp.int32)

    def col_shards(pieces):
        edges = np.cumsum([0] + [q.shape[1] for q in pieces])
        cs = int(edges[-1]) // 4
        shards = []
        for k in range(4):
            lo, hi = k * cs, (k + 1) * cs
            cut = [q[:, max(lo - e0, 0):min(hi - e0, q.shape[1])]
                   for q, e0, e1 in zip(pieces, edges[:-1], edges[1:]) if e0 < hi and e1 > lo]
            shards.append(cut[0] if len(cut) == 1 else jnp.concatenate(cut, axis=1))
        return jnp.stack(shards)

    def chip_sums(names, tag):
        stacked = [G[n] if n in _BIG_COLS else G[n].reshape(4, G[n].shape[0] // 4, G[n].shape[1]) for n in names]
        theirs = _swap_halves(stacked, "swap_halves_" + tag)
        return {n: _chip_sum(g, t, core, "chip_sum_" + n) for n, g, t in zip(names, stacked, theirs)}

    parts = {}

    def scatter_on(names, tag):
        sums = chip_sums(names, tag)
        return sums, ("scatter", [sums[n] for n in names])

    def take_parts(names, sums, got):
        for n, g in zip(names, got):
            parts[n] = _fill_own(g, lax.dynamic_index_in_dim(sums[n], chip, 0, keepdims=False), chip)

    def ffn_bwd(layer, dh_out, h_in, nt_in, uf, a_act, cb, rider=None):
        pre = f"l{layer}_"
        w_in, w_out, cw = W[pre + 'ffn_w_in'], W[pre + 'ffn_w_out'], W[pre + 'ffn_conv_w']
        da = _mm(dh_out, w_out, "nt", F32, pre + "ffn_out_dgrad")
        G[pre + 'ffn_w_out'] = _mm(a_act, dh_out, "tn", F32, pre + "ffn_out_wgrad")
        dug, duu, dwg, dwu, *rode = _ffn_act_bwd(da, uf, cw, cb, nch, pre + "ffn_act_bwd", rider=rider)
        G[pre + 'ffn_conv_w'] = jnp.concatenate([dwg[:3], dwu[:3]], axis=1)
        G[pre + 'ffn_conv_b'] = jnp.concatenate([dwg[7], dwu[7]])
        dn = _mm(dug, w_in, "nt", F32, pre + "ffn_in_dgrad_g")
        dn = _mm(duu, w_in, "nt", F32, pre + "ffn_in_dgrad_u", add=dn, b_off=FFN)
        G[pre + 'ffn_w_in'] = col_shards([_mm(nt_in, dug, "nn", F32, pre + "ffn_in_wgrad_g"),
                                          _mm(nt_in, duu, "nn", F32, pre + "ffn_in_wgrad_u")])
        dh_in, dg = _rmsnorm_bwd(h_in, row2(p[pre + 'ffn_norm']), dn, dh_out, nch, pre + "ffn_norm_bwd")
        G[pre + 'ffn_norm'] = dg[0]
        return dh_in, rode

    dh3, _ = ffn_bwd(1, dh4, h3, n3t, uf1, a1, f1_cb)
    dy1 = _mm(dh3, W['l1_w_out'], "nt", F32, "l1_out_dgrad")
    G['l1_w_out'] = _mm(ycat1, dh3, "tn", F32, "l1_out_wgrad")
    done = ['l1_ffn_w_in', 'l1_ffn_w_out', 'l1_w_out']
    sums, rider = scatter_on(done, "a")
    dq, dkt, dvt, *got = _sb_bwd(dy1, u1, B, nch, rider=rider)
    dk, dv = dkt.T, dvt.T
    take_parts(done, sums, got)
    dgate, dxc, pgl, dwa, dwx = _lru_bwd(dy1, u1, hs, *lru, B, nch)
    dxr, dcw = _conv_bwd(dxc, u1, 4096, W['l1_lru_conv_w'], 4, "l1_lru_conv_bwd")
    pgl = pgl.sum(0)
    G['l1_lru_ba'], G['l1_lru_bx'], G['l1_lru_lambda'] = pgl[0], pgl[1], pgl[2]
    G['l1_lru_wa'], G['l1_lru_wx'] = dwa.sum(0), dwx.sum(0)
    G['l1_lru_conv_w'], G['l1_lru_conv_b'] = dcw[:4], dcw[7]
    du1 = jnp.concatenate([piece.astype(_MXU) for piece in (dq, dk, dv, dgate, dxr)], axis=1)
    dn = _mm(du1, W['l1_w_in'], "nt", F32, "l1_in_dgrad")
    G['l1_w_in'] = col_shards([_mm(n2t, du1, "nn", F32, "l1_in_wgrad")])
    dh2, dg = _rmsnorm_bwd(h2, row2(p['l1_mix_norm']), dn, dh3, nch, "l1_mix_norm_bwd")
    G['l1_mix_norm'] = dg[0]

    dh1, _ = ffn_bwd(0, dh2, h1, n1t, uf0, a0, f0_cb)
    dy0 = _mm(dh1, W['l0_w_out'], "nt", F32, "l0_out_dgrad")
    G['l0_w_out'] = _mm(ycat0, dh1, "tn", F32, "l0_out_wgrad")
    done = ['l1_w_in', 'l0_ffn_w_in', 'l0_ffn_w_out', 'l0_w_out']
    sums, rider = scatter_on(done, "b")
    dz, dxs, dbm, dcm, ddt4, pgs, *got = _ssd_bwd(dy0, ypre, u0, act, dt, dtt, hin, a_log, d_skip,
                                                  row2(p['l0_ssd_norm']), B, nch, rider=rider)
    take_parts(done, sums, got)
    dpre, ddtr, pgd = _ssd_prep_bwd(dxs, dbm, dcm, ddt4, u0, udt, W['l0_ssd_conv_w'], ssd_cb, dt_bias, B, nch)
    dxbc, dcw0 = _conv_bwd(dpre, u0, U0_XBC, W['l0_ssd_conv_w'], 4, "l0_ssd_conv_bwd")
    dqkvg, pgr = _ret_bwd(dy0, u0, opre, rin, cos, sin, row2(p['l0_ret_norm']), B, nch)
    pgs = pgs.sum(0)
    G['l0_ssd_norm'] = pgs[:, 0, :].reshape(-1)
    G['l0_ssd_d'] = pgs[:, 1, :128].sum(0)[:SSD_HEADS]
    G['l0_ssd_a_log'] = pgs[:, 2, :128].sum(0)[:SSD_HEADS]
    G['l0_ssd_dt_bias'] = pgd.sum(0)[0, :SSD_HEADS]
    G['l0_ssd_conv_w'], G['l0_ssd_conv_b'] = dcw0[:4], dcw0[7]
    G['l0_ret_norm'] = pgr.sum(0)[0]
    dn = _mm(dqkvg, w0_main, "nt", F32, "l0_in_dgrad_qkvg")
    dn = _mm(dz, w0_main, "nt", F32, "l0_in_dgrad_z", add=dn, b_off=U0_Z)
    dn = _mm(dxbc, w0_main, "nt", F32, "l0_in_dgrad_xbc", add=dn, b_off=U0_XBC)
    dn = _mm(ddtr, w0_dt, "nt", F32, "l0_in_dgrad_dt", add=dn)
    G['l0_w_in'] = col_shards([
        _mm(n0t, dz, "nn", F32, "l0_in_wgrad_z"), _mm(n0t, dxbc, "nn", F32, "l0_in_wgrad_xbc"),
        _mm(n0t, ddtr, "nn", F32, "l0_in_wgrad_dt")[:, :SSD_HEADS], _mm(n0t, dqkvg, "nn", F32, "l0_in_wgrad_qkvg")])
    dh0, dg = _rmsnorm_bwd(h0, row2(p['l0_mix_norm']), dn, dh1, nch, "l0_mix_norm_bwd")
    G['l0_mix_norm'] = dg[0]
    dh0 = dh0.reshape(B, Pn, D)
    grad_x = dh0[:, CH:]
    G['meta_tokens'] = dh0[:, PAD:CH].sum(0)

    sums = chip_sums(['l0_w_in'], "d")
    parts['l0_w_in'], = _scatter_sums([sums['l0_w_in']])
    reds = [_sum_chips(parts[n], "sum_chips_" + n) for n in _BIG]
    grads = {}
    for n, own, other in zip(_BIG, reds, _join_halves(reds)):
        both = jnp.where(core[0] == 0, jnp.stack([own, other]), jnp.stack([other, own]))
        grads[n] = both.reshape(-1, both.shape[2])
    small_full = _unpack(_allreduce_small(_pack([G[n] for n in _SMALL])), [G[n].shape for n in _SMALL])
    for n, g in zip(_SMALL, small_full):
        if n in _SMALL_SHARDED:
            cs = g.shape[1] // 4
            g = lax.dynamic_slice_in_dim(g, chip * cs, cs, axis=1)
        grads[n] = g.reshape(p[n].shape)

    delta, new_m, new_v = {}, {}, {}
    for n in _BIG:
        delta[n], new_m[n], new_v[n] = _adamw(p[n], grads[n], p['m_' + n], p['v_' + n], "adamw_" + n)
    shapes = [p[n].shape for n in _SMALL]
    outs = _adamw(_pack([p[n] for n in _SMALL]), _pack([grads[n] for n in _SMALL]), _pack([p['m_' + n] for n in _SMALL]),
                  _pack([p['v_' + n] for n in _SMALL]), "adamw_small")
    for dst, buf in zip((delta, new_m, new_v), outs):
        for n, a in zip(_SMALL, _unpack(buf, shapes)):
            dst[n] = a
    return (loss, grad_x, *[grads[n] for n in _W_NAMES], *[delta[n] for n in _W_NAMES],
            *[new_m[n] for n in _W_NAMES], *[new_v[n] for n in _W_NAMES])
```

```python
import math

import numpy as np
import jax
import jax.numpy as jnp
from jax import lax
from jax.experimental import pallas as pl
from jax.experimental.pallas import tpu as pltpu

F32 = jnp.float32
BF16 = jnp.bfloat16
_MXU = jnp.bfloat16

D = 1024
CH = 128
N_META = 16
PAD = CH - N_META
EPS = 1e-6

SSD_HEADS = 16
SSD_HD = 64
SSD_GROUPS = 4
RET_HEADS = 4
RET_DK = 256
SB_HEADS = 16
SB_HD = 64
LRU_BLOCKS = 8
LRU_C = 8.0
FFN = 2816
U0_Z = 4096
U0_XBC = 5120

VMEM_LIMIT = 56 * 1024 * 1024


def _cparams(sem):
    return pltpu.CompilerParams(dimension_semantics=sem, vmem_limit_bytes=VMEM_LIMIT)


def _dot(a, b, dims=((1,), (0,))):
    return lax.dot_general(a.astype(_MXU), b.astype(_MXU), (dims, ((), ())), preferred_element_type=F32)


def _dot_nt(a, b):
    return _dot(a, b, ((1,), (1,)))


def _dot_tn(a, b):
    return _dot(a.T, b)


def _dot_exact(a, b):
    return lax.dot_general(a, b, (((1,), (0,)), ((), ())), preferred_element_type=F32,
                           precision=lax.Precision.HIGHEST)


def _dot_split(x, m01):
    hi = x.astype(BF16)
    lo = (x - hi.astype(F32)).astype(BF16)
    m = m01.astype(BF16)
    return jnp.dot(hi, m, preferred_element_type=F32) + jnp.dot(lo, m, preferred_element_type=F32)


def _sigmoid(x):
    return 0.5 * jnp.tanh(0.5 * x) + 0.5


def _softplus(x):
    return jnp.maximum(x, 0.0) + jnp.log1p(jnp.exp(-jnp.abs(x)))


def _silu(x):
    return x * _sigmoid(x)


def _dsilu(x):
    s = _sigmoid(x)
    return s * (1.0 + x * (1.0 - s))


_GELU_C = math.sqrt(2.0 / math.pi)


def _gelu(x):
    return 0.5 * x * (1.0 + jnp.tanh(_GELU_C * (x + 0.044715 * x * x * x)))


def _dgelu(x):
    t = jnp.tanh(_GELU_C * (x + 0.044715 * x * x * x))
    return 0.5 * (1.0 + t) + 0.5 * x * (1.0 - t * t) * _GELU_C * (1.0 + 3.0 * 0.044715 * x * x)


def _row_ids(n, cols=1):
    return lax.broadcasted_iota(jnp.int32, (n, cols), 0)


def _lane_ids(rows, n):
    return lax.broadcasted_iota(jnp.int32, (rows, n), 1)


def _real_rows(chunk):
    return chunk * CH + _row_ids(CH) >= PAD


def _shift_down(prev8, cur, s):
    cat = jnp.concatenate([prev8, cur], axis=0)
    return pltpu.roll(cat, s, axis=0)[8:]


def _shift_up(cur, next8, s):
    n = cur.shape[0]
    cat = jnp.concatenate([cur, next8], axis=0)
    return pltpu.roll(cat, n + 8 - s, axis=0)[:n]


def _conv_pre(prev8, cur, w_ref, b_ref, K):
    acc = cur * w_ref[K - 1:K, :] + b_ref[...]
    for s in range(1, K):
        acc = acc + _shift_down(prev8, cur, s) * w_ref[K - 1 - s:K - s, :]
    return acc


def _prev8_map(nch, col):
    return lambda b, c: (jnp.maximum((b * nch + c) * (CH // 8) - 1, 0), col)


def _matmul(a, b, mode, out_dtype, tm, tn, tk, name, add=None, b_off=0):
    if mode == "nn":
        (M, K), (_, N) = a.shape, b.shape
    elif mode == "nt":
        (M, K), N = a.shape, b.shape[0]
    else:
        (K, M), (_, N) = a.shape, b.shape
    tm, tn, tk = min(tm, M), min(tn, N), min(tk, K)
    assert M % tm == 0 and N % tn == 0 and K % tk == 0 and b_off % tk == 0, (name, M, N, K, tm, tn, tk)
    koff = b_off // tk
    nk = K // tk
    dims = {"nn": ((1,), (0,)), "nt": ((1,), (1,)), "tn": ((0,), (0,))}[mode]
    if mode == "tn":
        a_spec = pl.BlockSpec((tk, tm), lambda i, j, k: (k, i))
    else:
        a_spec = pl.BlockSpec((tm, tk), lambda i, j, k: (i, k))
    if mode == "nt":
        b_spec = pl.BlockSpec((tn, tk), lambda i, j, k: (j, k + koff))
    else:
        b_spec = pl.BlockSpec((tk, tn), lambda i, j, k: (k, j))
    o_spec = pl.BlockSpec((tm, tn), lambda i, j, k: (i, j))
    has_add = add is not None

    def body(a_ref, b_ref, *rest):
        if has_add:
            add_ref, o_ref, acc = rest
        else:
            o_ref, acc = rest
        k = pl.program_id(2)

        @pl.when(k == 0)
        def _():
            acc[...] = jnp.zeros_like(acc)

        acc[...] += _dot(a_ref[...], b_ref[...], dims)

        @pl.when(k == nk - 1)
        def _():
            r = acc[...]
            if has_add:
                r = r + add_ref[...].astype(F32)
            o_ref[...] = r.astype(out_dtype)

    in_specs = [a_spec, b_spec] + ([o_spec] if has_add else [])
    args = (a, b) + ((add,) if has_add else ())
    return pl.pallas_call(
        body, name=name, grid=(M // tm, N // tn, nk),
        in_specs=in_specs, out_specs=o_spec,
        out_shape=jax.ShapeDtypeStruct((M, N), out_dtype),
        scratch_shapes=[pltpu.VMEM((tm, tn), F32)],
        compiler_params=_cparams(("parallel", "parallel", "arbitrary")),
    )(*args)


def _tile(n, prefs):
    for t in prefs:
        if n % t == 0:
            return t
    return n


def _mm(a, b, mode, out_dtype, name, add=None, b_off=0):
    if mode == "tn":
        K, M = a.shape
        N = b.shape[1]
        tm, tn, tk = _tile(M, (1024, 1408, 512, 256, 128)), _tile(N, (1024, 1408, 512, 256, 128)), _tile(K, (2176, 384, 256, 128))
    else:
        M, K = a.shape
        N = b.shape[1] if mode == "nn" else b.shape[0]
        tm = _tile(M, (1088, 1024, 768, 512, 384, 256, 128))
        tn = _tile(N, (1024, 1408, 512, 256, 128))
        tk = _tile(K, (2176, 1024, 1408, 512, 256, 128))
    return _matmul(a, b, mode, out_dtype, tm, tn, tk, name, add=add, b_off=b_off)


def _rmsnorm_fwd(h, g, name):
    R = h.shape[0]
    tr = 2 * CH

    def body(h_ref, g_ref, o_ref, ot_ref):
        x = h_ref[...]
        r = lax.rsqrt(jnp.mean(x * x, axis=-1, keepdims=True) + EPS)
        y = x * r * g_ref[...]
        o_ref[...] = y.astype(o_ref.dtype)
        ot_ref[...] = y.T.astype(ot_ref.dtype)

    return pl.pallas_call(
        body, name=name, grid=(R // tr,),
        in_specs=[pl.BlockSpec((tr, D), lambda i: (i, 0)), pl.BlockSpec((1, D), lambda i: (0, 0))],
        out_specs=[pl.BlockSpec((tr, D), lambda i: (i, 0)), pl.BlockSpec((D, tr), lambda i: (0, i))],
        out_shape=[jax.ShapeDtypeStruct((R, D), _MXU), jax.ShapeDtypeStruct((D, R), _MXU)],
        compiler_params=_cparams(("parallel",)),
    )(h, g)


def _rmsnorm_bwd(h, g, dn, dres, nch, name):
    R = h.shape[0]
    per = 4
    tr = nch * CH // per

    def body(h_ref, g_ref, dn_ref, dres_ref, dh_ref, dhb_ref, dg_ref):
        i = pl.program_id(0)
        x = h_ref[...]
        r = lax.rsqrt(jnp.mean(x * x, axis=-1, keepdims=True) + EPS)
        xhat = x * r
        dn_v = dn_ref[...]
        dx = dn_v * g_ref[...]
        dh = r * (dx - xhat * jnp.mean(dx * xhat, axis=-1, keepdims=True))
        keep = (i % per) * tr + _row_ids(tr) >= PAD
        total = jnp.where(keep, dres_ref[...] + dh, 0.0)
        dh_ref[...] = total
        dhb_ref[...] = total.astype(dhb_ref.dtype)

        @pl.when(i == 0)
        def _():
            dg_ref[...] = jnp.zeros_like(dg_ref)

        dg_ref[...] += jnp.sum(dn_v * xhat, axis=0, keepdims=True)

    row = pl.BlockSpec((tr, D), lambda i: (i, 0))
    vec = pl.BlockSpec((1, D), lambda i: (0, 0))
    return pl.pallas_call(
        body, name=name, grid=(R // tr,),
        in_specs=[row, vec, row, row], out_specs=[row, row, vec],
        out_shape=[jax.ShapeDtypeStruct((R, D), F32), jax.ShapeDtypeStruct((R, D), _MXU),
                   jax.ShapeDtypeStruct((1, D), F32)],
        compiler_params=_cparams(("arbitrary",)),
    )(h, g, dn, dres)


def _ssd_prep(u0, udt, conv_w, conv_b, dt_bias, B, nch, rider=None):
    R = u0.shape[0]

    def body(xs_ref, xsp_ref, bc_ref, bcp_ref, udt_ref, w0_ref, w1_ref, b0_ref, b1_ref, dtb_ref,
             act_ref, dt_ref, dtt_ref):
        keep = _real_rows(pl.program_id(1))
        a0 = _silu(_conv_pre(xsp_ref[...], xs_ref[...], w0_ref, b0_ref, 4))
        a1 = _silu(_conv_pre(bcp_ref[...], bc_ref[...], w1_ref, b1_ref, 4))
        act_ref[:, :1024] = jnp.where(keep, a0, 0.0)
        act_ref[:, 1024:] = jnp.where(keep, a1, 0.0)
        ok = jnp.logical_and(keep, _lane_ids(1, 128) < SSD_HEADS)
        dt = jnp.where(ok, _softplus(udt_ref[...] + dtb_ref[...]), 0.0)
        dt_ref[...] = dt
        dtt_ref[...] = dt.T

    row = lambda col: pl.BlockSpec((CH, 1024), lambda b, c: (b * nch + c, col))
    prev = lambda col: pl.BlockSpec((8, 1024), _prev8_map(nch, col))
    kw = dict(
        grid=(B, nch),
        in_specs=[row(5), prev(5), row(6), prev(6),
                  pl.BlockSpec((CH, 128), lambda b, c: (b * nch + c, 0)),
                  pl.BlockSpec((4, 1024), lambda b, c: (0, 0)), pl.BlockSpec((4, 1024), lambda b, c: (0, 1)),
                  pl.BlockSpec((1, 1024), lambda b, c: (0, 0)), pl.BlockSpec((1, 1024), lambda b, c: (0, 1)),
                  pl.BlockSpec((1, 128), lambda b, c: (0, 0))],
        out_specs=[pl.BlockSpec((CH, 2048), lambda b, c: (b * nch + c, 0)),
                   pl.BlockSpec((CH, 128), lambda b, c: (b * nch + c, 0)),
                   pl.BlockSpec((128, CH), lambda b, c: (0, b * nch + c))],
        out_shape=[jax.ShapeDtypeStruct((R, 2048), F32), jax.ShapeDtypeStruct((R, 128), F32),
                   jax.ShapeDtypeStruct((128, R), F32)])
    return _call(body, "ssd_prep", ("arbitrary", "arbitrary"), kw,
                 (u0, u0, u0, u0, udt, conv_w, conv_w, conv_b, conv_b, dt_bias), rider)


def _ssd_head_terms(h, a_vec, dt_v, dtt_v, dsk_v):
    lane = _lane_ids(1, 128)
    sub = _row_ids(128)
    r = _row_ids(CH, CH)
    cidx = _lane_ids(CH, CH)
    a_h = jnp.sum(jnp.where(lane == h, a_vec, 0.0), axis=1, keepdims=True)
    dt_col = jnp.sum(jnp.where(lane == h, dt_v, 0.0), axis=1, keepdims=True)
    dt_row = jnp.sum(jnp.where(sub == h, dtt_v, 0.0), axis=0, keepdims=True)
    cs_col = jnp.sum(jnp.where(r >= cidx, dt_row * a_h, 0.0), axis=1, keepdims=True)
    cs_row = jnp.sum(jnp.where(r <= cidx, dt_col * a_h, 0.0), axis=0, keepdims=True)
    tot = jnp.sum(dt_col * a_h, axis=0, keepdims=True)
    dsk = jnp.sum(jnp.where(lane == h, dsk_v, 0.0), axis=1, keepdims=True)
    return a_h, dt_col, cs_col, cs_row, tot, dsk


def _ssd_fwd(act, u0, dt, dtt, a_log, d_skip, norm_g, B, nch, rider=None):
    R = act.shape[0]

    def body(xs_ref, bm_ref, cm_ref, z_ref, dt_ref, dtt_ref, alog_ref, dsk_ref, ng_ref,
             out_ref, ypre_ref, hin_ref, H):
        g = pl.program_id(1)
        c = pl.program_id(2)

        @pl.when(c == 0)
        def _():
            H[...] = jnp.zeros_like(H)

        hin_ref[...] = H[...]
        a_vec = -jnp.exp(alog_ref[...])
        dt_v = dt_ref[...]
        dtt_v = dtt_ref[...]
        hm = _lane_ids(1, 128) < SSD_HD
        r = _row_ids(CH, CH)
        cidx = _lane_ids(CH, CH)
        Bm = bm_ref[...]
        Cm = cm_ref[...]
        CB = _dot_nt(Cm, Bm)
        ys = []
        for pair in range(2):
            cols = slice(128 * pair, 128 * pair + 128)
            xraw = xs_ref[:, cols]
            t = [_ssd_head_terms(4 * g + 2 * pair + j, a_vec, dt_v, dtt_v, dsk_ref[...]) for j in range(2)]
            sel = lambda f: jnp.where(hm, f(t[0]), f(t[1]))
            dtp = sel(lambda q: q[1])
            Ep = sel(lambda q: jnp.exp(q[2]))
            Wp = sel(lambda q: jnp.exp(q[4] - q[2]))
            etot = sel(lambda q: jnp.exp(q[4]))
            dsk = sel(lambda q: q[5])
            X = xraw * dtp
            ydiag = jnp.zeros((CH, 128), F32)
            for j in range(2):
                Lm = jnp.where(r >= cidx, jnp.exp(t[j][2] - t[j][3]), 0.0)
                Xh = jnp.where(hm if j == 0 else jnp.logical_not(hm), X, 0.0)
                ydiag = ydiag + _dot(CB * Lm, Xh)
            Hp = H[:, cols]
            yoff = Ep * _dot(Cm, Hp)
            S = _dot(Bm.T, X * Wp)
            H[:, cols] = etot * Hp + S
            ys.append(ydiag + yoff + xraw * dsk)
        y = jnp.concatenate(ys, axis=1)
        ypre_ref[...] = y
        yg = y * _silu(z_ref[...])
        rr = lax.rsqrt(jnp.mean(yg * yg, axis=-1, keepdims=True) + EPS)
        out_ref[...] = jnp.where(_real_rows(c), yg * rr * ng_ref[...], 0.0).astype(out_ref.dtype)

    rowb = lambda w, colf: pl.BlockSpec((CH, w), lambda b, g, c: (b * nch + c, colf(g)))
    vec = pl.BlockSpec((1, 128), lambda b, g, c: (0, 0))
    kw = dict(
        grid=(B, SSD_GROUPS, nch),
        in_specs=[rowb(256, lambda g: g), rowb(128, lambda g: 8 + g), rowb(128, lambda g: 12 + g),
                  rowb(256, lambda g: 16 + g), rowb(128, lambda g: 0),
                  pl.BlockSpec((128, CH), lambda b, g, c: (0, b * nch + c)),
                  vec, vec, pl.BlockSpec((1, 256), lambda b, g, c: (0, g))],
        out_specs=[rowb(256, lambda g: g), rowb(256, lambda g: g),
                   pl.BlockSpec((None, None, None, 128, 256), lambda b, g, c: (b, g, c, 0, 0))],
        out_shape=[jax.ShapeDtypeStruct((R, 2048), _MXU), jax.ShapeDtypeStruct((R, 1024), F32),
                   jax.ShapeDtypeStruct((B, SSD_GROUPS, nch, 128, 256), F32)],
        scratch_shapes=[pltpu.VMEM((128, 256), F32)])
    return _call(body, "ssd_fwd", ("arbitrary", "arbitrary", "arbitrary"), kw,
                 (act, act, act, u0, dt, dtt, a_log, d_skip, norm_g), rider)


def _ssd_bwd(dycat, ypre, u0, act, dt, dtt, hin, a_log, d_skip, norm_g, B, nch, rider=None):
    R = act.shape[0]

    def body(dy_ref, ypre_ref, z_ref, xs_ref, bm_ref, cm_ref, dt_ref, dtt_ref, hin_ref, alog_ref, dsk_ref, ng_ref,
             dz_ref, dxs_ref, db_ref, dc_ref, ddt_ref, pg_ref, dH):
        g = pl.program_id(1)
        c = nch - 1 - pl.program_id(2)

        @pl.when(pl.program_id(2) == 0)
        def _():
            dH[...] = jnp.zeros_like(dH)
            pg_ref[...] = jnp.zeros_like(pg_ref)

        z = z_ref[...]
        y = ypre_ref[...]
        ng = ng_ref[...]
        dout = jnp.where(_real_rows(c), dy_ref[...], 0.0)
        sz = _sigmoid(z)
        yg = y * z * sz
        rr = lax.rsqrt(jnp.mean(yg * yg, axis=-1, keepdims=True) + EPS)
        nrm = yg * rr
        pg_ref[0:1, :] += jnp.sum(dout * nrm, axis=0, keepdims=True)
        dn = dout * ng
        dyg = rr * (dn - nrm * jnp.mean(dn * nrm, axis=-1, keepdims=True))
        dy = dyg * z * sz
        dz_ref[...] = (dyg * y * (sz * (1.0 + z * (1.0 - sz)))).astype(dz_ref.dtype)

        a_vec = -jnp.exp(alog_ref[...])
        dt_v = dt_ref[...]
        dtt_v = dtt_ref[...]
        lane = _lane_ids(1, 128)
        hm = lane < SSD_HD
        r = _row_ids(CH, CH)
        cidx = _lane_ids(CH, CH)
        last = _row_ids(CH) == CH - 1
        Bm = bm_ref[...]
        Cm = cm_ref[...]
        CB = _dot_nt(Cm, Bm)
        CBT = _dot_nt(Bm, Cm)
        dB = jnp.zeros((CH, 128), F32)
        dC = jnp.zeros((CH, 128), F32)
        dcs_all = jnp.zeros((CH, 128), F32)
        dtx_all = jnp.zeros((CH, 128), F32)
        dd_row = jnp.zeros((1, 128), F32)
        dxs = []
        for pair in range(2):
            cols = slice(128 * pair, 128 * pair + 128)
            xraw = xs_ref[:, cols]
            dyp = dy[:, cols]
            heads = [4 * g + 2 * pair + j for j in range(2)]
            t = [_ssd_head_terms(heads[j], a_vec, dt_v, dtt_v, dsk_ref[...]) for j in range(2)]
            sel = lambda f: jnp.where(hm, f(t[0]), f(t[1]))
            hsum = lambda v, j: jnp.sum(jnp.where(hm if j == 0 else jnp.logical_not(hm), v, 0.0), axis=1, keepdims=True)
            dtp = sel(lambda q: q[1])
            Ep = sel(lambda q: jnp.exp(q[2]))
            Wp = sel(lambda q: jnp.exp(q[4] - q[2]))
            etot = sel(lambda q: jnp.exp(q[4]))
            dsk = sel(lambda q: q[5])
            X = xraw * dtp
            Hp = hin_ref[:, cols]
            dHn = dH[:, cols]
            dskip = jnp.sum(dyp * xraw, axis=0, keepdims=True)
            yoff = Ep * _dot(Cm, Hp)
            dE = dyp * yoff
            dC = dC + _dot_nt(dyp * Ep, Hp)
            dH[:, cols] = etot * dHn + _dot(Cm.T, dyp * Ep)
            BdS = _dot(Bm, dHn)
            dX = Wp * BdS
            ew = X * BdS * Wp
            dB = dB + _dot_nt(X * Wp, dHn)
            hh = jnp.sum(dHn * Hp, axis=0, keepdims=True) * etot
            for j in range(2):
                hmask = hm if j == 0 else jnp.logical_not(hm)
                cs_col, cs_row = t[j][2], t[j][3]
                Lm = jnp.where(r >= cidx, jnp.exp(cs_col - cs_row), 0.0)
                LmT = jnp.where(cidx >= r, jnp.exp(cs_row - cs_col), 0.0)
                dyh = jnp.where(hmask, dyp, 0.0)
                Xh = jnp.where(hmask, X, 0.0)
                dM = _dot_nt(dyh, Xh)
                dMT = _dot_nt(Xh, dyh)
                M = CB * Lm
                MT = CBT * LmT
                dX = dX + _dot(MT, dyh)
                dC = dC + _dot(dM * Lm, Bm)
                dB = dB + _dot(dMT * LmT, Cm)
                g_rows = jnp.sum(dM * M, axis=1, keepdims=True)
                g_cols = jnp.sum(dMT * MT, axis=1, keepdims=True)
                dtot = (jnp.sum(hsum(ew, j), axis=0, keepdims=True)
                        + jnp.sum(jnp.where(hmask, hh, 0.0), axis=1, keepdims=True))
                dcs = g_rows - g_cols + hsum(dE, j) - hsum(ew, j) + jnp.where(last, dtot, 0.0)
                dcs_all = dcs_all + jnp.where(lane == heads[j], dcs, 0.0)
                dtx_all = dtx_all + jnp.where(lane == heads[j], hsum(dX * xraw, j), 0.0)
                dd_row = dd_row + jnp.where(lane == heads[j],
                                            jnp.sum(jnp.where(hmask, dskip, 0.0), axis=1, keepdims=True), 0.0)
            dxs.append(dX * dtp + dyp * dsk)
        dxs_ref[...] = jnp.concatenate(dxs, axis=1)
        db_ref[...] = dB
        dc_ref[...] = dC
        dadt = _dot_exact(jnp.where(cidx >= r, 1.0, 0.0), dcs_all)
        ddt_ref[...] = dadt * a_vec + dtx_all
        pg_ref[1:2, 0:128] += dd_row
        pg_ref[2:3, 0:128] += jnp.sum(dadt * dt_v, axis=0, keepdims=True) * a_vec

    rowb = lambda w, colf: pl.BlockSpec((CH, w), lambda b, g, c: (b * nch + nch - 1 - c, colf(g)))
    vec = pl.BlockSpec((1, 128), lambda b, g, c: (0, 0))
    kw = dict(
        grid=(B, SSD_GROUPS, nch),
        in_specs=[rowb(256, lambda g: g), rowb(256, lambda g: g), rowb(256, lambda g: 16 + g), rowb(256, lambda g: g),
                  rowb(128, lambda g: 8 + g), rowb(128, lambda g: 12 + g), rowb(128, lambda g: 0),
                  pl.BlockSpec((128, CH), lambda b, g, c: (0, b * nch + nch - 1 - c)),
                  pl.BlockSpec((None, None, None, 128, 256), lambda b, g, c: (b, g, nch - 1 - c, 0, 0)),
                  vec, vec, pl.BlockSpec((1, 256), lambda b, g, c: (0, g))],
        out_specs=[rowb(256, lambda g: g), rowb(256, lambda g: g), rowb(128, lambda g: g), rowb(128, lambda g: g),
                   rowb(128, lambda g: g),
                   pl.BlockSpec((None, None, 8, 256), lambda b, g, c: (b, g, 0, 0))],
        out_shape=[jax.ShapeDtypeStruct((R, 1024), _MXU), jax.ShapeDtypeStruct((R, 1024), F32),
                   jax.ShapeDtypeStruct((R, 512), F32), jax.ShapeDtypeStruct((R, 512), F32),
                   jax.ShapeDtypeStruct((R, 512), F32), jax.ShapeDtypeStruct((B, SSD_GROUPS, 8, 256), F32)],
        scratch_shapes=[pltpu.VMEM((128, 256), F32)])
    return _call(body, "ssd_bwd", ("arbitrary", "arbitrary", "arbitrary"), kw,
                 (dycat, ypre, u0, act, act, act, dt, dtt, hin, a_log, d_skip, norm_g), rider)


def _ssd_prep_bwd(dxs, dB, dC, ddt4, u0, udt, conv_w, conv_b, dt_bias, B, nch, rider=None):
    R = u0.shape[0]

    def body(dxs_ref, db_ref, dc_ref, ddt_ref, xs_ref, xsp_ref, bc_ref, bcp_ref, udt_ref, w0_ref, w1_ref, b0_ref, b1_ref,
             dtb_ref, dpre_ref, ddtr_ref, pgd_ref):
        c = pl.program_id(1)

        @pl.when(c == 0)
        def _():
            pgd_ref[...] = jnp.zeros_like(pgd_ref)

        keep = _real_rows(c)
        p0 = _conv_pre(xsp_ref[...], xs_ref[...], w0_ref, b0_ref, 4)
        p1 = _conv_pre(bcp_ref[...], bc_ref[...], w1_ref, b1_ref, 4)
        dpre_ref[:, :1024] = jnp.where(keep, dxs_ref[...] * _dsilu(p0), 0.0)
        dpre_ref[:, 1024:] = jnp.where(keep, jnp.concatenate([db_ref[...], dc_ref[...]], axis=1) * _dsilu(p1), 0.0)
        ddt = ddt_ref[:, 0:128] + ddt_ref[:, 128:256] + ddt_ref[:, 256:384] + ddt_ref[:, 384:512]
        ok = jnp.logical_and(keep, _lane_ids(1, 128) < SSD_HEADS)
        dr = jnp.where(ok, ddt * _sigmoid(udt_ref[...] + dtb_ref[...]), 0.0)
        ddtr_ref[...] = dr
        pgd_ref[0:1, :] += jnp.sum(dr, axis=0, keepdims=True)

    rw = lambda w: pl.BlockSpec((CH, w), lambda b, c: (b * nch + c, 0))
    row = lambda col: pl.BlockSpec((CH, 1024), lambda b, c: (b * nch + c, col))
    prev = lambda col: pl.BlockSpec((8, 1024), _prev8_map(nch, col))
    kw = dict(
        grid=(B, nch),
        in_specs=[rw(1024), rw(512), rw(512), rw(512), row(5), prev(5), row(6), prev(6), rw(128),
                  pl.BlockSpec((4, 1024), lambda b, c: (0, 0)), pl.BlockSpec((4, 1024), lambda b, c: (0, 1)),
                  pl.BlockSpec((1, 1024), lambda b, c: (0, 0)), pl.BlockSpec((1, 1024), lambda b, c: (0, 1)),
                  pl.BlockSpec((1, 128), lambda b, c: (0, 0))],
        out_specs=[rw(2048), rw(128), pl.BlockSpec((None, 8, 128), lambda b, c: (b, 0, 0))],
        out_shape=[jax.ShapeDtypeStruct((R, 2048), F32), jax.ShapeDtypeStruct((R, 128), F32),
                   jax.ShapeDtypeStruct((B, 8, 128), F32)])
    return _call(body, "ssd_prep_bwd", ("arbitrary", "arbitrary"), kw,
                 (dxs, dB, dC, ddt4, u0, u0, u0, u0, udt, conv_w, conv_w, conv_b, conv_b, dt_bias), rider)


def _conv_bwd(dpre, xin, xin_col, w, K, name, tc=1024):
    R, C = dpre.shape
    assert C % tc == 0 and xin_col % tc == 0
    nr = R // CH
    xoff = xin_col // tc

    def body(dp_ref, dpn_ref, x_ref, xp_ref, w_ref, din_ref, dw_ref):
        i = pl.program_id(1)

        @pl.when(i == 0)
        def _():
            dw_ref[...] = jnp.zeros_like(dw_ref)

        dp = dp_ref[...]
        nxt = dpn_ref[...] * (i < nr - 1).astype(F32)
        x = x_ref[...]
        xp = xp_ref[...]
        din = dp * w_ref[K - 1:K, :]
        dw_ref[K - 1:K, :] += jnp.sum(dp * x, axis=0, keepdims=True)
        dw_ref[7:8, :] += jnp.sum(dp, axis=0, keepdims=True)
        for s in range(1, K):
            din = din + _shift_up(dp, nxt, s) * w_ref[K - 1 - s:K - s, :]
            dw_ref[K - 1 - s:K - s, :] += jnp.sum(dp * _shift_down(xp, x, s), axis=0, keepdims=True)
        din_ref[...] = din.astype(din_ref.dtype)

    return pl.pallas_call(
        body, name=name, grid=(C // tc, nr),
        in_specs=[pl.BlockSpec((CH, tc), lambda j, i: (i, j)),
                  pl.BlockSpec((8, tc), lambda j, i: (jnp.minimum((i + 1) * (CH // 8), nr * (CH // 8) - 1), j)),
                  pl.BlockSpec((CH, tc), lambda j, i: (i, xoff + j)),
                  pl.BlockSpec((8, tc), lambda j, i: (jnp.maximum(i * (CH // 8) - 1, 0), xoff + j)),
                  pl.BlockSpec((K, tc), lambda j, i: (0, j))],
        out_specs=[pl.BlockSpec((CH, tc), lambda j, i: (i, j)),
                   pl.BlockSpec((8, tc), lambda j, i: (0, j))],
        out_shape=[jax.ShapeDtypeStruct((R, C), _MXU), jax.ShapeDtypeStruct((8, C), F32)],
        compiler_params=_cparams(("parallel", "arbitrary")),
    )(dpre, dpre, xin, xin, w)


_RET_LG = [float(v) for v in np.log1p(-np.exp2(-5.0 - np.arange(RET_HEADS, dtype=np.float32))).astype(np.float32)]
_RET_SCALE = RET_DK ** -0.5


def _rope_tables(nch):
    half = RET_DK // 2
    inv_freq = 1.0 / (10000.0 ** (jnp.arange(half, dtype=F32) / (half - 1)))
    pos = jnp.arange(nch * CH, dtype=F32) - PAD
    ang = pos[:, None] * inv_freq[None, :]
    return jnp.cos(ang), jnp.sin(ang)


def _rot(x, cos, sin):
    x1, x2 = x[:, :128], x[:, 128:]
    return jnp.concatenate([x1 * cos - x2 * sin, x1 * sin + x2 * cos], axis=1)


def _unrot(d, cos, sin):
    d1, d2 = d[:, :128], d[:, 128:]
    return jnp.concatenate([d1 * cos + d2 * sin, d2 * cos - d1 * sin], axis=1)


def _ret_decays(lg):
    r = _row_ids(CH, CH)
    cidx = _lane_ids(CH, CH)
    diff = (r - cidx).astype(F32)
    decay = jnp.where(r >= cidx, jnp.exp(lg * jnp.maximum(diff, 0.0)), 0.0)
    decay_t = jnp.where(cidx >= r, jnp.exp(lg * jnp.maximum(-diff, 0.0)), 0.0)
    idx = _row_ids(CH).astype(F32)
    zeta = jnp.exp(lg * (CH - 1.0 - idx))
    xi = jnp.exp(lg * (idx + 1.0))
    return decay, decay_t, zeta, xi


def _ret_fwd(u0, ycat, cos, sin, norm_g, B, nch, rider=None):
    R = u0.shape[0]

    def body(u_ref, cos_ref, sin_ref, ng_ref, ycat_in, out_ref, opre_ref, rin_ref, Rst):
        c = pl.program_id(1)

        @pl.when(c == 0)
        def _():
            Rst[...] = jnp.zeros_like(Rst)

        cos_v, sin_v = cos_ref[...], sin_ref[...]
        for h in range(RET_HEADS):
            lg = _RET_LG[h]
            cols = slice(256 * h, 256 * h + 256)
            decay, _, zeta, xi = _ret_decays(lg)
            qr = _rot(u_ref[:, cols], cos_v, sin_v)
            kr = _rot(u_ref[:, 1024 + 256 * h:1024 + 256 * h + 256], cos_v, sin_v) * _RET_SCALE
            v = u_ref[:, 2048 + 256 * h:2048 + 256 * h + 256]
            gate = u_ref[:, 3072 + 256 * h:3072 + 256 * h + 256]
            Rh = Rst[h]
            rin_ref[h] = Rh
            inner = _dot(_dot_nt(qr, kr) * decay, v)
            cross = _dot(qr, Rh) * xi
            Rst[h] = math.exp(CH * lg) * Rh + _dot((kr * zeta).T, v)
            o = inner + cross
            opre_ref[:, cols] = o
            oc = o - jnp.mean(o, axis=-1, keepdims=True)
            rr = lax.rsqrt(jnp.mean(oc * oc, axis=-1, keepdims=True) + EPS)
            out_ref[:, cols] = (_silu(gate) * (oc * rr * ng_ref[:, cols])).astype(out_ref.dtype)

    kw = dict(
        grid=(B, nch),
        in_specs=[pl.BlockSpec((CH, 4096), lambda b, c: (b * nch + c, 0)),
                  pl.BlockSpec((CH, 128), lambda b, c: (c, 0)), pl.BlockSpec((CH, 128), lambda b, c: (c, 0)),
                  pl.BlockSpec((1, 1024), lambda b, c: (0, 0)),
                  pl.BlockSpec(memory_space=pl.ANY)],
        out_specs=[pl.BlockSpec((CH, 1024), lambda b, c: (b * nch + c, 1)),
                   pl.BlockSpec((CH, 1024), lambda b, c: (b * nch + c, 0)),
                   pl.BlockSpec((None, None, RET_HEADS, 256, 256), lambda b, c: (b, c, 0, 0, 0))],
        out_shape=[jax.ShapeDtypeStruct(ycat.shape, ycat.dtype), jax.ShapeDtypeStruct((R, 1024), F32),
                   jax.ShapeDtypeStruct((B, nch, RET_HEADS, 256, 256), F32)],
        scratch_shapes=[pltpu.VMEM((RET_HEADS, 256, 256), F32)],
        input_output_aliases={4: 0})
    return _call(body, "ret_fwd", ("arbitrary", "arbitrary"), kw, (u0, cos, sin, norm_g, ycat), rider)


def _ret_bwd(dycat, u0, opre, rin, cos, sin, norm_g, B, nch, rider=None):
    R = u0.shape[0]

    def body(dy_ref, u_ref, opre_ref, rin_ref, cos_ref, sin_ref, ng_ref, du_ref, pg_ref, dR):
        @pl.when(pl.program_id(1) == 0)
        def _():
            dR[...] = jnp.zeros_like(dR)
            pg_ref[...] = jnp.zeros_like(pg_ref)

        cos_v, sin_v = cos_ref[...], sin_ref[...]
        for h in range(RET_HEADS):
            lg = _RET_LG[h]
            cols = slice(256 * h, 256 * h + 256)
            decay, decay_t, zeta, xi = _ret_decays(lg)
            qr = _rot(u_ref[:, cols], cos_v, sin_v)
            kr = _rot(u_ref[:, 1024 + 256 * h:1024 + 256 * h + 256], cos_v, sin_v) * _RET_SCALE
            v = u_ref[:, 2048 + 256 * h:2048 + 256 * h + 256]
            gate = u_ref[:, 3072 + 256 * h:3072 + 256 * h + 256]
            ng = ng_ref[:, cols]
            o = opre_ref[:, cols]
            oc = o - jnp.mean(o, axis=-1, keepdims=True)
            rr = lax.rsqrt(jnp.mean(oc * oc, axis=-1, keepdims=True) + EPS)
            ohat = oc * rr
            dout = dy_ref[:, cols]
            du_ref[:, 3072 + 256 * h:3072 + 256 * h + 256] = (dout * (ohat * ng) * _dsilu(gate)).astype(du_ref.dtype)
            don = dout * _silu(gate)
            pg_ref[0:1, cols] += jnp.sum(don * ohat, axis=0, keepdims=True)
            dohat = don * ng
            do = rr * (dohat - jnp.mean(dohat, axis=-1, keepdims=True)
                       - ohat * jnp.mean(dohat * ohat, axis=-1, keepdims=True))
            Rh = rin_ref[h]
            dRn = dR[h]
            sc_t = _dot_nt(kr, qr) * decay_t
            dv = _dot(sc_t, do) + _dot(kr * zeta, dRn)
            ds = _dot_nt(do, v) * decay
            ds_t = _dot_nt(v, do) * decay_t
            dox = do * xi
            dq = _dot(ds, kr) + _dot_nt(dox, Rh)
            dk = _dot(ds_t, qr) + zeta * _dot_nt(v, dRn)
            dR[h] = math.exp(CH * lg) * dRn + _dot(qr.T, dox)
            du_ref[:, cols] = _unrot(dq, cos_v, sin_v).astype(du_ref.dtype)
            du_ref[:, 1024 + 256 * h:1024 + 256 * h + 256] = (_unrot(dk, cos_v, sin_v) * _RET_SCALE).astype(du_ref.dtype)
            du_ref[:, 2048 + 256 * h:2048 + 256 * h + 256] = dv.astype(du_ref.dtype)

    rmap = lambda b, c: (b * nch + nch - 1 - c, 0)
    kw = dict(
        grid=(B, nch),
        in_specs=[pl.BlockSpec((CH, 1024), lambda b, c: (b * nch + nch - 1 - c, 1)),
                  pl.BlockSpec((CH, 4096), rmap), pl.BlockSpec((CH, 1024), rmap),
                  pl.BlockSpec((None, None, RET_HEADS, 256, 256), lambda b, c: (b, nch - 1 - c, 0, 0, 0)),
                  pl.BlockSpec((CH, 128), lambda b, c: (nch - 1 - c, 0)),
                  pl.BlockSpec((CH, 128), lambda b, c: (nch - 1 - c, 0)),
                  pl.BlockSpec((1, 1024), lambda b, c: (0, 0))],
        out_specs=[pl.BlockSpec((CH, 4096), rmap), pl.BlockSpec((None, 8, 1024), lambda b, c: (b, 0, 0))],
        out_shape=[jax.ShapeDtypeStruct((R, 4096), _MXU), jax.ShapeDtypeStruct((B, 8, 1024), F32)],
        scratch_shapes=[pltpu.VMEM((RET_HEADS, 256, 256), F32)])
    return _call(body, "ret_bwd", ("arbitrary", "arbitrary"), kw, (dycat, u0, opre, rin, cos, sin, norm_g), rider)


_SB_SCALE = SB_HD ** -0.5


_SB_NB = 3


def _sb_valid(qb, kb, live):
    qpos = qb * CH + jnp.bitwise_and(_row_ids(2 * CH, CH), CH - 1)
    kpos = kb * CH + _lane_ids(2 * CH, CH)
    first = PAD + (1 - live) * (1 << 24)
    return jnp.logical_and(kpos < qpos, kpos >= first)


_SB_DEAD = -100.0
_SB_OFF = -1e30


def _sb_alive(acc):
    return (jnp.max(acc) > _SB_DEAD).astype(jnp.int32)


def _sb_softplus(z):
    return jnp.maximum(z, 0.0) + jnp.log(1.0 + jnp.exp(-jnp.abs(z)))


def _stack_heads(x):
    hm = _lane_ids(1, 128) < SB_HD
    return jnp.concatenate([jnp.where(hm, x, 0.0), jnp.where(hm, 0.0, x)], axis=0)


def _unstack_heads(x2):
    return jnp.where(_lane_ids(1, 128) < SB_HD, x2[:CH], x2[CH:])


def _sb_fwd(u1, B, nch, rider=None):
    R = u1.shape[0]
    Pn = nch * CH

    def body(q_ref, k_ref, v_ref, out_ref):
        qb = pl.program_id(2)
        q2 = _stack_heads(q_ref[...] * _SB_SCALE).astype(_MXU)
        mgt = (_row_ids(CH, CH) > _lane_ids(CH, CH)).astype(F32)

        def step(i, carry):
            out2, acc = carry
            blocks = []
            for t in range(_SB_NB):
                kb = qb - _SB_NB * i - t
                live = (kb >= 0).astype(jnp.int32)
                kbc = jnp.maximum(kb, 0)
                start = pl.multiple_of(kbc * CH, CH)
                valid = _sb_valid(qb, kbc, live)
                z = _dot_nt(q2, k_ref[pl.ds(start, CH), :])
                sp = _sb_softplus(z)
                lm = jnp.where(valid, -sp, 0.0)
                blocks.append((valid, z - sp, _dot_split(lm, mgt), jnp.sum(lm, axis=1, keepdims=True), start))
            for valid, ls, loc, rs, start in blocks:
                w = jnp.where(valid, jnp.exp(ls + loc + acc), 0.0)
                out2 = out2 + _dot(w, v_ref[pl.ds(start, CH), :])
                acc = acc + rs
            return out2, acc

        trips = (qb + _SB_NB) // _SB_NB

        def more(c):
            return jnp.logical_and(c[0] < trips, c[1] > 0)

        def trip(c):
            out2, acc = step(c[0], c[2:])
            return c[0] + 1, _sb_alive(acc), out2, acc

        init = (jnp.int32(0), jnp.int32(1), jnp.zeros((2 * CH, 128), F32), jnp.zeros((2 * CH, 1), F32))
        out2 = lax.while_loop(more, trip, init)[2]
        out_ref[...] = _unstack_heads(out2).astype(out_ref.dtype)

    qspec = lambda off: pl.BlockSpec((CH, 128), lambda b, hp, qb: (b * nch + qb, off + hp))
    kspec = lambda off: pl.BlockSpec((Pn, 128), lambda b, hp, qb: (b, off + hp))
    kw = dict(grid=(B, SB_HEADS // 2, nch), in_specs=[qspec(0), kspec(8), kspec(16)], out_specs=[qspec(0)],
              out_shape=[jax.ShapeDtypeStruct((R, 2048), _MXU)])
    return _call(body, "sb_fwd", ("arbitrary", "arbitrary", "arbitrary"), kw, (u1, u1, u1), rider)


def _sb_bwd(dycat, u1, B, nch, rider=None):
    R = u1.shape[0]
    Pn = nch * CH

    def body(q_ref, k_ref, v_ref, do_ref, dq_ref, dk_ref, dv_ref, lm_scr, ls_scr):
        qb = pl.program_id(2)

        @pl.when(qb == 0)
        def _():
            dk_ref[...] = jnp.zeros_like(dk_ref)
            dv_ref[...] = jnp.zeros_like(dv_ref)

        q2 = _stack_heads(q_ref[...] * _SB_SCALE)
        do2 = _stack_heads(do_ref[...])
        q2t, do2t = q2.T.astype(_MXU), do2.T.astype(_MXU)
        q2, do2 = q2.astype(_MXU), do2.astype(_MXU)
        rr = _row_ids(CH, CH)
        cc = _lane_ids(CH, CH)
        mle = (rr <= cc).astype(F32)
        mlt = (rr < cc).astype(F32)
        trips = (qb + _SB_NB) // _SB_NB

        def more(c):
            return jnp.logical_and(c[0] < trips, c[1] > 0)

        def scan(c):
            acc = c[2]
            for t in range(_SB_NB):
                kb = qb - _SB_NB * c[0] - t
                kbc = jnp.maximum(kb, 0)
                valid = _sb_valid(qb, kbc, (kb >= 0).astype(jnp.int32))
                z = _dot_nt(q2, k_ref[pl.ds(pl.multiple_of(kbc * CH, CH), CH), :])
                sp = _sb_softplus(z)
                lm = jnp.where(valid, -sp, 0.0)
                lm_scr[c[0] * _SB_NB + t] = lm
                ls_scr[c[0] * _SB_NB + t] = jnp.where(valid, z - sp, _SB_OFF)
                acc = acc + jnp.sum(lm, axis=1, keepdims=True)
            return c[0] + 1, _sb_alive(acc), acc

        used, _, s2 = lax.while_loop(more, scan, (jnp.int32(0), jnp.int32(1), jnp.zeros((2 * CH, 1), F32)))
        base = qb + 1 - _SB_NB * used

        def step(i, carry):
            dq2, pacc, gacc = carry
            blocks = []
            for t in range(_SB_NB):
                kb = base + _SB_NB * i + t
                start = pl.multiple_of(jnp.maximum(kb, 0) * CH, CH)
                slot = (used - 1 - i) * _SB_NB + (_SB_NB - 1 - t)
                lm = lm_scr[slot]
                blocks.append((ls_scr[slot], _dot_split(lm, mle), jnp.sum(lm, axis=1, keepdims=True), start))
            stage = []
            for ls, ploc, rs, start in blocks:
                w = jnp.exp(ls + (s2 - (ploc + pacc)))
                gg = _dot_nt(do2, v_ref[pl.ds(start, CH), :]) * w
                stage.append((ls, w, gg, _dot_split(gg, mlt), jnp.sum(gg, axis=1, keepdims=True), start))
                pacc = pacc + rs
            for ls, w, gg, gloc, gs, start in stage:
                sig = jnp.exp(ls)
                dz = gg * (1.0 - sig) - (gloc + gacc) * sig
                dq2 = dq2 + _dot(dz, k_ref[pl.ds(start, CH), :])
                dk_ref[:, pl.ds(start, CH)] += _dot(q2t, dz)
                dv_ref[:, pl.ds(start, CH)] += _dot(do2t, w)
                gacc = gacc + gs
            return dq2, pacc, gacc

        zero = jnp.zeros((2 * CH, 1), F32)
        dq2 = lax.fori_loop(0, used, step, (jnp.zeros((2 * CH, 128), F32), zero, zero))[0]
        dq_ref[...] = (_unstack_heads(dq2) * _SB_SCALE).astype(dq_ref.dtype)

    qspec = lambda off: pl.BlockSpec((CH, 128), lambda b, hp, qb: (b * nch + qb, off + hp))
    kspec = lambda off: pl.BlockSpec((Pn, 128), lambda b, hp, qb: (b, off + hp))
    tspec = pl.BlockSpec((128, Pn), lambda b, hp, qb: (hp, b))
    full = jax.ShapeDtypeStruct((1024, R), F32)
    slots = (nch - 1 + _SB_NB) // _SB_NB * _SB_NB
    kw = dict(grid=(B, SB_HEADS // 2, nch), in_specs=[qspec(0), kspec(8), kspec(16), qspec(0)],
              out_specs=[qspec(0), tspec, tspec], out_shape=[jax.ShapeDtypeStruct((R, 1024), _MXU), full, full],
              scratch_shapes=[pltpu.VMEM((slots, 2 * CH, CH), F32)] * 2)
    return _call(body, "sb_bwd", ("arbitrary", "arbitrary", "arbitrary"), kw, (u1, u1, u1, dycat), rider)


def _neg_expm1(x):
    series = -(x * (1.0 + x * (0.5 + x * (1.0 / 6.0 + x * (1.0 / 24.0)))))
    return jnp.where(x > -0.05, series, 1.0 - jnp.exp(x))


def _lru_gates(x, wa_ref, ba_ref, wx_ref, bx_ref, lam_ref):
    rs, is_ = [], []
    for n in range(LRU_BLOCKS):
        xb = x[:, 128 * n:128 * n + 128]
        rs.append(_dot(xb, wa_ref[n]))
        is_.append(_dot(xb, wx_ref[n]))
    r = _sigmoid(jnp.concatenate(rs, axis=1) + ba_ref[...])
    i = _sigmoid(jnp.concatenate(is_, axis=1) + bx_ref[...])
    sp = _softplus(-lam_ref[...])
    la = -LRU_C * r * sp
    a = jnp.exp(la)
    mult = jnp.sqrt(jnp.maximum(_neg_expm1(2.0 * la), 0.0))
    return r, i, sp, a, mult


def _lru_fwd(u1, ycat, conv_w, conv_b, wa, ba, wx, bx, lam, B, nch):
    R = u1.shape[0]

    def body(x_ref, xp_ref, gate_ref, cw_ref, cb_ref, wa_ref, ba_ref, wx_ref, bx_ref, lam_ref, ycat_in,
             out_ref, hs_ref, hc):
        c = pl.program_id(1)

        @pl.when(c == 0)
        def _():
            hc[...] = jnp.zeros_like(hc)

        x = _conv_pre(xp_ref[...], x_ref[...], cw_ref, cb_ref, 4)
        r, i, sp, a, mult = _lru_gates(x, wa_ref, ba_ref, wx_ref, bx_ref, lam_ref)
        b = jnp.where(_real_rows(c), mult * (i * x), 0.0)
        rows = _row_ids(CH)
        s = 1
        while s < CH:
            a_s = jnp.where(rows >= s, pltpu.roll(a, s, axis=0), 1.0)
            b_s = jnp.where(rows >= s, pltpu.roll(b, s, axis=0), 0.0)
            b = a * b_s + b
            a = a * a_s
            s *= 2
        h = a * hc[0:1, :] + b
        hs_ref[...] = h
        hc[0:1, :] = hs_ref[CH - 1:CH, :]
        out_ref[...] = (h * _gelu(gate_ref[...])).astype(out_ref.dtype)

    row = lambda col: pl.BlockSpec((CH, 1024), lambda b, c: (b * nch + c, col))
    vec = pl.BlockSpec((1, 1024), lambda b, c: (0, 0))
    wsp = pl.BlockSpec((LRU_BLOCKS, 128, 128), lambda b, c: (0, 0, 0))
    return pl.pallas_call(
        body, name="lru_fwd", grid=(B, nch),
        in_specs=[row(4), pl.BlockSpec((8, 1024), _prev8_map(nch, 4)), row(3),
                  pl.BlockSpec((4, 1024), lambda b, c: (0, 0)), vec, wsp, vec, wsp, vec, vec,
                  pl.BlockSpec(memory_space=pl.ANY)],
        out_specs=[row(1), row(0)],
        out_shape=[jax.ShapeDtypeStruct(ycat.shape, ycat.dtype), jax.ShapeDtypeStruct((R, 1024), F32)],
        scratch_shapes=[pltpu.VMEM((8, 1024), F32)],
        input_output_aliases={10: 0},
        compiler_params=_cparams(("parallel", "arbitrary")),
    )(u1, u1, u1, conv_w, conv_b, wa, ba, wx, bx, lam, ycat)


def _lru_bwd(dycat, u1, hs, conv_w, conv_b, wa, ba, wx, bx, lam, B, nch):
    R = u1.shape[0]

    def body(dy_ref, x_ref, xp_ref, gate_ref, hs_ref, hsp_ref, cw_ref, cb_ref, wa_ref, ba_ref, wx_ref, bx_ref, lam_ref,
             dgate_ref, dxc_ref, pg_ref, dwa_ref, dwx_ref, lc):
        c = nch - 1 - pl.program_id(1)

        @pl.when(pl.program_id(1) == 0)
        def _():
            lc[...] = jnp.zeros_like(lc)
            pg_ref[...] = jnp.zeros_like(pg_ref)
            dwa_ref[...] = jnp.zeros_like(dwa_ref)
            dwx_ref[...] = jnp.zeros_like(dwx_ref)

        x = _conv_pre(xp_ref[...], x_ref[...], cw_ref, cb_ref, 4)
        r, i, sp, a, mult = _lru_gates(x, wa_ref, ba_ref, wx_ref, bx_ref, lam_ref)
        h = hs_ref[...]
        hprev = _shift_down(hsp_ref[...], h, 1)
        gate = gate_ref[...]
        dy = dy_ref[...]
        dgate_ref[...] = (dy * h * _dgelu(gate)).astype(dgate_ref.dtype)
        rows = _row_ids(CH)
        lam_t = dy * _gelu(gate) + jnp.where(rows == CH - 1, lc[0:1, :], 0.0)
        coef = jnp.where(rows < CH - 1, pltpu.roll(a, CH - 1, axis=0), 0.0)
        s = 1
        while s < CH:
            c_s = jnp.where(rows < CH - s, pltpu.roll(coef, CH - s, axis=0), 1.0)
            l_s = jnp.where(rows < CH - s, pltpu.roll(lam_t, CH - s, axis=0), 0.0)
            lam_t = coef * l_s + lam_t
            coef = coef * c_s
            s *= 2
        lc[0:1, :] = jnp.sum(jnp.where(rows == 0, a * lam_t, 0.0), axis=0, keepdims=True)
        db = jnp.where(_real_rows(c), lam_t, 0.0)
        da = db * hprev
        dmult = db * (i * x)
        di = db * mult * x
        dx = db * mult * i
        pos = mult > 0.0
        dla = da * a + jnp.where(pos, -dmult * (a * a) / jnp.where(pos, mult, 1.0), 0.0)
        dr = dla * (-LRU_C * sp)
        pg_ref[2:3, :] += jnp.sum(dla * (LRU_C * r) * _sigmoid(-lam_ref[...]), axis=0, keepdims=True)
        dpr = dr * r * (1.0 - r)
        dpi = di * i * (1.0 - i)
        pg_ref[0:1, :] += jnp.sum(dpr, axis=0, keepdims=True)
        pg_ref[1:2, :] += jnp.sum(dpi, axis=0, keepdims=True)
        dxs = []
        for n in range(LRU_BLOCKS):
            blk = slice(128 * n, 128 * n + 128)
            dxs.append(dx[:, blk] + _dot_nt(dpr[:, blk], wa_ref[n]) + _dot_nt(dpi[:, blk], wx_ref[n]))
            dwa_ref[n] += _dot_tn(x[:, blk], dpr[:, blk])
            dwx_ref[n] += _dot_tn(x[:, blk], dpi[:, blk])
        dxc_ref[...] = jnp.concatenate(dxs, axis=1)

    rmap = lambda col: (lambda b, c: (b * nch + nch - 1 - c, col))
    row = lambda col: pl.BlockSpec((CH, 1024), rmap(col))
    prev = lambda col: pl.BlockSpec(
        (8, 1024), lambda b, c: (jnp.maximum((b * nch + nch - 1 - c) * (CH // 8) - 1, 0), col))
    vec = pl.BlockSpec((1, 1024), lambda b, c: (0, 0))
    wsp = pl.BlockSpec((LRU_BLOCKS, 128, 128), lambda b, c: (0, 0, 0))
    full = jax.ShapeDtypeStruct((R, 1024), F32)
    return pl.pallas_call(
        body, name="lru_bwd", grid=(B, nch),
        in_specs=[row(1), row(4), prev(4), row(3), row(0), prev(0),
                  pl.BlockSpec((4, 1024), lambda b, c: (0, 0)), vec, wsp, vec, wsp, vec, vec],
        out_specs=[row(0), row(0), pl.BlockSpec((None, 8, 1024), lambda b, c: (b, 0, 0)),
                   pl.BlockSpec((None, LRU_BLOCKS, 128, 128), lambda b, c: (b, 0, 0, 0)),
                   pl.BlockSpec((None, LRU_BLOCKS, 128, 128), lambda b, c: (b, 0, 0, 0))],
        out_shape=[jax.ShapeDtypeStruct((R, 1024), _MXU), full, jax.ShapeDtypeStruct((B, 8, 1024), F32),
                   jax.ShapeDtypeStruct((B, LRU_BLOCKS, 128, 128), F32),
                   jax.ShapeDtypeStruct((B, LRU_BLOCKS, 128, 128), F32)],
        scratch_shapes=[pltpu.VMEM((8, 1024), F32)],
        compiler_params=_cparams(("parallel", "arbitrary")),
    )(dycat, u1, u1, u1, hs, hs, conv_w, conv_b, wa, ba, wx, bx, lam)


_FFN_TC = FFN


def _ffn_specs(nch):
    nt = FFN // _FFN_TC
    row = lambda off: pl.BlockSpec((CH, _FFN_TC), lambda b, c, j: (b * nch + c, off + j))
    prev = lambda off: pl.BlockSpec(
        (8, _FFN_TC), lambda b, c, j: (jnp.maximum((b * nch + c) * (CH // 8) - 1, 0), off + j))
    wsp = lambda off: pl.BlockSpec((3, _FFN_TC), lambda b, c, j: (0, off + j))
    bsp = lambda off: pl.BlockSpec((1, _FFN_TC), lambda b, c, j: (0, off + j))
    return nt, row, [row(0), prev(0), row(nt), prev(nt), wsp(0), wsp(nt), bsp(0), bsp(nt)]


def _ffn_act_fwd(uf, conv_w, conv_b, B, nch, rider=None):
    R = uf.shape[0]
    nt, row, specs = _ffn_specs(nch)

    def body(g_ref, gp_ref, u_ref, up_ref, wg_ref, wu_ref, bg_ref, bu_ref, o_ref):
        cg = _conv_pre(gp_ref[...], g_ref[...], wg_ref, bg_ref, 3)
        cu = _conv_pre(up_ref[...], u_ref[...], wu_ref, bu_ref, 3)
        o_ref[...] = jnp.where(_real_rows(pl.program_id(1)), _silu(cg) * cu, 0.0).astype(o_ref.dtype)

    kw = dict(grid=(B, nch, nt), in_specs=specs, out_specs=[row(0)],
              out_shape=[jax.ShapeDtypeStruct((R, FFN), _MXU)])
    return _call(body, "ffn_act_fwd", ("arbitrary", "arbitrary", "arbitrary"), kw,
                 (uf, uf, uf, uf, conv_w, conv_w, conv_b, conv_b), rider)


def _ffn_act_bwd(da, uf, conv_w, conv_b, nch, name, rider=None):
    R = uf.shape[0]
    nt = FFN // _FFN_TC
    nr = R // CH
    K = 3

    def body(da_ref, dan_ref, g_ref, gp_ref, gn_ref, u_ref, up_ref, un_ref, wg_ref, wu_ref, bg_ref, bu_ref,
             dug_ref, duu_ref, dwg_ref, dwu_ref):
        i = pl.program_id(1)

        @pl.when(i == 0)
        def _():
            dwg_ref[...] = jnp.zeros_like(dwg_ref)
            dwu_ref[...] = jnp.zeros_like(dwu_ref)

        c = i % nch
        ext = CH + 8
        rows = _row_ids(ext)
        follows = (c < nch - 1).astype(jnp.int32)
        keep = jnp.logical_and(c * CH + rows >= PAD, rows < CH + 8 * follows)
        dav = jnp.where(keep, jnp.concatenate([da_ref[...], dan_ref[...]], axis=0), 0.0)

        def conv_ext(x_ref, xp_ref, xn_ref, w_ref, b_ref):
            cat = jnp.concatenate([xp_ref[...], x_ref[...], xn_ref[...]], axis=0)
            shifted = [cat[8:]] + [pltpu.roll(cat, s, axis=0)[8:] for s in range(1, K)]
            acc = shifted[0] * w_ref[K - 1:K, :] + b_ref[...]
            for s in range(1, K):
                acc = acc + shifted[s] * w_ref[K - 1 - s:K - s, :]
            return acc, shifted

        cg, gsh = conv_ext(g_ref, gp_ref, gn_ref, wg_ref, bg_ref)
        cu, ush = conv_ext(u_ref, up_ref, un_ref, wu_ref, bu_ref)
        sg = _sigmoid(cg)
        dcg = dav * cu * (sg * (1.0 + cg * (1.0 - sg)))
        dcu = dav * (cg * sg)
        for dc, xsh, w_ref, din_ref, dw_ref in ((dcg, gsh, wg_ref, dug_ref, dwg_ref), (dcu, ush, wu_ref, duu_ref, dwu_ref)):
            dp = dc[:CH]
            din = dp * w_ref[K - 1:K, :]
            dw_ref[7:8, :] += jnp.sum(dp, axis=0, keepdims=True)
            dw_ref[K - 1:K, :] += jnp.sum(dp * xsh[0][:CH], axis=0, keepdims=True)
            for s in range(1, K):
                din = din + pltpu.roll(dc, ext - s, axis=0)[:CH] * w_ref[K - 1 - s:K - s, :]
                dw_ref[K - 1 - s:K - s, :] += jnp.sum(dp * xsh[s][:CH], axis=0, keepdims=True)
            din_ref[...] = din.astype(din_ref.dtype)

    row = lambda off: pl.BlockSpec((CH, _FFN_TC), lambda j, i: (i, off + j))
    prev = lambda off: pl.BlockSpec((8, _FFN_TC), lambda j, i: (jnp.maximum(i * (CH // 8) - 1, 0), off + j))
    nxt = lambda off: pl.BlockSpec(
        (8, _FFN_TC), lambda j, i: (jnp.minimum((i + 1) * (CH // 8), nr * (CH // 8) - 1), off + j))
    wsp = lambda off: pl.BlockSpec((K, _FFN_TC), lambda j, i: (0, off + j))
    bsp = lambda off: pl.BlockSpec((1, _FFN_TC), lambda j, i: (0, off + j))
    acc = pl.BlockSpec((8, _FFN_TC), lambda j, i: (0, j))
    half = jax.ShapeDtypeStruct((R, FFN), _MXU)
    dwsh = jax.ShapeDtypeStruct((8, FFN), F32)
    kw = dict(
        grid=(nt, nr),
        in_specs=[row(0), nxt(0), row(0), prev(0), nxt(0), row(nt), prev(nt), nxt(nt), wsp(0), wsp(nt), bsp(0), bsp(nt)],
        out_specs=[row(0), row(0), acc, acc],
        out_shape=[half, half, dwsh, dwsh])
    return _call(body, name, ("arbitrary", "arbitrary"), kw,
                 (da, da, uf, uf, uf, uf, uf, uf, conv_w, conv_w, conv_b, conv_b), rider)


def _head(h, g, target, B, nch):
    R = h.shape[0]

    def body(h_ref, g_ref, t_ref, dh_ref, dhb_ref, loss_ref, dg_ref):
        c = pl.program_id(1)

        @pl.when(c == 0)
        def _():
            dh_ref[...] = jnp.zeros_like(dh_ref)
            dhb_ref[...] = jnp.zeros_like(dhb_ref)
            loss_ref[...] = jnp.zeros_like(loss_ref)
            dg_ref[...] = jnp.zeros_like(dg_ref)

        @pl.when(c > 0)
        def _():
            x = h_ref[...]
            gv = g_ref[...]
            r = lax.rsqrt(jnp.mean(x * x, axis=-1, keepdims=True) + EPS)
            xhat = x * r
            e = xhat * gv - t_ref[...]
            loss_ref[...] += 0.5 * jnp.sum(jnp.mean(e * e, axis=-1, keepdims=True), axis=0, keepdims=True)
            dy = e * (1.0 / D)
            dg_ref[0:1, :] += jnp.sum(dy * xhat, axis=0, keepdims=True)
            dx = dy * gv
            dh = r * (dx - xhat * jnp.mean(dx * xhat, axis=-1, keepdims=True))
            dh_ref[...] = dh
            dhb_ref[...] = dh.astype(dhb_ref.dtype)

    row = pl.BlockSpec((CH, D), lambda b, c: (b * nch + c, 0))
    return pl.pallas_call(
        body, name="head", grid=(B, nch),
        in_specs=[row, pl.BlockSpec((1, D), lambda b, c: (0, 0)),
                  pl.BlockSpec((CH, D), lambda b, c: (b * (nch - 1) + jnp.maximum(c - 1, 0), 0))],
        out_specs=[row, row, pl.BlockSpec((None, 8, 128), lambda b, c: (b, 0, 0)),
                   pl.BlockSpec((None, 8, D), lambda b, c: (b, 0, 0))],
        out_shape=[jax.ShapeDtypeStruct((R, D), F32), jax.ShapeDtypeStruct((R, D), _MXU),
                   jax.ShapeDtypeStruct((B, 8, 128), F32), jax.ShapeDtypeStruct((B, 8, D), F32)],
        compiler_params=_cparams(("parallel", "arbitrary")),
    )(h, g, target)


ADAM_LR = 0.001
ADAM_B1 = 0.9
ADAM_B2 = 0.999
ADAM_EPS = 1e-08
ADAM_WD = 0.01
ADAM_STEP = 10


def _adamw(w, g, m, v, name):
    Rr, C = w.shape
    tr = _tile(Rr, (256, 64))

    def body(w_ref, g_ref, m_ref, v_ref, d_ref, nm_ref, nv_ref):
        gv = g_ref[...]
        nm = ADAM_B1 * m_ref[...] + (1.0 - ADAM_B1) * gv
        nv = ADAM_B2 * v_ref[...] + (1.0 - ADAM_B2) * (gv * gv)
        m_hat = nm / (1.0 - ADAM_B1 ** ADAM_STEP)
        v_hat = nv / (1.0 - ADAM_B2 ** ADAM_STEP)
        d_ref[...] = -ADAM_LR * (m_hat / (jnp.sqrt(v_hat) + ADAM_EPS) + ADAM_WD * w_ref[...])
        nm_ref[...] = nm
        nv_ref[...] = nv

    spec = pl.BlockSpec((tr, C), lambda i: (i, 0))
    sh = jax.ShapeDtypeStruct((Rr, C), F32)
    return pl.pallas_call(
        body, name=name, grid=(Rr // tr,),
        in_specs=[spec] * 4, out_specs=[spec] * 3, out_shape=[sh] * 3,
        compiler_params=_cparams(("parallel",)),
    )(w, g, m, v)


_MESH = pl.DeviceIdType.MESH
_ANY = pl.BlockSpec(memory_space=pl.ANY)


def _place():
    x, y, c = lax.axis_index("x"), lax.axis_index("y"), lax.axis_index("c")
    chips = [(1 - x, y), (x, 1 - y), (1 - x, 1 - y)]
    return x, y, c, chips


def _rcopy(src, dst, ssem, rsem, dev):
    return pltpu.make_async_remote_copy(src_ref=src, dst_ref=dst, send_sem=ssem, recv_sem=rsem,
                                        device_id=dev, device_id_type=_MESH)


def _with_riders(body, kw, kind, riders):
    n_in, n_out, n_scr = len(kw["in_specs"]), len(kw["out_specs"]), len(kw.get("scratch_shapes", []))
    grid = kw["grid"]
    nr = len(riders)
    nsem = 4 if kind == "gather" else 2

    def new_body(*refs):
        ins, srcs = refs[:n_in], refs[n_in:n_in + nr]
        outs, dsts = refs[n_in + nr:n_in + nr + n_out], refs[n_in + nr + n_out:n_in + 2 * nr + n_out]
        scr = refs[n_in + 2 * nr + n_out:n_in + 2 * nr + n_out + n_scr]
        sems = refs[n_in + 2 * nr + n_out + n_scr:]
        first = last = None
        for axis, size in enumerate(grid):
            i = pl.program_id(axis)
            first = (i == 0) if first is None else jnp.logical_and(first, i == 0)
            last = (i == size - 1) if last is None else jnp.logical_and(last, i == size - 1)
        x, y, c, chips = _place()
        k = 2 * x + y
        sib = (x, y, 1 - c)
        ssem, rsem = sems[:2]
        sends = []
        for a in range(nr):
            for j, (cx, cy) in enumerate(chips):
                if kind == "gather":
                    src, dst = srcs[a].at[c], dsts[a].at[k, c]
                else:
                    src, dst = srcs[a].at[2 * cx + cy], dsts[a].at[k]
                sends.append(_rcopy(src, dst, ssem.at[3 * a + j], rsem.at[3 * a + j], (cx, cy, c)))

        @pl.when(first)
        def _():
            for cp in sends:
                cp.start()

        body(*ins, *outs, *scr)

        @pl.when(last)
        def _():
            passed = []
            for a in range(nr):
                for j, (cx, cy) in enumerate(chips):
                    got = dsts[a].at[2 * cx + cy, c] if kind == "gather" else dsts[a].at[2 * cx + cy]
                    _rcopy(got, got, ssem.at[3 * a + j], rsem.at[3 * a + j], (cx, cy, c)).wait_recv()
                    if kind == "gather":
                        fw = _rcopy(got, got, sems[2].at[3 * a + j], sems[3].at[3 * a + j], sib)
                        fw.start()
                        passed.append(fw)
            if kind == "gather":
                for a in range(nr):
                    for j, (cx, cy) in enumerate(chips):
                        got = dsts[a].at[2 * cx + cy, 1 - c]
                        _rcopy(got, got, sems[2].at[3 * a + j], sems[3].at[3 * a + j], sib).wait_recv()
            for cp in sends + passed:
                cp.wait_send()

    kw = dict(kw)
    kw["in_specs"] = list(kw["in_specs"]) + [_ANY] * nr
    kw["out_specs"] = list(kw["out_specs"]) + [_ANY] * nr
    kw["out_shape"] = list(kw["out_shape"]) + [
        jax.ShapeDtypeStruct(((4,) + r.shape) if kind == "gather" else r.shape, r.dtype) for r in riders]
    kw["scratch_shapes"] = list(kw.get("scratch_shapes", [])) + [pltpu.SemaphoreType.DMA((3 * nr,))] * nsem
    return new_body, kw


def _call(body, name, sem, kw, args, rider=None):
    if rider is not None:
        body, kw = _with_riders(body, kw, *rider)
        args = tuple(args) + tuple(rider[1])
    return pl.pallas_call(body, name=name, compiler_params=_cparams(sem), **kw)(*args)


def _fill_own(result, own, chip):
    return lax.dynamic_update_index_in_dim(result, own, chip, 0)


def _gather_shards(bigs, small):
    nb = len(bigs)

    def body(*refs):
        ins, outs = refs[:nb + 1], refs[nb + 1:2 * nb + 2]
        ssem, rsem, fssem, frsem = refs[2 * nb + 2:]
        x, y, c, chips = _place()
        k = 2 * x + y
        sib = (x, y, 1 - c)

        def part(a, slot, hc):
            return outs[a].at[slot] if a == nb else outs[a].at[slot, hc]

        first = []
        for a in range(nb + 1):
            src = ins[a] if a == nb else ins[a].at[c]
            for j, (cx, cy) in enumerate(chips):
                first.append(_rcopy(src, part(a, k, c), ssem.at[3 * a + j], rsem.at[3 * a + j], (cx, cy, c)))
        for cp in first:
            cp.start()
        passed = []
        for a in range(nb + 1):
            for j, (cx, cy) in enumerate(chips):
                got = part(a, 2 * cx + cy, c)
                _rcopy(got, got, ssem.at[3 * a + j], rsem.at[3 * a + j], (cx, cy, c)).wait_recv()
                if a < nb:
                    fw = _rcopy(got, got, fssem.at[3 * a + j], frsem.at[3 * a + j], sib)
                    fw.start()
                    passed.append(fw)
        for a in range(nb):
            for j, (cx, cy) in enumerate(chips):
                got = part(a, 2 * cx + cy, 1 - c)
                _rcopy(got, got, fssem.at[3 * a + j], frsem.at[3 * a + j], sib).wait_recv()
        for cp in first + passed:
            cp.wait_send()

    arrs = list(bigs) + [small]
    n = 3 * (nb + 1)
    return pl.pallas_call(
        body, name="gather_shards",
        in_specs=[_ANY] * (nb + 1), out_specs=[_ANY] * (nb + 1),
        out_shape=[jax.ShapeDtypeStruct((4,) + a.shape, a.dtype) for a in arrs],
        scratch_shapes=[pltpu.SemaphoreType.DMA((n,)), pltpu.SemaphoreType.DMA((n,)),
                        pltpu.SemaphoreType.DMA((n,)), pltpu.SemaphoreType.DMA((n,))],
    )(*arrs)


def _swap_halves(grads, name):
    na = len(grads)
    halves = [g.shape[1] // 2 for g in grads]

    def body(*refs):
        ins, outs = refs[:na], refs[na:2 * na]
        ssem, rsem = refs[2 * na:]
        x, y, c, _ = _place()
        sib = (x, y, 1 - c)
        cps = [_rcopy(ins[a].at[:, pl.ds((1 - c) * halves[a], halves[a]), :], outs[a], ssem.at[a], rsem.at[a], sib)
               for a in range(na)]
        for cp in cps:
            cp.start()
        for cp in cps:
            cp.wait()

    return pl.pallas_call(
        body, name=name,
        in_specs=[_ANY] * na, out_specs=[_ANY] * na,
        out_shape=[jax.ShapeDtypeStruct((4, g.shape[1] // 2, g.shape[2]), g.dtype) for g in grads],
        scratch_shapes=[pltpu.SemaphoreType.DMA((na,)), pltpu.SemaphoreType.DMA((na,))],
    )(*grads)


def _sum_rows(rh):
    return rh if rh <= 512 else _tile(rh, (512, 256, 128, 64, 32))


def _chip_sum(grad, recv, core, name):
    _, r, cdim = grad.shape
    rh = r // 2
    tr = _sum_rows(rh)
    nblk = rh // tr

    def body(core_ref, g_ref, r_ref, o_ref):
        o_ref[...] = (g_ref[...] + r_ref[...]).astype(o_ref.dtype)

    return pl.pallas_call(
        body, name=name,
        grid_spec=pltpu.PrefetchScalarGridSpec(
            num_scalar_prefetch=1, grid=(4, nblk),
            in_specs=[pl.BlockSpec((None, tr, cdim), lambda s, i, cr: (s, cr[0] * nblk + i, 0)),
                      pl.BlockSpec((None, tr, cdim), lambda s, i, cr: (s, i, 0))],
            out_specs=pl.BlockSpec((None, tr, cdim), lambda s, i, cr: (s, i, 0))),
        out_shape=jax.ShapeDtypeStruct((4, rh, cdim), BF16),
        compiler_params=_cparams(("parallel", "parallel")),
    )(core, grad, recv)


def _scatter_sums(sums):
    na = len(sums)

    def body(*refs):
        ins, outs = refs[:na], refs[na:2 * na]
        ssem, rsem, lsem = refs[2 * na:]
        x, y, c, chips = _place()
        k = 2 * x + y
        local = [pltpu.make_async_copy(ins[a].at[k], outs[a].at[k], lsem.at[a]) for a in range(na)]
        for cp in local:
            cp.start()
        cps = []
        for a in range(na):
            for j, (cx, cy) in enumerate(chips):
                cps.append(_rcopy(ins[a].at[2 * cx + cy], outs[a].at[k], ssem.at[3 * a + j], rsem.at[3 * a + j],
                                  (cx, cy, c)))
        for cp in cps:
            cp.start()
        for a in range(na):
            for j, (cx, cy) in enumerate(chips):
                got = outs[a].at[2 * cx + cy]
                _rcopy(got, got, ssem.at[3 * a + j], rsem.at[3 * a + j], (cx, cy, c)).wait_recv()
        for cp in cps:
            cp.wait_send()
        for cp in local:
            cp.wait()

    return pl.pallas_call(
        body, name="scatter_sums",
        in_specs=[_ANY] * na, out_specs=[_ANY] * na,
        out_shape=[jax.ShapeDtypeStruct(s.shape, s.dtype) for s in sums],
        scratch_shapes=[pltpu.SemaphoreType.DMA((3 * na,)), pltpu.SemaphoreType.DMA((3 * na,)),
                        pltpu.SemaphoreType.DMA((na,))],
    )(*sums)


def _sum_chips(parts, name):
    _, rh, cdim = parts.shape
    tr = _sum_rows(rh)

    def body(p_ref, o_ref):
        acc = p_ref[0].astype(F32)
        for j in range(1, 4):
            acc = acc + p_ref[j].astype(F32)
        o_ref[...] = acc

    return pl.pallas_call(
        body, name=name, grid=(rh // tr,),
        in_specs=[pl.BlockSpec((4, tr, cdim), lambda i: (0, i, 0))],
        out_specs=pl.BlockSpec((tr, cdim), lambda i: (i, 0)),
        out_shape=jax.ShapeDtypeStruct((rh, cdim), F32),
        compiler_params=_cparams(("parallel",)),
    )(parts)


def _join_halves(reds):
    na = len(reds)

    def body(*refs):
        ins, outs = refs[:na], refs[na:2 * na]
        ssem, rsem = refs[2 * na:]
        x, y, c, _ = _place()
        cps = [_rcopy(ins[a], outs[a], ssem.at[a], rsem.at[a], (x, y, 1 - c)) for a in range(na)]
        for cp in cps:
            cp.start()
        for cp in cps:
            cp.wait()

    return pl.pallas_call(
        body, name="join_halves",
        in_specs=[_ANY] * na, out_specs=[_ANY] * na,
        out_shape=[jax.ShapeDtypeStruct(r.shape, r.dtype) for r in reds],
        scratch_shapes=[pltpu.SemaphoreType.DMA((na,)), pltpu.SemaphoreType.DMA((na,))],
    )(*reds)


def _allreduce_small(buf):
    n = buf.shape[0]

    def body(in_ref, out_ref, recv, ssem, rsem):
        x, y, c, _ = _place()
        peers = [(x, y, 1 - c), (1 - x, y, c), (x, 1 - y, c)]
        out_ref[...] = in_ref[...]
        for r, peer in enumerate(peers):
            cp = _rcopy(out_ref, recv.at[r], ssem.at[r], rsem.at[r], peer)
            cp.start()
            cp.wait()
            out_ref[...] = out_ref[...] + recv[r]

    vm = pl.BlockSpec(memory_space=pltpu.VMEM)
    return pl.pallas_call(
        body, name="allreduce_small",
        in_specs=[vm], out_specs=vm,
        out_shape=jax.ShapeDtypeStruct(buf.shape, F32),
        scratch_shapes=[pltpu.VMEM((3, n, 128), F32), pltpu.SemaphoreType.DMA((3,)), pltpu.SemaphoreType.DMA((3,))],
        compiler_params=pltpu.CompilerParams(vmem_limit_bytes=VMEM_LIMIT),
    )(buf)


_W_NAMES = ['meta_tokens', 'l0_mix_norm', 'l0_w_in', 'l0_ssd_conv_w', 'l0_ssd_conv_b', 'l0_ssd_dt_bias', 'l0_ssd_a_log',
            'l0_ssd_d', 'l0_ssd_norm', 'l0_ret_norm', 'l0_w_out', 'l0_ffn_norm', 'l0_ffn_w_in', 'l0_ffn_conv_w',
            'l0_ffn_conv_b', 'l0_ffn_w_out', 'l1_mix_norm', 'l1_w_in', 'l1_lru_conv_w', 'l1_lru_conv_b', 'l1_lru_wa',
            'l1_lru_ba', 'l1_lru_wx', 'l1_lru_bx', 'l1_lru_lambda', 'l1_w_out', 'l1_ffn_norm', 'l1_ffn_w_in',
            'l1_ffn_conv_w', 'l1_ffn_conv_b', 'l1_ffn_w_out', 'final_norm']
_IN_NAMES = ['x'] + _W_NAMES + ['loss_target'] + ['m_' + n for n in _W_NAMES] + ['v_' + n for n in _W_NAMES]
_BIG = ['l0_w_in', 'l0_w_out', 'l0_ffn_w_in', 'l0_ffn_w_out', 'l1_w_in', 'l1_w_out', 'l1_ffn_w_in', 'l1_ffn_w_out']
_BIG_COLS = ('l0_w_in', 'l0_ffn_w_in', 'l1_w_in', 'l1_ffn_w_in')
_SMALL_SHARDED = ['meta_tokens', 'l0_ssd_conv_w', 'l0_ffn_conv_w', 'l1_lru_conv_w', 'l1_ffn_conv_w']
_SMALL = [n for n in _W_NAMES if n not in _BIG]


def _pack(arrs):
    flat = []
    for a in arrs:
        v = a.reshape(-1).astype(F32)
        flat.append(jnp.pad(v, (0, (-v.shape[0]) % 128)))
    v = jnp.concatenate(flat)
    v = jnp.pad(v, (0, (-v.shape[0]) % 1024))
    return v.reshape(-1, 128)


def _unpack(buf, shapes):
    out, row = [], 0
    for sh in shapes:
        n = int(np.prod(sh))
        rows = -(-n // 128)
        out.append(buf[row:row + rows].reshape(-1)[:n].reshape(sh))
        row += rows
    return out


def kernel(x, meta_tokens, l0_mix_norm, l0_w_in, l0_ssd_conv_w, l0_ssd_conv_b, l0_ssd_dt_bias, l0_ssd_a_log, l0_ssd_d, l0_ssd_norm, l0_ret_norm, l0_w_out, l0_ffn_norm, l0_ffn_w_in, l0_ffn_conv_w, l0_ffn_conv_b, l0_ffn_w_out, l1_mix_norm, l1_w_in, l1_lru_conv_w, l1_lru_conv_b, l1_lru_wa, l1_lru_ba, l1_lru_wx, l1_lru_bx, l1_lru_lambda, l1_w_out, l1_ffn_norm, l1_ffn_w_in, l1_ffn_conv_w, l1_ffn_conv_b, l1_ffn_w_out, final_norm, loss_target, m_meta_tokens, m_l0_mix_norm, m_l0_w_in, m_l0_ssd_conv_w, m_l0_ssd_conv_b, m_l0_ssd_dt_bias, m_l0_ssd_a_log, m_l0_ssd_d, m_l0_ssd_norm, m_l0_ret_norm, m_l0_w_out, m_l0_ffn_norm, m_l0_ffn_w_in, m_l0_ffn_conv_w, m_l0_ffn_conv_b, m_l0_ffn_w_out, m_l1_mix_norm, m_l1_w_in, m_l1_lru_conv_w, m_l1_lru_conv_b, m_l1_lru_wa, m_l1_lru_ba, m_l1_lru_wx, m_l1_lru_bx, m_l1_lru_lambda, m_l1_w_out, m_l1_ffn_norm, m_l1_ffn_w_in, m_l1_ffn_conv_w, m_l1_ffn_conv_b, m_l1_ffn_w_out, m_final_norm, v_meta_tokens, v_l0_mix_norm, v_l0_w_in, v_l0_ssd_conv_w, v_l0_ssd_conv_b, v_l0_ssd_dt_bias, v_l0_ssd_a_log, v_l0_ssd_d, v_l0_ssd_norm, v_l0_ret_norm, v_l0_w_out, v_l0_ffn_norm, v_l0_ffn_w_in, v_l0_ffn_conv_w, v_l0_ffn_conv_b, v_l0_ffn_w_out, v_l1_mix_norm, v_l1_w_in, v_l1_lru_conv_w, v_l1_lru_conv_b, v_l1_lru_wa, v_l1_lru_ba, v_l1_lru_wx, v_l1_lru_bx, v_l1_lru_lambda, v_l1_w_out, v_l1_ffn_norm, v_l1_ffn_w_in, v_l1_ffn_conv_w, v_l1_ffn_conv_b, v_l1_ffn_w_out, v_final_norm):
    args = (x, meta_tokens, l0_mix_norm, l0_w_in, l0_ssd_conv_w, l0_ssd_conv_b, l0_ssd_dt_bias, l0_ssd_a_log, l0_ssd_d, l0_ssd_norm, l0_ret_norm, l0_w_out, l0_ffn_norm, l0_ffn_w_in, l0_ffn_conv_w, l0_ffn_conv_b, l0_ffn_w_out, l1_mix_norm, l1_w_in, l1_lru_conv_w, l1_lru_conv_b, l1_lru_wa, l1_lru_ba, l1_lru_wx, l1_lru_bx, l1_lru_lambda, l1_w_out, l1_ffn_norm, l1_ffn_w_in, l1_ffn_conv_w, l1_ffn_conv_b, l1_ffn_w_out, final_norm, loss_target, m_meta_tokens, m_l0_mix_norm, m_l0_w_in, m_l0_ssd_conv_w, m_l0_ssd_conv_b, m_l0_ssd_dt_bias, m_l0_ssd_a_log, m_l0_ssd_d, m_l0_ssd_norm, m_l0_ret_norm, m_l0_w_out, m_l0_ffn_norm, m_l0_ffn_w_in, m_l0_ffn_conv_w, m_l0_ffn_conv_b, m_l0_ffn_w_out, m_l1_mix_norm, m_l1_w_in, m_l1_lru_conv_w, m_l1_lru_conv_b, m_l1_lru_wa, m_l1_lru_ba, m_l1_lru_wx, m_l1_lru_bx, m_l1_lru_lambda, m_l1_w_out, m_l1_ffn_norm, m_l1_ffn_w_in, m_l1_ffn_conv_w, m_l1_ffn_conv_b, m_l1_ffn_w_out, m_final_norm, v_meta_tokens, v_l0_mix_norm, v_l0_w_in, v_l0_ssd_conv_w, v_l0_ssd_conv_b, v_l0_ssd_dt_bias, v_l0_ssd_a_log, v_l0_ssd_d, v_l0_ssd_norm, v_l0_ret_norm, v_l0_w_out, v_l0_ffn_norm, v_l0_ffn_w_in, v_l0_ffn_conv_w, v_l0_ffn_conv_b, v_l0_ffn_w_out, v_l1_mix_norm, v_l1_w_in, v_l1_lru_conv_w, v_l1_lru_conv_b, v_l1_lru_wa, v_l1_lru_ba, v_l1_lru_wx, v_l1_lru_bx, v_l1_lru_lambda, v_l1_w_out, v_l1_ffn_norm, v_l1_ffn_w_in, v_l1_ffn_conv_w, v_l1_ffn_conv_b, v_l1_ffn_w_out, v_final_norm)
    p = dict(zip(_IN_NAMES, args))
    B, seq, _ = x.shape
    nch = (seq + CH) // CH
    Pn = nch * CH
    R = B * Pn
    chip = 2 * lax.axis_index("x") + lax.axis_index("y")
    row2 = lambda v: v.reshape(1, -1)
    pad128 = lambda v: jnp.pad(v, (0, 128 - v.shape[0])).reshape(1, 128)

    small_shapes = [p[n].shape for n in _SMALL_SHARDED]
    halved = lambda w: w.astype(_MXU).reshape(2, w.shape[0] // 2, w.shape[1])
    mine = {n: halved(p[n]) for n in _BIG}
    mine_small = _pack([p[n] for n in _SMALL_SHARDED])
    W = {}

    def set_weight(n, g):
        g = _fill_own(g, mine[n], chip)
        g = g.reshape(4, -1, g.shape[3])
        W[n] = jnp.concatenate([g[k] for k in range(4)], axis=1) if n in _BIG_COLS else g.reshape(-1, g.shape[2])

    def gather_on(*names):
        return ("gather", [mine[n] for n in names])

    def take_weights(names, got):
        for n, g in zip(names, got):
            set_weight(n, g)

    gathered = _gather_shards([mine['l0_w_in']], mine_small)
    set_weight('l0_w_in', gathered[0])
    g_small = _fill_own(gathered[-1], mine_small, chip)
    per_chip = [_unpack(g_small[k], small_shapes) for k in range(4)]
    for i, n in enumerate(_SMALL_SHARDED):
        W[n] = jnp.concatenate([per_chip[k][i] for k in range(4)], axis=1)
    w0 = W['l0_w_in']
    w0_main = jnp.concatenate([w0[:, 3088:], w0[:, :3072]], axis=1)
    w0_dt = jnp.pad(w0[:, 3072:3088], ((0, 0), (0, 112)))
    cos, sin = _rope_tables(nch)

    meta = jnp.broadcast_to(W['meta_tokens'][None], (B, N_META, D))
    h0 = jnp.concatenate([jnp.zeros((B, PAD, D), F32), meta, x], axis=1).reshape(R, D)
    n0, n0t = _rmsnorm_fwd(h0, row2(p['l0_mix_norm']), "norm_l0_mix")
    u0 = _mm(n0, w0_main, "nn", F32, "l0_in_proj")
    udt = _mm(n0, w0_dt, "nn", F32, "l0_dt_proj")
    a_log, d_skip, dt_bias = pad128(p['l0_ssd_a_log']), pad128(p['l0_ssd_d']), pad128(p['l0_ssd_dt_bias'])
    ssd_cb = row2(p['l0_ssd_conv_b'])
    act, dt, dtt, *got = _ssd_prep(u0, udt, W['l0_ssd_conv_w'], ssd_cb, dt_bias, B, nch, rider=gather_on('l0_w_out'))
    take_weights(['l0_w_out'], got)
    ycat0, ypre, hin, *got = _ssd_fwd(act, u0, dt, dtt, a_log, d_skip, row2(p['l0_ssd_norm']), B, nch,
                                      rider=gather_on('l0_ffn_w_in'))
    take_weights(['l0_ffn_w_in'], got)
    ycat0, opre, rin, *got = _ret_fwd(u0, ycat0, cos, sin, row2(p['l0_ret_norm']), B, nch,
                                      rider=gather_on('l0_ffn_w_out'))
    take_weights(['l0_ffn_w_out'], got)
    h1 = _mm(ycat0, W['l0_w_out'], "nn", F32, "l0_out_proj", add=h0)
    n1, n1t = _rmsnorm_fwd(h1, row2(p['l0_ffn_norm']), "norm_l0_ffn")
    uf0 = _mm(n1, W['l0_ffn_w_in'], "nn", F32, "l0_ffn_in")
    f0_cb = row2(p['l0_ffn_conv_b'])
    a0, *got = _ffn_act_fwd(uf0, W['l0_ffn_conv_w'], f0_cb, B, nch, rider=gather_on('l1_w_in'))
    take_weights(['l1_w_in'], got)
    h2 = _mm(a0, W['l0_ffn_w_out'], "nn", F32, "l0_ffn_out", add=h1)
    n2, n2t = _rmsnorm_fwd(h2, row2(p['l1_mix_norm']), "norm_l1_mix")
    u1 = _mm(n2, W['l1_w_in'], "nn", F32, "l1_in_proj")
    lru = (W['l1_lru_conv_w'], row2(p['l1_lru_conv_b']), p['l1_lru_wa'], row2(p['l1_lru_ba']), p['l1_lru_wx'],
           row2(p['l1_lru_bx']), row2(p['l1_lru_lambda']))
    later = ['l1_w_out', 'l1_ffn_w_in', 'l1_ffn_w_out']
    ycat1, *got = _sb_fwd(u1, B, nch, rider=gather_on(*later))
    take_weights(later, got)
    ycat1, hs = _lru_fwd(u1, ycat1, *lru, B, nch)
    h3 = _mm(ycat1, W['l1_w_out'], "nn", F32, "l1_out_proj", add=h2)
    n3, n3t = _rmsnorm_fwd(h3, row2(p['l1_ffn_norm']), "norm_l1_ffn")
    uf1 = _mm(n3, W['l1_ffn_w_in'], "nn", F32, "l1_ffn_in")
    f1_cb = row2(p['l1_ffn_conv_b'])
    a1, = _ffn_act_fwd(uf1, W['l1_ffn_conv_w'], f1_cb, B, nch)
    h4 = _mm(a1, W['l1_ffn_w_out'], "nn", F32, "l1_ffn_out", add=h3)
    dh4, dh4b, lossp, dgf = _head(h4, row2(p['final_norm']), p['loss_target'].reshape(B * seq, D), B, nch)
    loss = lax.psum(jnp.sum(lossp[:, 0, 0]), ("x", "y", "c"))

    G = {'final_norm': dgf[:, 0].sum(0)}

    core = lax.axis_index("c").reshape(1).astype(jnp.int32)

    def col_shards(pieces):
        edges = np.cumsum([0] + [q.shape[1] for q in pieces])
        cs = int(edges[-1]) // 4
        shards = []
        for k in range(4):
            lo, hi = k * cs, (k + 1) * cs
            cut = [q[:, max(lo - e0, 0):min(hi - e0, q.shape[1])]
                   for q, e0, e1 in zip(pieces, edges[:-1], edges[1:]) if e0 < hi and e1 > lo]
            shards.append(cut[0] if len(cut) == 1 else jnp.concatenate(cut, axis=1))
        return jnp.stack(shards)

    def chip_sums(names, tag):
        stacked = [G[n] if n in _BIG_COLS else G[n].reshape(4, G[n].shape[0] // 4, G[n].shape[1]) for n in names]
        theirs = _swap_halves(stacked, "swap_halves_" + tag)
        return {n: _chip_sum(g, t, core, "chip_sum_" + n) for n, g, t in zip(names, stacked, theirs)}

    parts = {}

    def scatter_on(names, tag):
        sums = chip_sums(names, tag)
        return sums, ("scatter", [sums[n] for n in names])

    def take_parts(names, sums, got):
        for n, g in zip(names, got):
            parts[n] = _fill_own(g, lax.dynamic_index_in_dim(sums[n], chip, 0, keepdims=False), chip)

    def ffn_bwd(layer, dh_out, dhb_out, h_in, nt_in, uf, a_act, cb, rider=None):
        pre = f"l{layer}_"
        w_in, w_out, cw = W[pre + 'ffn_w_in'], W[pre + 'ffn_w_out'], W[pre + 'ffn_conv_w']
        da = _mm(dhb_out, w_out, "nt", F32, pre + "ffn_out_dgrad")
        G[pre + 'ffn_w_out'] = _mm(a_act, dhb_out, "tn", F32, pre + "ffn_out_wgrad")
        dug, duu, dwg, dwu, *rode = _ffn_act_bwd(da, uf, cw, cb, nch, pre + "ffn_act_bwd", rider=rider)
        G[pre + 'ffn_conv_w'] = jnp.concatenate([dwg[:3], dwu[:3]], axis=1)
        G[pre + 'ffn_conv_b'] = jnp.concatenate([dwg[7], dwu[7]])
        dn = _mm(dug, w_in, "nt", F32, pre + "ffn_in_dgrad_g")
        dn = _mm(duu, w_in, "nt", F32, pre + "ffn_in_dgrad_u", add=dn, b_off=FFN)
        G[pre + 'ffn_w_in'] = col_shards([_mm(nt_in, dug, "nn", F32, pre + "ffn_in_wgrad_g"),
                                          _mm(nt_in, duu, "nn", F32, pre + "ffn_in_wgrad_u")])
        dh_in, dhb_in, dg = _rmsnorm_bwd(h_in, row2(p[pre + 'ffn_norm']), dn, dh_out, nch, pre + "ffn_norm_bwd")
        G[pre + 'ffn_norm'] = dg[0]
        return dh_in, dhb_in, rode

    dh3, dh3b, _ = ffn_bwd(1, dh4, dh4b, h3, n3t, uf1, a1, f1_cb)
    dy1 = _mm(dh3b, W['l1_w_out'], "nt", F32, "l1_out_dgrad")
    G['l1_w_out'] = _mm(ycat1, dh3b, "tn", F32, "l1_out_wgrad")
    done = ['l1_ffn_w_in', 'l1_ffn_w_out', 'l1_w_out']
    sums, rider = scatter_on(done, "a")
    dq, dkt, dvt, *got = _sb_bwd(dy1, u1, B, nch, rider=rider)
    dk, dv = dkt.T, dvt.T
    take_parts(done, sums, got)
    dgate, dxc, pgl, dwa, dwx = _lru_bwd(dy1, u1, hs, *lru, B, nch)
    dxr, dcw = _conv_bwd(dxc, u1, 4096, W['l1_lru_conv_w'], 4, "l1_lru_conv_bwd")
    pgl = pgl.sum(0)
    G['l1_lru_ba'], G['l1_lru_bx'], G['l1_lru_lambda'] = pgl[0], pgl[1], pgl[2]
    G['l1_lru_wa'], G['l1_lru_wx'] = dwa.sum(0), dwx.sum(0)
    G['l1_lru_conv_w'], G['l1_lru_conv_b'] = dcw[:4], dcw[7]
    du1 = jnp.concatenate([piece.astype(_MXU) for piece in (dq, dk, dv, dgate, dxr)], axis=1)
    dn = _mm(du1, W['l1_w_in'], "nt", F32, "l1_in_dgrad")
    G['l1_w_in'] = col_shards([_mm(n2t, du1, "nn", F32, "l1_in_wgrad")])
    dh2, dh2b, dg = _rmsnorm_bwd(h2, row2(p['l1_mix_norm']), dn, dh3, nch, "l1_mix_norm_bwd")
    G['l1_mix_norm'] = dg[0]

    dh1, dh1b, _ = ffn_bwd(0, dh2, dh2b, h1, n1t, uf0, a0, f0_cb)
    dy0 = _mm(dh1b, W['l0_w_out'], "nt", F32, "l0_out_dgrad")
    G['l0_w_out'] = _mm(ycat0, dh1b, "tn", F32, "l0_out_wgrad")
    done = ['l1_w_in', 'l0_ffn_w_in', 'l0_ffn_w_out', 'l0_w_out']
    sums, rider = scatter_on(done, "b")
    dz, dxs, dbm, dcm, ddt4, pgs, *got = _ssd_bwd(dy0, ypre, u0, act, dt, dtt, hin, a_log, d_skip,
                                                  row2(p['l0_ssd_norm']), B, nch, rider=rider)
    take_parts(done, sums, got)
    dpre, ddtr, pgd = _ssd_prep_bwd(dxs, dbm, dcm, ddt4, u0, udt, W['l0_ssd_conv_w'], ssd_cb, dt_bias, B, nch)
    dxbc, dcw0 = _conv_bwd(dpre, u0, U0_XBC, W['l0_ssd_conv_w'], 4, "l0_ssd_conv_bwd")
    dqkvg, pgr = _ret_bwd(dy0, u0, opre, rin, cos, sin, row2(p['l0_ret_norm']), B, nch)
    pgs = pgs.sum(0)
    G['l0_ssd_norm'] = pgs[:, 0, :].reshape(-1)
    G['l0_ssd_d'] = pgs[:, 1, :128].sum(0)[:SSD_HEADS]
    G['l0_ssd_a_log'] = pgs[:, 2, :128].sum(0)[:SSD_HEADS]
    G['l0_ssd_dt_bias'] = pgd.sum(0)[0, :SSD_HEADS]
    G['l0_ssd_conv_w'], G['l0_ssd_conv_b'] = dcw0[:4], dcw0[7]
    G['l0_ret_norm'] = pgr.sum(0)[0]
    dn = _mm(dqkvg, w0_main, "nt", F32, "l0_in_dgrad_qkvg")
    dn = _mm(dz, w0_main, "nt", F32, "l0_in_dgrad_z", add=dn, b_off=U0_Z)
    dn = _mm(dxbc, w0_main, "nt", F32, "l0_in_dgrad_xbc", add=dn, b_off=U0_XBC)
    dn = _mm(ddtr, w0_dt, "nt", F32, "l0_in_dgrad_dt", add=dn)
    G['l0_w_in'] = col_shards([
        _mm(n0t, dz, "nn", F32, "l0_in_wgrad_z"), _mm(n0t, dxbc, "nn", F32, "l0_in_wgrad_xbc"),
        _mm(n0t, ddtr, "nn", F32, "l0_in_wgrad_dt")[:, :SSD_HEADS], _mm(n0t, dqkvg, "nn", F32, "l0_in_wgrad_qkvg")])
    dh0, _, dg = _rmsnorm_bwd(h0, row2(p['l0_mix_norm']), dn, dh1, nch, "l0_mix_norm_bwd")
    G['l0_mix_norm'] = dg[0]
    dh0 = dh0.reshape(B, Pn, D)
    grad_x = dh0[:, CH:]
    G['meta_tokens'] = dh0[:, PAD:CH].sum(0)

    sums = chip_sums(['l0_w_in'], "d")
    parts['l0_w_in'], = _scatter_sums([sums['l0_w_in']])
    reds = [_sum_chips(parts[n], "sum_chips_" + n) for n in _BIG]
    grads = {}
    for n, own, other in zip(_BIG, reds, _join_halves(reds)):
        both = jnp.where(core[0] == 0, jnp.stack([own, other]), jnp.stack([other, own]))
        grads[n] = both.reshape(-1, both.shape[2])
    small_full = _unpack(_allreduce_small(_pack([G[n] for n in _SMALL])), [G[n].shape for n in _SMALL])
    for n, g in zip(_SMALL, small_full):
        if n in _SMALL_SHARDED:
            cs = g.shape[1] // 4
            g = lax.dynamic_slice_in_dim(g, chip * cs, cs, axis=1)
        grads[n] = g.reshape(p[n].shape)

    delta, new_m, new_v = {}, {}, {}
    for n in _BIG:
        delta[n], new_m[n], new_v[n] = _adamw(p[n], grads[n], p['m_' + n], p['v_' + n], "adamw_" + n)
    shapes = [p[n].shape for n in _SMALL]
    outs = _adamw(_pack([p[n] for n in _SMALL]), _pack([grads[n] for n in _SMALL]), _pack([p['m_' + n] for n in _SMALL]),
                  _pack([p['v_' + n] for n in _SMALL]), "adamw_small")
    for dst, buf in zip((delta, new_m, new_v), outs):
        for n, a in zip(_SMALL, _unpack(buf, shapes)):
            dst[n] = a
    return (loss, grad_x, *[grads[n] for n in _W_NAMES], *[delta[n] for n in _W_NAMES],
            *[new_m[n] for n in _W_NAMES], *[new_v[n] for n in _W_NAMES])
```

```python
import math

import numpy as np
import jax
import jax.numpy as jnp
from jax import lax
from jax.experimental import pallas as pl
from jax.experimental.pallas import tpu as pltpu

F32 = jnp.float32
BF16 = jnp.bfloat16
_MXU = jnp.bfloat16

D = 1024
CH = 128
N_META = 16
PAD = CH - N_META
EPS = 1e-6

SSD_HEADS = 16
SSD_HD = 64
SSD_GROUPS = 4
RET_HEADS = 4
RET_DK = 256
SB_HEADS = 16
SB_HD = 64
LRU_BLOCKS = 8
LRU_C = 8.0
FFN = 2816
U0_Z = 4096
U0_XBC = 5120

VMEM_LIMIT = 56 * 1024 * 1024


def _cparams(sem):
    return pltpu.CompilerParams(dimension_semantics=sem, vmem_limit_bytes=VMEM_LIMIT)


def _dot(a, b, dims=((1,), (0,))):
    return lax.dot_general(a.astype(_MXU), b.astype(_MXU), (dims, ((), ())), preferred_element_type=F32)


def _dot_nt(a, b):
    return _dot(a, b, ((1,), (1,)))


def _dot_tn(a, b):
    return _dot(a.T, b)


def _dot_exact(a, b):
    return lax.dot_general(a, b, (((1,), (0,)), ((), ())), preferred_element_type=F32,
                           precision=lax.Precision.HIGHEST)


def _dot_split(x, m01):
    hi = x.astype(BF16)
    lo = (x - hi.astype(F32)).astype(BF16)
    m = m01.astype(BF16)
    return jnp.dot(hi, m, preferred_element_type=F32) + jnp.dot(lo, m, preferred_element_type=F32)


def _sigmoid(x):
    return 0.5 * jnp.tanh(0.5 * x) + 0.5


def _softplus(x):
    return jnp.maximum(x, 0.0) + jnp.log1p(jnp.exp(-jnp.abs(x)))


def _silu(x):
    return x * _sigmoid(x)


def _dsilu(x):
    s = _sigmoid(x)
    return s * (1.0 + x * (1.0 - s))


_GELU_C = math.sqrt(2.0 / math.pi)


def _gelu(x):
    return 0.5 * x * (1.0 + jnp.tanh(_GELU_C * (x + 0.044715 * x * x * x)))


def _dgelu(x):
    t = jnp.tanh(_GELU_C * (x + 0.044715 * x * x * x))
    return 0.5 * (1.0 + t) + 0.5 * x * (1.0 - t * t) * _GELU_C * (1.0 + 3.0 * 0.044715 * x * x)


def _row_ids(n, cols=1):
    return lax.broadcasted_iota(jnp.int32, (n, cols), 0)


def _lane_ids(rows, n):
    return lax.broadcasted_iota(jnp.int32, (rows, n), 1)


def _real_rows(chunk):
    return chunk * CH + _row_ids(CH) >= PAD


def _shift_down(prev8, cur, s):
    cat = jnp.concatenate([prev8, cur], axis=0)
    return pltpu.roll(cat, s, axis=0)[8:]


def _shift_up(cur, next8, s):
    n = cur.shape[0]
    cat = jnp.concatenate([cur, next8], axis=0)
    return pltpu.roll(cat, n + 8 - s, axis=0)[:n]


def _conv_pre(prev8, cur, w_ref, b_ref, K):
    acc = cur * w_ref[K - 1:K, :] + b_ref[...]
    for s in range(1, K):
        acc = acc + _shift_down(prev8, cur, s) * w_ref[K - 1 - s:K - s, :]
    return acc


def _prev8_map(nch, col):
    return lambda b, c: (jnp.maximum((b * nch + c) * (CH // 8) - 1, 0), col)


def _matmul(a, b, mode, out_dtype, tm, tn, tk, name, add=None, b_off=0):
    if mode == "nn":
        (M, K), (_, N) = a.shape, b.shape
    elif mode == "nt":
        (M, K), N = a.shape, b.shape[0]
    else:
        (K, M), (_, N) = a.shape, b.shape
    tm, tn, tk = min(tm, M), min(tn, N), min(tk, K)
    assert M % tm == 0 and N % tn == 0 and K % tk == 0 and b_off % tk == 0, (name, M, N, K, tm, tn, tk)
    koff = b_off // tk
    nk = K // tk
    dims = {"nn": ((1,), (0,)), "nt": ((1,), (1,)), "tn": ((0,), (0,))}[mode]
    if mode == "tn":
        a_spec = pl.BlockSpec((tk, tm), lambda i, j, k: (k, i))
    else:
        a_spec = pl.BlockSpec((tm, tk), lambda i, j, k: (i, k))
    if mode == "nt":
        b_spec = pl.BlockSpec((tn, tk), lambda i, j, k: (j, k + koff))
    else:
        b_spec = pl.BlockSpec((tk, tn), lambda i, j, k: (k, j))
    o_spec = pl.BlockSpec((tm, tn), lambda i, j, k: (i, j))
    has_add = add is not None

    def body(a_ref, b_ref, *rest):
        if has_add:
            add_ref, o_ref, acc = rest
        else:
            o_ref, acc = rest
        k = pl.program_id(2)

        @pl.when(k == 0)
        def _():
            acc[...] = jnp.zeros_like(acc)

        acc[...] += _dot(a_ref[...], b_ref[...], dims)

        @pl.when(k == nk - 1)
        def _():
            r = acc[...]
            if has_add:
                r = r + add_ref[...].astype(F32)
            o_ref[...] = r.astype(out_dtype)

    in_specs = [a_spec, b_spec] + ([o_spec] if has_add else [])
    args = (a, b) + ((add,) if has_add else ())
    return pl.pallas_call(
        body, name=name, grid=(M // tm, N // tn, nk),
        in_specs=in_specs, out_specs=o_spec,
        out_shape=jax.ShapeDtypeStruct((M, N), out_dtype),
        scratch_shapes=[pltpu.VMEM((tm, tn), F32)],
        compiler_params=_cparams(("parallel", "parallel", "arbitrary")),
    )(*args)


def _tile(n, prefs):
    for t in prefs:
        if n % t == 0:
            return t
    return n


def _mm(a, b, mode, out_dtype, name, add=None, b_off=0):
    if mode == "tn":
        K, M = a.shape
        N = b.shape[1]
        tm, tn, tk = _tile(M, (1024, 1408, 512, 256, 128)), _tile(N, (1024, 1408, 512, 256, 128)), _tile(K, (2176, 384, 256, 128))
    else:
        M, K = a.shape
        N = b.shape[1] if mode == "nn" else b.shape[0]
        tm = _tile(M, (1088, 1024, 768, 512, 384, 256, 128))
        tn = _tile(N, (1024, 1408, 512, 256, 128))
        tk = _tile(K, (2176, 1024, 1408, 512, 256, 128))
    return _matmul(a, b, mode, out_dtype, tm, tn, tk, name, add=add, b_off=b_off)


def _rmsnorm_fwd(h, g, name):
    R = h.shape[0]
    tr = 2 * CH

    def body(h_ref, g_ref, o_ref, ot_ref):
        x = h_ref[...]
        r = lax.rsqrt(jnp.mean(x * x, axis=-1, keepdims=True) + EPS)
        y = x * r * g_ref[...]
        o_ref[...] = y.astype(o_ref.dtype)
        ot_ref[...] = y.T.astype(ot_ref.dtype)

    return pl.pallas_call(
        body, name=name, grid=(R // tr,),
        in_specs=[pl.BlockSpec((tr, D), lambda i: (i, 0)), pl.BlockSpec((1, D), lambda i: (0, 0))],
        out_specs=[pl.BlockSpec((tr, D), lambda i: (i, 0)), pl.BlockSpec((D, tr), lambda i: (0, i))],
        out_shape=[jax.ShapeDtypeStruct((R, D), _MXU), jax.ShapeDtypeStruct((D, R), _MXU)],
        compiler_params=_cparams(("parallel",)),
    )(h, g)


def _rmsnorm_bwd(h, g, dn, dres, nch, name, rider=None):
    R = h.shape[0]
    per = 4
    tr = nch * CH // per

    def body(h_ref, g_ref, dn_ref, dres_ref, dh_ref, dhb_ref, dg_ref):
        i = pl.program_id(0)
        x = h_ref[...]
        r = lax.rsqrt(jnp.mean(x * x, axis=-1, keepdims=True) + EPS)
        xhat = x * r
        dn_v = dn_ref[...]
        dx = dn_v * g_ref[...]
        dh = r * (dx - xhat * jnp.mean(dx * xhat, axis=-1, keepdims=True))
        keep = (i % per) * tr + _row_ids(tr) >= PAD
        total = jnp.where(keep, dres_ref[...] + dh, 0.0)
        dh_ref[...] = total
        dhb_ref[...] = total.astype(dhb_ref.dtype)

        @pl.when(i == 0)
        def _():
            dg_ref[...] = jnp.zeros_like(dg_ref)

        dg_ref[...] += jnp.sum(dn_v * xhat, axis=0, keepdims=True)

    row = pl.BlockSpec((tr, D), lambda i: (i, 0))
    vec = pl.BlockSpec((1, D), lambda i: (0, 0))
    kw = dict(grid=(R // tr,), in_specs=[row, vec, row, row], out_specs=[row, row, vec],
              out_shape=[jax.ShapeDtypeStruct((R, D), F32), jax.ShapeDtypeStruct((R, D), _MXU),
                         jax.ShapeDtypeStruct((1, D), F32)])
    return _call(body, name, ("arbitrary",), kw, (h, g, dn, dres), rider)


def _ssd_prep(u0, udt, conv_w, conv_b, dt_bias, B, nch, rider=None):
    R = u0.shape[0]

    def body(xs_ref, xsp_ref, bc_ref, bcp_ref, udt_ref, w0_ref, w1_ref, b0_ref, b1_ref, dtb_ref,
             act_ref, dt_ref, dtt_ref):
        keep = _real_rows(pl.program_id(1))
        a0 = _silu(_conv_pre(xsp_ref[...], xs_ref[...], w0_ref, b0_ref, 4))
        a1 = _silu(_conv_pre(bcp_ref[...], bc_ref[...], w1_ref, b1_ref, 4))
        act_ref[:, :1024] = jnp.where(keep, a0, 0.0)
        act_ref[:, 1024:] = jnp.where(keep, a1, 0.0)
        ok = jnp.logical_and(keep, _lane_ids(1, 128) < SSD_HEADS)
        dt = jnp.where(ok, _softplus(udt_ref[...] + dtb_ref[...]), 0.0)
        dt_ref[...] = dt
        dtt_ref[...] = dt.T

    row = lambda col: pl.BlockSpec((CH, 1024), lambda b, c: (b * nch + c, col))
    prev = lambda col: pl.BlockSpec((8, 1024), _prev8_map(nch, col))
    kw = dict(
        grid=(B, nch),
        in_specs=[row(5), prev(5), row(6), prev(6),
                  pl.BlockSpec((CH, 128), lambda b, c: (b * nch + c, 0)),
                  pl.BlockSpec((4, 1024), lambda b, c: (0, 0)), pl.BlockSpec((4, 1024), lambda b, c: (0, 1)),
                  pl.BlockSpec((1, 1024), lambda b, c: (0, 0)), pl.BlockSpec((1, 1024), lambda b, c: (0, 1)),
                  pl.BlockSpec((1, 128), lambda b, c: (0, 0))],
        out_specs=[pl.BlockSpec((CH, 2048), lambda b, c: (b * nch + c, 0)),
                   pl.BlockSpec((CH, 128), lambda b, c: (b * nch + c, 0)),
                   pl.BlockSpec((128, CH), lambda b, c: (0, b * nch + c))],
        out_shape=[jax.ShapeDtypeStruct((R, 2048), F32), jax.ShapeDtypeStruct((R, 128), F32),
                   jax.ShapeDtypeStruct((128, R), F32)])
    return _call(body, "ssd_prep", ("arbitrary", "arbitrary"), kw,
                 (u0, u0, u0, u0, udt, conv_w, conv_w, conv_b, conv_b, dt_bias), rider)


def _ssd_head_terms(h, a_vec, dt_v, dtt_v, dsk_v):
    lane = _lane_ids(1, 128)
    sub = _row_ids(128)
    r = _row_ids(CH, CH)
    cidx = _lane_ids(CH, CH)
    a_h = jnp.sum(jnp.where(lane == h, a_vec, 0.0), axis=1, keepdims=True)
    dt_col = jnp.sum(jnp.where(lane == h, dt_v, 0.0), axis=1, keepdims=True)
    dt_row = jnp.sum(jnp.where(sub == h, dtt_v, 0.0), axis=0, keepdims=True)
    cs_col = jnp.sum(jnp.where(r >= cidx, dt_row * a_h, 0.0), axis=1, keepdims=True)
    cs_row = jnp.sum(jnp.where(r <= cidx, dt_col * a_h, 0.0), axis=0, keepdims=True)
    tot = jnp.sum(dt_col * a_h, axis=0, keepdims=True)
    dsk = jnp.sum(jnp.where(lane == h, dsk_v, 0.0), axis=1, keepdims=True)
    return a_h, dt_col, cs_col, cs_row, tot, dsk


def _ssd_fwd(act, u0, dt, dtt, a_log, d_skip, norm_g, B, nch, rider=None):
    R = act.shape[0]

    def body(xs_ref, bm_ref, cm_ref, z_ref, dt_ref, dtt_ref, alog_ref, dsk_ref, ng_ref,
             out_ref, ypre_ref, hin_ref, H):
        g = pl.program_id(1)
        c = pl.program_id(2)

        @pl.when(c == 0)
        def _():
            H[...] = jnp.zeros_like(H)

        hin_ref[...] = H[...]
        a_vec = -jnp.exp(alog_ref[...])
        dt_v = dt_ref[...]
        dtt_v = dtt_ref[...]
        hm = _lane_ids(1, 128) < SSD_HD
        r = _row_ids(CH, CH)
        cidx = _lane_ids(CH, CH)
        Bm = bm_ref[...]
        Cm = cm_ref[...]
        CB = _dot_nt(Cm, Bm)
        ys = []
        for pair in range(2):
            cols = slice(128 * pair, 128 * pair + 128)
            xraw = xs_ref[:, cols]
            t = [_ssd_head_terms(4 * g + 2 * pair + j, a_vec, dt_v, dtt_v, dsk_ref[...]) for j in range(2)]
            sel = lambda f: jnp.where(hm, f(t[0]), f(t[1]))
            dtp = sel(lambda q: q[1])
            Ep = sel(lambda q: jnp.exp(q[2]))
            Wp = sel(lambda q: jnp.exp(q[4] - q[2]))
            etot = sel(lambda q: jnp.exp(q[4]))
            dsk = sel(lambda q: q[5])
            X = xraw * dtp
            ydiag = jnp.zeros((CH, 128), F32)
            for j in range(2):
                Lm = jnp.where(r >= cidx, jnp.exp(t[j][2] - t[j][3]), 0.0)
                Xh = jnp.where(hm if j == 0 else jnp.logical_not(hm), X, 0.0)
                ydiag = ydiag + _dot(CB * Lm, Xh)
            Hp = H[:, cols]
            yoff = Ep * _dot(Cm, Hp)
            S = _dot(Bm.T, X * Wp)
            H[:, cols] = etot * Hp + S
            ys.append(ydiag + yoff + xraw * dsk)
        y = jnp.concatenate(ys, axis=1)
        ypre_ref[...] = y
        yg = y * _silu(z_ref[...])
        rr = lax.rsqrt(jnp.mean(yg * yg, axis=-1, keepdims=True) + EPS)
        out_ref[...] = jnp.where(_real_rows(c), yg * rr * ng_ref[...], 0.0).astype(out_ref.dtype)

    rowb = lambda w, colf: pl.BlockSpec((CH, w), lambda b, g, c: (b * nch + c, colf(g)))
    vec = pl.BlockSpec((1, 128), lambda b, g, c: (0, 0))
    kw = dict(
        grid=(B, SSD_GROUPS, nch),
        in_specs=[rowb(256, lambda g: g), rowb(128, lambda g: 8 + g), rowb(128, lambda g: 12 + g),
                  rowb(256, lambda g: 16 + g), rowb(128, lambda g: 0),
                  pl.BlockSpec((128, CH), lambda b, g, c: (0, b * nch + c)),
                  vec, vec, pl.BlockSpec((1, 256), lambda b, g, c: (0, g))],
        out_specs=[rowb(256, lambda g: g), rowb(256, lambda g: g),
                   pl.BlockSpec((None, None, None, 128, 256), lambda b, g, c: (b, g, c, 0, 0))],
        out_shape=[jax.ShapeDtypeStruct((R, 2048), _MXU), jax.ShapeDtypeStruct((R, 1024), F32),
                   jax.ShapeDtypeStruct((B, SSD_GROUPS, nch, 128, 256), F32)],
        scratch_shapes=[pltpu.VMEM((128, 256), F32)])
    return _call(body, "ssd_fwd", ("arbitrary", "arbitrary", "arbitrary"), kw,
                 (act, act, act, u0, dt, dtt, a_log, d_skip, norm_g), rider)


def _ssd_bwd(dycat, ypre, u0, act, dt, dtt, hin, a_log, d_skip, norm_g, B, nch, rider=None):
    R = act.shape[0]

    def body(dy_ref, ypre_ref, z_ref, xs_ref, bm_ref, cm_ref, dt_ref, dtt_ref, hin_ref, alog_ref, dsk_ref, ng_ref,
             dz_ref, dxs_ref, db_ref, dc_ref, ddt_ref, pg_ref, dH):
        g = pl.program_id(1)
        c = nch - 1 - pl.program_id(2)

        @pl.when(pl.program_id(2) == 0)
        def _():
            dH[...] = jnp.zeros_like(dH)
            pg_ref[...] = jnp.zeros_like(pg_ref)

        z = z_ref[...]
        y = ypre_ref[...]
        ng = ng_ref[...]
        dout = jnp.where(_real_rows(c), dy_ref[...], 0.0)
        sz = _sigmoid(z)
        yg = y * z * sz
        rr = lax.rsqrt(jnp.mean(yg * yg, axis=-1, keepdims=True) + EPS)
        nrm = yg * rr
        pg_ref[0:1, :] += jnp.sum(dout * nrm, axis=0, keepdims=True)
        dn = dout * ng
        dyg = rr * (dn - nrm * jnp.mean(dn * nrm, axis=-1, keepdims=True))
        dy = dyg * z * sz
        dz_ref[...] = (dyg * y * (sz * (1.0 + z * (1.0 - sz)))).astype(dz_ref.dtype)

        a_vec = -jnp.exp(alog_ref[...])
        dt_v = dt_ref[...]
        dtt_v = dtt_ref[...]
        lane = _lane_ids(1, 128)
        hm = lane < SSD_HD
        r = _row_ids(CH, CH)
        cidx = _lane_ids(CH, CH)
        last = _row_ids(CH) == CH - 1
        Bm = bm_ref[...]
        Cm = cm_ref[...]
        CB = _dot_nt(Cm, Bm)
        CBT = _dot_nt(Bm, Cm)
        dB = jnp.zeros((CH, 128), F32)
        dC = jnp.zeros((CH, 128), F32)
        dcs_all = jnp.zeros((CH, 128), F32)
        dtx_all = jnp.zeros((CH, 128), F32)
        dd_row = jnp.zeros((1, 128), F32)
        dxs = []
        for pair in range(2):
            cols = slice(128 * pair, 128 * pair + 128)
            xraw = xs_ref[:, cols]
            dyp = dy[:, cols]
            heads = [4 * g + 2 * pair + j for j in range(2)]
            t = [_ssd_head_terms(heads[j], a_vec, dt_v, dtt_v, dsk_ref[...]) for j in range(2)]
            sel = lambda f: jnp.where(hm, f(t[0]), f(t[1]))
            hsum = lambda v, j: jnp.sum(jnp.where(hm if j == 0 else jnp.logical_not(hm), v, 0.0), axis=1, keepdims=True)
            dtp = sel(lambda q: q[1])
            Ep = sel(lambda q: jnp.exp(q[2]))
            Wp = sel(lambda q: jnp.exp(q[4] - q[2]))
            etot = sel(lambda q: jnp.exp(q[4]))
            dsk = sel(lambda q: q[5])
            X = xraw * dtp
            Hp = hin_ref[:, cols]
            dHn = dH[:, cols]
            dskip = jnp.sum(dyp * xraw, axis=0, keepdims=True)
            yoff = Ep * _dot(Cm, Hp)
            dE = dyp * yoff
            dC = dC + _dot_nt(dyp * Ep, Hp)
            dH[:, cols] = etot * dHn + _dot(Cm.T, dyp * Ep)
            BdS = _dot(Bm, dHn)
            dX = Wp * BdS
            ew = X * BdS * Wp
            dB = dB + _dot_nt(X * Wp, dHn)
            hh = jnp.sum(dHn * Hp, axis=0, keepdims=True) * etot
            for j in range(2):
                hmask = hm if j == 0 else jnp.logical_not(hm)
                cs_col, cs_row = t[j][2], t[j][3]
                Lm = jnp.where(r >= cidx, jnp.exp(cs_col - cs_row), 0.0)
                LmT = jnp.where(cidx >= r, jnp.exp(cs_row - cs_col), 0.0)
                dyh = jnp.where(hmask, dyp, 0.0)
                Xh = jnp.where(hmask, X, 0.0)
                dM = _dot_nt(dyh, Xh)
                dMT = _dot_nt(Xh, dyh)
                M = CB * Lm
                MT = CBT * LmT
                dX = dX + _dot(MT, dyh)
                dC = dC + _dot(dM * Lm, Bm)
                dB = dB + _dot(dMT * LmT, Cm)
                g_rows = jnp.sum(dM * M, axis=1, keepdims=True)
                g_cols = jnp.sum(dMT * MT, axis=1, keepdims=True)
                dtot = (jnp.sum(hsum(ew, j), axis=0, keepdims=True)
                        + jnp.sum(jnp.where(hmask, hh, 0.0), axis=1, keepdims=True))
                dcs = g_rows - g_cols + hsum(dE, j) - hsum(ew, j) + jnp.where(last, dtot, 0.0)
                dcs_all = dcs_all + jnp.where(lane == heads[j], dcs, 0.0)
                dtx_all = dtx_all + jnp.where(lane == heads[j], hsum(dX * xraw, j), 0.0)
                dd_row = dd_row + jnp.where(lane == heads[j],
                                            jnp.sum(jnp.where(hmask, dskip, 0.0), axis=1, keepdims=True), 0.0)
            dxs.append(dX * dtp + dyp * dsk)
        dxs_ref[...] = jnp.concatenate(dxs, axis=1)
        db_ref[...] = dB
        dc_ref[...] = dC
        dadt = _dot_exact(jnp.where(cidx >= r, 1.0, 0.0), dcs_all)
        ddt_ref[...] = dadt * a_vec + dtx_all
        pg_ref[1:2, 0:128] += dd_row
        pg_ref[2:3, 0:128] += jnp.sum(dadt * dt_v, axis=0, keepdims=True) * a_vec

    rowb = lambda w, colf: pl.BlockSpec((CH, w), lambda b, g, c: (b * nch + nch - 1 - c, colf(g)))
    vec = pl.BlockSpec((1, 128), lambda b, g, c: (0, 0))
    kw = dict(
        grid=(B, SSD_GROUPS, nch),
        in_specs=[rowb(256, lambda g: g), rowb(256, lambda g: g), rowb(256, lambda g: 16 + g), rowb(256, lambda g: g),
                  rowb(128, lambda g: 8 + g), rowb(128, lambda g: 12 + g), rowb(128, lambda g: 0),
                  pl.BlockSpec((128, CH), lambda b, g, c: (0, b * nch + nch - 1 - c)),
                  pl.BlockSpec((None, None, None, 128, 256), lambda b, g, c: (b, g, nch - 1 - c, 0, 0)),
                  vec, vec, pl.BlockSpec((1, 256), lambda b, g, c: (0, g))],
        out_specs=[rowb(256, lambda g: g), rowb(256, lambda g: g), rowb(128, lambda g: g), rowb(128, lambda g: g),
                   rowb(128, lambda g: g),
                   pl.BlockSpec((None, None, 8, 256), lambda b, g, c: (b, g, 0, 0))],
        out_shape=[jax.ShapeDtypeStruct((R, 1024), _MXU), jax.ShapeDtypeStruct((R, 1024), F32),
                   jax.ShapeDtypeStruct((R, 512), F32), jax.ShapeDtypeStruct((R, 512), F32),
                   jax.ShapeDtypeStruct((R, 512), F32), jax.ShapeDtypeStruct((B, SSD_GROUPS, 8, 256), F32)],
        scratch_shapes=[pltpu.VMEM((128, 256), F32)])
    return _call(body, "ssd_bwd", ("arbitrary", "arbitrary", "arbitrary"), kw,
                 (dycat, ypre, u0, act, act, act, dt, dtt, hin, a_log, d_skip, norm_g), rider)


def _ssd_prep_bwd(dxs, dB, dC, ddt4, u0, udt, conv_w, conv_b, dt_bias, B, nch, rider=None):
    R = u0.shape[0]

    def body(dxs_ref, db_ref, dc_ref, ddt_ref, xs_ref, xsp_ref, bc_ref, bcp_ref, udt_ref, w0_ref, w1_ref, b0_ref, b1_ref,
             dtb_ref, dpre_ref, ddtr_ref, pgd_ref):
        c = pl.program_id(1)

        @pl.when(c == 0)
        def _():
            pgd_ref[...] = jnp.zeros_like(pgd_ref)

        keep = _real_rows(c)
        p0 = _conv_pre(xsp_ref[...], xs_ref[...], w0_ref, b0_ref, 4)
        p1 = _conv_pre(bcp_ref[...], bc_ref[...], w1_ref, b1_ref, 4)
        dpre_ref[:, :1024] = jnp.where(keep, dxs_ref[...] * _dsilu(p0), 0.0)
        dpre_ref[:, 1024:] = jnp.where(keep, jnp.concatenate([db_ref[...], dc_ref[...]], axis=1) * _dsilu(p1), 0.0)
        ddt = ddt_ref[:, 0:128] + ddt_ref[:, 128:256] + ddt_ref[:, 256:384] + ddt_ref[:, 384:512]
        ok = jnp.logical_and(keep, _lane_ids(1, 128) < SSD_HEADS)
        dr = jnp.where(ok, ddt * _sigmoid(udt_ref[...] + dtb_ref[...]), 0.0)
        ddtr_ref[...] = dr
        pgd_ref[0:1, :] += jnp.sum(dr, axis=0, keepdims=True)

    rw = lambda w: pl.BlockSpec((CH, w), lambda b, c: (b * nch + c, 0))
    row = lambda col: pl.BlockSpec((CH, 1024), lambda b, c: (b * nch + c, col))
    prev = lambda col: pl.BlockSpec((8, 1024), _prev8_map(nch, col))
    kw = dict(
        grid=(B, nch),
        in_specs=[rw(1024), rw(512), rw(512), rw(512), row(5), prev(5), row(6), prev(6), rw(128),
                  pl.BlockSpec((4, 1024), lambda b, c: (0, 0)), pl.BlockSpec((4, 1024), lambda b, c: (0, 1)),
                  pl.BlockSpec((1, 1024), lambda b, c: (0, 0)), pl.BlockSpec((1, 1024), lambda b, c: (0, 1)),
                  pl.BlockSpec((1, 128), lambda b, c: (0, 0))],
        out_specs=[rw(2048), rw(128), pl.BlockSpec((None, 8, 128), lambda b, c: (b, 0, 0))],
        out_shape=[jax.ShapeDtypeStruct((R, 2048), F32), jax.ShapeDtypeStruct((R, 128), F32),
                   jax.ShapeDtypeStruct((B, 8, 128), F32)])
    return _call(body, "ssd_prep_bwd", ("arbitrary", "arbitrary"), kw,
                 (dxs, dB, dC, ddt4, u0, u0, u0, u0, udt, conv_w, conv_w, conv_b, conv_b, dt_bias), rider)


def _conv_bwd(dpre, xin, xin_col, w, K, name, tc=1024):
    R, C = dpre.shape
    assert C % tc == 0 and xin_col % tc == 0
    nr = R // CH
    xoff = xin_col // tc

    def body(dp_ref, dpn_ref, x_ref, xp_ref, w_ref, din_ref, dw_ref):
        i = pl.program_id(1)

        @pl.when(i == 0)
        def _():
            dw_ref[...] = jnp.zeros_like(dw_ref)

        dp = dp_ref[...]
        nxt = dpn_ref[...] * (i < nr - 1).astype(F32)
        x = x_ref[...]
        xp = xp_ref[...]
        din = dp * w_ref[K - 1:K, :]
        dw_ref[K - 1:K, :] += jnp.sum(dp * x, axis=0, keepdims=True)
        dw_ref[7:8, :] += jnp.sum(dp, axis=0, keepdims=True)
        for s in range(1, K):
            din = din + _shift_up(dp, nxt, s) * w_ref[K - 1 - s:K - s, :]
            dw_ref[K - 1 - s:K - s, :] += jnp.sum(dp * _shift_down(xp, x, s), axis=0, keepdims=True)
        din_ref[...] = din.astype(din_ref.dtype)

    return pl.pallas_call(
        body, name=name, grid=(C // tc, nr),
        in_specs=[pl.BlockSpec((CH, tc), lambda j, i: (i, j)),
                  pl.BlockSpec((8, tc), lambda j, i: (jnp.minimum((i + 1) * (CH // 8), nr * (CH // 8) - 1), j)),
                  pl.BlockSpec((CH, tc), lambda j, i: (i, xoff + j)),
                  pl.BlockSpec((8, tc), lambda j, i: (jnp.maximum(i * (CH // 8) - 1, 0), xoff + j)),
                  pl.BlockSpec((K, tc), lambda j, i: (0, j))],
        out_specs=[pl.BlockSpec((CH, tc), lambda j, i: (i, j)),
                   pl.BlockSpec((8, tc), lambda j, i: (0, j))],
        out_shape=[jax.ShapeDtypeStruct((R, C), _MXU), jax.ShapeDtypeStruct((8, C), F32)],
        compiler_params=_cparams(("parallel", "arbitrary")),
    )(dpre, dpre, xin, xin, w)


_RET_LG = [float(v) for v in np.log1p(-np.exp2(-5.0 - np.arange(RET_HEADS, dtype=np.float32))).astype(np.float32)]
_RET_SCALE = RET_DK ** -0.5


def _rope_tables(nch):
    half = RET_DK // 2
    inv_freq = 1.0 / (10000.0 ** (jnp.arange(half, dtype=F32) / (half - 1)))
    pos = jnp.arange(nch * CH, dtype=F32) - PAD
    ang = pos[:, None] * inv_freq[None, :]
    return jnp.cos(ang), jnp.sin(ang)


def _rot(x, cos, sin):
    x1, x2 = x[:, :128], x[:, 128:]
    return jnp.concatenate([x1 * cos - x2 * sin, x1 * sin + x2 * cos], axis=1)


def _unrot(d, cos, sin):
    d1, d2 = d[:, :128], d[:, 128:]
    return jnp.concatenate([d1 * cos + d2 * sin, d2 * cos - d1 * sin], axis=1)


def _ret_decays(lg):
    r = _row_ids(CH, CH)
    cidx = _lane_ids(CH, CH)
    diff = (r - cidx).astype(F32)
    decay = jnp.where(r >= cidx, jnp.exp(lg * jnp.maximum(diff, 0.0)), 0.0)
    decay_t = jnp.where(cidx >= r, jnp.exp(lg * jnp.maximum(-diff, 0.0)), 0.0)
    idx = _row_ids(CH).astype(F32)
    zeta = jnp.exp(lg * (CH - 1.0 - idx))
    xi = jnp.exp(lg * (idx + 1.0))
    return decay, decay_t, zeta, xi


def _ret_fwd(u0, ycat, cos, sin, norm_g, B, nch, rider=None):
    R = u0.shape[0]

    def body(u_ref, cos_ref, sin_ref, ng_ref, ycat_in, out_ref, opre_ref, rin_ref, Rst):
        c = pl.program_id(1)

        @pl.when(c == 0)
        def _():
            Rst[...] = jnp.zeros_like(Rst)

        cos_v, sin_v = cos_ref[...], sin_ref[...]
        for h in range(RET_HEADS):
            lg = _RET_LG[h]
            cols = slice(256 * h, 256 * h + 256)
            decay, _, zeta, xi = _ret_decays(lg)
            qr = _rot(u_ref[:, cols], cos_v, sin_v)
            kr = _rot(u_ref[:, 1024 + 256 * h:1024 + 256 * h + 256], cos_v, sin_v) * _RET_SCALE
            v = u_ref[:, 2048 + 256 * h:2048 + 256 * h + 256]
            gate = u_ref[:, 3072 + 256 * h:3072 + 256 * h + 256]
            Rh = Rst[h]
            rin_ref[h] = Rh
            inner = _dot(_dot_nt(qr, kr) * decay, v)
            cross = _dot(qr, Rh) * xi
            Rst[h] = math.exp(CH * lg) * Rh + _dot((kr * zeta).T, v)
            o = inner + cross
            opre_ref[:, cols] = o
            oc = o - jnp.mean(o, axis=-1, keepdims=True)
            rr = lax.rsqrt(jnp.mean(oc * oc, axis=-1, keepdims=True) + EPS)
            out_ref[:, cols] = (_silu(gate) * (oc * rr * ng_ref[:, cols])).astype(out_ref.dtype)

    kw = dict(
        grid=(B, nch),
        in_specs=[pl.BlockSpec((CH, 4096), lambda b, c: (b * nch + c, 0)),
                  pl.BlockSpec((CH, 128), lambda b, c: (c, 0)), pl.BlockSpec((CH, 128), lambda b, c: (c, 0)),
                  pl.BlockSpec((1, 1024), lambda b, c: (0, 0)),
                  pl.BlockSpec(memory_space=pl.ANY)],
        out_specs=[pl.BlockSpec((CH, 1024), lambda b, c: (b * nch + c, 1)),
                   pl.BlockSpec((CH, 1024), lambda b, c: (b * nch + c, 0)),
                   pl.BlockSpec((None, None, RET_HEADS, 256, 256), lambda b, c: (b, c, 0, 0, 0))],
        out_shape=[jax.ShapeDtypeStruct(ycat.shape, ycat.dtype), jax.ShapeDtypeStruct((R, 1024), F32),
                   jax.ShapeDtypeStruct((B, nch, RET_HEADS, 256, 256), F32)],
        scratch_shapes=[pltpu.VMEM((RET_HEADS, 256, 256), F32)],
        input_output_aliases={4: 0})
    return _call(body, "ret_fwd", ("arbitrary", "arbitrary"), kw, (u0, cos, sin, norm_g, ycat), rider)


def _ret_bwd(dycat, u0, opre, rin, cos, sin, norm_g, B, nch, rider=None):
    R = u0.shape[0]

    def body(dy_ref, u_ref, opre_ref, rin_ref, cos_ref, sin_ref, ng_ref, du_ref, pg_ref, dR):
        @pl.when(pl.program_id(1) == 0)
        def _():
            dR[...] = jnp.zeros_like(dR)
            pg_ref[...] = jnp.zeros_like(pg_ref)

        cos_v, sin_v = cos_ref[...], sin_ref[...]
        for h in range(RET_HEADS):
            lg = _RET_LG[h]
            cols = slice(256 * h, 256 * h + 256)
            decay, decay_t, zeta, xi = _ret_decays(lg)
            qr = _rot(u_ref[:, cols], cos_v, sin_v)
            kr = _rot(u_ref[:, 1024 + 256 * h:1024 + 256 * h + 256], cos_v, sin_v) * _RET_SCALE
            v = u_ref[:, 2048 + 256 * h:2048 + 256 * h + 256]
            gate = u_ref[:, 3072 + 256 * h:3072 + 256 * h + 256]
            ng = ng_ref[:, cols]
            o = opre_ref[:, cols]
            oc = o - jnp.mean(o, axis=-1, keepdims=True)
            rr = lax.rsqrt(jnp.mean(oc * oc, axis=-1, keepdims=True) + EPS)
            ohat = oc * rr
            dout = dy_ref[:, cols]
            du_ref[:, 3072 + 256 * h:3072 + 256 * h + 256] = (dout * (ohat * ng) * _dsilu(gate)).astype(du_ref.dtype)
            don = dout * _silu(gate)
            pg_ref[0:1, cols] += jnp.sum(don * ohat, axis=0, keepdims=True)
            dohat = don * ng
            do = rr * (dohat - jnp.mean(dohat, axis=-1, keepdims=True)
                       - ohat * jnp.mean(dohat * ohat, axis=-1, keepdims=True))
            Rh = rin_ref[h]
            dRn = dR[h]
            sc_t = _dot_nt(kr, qr) * decay_t
            dv = _dot(sc_t, do) + _dot(kr * zeta, dRn)
            ds = _dot_nt(do, v) * decay
            ds_t = _dot_nt(v, do) * decay_t
            dox = do * xi
            dq = _dot(ds, kr) + _dot_nt(dox, Rh)
            dk = _dot(ds_t, qr) + zeta * _dot_nt(v, dRn)
            dR[h] = math.exp(CH * lg) * dRn + _dot(qr.T, dox)
            du_ref[:, cols] = _unrot(dq, cos_v, sin_v).astype(du_ref.dtype)
            du_ref[:, 1024 + 256 * h:1024 + 256 * h + 256] = (_unrot(dk, cos_v, sin_v) * _RET_SCALE).astype(du_ref.dtype)
            du_ref[:, 2048 + 256 * h:2048 + 256 * h + 256] = dv.astype(du_ref.dtype)

    rmap = lambda b, c: (b * nch + nch - 1 - c, 0)
    kw = dict(
        grid=(B, nch),
        in_specs=[pl.BlockSpec((CH, 1024), lambda b, c: (b * nch + nch - 1 - c, 1)),
                  pl.BlockSpec((CH, 4096), rmap), pl.BlockSpec((CH, 1024), rmap),
                  pl.BlockSpec((None, None, RET_HEADS, 256, 256), lambda b, c: (b, nch - 1 - c, 0, 0, 0)),
                  pl.BlockSpec((CH, 128), lambda b, c: (nch - 1 - c, 0)),
                  pl.BlockSpec((CH, 128), lambda b, c: (nch - 1 - c, 0)),
                  pl.BlockSpec((1, 1024), lambda b, c: (0, 0))],
        out_specs=[pl.BlockSpec((CH, 4096), rmap), pl.BlockSpec((None, 8, 1024), lambda b, c: (b, 0, 0))],
        out_shape=[jax.ShapeDtypeStruct((R, 4096), _MXU), jax.ShapeDtypeStruct((B, 8, 1024), F32)],
        scratch_shapes=[pltpu.VMEM((RET_HEADS, 256, 256), F32)])
    return _call(body, "ret_bwd", ("arbitrary", "arbitrary"), kw, (dycat, u0, opre, rin, cos, sin, norm_g), rider)


_SB_SCALE = SB_HD ** -0.5


_SB_NB = 3


def _sb_valid(qb, kb, live):
    qpos = qb * CH + jnp.bitwise_and(_row_ids(2 * CH, CH), CH - 1)
    kpos = kb * CH + _lane_ids(2 * CH, CH)
    first = PAD + (1 - live) * (1 << 24)
    return jnp.logical_and(kpos < qpos, kpos >= first)


_SB_DEAD = -100.0
_SB_OFF = -1e30


def _sb_alive(acc):
    return (jnp.max(acc) > _SB_DEAD).astype(jnp.int32)


def _sb_softplus(z):
    return jnp.maximum(z, 0.0) + jnp.log(1.0 + jnp.exp(-jnp.abs(z)))


def _stack_heads(x):
    hm = _lane_ids(1, 128) < SB_HD
    return jnp.concatenate([jnp.where(hm, x, 0.0), jnp.where(hm, 0.0, x)], axis=0)


def _unstack_heads(x2):
    return jnp.where(_lane_ids(1, 128) < SB_HD, x2[:CH], x2[CH:])


def _sb_fwd(u1, B, nch, rider=None):
    R = u1.shape[0]
    Pn = nch * CH

    def body(q_ref, k_ref, v_ref, out_ref):
        qb = pl.program_id(2)
        q2 = _stack_heads(q_ref[...] * _SB_SCALE).astype(_MXU)
        mgt = (_row_ids(CH, CH) > _lane_ids(CH, CH)).astype(F32)

        def step(i, carry):
            out2, acc = carry
            blocks = []
            for t in range(_SB_NB):
                kb = qb - _SB_NB * i - t
                live = (kb >= 0).astype(jnp.int32)
                kbc = jnp.maximum(kb, 0)
                start = pl.multiple_of(kbc * CH, CH)
                valid = _sb_valid(qb, kbc, live)
                z = _dot_nt(q2, k_ref[pl.ds(start, CH), :])
                sp = _sb_softplus(z)
                lm = jnp.where(valid, -sp, 0.0)
                blocks.append((valid, z - sp, _dot_split(lm, mgt), jnp.sum(lm, axis=1, keepdims=True), start))
            for valid, ls, loc, rs, start in blocks:
                w = jnp.where(valid, jnp.exp(ls + loc + acc), 0.0)
                out2 = out2 + _dot(w, v_ref[pl.ds(start, CH), :])
                acc = acc + rs
            return out2, acc

        trips = (qb + _SB_NB) // _SB_NB

        def more(c):
            return jnp.logical_and(c[0] < trips, c[1] > 0)

        def trip(c):
            out2, acc = step(c[0], c[2:])
            return c[0] + 1, _sb_alive(acc), out2, acc

        init = (jnp.int32(0), jnp.int32(1), jnp.zeros((2 * CH, 128), F32), jnp.zeros((2 * CH, 1), F32))
        out2 = lax.while_loop(more, trip, init)[2]
        out_ref[...] = _unstack_heads(out2).astype(out_ref.dtype)

    qspec = lambda off: pl.BlockSpec((CH, 128), lambda b, hp, qb: (b * nch + qb, off + hp))
    kspec = lambda off: pl.BlockSpec((Pn, 128), lambda b, hp, qb: (b, off + hp))
    kw = dict(grid=(B, SB_HEADS // 2, nch), in_specs=[qspec(0), kspec(8), kspec(16)], out_specs=[qspec(0)],
              out_shape=[jax.ShapeDtypeStruct((R, 2048), _MXU)])
    return _call(body, "sb_fwd", ("arbitrary", "arbitrary", "arbitrary"), kw, (u1, u1, u1), rider)


def _sb_bwd(dycat, u1, B, nch, rider=None):
    R = u1.shape[0]
    Pn = nch * CH

    def body(q_ref, k_ref, v_ref, do_ref, dq_ref, dk_ref, dv_ref, lm_scr, ls_scr):
        qb = pl.program_id(2)

        @pl.when(qb == 0)
        def _():
            dk_ref[...] = jnp.zeros_like(dk_ref)
            dv_ref[...] = jnp.zeros_like(dv_ref)

        q2 = _stack_heads(q_ref[...] * _SB_SCALE)
        do2 = _stack_heads(do_ref[...])
        q2t, do2t = q2.T.astype(_MXU), do2.T.astype(_MXU)
        q2, do2 = q2.astype(_MXU), do2.astype(_MXU)
        rr = _row_ids(CH, CH)
        cc = _lane_ids(CH, CH)
        mle = (rr <= cc).astype(F32)
        mlt = (rr < cc).astype(F32)
        trips = (qb + _SB_NB) // _SB_NB

        def more(c):
            return jnp.logical_and(c[0] < trips, c[1] > 0)

        def scan(c):
            acc = c[2]
            for t in range(_SB_NB):
                kb = qb - _SB_NB * c[0] - t
                kbc = jnp.maximum(kb, 0)
                valid = _sb_valid(qb, kbc, (kb >= 0).astype(jnp.int32))
                z = _dot_nt(q2, k_ref[pl.ds(pl.multiple_of(kbc * CH, CH), CH), :])
                sp = _sb_softplus(z)
                lm = jnp.where(valid, -sp, 0.0)
                lm_scr[c[0] * _SB_NB + t] = lm
                ls_scr[c[0] * _SB_NB + t] = jnp.where(valid, z - sp, _SB_OFF)
                acc = acc + jnp.sum(lm, axis=1, keepdims=True)
            return c[0] + 1, _sb_alive(acc), acc

        used, _, s2 = lax.while_loop(more, scan, (jnp.int32(0), jnp.int32(1), jnp.zeros((2 * CH, 1), F32)))
        base = qb + 1 - _SB_NB * used

        def step(i, carry):
            dq2, pacc, gacc = carry
            blocks = []
            for t in range(_SB_NB):
                kb = base + _SB_NB * i + t
                start = pl.multiple_of(jnp.maximum(kb, 0) * CH, CH)
                slot = (used - 1 - i) * _SB_NB + (_SB_NB - 1 - t)
                lm = lm_scr[slot]
                blocks.append((ls_scr[slot], _dot_split(lm, mle), jnp.sum(lm, axis=1, keepdims=True), start))
            stage = []
            for ls, ploc, rs, start in blocks:
                w = jnp.exp(ls + (s2 - (ploc + pacc)))
                gg = _dot_nt(do2, v_ref[pl.ds(start, CH), :]) * w
                stage.append((ls, w, gg, _dot_split(gg, mlt), jnp.sum(gg, axis=1, keepdims=True), start))
                pacc = pacc + rs
            for ls, w, gg, gloc, gs, start in stage:
                sig = jnp.exp(ls)
                dz = gg * (1.0 - sig) - (gloc + gacc) * sig
                dq2 = dq2 + _dot(dz, k_ref[pl.ds(start, CH), :])
                dk_ref[:, pl.ds(start, CH)] += _dot(q2t, dz)
                dv_ref[:, pl.ds(start, CH)] += _dot(do2t, w)
                gacc = gacc + gs
            return dq2, pacc, gacc

        zero = jnp.zeros((2 * CH, 1), F32)
        dq2 = lax.fori_loop(0, used, step, (jnp.zeros((2 * CH, 128), F32), zero, zero))[0]
        dq_ref[...] = (_unstack_heads(dq2) * _SB_SCALE).astype(dq_ref.dtype)

    qspec = lambda off: pl.BlockSpec((CH, 128), lambda b, hp, qb: (b * nch + qb, off + hp))
    kspec = lambda off: pl.BlockSpec((Pn, 128), lambda b, hp, qb: (b, off + hp))
    tspec = pl.BlockSpec((128, Pn), lambda b, hp, qb: (hp, b))
    full = jax.ShapeDtypeStruct((1024, R), F32)
    slots = (nch - 1 + _SB_NB) // _SB_NB * _SB_NB
    kw = dict(grid=(B, SB_HEADS // 2, nch), in_specs=[qspec(0), kspec(8), kspec(16), qspec(0)],
              out_specs=[qspec(0), tspec, tspec], out_shape=[jax.ShapeDtypeStruct((R, 1024), _MXU), full, full],
              scratch_shapes=[pltpu.VMEM((slots, 2 * CH, CH), F32)] * 2)
    return _call(body, "sb_bwd", ("arbitrary", "arbitrary", "arbitrary"), kw, (u1, u1, u1, dycat), rider)


def _neg_expm1(x):
    series = -(x * (1.0 + x * (0.5 + x * (1.0 / 6.0 + x * (1.0 / 24.0)))))
    return jnp.where(x > -0.05, series, 1.0 - jnp.exp(x))


def _lru_gates(x, wa_ref, ba_ref, wx_ref, bx_ref, lam_ref):
    rs, is_ = [], []
    for n in range(LRU_BLOCKS):
        xb = x[:, 128 * n:128 * n + 128]
        rs.append(_dot(xb, wa_ref[n]))
        is_.append(_dot(xb, wx_ref[n]))
    r = _sigmoid(jnp.concatenate(rs, axis=1) + ba_ref[...])
    i = _sigmoid(jnp.concatenate(is_, axis=1) + bx_ref[...])
    sp = _softplus(-lam_ref[...])
    la = -LRU_C * r * sp
    a = jnp.exp(la)
    mult = jnp.sqrt(jnp.maximum(_neg_expm1(2.0 * la), 0.0))
    return r, i, sp, a, mult


def _lru_fwd(u1, ycat, conv_w, conv_b, wa, ba, wx, bx, lam, B, nch):
    R = u1.shape[0]

    def body(x_ref, xp_ref, gate_ref, cw_ref, cb_ref, wa_ref, ba_ref, wx_ref, bx_ref, lam_ref, ycat_in,
             out_ref, hs_ref, hc):
        c = pl.program_id(1)

        @pl.when(c == 0)
        def _():
            hc[...] = jnp.zeros_like(hc)

        x = _conv_pre(xp_ref[...], x_ref[...], cw_ref, cb_ref, 4)
        r, i, sp, a, mult = _lru_gates(x, wa_ref, ba_ref, wx_ref, bx_ref, lam_ref)
        b = jnp.where(_real_rows(c), mult * (i * x), 0.0)
        rows = _row_ids(CH)
        s = 1
        while s < CH:
            a_s = jnp.where(rows >= s, pltpu.roll(a, s, axis=0), 1.0)
            b_s = jnp.where(rows >= s, pltpu.roll(b, s, axis=0), 0.0)
            b = a * b_s + b
            a = a * a_s
            s *= 2
        h = a * hc[0:1, :] + b
        hs_ref[...] = h
        hc[0:1, :] = hs_ref[CH - 1:CH, :]
        out_ref[...] = (h * _gelu(gate_ref[...])).astype(out_ref.dtype)

    row = lambda col: pl.BlockSpec((CH, 1024), lambda b, c: (b * nch + c, col))
    vec = pl.BlockSpec((1, 1024), lambda b, c: (0, 0))
    wsp = pl.BlockSpec((LRU_BLOCKS, 128, 128), lambda b, c: (0, 0, 0))
    return pl.pallas_call(
        body, name="lru_fwd", grid=(B, nch),
        in_specs=[row(4), pl.BlockSpec((8, 1024), _prev8_map(nch, 4)), row(3),
                  pl.BlockSpec((4, 1024), lambda b, c: (0, 0)), vec, wsp, vec, wsp, vec, vec,
                  pl.BlockSpec(memory_space=pl.ANY)],
        out_specs=[row(1), row(0)],
        out_shape=[jax.ShapeDtypeStruct(ycat.shape, ycat.dtype), jax.ShapeDtypeStruct((R, 1024), F32)],
        scratch_shapes=[pltpu.VMEM((8, 1024), F32)],
        input_output_aliases={10: 0},
        compiler_params=_cparams(("parallel", "arbitrary")),
    )(u1, u1, u1, conv_w, conv_b, wa, ba, wx, bx, lam, ycat)


def _lru_bwd(dycat, u1, hs, conv_w, conv_b, wa, ba, wx, bx, lam, B, nch):
    R = u1.shape[0]

    def body(dy_ref, x_ref, xp_ref, gate_ref, hs_ref, hsp_ref, cw_ref, cb_ref, wa_ref, ba_ref, wx_ref, bx_ref, lam_ref,
             dgate_ref, dxc_ref, pg_ref, dwa_ref, dwx_ref, lc):
        c = nch - 1 - pl.program_id(1)

        @pl.when(pl.program_id(1) == 0)
        def _():
            lc[...] = jnp.zeros_like(lc)
            pg_ref[...] = jnp.zeros_like(pg_ref)
            dwa_ref[...] = jnp.zeros_like(dwa_ref)
            dwx_ref[...] = jnp.zeros_like(dwx_ref)

        x = _conv_pre(xp_ref[...], x_ref[...], cw_ref, cb_ref, 4)
        r, i, sp, a, mult = _lru_gates(x, wa_ref, ba_ref, wx_ref, bx_ref, lam_ref)
        h = hs_ref[...]
        hprev = _shift_down(hsp_ref[...], h, 1)
        gate = gate_ref[...]
        dy = dy_ref[...]
        dgate_ref[...] = (dy * h * _dgelu(gate)).astype(dgate_ref.dtype)
        rows = _row_ids(CH)
        lam_t = dy * _gelu(gate) + jnp.where(rows == CH - 1, lc[0:1, :], 0.0)
        coef = jnp.where(rows < CH - 1, pltpu.roll(a, CH - 1, axis=0), 0.0)
        s = 1
        while s < CH:
            c_s = jnp.where(rows < CH - s, pltpu.roll(coef, CH - s, axis=0), 1.0)
            l_s = jnp.where(rows < CH - s, pltpu.roll(lam_t, CH - s, axis=0), 0.0)
            lam_t = coef * l_s + lam_t
            coef = coef * c_s
            s *= 2
        lc[0:1, :] = jnp.sum(jnp.where(rows == 0, a * lam_t, 0.0), axis=0, keepdims=True)
        db = jnp.where(_real_rows(c), lam_t, 0.0)
        da = db * hprev
        dmult = db * (i * x)
        di = db * mult * x
        dx = db * mult * i
        pos = mult > 0.0
        dla = da * a + jnp.where(pos, -dmult * (a * a) / jnp.where(pos, mult, 1.0), 0.0)
        dr = dla * (-LRU_C * sp)
        pg_ref[2:3, :] += jnp.sum(dla * (LRU_C * r) * _sigmoid(-lam_ref[...]), axis=0, keepdims=True)
        dpr = dr * r * (1.0 - r)
        dpi = di * i * (1.0 - i)
        pg_ref[0:1, :] += jnp.sum(dpr, axis=0, keepdims=True)
        pg_ref[1:2, :] += jnp.sum(dpi, axis=0, keepdims=True)
        dxs = []
        for n in range(LRU_BLOCKS):
            blk = slice(128 * n, 128 * n + 128)
            dxs.append(dx[:, blk] + _dot_nt(dpr[:, blk], wa_ref[n]) + _dot_nt(dpi[:, blk], wx_ref[n]))
            dwa_ref[n] += _dot_tn(x[:, blk], dpr[:, blk])
            dwx_ref[n] += _dot_tn(x[:, blk], dpi[:, blk])
        dxc_ref[...] = jnp.concatenate(dxs, axis=1)

    rmap = lambda col: (lambda b, c: (b * nch + nch - 1 - c, col))
    row = lambda col: pl.BlockSpec((CH, 1024), rmap(col))
    prev = lambda col: pl.BlockSpec(
        (8, 1024), lambda b, c: (jnp.maximum((b * nch + nch - 1 - c) * (CH // 8) - 1, 0), col))
    vec = pl.BlockSpec((1, 1024), lambda b, c: (0, 0))
    wsp = pl.BlockSpec((LRU_BLOCKS, 128, 128), lambda b, c: (0, 0, 0))
    full = jax.ShapeDtypeStruct((R, 1024), F32)
    return pl.pallas_call(
        body, name="lru_bwd", grid=(B, nch),
        in_specs=[row(1), row(4), prev(4), row(3), row(0), prev(0),
                  pl.BlockSpec((4, 1024), lambda b, c: (0, 0)), vec, wsp, vec, wsp, vec, vec],
        out_specs=[row(0), row(0), pl.BlockSpec((None, 8, 1024), lambda b, c: (b, 0, 0)),
                   pl.BlockSpec((None, LRU_BLOCKS, 128, 128), lambda b, c: (b, 0, 0, 0)),
                   pl.BlockSpec((None, LRU_BLOCKS, 128, 128), lambda b, c: (b, 0, 0, 0))],
        out_shape=[jax.ShapeDtypeStruct((R, 1024), _MXU), full, jax.ShapeDtypeStruct((B, 8, 1024), F32),
                   jax.ShapeDtypeStruct((B, LRU_BLOCKS, 128, 128), F32),
                   jax.ShapeDtypeStruct((B, LRU_BLOCKS, 128, 128), F32)],
        scratch_shapes=[pltpu.VMEM((8, 1024), F32)],
        compiler_params=_cparams(("parallel", "arbitrary")),
    )(dycat, u1, u1, u1, hs, hs, conv_w, conv_b, wa, ba, wx, bx, lam)


_FFN_TC = FFN


def _ffn_specs(nch):
    nt = FFN // _FFN_TC
    row = lambda off: pl.BlockSpec((CH, _FFN_TC), lambda b, c, j: (b * nch + c, off + j))
    prev = lambda off: pl.BlockSpec(
        (8, _FFN_TC), lambda b, c, j: (jnp.maximum((b * nch + c) * (CH // 8) - 1, 0), off + j))
    wsp = lambda off: pl.BlockSpec((3, _FFN_TC), lambda b, c, j: (0, off + j))
    bsp = lambda off: pl.BlockSpec((1, _FFN_TC), lambda b, c, j: (0, off + j))
    return nt, row, [row(0), prev(0), row(nt), prev(nt), wsp(0), wsp(nt), bsp(0), bsp(nt)]


def _ffn_act_fwd(uf, conv_w, conv_b, B, nch, rider=None):
    R = uf.shape[0]
    nt, row, specs = _ffn_specs(nch)

    def body(g_ref, gp_ref, u_ref, up_ref, wg_ref, wu_ref, bg_ref, bu_ref, o_ref):
        cg = _conv_pre(gp_ref[...], g_ref[...], wg_ref, bg_ref, 3)
        cu = _conv_pre(up_ref[...], u_ref[...], wu_ref, bu_ref, 3)
        o_ref[...] = jnp.where(_real_rows(pl.program_id(1)), _silu(cg) * cu, 0.0).astype(o_ref.dtype)

    kw = dict(grid=(B, nch, nt), in_specs=specs, out_specs=[row(0)],
              out_shape=[jax.ShapeDtypeStruct((R, FFN), _MXU)])
    return _call(body, "ffn_act_fwd", ("arbitrary", "arbitrary", "arbitrary"), kw,
                 (uf, uf, uf, uf, conv_w, conv_w, conv_b, conv_b), rider)


def _ffn_act_bwd(da, uf, conv_w, conv_b, nch, name, rider=None):
    R = uf.shape[0]
    nt = FFN // _FFN_TC
    nr = R // CH
    K = 3

    def body(da_ref, dan_ref, g_ref, gp_ref, gn_ref, u_ref, up_ref, un_ref, wg_ref, wu_ref, bg_ref, bu_ref,
             dug_ref, duu_ref, dwg_ref, dwu_ref):
        i = pl.program_id(1)

        @pl.when(i == 0)
        def _():
            dwg_ref[...] = jnp.zeros_like(dwg_ref)
            dwu_ref[...] = jnp.zeros_like(dwu_ref)

        c = i % nch
        ext = CH + 8
        rows = _row_ids(ext)
        follows = (c < nch - 1).astype(jnp.int32)
        keep = jnp.logical_and(c * CH + rows >= PAD, rows < CH + 8 * follows)
        dav = jnp.where(keep, jnp.concatenate([da_ref[...], dan_ref[...]], axis=0), 0.0)

        def conv_ext(x_ref, xp_ref, xn_ref, w_ref, b_ref):
            cat = jnp.concatenate([xp_ref[...], x_ref[...], xn_ref[...]], axis=0)
            shifted = [cat[8:]] + [pltpu.roll(cat, s, axis=0)[8:] for s in range(1, K)]
            acc = shifted[0] * w_ref[K - 1:K, :] + b_ref[...]
            for s in range(1, K):
                acc = acc + shifted[s] * w_ref[K - 1 - s:K - s, :]
            return acc, shifted

        cg, gsh = conv_ext(g_ref, gp_ref, gn_ref, wg_ref, bg_ref)
        cu, ush = conv_ext(u_ref, up_ref, un_ref, wu_ref, bu_ref)
        sg = _sigmoid(cg)
        dcg = dav * cu * (sg * (1.0 + cg * (1.0 - sg)))
        dcu = dav * (cg * sg)
        for dc, xsh, w_ref, din_ref, dw_ref in ((dcg, gsh, wg_ref, dug_ref, dwg_ref), (dcu, ush, wu_ref, duu_ref, dwu_ref)):
            dp = dc[:CH]
            din = dp * w_ref[K - 1:K, :]
            dw_ref[7:8, :] += jnp.sum(dp, axis=0, keepdims=True)
            dw_ref[K - 1:K, :] += jnp.sum(dp * xsh[0][:CH], axis=0, keepdims=True)
            for s in range(1, K):
                din = din + pltpu.roll(dc, ext - s, axis=0)[:CH] * w_ref[K - 1 - s:K - s, :]
                dw_ref[K - 1 - s:K - s, :] += jnp.sum(dp * xsh[s][:CH], axis=0, keepdims=True)
            din_ref[...] = din.astype(din_ref.dtype)

    row = lambda off: pl.BlockSpec((CH, _FFN_TC), lambda j, i: (i, off + j))
    prev = lambda off: pl.BlockSpec((8, _FFN_TC), lambda j, i: (jnp.maximum(i * (CH // 8) - 1, 0), off + j))
    nxt = lambda off: pl.BlockSpec(
        (8, _FFN_TC), lambda j, i: (jnp.minimum((i + 1) * (CH // 8), nr * (CH // 8) - 1), off + j))
    wsp = lambda off: pl.BlockSpec((K, _FFN_TC), lambda j, i: (0, off + j))
    bsp = lambda off: pl.BlockSpec((1, _FFN_TC), lambda j, i: (0, off + j))
    acc = pl.BlockSpec((8, _FFN_TC), lambda j, i: (0, j))
    half = jax.ShapeDtypeStruct((R, FFN), _MXU)
    dwsh = jax.ShapeDtypeStruct((8, FFN), F32)
    kw = dict(
        grid=(nt, nr),
        in_specs=[row(0), nxt(0), row(0), prev(0), nxt(0), row(nt), prev(nt), nxt(nt), wsp(0), wsp(nt), bsp(0), bsp(nt)],
        out_specs=[row(0), row(0), acc, acc],
        out_shape=[half, half, dwsh, dwsh])
    return _call(body, name, ("arbitrary", "arbitrary"), kw,
                 (da, da, uf, uf, uf, uf, uf, uf, conv_w, conv_w, conv_b, conv_b), rider)


def _head(h, g, target, B, nch):
    R = h.shape[0]

    def body(h_ref, g_ref, t_ref, dh_ref, dhb_ref, loss_ref, dg_ref):
        c = pl.program_id(1)

        @pl.when(c == 0)
        def _():
            dh_ref[...] = jnp.zeros_like(dh_ref)
            dhb_ref[...] = jnp.zeros_like(dhb_ref)
            loss_ref[...] = jnp.zeros_like(loss_ref)
            dg_ref[...] = jnp.zeros_like(dg_ref)

        @pl.when(c > 0)
        def _():
            x = h_ref[...]
            gv = g_ref[...]
            r = lax.rsqrt(jnp.mean(x * x, axis=-1, keepdims=True) + EPS)
            xhat = x * r
            e = xhat * gv - t_ref[...]
            loss_ref[...] += 0.5 * jnp.sum(jnp.mean(e * e, axis=-1, keepdims=True), axis=0, keepdims=True)
            dy = e * (1.0 / D)
            dg_ref[0:1, :] += jnp.sum(dy * xhat, axis=0, keepdims=True)
            dx = dy * gv
            dh = r * (dx - xhat * jnp.mean(dx * xhat, axis=-1, keepdims=True))
            dh_ref[...] = dh
            dhb_ref[...] = dh.astype(dhb_ref.dtype)

    row = pl.BlockSpec((CH, D), lambda b, c: (b * nch + c, 0))
    return pl.pallas_call(
        body, name="head", grid=(B, nch),
        in_specs=[row, pl.BlockSpec((1, D), lambda b, c: (0, 0)),
                  pl.BlockSpec((CH, D), lambda b, c: (b * (nch - 1) + jnp.maximum(c - 1, 0), 0))],
        out_specs=[row, row, pl.BlockSpec((None, 8, 128), lambda b, c: (b, 0, 0)),
                   pl.BlockSpec((None, 8, D), lambda b, c: (b, 0, 0))],
        out_shape=[jax.ShapeDtypeStruct((R, D), F32), jax.ShapeDtypeStruct((R, D), _MXU),
                   jax.ShapeDtypeStruct((B, 8, 128), F32), jax.ShapeDtypeStruct((B, 8, D), F32)],
        compiler_params=_cparams(("parallel", "arbitrary")),
    )(h, g, target)


ADAM_LR = 0.001
ADAM_B1 = 0.9
ADAM_B2 = 0.999
ADAM_EPS = 1e-08
ADAM_WD = 0.01
ADAM_STEP = 10


def _adamw(w, g, m, v, name):
    Rr, C = w.shape
    tr = _tile(Rr, (256, 64))

    def body(w_ref, g_ref, m_ref, v_ref, d_ref, nm_ref, nv_ref):
        gv = g_ref[...]
        nm = ADAM_B1 * m_ref[...] + (1.0 - ADAM_B1) * gv
        nv = ADAM_B2 * v_ref[...] + (1.0 - ADAM_B2) * (gv * gv)
        m_hat = nm / (1.0 - ADAM_B1 ** ADAM_STEP)
        v_hat = nv / (1.0 - ADAM_B2 ** ADAM_STEP)
        d_ref[...] = -ADAM_LR * (m_hat / (jnp.sqrt(v_hat) + ADAM_EPS) + ADAM_WD * w_ref[...])
        nm_ref[...] = nm
        nv_ref[...] = nv

    spec = pl.BlockSpec((tr, C), lambda i: (i, 0))
    sh = jax.ShapeDtypeStruct((Rr, C), F32)
    return pl.pallas_call(
        body, name=name, grid=(Rr // tr,),
        in_specs=[spec] * 4, out_specs=[spec] * 3, out_shape=[sh] * 3,
        compiler_params=_cparams(("parallel",)),
    )(w, g, m, v)


_MESH = pl.DeviceIdType.MESH
_ANY = pl.BlockSpec(memory_space=pl.ANY)


def _place():
    x, y, c = lax.axis_index("x"), lax.axis_index("y"), lax.axis_index("c")
    chips = [(1 - x, y), (x, 1 - y), (1 - x, 1 - y)]
    return x, y, c, chips


def _rcopy(src, dst, ssem, rsem, dev):
    return pltpu.make_async_remote_copy(src_ref=src, dst_ref=dst, send_sem=ssem, recv_sem=rsem,
                                        device_id=dev, device_id_type=_MESH)


def _with_riders(body, kw, kind, riders):
    n_in, n_out, n_scr = len(kw["in_specs"]), len(kw["out_specs"]), len(kw.get("scratch_shapes", []))
    grid = kw["grid"]
    nr = len(riders)
    nsem = 4 if kind == "gather" else 2

    def new_body(*refs):
        ins, srcs = refs[:n_in], refs[n_in:n_in + nr]
        outs, dsts = refs[n_in + nr:n_in + nr + n_out], refs[n_in + nr + n_out:n_in + 2 * nr + n_out]
        scr = refs[n_in + 2 * nr + n_out:n_in + 2 * nr + n_out + n_scr]
        sems = refs[n_in + 2 * nr + n_out + n_scr:]
        first = last = None
        for axis, size in enumerate(grid):
            i = pl.program_id(axis)
            first = (i == 0) if first is None else jnp.logical_and(first, i == 0)
            last = (i == size - 1) if last is None else jnp.logical_and(last, i == size - 1)
        x, y, c, chips = _place()
        k = 2 * x + y
        sib = (x, y, 1 - c)
        ssem, rsem = sems[:2]
        sends = []
        for a in range(nr):
            for j, (cx, cy) in enumerate(chips):
                if kind == "gather":
                    src, dst = srcs[a].at[c], dsts[a].at[k, c]
                else:
                    src, dst = srcs[a].at[2 * cx + cy], dsts[a].at[k]
                sends.append(_rcopy(src, dst, ssem.at[3 * a + j], rsem.at[3 * a + j], (cx, cy, c)))

        @pl.when(first)
        def _():
            for cp in sends:
                cp.start()

        body(*ins, *outs, *scr)

        @pl.when(last)
        def _():
            passed = []
            for a in range(nr):
                for j, (cx, cy) in enumerate(chips):
                    got = dsts[a].at[2 * cx + cy, c] if kind == "gather" else dsts[a].at[2 * cx + cy]
                    _rcopy(got, got, ssem.at[3 * a + j], rsem.at[3 * a + j], (cx, cy, c)).wait_recv()
                    if kind == "gather":
                        fw = _rcopy(got, got, sems[2].at[3 * a + j], sems[3].at[3 * a + j], sib)
                        fw.start()
                        passed.append(fw)
            if kind == "gather":
                for a in range(nr):
                    for j, (cx, cy) in enumerate(chips):
                        got = dsts[a].at[2 * cx + cy, 1 - c]
                        _rcopy(got, got, sems[2].at[3 * a + j], sems[3].at[3 * a + j], sib).wait_recv()
            for cp in sends + passed:
                cp.wait_send()

    kw = dict(kw)
    kw["in_specs"] = list(kw["in_specs"]) + [_ANY] * nr
    kw["out_specs"] = list(kw["out_specs"]) + [_ANY] * nr
    kw["out_shape"] = list(kw["out_shape"]) + [
        jax.ShapeDtypeStruct(((4,) + r.shape) if kind == "gather" else r.shape, r.dtype) for r in riders]
    kw["scratch_shapes"] = list(kw.get("scratch_shapes", [])) + [pltpu.SemaphoreType.DMA((3 * nr,))] * nsem
    return new_body, kw


def _call(body, name, sem, kw, args, rider=None):
    if rider is not None:
        body, kw = _with_riders(body, kw, *rider)
        args = tuple(args) + tuple(rider[1])
    return pl.pallas_call(body, name=name, compiler_params=_cparams(sem), **kw)(*args)


def _fill_own(result, own, chip):
    return lax.dynamic_update_index_in_dim(result, own, chip, 0)


def _gather_shards(bigs, small):
    nb = len(bigs)

    def body(*refs):
        ins, outs = refs[:nb + 1], refs[nb + 1:2 * nb + 2]
        ssem, rsem, fssem, frsem = refs[2 * nb + 2:]
        x, y, c, chips = _place()
        k = 2 * x + y
        sib = (x, y, 1 - c)

        def part(a, slot, hc):
            return outs[a].at[slot] if a == nb else outs[a].at[slot, hc]

        first = []
        for a in range(nb + 1):
            src = ins[a] if a == nb else ins[a].at[c]
            for j, (cx, cy) in enumerate(chips):
                first.append(_rcopy(src, part(a, k, c), ssem.at[3 * a + j], rsem.at[3 * a + j], (cx, cy, c)))
        for cp in first:
            cp.start()
        passed = []
        for a in range(nb + 1):
            for j, (cx, cy) in enumerate(chips):
                got = part(a, 2 * cx + cy, c)
                _rcopy(got, got, ssem.at[3 * a + j], rsem.at[3 * a + j], (cx, cy, c)).wait_recv()
                if a < nb:
                    fw = _rcopy(got, got, fssem.at[3 * a + j], frsem.at[3 * a + j], sib)
                    fw.start()
                    passed.append(fw)
        for a in range(nb):
            for j, (cx, cy) in enumerate(chips):
                got = part(a, 2 * cx + cy, 1 - c)
                _rcopy(got, got, fssem.at[3 * a + j], frsem.at[3 * a + j], sib).wait_recv()
        for cp in first + passed:
            cp.wait_send()

    arrs = list(bigs) + [small]
    n = 3 * (nb + 1)
    return pl.pallas_call(
        body, name="gather_shards",
        in_specs=[_ANY] * (nb + 1), out_specs=[_ANY] * (nb + 1),
        out_shape=[jax.ShapeDtypeStruct((4,) + a.shape, a.dtype) for a in arrs],
        scratch_shapes=[pltpu.SemaphoreType.DMA((n,)), pltpu.SemaphoreType.DMA((n,)),
                        pltpu.SemaphoreType.DMA((n,)), pltpu.SemaphoreType.DMA((n,))],
    )(*arrs)


def _swap_halves(grads, name):
    na = len(grads)
    halves = [g.shape[1] // 2 for g in grads]

    def body(*refs):
        ins, outs = refs[:na], refs[na:2 * na]
        ssem, rsem = refs[2 * na:]
        x, y, c, _ = _place()
        sib = (x, y, 1 - c)
        cps = [_rcopy(ins[a].at[:, pl.ds((1 - c) * halves[a], halves[a]), :], outs[a], ssem.at[a], rsem.at[a], sib)
               for a in range(na)]
        for cp in cps:
            cp.start()
        for cp in cps:
            cp.wait()

    return pl.pallas_call(
        body, name=name,
        in_specs=[_ANY] * na, out_specs=[_ANY] * na,
        out_shape=[jax.ShapeDtypeStruct((4, g.shape[1] // 2, g.shape[2]), g.dtype) for g in grads],
        scratch_shapes=[pltpu.SemaphoreType.DMA((na,)), pltpu.SemaphoreType.DMA((na,))],
    )(*grads)


def _sum_rows(rh):
    return rh if rh <= 512 else _tile(rh, (512, 256, 128, 64, 32))


def _chip_sum(grad, recv, core, name):
    _, r, cdim = grad.shape
    rh = r // 2
    tr = _sum_rows(rh)
    nblk = rh // tr

    def body(core_ref, g_ref, r_ref, o_ref):
        o_ref[...] = (g_ref[...] + r_ref[...]).astype(o_ref.dtype)

    return pl.pallas_call(
        body, name=name,
        grid_spec=pltpu.PrefetchScalarGridSpec(
            num_scalar_prefetch=1, grid=(4, nblk),
            in_specs=[pl.BlockSpec((None, tr, cdim), lambda s, i, cr: (s, cr[0] * nblk + i, 0)),
                      pl.BlockSpec((None, tr, cdim), lambda s, i, cr: (s, i, 0))],
            out_specs=pl.BlockSpec((None, tr, cdim), lambda s, i, cr: (s, i, 0))),
        out_shape=jax.ShapeDtypeStruct((4, rh, cdim), BF16),
        compiler_params=_cparams(("parallel", "parallel")),
    )(core, grad, recv)


def _scatter_sums(sums):
    na = len(sums)

    def body(*refs):
        ins, outs = refs[:na], refs[na:2 * na]
        ssem, rsem, lsem = refs[2 * na:]
        x, y, c, chips = _place()
        k = 2 * x + y
        local = [pltpu.make_async_copy(ins[a].at[k], outs[a].at[k], lsem.at[a]) for a in range(na)]
        for cp in local:
            cp.start()
        cps = []
        for a in range(na):
            for j, (cx, cy) in enumerate(chips):
                cps.append(_rcopy(ins[a].at[2 * cx + cy], outs[a].at[k], ssem.at[3 * a + j], rsem.at[3 * a + j],
                                  (cx, cy, c)))
        for cp in cps:
            cp.start()
        for a in range(na):
            for j, (cx, cy) in enumerate(chips):
                got = outs[a].at[2 * cx + cy]
                _rcopy(got, got, ssem.at[3 * a + j], rsem.at[3 * a + j], (cx, cy, c)).wait_recv()
        for cp in cps:
            cp.wait_send()
        for cp in local:
            cp.wait()

    return pl.pallas_call(
        body, name="scatter_sums",
        in_specs=[_ANY] * na, out_specs=[_ANY] * na,
        out_shape=[jax.ShapeDtypeStruct(s.shape, s.dtype) for s in sums],
        scratch_shapes=[pltpu.SemaphoreType.DMA((3 * na,)), pltpu.SemaphoreType.DMA((3 * na,)),
                        pltpu.SemaphoreType.DMA((na,))],
    )(*sums)


def _sum_chips(parts, name):
    _, rh, cdim = parts.shape
    tr = _sum_rows(rh)

    def body(p_ref, o_ref):
        acc = p_ref[0].astype(F32)
        for j in range(1, 4):
            acc = acc + p_ref[j].astype(F32)
        o_ref[...] = acc

    return pl.pallas_call(
        body, name=name, grid=(rh // tr,),
        in_specs=[pl.BlockSpec((4, tr, cdim), lambda i: (0, i, 0))],
        out_specs=pl.BlockSpec((tr, cdim), lambda i: (i, 0)),
        out_shape=jax.ShapeDtypeStruct((rh, cdim), F32),
        compiler_params=_cparams(("parallel",)),
    )(parts)


def _join_halves(reds):
    na = len(reds)

    def body(*refs):
        ins, outs = refs[:na], refs[na:2 * na]
        ssem, rsem = refs[2 * na:]
        x, y, c, _ = _place()
        cps = [_rcopy(ins[a], outs[a], ssem.at[a], rsem.at[a], (x, y, 1 - c)) for a in range(na)]
        for cp in cps:
            cp.start()
        for cp in cps:
            cp.wait()

    return pl.pallas_call(
        body, name="join_halves",
        in_specs=[_ANY] * na, out_specs=[_ANY] * na,
        out_shape=[jax.ShapeDtypeStruct(r.shape, r.dtype) for r in reds],
        scratch_shapes=[pltpu.SemaphoreType.DMA((na,)), pltpu.SemaphoreType.DMA((na,))],
    )(*reds)


def _allreduce_small(buf):
    n = buf.shape[0]

    def body(in_ref, out_ref, recv, ssem, rsem):
        x, y, c, _ = _place()
        peers = [(x, y, 1 - c), (1 - x, y, c), (x, 1 - y, c)]
        out_ref[...] = in_ref[...]
        for r, peer in enumerate(peers):
            cp = _rcopy(out_ref, recv.at[r], ssem.at[r], rsem.at[r], peer)
            cp.start()
            cp.wait()
            out_ref[...] = out_ref[...] + recv[r]

    vm = pl.BlockSpec(memory_space=pltpu.VMEM)
    return pl.pallas_call(
        body, name="allreduce_small",
        in_specs=[vm], out_specs=vm,
        out_shape=jax.ShapeDtypeStruct(buf.shape, F32),
        scratch_shapes=[pltpu.VMEM((3, n, 128), F32), pltpu.SemaphoreType.DMA((3,)), pltpu.SemaphoreType.DMA((3,))],
        compiler_params=pltpu.CompilerParams(vmem_limit_bytes=VMEM_LIMIT),
    )(buf)


_W_NAMES = ['meta_tokens', 'l0_mix_norm', 'l0_w_in', 'l0_ssd_conv_w', 'l0_ssd_conv_b', 'l0_ssd_dt_bias', 'l0_ssd_a_log',
            'l0_ssd_d', 'l0_ssd_norm', 'l0_ret_norm', 'l0_w_out', 'l0_ffn_norm', 'l0_ffn_w_in', 'l0_ffn_conv_w',
            'l0_ffn_conv_b', 'l0_ffn_w_out', 'l1_mix_norm', 'l1_w_in', 'l1_lru_conv_w', 'l1_lru_conv_b', 'l1_lru_wa',
            'l1_lru_ba', 'l1_lru_wx', 'l1_lru_bx', 'l1_lru_lambda', 'l1_w_out', 'l1_ffn_norm', 'l1_ffn_w_in',
            'l1_ffn_conv_w', 'l1_ffn_conv_b', 'l1_ffn_w_out', 'final_norm']
_IN_NAMES = ['x'] + _W_NAMES + ['loss_target'] + ['m_' + n for n in _W_NAMES] + ['v_' + n for n in _W_NAMES]
_BIG = ['l0_w_in', 'l0_w_out', 'l0_ffn_w_in', 'l0_ffn_w_out', 'l1_w_in', 'l1_w_out', 'l1_ffn_w_in', 'l1_ffn_w_out']
_BIG_COLS = ('l0_w_in', 'l0_ffn_w_in', 'l1_w_in', 'l1_ffn_w_in')
_SMALL_SHARDED = ['meta_tokens', 'l0_ssd_conv_w', 'l0_ffn_conv_w', 'l1_lru_conv_w', 'l1_ffn_conv_w']
_SMALL = [n for n in _W_NAMES if n not in _BIG]


def _pack(arrs):
    flat = []
    for a in arrs:
        v = a.reshape(-1).astype(F32)
        flat.append(jnp.pad(v, (0, (-v.shape[0]) % 128)))
    v = jnp.concatenate(flat)
    v = jnp.pad(v, (0, (-v.shape[0]) % 1024))
    return v.reshape(-1, 128)


def _unpack(buf, shapes):
    out, row = [], 0
    for sh in shapes:
        n = int(np.prod(sh))
        rows = -(-n // 128)
        out.append(buf[row:row + rows].reshape(-1)[:n].reshape(sh))
        row += rows
    return out


def kernel(x, meta_tokens, l0_mix_norm, l0_w_in, l0_ssd_conv_w, l0_ssd_conv_b, l0_ssd_dt_bias, l0_ssd_a_log, l0_ssd_d, l0_ssd_norm, l0_ret_norm, l0_w_out, l0_ffn_norm, l0_ffn_w_in, l0_ffn_conv_w, l0_ffn_conv_b, l0_ffn_w_out, l1_mix_norm, l1_w_in, l1_lru_conv_w, l1_lru_conv_b, l1_lru_wa, l1_lru_ba, l1_lru_wx, l1_lru_bx, l1_lru_lambda, l1_w_out, l1_ffn_norm, l1_ffn_w_in, l1_ffn_conv_w, l1_ffn_conv_b, l1_ffn_w_out, final_norm, loss_target, m_meta_tokens, m_l0_mix_norm, m_l0_w_in, m_l0_ssd_conv_w, m_l0_ssd_conv_b, m_l0_ssd_dt_bias, m_l0_ssd_a_log, m_l0_ssd_d, m_l0_ssd_norm, m_l0_ret_norm, m_l0_w_out, m_l0_ffn_norm, m_l0_ffn_w_in, m_l0_ffn_conv_w, m_l0_ffn_conv_b, m_l0_ffn_w_out, m_l1_mix_norm, m_l1_w_in, m_l1_lru_conv_w, m_l1_lru_conv_b, m_l1_lru_wa, m_l1_lru_ba, m_l1_lru_wx, m_l1_lru_bx, m_l1_lru_lambda, m_l1_w_out, m_l1_ffn_norm, m_l1_ffn_w_in, m_l1_ffn_conv_w, m_l1_ffn_conv_b, m_l1_ffn_w_out, m_final_norm, v_meta_tokens, v_l0_mix_norm, v_l0_w_in, v_l0_ssd_conv_w, v_l0_ssd_conv_b, v_l0_ssd_dt_bias, v_l0_ssd_a_log, v_l0_ssd_d, v_l0_ssd_norm, v_l0_ret_norm, v_l0_w_out, v_l0_ffn_norm, v_l0_ffn_w_in, v_l0_ffn_conv_w, v_l0_ffn_conv_b, v_l0_ffn_w_out, v_l1_mix_norm, v_l1_w_in, v_l1_lru_conv_w, v_l1_lru_conv_b, v_l1_lru_wa, v_l1_lru_ba, v_l1_lru_wx, v_l1_lru_bx, v_l1_lru_lambda, v_l1_w_out, v_l1_ffn_norm, v_l1_ffn_w_in, v_l1_ffn_conv_w, v_l1_ffn_conv_b, v_l1_ffn_w_out, v_final_norm):
    args = (x, meta_tokens, l0_mix_norm, l0_w_in, l0_ssd_conv_w, l0_ssd_conv_b, l0_ssd_dt_bias, l0_ssd_a_log, l0_ssd_d, l0_ssd_norm, l0_ret_norm, l0_w_out, l0_ffn_norm, l0_ffn_w_in, l0_ffn_conv_w, l0_ffn_conv_b, l0_ffn_w_out, l1_mix_norm, l1_w_in, l1_lru_conv_w, l1_lru_conv_b, l1_lru_wa, l1_lru_ba, l1_lru_wx, l1_lru_bx, l1_lru_lambda, l1_w_out, l1_ffn_norm, l1_ffn_w_in, l1_ffn_conv_w, l1_ffn_conv_b, l1_ffn_w_out, final_norm, loss_target, m_meta_tokens, m_l0_mix_norm, m_l0_w_in, m_l0_ssd_conv_w, m_l0_ssd_conv_b, m_l0_ssd_dt_bias, m_l0_ssd_a_log, m_l0_ssd_d, m_l0_ssd_norm, m_l0_ret_norm, m_l0_w_out, m_l0_ffn_norm, m_l0_ffn_w_in, m_l0_ffn_conv_w, m_l0_ffn_conv_b, m_l0_ffn_w_out, m_l1_mix_norm, m_l1_w_in, m_l1_lru_conv_w, m_l1_lru_conv_b, m_l1_lru_wa, m_l1_lru_ba, m_l1_lru_wx, m_l1_lru_bx, m_l1_lru_lambda, m_l1_w_out, m_l1_ffn_norm, m_l1_ffn_w_in, m_l1_ffn_conv_w, m_l1_ffn_conv_b, m_l1_ffn_w_out, m_final_norm, v_meta_tokens, v_l0_mix_norm, v_l0_w_in, v_l0_ssd_conv_w, v_l0_ssd_conv_b, v_l0_ssd_dt_bias, v_l0_ssd_a_log, v_l0_ssd_d, v_l0_ssd_norm, v_l0_ret_norm, v_l0_w_out, v_l0_ffn_norm, v_l0_ffn_w_in, v_l0_ffn_conv_w, v_l0_ffn_conv_b, v_l0_ffn_w_out, v_l1_mix_norm, v_l1_w_in, v_l1_lru_conv_w, v_l1_lru_conv_b, v_l1_lru_wa, v_l1_lru_ba, v_l1_lru_wx, v_l1_lru_bx, v_l1_lru_lambda, v_l1_w_out, v_l1_ffn_norm, v_l1_ffn_w_in, v_l1_ffn_conv_w, v_l1_ffn_conv_b, v_l1_ffn_w_out, v_final_norm)
    p = dict(zip(_IN_NAMES, args))
    B, seq, _ = x.shape
    nch = (seq + CH) // CH
    Pn = nch * CH
    R = B * Pn
    chip = 2 * lax.axis_index("x") + lax.axis_index("y")
    row2 = lambda v: v.reshape(1, -1)
    pad128 = lambda v: jnp.pad(v, (0, 128 - v.shape[0])).reshape(1, 128)

    small_shapes = [p[n].shape for n in _SMALL_SHARDED]
    halved = lambda w: w.astype(_MXU).reshape(2, w.shape[0] // 2, w.shape[1])
    mine = {n: halved(p[n]) for n in _BIG}
    mine_small = _pack([p[n] for n in _SMALL_SHARDED])
    W = {}

    def set_weight(n, g):
        g = _fill_own(g, mine[n], chip)
        g = g.reshape(4, -1, g.shape[3])
        W[n] = jnp.concatenate([g[k] for k in range(4)], axis=1) if n in _BIG_COLS else g.reshape(-1, g.shape[2])

    def gather_on(*names):
        return ("gather", [mine[n] for n in names])

    def take_weights(names, got):
        for n, g in zip(names, got):
            set_weight(n, g)

    gathered = _gather_shards([mine['l0_w_in']], mine_small)
    set_weight('l0_w_in', gathered[0])
    g_small = _fill_own(gathered[-1], mine_small, chip)
    per_chip = [_unpack(g_small[k], small_shapes) for k in range(4)]
    for i, n in enumerate(_SMALL_SHARDED):
        W[n] = jnp.concatenate([per_chip[k][i] for k in range(4)], axis=1)
    w0 = W['l0_w_in']
    w0_main = jnp.concatenate([w0[:, 3088:], w0[:, :3072]], axis=1)
    w0_dt = jnp.pad(w0[:, 3072:3088], ((0, 0), (0, 112)))
    cos, sin = _rope_tables(nch)

    meta = jnp.broadcast_to(W['meta_tokens'][None], (B, N_META, D))
    h0 = jnp.concatenate([jnp.zeros((B, PAD, D), F32), meta, x], axis=1).reshape(R, D)
    n0, n0t = _rmsnorm_fwd(h0, row2(p['l0_mix_norm']), "norm_l0_mix")
    u0 = _mm(n0, w0_main, "nn", F32, "l0_in_proj")
    udt = _mm(n0, w0_dt, "nn", F32, "l0_dt_proj")
    a_log, d_skip, dt_bias = pad128(p['l0_ssd_a_log']), pad128(p['l0_ssd_d']), pad128(p['l0_ssd_dt_bias'])
    ssd_cb = row2(p['l0_ssd_conv_b'])
    act, dt, dtt, *got = _ssd_prep(u0, udt, W['l0_ssd_conv_w'], ssd_cb, dt_bias, B, nch, rider=gather_on('l0_w_out'))
    take_weights(['l0_w_out'], got)
    ycat0, ypre, hin, *got = _ssd_fwd(act, u0, dt, dtt, a_log, d_skip, row2(p['l0_ssd_norm']), B, nch,
                                      rider=gather_on('l0_ffn_w_in'))
    take_weights(['l0_ffn_w_in'], got)
    ycat0, opre, rin, *got = _ret_fwd(u0, ycat0, cos, sin, row2(p['l0_ret_norm']), B, nch,
                                      rider=gather_on('l0_ffn_w_out'))
    take_weights(['l0_ffn_w_out'], got)
    h1 = _mm(ycat0, W['l0_w_out'], "nn", F32, "l0_out_proj", add=h0)
    n1, n1t = _rmsnorm_fwd(h1, row2(p['l0_ffn_norm']), "norm_l0_ffn")
    uf0 = _mm(n1, W['l0_ffn_w_in'], "nn", F32, "l0_ffn_in")
    f0_cb = row2(p['l0_ffn_conv_b'])
    a0, *got = _ffn_act_fwd(uf0, W['l0_ffn_conv_w'], f0_cb, B, nch, rider=gather_on('l1_w_in'))
    take_weights(['l1_w_in'], got)
    h2 = _mm(a0, W['l0_ffn_w_out'], "nn", F32, "l0_ffn_out", add=h1)
    n2, n2t = _rmsnorm_fwd(h2, row2(p['l1_mix_norm']), "norm_l1_mix")
    u1 = _mm(n2, W['l1_w_in'], "nn", F32, "l1_in_proj")
    lru = (W['l1_lru_conv_w'], row2(p['l1_lru_conv_b']), p['l1_lru_wa'], row2(p['l1_lru_ba']), p['l1_lru_wx'],
           row2(p['l1_lru_bx']), row2(p['l1_lru_lambda']))
    later = ['l1_w_out', 'l1_ffn_w_in', 'l1_ffn_w_out']
    ycat1, *got = _sb_fwd(u1, B, nch, rider=gather_on(*later))
    take_weights(later, got)
    ycat1, hs = _lru_fwd(u1, ycat1, *lru, B, nch)
    h3 = _mm(ycat1, W['l1_w_out'], "nn", F32, "l1_out_proj", add=h2)
    n3, n3t = _rmsnorm_fwd(h3, row2(p['l1_ffn_norm']), "norm_l1_ffn")
    uf1 = _mm(n3, W['l1_ffn_w_in'], "nn", F32, "l1_ffn_in")
    f1_cb = row2(p['l1_ffn_conv_b'])
    a1, = _ffn_act_fwd(uf1, W['l1_ffn_conv_w'], f1_cb, B, nch)
    h4 = _mm(a1, W['l1_ffn_w_out'], "nn", F32, "l1_ffn_out", add=h3)
    dh4, dh4b, lossp, dgf = _head(h4, row2(p['final_norm']), p['loss_target'].reshape(B * seq, D), B, nch)
    loss = lax.psum(jnp.sum(lossp[:, 0, 0]), ("x", "y", "c"))

    G = {'final_norm': dgf[:, 0].sum(0)}

    core = lax.axis_index("c").reshape(1).astype(jnp.int32)

    def col_shards(pieces):
        edges = np.cumsum([0] + [q.shape[1] for q in pieces])
        cs = int(edges[-1]) // 4
        shards = []
        for k in range(4):
            lo, hi = k * cs, (k + 1) * cs
            cut = [q[:, max(lo - e0, 0):min(hi - e0, q.shape[1])]
                   for q, e0, e1 in zip(pieces, edges[:-1], edges[1:]) if e0 < hi and e1 > lo]
            shards.append(cut[0] if len(cut) == 1 else jnp.concatenate(cut, axis=1))
        return jnp.stack(shards)

    def chip_sums(names, tag):
        stacked = [G[n] if n in _BIG_COLS else G[n].reshape(4, G[n].shape[0] // 4, G[n].shape[1]) for n in names]
        theirs = _swap_halves(stacked, "swap_halves_" + tag)
        return {n: _chip_sum(g, t, core, "chip_sum_" + n) for n, g, t in zip(names, stacked, theirs)}

    parts = {}

    def scatter_on(names, tag):
        sums = chip_sums(names, tag)
        return sums, ("scatter", [sums[n] for n in names])

    def take_parts(names, sums, got):
        for n, g in zip(names, got):
            parts[n] = _fill_own(g, lax.dynamic_index_in_dim(sums[n], chip, 0, keepdims=False), chip)

    def ffn_bwd(layer, dh_out, dhb_out, h_in, nt_in, uf, a_act, cb, rider=None):
        pre = f"l{layer}_"
        w_in, w_out, cw = W[pre + 'ffn_w_in'], W[pre + 'ffn_w_out'], W[pre + 'ffn_conv_w']
        da = _mm(dhb_out, w_out, "nt", F32, pre + "ffn_out_dgrad")
        G[pre + 'ffn_w_out'] = _mm(a_act, dhb_out, "tn", F32, pre + "ffn_out_wgrad")
        dug, duu, dwg, dwu, *rode = _ffn_act_bwd(da, uf, cw, cb, nch, pre + "ffn_act_bwd", rider=rider)
        G[pre + 'ffn_conv_w'] = jnp.concatenate([dwg[:3], dwu[:3]], axis=1)
        G[pre + 'ffn_conv_b'] = jnp.concatenate([dwg[7], dwu[7]])
        dn = _mm(dug, w_in, "nt", F32, pre + "ffn_in_dgrad_g")
        dn = _mm(duu, w_in, "nt", F32, pre + "ffn_in_dgrad_u", add=dn, b_off=FFN)
        G[pre + 'ffn_w_in'] = col_shards([_mm(nt_in, dug, "nn", F32, pre + "ffn_in_wgrad_g"),
                                          _mm(nt_in, duu, "nn", F32, pre + "ffn_in_wgrad_u")])
        dh_in, dhb_in, dg = _rmsnorm_bwd(h_in, row2(p[pre + 'ffn_norm']), dn, dh_out, nch, pre + "ffn_norm_bwd")
        G[pre + 'ffn_norm'] = dg[0]
        return dh_in, dhb_in, rode

    dh3, dh3b, _ = ffn_bwd(1, dh4, dh4b, h3, n3t, uf1, a1, f1_cb)
    dy1 = _mm(dh3b, W['l1_w_out'], "nt", F32, "l1_out_dgrad")
    G['l1_w_out'] = _mm(ycat1, dh3b, "tn", F32, "l1_out_wgrad")
    done = ['l1_ffn_w_in', 'l1_ffn_w_out', 'l1_w_out']
    sums, rider = scatter_on(done, "a")
    dq, dkt, dvt, *got = _sb_bwd(dy1, u1, B, nch, rider=rider)
    dk, dv = dkt.T, dvt.T
    take_parts(done, sums, got)
    dgate, dxc, pgl, dwa, dwx = _lru_bwd(dy1, u1, hs, *lru, B, nch)
    dxr, dcw = _conv_bwd(dxc, u1, 4096, W['l1_lru_conv_w'], 4, "l1_lru_conv_bwd")
    pgl = pgl.sum(0)
    G['l1_lru_ba'], G['l1_lru_bx'], G['l1_lru_lambda'] = pgl[0], pgl[1], pgl[2]
    G['l1_lru_wa'], G['l1_lru_wx'] = dwa.sum(0), dwx.sum(0)
    G['l1_lru_conv_w'], G['l1_lru_conv_b'] = dcw[:4], dcw[7]
    du1 = jnp.concatenate([piece.astype(_MXU) for piece in (dq, dk, dv, dgate, dxr)], axis=1)
    dn = _mm(du1, W['l1_w_in'], "nt", F32, "l1_in_dgrad")
    G['l1_w_in'] = col_shards([_mm(n2t, du1, "nn", F32, "l1_in_wgrad")])
    dh2, dh2b, dg = _rmsnorm_bwd(h2, row2(p['l1_mix_norm']), dn, dh3, nch, "l1_mix_norm_bwd")
    G['l1_mix_norm'] = dg[0]

    dh1, dh1b, _ = ffn_bwd(0, dh2, dh2b, h1, n1t, uf0, a0, f0_cb)
    dy0 = _mm(dh1b, W['l0_w_out'], "nt", F32, "l0_out_dgrad")
    G['l0_w_out'] = _mm(ycat0, dh1b, "tn", F32, "l0_out_wgrad")
    done = ['l1_w_in', 'l0_ffn_w_in', 'l0_ffn_w_out', 'l0_w_out']
    sums, rider = scatter_on(done, "b")
    dz, dxs, dbm, dcm, ddt4, pgs, *got = _ssd_bwd(dy0, ypre, u0, act, dt, dtt, hin, a_log, d_skip,
                                                  row2(p['l0_ssd_norm']), B, nch, rider=rider)
    take_parts(done, sums, got)
    dpre, ddtr, pgd = _ssd_prep_bwd(dxs, dbm, dcm, ddt4, u0, udt, W['l0_ssd_conv_w'], ssd_cb, dt_bias, B, nch)
    dxbc, dcw0 = _conv_bwd(dpre, u0, U0_XBC, W['l0_ssd_conv_w'], 4, "l0_ssd_conv_bwd")
    dqkvg, pgr = _ret_bwd(dy0, u0, opre, rin, cos, sin, row2(p['l0_ret_norm']), B, nch)
    pgs = pgs.sum(0)
    G['l0_ssd_norm'] = pgs[:, 0, :].reshape(-1)
    G['l0_ssd_d'] = pgs[:, 1, :128].sum(0)[:SSD_HEADS]
    G['l0_ssd_a_log'] = pgs[:, 2, :128].sum(0)[:SSD_HEADS]
    G['l0_ssd_dt_bias'] = pgd.sum(0)[0, :SSD_HEADS]
    G['l0_ssd_conv_w'], G['l0_ssd_conv_b'] = dcw0[:4], dcw0[7]
    G['l0_ret_norm'] = pgr.sum(0)[0]
    dn = _mm(dqkvg, w0_main, "nt", F32, "l0_in_dgrad_qkvg")
    dn = _mm(dz, w0_main, "nt", F32, "l0_in_dgrad_z", add=dn, b_off=U0_Z)
    dn = _mm(dxbc, w0_main, "nt", F32, "l0_in_dgrad_xbc", add=dn, b_off=U0_XBC)
    dn = _mm(ddtr, w0_dt, "nt", F32, "l0_in_dgrad_dt", add=dn)
    G['l0_w_in'] = col_shards([
        _mm(n0t, dz, "nn", F32, "l0_in_wgrad_z"), _mm(n0t, dxbc, "nn", F32, "l0_in_wgrad_xbc"),
        _mm(n0t, ddtr, "nn", F32, "l0_in_wgrad_dt")[:, :SSD_HEADS], _mm(n0t, dqkvg, "nn", F32, "l0_in_wgrad_qkvg")])
    sums, rider = scatter_on(['l0_w_in'], "d")
    dh0, _, dg, *got = _rmsnorm_bwd(h0, row2(p['l0_mix_norm']), dn, dh1, nch, "l0_mix_norm_bwd", rider=rider)
    take_parts(['l0_w_in'], sums, got)
    G['l0_mix_norm'] = dg[0]
    dh0 = dh0.reshape(B, Pn, D)
    grad_x = dh0[:, CH:]
    G['meta_tokens'] = dh0[:, PAD:CH].sum(0)

    reds =[_sum_chips(parts[n], "sum_chips_" + n) for n in _BIG]
    grads = {}
    for n, own, other in zip(_BIG, reds, _join_halves(reds)):
        both = jnp.where(core[0] == 0, jnp.stack([own, other]), jnp.stack([other, own]))
        grads[n] = both.reshape(-1, both.shape[2])
    small_full = _unpack(_allreduce_small(_pack([G[n] for n in _SMALL])), [G[n].shape for n in _SMALL])
    for n, g in zip(_SMALL, small_full):
        if n in _SMALL_SHARDED:
            cs = g.shape[1] // 4
            g = lax.dynamic_slice_in_dim(g, chip * cs, cs, axis=1)
        grads[n] = g.reshape(p[n].shape)

    delta, new_m, new_v = {}, {}, {}
    for n in _BIG:
        delta[n], new_m[n], new_v[n] = _adamw(p[n], grads[n], p['m_' + n], p['v_' + n], "adamw_" + n)
    shapes = [p[n].shape for n in _SMALL]
    outs = _adamw(_pack([p[n] for n in _SMALL]), _pack([grads[n] for n in _SMALL]), _pack([p['m_' + n] for n in _SMALL]),
                  _pack([p['v_' + n] for n in _SMALL]), "adamw_small")
    for dst, buf in zip((delta, new_m, new_v), outs):
        for n, a in zip(_SMALL, _unpack(buf, shapes)):
            dst[n] = a
    return (loss, grad_x, *[grads[n] for n in _W_NAMES], *[delta[n] for n in _W_NAMES],
            *[new_m[n] for n in _W_NAMES], *[new_v[n] for n in _W_NAMES])
```
